```python
import math
import jax, jax.numpy as jnp
from jax import lax
import numpy as np

D_MODEL = 1024
BATCH = 16
SEQ = 2048
DEPTH = 2

HEAD_DIM = 64
N_HEADS_SB = 4
N_HEADS_DIL = 4
N_HEADS_FOX = 4
W_SB = N_HEADS_SB * HEAD_DIM
W_DIL = N_HEADS_DIL * HEAD_DIM
W_FOX = N_HEADS_FOX * HEAD_DIM
CONV_WIDTH = D_MODEL - W_SB - W_DIL - W_FOX
N_CONV_GROUPS = 4
CONV_K = 3
DIL_PATTERNS = ((128, 1), (512, 4), (2048, 16))
BLOCK_Q = 128
REL_BUCKETS = 32
REL_MAX_DIST = 2048
D_FF = ((8 * D_MODEL + 3 * 256 - 1) // (3 * 256)) * 256
PROJ_WIDTH = 3 * W_SB + 3 * W_DIL + 3 * W_FOX + 3 * CONV_WIDTH + N_HEADS_FOX
DN_ALPHA = (2 * DEPTH) ** 0.25
DN_BETA = (8 * DEPTH) ** -0.25
LN_EPS = 1e-5

kernel_name = "hybrid_parallel_sb_dilated_fox_shortconv"


def split_heads(t, n_heads):
    b, s, _ = t.shape
    return t.reshape(b, s, n_heads, HEAD_DIM).transpose(0, 2, 1, 3)


def merge_heads(t):
    b, h, s, d = t.shape
    return t.transpose(0, 2, 1, 3).reshape(b, s, h * d)


def layer_norm(x, g, b):
    xf = x.astype(jnp.float32)
    mu = jnp.mean(xf, axis=-1, keepdims=True)
    var = jnp.mean(jnp.square(xf - mu), axis=-1, keepdims=True)
    y = (xf - mu) * lax.rsqrt(var + LN_EPS) * g.astype(jnp.float32) + b.astype(jnp.float32)
    return y.astype(x.dtype)


def stick_breaking_attention(q, k, v):
    b, h, s, d = q.shape
    nb = s // BLOCK_Q
    scale = d ** -0.5
    qb = q.reshape(b, h, nb, BLOCK_Q, d).transpose(2, 0, 1, 3, 4)
    k_pos = jnp.arange(s)

    def block(args):
        q_blk, i = args
        q_pos = i * BLOCK_Q + jnp.arange(BLOCK_Q)
        z = jnp.einsum('bhqd,bhkd->bhqk', q_blk, k).astype(jnp.float32) * scale
        strict = k_pos[None, :] < q_pos[:, None]
        log_beta = jax.nn.log_sigmoid(z)
        log_rest = jnp.where(strict, jax.nn.log_sigmoid(-z), 0.0)
        tail = lax.cumsum(log_rest, axis=3, reverse=True) - log_rest
        a = jnp.where(strict, jnp.exp(log_beta + tail), 0.0)
        return jnp.einsum('bhqk,bhkd->bhqd', a.astype(v.dtype), v)

    out = lax.map(block, (qb, jnp.arange(nb)))
    return out.transpose(1, 2, 0, 3, 4).reshape(b, h, s, d)


def forgetting_attention(q, k, v, log_f):
    b, h, s, d = q.shape
    nb = s // BLOCK_Q
    scale = d ** -0.5
    cum = jnp.cumsum(log_f, axis=-1)
    qb = q.reshape(b, h, nb, BLOCK_Q, d).transpose(2, 0, 1, 3, 4)
    cb = cum.reshape(b, h, nb, BLOCK_Q).transpose(2, 0, 1, 3)
    k_pos = jnp.arange(s)

    def block(args):
        q_blk, c_blk, i = args
        q_pos = i * BLOCK_Q + jnp.arange(BLOCK_Q)
        z = (jnp.einsum('bhqd,bhkd->bhqk', q_blk, k).astype(jnp.float32) * scale
             + (c_blk[..., :, None] - cum[..., None, :]))
        z = jnp.where(k_pos[None, :] <= q_pos[:, None], z, -jnp.inf)
        p = jax.nn.softmax(z, axis=-1)
        return jnp.einsum('bhqk,bhkd->bhqd', p.astype(v.dtype), v)

    out = lax.map(block, (qb, cb, jnp.arange(nb)))
    return out.transpose(1, 2, 0, 3, 4).reshape(b, h, s, d)


def t5_bucket(dist):
    max_exact = REL_BUCKETS // 2
    nf = jnp.maximum(dist, 1).astype(jnp.float32)
    large = max_exact + (jnp.log(nf / max_exact) / math.log(REL_MAX_DIST / max_exact)
                         * (REL_BUCKETS - max_exact)).astype(jnp.int32)
    large = jnp.minimum(large, REL_BUCKETS - 1)
    return jnp.where(dist < max_exact, dist, large)


def to_residue_blocks(t, dil, n_blk):
    b, h, s, e = t.shape
    sub_len = s // dil
    t = t.reshape(b, h, sub_len, dil, e).transpose(0, 1, 3, 2, 4)
    t = jnp.pad(t, ((0, 0), (0, 0), (0, 0), (0, n_blk * BLOCK_Q - sub_len), (0, 0)))
    return t.reshape(b, h, dil, n_blk, BLOCK_Q, e)


def from_residue_blocks(t, s):
    b, h, dil, n_blk, bq, e = t.shape
    t = t.reshape(b, h, dil, n_blk * bq, e)[:, :, :, : s // dil]
    return t.transpose(0, 1, 3, 2, 4).reshape(b, h, s, e)


def with_previous_block(t):
    prev = jnp.pad(t[:, :, :, :-1], ((0, 0), (0, 0), (0, 0), (1, 0), (0, 0), (0, 0)))
    return jnp.concatenate([prev, t], axis=4)


def dilated_window_attention(q, k, v, rel_bias):
    b, h, s, d = q.shape
    scale = d ** -0.5
    outs, maxes, dens = [], [], []
    for window, dil in DIL_PATTERNS:
        n_back = window // dil
        n_blk = -(-(s // dil) // BLOCK_Q)
        qs = to_residue_blocks(q, dil, n_blk)
        kb = with_previous_block(to_residue_blocks(k, dil, n_blk))
        vb = with_previous_block(to_residue_blocks(v, dil, n_blk))
        qi = jnp.arange(BLOCK_Q)[:, None]
        kj = jnp.arange(2 * BLOCK_Q)[None, :]
        sub_dist = qi + BLOCK_Q - kj
        in_band = (sub_dist >= 0) & (sub_dist <= n_back)
        key_valid = (jnp.arange(n_blk)[:, None, None] > 0) | (kj[None] >= BLOCK_Q)
        mask = in_band[None] & key_valid
        bias = rel_bias[t5_bucket(jnp.maximum(sub_dist, 0) * dil)]
        bias = bias.transpose(2, 0, 1).astype(jnp.float32)
        z = (jnp.einsum('bhrnqd,bhrnkd->bhrnqk', qs, kb).astype(jnp.float32) * scale
             + bias[None, :, None, None])
        z = jnp.where(mask[None, None, None], z, -jnp.inf)
        m = jnp.max(z, axis=-1, keepdims=True)
        p = jnp.exp(z - m)
        den = jnp.sum(p, axis=-1, keepdims=True)
        o = jnp.einsum('bhrnqk,bhrnkd->bhrnqd', p.astype(v.dtype), vb).astype(jnp.float32) / den
        outs.append(from_residue_blocks(o, s))
        maxes.append(from_residue_blocks(m, s))
        dens.append(from_residue_blocks(den, s))
    m_all = jnp.stack(maxes)
    w = jnp.stack(dens) * jnp.exp(m_all - jnp.max(m_all, axis=0, keepdims=True))
    out = jnp.sum(w * jnp.stack(outs), axis=0) / jnp.sum(w, axis=0)
    return out.astype(q.dtype)


def short_gated_conv(b_gate, c_gate, h, conv_w):
    u = c_gate * h
    y = lax.conv_general_dilated(
        u, conv_w[:, None, :].astype(u.dtype), window_strides=(1,), padding=[(CONV_K - 1, 0)],
        dimension_numbers=('NWC', 'WIO', 'NWC'), feature_group_count=CONV_WIDTH)
    return b_gate * y


def hybrid_mixer(x, w_in, f_bias, conv_w, w_out, rel_bias):
    proj = x @ w_in
    widths = (W_SB,) * 3 + (W_DIL,) * 3 + (W_FOX,) * 3 + (CONV_WIDTH,) * 3 + (N_HEADS_FOX,)
    points, acc = [], 0
    for wdt in widths[:-1]:
        acc += wdt
        points.append(acc)
    (sb_q, sb_k, sb_v, dl_q, dl_k, dl_v, fx_q, fx_k, fx_v,
     cv_b, cv_c, cv_h, fx_f) = jnp.split(proj, points, axis=-1)
    out_a = stick_breaking_attention(split_heads(sb_q, N_HEADS_SB), split_heads(sb_k, N_HEADS_SB),
                                     split_heads(sb_v, N_HEADS_SB))
    out_b = dilated_window_attention(split_heads(dl_q, N_HEADS_DIL), split_heads(dl_k, N_HEADS_DIL),
                                     split_heads(dl_v, N_HEADS_DIL), rel_bias)
    log_f = jax.nn.log_sigmoid((fx_f + f_bias).astype(jnp.float32)).transpose(0, 2, 1)
    out_c = forgetting_attention(split_heads(fx_q, N_HEADS_FOX), split_heads(fx_k, N_HEADS_FOX),
                                 split_heads(fx_v, N_HEADS_FOX), log_f)
    out_d = short_gated_conv(cv_b, cv_c, cv_h, conv_w)
    mixed = jnp.concatenate([merge_heads(out_a), merge_heads(out_b), merge_heads(out_c),
                             out_d.astype(x.dtype)], axis=-1)
    return mixed @ w_out


def swiglu_ffn(x, w_gate, w_up, w_down):
    return (jax.nn.silu(x @ w_gate) * (x @ w_up)) @ w_down


def _fwd_setup_inputs(seed: int = 0) -> dict:
    key = jax.random.key(seed)
    ks = jax.random.split(key, 13)
    f32 = jnp.float32
    nrm = lambda k, shape, s: jax.random.normal(k, shape, f32) * s
    return {
        "x": jax.random.normal(ks[0], (BATCH, SEQ, D_MODEL), f32),
        "w_in": nrm(ks[1], (DEPTH, D_MODEL, PROJ_WIDTH), D_MODEL ** -0.5),
        "f_bias": 1.0 + nrm(ks[2], (DEPTH, N_HEADS_FOX), 0.1),
        "conv_w": nrm(ks[3], (DEPTH, CONV_K, CONV_WIDTH), CONV_K ** -0.5),
        "w_out": nrm(ks[4], (DEPTH, D_MODEL, D_MODEL), D_MODEL ** -0.5 * DN_BETA),
        "rel_bias": nrm(ks[5], (REL_BUCKETS, N_HEADS_DIL), 0.1),
        "ln1_g": 1.0 + nrm(ks[6], (DEPTH, D_MODEL), 0.02),
        "ln1_b": nrm(ks[7], (DEPTH, D_MODEL), 0.02),
        "w_gate": nrm(ks[8], (DEPTH, D_MODEL, D_FF), D_MODEL ** -0.5),
        "w_up": nrm(ks[9], (DEPTH, D_MODEL, D_FF), D_MODEL ** -0.5),
        "w_down": nrm(ks[10], (DEPTH, D_FF, D_MODEL), D_FF ** -0.5 * DN_BETA),
        "ln2_g": 1.0 + nrm(ks[11], (DEPTH, D_MODEL), 0.02),
        "ln2_b": nrm(ks[12], (DEPTH, D_MODEL), 0.02),
    }


def _fwd_reference(x, w_in, f_bias, conv_w, w_out, rel_bias, ln1_g, ln1_b, w_gate, w_up, w_down,
              ln2_g, ln2_b):
    for layer in range(DEPTH):
        mix = hybrid_mixer(x, w_in[layer], f_bias[layer], conv_w[layer], w_out[layer], rel_bias)
        x = layer_norm(DN_ALPHA * x + mix, ln1_g[layer], ln1_b[layer])
        ffn = swiglu_ffn(x, w_gate[layer], w_up[layer], w_down[layer])
        x = layer_norm(DN_ALPHA * x + ffn, ln2_g[layer], ln2_b[layer])
    return x


import jax as _jax
import jax.numpy as _jnp

TWIN_FORMAT = 'train_step'
FWD_PARAMS = ['x', 'w_in', 'f_bias', 'conv_w', 'w_out', 'rel_bias', 'ln1_g', 'ln1_b', 'w_gate', 'w_up', 'w_down', 'ln2_g', 'ln2_b']
TWIN_WEIGHTS = ['w_in', 'f_bias', 'conv_w', 'w_out', 'rel_bias', 'ln1_g', 'ln1_b', 'w_gate', 'w_up', 'w_down', 'ln2_g', 'ln2_b']
TWIN_DIFF_INPUT = 'x'
TWIN_INPUTS = ['x', 'w_in', 'f_bias', 'conv_w', 'w_out', 'rel_bias', 'ln1_g', 'ln1_b', 'w_gate', 'w_up', 'w_down', 'ln2_g', 'ln2_b', 'loss_target', 'm_w_in', 'm_f_bias', 'm_conv_w', 'm_w_out', 'm_rel_bias', 'm_ln1_g', 'm_ln1_b', 'm_w_gate', 'm_w_up', 'm_w_down', 'm_ln2_g', 'm_ln2_b', 'v_w_in', 'v_f_bias', 'v_conv_w', 'v_w_out', 'v_rel_bias', 'v_ln1_g', 'v_ln1_b', 'v_w_gate', 'v_w_up', 'v_w_down', 'v_ln2_g', 'v_ln2_b']
TWIN_OUTPUTS = ['loss', 'grad_x', 'grad_w_in', 'grad_f_bias', 'grad_conv_w', 'grad_w_out', 'grad_rel_bias', 'grad_ln1_g', 'grad_ln1_b', 'grad_w_gate', 'grad_w_up', 'grad_w_down', 'grad_ln2_g', 'grad_ln2_b', 'delta_w_in', 'delta_f_bias', 'delta_conv_w', 'delta_w_out', 'delta_rel_bias', 'delta_ln1_g', 'delta_ln1_b', 'delta_w_gate', 'delta_w_up', 'delta_w_down', 'delta_ln2_g', 'delta_ln2_b', 'new_m_w_in', 'new_m_f_bias', 'new_m_conv_w', 'new_m_w_out', 'new_m_rel_bias', 'new_m_ln1_g', 'new_m_ln1_b', 'new_m_w_gate', 'new_m_w_up', 'new_m_w_down', 'new_m_ln2_g', 'new_m_ln2_b', 'new_v_w_in', 'new_v_f_bias', 'new_v_conv_w', 'new_v_w_out', 'new_v_rel_bias', 'new_v_ln1_g', 'new_v_ln1_b', 'new_v_w_gate', 'new_v_w_up', 'new_v_w_down', 'new_v_ln2_g', 'new_v_ln2_b']
TWIN_LEAF_KINDS = {'loss': 'loss', 'grad_x': 'grad_x', 'grad_w_in': 'grad_w', 'grad_f_bias': 'grad_w', 'grad_conv_w': 'grad_w', 'grad_w_out': 'grad_w', 'grad_rel_bias': 'grad_w', 'grad_ln1_g': 'grad_w', 'grad_ln1_b': 'grad_w', 'grad_w_gate': 'grad_w', 'grad_w_up': 'grad_w', 'grad_w_down': 'grad_w', 'grad_ln2_g': 'grad_w', 'grad_ln2_b': 'grad_w', 'delta_w_in': 'delta_w', 'delta_f_bias': 'delta_w', 'delta_conv_w': 'delta_w', 'delta_w_out': 'delta_w', 'delta_rel_bias': 'delta_w', 'delta_ln1_g': 'delta_w', 'delta_ln1_b': 'delta_w', 'delta_w_gate': 'delta_w', 'delta_w_up': 'delta_w', 'delta_w_down': 'delta_w', 'delta_ln2_g': 'delta_w', 'delta_ln2_b': 'delta_w', 'new_m_w_in': 'new_m', 'new_m_f_bias': 'new_m', 'new_m_conv_w': 'new_m', 'new_m_w_out': 'new_m', 'new_m_rel_bias': 'new_m', 'new_m_ln1_g': 'new_m', 'new_m_ln1_b': 'new_m', 'new_m_w_gate': 'new_m', 'new_m_w_up': 'new_m', 'new_m_w_down': 'new_m', 'new_m_ln2_g': 'new_m', 'new_m_ln2_b': 'new_m', 'new_v_w_in': 'new_v', 'new_v_f_bias': 'new_v', 'new_v_conv_w': 'new_v', 'new_v_w_out': 'new_v', 'new_v_rel_bias': 'new_v', 'new_v_ln1_g': 'new_v', 'new_v_ln1_b': 'new_v', 'new_v_w_gate': 'new_v', 'new_v_w_up': 'new_v', 'new_v_w_down': 'new_v', 'new_v_ln2_g': 'new_v', 'new_v_ln2_b': 'new_v'}


def _forward(args):
    return _fwd_reference(*[args[k] for k in FWD_PARAMS])


def _output_shape():
    out = _jax.eval_shape(lambda: _forward(_fwd_setup_inputs(0)))
    return out.shape, out.dtype

N_MICROBATCH = 1
ADAM_LR = 0.001
ADAM_B1 = 0.9
ADAM_B2 = 0.999
ADAM_EPS = 1e-08
ADAM_WD = 0.01
ADAM_STEP = 10
PER_EXAMPLE_BATCH_AXIS = {'x': 0, 'loss_target': 0}
SHARED_INPUTS = []
_WEIGHT_DTYPES = {'w_in': _jnp.float32, 'f_bias': _jnp.float32, 'conv_w': _jnp.float32, 'w_out': _jnp.float32, 'rel_bias': _jnp.float32, 'ln1_g': _jnp.float32, 'ln1_b': _jnp.float32, 'w_gate': _jnp.float32, 'w_up': _jnp.float32, 'w_down': _jnp.float32, 'ln2_g': _jnp.float32, 'ln2_b': _jnp.float32}
MOMENT_SCALE = {'w_in': 3.713175e-02, 'f_bias': 1.762538e-01, 'conv_w': 6.698955e-02, 'w_out': 8.318473e-02, 'rel_bias': 1.908093e-02, 'ln1_g': 9.010847e-01, 'ln1_b': 4.209502e-01, 'w_gate': 2.344919e-02, 'w_up': 2.266031e-02, 'w_down': 7.510945e-02, 'ln2_g': 2.261383e+01, 'ln2_b': 7.154090e-01}


def _to_microbatches(a, axis):
    t = _jnp.moveaxis(a, axis, 0)
    t = t.reshape((N_MICROBATCH, t.shape[0] // N_MICROBATCH) + t.shape[1:])
    return _jnp.moveaxis(t, 1, axis + 1)


def setup_inputs(seed: int = 0) -> dict:
    inp = _fwd_setup_inputs(seed)
    key = _jax.random.fold_in(_jax.random.key(seed), 7919)
    shape, _ = _output_shape()
    out = dict(inp)
    out["loss_target"] = _jax.random.normal(_jax.random.fold_in(key, 0), shape, _jnp.float32)
    for i, name in enumerate(TWIN_WEIGHTS):
        w = inp[name].astype(_jnp.float32)
        if MOMENT_SCALE is None:
            s = _jnp.sqrt(_jnp.mean(_jnp.square(w)) + 1e-30)
        else:
            s = MOMENT_SCALE[name]
        km, kv = _jax.random.split(_jax.random.fold_in(key, i + 1))
        out[name] = w
        out["m_" + name] = s * _jax.random.normal(km, w.shape, _jnp.float32)
        out["v_" + name] = (s * s) * _jax.random.uniform(kv, w.shape, _jnp.float32, 0.5, 1.5)
    if N_MICROBATCH > 1:
        for name, axis in PER_EXAMPLE_BATCH_AXIS.items():
            out[name] = _to_microbatches(out[name], axis)
    return {'x': out['x'], 'w_in': out['w_in'], 'f_bias': out['f_bias'], 'conv_w': out['conv_w'], 'w_out': out['w_out'], 'rel_bias': out['rel_bias'], 'ln1_g': out['ln1_g'], 'ln1_b': out['ln1_b'], 'w_gate': out['w_gate'], 'w_up': out['w_up'], 'w_down': out['w_down'], 'ln2_g': out['ln2_g'], 'ln2_b': out['ln2_b'], 'loss_target': out['loss_target'], 'm_w_in': out['m_w_in'], 'm_f_bias': out['m_f_bias'], 'm_conv_w': out['m_conv_w'], 'm_w_out': out['m_w_out'], 'm_rel_bias': out['m_rel_bias'], 'm_ln1_g': out['m_ln1_g'], 'm_ln1_b': out['m_ln1_b'], 'm_w_gate': out['m_w_gate'], 'm_w_up': out['m_w_up'], 'm_w_down': out['m_w_down'], 'm_ln2_g': out['m_ln2_g'], 'm_ln2_b': out['m_ln2_b'], 'v_w_in': out['v_w_in'], 'v_f_bias': out['v_f_bias'], 'v_conv_w': out['v_conv_w'], 'v_w_out': out['v_w_out'], 'v_rel_bias': out['v_rel_bias'], 'v_ln1_g': out['v_ln1_g'], 'v_ln1_b': out['v_ln1_b'], 'v_w_gate': out['v_w_gate'], 'v_w_up': out['v_w_up'], 'v_w_down': out['v_w_down'], 'v_ln2_g': out['v_ln2_g'], 'v_ln2_b': out['v_ln2_b']}


def _loss(weights, diff, rest, loss_target):
    with _jax.named_scope("forward"):
        args = {**rest, TWIN_DIFF_INPUT: diff, **{k: w.astype(_WEIGHT_DTYPES[k]) for k, w in weights.items()}}
        y = _forward(args)
    with _jax.named_scope("loss_head"):
        err = _jnp.square(y.astype(_jnp.float32) - loss_target)
        return 0.5 * _jnp.sum(_jnp.mean(err, axis=-1)) if err.ndim else 0.5 * err


def _adamw(w, g, m, v):
    m = ADAM_B1 * m + (1.0 - ADAM_B1) * g
    v = ADAM_B2 * v + (1.0 - ADAM_B2) * _jnp.square(g)
    m_hat = m / (1.0 - ADAM_B1 ** ADAM_STEP)
    v_hat = v / (1.0 - ADAM_B2 ** ADAM_STEP)
    delta = -ADAM_LR * (m_hat / (_jnp.sqrt(v_hat) + ADAM_EPS) + ADAM_WD * w)
    return delta, m, v


def reference(x, w_in, f_bias, conv_w, w_out, rel_bias, ln1_g, ln1_b, w_gate, w_up, w_down, ln2_g, ln2_b, loss_target, m_w_in, m_f_bias, m_conv_w, m_w_out, m_rel_bias, m_ln1_g, m_ln1_b, m_w_gate, m_w_up, m_w_down, m_ln2_g, m_ln2_b, v_w_in, v_f_bias, v_conv_w, v_w_out, v_rel_bias, v_ln1_g, v_ln1_b, v_w_gate, v_w_up, v_w_down, v_ln2_g, v_ln2_b):
    given = dict(x=x, w_in=w_in, f_bias=f_bias, conv_w=conv_w, w_out=w_out, rel_bias=rel_bias, ln1_g=ln1_g, ln1_b=ln1_b, w_gate=w_gate, w_up=w_up, w_down=w_down, ln2_g=ln2_g, ln2_b=ln2_b, loss_target=loss_target, m_w_in=m_w_in, m_f_bias=m_f_bias, m_conv_w=m_conv_w, m_w_out=m_w_out, m_rel_bias=m_rel_bias, m_ln1_g=m_ln1_g, m_ln1_b=m_ln1_b, m_w_gate=m_w_gate, m_w_up=m_w_up, m_w_down=m_w_down, m_ln2_g=m_ln2_g, m_ln2_b=m_ln2_b, v_w_in=v_w_in, v_f_bias=v_f_bias, v_conv_w=v_conv_w, v_w_out=v_w_out, v_rel_bias=v_rel_bias, v_ln1_g=v_ln1_g, v_ln1_b=v_ln1_b, v_w_gate=v_w_gate, v_w_up=v_w_up, v_w_down=v_w_down, v_ln2_g=v_ln2_g, v_ln2_b=v_ln2_b)
    weights = {n: given[n] for n in TWIN_WEIGHTS}
    shared = {n: given[n] for n in SHARED_INPUTS}
    per_example = {n: given[n] for n in ['x']}
    grad_fn = _jax.value_and_grad(_loss, argnums=(0, 1))

    def one_microbatch(ex, loss_target):
        ex = dict(ex)
        diff = ex.pop(TWIN_DIFF_INPUT)
        return grad_fn(weights, diff, {**shared, **ex}, loss_target)

    if N_MICROBATCH == 1:
        loss, (grad_w, grad_x) = one_microbatch(per_example, given["loss_target"])
    else:
        def body(carry, xs):
            loss_sum, grad_sum = carry
            l_k, (gw_k, gx_k) = one_microbatch(xs[0], xs[1])
            with _jax.named_scope("update"):
                return (loss_sum + l_k, _jax.tree.map(_jnp.add, grad_sum, gw_k)), gx_k

        init = (_jnp.zeros((), _jnp.float32), _jax.tree.map(_jnp.zeros_like, weights))
        (loss, grad_w), grad_x = _jax.lax.scan(body, init, (per_example, given["loss_target"]))
    with _jax.named_scope("update"):
        delta_w, new_m, new_v = {}, {}, {}
        for n in TWIN_WEIGHTS:
            delta_w[n], new_m[n], new_v[n] = _adamw(weights[n], grad_w[n], given["m_" + n], given["v_" + n])
    return (loss, grad_x, *[grad_w[n] for n in TWIN_WEIGHTS], *[delta_w[n] for n in TWIN_WEIGHTS],
            *[new_m[n] for n in TWIN_WEIGHTS], *[new_v[n] for n in TWIN_WEIGHTS])
```

```python
import functools
import math

import numpy as np
import jax
import jax.numpy as jnp
from jax import lax
from jax.experimental import pallas as pl
from jax.experimental.pallas import tpu as pltpu

F32 = jnp.float32
BF16 = jnp.bfloat16
MXU_DTYPE = BF16

D_MODEL = 1024
HEAD_DIM = 64
N_HEADS = 4
BLK = 128
N_SLOTS = 36
W_QKV = N_SLOTS * HEAD_DIM
CONV_W = 256
PROJ = 3076
PROJ_PAD = 3200
D_FF = 2816
DEPTH = 2
ALPHA = (2 * DEPTH) ** 0.25
LN_EPS = 1e-5
NEG = -1e30
DIL_PATTERNS = ((128, 1), (512, 4), (2048, 16))
REL_BUCKETS = 32
N_CHIPS = 4
N_DEV = 8
PACK_ROWS = 3137
SMALL_ROWS = 16

ADAM_LR = 0.001
ADAM_B1 = 0.9
ADAM_B2 = 0.999
ADAM_EPS = 1e-08
ADAM_WD = 0.01
ADAM_STEP = 10

VMEM_LIMIT = 48 * 2 ** 20
MESH = pl.DeviceIdType.MESH


def _cparams(*sem):
    return pltpu.CompilerParams(dimension_semantics=tuple(sem), vmem_limit_bytes=VMEM_LIMIT)


def _dot(a, b):
    return jnp.dot(a.astype(MXU_DTYPE), b.astype(MXU_DTYPE), preferred_element_type=F32)


def _dot_nt(a, b):
    return lax.dot_general(a.astype(MXU_DTYPE), b.astype(MXU_DTYPE), (((1,), (1,)), ((), ())),
                           preferred_element_type=F32)


def _dot_tn(a, b):
    return lax.dot_general(a.astype(MXU_DTYPE), b.astype(MXU_DTYPE), (((0,), (0,)), ((), ())),
                           preferred_element_type=F32)


def _split_dot(x, ones, passes):
    acc, rest = None, x
    for p in range(passes):
        piece = rest.astype(MXU_DTYPE)
        part = jnp.dot(piece, ones, preferred_element_type=F32)
        acc = part if acc is None else acc + part
        if p + 1 < passes:
            rest = rest - piece.astype(F32)
    return acc


def _split_dot_lhs(ones, x, passes):
    acc, rest = None, x
    for p in range(passes):
        piece = rest.astype(MXU_DTYPE)
        part = jnp.dot(ones, piece, preferred_element_type=F32)
        acc = part if acc is None else acc + part
        if p + 1 < passes:
            rest = rest - piece.astype(F32)
    return acc


def _iota2(shape, axis):
    return lax.broadcasted_iota(jnp.int32, shape, axis)


def _matmul(a, b, *, tm, tn, tk, name, out_dtype=F32, add=None, add_scale=1.0):
    m, k = a.shape
    _, n = b.shape
    assert m % tm == 0 and n % tn == 0 and k % tk == 0, (a.shape, b.shape, tm, tn, tk)
    nk = k // tk

    def body(*refs):
        if add is None:
            a_ref, b_ref, o_ref = refs[:3]
            c_ref, scr = None, refs[3:]
        else:
            a_ref, b_ref, c_ref, o_ref = refs[:4]
            scr = refs[4:]
        part = _dot(a_ref[...], b_ref[...])

        def finish(acc):
            if c_ref is not None:
                acc = acc + add_scale * c_ref[...]
            o_ref[...] = acc.astype(out_dtype)

        if nk == 1:
            finish(part)
        else:
            acc_ref = scr[0]
            kk = pl.program_id(2)

            @pl.when(kk == 0)
            def _():
                acc_ref[...] = part

            @pl.when(kk > 0)
            def _():
                acc_ref[...] += part

            @pl.when(kk == nk - 1)
            def _():
                finish(acc_ref[...])

    in_specs = [pl.BlockSpec((tm, tk), lambda i, j, kk: (i, kk)), pl.BlockSpec((tk, tn), lambda i, j, kk: (kk, j))]
    operands = [a, b]
    if add is not None:
        in_specs.append(pl.BlockSpec((tm, tn), lambda i, j, kk: (i, j)))
        operands.append(add)
    return pl.pallas_call(
        body, name=name, grid=(m // tm, n // tn, nk), in_specs=in_specs,
        out_specs=pl.BlockSpec((tm, tn), lambda i, j, kk: (i, j)),
        out_shape=jax.ShapeDtypeStruct((m, n), out_dtype),
        scratch_shapes=[pltpu.VMEM((tm, tn), F32)] if nk > 1 else [],
        compiler_params=_cparams("parallel", "parallel", "arbitrary"),
    )(*operands)


def _ln_stats(pre):
    mu = jnp.mean(pre, axis=-1, keepdims=True)
    xc = pre - mu
    var = jnp.mean(xc * xc, axis=-1, keepdims=True)
    rstd = lax.rsqrt(var + LN_EPS)
    return xc * rstd, rstd


def _ln_fwd(xin, branch, g, b, name):
    t, d = xin.shape
    tile = 256

    def body(x_ref, br_ref, g_ref, b_ref, pre_ref, y_ref):
        pre = ALPHA * x_ref[...] + br_ref[...]
        xhat, _ = _ln_stats(pre)
        pre_ref[...] = pre
        y_ref[...] = xhat * g_ref[...] + b_ref[...]

    row = pl.BlockSpec((tile, d), lambda i: (i, 0))
    vec = pl.BlockSpec((1, d), lambda i: (0, 0))
    return pl.pallas_call(
        body, name=name, grid=(t // tile,), in_specs=[row, row, vec, vec], out_specs=[row, row],
        out_shape=[jax.ShapeDtypeStruct((t, d), F32)] * 2, compiler_params=_cparams("parallel"),
    )(xin, branch, g.reshape(1, d), b.reshape(1, d))


def _ln_bwd(dy, pre, g, name):
    t, d = dy.shape
    tile = 256

    def body(dy_ref, pre_ref, g_ref, dpre_ref, dgb_ref):
        dyv = dy_ref[...]
        xhat, rstd = _ln_stats(pre_ref[...])
        dxh = dyv * g_ref[...]
        m1 = jnp.mean(dxh, axis=-1, keepdims=True)
        m2 = jnp.mean(dxh * xhat, axis=-1, keepdims=True)
        dpre_ref[...] = rstd * (dxh - m1 - xhat * m2)

        @pl.when(pl.program_id(0) == 0)
        def _():
            dgb_ref[...] = jnp.zeros_like(dgb_ref)

        dgb_ref[0:1, :] += jnp.sum(dyv * xhat, axis=0, keepdims=True)
        dgb_ref[1:2, :] += jnp.sum(dyv, axis=0, keepdims=True)

    row = pl.BlockSpec((tile, d), lambda i: (i, 0))
    return pl.pallas_call(
        body, name=name, grid=(t // tile,), in_specs=[row, row, pl.BlockSpec((1, d), lambda i: (0, 0))],
        out_specs=[row, pl.BlockSpec((8, d), lambda i: (0, 0))],
        out_shape=[jax.ShapeDtypeStruct((t, d), F32), jax.ShapeDtypeStruct((8, d), F32)],
        compiler_params=_cparams("arbitrary"),
    )(dy, pre, g.reshape(1, d))


def _swiglu_fwd(gu, name):
    t = gu.shape[0]
    tile = 256

    def body(gu_ref, h_ref):
        gate = gu_ref[:, :D_FF]
        up = gu_ref[:, D_FF:]
        h_ref[...] = (gate * (1.0 / (1.0 + jnp.exp(-gate))) * up).astype(h_ref.dtype)

    return pl.pallas_call(
        body, name=name, grid=(t // tile,), in_specs=[pl.BlockSpec((tile, 2 * D_FF), lambda i: (i, 0))],
        out_specs=pl.BlockSpec((tile, D_FF), lambda i: (i, 0)),
        out_shape=jax.ShapeDtypeStruct((t, D_FF), MXU_DTYPE), compiler_params=_cparams("parallel"),
    )(gu)


def _swiglu_bwd(dh, gu, name):
    t = gu.shape[0]
    tile = 256

    def body(dh_ref, gu_ref, dgu_ref):
        gate = gu_ref[:, :D_FF]
        up = gu_ref[:, D_FF:]
        dhv = dh_ref[...]
        sig = 1.0 / (1.0 + jnp.exp(-gate))
        dgu_ref[:, :D_FF] = (dhv * up * sig * (1.0 + gate * (1.0 - sig))).astype(dgu_ref.dtype)
        dgu_ref[:, D_FF:] = (dhv * gate * sig).astype(dgu_ref.dtype)

    return pl.pallas_call(
        body, name=name, grid=(t // tile,),
        in_specs=[pl.BlockSpec((tile, D_FF), lambda i: (i, 0)), pl.BlockSpec((tile, 2 * D_FF), lambda i: (i, 0))],
        out_specs=pl.BlockSpec((tile, 2 * D_FF), lambda i: (i, 0)),
        out_shape=jax.ShapeDtypeStruct((t, 2 * D_FF), MXU_DTYPE), compiler_params=_cparams("parallel"),
    )(dh, gu)


def _loss_kernel(y, target, name):
    t, d = y.shape
    tile = 512

    def body(y_ref, t_ref, dy_ref, l_ref):
        err = y_ref[...] - t_ref[...]
        dy_ref[...] = err * (1.0 / d)

        @pl.when(pl.program_id(0) == 0)
        def _():
            l_ref[...] = jnp.zeros_like(l_ref)

        l_ref[...] += jnp.sum(err * err) * (0.5 / d)

    row = pl.BlockSpec((tile, d), lambda i: (i, 0))
    return pl.pallas_call(
        body, name=name, grid=(t // tile,), in_specs=[row, row],
        out_specs=[row, pl.BlockSpec((8, 128), lambda i: (0, 0))],
        out_shape=[jax.ShapeDtypeStruct((t, d), F32), jax.ShapeDtypeStruct((8, 128), F32)],
        compiler_params=_cparams("arbitrary"),
    )(y, target)


def _adamw(w, g, m, v, name):
    nl, r, c = w.shape
    tr = r
    for cand in (256, 352, 128, 64, 16, 8):
        if r % cand == 0:
            tr = cand
            break

    def body(w_ref, g_ref, m_ref, v_ref, d_ref, nm_ref, nv_ref):
        gv = g_ref[...]
        nm = ADAM_B1 * m_ref[...] + (1.0 - ADAM_B1) * gv
        nv = ADAM_B2 * v_ref[...] + (1.0 - ADAM_B2) * (gv * gv)
        m_hat = nm / (1.0 - ADAM_B1 ** ADAM_STEP)
        v_hat = nv / (1.0 - ADAM_B2 ** ADAM_STEP)
        d_ref[...] = -ADAM_LR * (m_hat / (jnp.sqrt(v_hat) + ADAM_EPS) + ADAM_WD * w_ref[...])
        nm_ref[...] = nm
        nv_ref[...] = nv

    blk = pl.BlockSpec((1, tr, c), lambda l, i: (l, i, 0))
    return pl.pallas_call(
        body, name=name, grid=(nl, r // tr), in_specs=[blk] * 4, out_specs=[blk] * 3,
        out_shape=[jax.ShapeDtypeStruct(w.shape, F32)] * 3, compiler_params=_cparams("parallel", "parallel"),
    )(w, g, m, v)


def _shift_down(u, k, rows):
    return jnp.where(rows >= k, pltpu.roll(u, k, 0), 0.0)


def _shift_up(u, k, rows, s):
    return jnp.where(rows < s - k, pltpu.roll(u, s - k, 0), 0.0)


def _conv_fwd(proj, conv_w, nb, s, name):
    def body(b_ref, c_ref, h_ref, w_ref, o_ref):
        rows = _iota2((s, CONV_W), 0)
        u = c_ref[...] * h_ref[...]
        y = w_ref[2:3, :] * u + w_ref[1:2, :] * _shift_down(u, 1, rows) + w_ref[0:1, :] * _shift_down(u, 2, rows)
        o_ref[...] = b_ref[...] * y

    col = lambda j: pl.BlockSpec((s, CONV_W), lambda b: (b, j))
    return pl.pallas_call(
        body, name=name, grid=(nb,),
        in_specs=[col(9), col(10), col(11), pl.BlockSpec((8, CONV_W), lambda b: (0, 0))],
        out_specs=pl.BlockSpec((s, CONV_W), lambda b: (b, 0)),
        out_shape=jax.ShapeDtypeStruct((nb * s, CONV_W), F32), compiler_params=_cparams("parallel"),
    )(proj, proj, proj, conv_w)


def _conv_bwd(dmixed, proj, conv_w, nb, s, name):
    def body(do_ref, b_ref, c_ref, h_ref, w_ref, dg_ref, dw_ref):
        rows = _iota2((s, CONV_W), 0)
        cg, hg, bg, dout = c_ref[...], h_ref[...], b_ref[...], do_ref[...]
        u = cg * hg
        u1 = _shift_down(u, 1, rows)
        u2 = _shift_down(u, 2, rows)
        y = w_ref[2:3, :] * u + w_ref[1:2, :] * u1 + w_ref[0:1, :] * u2
        dy = dout * bg
        du = w_ref[2:3, :] * dy + w_ref[1:2, :] * _shift_up(dy, 1, rows, s) + w_ref[0:1, :] * _shift_up(dy, 2, rows, s)
        dg_ref[:, 0:CONV_W] = dout * y
        dg_ref[:, CONV_W:2 * CONV_W] = du * hg
        dg_ref[:, 2 * CONV_W:3 * CONV_W] = du * cg

        @pl.when(pl.program_id(0) == 0)
        def _():
            dw_ref[...] = jnp.zeros_like(dw_ref)

        dw_ref[0:1, :] += jnp.sum(dy * u2, axis=0, keepdims=True)
        dw_ref[1:2, :] += jnp.sum(dy * u1, axis=0, keepdims=True)
        dw_ref[2:3, :] += jnp.sum(dy * u, axis=0, keepdims=True)

    col = lambda j: pl.BlockSpec((s, CONV_W), lambda b: (b, j))
    return pl.pallas_call(
        body, name=name, grid=(nb,),
        in_specs=[col(3), col(9), col(10), col(11), pl.BlockSpec((8, CONV_W), lambda b: (0, 0))],
        out_specs=[pl.BlockSpec((s, 3 * CONV_W), lambda b: (b, 0)), pl.BlockSpec((8, CONV_W), lambda b: (0, 0))],
        out_shape=[jax.ShapeDtypeStruct((nb * s, 3 * CONV_W), F32), jax.ShapeDtypeStruct((8, CONV_W), F32)],
        compiler_params=_cparams("arbitrary"),
    )(dmixed, proj, proj, proj, conv_w)


def _slot_spec(s, base, width=HEAD_DIM):
    return pl.BlockSpec((None, None, s, width), lambda b, h: (b, base + h, 0, 0))


def _rows(i):
    return pl.ds(pl.multiple_of(i * BLK, BLK), BLK)


def _log_sigmoid_parts(z):
    e = jnp.exp(-jnp.abs(z))
    l1p = jnp.log(1.0 + e)
    lb = jnp.minimum(z, 0.0) - l1p
    return lb, lb - z, e


def _sb_fwd(qkv, name):
    nb, _, s, _ = qkv.shape
    nblk = s // BLK

    def body(q_ref, k_ref, v_ref, o_ref):
        rows = _iota2((BLK, BLK), 0)
        cols = _iota2((BLK, BLK), 1)
        later = (rows > cols).astype(MXU_DTYPE)

        def qblock(i, _):
            qi = q_ref[_rows(i), :] * 0.125

            def kblock(t, state):
                carry, acc = state
                j = i - t
                z = _dot_nt(qi, k_ref[_rows(j), :])
                strict = (cols + (j - i) * BLK) < rows
                lb, lr, _ = _log_sigmoid_parts(z)
                lr = jnp.where(strict, lr, 0.0)
                tail = _split_dot(lr, later, 2) + carry
                a = jnp.where(strict, jnp.exp(lb + tail), 0.0)
                acc = acc + _dot(a, v_ref[_rows(j), :])
                return carry + jnp.sum(lr, axis=-1, keepdims=True), acc

            _, acc = lax.fori_loop(0, i + 1, kblock, (jnp.zeros((BLK, 1), F32), jnp.zeros((BLK, HEAD_DIM), F32)))
            o_ref[_rows(i), :] = acc
            return 0

        lax.fori_loop(0, nblk, qblock, 0)

    return pl.pallas_call(
        body, name=name, grid=(nb, N_HEADS), in_specs=[_slot_spec(s, 0), _slot_spec(s, 4), _slot_spec(s, 8)],
        out_specs=pl.BlockSpec((None, None, s, HEAD_DIM), lambda b, h: (b, h, 0, 0)),
        out_shape=jax.ShapeDtypeStruct((nb, N_HEADS, s, HEAD_DIM), F32), compiler_params=_cparams("parallel", "parallel"),
    )(qkv, qkv, qkv)


def _sb_bwd(qkv, do, name):
    nb, _, s, _ = qkv.shape
    nblk = s // BLK

    def body(q_ref, k_ref, v_ref, do_ref, dq_ref, dk_ref, dv_ref, a_scr, dl_scr, beta_scr):
        rows = _iota2((BLK, BLK), 0)
        cols = _iota2((BLK, BLK), 1)
        later = (rows > cols).astype(MXU_DTYPE)
        earlier = (rows < cols).astype(MXU_DTYPE)
        dk_ref[...] = jnp.zeros_like(dk_ref)
        dv_ref[...] = jnp.zeros_like(dv_ref)

        def qblock(i, _):
            qi = q_ref[_rows(i), :] * 0.125
            doi = do_ref[_rows(i), :]

            def first(t, carry):
                j = i - t
                z = _dot_nt(qi, k_ref[_rows(j), :])
                strict = (cols + (j - i) * BLK) < rows
                lb, lr, e = _log_sigmoid_parts(z)
                lr = jnp.where(strict, lr, 0.0)
                tail = _split_dot(lr, later, 2) + carry
                a = jnp.where(strict, jnp.exp(lb + tail), 0.0)
                a_scr[j] = a
                dl_scr[j] = a * _dot_nt(doi, v_ref[_rows(j), :])
                beta_scr[j] = jnp.where(z >= 0.0, 1.0, e) / (1.0 + e)
                return carry + jnp.sum(lr, axis=-1, keepdims=True)

            lax.fori_loop(0, i + 1, first, jnp.zeros((BLK, 1), F32))

            def second(j, state):
                csum, dq = state
                dl = dl_scr[j]
                beta = beta_scr[j]
                strict = (cols + (j - i) * BLK) < rows
                before = _split_dot(dl, earlier, 2) + csum
                dz = jnp.where(strict, dl * (1.0 - beta) - beta * before, 0.0).astype(MXU_DTYPE)
                dq = dq + _dot(dz, k_ref[_rows(j), :])
                dk_ref[_rows(j), :] += _dot_tn(dz, qi)
                dv_ref[_rows(j), :] += _dot_tn(a_scr[j], doi)
                return csum + jnp.sum(dl, axis=-1, keepdims=True), dq

            _, dq = lax.fori_loop(0, i + 1, second, (jnp.zeros((BLK, 1), F32), jnp.zeros((BLK, HEAD_DIM), F32)))
            dq_ref[_rows(i), :] = dq * 0.125
            return 0

        lax.fori_loop(0, nblk, qblock, 0)

    out = pl.BlockSpec((None, None, s, HEAD_DIM), lambda b, h: (b, h, 0, 0))
    return pl.pallas_call(
        body, name=name, grid=(nb, N_HEADS),
        in_specs=[_slot_spec(s, 0), _slot_spec(s, 4), _slot_spec(s, 8), out], out_specs=[out] * 3,
        out_shape=[jax.ShapeDtypeStruct((nb, N_HEADS, s, HEAD_DIM), F32)] * 3,
        scratch_shapes=[pltpu.VMEM((nblk, BLK, BLK), F32)] * 3, compiler_params=_cparams("parallel", "parallel"),
    )(qkv, qkv, qkv, do)


def _head_spec(s, width):
    return pl.BlockSpec((None, None, s, width), lambda b, h: (b, h, 0, 0))


def _fox_fwd(qkv, ccol, crow, name):
    nb, _, s, _ = qkv.shape
    nblk = s // BLK

    def body(q_ref, k_ref, v_ref, cc_ref, cr_ref, o_ref, lse_ref):
        rows = _iota2((BLK, BLK), 0)
        cols = _iota2((BLK, BLK), 1)

        def qblock(i, _):
            qi = q_ref[_rows(i), :] * 0.125
            ci = cc_ref[_rows(i), :]

            def kblock(j, state):
                m, l, acc = state
                z = _dot_nt(qi, k_ref[_rows(j), :]) + (ci - cr_ref[j][0:1, :])
                z = jnp.where((cols + (j - i) * BLK) <= rows, z, NEG)
                m_new = jnp.maximum(m, jnp.max(z, axis=-1, keepdims=True))
                p = jnp.exp(z - m_new)
                scale = jnp.exp(m - m_new)
                l = scale * l + jnp.sum(p, axis=-1, keepdims=True)
                acc = scale * acc + _dot(p, v_ref[_rows(j), :])
                return m_new, l, acc

            init = (jnp.full((BLK, 1), NEG, F32), jnp.zeros((BLK, 1), F32), jnp.zeros((BLK, HEAD_DIM), F32))
            m, l, acc = lax.fori_loop(0, i + 1, kblock, init)
            o_ref[_rows(i), :] = acc / l
            lse_ref[_rows(i), :] = jnp.broadcast_to(m + jnp.log(l), (BLK, BLK))
            return 0

        lax.fori_loop(0, nblk, qblock, 0)

    crow_spec = pl.BlockSpec((None, None, nblk, 8, BLK), lambda b, h: (b, h, 0, 0, 0))
    return pl.pallas_call(
        body, name=name, grid=(nb, N_HEADS),
        in_specs=[_slot_spec(s, 24), _slot_spec(s, 28), _slot_spec(s, 32), _head_spec(s, BLK), crow_spec],
        out_specs=[_head_spec(s, HEAD_DIM), _head_spec(s, BLK)],
        out_shape=[jax.ShapeDtypeStruct((nb, N_HEADS, s, HEAD_DIM), F32), jax.ShapeDtypeStruct((nb, N_HEADS, s, BLK), F32)],
        compiler_params=_cparams("parallel", "parallel"),
    )(qkv, qkv, qkv, ccol, crow)


def _fox_bwd(qkv, do, lse, ccol, crow, name):
    nb, _, s, _ = qkv.shape
    nblk = s // BLK

    def body(q_ref, k_ref, v_ref, do_ref, lse_ref, cc_ref, cr_ref, dq_ref, dk_ref, dv_ref, dc_ref):
        rows = _iota2((BLK, BLK), 0)
        cols = _iota2((BLK, BLK), 1)
        dk_ref[...] = jnp.zeros_like(dk_ref)
        dv_ref[...] = jnp.zeros_like(dv_ref)
        dc_ref[...] = jnp.zeros_like(dc_ref)

        def qblock(i, _):
            qi = q_ref[_rows(i), :] * 0.125
            doi = do_ref[_rows(i), :]
            ci = cc_ref[_rows(i), :]
            lsei = lse_ref[_rows(i), :]

            def probs(j):
                z = _dot_nt(qi, k_ref[_rows(j), :]) + (ci - cr_ref[j][0:1, :])
                p = jnp.where((cols + (j - i) * BLK) <= rows, jnp.exp(z - lsei), 0.0)
                return p, _dot_nt(doi, v_ref[_rows(j), :])

            def row_term(j, acc):
                p, dp = probs(j)
                return acc + jnp.sum(p * dp, axis=-1, keepdims=True)

            di = lax.fori_loop(0, i + 1, row_term, jnp.zeros((BLK, 1), F32))

            def kblock(j, dq):
                kj = k_ref[_rows(j), :]
                p, dp = probs(j)
                ds = p * (dp - di)
                dc_ref[j] += jnp.broadcast_to(jnp.sum(ds, axis=0, keepdims=True), (8, BLK))
                ds = ds.astype(MXU_DTYPE)
                dk_ref[_rows(j), :] += _dot_tn(ds, qi)
                dv_ref[_rows(j), :] += _dot_tn(p, doi)
                return dq + _dot(ds, kj)

            dq = lax.fori_loop(0, i + 1, kblock, jnp.zeros((BLK, HEAD_DIM), F32))
            dq_ref[_rows(i), :] = dq * 0.125
            return 0

        lax.fori_loop(0, nblk, qblock, 0)

    crow_spec = pl.BlockSpec((None, None, nblk, 8, BLK), lambda b, h: (b, h, 0, 0, 0))
    wide, narrow = _head_spec(s, BLK), _head_spec(s, HEAD_DIM)
    return pl.pallas_call(
        body, name=name, grid=(nb, N_HEADS),
        in_specs=[_slot_spec(s, 24), _slot_spec(s, 28), _slot_spec(s, 32), narrow, wide, wide, crow_spec],
        out_specs=[narrow, narrow, narrow, crow_spec],
        out_shape=[jax.ShapeDtypeStruct((nb, N_HEADS, s, HEAD_DIM), F32)] * 3
        + [jax.ShapeDtypeStruct((nb, N_HEADS, nblk, 8, BLK), F32)],
        compiler_params=_cparams("parallel", "parallel"),
    )(qkv, qkv, qkv, do, lse, ccol, crow)


def _fox_gates_fwd(proj, f_bias, nb, s, name):
    chunk = 256

    def body(f_ref, b_ref, c_ref):
        lower = (_iota2((chunk, chunk), 0) >= _iota2((chunk, chunk), 1)).astype(MXU_DTYPE)
        carry = jnp.zeros((1, BLK), F32)
        for n in range(s // chunk):
            rows = pl.ds(n * chunk, chunk)
            lf, _, _ = _log_sigmoid_parts(f_ref[rows, :] + b_ref[0:1, :])
            c = _split_dot_lhs(lower, lf, 3) + carry
            c_ref[rows, :] = c
            carry = c[chunk - 1:chunk, :]

    return pl.pallas_call(
        body, name=name, grid=(nb,),
        in_specs=[pl.BlockSpec((s, BLK), lambda b: (b, (PROJ_PAD - BLK) // BLK)), pl.BlockSpec((8, BLK), lambda b: (0, 0))],
        out_specs=pl.BlockSpec((s, BLK), lambda b: (b, 0)),
        out_shape=jax.ShapeDtypeStruct((nb * s, BLK), F32), compiler_params=_cparams("parallel"),
    )(proj, f_bias)


def _fox_gates_bwd(dc, proj, f_bias, nb, s, name):
    chunk = 256

    def body(dc_ref, f_ref, b_ref, df_ref, db_ref):
        upper = (_iota2((chunk, chunk), 0) <= _iota2((chunk, chunk), 1)).astype(MXU_DTYPE)
        carry = jnp.zeros((1, BLK), F32)
        total = jnp.zeros((1, BLK), F32)
        for n in reversed(range(s // chunk)):
            rows = pl.ds(n * chunk, chunk)
            dlf = _split_dot_lhs(upper, dc_ref[rows, :], 3) + carry
            carry = dlf[0:1, :]
            pre = f_ref[rows, :] + b_ref[0:1, :]
            e = jnp.exp(-jnp.abs(pre))
            df = dlf * (jnp.where(pre >= 0.0, e, 1.0) / (1.0 + e))
            df_ref[rows, :] = df
            total = total + jnp.sum(df, axis=0, keepdims=True)

        @pl.when(pl.program_id(0) == 0)
        def _():
            db_ref[...] = jnp.zeros_like(db_ref)

        db_ref[0:1, :] += total

    return pl.pallas_call(
        body, name=name, grid=(nb,),
        in_specs=[pl.BlockSpec((s, BLK), lambda b: (b, 0)), pl.BlockSpec((s, BLK), lambda b: (b, (PROJ_PAD - BLK) // BLK)),
                  pl.BlockSpec((8, BLK), lambda b: (0, 0))],
        out_specs=[pl.BlockSpec((s, BLK), lambda b: (b, 0)), pl.BlockSpec((8, BLK), lambda b: (0, 0))],
        out_shape=[jax.ShapeDtypeStruct((nb * s, BLK), F32), jax.ShapeDtypeStruct((8, BLK), F32)],
        compiler_params=_cparams("arbitrary"),
    )(dc, proj, f_bias)


def _delta_kernel(do, o, name):
    nb, _, s, _ = do.shape

    def body(do_ref, o_ref, d_ref):
        d_ref[...] = jnp.broadcast_to(jnp.sum(do_ref[...] * o_ref[...], axis=-1, keepdims=True), (s, BLK))

    return pl.pallas_call(
        body, name=name, grid=(nb, N_HEADS), in_specs=[_head_spec(s, HEAD_DIM)] * 2, out_specs=_head_spec(s, BLK),
        out_shape=jax.ShapeDtypeStruct((nb, N_HEADS, s, BLK), F32), compiler_params=_cparams("parallel", "parallel"),
    )(do, o)


def _t5_bucket_np(dist):
    max_exact = REL_BUCKETS // 2
    nf = np.maximum(dist, 1).astype(np.float32)
    large = max_exact + (np.log(nf / max_exact) / math.log(2048 / max_exact) * (REL_BUCKETS - max_exact)).astype(np.int32)
    large = np.minimum(large, REL_BUCKETS - 1)
    return np.where(dist < max_exact, dist, large)


def _bucket_table():
    qi = np.arange(BLK)[:, None]
    kj = np.arange(2 * BLK)[None, :]
    dist = qi + BLK - kj
    tables = []
    for window, dil in DIL_PATTERNS:
        in_band = (dist >= 0) & (dist <= window // dil)
        tables.append(np.where(in_band, _t5_bucket_np(np.maximum(dist, 0) * dil), -1).astype(np.int32))
    return np.stack(tables)


def _to_residue(t, dil):
    if dil == 1:
        return t
    *lead, s, e = t.shape
    return jnp.swapaxes(t.reshape(*lead, s // dil, dil, e), -3, -2).reshape(*lead, s, e)


def _from_residue(t, dil):
    if dil == 1:
        return t
    *lead, s, e = t.shape
    return jnp.swapaxes(t.reshape(*lead, dil, s // dil, e), -3, -2).reshape(*lead, s, e)


def _pat_spec(s, width, base=0):
    return pl.BlockSpec((3, None, None, s, width), lambda b, h: (0, b, base + h, 0, 0))


def _dil_fwd(qkvp, bias, name):
    _, nb, _, s, _ = qkvp.shape
    nblk = s // BLK

    def body(q_ref, k_ref, v_ref, b_ref, o_ref, lse_ref):
        for p, (_, dil) in enumerate(DIL_PATTERNS):
            seg = s // dil // BLK

            def block(b, _, p=p, seg=seg):
                cur = _rows(b)
                prev = _rows(jnp.maximum(b - 1, 0))
                qb = q_ref[p, cur, :] * 0.125
                zp = _dot_nt(qb, k_ref[p, prev, :]) + b_ref[p, :, 0:BLK]
                zp = jnp.where(b % seg > 0, zp, NEG)
                zc = _dot_nt(qb, k_ref[p, cur, :]) + b_ref[p, :, BLK:2 * BLK]
                m = jnp.maximum(jnp.max(zp, axis=-1, keepdims=True), jnp.max(zc, axis=-1, keepdims=True))
                pp = jnp.exp(zp - m)
                pc = jnp.exp(zc - m)
                den = jnp.sum(pp, axis=-1, keepdims=True) + jnp.sum(pc, axis=-1, keepdims=True)
                o_ref[p, cur, :] = (_dot(pp, v_ref[p, prev, :]) + _dot(pc, v_ref[p, cur, :])) / den
                lse_ref[p, cur, :] = jnp.broadcast_to(m + jnp.log(den), (BLK, BLK))
                return 0

            lax.fori_loop(0, nblk, block, 0)

    bias_spec = pl.BlockSpec((3, None, BLK, 2 * BLK), lambda b, h: (0, h, 0, 0))
    return pl.pallas_call(
        body, name=name, grid=(nb, N_HEADS),
        in_specs=[_pat_spec(s, HEAD_DIM, 0), _pat_spec(s, HEAD_DIM, 4), _pat_spec(s, HEAD_DIM, 8), bias_spec],
        out_specs=[_pat_spec(s, HEAD_DIM), _pat_spec(s, BLK)],
        out_shape=[jax.ShapeDtypeStruct((3, nb, N_HEADS, s, HEAD_DIM), F32), jax.ShapeDtypeStruct((3, nb, N_HEADS, s, BLK), F32)],
        compiler_params=_cparams("parallel", "parallel"),
    )(qkvp, qkvp, qkvp, bias)


def _dil_combine(o, lse, name):
    _, nb, _, s, _ = o.shape

    def body(o_ref, l_ref, out_ref, lse_ref):
        m = jnp.maximum(jnp.maximum(l_ref[0], l_ref[1]), l_ref[2])
        w = [jnp.exp(l_ref[p] - m) for p in range(3)]
        den = w[0] + w[1] + w[2]
        num = w[0][:, :HEAD_DIM] * o_ref[0] + w[1][:, :HEAD_DIM] * o_ref[1] + w[2][:, :HEAD_DIM] * o_ref[2]
        out_ref[...] = num / den[:, :HEAD_DIM]
        lse_ref[...] = m + jnp.log(den)

    return pl.pallas_call(
        body, name=name, grid=(nb, N_HEADS), in_specs=[_pat_spec(s, HEAD_DIM), _pat_spec(s, BLK)],
        out_specs=[_head_spec(s, HEAD_DIM), _head_spec(s, BLK)],
        out_shape=[jax.ShapeDtypeStruct((nb, N_HEADS, s, HEAD_DIM), F32), jax.ShapeDtypeStruct((nb, N_HEADS, s, BLK), F32)],
        compiler_params=_cparams("parallel", "parallel"),
    )(o, lse)


def _dil_bwd(qkvp, dop, lsep, deltap, bias, name):
    _, nb, _, s, _ = qkvp.shape
    nblk = s // BLK

    def body(q_ref, k_ref, v_ref, do_ref, lse_ref, dl_ref, b_ref, dq_ref, dk_ref, dv_ref, g_ref):
        dk_ref[...] = jnp.zeros_like(dk_ref)
        dv_ref[...] = jnp.zeros_like(dv_ref)
        g_ref[...] = jnp.zeros_like(g_ref)
        for p, (_, dil) in enumerate(DIL_PATTERNS):
            seg = s // dil // BLK

            def block(b, _, p=p, seg=seg):
                cur = _rows(b)
                prev = _rows(jnp.maximum(b - 1, 0))
                qb = q_ref[p, cur, :] * 0.125
                dob = do_ref[p, cur, :]
                lse = lse_ref[p, cur, :]
                dlt = dl_ref[p, cur, :]
                kp, kc = k_ref[p, prev, :], k_ref[p, cur, :]
                zp = _dot_nt(qb, kp) + b_ref[p, :, 0:BLK]
                zp = jnp.where(b % seg > 0, zp, NEG)
                zc = _dot_nt(qb, kc) + b_ref[p, :, BLK:2 * BLK]
                pp = jnp.exp(zp - lse)
                pc = jnp.exp(zc - lse)
                dsp = pp * (_dot_nt(dob, v_ref[p, prev, :]) - dlt)
                dsc = pc * (_dot_nt(dob, v_ref[p, cur, :]) - dlt)
                g_ref[p, :, 0:BLK] += dsp
                g_ref[p, :, BLK:2 * BLK] += dsc
                dsp = dsp.astype(MXU_DTYPE)
                dsc = dsc.astype(MXU_DTYPE)
                dq_ref[p, cur, :] = (_dot(dsp, kp) + _dot(dsc, kc)) * 0.125
                dk_ref[p, prev, :] += _dot_tn(dsp, qb)
                dk_ref[p, cur, :] += _dot_tn(dsc, qb)
                dv_ref[p, prev, :] += _dot_tn(pp, dob)
                dv_ref[p, cur, :] += _dot_tn(pc, dob)
                return 0

            lax.fori_loop(0, nblk, block, 0)

    bias_spec = pl.BlockSpec((3, None, BLK, 2 * BLK), lambda b, h: (0, h, 0, 0))
    narrow, wide = _pat_spec(s, HEAD_DIM), _pat_spec(s, BLK)
    return pl.pallas_call(
        body, name=name, grid=(nb, N_HEADS),
        in_specs=[_pat_spec(s, HEAD_DIM, 0), _pat_spec(s, HEAD_DIM, 4), _pat_spec(s, HEAD_DIM, 8), narrow, wide, wide, bias_spec],
        out_specs=[narrow, narrow, narrow, pl.BlockSpec((None, None, 3, BLK, 2 * BLK), lambda b, h: (b, h, 0, 0, 0))],
        out_shape=[jax.ShapeDtypeStruct((3, nb, N_HEADS, s, HEAD_DIM), F32)] * 3
        + [jax.ShapeDtypeStruct((nb, N_HEADS, 3, BLK, 2 * BLK), F32)],
        compiler_params=_cparams("parallel", "parallel"),
    )(qkvp, qkvp, qkvp, dop, lsep, deltap, bias)


def _bucket_reduce(gbias, table, name):
    nb = gbias.shape[0]

    def body(g_ref, t_ref, o_ref):
        row = _iota2((8, BLK), 0)
        lane = _iota2((8, BLK), 1)
        gsum = [[sum(g_ref[b, h, p] for b in range(nb)) for p in range(3)] for h in range(N_HEADS)]

        def bucket(k, acc):
            for h in range(N_HEADS):
                tot = sum(jnp.sum(jnp.where(t_ref[p] == k, gsum[h][p], 0.0)) for p in range(3))
                acc = acc + jnp.where((row == h) & (lane == k), tot, 0.0)
            return acc

        o_ref[...] = lax.fori_loop(0, REL_BUCKETS, bucket, jnp.zeros((8, BLK), F32))

    vm = pl.BlockSpec(memory_space=pltpu.VMEM)
    return pl.pallas_call(
        body, name=name, in_specs=[vm, vm], out_specs=vm, out_shape=jax.ShapeDtypeStruct((8, BLK), F32),
        compiler_params=pltpu.CompilerParams(vmem_limit_bytes=VMEM_LIMIT),
    )(gbias, table)


def _place():
    x, y, c = lax.axis_index("x"), lax.axis_index("y"), lax.axis_index("c")
    others = [(1 - x, y), (x, 1 - y), (1 - x, 1 - y)]
    return x, y, c, others


def _remote(src, dst, send_sem, recv_sem, to):
    return pltpu.make_async_remote_copy(src_ref=src, dst_ref=dst, send_sem=send_sem, recv_sem=recv_sem,
                                        device_id=to, device_id_type=MESH)


def _gather_weights(wp):
    nl, r, w = wp.shape

    def body(wp_ref, out_ref, send_sems, recv_sems, local_sem):
        x, y, c, others = _place()
        me = 2 * x + y
        sibling = (x, y, 1 - c)
        mine = pltpu.make_async_copy(wp_ref, out_ref.at[me], local_sem)
        mine.start()
        sends = [_remote(wp_ref.at[c], out_ref.at[me, c], send_sems.at[k], recv_sems.at[k], (ox, oy, c))
                 for k, (ox, oy) in enumerate(others)]
        for cp in sends:
            cp.start()
        passed = []
        for k, (ox, oy) in enumerate(others):
            landed = out_ref.at[2 * ox + oy, c]
            _remote(landed, landed, send_sems.at[k], recv_sems.at[k], (ox, oy, c)).wait_recv()
            cp = _remote(landed, landed, send_sems.at[3 + k], recv_sems.at[3 + k], sibling)
            cp.start()
            passed.append(cp)
        for k, (ox, oy) in enumerate(others):
            theirs = out_ref.at[2 * ox + oy, 1 - c]
            _remote(theirs, theirs, send_sems.at[3 + k], recv_sems.at[3 + k], sibling).wait_recv()
        for cp in sends + passed:
            cp.wait_send()
        mine.wait()

    hbm = pl.BlockSpec(memory_space=pl.ANY)
    return pl.pallas_call(
        body, name="gather_weights", in_specs=[hbm], out_specs=hbm,
        out_shape=jax.ShapeDtypeStruct((N_CHIPS, nl, r, w), wp.dtype),
        scratch_shapes=[pltpu.SemaphoreType.DMA((6,)), pltpu.SemaphoreType.DMA((6,)), pltpu.SemaphoreType.DMA],
    )(wp)


def _swap_layers(g):
    _, ns, r, w = g.shape

    def body(g_ref, out_ref, send_sem, recv_sem):
        x, y, c, _ = _place()
        cp = _remote(g_ref.at[1 - c], out_ref, send_sem, recv_sem, (x, y, 1 - c))
        cp.start()
        cp.wait()

    hbm = pl.BlockSpec(memory_space=pl.ANY)
    return pl.pallas_call(
        body, name="swap_layers", in_specs=[hbm], out_specs=hbm, out_shape=jax.ShapeDtypeStruct((ns, r, w), g.dtype),
        scratch_shapes=[pltpu.SemaphoreType.DMA, pltpu.SemaphoreType.DMA],
    )(g)


def _pair_sum(g, other, core):
    _, ns, r, w = g.shape
    tw = 256

    def body(core_ref, g_ref, o_ref, out_ref):
        out_ref[...] = (g_ref[...] + o_ref[...]).astype(out_ref.dtype)

    grid_spec = pltpu.PrefetchScalarGridSpec(
        num_scalar_prefetch=1, grid=(ns, w // tw),
        in_specs=[pl.BlockSpec((None, None, r, tw), lambda k, j, core_ref: (core_ref[0], k, 0, j)),
                  pl.BlockSpec((None, r, tw), lambda k, j, core_ref: (k, 0, j))],
        out_specs=pl.BlockSpec((None, r, tw), lambda k, j, core_ref: (k, 0, j)))
    return pl.pallas_call(
        body, name="pair_sum", grid_spec=grid_spec, out_shape=jax.ShapeDtypeStruct((ns, r, w), MXU_DTYPE),
        compiler_params=_cparams("parallel", "parallel"),
    )(core.reshape(1).astype(jnp.int32), g, other)


def _scatter_shards(p):
    ns, r, w = p.shape

    def body(p_ref, q_ref, send_sems, recv_sems, local_sem):
        x, y, c, others = _place()
        me = 2 * x + y
        mine = pltpu.make_async_copy(p_ref.at[me], q_ref.at[me], local_sem)
        mine.start()
        sends = [_remote(p_ref.at[2 * ox + oy], q_ref.at[me], send_sems.at[k], recv_sems.at[k], (ox, oy, c))
                 for k, (ox, oy) in enumerate(others)]
        for cp in sends:
            cp.start()
        for k, (ox, oy) in enumerate(others):
            slot = q_ref.at[2 * ox + oy]
            _remote(slot, slot, send_sems.at[k], recv_sems.at[k], (ox, oy, c)).wait_recv()
        for cp in sends:
            cp.wait_send()
        mine.wait()

    hbm = pl.BlockSpec(memory_space=pl.ANY)
    return pl.pallas_call(
        body, name="scatter_shards", in_specs=[hbm], out_specs=hbm, out_shape=jax.ShapeDtypeStruct(p.shape, p.dtype),
        scratch_shapes=[pltpu.SemaphoreType.DMA((3,)), pltpu.SemaphoreType.DMA((3,)), pltpu.SemaphoreType.DMA],
    )(p)


def _chip_sum(q):
    ns, r, w = q.shape

    def body(q_ref, out_ref):
        out_ref[...] = ((q_ref[0].astype(F32) + q_ref[1].astype(F32)) + q_ref[2].astype(F32)) + q_ref[3].astype(F32)

    tw = 128
    return pl.pallas_call(
        body, name="chip_sum", grid=(w // tw,), in_specs=[pl.BlockSpec((ns, r, tw), lambda j: (0, 0, j))],
        out_specs=pl.BlockSpec((r, tw), lambda j: (0, j)), out_shape=jax.ShapeDtypeStruct((r, w), F32),
        compiler_params=_cparams("parallel"),
    )(q)


def _share_layers(gl):
    r, w = gl.shape

    def body(gl_ref, out_ref, send_sem, recv_sem, local_sem):
        x, y, c, _ = _place()
        mine = pltpu.make_async_copy(gl_ref, out_ref.at[c], local_sem)
        mine.start()
        cp = _remote(gl_ref, out_ref.at[c], send_sem, recv_sem, (x, y, 1 - c))
        cp.start()
        cp.wait_send()
        theirs = out_ref.at[1 - c]
        _remote(theirs, theirs, send_sem, recv_sem, (x, y, 1 - c)).wait_recv()
        mine.wait()

    hbm = pl.BlockSpec(memory_space=pl.ANY)
    return pl.pallas_call(
        body, name="share_layers", in_specs=[hbm], out_specs=hbm, out_shape=jax.ShapeDtypeStruct((2, r, w), gl.dtype),
        scratch_shapes=[pltpu.SemaphoreType.DMA, pltpu.SemaphoreType.DMA, pltpu.SemaphoreType.DMA],
    )(gl)


def _gather_small(pk, name):
    rows, w = pk.shape

    def body(pk_ref, all_ref, sum_ref, send_sems, recv_sems):
        x, y, c, _ = _place()
        me = 4 * x + 2 * y + c
        all_ref[me] = pk_ref[...]
        flips = [(fx, fy, fc) for fx in (0, 1) for fy in (0, 1) for fc in (0, 1)][1:]
        peers = [(x ^ fx, y ^ fy, c ^ fc) for fx, fy, fc in flips]
        sends = [_remote(pk_ref, all_ref.at[me], send_sems.at[k], recv_sems.at[k], peer) for k, peer in enumerate(peers)]
        for cp in sends:
            cp.start()
        for k, (px, py, pc) in enumerate(peers):
            slot = all_ref.at[4 * px + 2 * py + pc]
            _remote(slot, slot, send_sems.at[k], recv_sems.at[k], (px, py, pc)).wait_recv()
        for cp in sends:
            cp.wait_send()
        total = all_ref[0]
        for d in range(1, N_DEV):
            total = total + all_ref[d]
        sum_ref[...] = total

    vm = pl.BlockSpec(memory_space=pltpu.VMEM)
    return pl.pallas_call(
        body, name=name, in_specs=[vm], out_specs=[vm, vm],
        out_shape=[jax.ShapeDtypeStruct((N_DEV, rows, w), F32), jax.ShapeDtypeStruct((rows, w), F32)],
        scratch_shapes=[pltpu.SemaphoreType.DMA((7,)), pltpu.SemaphoreType.DMA((7,))],
    )(pk)


def _heads_to_cols(t):
    nb, h, s, e = t.shape
    return jnp.swapaxes(t, 1, 2).reshape(nb * s, h * e)


def _cols_to_heads(t, nb, s):
    h = t.shape[1] // HEAD_DIM
    return jnp.swapaxes(t.reshape(nb, s, h, HEAD_DIM), 1, 2)


def _row_layout(c, nb, s):
    ch = jnp.swapaxes(c[:, :N_HEADS].reshape(nb, s, N_HEADS), 1, 2)
    ccol = jnp.broadcast_to(ch[..., None], (nb, N_HEADS, s, BLK))
    crow = jnp.broadcast_to(ch.reshape(nb, N_HEADS, s // BLK, 1, BLK), (nb, N_HEADS, s // BLK, 8, BLK))
    return ccol, crow


def _dil_bias(rel_bias, name):
    def body(rel_ref, t_ref, o_ref):
        for p in range(len(DIL_PATTERNS)):
            table = t_ref[p]

            def bucket(k, accs, table=table):
                return tuple(jnp.where(table == k, rel_ref[k, h], acc) for h, acc in enumerate(accs))

            accs = lax.fori_loop(0, REL_BUCKETS, bucket, tuple(jnp.full((BLK, 2 * BLK), NEG, F32) for _ in range(N_HEADS)))
            for h in range(N_HEADS):
                o_ref[p, h] = accs[h]

    vm = pl.BlockSpec(memory_space=pltpu.VMEM)
    return pl.pallas_call(
        body, name=name, in_specs=[pl.BlockSpec(memory_space=pltpu.SMEM), vm], out_specs=vm,
        out_shape=jax.ShapeDtypeStruct((len(DIL_PATTERNS), N_HEADS, BLK, 2 * BLK), F32),
        compiler_params=pltpu.CompilerParams(vmem_limit_bytes=VMEM_LIMIT),
    )(rel_bias, jnp.asarray(_bucket_table()))


def _layer_forward(x, wts, small, nb, s, tag):
    proj = _matmul(x, wts["w_in"], tm=512, tn=640, tk=D_MODEL, name=f"proj_{tag}")
    qkv = jnp.swapaxes(proj[:, :W_QKV].reshape(nb, s, N_SLOTS, HEAD_DIM), 1, 2).astype(MXU_DTYPE)

    o_sb = _sb_fwd(qkv, f"sb_fwd_{tag}")

    dl = qkv[:, 12:24]
    qkvp = jnp.stack([_to_residue(dl, dil) for _, dil in DIL_PATTERNS])
    bias = _dil_bias(small["rel_bias"], f"dil_bias_{tag}")
    o_p, lse_p = _dil_fwd(qkvp, bias, f"dil_fwd_{tag}")
    o_nat = jnp.stack([_from_residue(o_p[p], dil) for p, (_, dil) in enumerate(DIL_PATTERNS)])
    lse_nat = jnp.stack([_from_residue(lse_p[p], dil) for p, (_, dil) in enumerate(DIL_PATTERNS)])
    o_dl, lse_dl = _dil_combine(o_nat, lse_nat, f"dil_mix_{tag}")

    fb = jnp.zeros((8, BLK), F32).at[0, :N_HEADS].set(small["f_bias"])
    csum = _fox_gates_fwd(proj, fb, nb, s, f"fox_gates_{tag}")
    ccol, crow = _row_layout(csum, nb, s)
    o_fx, lse_fx = _fox_fwd(qkv, ccol, crow, f"fox_fwd_{tag}")

    cw = jnp.zeros((8, CONV_W), F32).at[:3].set(small["conv_w"])
    o_cv = _conv_fwd(proj, cw, nb, s, f"conv_fwd_{tag}")

    mixed = jnp.concatenate([_heads_to_cols(o_sb), _heads_to_cols(o_dl), _heads_to_cols(o_fx), o_cv], axis=-1).astype(MXU_DTYPE)
    mix = _matmul(mixed, wts["w_out"], tm=512, tn=1024, tk=D_MODEL, name=f"out_proj_{tag}")
    pre1, x1 = _ln_fwd(x, mix, small["ln1_g"], small["ln1_b"], f"ln1_fwd_{tag}")
    gu = _matmul(x1, wts["w_gu"], tm=512, tn=512, tk=D_MODEL, name=f"ffn_in_{tag}")
    hid = _swiglu_fwd(gu, f"swiglu_fwd_{tag}")
    ffn = _matmul(hid, wts["w_down"], tm=512, tn=1024, tk=D_FF, name=f"ffn_out_{tag}")
    pre2, x2 = _ln_fwd(x1, ffn, small["ln2_g"], small["ln2_b"], f"ln2_fwd_{tag}")
    saved = dict(x=x, proj=proj, qkv=qkv, qkvp=qkvp, bias=bias, o_dl=o_dl, lse_dl=lse_dl, fb=fb, ccol=ccol, crow=crow,
                 o_fx=o_fx, lse_fx=lse_fx, cw=cw, mixed=mixed, pre1=pre1, x1=x1, gu=gu, hid=hid, pre2=pre2)
    return x2, saved


def _layer_backward(dx2, sv, wts, small, nb, s, tag):
    t = nb * s
    dpre2, dgb2 = _ln_bwd(dx2, sv["pre2"], small["ln2_g"], f"ln2_bwd_{tag}")
    dpre2_b = dpre2.astype(MXU_DTYPE)
    dhid = _matmul(dpre2_b, wts["w_down_t"], tm=512, tn=1408, tk=D_MODEL, name=f"ffn_out_dx_{tag}")
    dw_down = _matmul(sv["hid"].T, dpre2_b, tm=704, tn=1024, tk=1024, name=f"ffn_out_dw_{tag}")
    dgu = _swiglu_bwd(dhid, sv["gu"], f"swiglu_bwd_{tag}")
    dx1 = _matmul(dgu, wts["w_gu_t"], tm=512, tn=1024, tk=1408, name=f"ffn_in_dx_{tag}", add=dpre2, add_scale=ALPHA)
    dw_gu = _matmul(sv["x1"].astype(MXU_DTYPE).T, dgu, tm=512, tn=512, tk=1024, name=f"ffn_in_dw_{tag}")

    dpre1, dgb1 = _ln_bwd(dx1, sv["pre1"], small["ln1_g"], f"ln1_bwd_{tag}")
    dpre1_b = dpre1.astype(MXU_DTYPE)
    dmixed = _matmul(dpre1_b, wts["w_out_t"], tm=512, tn=1024, tk=D_MODEL, name=f"out_proj_dx_{tag}")
    dw_out = _matmul(sv["mixed"].T, dpre1_b, tm=512, tn=1024, tk=1024, name=f"out_proj_dw_{tag}")

    do_heads = _cols_to_heads(dmixed[:, :3 * CONV_W], nb, s)
    do_sb, do_dl, do_fx = do_heads[:, 0:4], do_heads[:, 4:8], do_heads[:, 8:12]
    qkv = sv["qkv"]

    dq_sb, dk_sb, dv_sb = _sb_bwd(qkv, do_sb.astype(MXU_DTYPE), f"sb_bwd_{tag}")

    delta_dl = _delta_kernel(do_dl, sv["o_dl"], f"dil_delta_{tag}")
    do_dl_b = do_dl.astype(MXU_DTYPE)
    dop = jnp.stack([_to_residue(do_dl_b, dil) for _, dil in DIL_PATTERNS])
    lsep = jnp.stack([_to_residue(sv["lse_dl"], dil) for _, dil in DIL_PATTERNS])
    deltap = jnp.stack([_to_residue(delta_dl, dil) for _, dil in DIL_PATTERNS])
    dqp, dkp, dvp, gbias = _dil_bwd(sv["qkvp"], dop, lsep, deltap, sv["bias"], f"dil_bwd_{tag}")
    unperm = lambda tp: sum(_from_residue(tp[p], dil) for p, (_, dil) in enumerate(DIL_PATTERNS))
    dq_dl, dk_dl, dv_dl = unperm(dqp), unperm(dkp), unperm(dvp)
    drel = _bucket_reduce(gbias, jnp.asarray(_bucket_table()), f"rel_bias_grad_{tag}")

    dq_fx, dk_fx, dv_fx, dcol = _fox_bwd(qkv, do_fx.astype(MXU_DTYPE), sv["lse_fx"], sv["ccol"], sv["crow"], f"fox_bwd_{tag}")
    dcs = -jnp.swapaxes(dcol[:, :, :, 0, :].reshape(nb, N_HEADS, s), 1, 2).reshape(t, N_HEADS)
    dcs = jnp.pad(dcs, ((0, 0), (0, BLK - N_HEADS)))
    dfx, dfb = _fox_gates_bwd(dcs, sv["proj"], sv["fb"], nb, s, f"fox_gates_bwd_{tag}")

    dgates, dcw = _conv_bwd(dmixed, sv["proj"], sv["cw"], nb, s, f"conv_bwd_{tag}")

    dslots = jnp.concatenate([dq_sb, dk_sb, dv_sb, dq_dl, dk_dl, dv_dl, dq_fx, dk_fx, dv_fx], axis=1)
    dproj = jnp.concatenate([_heads_to_cols(dslots), dgates, dfx], axis=-1).astype(MXU_DTYPE)
    dx = _matmul(dproj, wts["w_in_t"], tm=512, tn=1024, tk=640, name=f"proj_dx_{tag}", add=dpre1, add_scale=ALPHA)
    dw_in = _matmul(sv["x"].astype(MXU_DTYPE).T, dproj, tm=512, tn=640, tk=1024, name=f"proj_dw_{tag}")

    grads = dict(w_in=dw_in[:, :PROJ], w_out=dw_out, w_gate=dw_gu[:, :D_FF], w_up=dw_gu[:, D_FF:], w_down=dw_down,
                 ln1_g=dgb1[0], ln1_b=dgb1[1], ln2_g=dgb2[0], ln2_b=dgb2[1], conv_w=dcw[:3], f_bias=dfb[0, :N_HEADS],
                 rel_bias=drel[:N_HEADS, :REL_BUCKETS].T)
    return dx, grads


def _local_step(x, target, full, small_all):
    nb, s, d = x.shape
    h = x.reshape(nb * s, d)
    saved = []
    for layer in range(DEPTH):
        h, sv = _layer_forward(h, full[layer], small_all[layer], nb, s, f"l{layer}")
        saved.append(sv)
    dy, lossp = _loss_kernel(h, target.reshape(nb * s, d), "loss")
    grads = [None] * DEPTH
    for layer in reversed(range(DEPTH)):
        dy, grads[layer] = _layer_backward(dy, saved[layer], full[layer], small_all[layer], nb, s, f"l{layer}")
    return lossp, dy.reshape(nb, s, d), grads


_SHARD_SHAPES = (("w_in", (D_MODEL, PROJ // N_CHIPS)), ("w_out", (D_MODEL // N_CHIPS, D_MODEL)),
                 ("w_gate", (D_MODEL, D_FF // N_CHIPS)), ("w_up", (D_MODEL, D_FF // N_CHIPS)),
                 ("w_down", (D_FF // N_CHIPS, D_MODEL)))


def _pack_shards(parts, lead):
    flat = [parts[name].reshape(*lead, -1, D_MODEL) for name, _ in _SHARD_SHAPES]
    return jnp.concatenate(flat, axis=-2)


def _unpack_shards(slab):
    lead = slab.shape[:-2]
    out, row = {}, 0
    for name, (r, c) in _SHARD_SHAPES:
        n = r * c // D_MODEL
        out[name] = slab[..., row:row + n, :].reshape(*lead, r, c)
        row += n
    assert row == PACK_ROWS
    return out


def _full_weights(gathered):
    sh = _unpack_shards(gathered)
    cols = lambda t: jnp.moveaxis(t, 0, 2).reshape(DEPTH, t.shape[2], -1)
    rows = lambda t: jnp.moveaxis(t, 0, 1).reshape(DEPTH, -1, t.shape[3])
    w_in = jnp.pad(cols(sh["w_in"]), ((0, 0), (0, 0), (0, PROJ_PAD - PROJ)))
    w_gu = jnp.concatenate([cols(sh["w_gate"]), cols(sh["w_up"])], axis=-1)
    w_out, w_down = rows(sh["w_out"]), rows(sh["w_down"])
    return [dict(w_in=w_in[l], w_in_t=w_in[l].T, w_out=w_out[l], w_out_t=w_out[l].T, w_gu=w_gu[l], w_gu_t=w_gu[l].T,
                 w_down=w_down[l], w_down_t=w_down[l].T) for l in range(DEPTH)]


def _shard_major(grads):
    by_cols = lambda g: jnp.moveaxis(g.reshape(g.shape[0], N_CHIPS, -1), 1, 0)
    by_rows = lambda g: g.reshape(N_CHIPS, -1, g.shape[1])
    per_layer = []
    for g in grads:
        parts = dict(w_in=by_cols(g["w_in"]), w_out=by_rows(g["w_out"]), w_gate=by_cols(g["w_gate"]),
                     w_up=by_cols(g["w_up"]), w_down=by_rows(g["w_down"]))
        per_layer.append(_pack_shards(parts, (N_CHIPS,)))
    return jnp.stack(per_layer)


_SMALL_LAYOUT = (("ln1_g", 0), ("ln1_b", 2), ("ln2_g", 4), ("ln2_b", 6), ("conv_w", 8))
_ROW_MISC = 10
_ROW_LOSS = 11


def _pack_small(per_layer, rel_bias, loss=None):
    pk = jnp.zeros((SMALL_ROWS, D_MODEL), F32)
    for name, row in _SMALL_LAYOUT:
        for l in range(DEPTH):
            v = per_layer[l][name].reshape(-1)
            pk = pk.at[row + l, :v.shape[0]].set(v)
    fb = jnp.concatenate([per_layer[l]["f_bias"] for l in range(DEPTH)])
    pk = pk.at[_ROW_MISC, :2 * N_HEADS].set(fb)
    pk = pk.at[_ROW_MISC, BLK:BLK + REL_BUCKETS * N_HEADS].set(rel_bias.reshape(-1))
    if loss is not None:
        pk = pk.at[_ROW_LOSS, 0].set(loss)
    return pk


def _unpack_small(pk, conv_cols):
    out = {}
    for name, row in _SMALL_LAYOUT:
        n = 3 * conv_cols if name == "conv_w" else D_MODEL
        v = pk[row:row + DEPTH, :n]
        out[name] = v.reshape(DEPTH, 3, conv_cols) if name == "conv_w" else v
    out["f_bias"] = pk[_ROW_MISC, :2 * N_HEADS].reshape(DEPTH, N_HEADS)
    out["rel_bias"] = pk[_ROW_MISC, BLK:BLK + REL_BUCKETS * N_HEADS].reshape(REL_BUCKETS, N_HEADS)
    return out


_WEIGHTS = ("w_in", "f_bias", "conv_w", "w_out", "rel_bias", "ln1_g", "ln1_b", "w_gate", "w_up", "w_down", "ln2_g", "ln2_b")
_BIG = ("w_in", "w_out", "w_gate", "w_up", "w_down")


def kernel(x, w_in, f_bias, conv_w, w_out, rel_bias, ln1_g, ln1_b, w_gate, w_up, w_down, ln2_g, ln2_b, loss_target, m_w_in, m_f_bias, m_conv_w, m_w_out, m_rel_bias, m_ln1_g, m_ln1_b, m_w_gate, m_w_up, m_w_down, m_ln2_g, m_ln2_b, v_w_in, v_f_bias, v_conv_w, v_w_out, v_rel_bias, v_ln1_g, v_ln1_b, v_w_gate, v_w_up, v_w_down, v_ln2_g, v_ln2_b):
    w = dict(w_in=w_in, f_bias=f_bias, conv_w=conv_w, w_out=w_out, rel_bias=rel_bias, ln1_g=ln1_g, ln1_b=ln1_b,
             w_gate=w_gate, w_up=w_up, w_down=w_down, ln2_g=ln2_g, ln2_b=ln2_b)
    m = dict(w_in=m_w_in, f_bias=m_f_bias, conv_w=m_conv_w, w_out=m_w_out, rel_bias=m_rel_bias, ln1_g=m_ln1_g,
             ln1_b=m_ln1_b, w_gate=m_w_gate, w_up=m_w_up, w_down=m_w_down, ln2_g=m_ln2_g, ln2_b=m_ln2_b)
    v = dict(w_in=v_w_in, f_bias=v_f_bias, conv_w=v_conv_w, w_out=v_w_out, rel_bias=v_rel_bias, ln1_g=v_ln1_g,
             ln1_b=v_ln1_b, w_gate=v_w_gate, w_up=v_w_up, w_down=v_w_down, ln2_g=v_ln2_g, ln2_b=v_ln2_b)
    chip = 2 * lax.axis_index("x") + lax.axis_index("y")
    core = lax.axis_index("c")
    conv_shard = CONV_W // N_CHIPS

    slab = _pack_shards({name: w[name] for name in _BIG}, (DEPTH,)).astype(MXU_DTYPE)
    full = _full_weights(_gather_weights(slab))
    cw_pk = jnp.zeros((8, D_MODEL), F32).at[0, :DEPTH * 3 * conv_shard].set(conv_w.reshape(-1))
    cw_all, _ = _gather_small(cw_pk, "gather_conv_w")
    cw_chips = cw_all[0::2, 0, :DEPTH * 3 * conv_shard].reshape(N_CHIPS, DEPTH, 3, conv_shard)
    conv_full = jnp.moveaxis(cw_chips, 0, 2).reshape(DEPTH, 3, CONV_W)
    small_all = [dict(f_bias=f_bias[l], conv_w=conv_full[l], rel_bias=rel_bias, ln1_g=ln1_g[l], ln1_b=ln1_b[l],
                      ln2_g=ln2_g[l], ln2_b=ln2_b[l]) for l in range(DEPTH)]

    lossp, grad_x, grads = _local_step(x, loss_target, full, small_all)

    g = _shard_major(grads)
    pair = _pair_sum(g, _swap_layers(g), core)
    both = _share_layers(_chip_sum(_scatter_shards(pair)))
    big_g = _unpack_shards(both)

    drel = grads[0]["rel_bias"] + grads[1]["rel_bias"]
    small_pk = _pack_small(grads, drel, lossp[0, 0])
    _, small_sum = _gather_small(small_pk, "gather_small_grads")
    loss = small_sum[_ROW_LOSS, 0]
    small_g = _unpack_small(small_sum, CONV_W)
    small_g["conv_w"] = lax.dynamic_slice_in_dim(small_g["conv_w"], chip * conv_shard, conv_shard, axis=2)

    out_g, out_d, out_m, out_v = dict(small_g), {}, {}, {}
    for name in _BIG:
        out_g[name] = big_g[name]
        out_d[name], out_m[name], out_v[name] = _adamw(w[name], big_g[name], m[name], v[name], f"adamw_{name}")
    per_layer = lambda src: [{name: src[name][l] for name in ("ln1_g", "ln1_b", "ln2_g", "ln2_b", "conv_w", "f_bias")}
                             for l in range(DEPTH)]
    packs = [_pack_small(per_layer(src), src["rel_bias"])[None] for src in (w, small_g, m, v)]
    for dst, pk in zip((out_d, out_m, out_v), _adamw(*packs, "adamw_small")):
        dst.update(_unpack_small(pk[0], conv_shard))

    return (loss, grad_x, *[out_g[n] for n in _WEIGHTS], *[out_d[n] for n in _WEIGHTS],
            *[out_m[n] for n in _WEIGHTS], *[out_v[n] for n in _WEIGHTS])
```

```python
import functools
import math

import numpy as np
import jax
import jax.numpy as jnp
from jax import lax
from jax.experimental import pallas as pl
from jax.experimental.pallas import tpu as pltpu

F32 = jnp.float32
BF16 = jnp.bfloat16
MXU_DTYPE = BF16

D_MODEL = 1024
HEAD_DIM = 64
N_HEADS = 4
BLK = 128
ATT = 256
N_SLOTS = 36
W_QKV = N_SLOTS * HEAD_DIM
CONV_W = 256
PROJ = 3076
PROJ_PAD = 3200
D_FF = 2816
DEPTH = 2
ALPHA = (2 * DEPTH) ** 0.25
LN_EPS = 1e-5
NEG = -1e30
DIL_PATTERNS = ((128, 1), (512, 4), (2048, 16))
REL_BUCKETS = 32
N_CHIPS = 4
N_DEV = 8
PACK_ROWS = 3137
SMALL_ROWS = 16

ADAM_LR = 0.001
ADAM_B1 = 0.9
ADAM_B2 = 0.999
ADAM_EPS = 1e-08
ADAM_WD = 0.01
ADAM_STEP = 10

VMEM_LIMIT = 48 * 2 ** 20
MESH = pl.DeviceIdType.MESH


def _cparams(*sem):
    return pltpu.CompilerParams(dimension_semantics=tuple(sem), vmem_limit_bytes=VMEM_LIMIT)


def _dot(a, b):
    return jnp.dot(a.astype(MXU_DTYPE), b.astype(MXU_DTYPE), preferred_element_type=F32)


def _dot_nt(a, b):
    return lax.dot_general(a.astype(MXU_DTYPE), b.astype(MXU_DTYPE), (((1,), (1,)), ((), ())),
                           preferred_element_type=F32)


def _dot_tn(a, b):
    return lax.dot_general(a.astype(MXU_DTYPE), b.astype(MXU_DTYPE), (((0,), (0,)), ((), ())),
                           preferred_element_type=F32)


def _split_dot(x, ones, passes):
    acc, rest = None, x
    for p in range(passes):
        piece = rest.astype(MXU_DTYPE)
        part = jnp.dot(piece, ones, preferred_element_type=F32)
        acc = part if acc is None else acc + part
        if p + 1 < passes:
            rest = rest - piece.astype(F32)
    return acc


def _split_dot_lhs(ones, x, passes):
    acc, rest = None, x
    for p in range(passes):
        piece = rest.astype(MXU_DTYPE)
        part = jnp.dot(ones, piece, preferred_element_type=F32)
        acc = part if acc is None else acc + part
        if p + 1 < passes:
            rest = rest - piece.astype(F32)
    return acc


def _iota2(shape, axis):
    return lax.broadcasted_iota(jnp.int32, shape, axis)


def _matmul(a, b, *, tm, tn, tk, name, out_dtype=F32, add=None, add_scale=1.0):
    m, k = a.shape
    _, n = b.shape
    assert m % tm == 0 and n % tn == 0 and k % tk == 0, (a.shape, b.shape, tm, tn, tk)
    nk = k // tk

    def body(*refs):
        if add is None:
            a_ref, b_ref, o_ref = refs[:3]
            c_ref, scr = None, refs[3:]
        else:
            a_ref, b_ref, c_ref, o_ref = refs[:4]
            scr = refs[4:]
        part = _dot(a_ref[...], b_ref[...])

        def finish(acc):
            if c_ref is not None:
                acc = acc + add_scale * c_ref[...]
            o_ref[...] = acc.astype(out_dtype)

        if nk == 1:
            finish(part)
        else:
            acc_ref = scr[0]
            kk = pl.program_id(2)

            @pl.when(kk == 0)
            def _():
                acc_ref[...] = part

            @pl.when(kk > 0)
            def _():
                acc_ref[...] += part

            @pl.when(kk == nk - 1)
            def _():
                finish(acc_ref[...])

    in_specs = [pl.BlockSpec((tm, tk), lambda i, j, kk: (i, kk)), pl.BlockSpec((tk, tn), lambda i, j, kk: (kk, j))]
    operands = [a, b]
    if add is not None:
        in_specs.append(pl.BlockSpec((tm, tn), lambda i, j, kk: (i, j)))
        operands.append(add)
    return pl.pallas_call(
        body, name=name, grid=(m // tm, n // tn, nk), in_specs=in_specs,
        out_specs=pl.BlockSpec((tm, tn), lambda i, j, kk: (i, j)),
        out_shape=jax.ShapeDtypeStruct((m, n), out_dtype),
        scratch_shapes=[pltpu.VMEM((tm, tn), F32)] if nk > 1 else [],
        compiler_params=_cparams("parallel", "parallel", "arbitrary"),
    )(*operands)


def _ln_stats(pre):
    mu = jnp.mean(pre, axis=-1, keepdims=True)
    xc = pre - mu
    var = jnp.mean(xc * xc, axis=-1, keepdims=True)
    rstd = lax.rsqrt(var + LN_EPS)
    return xc * rstd, rstd


def _ln_fwd(xin, branch, g, b, name):
    t, d = xin.shape
    tile = 256

    def body(x_ref, br_ref, g_ref, b_ref, pre_ref, y_ref):
        pre = ALPHA * x_ref[...] + br_ref[...]
        xhat, _ = _ln_stats(pre)
        pre_ref[...] = pre
        y_ref[...] = xhat * g_ref[...] + b_ref[...]

    row = pl.BlockSpec((tile, d), lambda i: (i, 0))
    vec = pl.BlockSpec((1, d), lambda i: (0, 0))
    return pl.pallas_call(
        body, name=name, grid=(t // tile,), in_specs=[row, row, vec, vec], out_specs=[row, row],
        out_shape=[jax.ShapeDtypeStruct((t, d), F32)] * 2, compiler_params=_cparams("parallel"),
    )(xin, branch, g.reshape(1, d), b.reshape(1, d))


def _ln_bwd(dy, pre, g, name):
    t, d = dy.shape
    tile = 256

    def body(dy_ref, pre_ref, g_ref, dpre_ref, dgb_ref):
        dyv = dy_ref[...]
        xhat, rstd = _ln_stats(pre_ref[...])
        dxh = dyv * g_ref[...]
        m1 = jnp.mean(dxh, axis=-1, keepdims=True)
        m2 = jnp.mean(dxh * xhat, axis=-1, keepdims=True)
        dpre_ref[...] = rstd * (dxh - m1 - xhat * m2)

        @pl.when(pl.program_id(0) == 0)
        def _():
            dgb_ref[...] = jnp.zeros_like(dgb_ref)

        dgb_ref[0:1, :] += jnp.sum(dyv * xhat, axis=0, keepdims=True)
        dgb_ref[1:2, :] += jnp.sum(dyv, axis=0, keepdims=True)

    row = pl.BlockSpec((tile, d), lambda i: (i, 0))
    return pl.pallas_call(
        body, name=name, grid=(t // tile,), in_specs=[row, row, pl.BlockSpec((1, d), lambda i: (0, 0))],
        out_specs=[row, pl.BlockSpec((8, d), lambda i: (0, 0))],
        out_shape=[jax.ShapeDtypeStruct((t, d), F32), jax.ShapeDtypeStruct((8, d), F32)],
        compiler_params=_cparams("arbitrary"),
    )(dy, pre, g.reshape(1, d))


def _swiglu_fwd(gu, name):
    t = gu.shape[0]
    tile = 256

    def body(gu_ref, h_ref):
        gate = gu_ref[:, :D_FF]
        up = gu_ref[:, D_FF:]
        h_ref[...] = (gate * (1.0 / (1.0 + jnp.exp(-gate))) * up).astype(h_ref.dtype)

    return pl.pallas_call(
        body, name=name, grid=(t // tile,), in_specs=[pl.BlockSpec((tile, 2 * D_FF), lambda i: (i, 0))],
        out_specs=pl.BlockSpec((tile, D_FF), lambda i: (i, 0)),
        out_shape=jax.ShapeDtypeStruct((t, D_FF), MXU_DTYPE), compiler_params=_cparams("parallel"),
    )(gu)


def _swiglu_bwd(dh, gu, name):
    t = gu.shape[0]
    tile = 256

    def body(dh_ref, gu_ref, dgu_ref):
        gate = gu_ref[:, :D_FF]
        up = gu_ref[:, D_FF:]
        dhv = dh_ref[...]
        sig = 1.0 / (1.0 + jnp.exp(-gate))
        dgu_ref[:, :D_FF] = (dhv * up * sig * (1.0 + gate * (1.0 - sig))).astype(dgu_ref.dtype)
        dgu_ref[:, D_FF:] = (dhv * gate * sig).astype(dgu_ref.dtype)

    return pl.pallas_call(
        body, name=name, grid=(t // tile,),
        in_specs=[pl.BlockSpec((tile, D_FF), lambda i: (i, 0)), pl.BlockSpec((tile, 2 * D_FF), lambda i: (i, 0))],
        out_specs=pl.BlockSpec((tile, 2 * D_FF), lambda i: (i, 0)),
        out_shape=jax.ShapeDtypeStruct((t, 2 * D_FF), MXU_DTYPE), compiler_params=_cparams("parallel"),
    )(dh, gu)


def _loss_kernel(y, target, name):
    t, d = y.shape
    tile = 512

    def body(y_ref, t_ref, dy_ref, l_ref):
        err = y_ref[...] - t_ref[...]
        dy_ref[...] = err * (1.0 / d)

        @pl.when(pl.program_id(0) == 0)
        def _():
            l_ref[...] = jnp.zeros_like(l_ref)

        l_ref[...] += jnp.sum(err * err) * (0.5 / d)

    row = pl.BlockSpec((tile, d), lambda i: (i, 0))
    return pl.pallas_call(
        body, name=name, grid=(t // tile,), in_specs=[row, row],
        out_specs=[row, pl.BlockSpec((8, 128), lambda i: (0, 0))],
        out_shape=[jax.ShapeDtypeStruct((t, d), F32), jax.ShapeDtypeStruct((8, 128), F32)],
        compiler_params=_cparams("arbitrary"),
    )(y, target)


def _adamw(w, g, m, v, name):
    nl, r, c = w.shape
    tr = r
    for cand in (256, 352, 128, 64, 16, 8):
        if r % cand == 0:
            tr = cand
            break

    def body(w_ref, g_ref, m_ref, v_ref, d_ref, nm_ref, nv_ref):
        gv = g_ref[...]
        nm = ADAM_B1 * m_ref[...] + (1.0 - ADAM_B1) * gv
        nv = ADAM_B2 * v_ref[...] + (1.0 - ADAM_B2) * (gv * gv)
        m_hat = nm / (1.0 - ADAM_B1 ** ADAM_STEP)
        v_hat = nv / (1.0 - ADAM_B2 ** ADAM_STEP)
        d_ref[...] = -ADAM_LR * (m_hat / (jnp.sqrt(v_hat) + ADAM_EPS) + ADAM_WD * w_ref[...])
        nm_ref[...] = nm
        nv_ref[...] = nv

    blk = pl.BlockSpec((1, tr, c), lambda l, i: (l, i, 0))
    return pl.pallas_call(
        body, name=name, grid=(nl, r // tr), in_specs=[blk] * 4, out_specs=[blk] * 3,
        out_shape=[jax.ShapeDtypeStruct(w.shape, F32)] * 3, compiler_params=_cparams("parallel", "parallel"),
    )(w, g, m, v)


def _shift_down(u, k, rows):
    return jnp.where(rows >= k, pltpu.roll(u, k, 0), 0.0)


def _shift_up(u, k, rows, s):
    return jnp.where(rows < s - k, pltpu.roll(u, s - k, 0), 0.0)


def _conv_fwd(proj, conv_w, nb, s, name):
    def body(b_ref, c_ref, h_ref, w_ref, o_ref):
        rows = _iota2((s, CONV_W), 0)
        u = c_ref[...] * h_ref[...]
        y = w_ref[2:3, :] * u + w_ref[1:2, :] * _shift_down(u, 1, rows) + w_ref[0:1, :] * _shift_down(u, 2, rows)
        o_ref[...] = b_ref[...] * y

    col = lambda j: pl.BlockSpec((s, CONV_W), lambda b: (b, j))
    return pl.pallas_call(
        body, name=name, grid=(nb,),
        in_specs=[col(9), col(10), col(11), pl.BlockSpec((8, CONV_W), lambda b: (0, 0))],
        out_specs=pl.BlockSpec((s, CONV_W), lambda b: (b, 0)),
        out_shape=jax.ShapeDtypeStruct((nb * s, CONV_W), F32), compiler_params=_cparams("parallel"),
    )(proj, proj, proj, conv_w)


def _conv_bwd(dmixed, proj, conv_w, nb, s, name):
    def body(do_ref, b_ref, c_ref, h_ref, w_ref, dg_ref, dw_ref):
        rows = _iota2((s, CONV_W), 0)
        cg, hg, bg, dout = c_ref[...], h_ref[...], b_ref[...], do_ref[...]
        u = cg * hg
        u1 = _shift_down(u, 1, rows)
        u2 = _shift_down(u, 2, rows)
        y = w_ref[2:3, :] * u + w_ref[1:2, :] * u1 + w_ref[0:1, :] * u2
        dy = dout * bg
        du = w_ref[2:3, :] * dy + w_ref[1:2, :] * _shift_up(dy, 1, rows, s) + w_ref[0:1, :] * _shift_up(dy, 2, rows, s)
        dg_ref[:, 0:CONV_W] = dout * y
        dg_ref[:, CONV_W:2 * CONV_W] = du * hg
        dg_ref[:, 2 * CONV_W:3 * CONV_W] = du * cg

        @pl.when(pl.program_id(0) == 0)
        def _():
            dw_ref[...] = jnp.zeros_like(dw_ref)

        dw_ref[0:1, :] += jnp.sum(dy * u2, axis=0, keepdims=True)
        dw_ref[1:2, :] += jnp.sum(dy * u1, axis=0, keepdims=True)
        dw_ref[2:3, :] += jnp.sum(dy * u, axis=0, keepdims=True)

    col = lambda j: pl.BlockSpec((s, CONV_W), lambda b: (b, j))
    return pl.pallas_call(
        body, name=name, grid=(nb,),
        in_specs=[col(3), col(9), col(10), col(11), pl.BlockSpec((8, CONV_W), lambda b: (0, 0))],
        out_specs=[pl.BlockSpec((s, 3 * CONV_W), lambda b: (b, 0)), pl.BlockSpec((8, CONV_W), lambda b: (0, 0))],
        out_shape=[jax.ShapeDtypeStruct((nb * s, 3 * CONV_W), F32), jax.ShapeDtypeStruct((8, CONV_W), F32)],
        compiler_params=_cparams("arbitrary"),
    )(dmixed, proj, proj, proj, conv_w)


def _slot_spec(s, base, width=HEAD_DIM):
    return pl.BlockSpec((None, None, s, width), lambda b, h: (b, base + h, 0, 0))


def _rows(i):
    return pl.ds(pl.multiple_of(i * ATT, ATT), ATT)


def _rows128(i):
    return pl.ds(pl.multiple_of(i * BLK, BLK), BLK)


def _log_sigmoid_parts(z):
    e = jnp.exp(-jnp.abs(z))
    l1p = jnp.log(1.0 + e)
    lb = jnp.minimum(z, 0.0) - l1p
    return lb, lb - z, e


def _sb_fwd(qkv, name):
    nb, _, s, _ = qkv.shape
    nblk = s // ATT

    def body(q_ref, k_ref, v_ref, o_ref):
        rows = _iota2((ATT, ATT),0)
        cols = _iota2((ATT, ATT),1)
        later = (rows > cols).astype(MXU_DTYPE)

        def qblock(i, _):
            qi = q_ref[_rows(i), :] * 0.125

            def kblock(t, state):
                carry, acc = state
                j = i - t
                z = _dot_nt(qi, k_ref[_rows(j), :])
                strict = (cols + (j - i) * ATT) < rows
                lb, lr, _ = _log_sigmoid_parts(z)
                lr = jnp.where(strict, lr, 0.0)
                tail = _split_dot(lr, later, 2) + carry
                a = jnp.where(strict, jnp.exp(lb + tail), 0.0)
                acc = acc + _dot(a, v_ref[_rows(j), :])
                return carry + jnp.sum(lr, axis=-1, keepdims=True), acc

            _, acc = lax.fori_loop(0, i + 1, kblock, (jnp.zeros((ATT, 1), F32), jnp.zeros((ATT, HEAD_DIM), F32)))
            o_ref[_rows(i), :] = acc
            return 0

        lax.fori_loop(0, nblk, qblock, 0)

    return pl.pallas_call(
        body, name=name, grid=(nb, N_HEADS), in_specs=[_slot_spec(s, 0), _slot_spec(s, 4), _slot_spec(s, 8)],
        out_specs=pl.BlockSpec((None, None, s, HEAD_DIM), lambda b, h: (b, h, 0, 0)),
        out_shape=jax.ShapeDtypeStruct((nb, N_HEADS, s, HEAD_DIM), F32), compiler_params=_cparams("parallel", "parallel"),
    )(qkv, qkv, qkv)


def _sb_bwd(qkv, do, name):
    nb, _, s, _ = qkv.shape
    nblk = s // ATT

    def body(q_ref, k_ref, v_ref, do_ref, dq_ref, dk_ref, dv_ref, a_scr, dl_scr, beta_scr):
        rows = _iota2((ATT, ATT),0)
        cols = _iota2((ATT, ATT),1)
        later = (rows > cols).astype(MXU_DTYPE)
        earlier = (rows < cols).astype(MXU_DTYPE)
        dk_ref[...] = jnp.zeros_like(dk_ref)
        dv_ref[...] = jnp.zeros_like(dv_ref)

        def qblock(i, _):
            qi = q_ref[_rows(i), :] * 0.125
            doi = do_ref[_rows(i), :]

            def first(t, carry):
                j = i - t
                z = _dot_nt(qi, k_ref[_rows(j), :])
                strict = (cols + (j - i) * ATT) < rows
                lb, lr, e = _log_sigmoid_parts(z)
                lr = jnp.where(strict, lr, 0.0)
                tail = _split_dot(lr, later, 2) + carry
                a = jnp.where(strict, jnp.exp(lb + tail), 0.0)
                a_scr[j] = a
                dl_scr[j] = a * _dot_nt(doi, v_ref[_rows(j), :])
                beta_scr[j] = jnp.where(z >= 0.0, 1.0, e) / (1.0 + e)
                return carry + jnp.sum(lr, axis=-1, keepdims=True)

            lax.fori_loop(0, i + 1, first, jnp.zeros((ATT, 1), F32))

            def second(j, state):
                csum, dq = state
                dl = dl_scr[j]
                beta = beta_scr[j]
                strict = (cols + (j - i) * ATT) < rows
                before = _split_dot(dl, earlier, 2) + csum
                dz = jnp.where(strict, dl * (1.0 - beta) - beta * before, 0.0).astype(MXU_DTYPE)
                dq = dq + _dot(dz, k_ref[_rows(j), :])
                dk_ref[_rows(j), :] += _dot_tn(dz, qi)
                dv_ref[_rows(j), :] += _dot_tn(a_scr[j], doi)
                return csum + jnp.sum(dl, axis=-1, keepdims=True), dq

            _, dq = lax.fori_loop(0, i + 1, second, (jnp.zeros((ATT, 1), F32), jnp.zeros((ATT, HEAD_DIM), F32)))
            dq_ref[_rows(i), :] = dq * 0.125
            return 0

        lax.fori_loop(0, nblk, qblock, 0)

    out = pl.BlockSpec((None, None, s, HEAD_DIM), lambda b, h: (b, h, 0, 0))
    return pl.pallas_call(
        body, name=name, grid=(nb, N_HEADS),
        in_specs=[_slot_spec(s, 0), _slot_spec(s, 4), _slot_spec(s, 8), out], out_specs=[out] * 3,
        out_shape=[jax.ShapeDtypeStruct((nb, N_HEADS, s, HEAD_DIM), F32)] * 3,
        scratch_shapes=[pltpu.VMEM((nblk, ATT, ATT), F32)] * 3, compiler_params=_cparams("parallel", "parallel"),
    )(qkv, qkv, qkv, do)


def _head_spec(s, width):
    return pl.BlockSpec((None, None, s, width), lambda b, h: (b, h, 0, 0))


def _fox_fwd(qkv, ccol, crow, name):
    nb, _, s, _ = qkv.shape
    nblk = s // ATT

    def body(q_ref, k_ref, v_ref, cc_ref, cr_ref, o_ref, lse_ref):
        rows = _iota2((ATT, ATT),0)
        cols = _iota2((ATT, ATT),1)

        def qblock(i, _):
            qi = q_ref[_rows(i), :] * 0.125
            ci = cc_ref[_rows(i), :]

            def kblock(j, state):
                m, l, acc = state
                z = _dot_nt(qi, k_ref[_rows(j), :]) + (ci - cr_ref[j][0:1, :])
                z = jnp.where((cols + (j - i) * ATT) <= rows, z, NEG)
                m_new = jnp.maximum(m, jnp.max(z, axis=-1, keepdims=True))
                p = jnp.exp(z - m_new)
                scale = jnp.exp(m - m_new)
                l = scale * l + jnp.sum(p, axis=-1, keepdims=True)
                acc = scale * acc + _dot(p, v_ref[_rows(j), :])
                return m_new, l, acc

            init = (jnp.full((ATT, 1), NEG, F32), jnp.zeros((ATT, 1), F32), jnp.zeros((ATT, HEAD_DIM), F32))
            m, l, acc = lax.fori_loop(0, i + 1, kblock, init)
            o_ref[_rows(i), :] = acc / l
            lse_ref[_rows(i), :] = jnp.broadcast_to(m + jnp.log(l), (ATT, ATT))
            return 0

        lax.fori_loop(0, nblk, qblock, 0)

    crow_spec = pl.BlockSpec((None, None, nblk, 8, ATT), lambda b, h: (b, h, 0, 0, 0))
    return pl.pallas_call(
        body, name=name, grid=(nb, N_HEADS),
        in_specs=[_slot_spec(s, 24), _slot_spec(s, 28), _slot_spec(s, 32), _head_spec(s, ATT), crow_spec],
        out_specs=[_head_spec(s, HEAD_DIM), _head_spec(s, ATT)],
        out_shape=[jax.ShapeDtypeStruct((nb, N_HEADS, s, HEAD_DIM), F32), jax.ShapeDtypeStruct((nb, N_HEADS, s, ATT), F32)],
        compiler_params=_cparams("parallel", "parallel"),
    )(qkv, qkv, qkv, ccol, crow)


def _fox_bwd(qkv, do, lse, ccol, crow, name):
    nb, _, s, _ = qkv.shape
    nblk = s // ATT

    def body(q_ref, k_ref, v_ref, do_ref, lse_ref, cc_ref, cr_ref, dq_ref, dk_ref, dv_ref, dc_ref):
        rows = _iota2((ATT, ATT),0)
        cols = _iota2((ATT, ATT),1)
        dk_ref[...] = jnp.zeros_like(dk_ref)
        dv_ref[...] = jnp.zeros_like(dv_ref)
        dc_ref[...] = jnp.zeros_like(dc_ref)

        def qblock(i, _):
            qi = q_ref[_rows(i), :] * 0.125
            doi = do_ref[_rows(i), :]
            ci = cc_ref[_rows(i), :]
            lsei = lse_ref[_rows(i), :]

            def probs(j):
                z = _dot_nt(qi, k_ref[_rows(j), :]) + (ci - cr_ref[j][0:1, :])
                p = jnp.where((cols + (j - i) * ATT) <= rows, jnp.exp(z - lsei), 0.0)
                return p, _dot_nt(doi, v_ref[_rows(j), :])

            def row_term(j, acc):
                p, dp = probs(j)
                return acc + jnp.sum(p * dp, axis=-1, keepdims=True)

            di = lax.fori_loop(0, i + 1, row_term, jnp.zeros((ATT, 1), F32))

            def kblock(j, dq):
                kj = k_ref[_rows(j), :]
                p, dp = probs(j)
                ds = p * (dp - di)
                dc_ref[j] += jnp.broadcast_to(jnp.sum(ds, axis=0, keepdims=True), (8, ATT))
                ds = ds.astype(MXU_DTYPE)
                dk_ref[_rows(j), :] += _dot_tn(ds, qi)
                dv_ref[_rows(j), :] += _dot_tn(p, doi)
                return dq + _dot(ds, kj)

            dq = lax.fori_loop(0, i + 1, kblock, jnp.zeros((ATT, HEAD_DIM), F32))
            dq_ref[_rows(i), :] = dq * 0.125
            return 0

        lax.fori_loop(0, nblk, qblock, 0)

    crow_spec = pl.BlockSpec((None, None, nblk, 8, ATT), lambda b, h: (b, h, 0, 0, 0))
    wide, narrow = _head_spec(s, ATT), _head_spec(s, HEAD_DIM)
    return pl.pallas_call(
        body, name=name, grid=(nb, N_HEADS),
        in_specs=[_slot_spec(s, 24), _slot_spec(s, 28), _slot_spec(s, 32), narrow, wide, wide, crow_spec],
        out_specs=[narrow, narrow, narrow, crow_spec],
        out_shape=[jax.ShapeDtypeStruct((nb, N_HEADS, s, HEAD_DIM), F32)] * 3
        + [jax.ShapeDtypeStruct((nb, N_HEADS, nblk, 8, ATT), F32)],
        compiler_params=_cparams("parallel", "parallel"),
    )(qkv, qkv, qkv, do, lse, ccol, crow)


def _fox_gates_fwd(proj, f_bias, nb, s, name):
    chunk = 256

    def body(f_ref, b_ref, c_ref):
        lower = (_iota2((chunk, chunk), 0) >= _iota2((chunk, chunk), 1)).astype(MXU_DTYPE)
        carry = jnp.zeros((1, BLK), F32)
        for n in range(s // chunk):
            rows = pl.ds(n * chunk, chunk)
            lf, _, _ = _log_sigmoid_parts(f_ref[rows, :] + b_ref[0:1, :])
            c = _split_dot_lhs(lower, lf, 3) + carry
            c_ref[rows, :] = c
            carry = c[chunk - 1:chunk, :]

    return pl.pallas_call(
        body, name=name, grid=(nb,),
        in_specs=[pl.BlockSpec((s, BLK), lambda b: (b, (PROJ_PAD - BLK) // BLK)), pl.BlockSpec((8, BLK), lambda b: (0, 0))],
        out_specs=pl.BlockSpec((s, BLK), lambda b: (b, 0)),
        out_shape=jax.ShapeDtypeStruct((nb * s, BLK), F32), compiler_params=_cparams("parallel"),
    )(proj, f_bias)


def _fox_gates_bwd(dc, proj, f_bias, nb, s, name):
    chunk = 256

    def body(dc_ref, f_ref, b_ref, df_ref, db_ref):
        upper = (_iota2((chunk, chunk), 0) <= _iota2((chunk, chunk), 1)).astype(MXU_DTYPE)
        carry = jnp.zeros((1, BLK), F32)
        total = jnp.zeros((1, BLK), F32)
        for n in reversed(range(s // chunk)):
            rows = pl.ds(n * chunk, chunk)
            dlf = _split_dot_lhs(upper, dc_ref[rows, :], 3) + carry
            carry = dlf[0:1, :]
            pre = f_ref[rows, :] + b_ref[0:1, :]
            e = jnp.exp(-jnp.abs(pre))
            df = dlf * (jnp.where(pre >= 0.0, e, 1.0) / (1.0 + e))
            df_ref[rows, :] = df
            total = total + jnp.sum(df, axis=0, keepdims=True)

        @pl.when(pl.program_id(0) == 0)
        def _():
            db_ref[...] = jnp.zeros_like(db_ref)

        db_ref[0:1, :] += total

    return pl.pallas_call(
        body, name=name, grid=(nb,),
        in_specs=[pl.BlockSpec((s, BLK), lambda b: (b, 0)), pl.BlockSpec((s, BLK), lambda b: (b, (PROJ_PAD - BLK) // BLK)),
                  pl.BlockSpec((8, BLK), lambda b: (0, 0))],
        out_specs=[pl.BlockSpec((s, BLK), lambda b: (b, 0)), pl.BlockSpec((8, BLK), lambda b: (0, 0))],
        out_shape=[jax.ShapeDtypeStruct((nb * s, BLK), F32), jax.ShapeDtypeStruct((8, BLK), F32)],
        compiler_params=_cparams("arbitrary"),
    )(dc, proj, f_bias)


def _delta_kernel(do, o, name):
    nb, _, s, _ = do.shape

    def body(do_ref, o_ref, d_ref):
        d_ref[...] = jnp.broadcast_to(jnp.sum(do_ref[...] * o_ref[...], axis=-1, keepdims=True), (s, BLK))

    return pl.pallas_call(
        body, name=name, grid=(nb, N_HEADS), in_specs=[_head_spec(s, HEAD_DIM)] * 2, out_specs=_head_spec(s, BLK),
        out_shape=jax.ShapeDtypeStruct((nb, N_HEADS, s, BLK), F32), compiler_params=_cparams("parallel", "parallel"),
    )(do, o)


def _t5_bucket_np(dist):
    max_exact = REL_BUCKETS // 2
    nf = np.maximum(dist, 1).astype(np.float32)
    large = max_exact + (np.log(nf / max_exact) / math.log(2048 / max_exact) * (REL_BUCKETS - max_exact)).astype(np.int32)
    large = np.minimum(large, REL_BUCKETS - 1)
    return np.where(dist < max_exact, dist, large)


def _bucket_table():
    qi = np.arange(BLK)[:, None]
    kj = np.arange(2 * BLK)[None, :]
    dist = qi + BLK - kj
    tables = []
    for window, dil in DIL_PATTERNS:
        in_band = (dist >= 0) & (dist <= window // dil)
        tables.append(np.where(in_band, _t5_bucket_np(np.maximum(dist, 0) * dil), -1).astype(np.int32))
    return np.stack(tables)


def _to_residue(t, dil):
    if dil == 1:
        return t
    *lead, s, e = t.shape
    return jnp.swapaxes(t.reshape(*lead, s // dil, dil, e), -3, -2).reshape(*lead, s, e)


def _from_residue(t, dil):
    if dil == 1:
        return t
    *lead, s, e = t.shape
    return jnp.swapaxes(t.reshape(*lead, dil, s // dil, e), -3, -2).reshape(*lead, s, e)


def _pat_spec(s, width, base=0):
    return pl.BlockSpec((3, None, None, s, width), lambda b, h: (0, b, base + h, 0, 0))


def _dil_fwd(qkvp, bias, name):
    _, nb, _, s, _ = qkvp.shape
    nblk = s // BLK

    def body(q_ref, k_ref, v_ref, b_ref, o_ref, lse_ref):
        for p, (_, dil) in enumerate(DIL_PATTERNS):
            seg = s // dil // BLK

            def block(b, _, p=p, seg=seg):
                cur = _rows128(b)
                prev = _rows128(jnp.maximum(b - 1, 0))
                qb = q_ref[p, cur, :] * 0.125
                zp = _dot_nt(qb, k_ref[p, prev, :]) + b_ref[p, :, 0:BLK]
                zp = jnp.where(b % seg > 0, zp, NEG)
                zc = _dot_nt(qb, k_ref[p, cur, :]) + b_ref[p, :, BLK:2 * BLK]
                m = jnp.maximum(jnp.max(zp, axis=-1, keepdims=True), jnp.max(zc, axis=-1, keepdims=True))
                pp = jnp.exp(zp - m)
                pc = jnp.exp(zc - m)
                den = jnp.sum(pp, axis=-1, keepdims=True) + jnp.sum(pc, axis=-1, keepdims=True)
                o_ref[p, cur, :] = (_dot(pp, v_ref[p, prev, :]) + _dot(pc, v_ref[p, cur, :])) / den
                lse_ref[p, cur, :] = jnp.broadcast_to(m + jnp.log(den), (BLK, BLK))
                return 0

            lax.fori_loop(0, nblk, block, 0, unroll=2)

    bias_spec = pl.BlockSpec((3, None, BLK, 2 * BLK), lambda b, h: (0, h, 0, 0))
    return pl.pallas_call(
        body, name=name, grid=(nb, N_HEADS),
        in_specs=[_pat_spec(s, HEAD_DIM, 0), _pat_spec(s, HEAD_DIM, 4), _pat_spec(s, HEAD_DIM, 8), bias_spec],
        out_specs=[_pat_spec(s, HEAD_DIM), _pat_spec(s, BLK)],
        out_shape=[jax.ShapeDtypeStruct((3, nb, N_HEADS, s, HEAD_DIM), F32), jax.ShapeDtypeStruct((3, nb, N_HEADS, s, BLK), F32)],
        compiler_params=_cparams("parallel", "parallel"),
    )(qkvp, qkvp, qkvp, bias)


def _dil_combine(o, lse, name):
    _, nb, _, s, _ = o.shape

    def body(o_ref, l_ref, out_ref, lse_ref):
        m = jnp.maximum(jnp.maximum(l_ref[0], l_ref[1]), l_ref[2])
        w = [jnp.exp(l_ref[p] - m) for p in range(3)]
        den = w[0] + w[1] + w[2]
        num = w[0][:, :HEAD_DIM] * o_ref[0] + w[1][:, :HEAD_DIM] * o_ref[1] + w[2][:, :HEAD_DIM] * o_ref[2]
        out_ref[...] = num / den[:, :HEAD_DIM]
        lse_ref[...] = m + jnp.log(den)

    return pl.pallas_call(
        body, name=name, grid=(nb, N_HEADS), in_specs=[_pat_spec(s, HEAD_DIM), _pat_spec(s, BLK)],
        out_specs=[_head_spec(s, HEAD_DIM), _head_spec(s, BLK)],
        out_shape=[jax.ShapeDtypeStruct((nb, N_HEADS, s, HEAD_DIM), F32), jax.ShapeDtypeStruct((nb, N_HEADS, s, BLK), F32)],
        compiler_params=_cparams("parallel", "parallel"),
    )(o, lse)


def _dil_bwd(qkvp, dop, lsep, deltap, bias, name):
    _, nb, _, s, _ = qkvp.shape
    nblk = s // BLK

    def body(q_ref, k_ref, v_ref, do_ref, lse_ref, dl_ref, b_ref, dq_ref, dk_ref, dv_ref, g_ref):
        dk_ref[...] = jnp.zeros_like(dk_ref)
        dv_ref[...] = jnp.zeros_like(dv_ref)
        g_ref[...] = jnp.zeros_like(g_ref)
        for p, (_, dil) in enumerate(DIL_PATTERNS):
            seg = s // dil // BLK

            def block(b, _, p=p, seg=seg):
                cur = _rows128(b)
                prev = _rows128(jnp.maximum(b - 1, 0))
                qb = q_ref[p, cur, :] * 0.125
                dob = do_ref[p, cur, :]
                lse = lse_ref[p, cur, :]
                dlt = dl_ref[p, cur, :]
                kp, kc = k_ref[p, prev, :], k_ref[p, cur, :]
                zp = _dot_nt(qb, kp) + b_ref[p, :, 0:BLK]
                zp = jnp.where(b % seg > 0, zp, NEG)
                zc = _dot_nt(qb, kc) + b_ref[p, :, BLK:2 * BLK]
                pp = jnp.exp(zp - lse)
                pc = jnp.exp(zc - lse)
                dsp = pp * (_dot_nt(dob, v_ref[p, prev, :]) - dlt)
                dsc = pc * (_dot_nt(dob, v_ref[p, cur, :]) - dlt)
                g_ref[p, :, 0:BLK] += dsp
                g_ref[p, :, BLK:2 * BLK] += dsc
                dsp = dsp.astype(MXU_DTYPE)
                dsc = dsc.astype(MXU_DTYPE)
                dq_ref[p, cur, :] = (_dot(dsp, kp) + _dot(dsc, kc)) * 0.125
                dk_ref[p, prev, :] += _dot_tn(dsp, qb)
                dk_ref[p, cur, :] += _dot_tn(dsc, qb)
                dv_ref[p, prev, :] += _dot_tn(pp, dob)
                dv_ref[p, cur, :] += _dot_tn(pc, dob)
                return 0

            lax.fori_loop(0, nblk, block, 0, unroll=2)

    bias_spec = pl.BlockSpec((3, None, BLK, 2 * BLK), lambda b, h: (0, h, 0, 0))
    narrow, wide = _pat_spec(s, HEAD_DIM), _pat_spec(s, BLK)
    return pl.pallas_call(
        body, name=name, grid=(nb, N_HEADS),
        in_specs=[_pat_spec(s, HEAD_DIM, 0), _pat_spec(s, HEAD_DIM, 4), _pat_spec(s, HEAD_DIM, 8), narrow, wide, wide, bias_spec],
        out_specs=[narrow, narrow, narrow, pl.BlockSpec((None, None, 3, BLK, 2 * BLK), lambda b, h: (b, h, 0, 0, 0))],
        out_shape=[jax.ShapeDtypeStruct((3, nb, N_HEADS, s, HEAD_DIM), F32)] * 3
        + [jax.ShapeDtypeStruct((nb, N_HEADS, 3, BLK, 2 * BLK), F32)],
        compiler_params=_cparams("parallel", "parallel"),
    )(qkvp, qkvp, qkvp, dop, lsep, deltap, bias)


def _bucket_reduce(gbias, table, name):
    nb = gbias.shape[0]

    def body(g_ref, t_ref, o_ref):
        row = _iota2((8, BLK), 0)
        lane = _iota2((8, BLK), 1)
        gsum = [[sum(g_ref[b, h, p] for b in range(nb)) for p in range(3)] for h in range(N_HEADS)]

        def bucket(k, acc):
            for h in range(N_HEADS):
                tot = sum(jnp.sum(jnp.where(t_ref[p] == k, gsum[h][p], 0.0)) for p in range(3))
                acc = acc + jnp.where((row == h) & (lane == k), tot, 0.0)
            return acc

        o_ref[...] = lax.fori_loop(0, REL_BUCKETS, bucket, jnp.zeros((8, BLK), F32))

    vm = pl.BlockSpec(memory_space=pltpu.VMEM)
    return pl.pallas_call(
        body, name=name, in_specs=[vm, vm], out_specs=vm, out_shape=jax.ShapeDtypeStruct((8, BLK), F32),
        compiler_params=pltpu.CompilerParams(vmem_limit_bytes=VMEM_LIMIT),
    )(gbias, table)


def _place():
    x, y, c = lax.axis_index("x"), lax.axis_index("y"), lax.axis_index("c")
    others = [(1 - x, y), (x, 1 - y), (1 - x, 1 - y)]
    return x, y, c, others


def _remote(src, dst, send_sem, recv_sem, to):
    return pltpu.make_async_remote_copy(src_ref=src, dst_ref=dst, send_sem=send_sem, recv_sem=recv_sem,
                                        device_id=to, device_id_type=MESH)


def _gather_weights(wp):
    nl, r, w = wp.shape

    def body(wp_ref, out_ref, send_sems, recv_sems, local_sem):
        x, y, c, others = _place()
        me = 2 * x + y
        sibling = (x, y, 1 - c)
        mine = pltpu.make_async_copy(wp_ref, out_ref.at[me], local_sem)
        mine.start()
        sends = [_remote(wp_ref.at[c], out_ref.at[me, c], send_sems.at[k], recv_sems.at[k], (ox, oy, c))
                 for k, (ox, oy) in enumerate(others)]
        for cp in sends:
            cp.start()
        passed = []
        for k, (ox, oy) in enumerate(others):
            landed = out_ref.at[2 * ox + oy, c]
            _remote(landed, landed, send_sems.at[k], recv_sems.at[k], (ox, oy, c)).wait_recv()
            cp = _remote(landed, landed, send_sems.at[3 + k], recv_sems.at[3 + k], sibling)
            cp.start()
            passed.append(cp)
        for k, (ox, oy) in enumerate(others):
            theirs = out_ref.at[2 * ox + oy, 1 - c]
            _remote(theirs, theirs, send_sems.at[3 + k], recv_sems.at[3 + k], sibling).wait_recv()
        for cp in sends + passed:
            cp.wait_send()
        mine.wait()

    hbm = pl.BlockSpec(memory_space=pl.ANY)
    return pl.pallas_call(
        body, name="gather_weights", in_specs=[hbm], out_specs=hbm,
        out_shape=jax.ShapeDtypeStruct((N_CHIPS, nl, r, w), wp.dtype),
        scratch_shapes=[pltpu.SemaphoreType.DMA((6,)), pltpu.SemaphoreType.DMA((6,)), pltpu.SemaphoreType.DMA],
    )(wp)


def _swap_layers(g):
    _, ns, r, w = g.shape

    def body(g_ref, out_ref, send_sem, recv_sem):
        x, y, c, _ = _place()
        cp = _remote(g_ref.at[1 - c], out_ref, send_sem, recv_sem, (x, y, 1 - c))
        cp.start()
        cp.wait()

    hbm = pl.BlockSpec(memory_space=pl.ANY)
    return pl.pallas_call(
        body, name="swap_layers", in_specs=[hbm], out_specs=hbm, out_shape=jax.ShapeDtypeStruct((ns, r, w), g.dtype),
        scratch_shapes=[pltpu.SemaphoreType.DMA, pltpu.SemaphoreType.DMA],
    )(g)


def _pair_sum(g, other, core):
    _, ns, r, w = g.shape
    tw = 256

    def body(core_ref, g_ref, o_ref, out_ref):
        out_ref[...] = (g_ref[...] + o_ref[...]).astype(out_ref.dtype)

    grid_spec = pltpu.PrefetchScalarGridSpec(
        num_scalar_prefetch=1, grid=(ns, w // tw),
        in_specs=[pl.BlockSpec((None, None, r, tw), lambda k, j, core_ref: (core_ref[0], k, 0, j)),
                  pl.BlockSpec((None, r, tw), lambda k, j, core_ref: (k, 0, j))],
        out_specs=pl.BlockSpec((None, r, tw), lambda k, j, core_ref: (k, 0, j)))
    return pl.pallas_call(
        body, name="pair_sum", grid_spec=grid_spec, out_shape=jax.ShapeDtypeStruct((ns, r, w), MXU_DTYPE),
        compiler_params=_cparams("parallel", "parallel"),
    )(core.reshape(1).astype(jnp.int32), g, other)


def _scatter_shards(p):
    ns, r, w = p.shape

    def body(p_ref, q_ref, send_sems, recv_sems, local_sem):
        x, y, c, others = _place()
        me = 2 * x + y
        mine = pltpu.make_async_copy(p_ref.at[me], q_ref.at[me], local_sem)
        mine.start()
        sends = [_remote(p_ref.at[2 * ox + oy], q_ref.at[me], send_sems.at[k], recv_sems.at[k], (ox, oy, c))
                 for k, (ox, oy) in enumerate(others)]
        for cp in sends:
            cp.start()
        for k, (ox, oy) in enumerate(others):
            slot = q_ref.at[2 * ox + oy]
            _remote(slot, slot, send_sems.at[k], recv_sems.at[k], (ox, oy, c)).wait_recv()
        for cp in sends:
            cp.wait_send()
        mine.wait()

    hbm = pl.BlockSpec(memory_space=pl.ANY)
    return pl.pallas_call(
        body, name="scatter_shards", in_specs=[hbm], out_specs=hbm, out_shape=jax.ShapeDtypeStruct(p.shape, p.dtype),
        scratch_shapes=[pltpu.SemaphoreType.DMA((3,)), pltpu.SemaphoreType.DMA((3,)), pltpu.SemaphoreType.DMA],
    )(p)


def _chip_sum(q):
    ns, r, w = q.shape

    def body(q_ref, out_ref):
        out_ref[...] = ((q_ref[0].astype(F32) + q_ref[1].astype(F32)) + q_ref[2].astype(F32)) + q_ref[3].astype(F32)

    tw = 128
    return pl.pallas_call(
        body, name="chip_sum", grid=(w // tw,), in_specs=[pl.BlockSpec((ns, r, tw), lambda j: (0, 0, j))],
        out_specs=pl.BlockSpec((r, tw), lambda j: (0, j)), out_shape=jax.ShapeDtypeStruct((r, w), F32),
        compiler_params=_cparams("parallel"),
    )(q)


def _share_layers(gl):
    r, w = gl.shape

    def body(gl_ref, out_ref, send_sem, recv_sem, local_sem):
        x, y, c, _ = _place()
        mine = pltpu.make_async_copy(gl_ref, out_ref.at[c], local_sem)
        mine.start()
        cp = _remote(gl_ref, out_ref.at[c], send_sem, recv_sem, (x, y, 1 - c))
        cp.start()
        cp.wait_send()
        theirs = out_ref.at[1 - c]
        _remote(theirs, theirs, send_sem, recv_sem, (x, y, 1 - c)).wait_recv()
        mine.wait()

    hbm = pl.BlockSpec(memory_space=pl.ANY)
    return pl.pallas_call(
        body, name="share_layers", in_specs=[hbm], out_specs=hbm, out_shape=jax.ShapeDtypeStruct((2, r, w), gl.dtype),
        scratch_shapes=[pltpu.SemaphoreType.DMA, pltpu.SemaphoreType.DMA, pltpu.SemaphoreType.DMA],
    )(gl)


def _gather_small(pk, name):
    rows, w = pk.shape

    def body(pk_ref, all_ref, sum_ref, send_sems, recv_sems):
        x, y, c, _ = _place()
        me = 4 * x + 2 * y + c
        all_ref[me] = pk_ref[...]
        flips = [(fx, fy, fc) for fx in (0, 1) for fy in (0, 1) for fc in (0, 1)][1:]
        peers = [(x ^ fx, y ^ fy, c ^ fc) for fx, fy, fc in flips]
        sends = [_remote(pk_ref, all_ref.at[me], send_sems.at[k], recv_sems.at[k], peer) for k, peer in enumerate(peers)]
        for cp in sends:
            cp.start()
        for k, (px, py, pc) in enumerate(peers):
            slot = all_ref.at[4 * px + 2 * py + pc]
            _remote(slot, slot, send_sems.at[k], recv_sems.at[k], (px, py, pc)).wait_recv()
        for cp in sends:
            cp.wait_send()
        total = all_ref[0]
        for d in range(1, N_DEV):
            total = total + all_ref[d]
        sum_ref[...] = total

    vm = pl.BlockSpec(memory_space=pltpu.VMEM)
    return pl.pallas_call(
        body, name=name, in_specs=[vm], out_specs=[vm, vm],
        out_shape=[jax.ShapeDtypeStruct((N_DEV, rows, w), F32), jax.ShapeDtypeStruct((rows, w), F32)],
        scratch_shapes=[pltpu.SemaphoreType.DMA((7,)), pltpu.SemaphoreType.DMA((7,))],
    )(pk)


def _heads_to_cols(t):
    nb, h, s, e = t.shape
    return jnp.swapaxes(t, 1, 2).reshape(nb * s, h * e)


def _cols_to_heads(t, nb, s):
    h = t.shape[1] // HEAD_DIM
    return jnp.swapaxes(t.reshape(nb, s, h, HEAD_DIM), 1, 2)


def _row_layout(c, nb, s):
    ch = jnp.swapaxes(c[:, :N_HEADS].reshape(nb, s, N_HEADS), 1, 2)
    ccol = jnp.broadcast_to(ch[..., None], (nb, N_HEADS, s, ATT))
    crow = jnp.broadcast_to(ch.reshape(nb, N_HEADS, s // ATT, 1, ATT), (nb, N_HEADS, s // ATT, 8, ATT))
    return ccol, crow


def _dil_bias(rel_bias, name):
    def body(rel_ref, t_ref, o_ref):
        for p in range(len(DIL_PATTERNS)):
            table = t_ref[p]

            def bucket(k, accs, table=table):
                return tuple(jnp.where(table == k, rel_ref[k, h], acc) for h, acc in enumerate(accs))

            accs = lax.fori_loop(0, REL_BUCKETS, bucket, tuple(jnp.full((BLK, 2 * BLK), NEG, F32) for _ in range(N_HEADS)))
            for h in range(N_HEADS):
                o_ref[p, h] = accs[h]

    vm = pl.BlockSpec(memory_space=pltpu.VMEM)
    return pl.pallas_call(
        body, name=name, in_specs=[pl.BlockSpec(memory_space=pltpu.SMEM), vm], out_specs=vm,
        out_shape=jax.ShapeDtypeStruct((len(DIL_PATTERNS), N_HEADS, BLK, 2 * BLK), F32),
        compiler_params=pltpu.CompilerParams(vmem_limit_bytes=VMEM_LIMIT),
    )(rel_bias, jnp.asarray(_bucket_table()))


def _layer_forward(x, wts, small, nb, s, tag):
    proj = _matmul(x, wts["w_in"], tm=512, tn=640, tk=D_MODEL, name=f"proj_{tag}")
    qkv = jnp.swapaxes(proj[:, :W_QKV].reshape(nb, s, N_SLOTS, HEAD_DIM), 1, 2).astype(MXU_DTYPE)

    o_sb = _sb_fwd(qkv, f"sb_fwd_{tag}")

    dl = qkv[:, 12:24]
    qkvp = jnp.stack([_to_residue(dl, dil) for _, dil in DIL_PATTERNS])
    bias = _dil_bias(small["rel_bias"], f"dil_bias_{tag}")
    o_p, lse_p = _dil_fwd(qkvp, bias, f"dil_fwd_{tag}")
    o_nat = jnp.stack([_from_residue(o_p[p], dil) for p, (_, dil) in enumerate(DIL_PATTERNS)])
    lse_nat = jnp.stack([_from_residue(lse_p[p], dil) for p, (_, dil) in enumerate(DIL_PATTERNS)])
    o_dl, lse_dl = _dil_combine(o_nat, lse_nat, f"dil_mix_{tag}")

    fb = jnp.zeros((8, BLK), F32).at[0, :N_HEADS].set(small["f_bias"])
    csum = _fox_gates_fwd(proj, fb, nb, s, f"fox_gates_{tag}")
    ccol, crow = _row_layout(csum, nb, s)
    o_fx, lse_fx = _fox_fwd(qkv, ccol, crow, f"fox_fwd_{tag}")

    cw = jnp.zeros((8, CONV_W), F32).at[:3].set(small["conv_w"])
    o_cv = _conv_fwd(proj, cw, nb, s, f"conv_fwd_{tag}")

    mixed = jnp.concatenate([_heads_to_cols(o_sb), _heads_to_cols(o_dl), _heads_to_cols(o_fx), o_cv], axis=-1).astype(MXU_DTYPE)
    mix = _matmul(mixed, wts["w_out"], tm=512, tn=1024, tk=D_MODEL, name=f"out_proj_{tag}")
    pre1, x1 = _ln_fwd(x, mix, small["ln1_g"], small["ln1_b"], f"ln1_fwd_{tag}")
    gu = _matmul(x1, wts["w_gu"], tm=512, tn=512, tk=D_MODEL, name=f"ffn_in_{tag}")
    hid = _swiglu_fwd(gu, f"swiglu_fwd_{tag}")
    ffn = _matmul(hid, wts["w_down"], tm=512, tn=1024, tk=D_FF, name=f"ffn_out_{tag}")
    pre2, x2 = _ln_fwd(x1, ffn, small["ln2_g"], small["ln2_b"], f"ln2_fwd_{tag}")
    saved = dict(x=x, proj=proj, qkv=qkv, qkvp=qkvp, bias=bias, o_dl=o_dl, lse_dl=lse_dl, fb=fb, ccol=ccol, crow=crow,
                 o_fx=o_fx, lse_fx=lse_fx, cw=cw, mixed=mixed, pre1=pre1, x1=x1, gu=gu, hid=hid, pre2=pre2)
    return x2, saved


def _layer_backward(dx2, sv, wts, small, nb, s, tag):
    t = nb * s
    dpre2, dgb2 = _ln_bwd(dx2, sv["pre2"], small["ln2_g"], f"ln2_bwd_{tag}")
    dpre2_b = dpre2.astype(MXU_DTYPE)
    dhid = _matmul(dpre2_b, wts["w_down_t"], tm=512, tn=1408, tk=D_MODEL, name=f"ffn_out_dx_{tag}")
    dw_down = _matmul(sv["hid"].T, dpre2_b, tm=704, tn=1024, tk=1024, name=f"ffn_out_dw_{tag}")
    dgu = _swiglu_bwd(dhid, sv["gu"], f"swiglu_bwd_{tag}")
    dx1 = _matmul(dgu, wts["w_gu_t"], tm=512, tn=1024, tk=1408, name=f"ffn_in_dx_{tag}", add=dpre2, add_scale=ALPHA)
    dw_gu = _matmul(sv["x1"].astype(MXU_DTYPE).T, dgu, tm=512, tn=512, tk=1024, name=f"ffn_in_dw_{tag}")

    dpre1, dgb1 = _ln_bwd(dx1, sv["pre1"], small["ln1_g"], f"ln1_bwd_{tag}")
    dpre1_b = dpre1.astype(MXU_DTYPE)
    dmixed = _matmul(dpre1_b, wts["w_out_t"], tm=512, tn=1024, tk=D_MODEL, name=f"out_proj_dx_{tag}")
    dw_out = _matmul(sv["mixed"].T, dpre1_b, tm=512, tn=1024, tk=1024, name=f"out_proj_dw_{tag}")

    do_heads = _cols_to_heads(dmixed[:, :3 * CONV_W], nb, s)
    do_sb, do_dl, do_fx = do_heads[:, 0:4], do_heads[:, 4:8], do_heads[:, 8:12]
    qkv = sv["qkv"]

    dq_sb, dk_sb, dv_sb = _sb_bwd(qkv, do_sb.astype(MXU_DTYPE), f"sb_bwd_{tag}")

    delta_dl = _delta_kernel(do_dl, sv["o_dl"], f"dil_delta_{tag}")
    do_dl_b = do_dl.astype(MXU_DTYPE)
    dop = jnp.stack([_to_residue(do_dl_b, dil) for _, dil in DIL_PATTERNS])
    lsep = jnp.stack([_to_residue(sv["lse_dl"], dil) for _, dil in DIL_PATTERNS])
    deltap = jnp.stack([_to_residue(delta_dl, dil) for _, dil in DIL_PATTERNS])
    dqp, dkp, dvp, gbias = _dil_bwd(sv["qkvp"], dop, lsep, deltap, sv["bias"], f"dil_bwd_{tag}")
    unperm = lambda tp: sum(_from_residue(tp[p], dil) for p, (_, dil) in enumerate(DIL_PATTERNS))
    dq_dl, dk_dl, dv_dl = unperm(dqp), unperm(dkp), unperm(dvp)
    drel = _bucket_reduce(gbias, jnp.asarray(_bucket_table()), f"rel_bias_grad_{tag}")

    dq_fx, dk_fx, dv_fx, dcol = _fox_bwd(qkv, do_fx.astype(MXU_DTYPE), sv["lse_fx"], sv["ccol"], sv["crow"], f"fox_bwd_{tag}")
    dcs = -jnp.swapaxes(dcol[:, :, :, 0, :].reshape(nb, N_HEADS, s), 1, 2).reshape(t, N_HEADS)
    dcs = jnp.pad(dcs, ((0, 0), (0, BLK - N_HEADS)))
    dfx, dfb = _fox_gates_bwd(dcs, sv["proj"], sv["fb"], nb, s, f"fox_gates_bwd_{tag}")

    dgates, dcw = _conv_bwd(dmixed, sv["proj"], sv["cw"], nb, s, f"conv_bwd_{tag}")

    dslots = jnp.concatenate([dq_sb, dk_sb, dv_sb, dq_dl, dk_dl, dv_dl, dq_fx, dk_fx, dv_fx], axis=1)
    dproj = jnp.concatenate([_heads_to_cols(dslots), dgates, dfx], axis=-1).astype(MXU_DTYPE)
    dx = _matmul(dproj, wts["w_in_t"], tm=512, tn=1024, tk=640, name=f"proj_dx_{tag}", add=dpre1, add_scale=ALPHA)
    dw_in = _matmul(sv["x"].astype(MXU_DTYPE).T, dproj, tm=512, tn=640, tk=1024, name=f"proj_dw_{tag}")

    grads = dict(w_in=dw_in[:, :PROJ], w_out=dw_out, w_gate=dw_gu[:, :D_FF], w_up=dw_gu[:, D_FF:], w_down=dw_down,
                 ln1_g=dgb1[0], ln1_b=dgb1[1], ln2_g=dgb2[0], ln2_b=dgb2[1], conv_w=dcw[:3], f_bias=dfb[0, :N_HEADS],
                 rel_bias=drel[:N_HEADS, :REL_BUCKETS].T)
    return dx, grads


def _local_step(x, target, full, small_all):
    nb, s, d = x.shape
    h = x.reshape(nb * s, d)
    saved = []
    for layer in range(DEPTH):
        h, sv = _layer_forward(h, full[layer], small_all[layer], nb, s, f"l{layer}")
        saved.append(sv)
    dy, lossp = _loss_kernel(h, target.reshape(nb * s, d), "loss")
    grads = [None] * DEPTH
    for layer in reversed(range(DEPTH)):
        dy, grads[layer] = _layer_backward(dy, saved[layer], full[layer], small_all[layer], nb, s, f"l{layer}")
    return lossp, dy.reshape(nb, s, d), grads


_SHARD_SHAPES = (("w_in", (D_MODEL, PROJ // N_CHIPS)), ("w_out", (D_MODEL // N_CHIPS, D_MODEL)),
                 ("w_gate", (D_MODEL, D_FF // N_CHIPS)), ("w_up", (D_MODEL, D_FF // N_CHIPS)),
                 ("w_down", (D_FF // N_CHIPS, D_MODEL)))


def _pack_shards(parts, lead):
    flat = [parts[name].reshape(*lead, -1, D_MODEL) for name, _ in _SHARD_SHAPES]
    return jnp.concatenate(flat, axis=-2)


def _unpack_shards(slab):
    lead = slab.shape[:-2]
    out, row = {}, 0
    for name, (r, c) in _SHARD_SHAPES:
        n = r * c // D_MODEL
        out[name] = slab[..., row:row + n, :].reshape(*lead, r, c)
        row += n
    assert row == PACK_ROWS
    return out


def _full_weights(gathered):
    sh = _unpack_shards(gathered)
    cols = lambda t: jnp.moveaxis(t, 0, 2).reshape(DEPTH, t.shape[2], -1)
    rows = lambda t: jnp.moveaxis(t, 0, 1).reshape(DEPTH, -1, t.shape[3])
    w_in = jnp.pad(cols(sh["w_in"]), ((0, 0), (0, 0), (0, PROJ_PAD - PROJ)))
    w_gu = jnp.concatenate([cols(sh["w_gate"]), cols(sh["w_up"])], axis=-1)
    w_out, w_down = rows(sh["w_out"]), rows(sh["w_down"])
    return [dict(w_in=w_in[l], w_in_t=w_in[l].T, w_out=w_out[l], w_out_t=w_out[l].T, w_gu=w_gu[l], w_gu_t=w_gu[l].T,
                 w_down=w_down[l], w_down_t=w_down[l].T) for l in range(DEPTH)]


def _shard_major(grads):
    by_cols = lambda g: jnp.moveaxis(g.reshape(g.shape[0], N_CHIPS, -1), 1, 0)
    by_rows = lambda g: g.reshape(N_CHIPS, -1, g.shape[1])
    per_layer = []
    for g in grads:
        parts = dict(w_in=by_cols(g["w_in"]), w_out=by_rows(g["w_out"]), w_gate=by_cols(g["w_gate"]),
                     w_up=by_cols(g["w_up"]), w_down=by_rows(g["w_down"]))
        per_layer.append(_pack_shards(parts, (N_CHIPS,)))
    return jnp.stack(per_layer)


_SMALL_LAYOUT = (("ln1_g", 0), ("ln1_b", 2), ("ln2_g", 4), ("ln2_b", 6), ("conv_w", 8))
_ROW_MISC = 10
_ROW_LOSS = 11


def _pack_small(per_layer, rel_bias, loss=None):
    pk = jnp.zeros((SMALL_ROWS, D_MODEL), F32)
    for name, row in _SMALL_LAYOUT:
        for l in range(DEPTH):
            v = per_layer[l][name].reshape(-1)
            pk = pk.at[row + l, :v.shape[0]].set(v)
    fb = jnp.concatenate([per_layer[l]["f_bias"] for l in range(DEPTH)])
    pk = pk.at[_ROW_MISC, :2 * N_HEADS].set(fb)
    pk = pk.at[_ROW_MISC, BLK:BLK + REL_BUCKETS * N_HEADS].set(rel_bias.reshape(-1))
    if loss is not None:
        pk = pk.at[_ROW_LOSS, 0].set(loss)
    return pk


def _unpack_small(pk, conv_cols):
    out = {}
    for name, row in _SMALL_LAYOUT:
        n = 3 * conv_cols if name == "conv_w" else D_MODEL
        v = pk[row:row + DEPTH, :n]
        out[name] = v.reshape(DEPTH, 3, conv_cols) if name == "conv_w" else v
    out["f_bias"] = pk[_ROW_MISC, :2 * N_HEADS].reshape(DEPTH, N_HEADS)
    out["rel_bias"] = pk[_ROW_MISC, BLK:BLK + REL_BUCKETS * N_HEADS].reshape(REL_BUCKETS, N_HEADS)
    return out


_WEIGHTS = ("w_in", "f_bias", "conv_w", "w_out", "rel_bias", "ln1_g", "ln1_b", "w_gate", "w_up", "w_down", "ln2_g", "ln2_b")
_BIG = ("w_in", "w_out", "w_gate", "w_up", "w_down")


def kernel(x, w_in, f_bias, conv_w, w_out, rel_bias, ln1_g, ln1_b, w_gate, w_up, w_down, ln2_g, ln2_b, loss_target, m_w_in, m_f_bias, m_conv_w, m_w_out, m_rel_bias, m_ln1_g, m_ln1_b, m_w_gate, m_w_up, m_w_down, m_ln2_g, m_ln2_b, v_w_in, v_f_bias, v_conv_w, v_w_out, v_rel_bias, v_ln1_g, v_ln1_b, v_w_gate, v_w_up, v_w_down, v_ln2_g, v_ln2_b):
    w = dict(w_in=w_in, f_bias=f_bias, conv_w=conv_w, w_out=w_out, rel_bias=rel_bias, ln1_g=ln1_g, ln1_b=ln1_b,
             w_gate=w_gate, w_up=w_up, w_down=w_down, ln2_g=ln2_g, ln2_b=ln2_b)
    m = dict(w_in=m_w_in, f_bias=m_f_bias, conv_w=m_conv_w, w_out=m_w_out, rel_bias=m_rel_bias, ln1_g=m_ln1_g,
             ln1_b=m_ln1_b, w_gate=m_w_gate, w_up=m_w_up, w_down=m_w_down, ln2_g=m_ln2_g, ln2_b=m_ln2_b)
    v = dict(w_in=v_w_in, f_bias=v_f_bias, conv_w=v_conv_w, w_out=v_w_out, rel_bias=v_rel_bias, ln1_g=v_ln1_g,
             ln1_b=v_ln1_b, w_gate=v_w_gate, w_up=v_w_up, w_down=v_w_down, ln2_g=v_ln2_g, ln2_b=v_ln2_b)
    chip = 2 * lax.axis_index("x") + lax.axis_index("y")
    core = lax.axis_index("c")
    conv_shard = CONV_W // N_CHIPS

    slab = _pack_shards({name: w[name] for name in _BIG}, (DEPTH,)).astype(MXU_DTYPE)
    full = _full_weights(_gather_weights(slab))
    cw_pk = jnp.zeros((8, D_MODEL), F32).at[0, :DEPTH * 3 * conv_shard].set(conv_w.reshape(-1))
    cw_all, _ = _gather_small(cw_pk, "gather_conv_w")
    cw_chips = cw_all[0::2, 0, :DEPTH * 3 * conv_shard].reshape(N_CHIPS, DEPTH, 3, conv_shard)
    conv_full = jnp.moveaxis(cw_chips, 0, 2).reshape(DEPTH, 3, CONV_W)
    small_all = [dict(f_bias=f_bias[l], conv_w=conv_full[l], rel_bias=rel_bias, ln1_g=ln1_g[l], ln1_b=ln1_b[l],
                      ln2_g=ln2_g[l], ln2_b=ln2_b[l]) for l in range(DEPTH)]

    lossp, grad_x, grads = _local_step(x, loss_target, full, small_all)

    g = _shard_major(grads)
    pair = _pair_sum(g, _swap_layers(g), core)
    both = _share_layers(_chip_sum(_scatter_shards(pair)))
    big_g = _unpack_shards(both)

    drel = grads[0]["rel_bias"] + grads[1]["rel_bias"]
    small_pk = _pack_small(grads, drel, lossp[0, 0])
    _, small_sum = _gather_small(small_pk, "gather_small_grads")
    loss = small_sum[_ROW_LOSS, 0]
    small_g = _unpack_small(small_sum, CONV_W)
    small_g["conv_w"] = lax.dynamic_slice_in_dim(small_g["conv_w"], chip * conv_shard, conv_shard, axis=2)

    out_g, out_d, out_m, out_v = dict(small_g), {}, {}, {}
    for name in _BIG:
        out_g[name] = big_g[name]
        out_d[name], out_m[name], out_v[name] = _adamw(w[name], big_g[name], m[name], v[name], f"adamw_{name}")
    per_layer = lambda src: [{name: src[name][l] for name in ("ln1_g", "ln1_b", "ln2_g", "ln2_b", "conv_w", "f_bias")}
                             for l in range(DEPTH)]
    packs = [_pack_small(per_layer(src), src["rel_bias"])[None] for src in (w, small_g, m, v)]
    for dst, pk in zip((out_d, out_m, out_v), _adamw(*packs, "adamw_small")):
        dst.update(_unpack_small(pk[0], conv_shard))

    return (loss, grad_x, *[out_g[n] for n in _WEIGHTS], *[out_d[n] for n in _WEIGHTS],
            *[out_m[n] for n in _WEIGHTS], *[out_v[n] for n in _WEIGHTS])
```

```python
import functools
import math

import numpy as np
import jax
import jax.numpy as jnp
from jax import lax
from jax.experimental import pallas as pl
from jax.experimental.pallas import tpu as pltpu

F32 = jnp.float32
BF16 = jnp.bfloat16
MXU_DTYPE = BF16

D_MODEL = 1024
HEAD_DIM = 64
N_HEADS = 4
BLK = 128
ATT = 256
CONV_W = 256
PROJ = 3076
PROJ_PAD = 3200
D_FF = 2816
DEPTH = 2
ALPHA = (2 * DEPTH) ** 0.25
LN_EPS = 1e-5
NEG = -1e30
DIL_PATTERNS = ((128, 1), (512, 4), (2048, 16))
REL_BUCKETS = 32
N_CHIPS = 4
N_DEV = 8
PACK_ROWS = 3137
SMALL_ROWS = 16

ADAM_LR = 0.001
ADAM_B1 = 0.9
ADAM_B2 = 0.999
ADAM_EPS = 1e-08
ADAM_WD = 0.01
ADAM_STEP = 10

VMEM_LIMIT = 48 * 2 ** 20
MESH = pl.DeviceIdType.MESH


def _cparams(*sem):
    return pltpu.CompilerParams(dimension_semantics=tuple(sem), vmem_limit_bytes=VMEM_LIMIT)


def _dot(a, b):
    return jnp.dot(a.astype(MXU_DTYPE), b.astype(MXU_DTYPE), preferred_element_type=F32)


def _dot_nt(a, b):
    return lax.dot_general(a.astype(MXU_DTYPE), b.astype(MXU_DTYPE), (((1,), (1,)), ((), ())),
                           preferred_element_type=F32)


def _dot_tn(a, b):
    return lax.dot_general(a.astype(MXU_DTYPE), b.astype(MXU_DTYPE), (((0,), (0,)), ((), ())),
                           preferred_element_type=F32)


def _split_dot(x, ones, passes):
    acc, rest = None, x
    for p in range(passes):
        piece = rest.astype(MXU_DTYPE)
        part = jnp.dot(piece, ones, preferred_element_type=F32)
        acc = part if acc is None else acc + part
        if p + 1 < passes:
            rest = rest - piece.astype(F32)
    return acc


def _split_dot_lhs(ones, x, passes):
    acc, rest = None, x
    for p in range(passes):
        piece = rest.astype(MXU_DTYPE)
        part = jnp.dot(ones, piece, preferred_element_type=F32)
        acc = part if acc is None else acc + part
        if p + 1 < passes:
            rest = rest - piece.astype(F32)
    return acc


def _iota2(shape, axis):
    return lax.broadcasted_iota(jnp.int32, shape, axis)


def _matmul(a, b, *, tm, tn, tk, name, out_dtype=F32, add=None, add_scale=1.0, trans_b=False):
    m, k = a.shape
    n = b.shape[0] if trans_b else b.shape[1]
    assert m % tm == 0 and n % tn == 0 and k % tk == 0, (a.shape, b.shape, tm, tn, tk)
    nk = k // tk

    def body(*refs):
        if add is None:
            a_ref, b_ref, o_ref = refs[:3]
            c_ref, scr = None, refs[3:]
        else:
            a_ref, b_ref, c_ref, o_ref = refs[:4]
            scr = refs[4:]
        part = _dot_nt(a_ref[...], b_ref[...]) if trans_b else _dot(a_ref[...], b_ref[...])

        def finish(acc):
            if c_ref is not None:
                acc = acc + add_scale * c_ref[...]
            o_ref[...] = acc.astype(out_dtype)

        if nk == 1:
            finish(part)
        else:
            acc_ref = scr[0]
            kk = pl.program_id(2)

            @pl.when(kk == 0)
            def _():
                acc_ref[...] = part

            @pl.when(kk > 0)
            def _():
                acc_ref[...] += part

            @pl.when(kk == nk - 1)
            def _():
                finish(acc_ref[...])

    b_spec = pl.BlockSpec((tn, tk), lambda i, j, kk: (j, kk)) if trans_b else pl.BlockSpec((tk, tn), lambda i, j, kk: (kk, j))
    in_specs = [pl.BlockSpec((tm, tk), lambda i, j, kk: (i, kk)), b_spec]
    operands = [a, b]
    if add is not None:
        in_specs.append(pl.BlockSpec((tm, tn), lambda i, j, kk: (i, j)))
        operands.append(add)
    return pl.pallas_call(
        body, name=name, grid=(m // tm, n // tn, nk), in_specs=in_specs,
        out_specs=pl.BlockSpec((tm, tn), lambda i, j, kk: (i, j)),
        out_shape=jax.ShapeDtypeStruct((m, n), out_dtype),
        scratch_shapes=[pltpu.VMEM((tm, tn), F32)] if nk > 1 else [],
        compiler_params=_cparams("parallel", "parallel", "arbitrary"),
    )(*operands)


def _ln_stats(pre):
    mu = jnp.mean(pre, axis=-1, keepdims=True)
    xc = pre - mu
    var = jnp.mean(xc * xc, axis=-1, keepdims=True)
    rstd = lax.rsqrt(var + LN_EPS)
    return xc * rstd, rstd


def _ln_fwd(xin, branch, g, b, name):
    t, d = xin.shape
    tile = 256

    def body(x_ref, br_ref, g_ref, b_ref, pre_ref, y_ref):
        pre = ALPHA * x_ref[...] + br_ref[...]
        xhat, _ = _ln_stats(pre)
        pre_ref[...] = pre
        y_ref[...] = xhat * g_ref[...] + b_ref[...]

    row = pl.BlockSpec((tile, d), lambda i: (i, 0))
    vec = pl.BlockSpec((1, d), lambda i: (0, 0))
    return pl.pallas_call(
        body, name=name, grid=(t // tile,), in_specs=[row, row, vec, vec], out_specs=[row, row],
        out_shape=[jax.ShapeDtypeStruct((t, d), F32)] * 2, compiler_params=_cparams("parallel"),
    )(xin, branch, g.reshape(1, d), b.reshape(1, d))


def _ln_bwd(dy, pre, g, name):
    t, d = dy.shape
    tile = 256

    def body(dy_ref, pre_ref, g_ref, dpre_ref, dgb_ref):
        dyv = dy_ref[...]
        xhat, rstd = _ln_stats(pre_ref[...])
        dxh = dyv * g_ref[...]
        m1 = jnp.mean(dxh, axis=-1, keepdims=True)
        m2 = jnp.mean(dxh * xhat, axis=-1, keepdims=True)
        dpre_ref[...] = rstd * (dxh - m1 - xhat * m2)

        @pl.when(pl.program_id(0) == 0)
        def _():
            dgb_ref[...] = jnp.zeros_like(dgb_ref)

        dgb_ref[0:1, :] += jnp.sum(dyv * xhat, axis=0, keepdims=True)
        dgb_ref[1:2, :] += jnp.sum(dyv, axis=0, keepdims=True)

    row = pl.BlockSpec((tile, d), lambda i: (i, 0))
    return pl.pallas_call(
        body, name=name, grid=(t // tile,), in_specs=[row, row, pl.BlockSpec((1, d), lambda i: (0, 0))],
        out_specs=[row, pl.BlockSpec((8, d), lambda i: (0, 0))],
        out_shape=[jax.ShapeDtypeStruct((t, d), F32), jax.ShapeDtypeStruct((8, d), F32)],
        compiler_params=_cparams("arbitrary"),
    )(dy, pre, g.reshape(1, d))


def _swiglu_fwd(gu, name):
    t = gu.shape[0]
    tile = 256

    def body(gu_ref, h_ref):
        gate = gu_ref[:, :D_FF]
        up = gu_ref[:, D_FF:]
        h_ref[...] = (gate * (1.0 / (1.0 + jnp.exp(-gate))) * up).astype(h_ref.dtype)

    return pl.pallas_call(
        body, name=name, grid=(t // tile,), in_specs=[pl.BlockSpec((tile, 2 * D_FF), lambda i: (i, 0))],
        out_specs=pl.BlockSpec((tile, D_FF), lambda i: (i, 0)),
        out_shape=jax.ShapeDtypeStruct((t, D_FF), MXU_DTYPE), compiler_params=_cparams("parallel"),
    )(gu)


def _swiglu_bwd(dh, gu, name):
    t = gu.shape[0]
    tile = 256

    def body(dh_ref, gu_ref, dgu_ref):
        gate = gu_ref[:, :D_FF]
        up = gu_ref[:, D_FF:]
        dhv = dh_ref[...]
        sig = 1.0 / (1.0 + jnp.exp(-gate))
        dgu_ref[:, :D_FF] = (dhv * up * sig * (1.0 + gate * (1.0 - sig))).astype(dgu_ref.dtype)
        dgu_ref[:, D_FF:] = (dhv * gate * sig).astype(dgu_ref.dtype)

    return pl.pallas_call(
        body, name=name, grid=(t // tile,),
        in_specs=[pl.BlockSpec((tile, D_FF), lambda i: (i, 0)), pl.BlockSpec((tile, 2 * D_FF), lambda i: (i, 0))],
        out_specs=pl.BlockSpec((tile, 2 * D_FF), lambda i: (i, 0)),
        out_shape=jax.ShapeDtypeStruct((t, 2 * D_FF), MXU_DTYPE), compiler_params=_cparams("parallel"),
    )(dh, gu)


def _loss_kernel(y, target, name):
    t, d = y.shape
    tile = 512

    def body(y_ref, t_ref, dy_ref, l_ref):
        err = y_ref[...] - t_ref[...]
        dy_ref[...] = err * (1.0 / d)

        @pl.when(pl.program_id(0) == 0)
        def _():
            l_ref[...] = jnp.zeros_like(l_ref)

        l_ref[...] += jnp.sum(err * err) * (0.5 / d)

    row = pl.BlockSpec((tile, d), lambda i: (i, 0))
    return pl.pallas_call(
        body, name=name, grid=(t // tile,), in_specs=[row, row],
        out_specs=[row, pl.BlockSpec((8, 128), lambda i: (0, 0))],
        out_shape=[jax.ShapeDtypeStruct((t, d), F32), jax.ShapeDtypeStruct((8, 128), F32)],
        compiler_params=_cparams("arbitrary"),
    )(y, target)


def _adamw(w, g, m, v, name):
    nl, r, c = w.shape
    tr = r
    for cand in (256, 352, 128, 64, 16, 8):
        if r % cand == 0:
            tr = cand
            break

    def body(w_ref, g_ref, m_ref, v_ref, d_ref, nm_ref, nv_ref):
        gv = g_ref[...]
        nm = ADAM_B1 * m_ref[...] + (1.0 - ADAM_B1) * gv
        nv = ADAM_B2 * v_ref[...] + (1.0 - ADAM_B2) * (gv * gv)
        m_hat = nm / (1.0 - ADAM_B1 ** ADAM_STEP)
        v_hat = nv / (1.0 - ADAM_B2 ** ADAM_STEP)
        d_ref[...] = -ADAM_LR * (m_hat / (jnp.sqrt(v_hat) + ADAM_EPS) + ADAM_WD * w_ref[...])
        nm_ref[...] = nm
        nv_ref[...] = nv

    blk = pl.BlockSpec((1, tr, c), lambda l, i: (l, i, 0))
    return pl.pallas_call(
        body, name=name, grid=(nl, r // tr), in_specs=[blk] * 4, out_specs=[blk] * 3,
        out_shape=[jax.ShapeDtypeStruct(w.shape, F32)] * 3, compiler_params=_cparams("parallel", "parallel"),
    )(w, g, m, v)


def _shift_down(u, k, rows):
    return jnp.where(rows >= k, pltpu.roll(u, k, 0), 0.0)


def _shift_up(u, k, rows, s):
    return jnp.where(rows < s - k, pltpu.roll(u, s - k, 0), 0.0)


def _conv_fwd(proj, conv_w, nb, s, name):
    def body(b_ref, c_ref, h_ref, w_ref, o_ref):
        rows = _iota2((s, CONV_W), 0)
        u = c_ref[...] * h_ref[...]
        y = w_ref[2:3, :] * u + w_ref[1:2, :] * _shift_down(u, 1, rows) + w_ref[0:1, :] * _shift_down(u, 2, rows)
        o_ref[...] = b_ref[...] * y

    col = lambda j: pl.BlockSpec((s, CONV_W), lambda b: (b, j))
    return pl.pallas_call(
        body, name=name, grid=(nb,),
        in_specs=[col(9), col(10), col(11), pl.BlockSpec((8, CONV_W), lambda b: (0, 0))],
        out_specs=pl.BlockSpec((s, CONV_W), lambda b: (b, 0)),
        out_shape=jax.ShapeDtypeStruct((nb * s, CONV_W), F32), compiler_params=_cparams("parallel"),
    )(proj, proj, proj, conv_w)


def _conv_bwd(dmixed, proj, conv_w, nb, s, name):
    def body(do_ref, b_ref, c_ref, h_ref, w_ref, dg_ref, dw_ref):
        rows = _iota2((s, CONV_W), 0)
        cg, hg, bg, dout = c_ref[...], h_ref[...], b_ref[...], do_ref[...]
        u = cg * hg
        u1 = _shift_down(u, 1, rows)
        u2 = _shift_down(u, 2, rows)
        y = w_ref[2:3, :] * u + w_ref[1:2, :] * u1 + w_ref[0:1, :] * u2
        dy = dout * bg
        du = w_ref[2:3, :] * dy + w_ref[1:2, :] * _shift_up(dy, 1, rows, s) + w_ref[0:1, :] * _shift_up(dy, 2, rows, s)
        dg_ref[:, 0:CONV_W] = dout * y
        dg_ref[:, CONV_W:2 * CONV_W] = du * hg
        dg_ref[:, 2 * CONV_W:3 * CONV_W] = du * cg

        @pl.when(pl.program_id(0) == 0)
        def _():
            dw_ref[...] = jnp.zeros_like(dw_ref)

        dw_ref[0:1, :] += jnp.sum(dy * u2, axis=0, keepdims=True)
        dw_ref[1:2, :] += jnp.sum(dy * u1, axis=0, keepdims=True)
        dw_ref[2:3, :] += jnp.sum(dy * u, axis=0, keepdims=True)

    col = lambda j: pl.BlockSpec((s, CONV_W), lambda b: (b, j))
    return pl.pallas_call(
        body, name=name, grid=(nb,),
        in_specs=[col(3), col(9), col(10), col(11), pl.BlockSpec((8, CONV_W), lambda b: (0, 0))],
        out_specs=[pl.BlockSpec((s, 3 * CONV_W), lambda b: (b, 0)), pl.BlockSpec((8, CONV_W), lambda b: (0, 0))],
        out_shape=[jax.ShapeDtypeStruct((nb * s, 3 * CONV_W), F32), jax.ShapeDtypeStruct((8, CONV_W), F32)],
        compiler_params=_cparams("arbitrary"),
    )(dmixed, proj, proj, proj, conv_w)


def _col_spec(s, base):
    return pl.BlockSpec((s, BLK), lambda b, p: (b, base + p))


def _rows(i):
    return pl.ds(pl.multiple_of(i * ATT, ATT), ATT)


def _rows128(i):
    return pl.ds(pl.multiple_of(i * BLK, BLK), BLK)


def _log_sigmoid_parts(z):
    e = jnp.exp(-jnp.abs(z))
    l1p = jnp.log(1.0 + e)
    lb = jnp.minimum(z, 0.0) - l1p
    return lb, lb - z, e


def _head_masks():
    lane = _iota2((1, BLK), 1)
    return [(lane >= h * HEAD_DIM) & (lane < (h + 1) * HEAD_DIM) for h in range(2)]


def _split_heads(ref, scr, sels):
    for h, sel in enumerate(sels):
        scr[h] = jnp.where(sel, ref[...], 0.0).astype(MXU_DTYPE)


def _sb_fwd(proj, nb, s, name):
    nblk = s // ATT

    def body(q_ref, k_ref, v_ref, o_ref, km, vm):
        sels = _head_masks()
        _split_heads(k_ref, km, sels)
        _split_heads(v_ref, vm, sels)
        rows = _iota2((ATT, ATT), 0)
        cols = _iota2((ATT, ATT), 1)
        later = (rows > cols).astype(MXU_DTYPE)

        def qblock(i, _):
            qi = (q_ref[_rows(i), :] * 0.125).astype(MXU_DTYPE)

            def kblock(t, state):
                carries, acc = state
                j = i - t
                strict = (cols + (j - i) * ATT) < rows
                out = []
                for h in range(2):
                    z = _dot_nt(qi, km[h, _rows(j), :])
                    lb, lr, _ = _log_sigmoid_parts(z)
                    lr = jnp.where(strict, lr, 0.0)
                    tail = _split_dot(lr, later, 2) + carries[h]
                    a = jnp.where(strict, jnp.exp(lb + tail), 0.0)
                    acc = acc + _dot(a, vm[h, _rows(j), :])
                    out.append(carries[h] + jnp.sum(lr, axis=-1, keepdims=True))
                return tuple(out), acc

            init = ((jnp.zeros((ATT, 1), F32),) * 2, jnp.zeros((ATT, BLK), F32))
            _, acc = lax.fori_loop(0, i + 1, kblock, init)
            o_ref[_rows(i), :] = acc
            return 0

        lax.fori_loop(0, nblk, qblock, 0)

    return pl.pallas_call(
        body, name=name, grid=(nb, 2), in_specs=[_col_spec(s, 0), _col_spec(s, 2), _col_spec(s, 4)],
        out_specs=_col_spec(s, 0), out_shape=jax.ShapeDtypeStruct((nb * s, 2 * BLK), F32),
        scratch_shapes=[pltpu.VMEM((2, s, BLK), MXU_DTYPE)] * 2, compiler_params=_cparams("parallel", "parallel"),
    )(proj, proj, proj)


def _sb_bwd(proj, dmixed, nb, s, name):
    nblk = s // ATT

    def body(q_ref, k_ref, v_ref, do_ref, dq_ref, dk_ref, dv_ref, km, vm, a_scr, dl_scr, beta_scr):
        sels = _head_masks()
        _split_heads(k_ref, km, sels)
        _split_heads(v_ref, vm, sels)
        rows = _iota2((ATT, ATT), 0)
        cols = _iota2((ATT, ATT), 1)
        later = (rows > cols).astype(MXU_DTYPE)
        earlier = (rows < cols).astype(MXU_DTYPE)
        dk_ref[...] = jnp.zeros_like(dk_ref)
        dv_ref[...] = jnp.zeros_like(dv_ref)

        def qblock(i, _):
            qi = (q_ref[_rows(i), :] * 0.125).astype(MXU_DTYPE)
            doi = do_ref[_rows(i), :].astype(MXU_DTYPE)
            qm = [jnp.where(sel, qi, 0.0) for sel in sels]
            dom = [jnp.where(sel, doi, 0.0) for sel in sels]

            def first(t, carries):
                j = i - t
                strict = (cols + (j - i) * ATT) < rows
                out = []
                for h in range(2):
                    z = _dot_nt(qi, km[h, _rows(j), :])
                    lb, lr, e = _log_sigmoid_parts(z)
                    lr = jnp.where(strict, lr, 0.0)
                    tail = _split_dot(lr, later, 2) + carries[h]
                    a = jnp.where(strict, jnp.exp(lb + tail), 0.0)
                    a_scr[h, j] = a
                    dl_scr[h, j] = a * _dot_nt(doi, vm[h, _rows(j), :])
                    beta_scr[h, j] = jnp.where(z >= 0.0, 1.0, e) / (1.0 + e)
                    out.append(carries[h] + jnp.sum(lr, axis=-1, keepdims=True))
                return tuple(out)

            lax.fori_loop(0, i + 1, first, (jnp.zeros((ATT, 1), F32),) * 2)

            def second(j, state):
                csums, dq = state
                strict = (cols + (j - i) * ATT) < rows
                out = []
                for h in range(2):
                    dl = dl_scr[h, j]
                    beta = beta_scr[h, j]
                    before = _split_dot(dl, earlier, 2) + csums[h]
                    dz = jnp.where(strict, dl * (1.0 - beta) - beta * before, 0.0).astype(MXU_DTYPE)
                    dq = dq + _dot(dz, km[h, _rows(j), :])
                    dk_ref[_rows(j), :] += _dot_tn(dz, qm[h])
                    dv_ref[_rows(j), :] += _dot_tn(a_scr[h, j], dom[h])
                    out.append(csums[h] + jnp.sum(dl, axis=-1, keepdims=True))
                return tuple(out), dq

            init = ((jnp.zeros((ATT, 1), F32),) * 2, jnp.zeros((ATT, BLK), F32))
            _, dq = lax.fori_loop(0, i + 1, second, init)
            dq_ref[_rows(i), :] = dq * 0.125
            return 0

        lax.fori_loop(0, nblk, qblock, 0)

    out = _col_spec(s, 0)
    return pl.pallas_call(
        body, name=name, grid=(nb, 2),
        in_specs=[_col_spec(s, 0), _col_spec(s, 2), _col_spec(s, 4), out], out_specs=[out] * 3,
        out_shape=[jax.ShapeDtypeStruct((nb * s, 2 * BLK), F32)] * 3,
        scratch_shapes=[pltpu.VMEM((2, s, BLK), MXU_DTYPE)] * 2 + [pltpu.VMEM((2, nblk, ATT, ATT), F32)] * 3,
        compiler_params=_cparams("parallel", "parallel"),
    )(proj, proj, proj, dmixed)


def _pair_spec(s, width):
    return pl.BlockSpec((None, 2, s, width), lambda b, p: (b, p, 0, 0))


def _fox_fwd(proj, ccol, crow, nb, s, name):
    nblk = s // ATT

    def body(q_ref, k_ref, v_ref, cc_ref, cr_ref, o_ref, lse_ref, km, vm):
        sels = _head_masks()
        _split_heads(k_ref, km, sels)
        _split_heads(v_ref, vm, sels)
        rows = _iota2((ATT, ATT), 0)
        cols = _iota2((ATT, ATT), 1)

        def qblock(i, _):
            qi = (q_ref[_rows(i), :] * 0.125).astype(MXU_DTYPE)
            ci = [cc_ref[h, _rows(i), :] for h in range(2)]

            def kblock(j, state):
                ms, ls, acc = state
                causal = (cols + (j - i) * ATT) <= rows
                new_m, new_l, scales, parts = [], [], [], []
                for h in range(2):
                    z = _dot_nt(qi, km[h, _rows(j), :]) + (ci[h] - cr_ref[h, j][0:1, :])
                    z = jnp.where(causal, z, NEG)
                    m_new = jnp.maximum(ms[h], jnp.max(z, axis=-1, keepdims=True))
                    p = jnp.exp(z - m_new)
                    scale = jnp.exp(ms[h] - m_new)
                    new_m.append(m_new)
                    new_l.append(scale * ls[h] + jnp.sum(p, axis=-1, keepdims=True))
                    scales.append(scale)
                    parts.append(_dot(p, vm[h, _rows(j), :]))
                acc = jnp.where(sels[0], scales[0], scales[1]) * acc + parts[0] + parts[1]
                return tuple(new_m), tuple(new_l), acc

            init = ((jnp.full((ATT, 1), NEG, F32),) * 2, (jnp.zeros((ATT, 1), F32),) * 2, jnp.zeros((ATT, BLK), F32))
            ms, ls, acc = lax.fori_loop(0, i + 1, kblock, init)
            o_ref[_rows(i), :] = acc / jnp.where(sels[0], ls[0], ls[1])
            for h in range(2):
                lse_ref[h, _rows(i), :] = jnp.broadcast_to(ms[h] + jnp.log(ls[h]), (ATT, ATT))
            return 0

        lax.fori_loop(0, nblk, qblock, 0)

    crow_spec = pl.BlockSpec((None, 2, nblk, 8, ATT), lambda b, p: (b, p, 0, 0, 0))
    return pl.pallas_call(
        body, name=name, grid=(nb, 2),
        in_specs=[_col_spec(s, 12), _col_spec(s, 14), _col_spec(s, 16), _pair_spec(s, ATT), crow_spec],
        out_specs=[_col_spec(s, 0), _pair_spec(s, ATT)],
        out_shape=[jax.ShapeDtypeStruct((nb * s, 2 * BLK), F32), jax.ShapeDtypeStruct((nb, N_HEADS, s, ATT), F32)],
        scratch_shapes=[pltpu.VMEM((2, s, BLK), MXU_DTYPE)] * 2, compiler_params=_cparams("parallel", "parallel"),
    )(proj, proj, proj, ccol, crow)


def _fox_bwd(proj, dmixed, lse, ccol, crow, nb, s, name):
    nblk = s // ATT

    def body(q_ref, k_ref, v_ref, do_ref, lse_ref, cc_ref, cr_ref, dq_ref, dk_ref, dv_ref, dc_ref, km, vm):
        sels = _head_masks()
        _split_heads(k_ref, km, sels)
        _split_heads(v_ref, vm, sels)
        rows = _iota2((ATT, ATT), 0)
        cols = _iota2((ATT, ATT), 1)
        dk_ref[...] = jnp.zeros_like(dk_ref)
        dv_ref[...] = jnp.zeros_like(dv_ref)
        dc_ref[...] = jnp.zeros_like(dc_ref)

        def qblock(i, _):
            qi = (q_ref[_rows(i), :] * 0.125).astype(MXU_DTYPE)
            doi = do_ref[_rows(i), :].astype(MXU_DTYPE)
            qm = [jnp.where(sel, qi, 0.0) for sel in sels]
            dom = [jnp.where(sel, doi, 0.0) for sel in sels]
            ci = [cc_ref[h, _rows(i), :] for h in range(2)]
            lsei = [lse_ref[h, _rows(i), :] for h in range(2)]

            def probs(j, h):
                z = _dot_nt(qi, km[h, _rows(j), :]) + (ci[h] - cr_ref[h, j][0:1, :])
                p = jnp.where((cols + (j - i) * ATT) <= rows, jnp.exp(z - lsei[h]), 0.0)
                return p, _dot_nt(doi, vm[h, _rows(j), :])

            def row_term(j, accs):
                out = []
                for h in range(2):
                    p, dp = probs(j, h)
                    out.append(accs[h] + jnp.sum(p * dp, axis=-1, keepdims=True))
                return tuple(out)

            di = lax.fori_loop(0, i + 1, row_term, (jnp.zeros((ATT, 1), F32),) * 2)

            def kblock(j, dq):
                for h in range(2):
                    p, dp = probs(j, h)
                    ds = p * (dp - di[h])
                    dc_ref[h, j] += jnp.broadcast_to(jnp.sum(ds, axis=0, keepdims=True), (8, ATT))
                    ds = ds.astype(MXU_DTYPE)
                    dk_ref[_rows(j), :] += _dot_tn(ds, qm[h])
                    dv_ref[_rows(j), :] += _dot_tn(p, dom[h])
                    dq = dq + _dot(ds, km[h, _rows(j), :])
                return dq

            dq = lax.fori_loop(0, i + 1, kblock, jnp.zeros((ATT, BLK), F32))
            dq_ref[_rows(i), :] = dq * 0.125
            return 0

        lax.fori_loop(0, nblk, qblock, 0)

    crow_spec = pl.BlockSpec((None, 2, nblk, 8, ATT), lambda b, p: (b, p, 0, 0, 0))
    wide, cols_out = _pair_spec(s, ATT), _col_spec(s, 0)
    return pl.pallas_call(
        body, name=name, grid=(nb, 2),
        in_specs=[_col_spec(s, 12), _col_spec(s, 14), _col_spec(s, 16), _col_spec(s, 4), wide, wide, crow_spec],
        out_specs=[cols_out, cols_out, cols_out, crow_spec],
        out_shape=[jax.ShapeDtypeStruct((nb * s, 2 * BLK), F32)] * 3 + [jax.ShapeDtypeStruct((nb, N_HEADS, nblk, 8, ATT), F32)],
        scratch_shapes=[pltpu.VMEM((2, s, BLK), MXU_DTYPE)] * 2, compiler_params=_cparams("parallel", "parallel"),
    )(proj, proj, proj, dmixed, lse, ccol, crow)


def _fox_gates_fwd(proj, f_bias, nb, s, name):
    chunk = 256

    def body(f_ref, b_ref, c_ref):
        lower = (_iota2((chunk, chunk), 0) >= _iota2((chunk, chunk), 1)).astype(MXU_DTYPE)
        carry = jnp.zeros((1, BLK), F32)
        for n in range(s // chunk):
            rows = pl.ds(n * chunk, chunk)
            lf, _, _ = _log_sigmoid_parts(f_ref[rows, :] + b_ref[0:1, :])
            c = _split_dot_lhs(lower, lf, 3) + carry
            c_ref[rows, :] = c
            carry = c[chunk - 1:chunk, :]

    return pl.pallas_call(
        body, name=name, grid=(nb,),
        in_specs=[pl.BlockSpec((s, BLK), lambda b: (b, (PROJ_PAD - BLK) // BLK)), pl.BlockSpec((8, BLK), lambda b: (0, 0))],
        out_specs=pl.BlockSpec((s, BLK), lambda b: (b, 0)),
        out_shape=jax.ShapeDtypeStruct((nb * s, BLK), F32), compiler_params=_cparams("parallel"),
    )(proj, f_bias)


def _fox_gates_bwd(dc, proj, f_bias, nb, s, name):
    chunk = 256

    def body(dc_ref, f_ref, b_ref, df_ref, db_ref):
        upper = (_iota2((chunk, chunk), 0) <= _iota2((chunk, chunk), 1)).astype(MXU_DTYPE)
        carry = jnp.zeros((1, BLK), F32)
        total = jnp.zeros((1, BLK), F32)
        for n in reversed(range(s // chunk)):
            rows = pl.ds(n * chunk, chunk)
            dlf = _split_dot_lhs(upper, dc_ref[rows, :], 3) + carry
            carry = dlf[0:1, :]
            pre = f_ref[rows, :] + b_ref[0:1, :]
            e = jnp.exp(-jnp.abs(pre))
            df = dlf * (jnp.where(pre >= 0.0, e, 1.0) / (1.0 + e))
            df_ref[rows, :] = df
            total = total + jnp.sum(df, axis=0, keepdims=True)

        @pl.when(pl.program_id(0) == 0)
        def _():
            db_ref[...] = jnp.zeros_like(db_ref)

        db_ref[0:1, :] += total

    return pl.pallas_call(
        body, name=name, grid=(nb,),
        in_specs=[pl.BlockSpec((s, BLK), lambda b: (b, 0)), pl.BlockSpec((s, BLK), lambda b: (b, (PROJ_PAD - BLK) // BLK)),
                  pl.BlockSpec((8, BLK), lambda b: (0, 0))],
        out_specs=[pl.BlockSpec((s, BLK), lambda b: (b, 0)), pl.BlockSpec((8, BLK), lambda b: (0, 0))],
        out_shape=[jax.ShapeDtypeStruct((nb * s, BLK), F32), jax.ShapeDtypeStruct((8, BLK), F32)],
        compiler_params=_cparams("arbitrary"),
    )(dc, proj, f_bias)


def _delta_kernel(dmixed, o, nb, s, name):
    def body(do_ref, o_ref, d_ref):
        prod = do_ref[...] * o_ref[...]
        for h, sel in enumerate(_head_masks()):
            d_ref[h] = jnp.broadcast_to(jnp.sum(jnp.where(sel, prod, 0.0), axis=-1, keepdims=True), (s, BLK))

    return pl.pallas_call(
        body, name=name, grid=(nb, 2), in_specs=[_col_spec(s, 2), _col_spec(s, 0)], out_specs=_pair_spec(s, BLK),
        out_shape=jax.ShapeDtypeStruct((nb, N_HEADS, s, BLK), F32), compiler_params=_cparams("parallel", "parallel"),
    )(dmixed, o)


def _t5_bucket_np(dist):
    max_exact = REL_BUCKETS // 2
    nf = np.maximum(dist, 1).astype(np.float32)
    large = max_exact + (np.log(nf / max_exact) / math.log(2048 / max_exact) * (REL_BUCKETS - max_exact)).astype(np.int32)
    large = np.minimum(large, REL_BUCKETS - 1)
    return np.where(dist < max_exact, dist, large)


def _bucket_table():
    qi = np.arange(BLK)[:, None]
    kj = np.arange(2 * BLK)[None, :]
    dist = qi + BLK - kj
    tables = []
    for window, dil in DIL_PATTERNS:
        in_band = (dist >= 0) & (dist <= window // dil)
        tables.append(np.where(in_band, _t5_bucket_np(np.maximum(dist, 0) * dil), -1).astype(np.int32))
    return np.stack(tables)


def _to_residue(t, dil):
    if dil == 1:
        return t
    *lead, s, e = t.shape
    return jnp.swapaxes(t.reshape(*lead, s // dil, dil, e), -3, -2).reshape(*lead, s, e)


def _from_residue(t, dil):
    if dil == 1:
        return t
    *lead, s, e = t.shape
    return jnp.swapaxes(t.reshape(*lead, dil, s // dil, e), -3, -2).reshape(*lead, s, e)


def _pat_col_spec(s, base=0):
    return pl.BlockSpec((3, None, s, BLK), lambda b, p: (0, b, 0, base + p))


def _pat_pair_spec(s):
    return pl.BlockSpec((3, None, 2, s, BLK), lambda b, p: (0, b, p, 0, 0))


def _one_pat_col_spec(s, base=0):
    return pl.BlockSpec((None, None, s, BLK), lambda b, p, t: (t, b, 0, base + p))


def _one_pat_pair_spec(s):
    return pl.BlockSpec((None, None, 2, s, BLK), lambda b, p, t: (t, b, p, 0, 0))


def _one_pat_bias_spec():
    return pl.BlockSpec((None, 2, BLK, 2 * BLK), lambda b, p, t: (t, p, 0, 0))


def _blocks_per_class(s):
    t = pl.program_id(2)
    segs = [s // dil // BLK for _, dil in DIL_PATTERNS]
    return jnp.where(t == 0, segs[0], jnp.where(t == 1, segs[1], segs[2]))


def _dil_scores(qb, kp, kc, b_ref, h, prev_valid):
    zp = _dot_nt(qb, kp) + b_ref[h, :, 0:BLK]
    zp = jnp.where(prev_valid, zp, NEG)
    zc = _dot_nt(qb, kc) + b_ref[h, :, BLK:2 * BLK]
    return zp, zc


def _dil_fwd(qkvp, bias, name):
    _, nb, s, _ = qkvp.shape
    nblk = s // BLK

    def body(q_ref, k_ref, v_ref, b_ref, o_ref, lse_ref):
        sels = _head_masks()
        seg = _blocks_per_class(s)

        def block(b, _):
            cur = _rows128(b)
            prev = _rows128(jnp.maximum(b - 1, 0))
            qb = q_ref[cur, :] * 0.125
            kp, kc, vp, vc = k_ref[prev, :], k_ref[cur, :], v_ref[prev, :], v_ref[cur, :]
            acc = jnp.zeros((BLK, BLK), F32)
            for h, sel in enumerate(sels):
                zp, zc = _dil_scores(qb, jnp.where(sel, kp, 0.0), jnp.where(sel, kc, 0.0), b_ref, h, b % seg > 0)
                m = jnp.maximum(jnp.max(zp, axis=-1, keepdims=True), jnp.max(zc, axis=-1, keepdims=True))
                pp = jnp.exp(zp - m)
                pc = jnp.exp(zc - m)
                den = jnp.sum(pp, axis=-1, keepdims=True) + jnp.sum(pc, axis=-1, keepdims=True)
                acc = acc + (_dot(pp, jnp.where(sel, vp, 0.0)) + _dot(pc, jnp.where(sel, vc, 0.0))) / den
                lse_ref[h, cur, :] = jnp.broadcast_to(m + jnp.log(den), (BLK, BLK))
            o_ref[cur, :] = acc
            return 0

        lax.fori_loop(0, nblk, block, 0, unroll=2)

    return pl.pallas_call(
        body, name=name, grid=(nb, 2, len(DIL_PATTERNS)),
        in_specs=[_one_pat_col_spec(s, 0), _one_pat_col_spec(s, 2), _one_pat_col_spec(s, 4), _one_pat_bias_spec()],
        out_specs=[_one_pat_col_spec(s), _one_pat_pair_spec(s)],
        out_shape=[jax.ShapeDtypeStruct((3, nb, s, 2 * BLK), F32), jax.ShapeDtypeStruct((3, nb, N_HEADS, s, BLK), F32)],
        compiler_params=_cparams("parallel", "parallel", "parallel"),
    )(qkvp, qkvp, qkvp, bias)


def _dil_combine(o, lse, name):
    _, nb, s, _ = o.shape

    def body(o_ref, l_ref, out_ref, lse_ref):
        sels = _head_masks()
        weights, dens = [], []
        for h in range(2):
            m = jnp.maximum(jnp.maximum(l_ref[0, h], l_ref[1, h]), l_ref[2, h])
            w = [jnp.exp(l_ref[p, h] - m) for p in range(3)]
            den = w[0] + w[1] + w[2]
            lse_ref[h] = m + jnp.log(den)
            weights.append(w)
            dens.append(den)
        num = sum(jnp.where(sels[0], weights[0][p], weights[1][p]) * o_ref[p] for p in range(3))
        out_ref[...] = num / jnp.where(sels[0], dens[0], dens[1])

    return pl.pallas_call(
        body, name=name, grid=(nb, 2), in_specs=[_pat_col_spec(s), _pat_pair_spec(s)],
        out_specs=[_col_spec(s, 0), _pair_spec(s, BLK)],
        out_shape=[jax.ShapeDtypeStruct((nb * s, 2 * BLK), F32), jax.ShapeDtypeStruct((nb, N_HEADS, s, BLK), F32)],
        compiler_params=_cparams("parallel", "parallel"),
    )(o, lse)


def _dil_bwd(qkvp, dop, lsep, deltap, bias, name):
    _, nb, s, _ = qkvp.shape
    nblk = s // BLK

    def body(q_ref, k_ref, v_ref, do_ref, lse_ref, dl_ref, b_ref, dq_ref, dk_ref, dv_ref, g_ref):
        sels = _head_masks()
        seg = _blocks_per_class(s)
        dk_ref[...] = jnp.zeros_like(dk_ref)
        dv_ref[...] = jnp.zeros_like(dv_ref)
        g_ref[...] = jnp.zeros_like(g_ref)

        def block(b, _):
            cur = _rows128(b)
            prev = _rows128(jnp.maximum(b - 1, 0))
            qb = q_ref[cur, :] * 0.125
            dob = do_ref[cur, :]
            kp, kc, vp, vc = k_ref[prev, :], k_ref[cur, :], v_ref[prev, :], v_ref[cur, :]
            dq = jnp.zeros((BLK, BLK), F32)
            for h, sel in enumerate(sels):
                kph, kch = jnp.where(sel, kp, 0.0), jnp.where(sel, kc, 0.0)
                qh, doh = jnp.where(sel, qb, 0.0), jnp.where(sel, dob, 0.0)
                lse = lse_ref[h, cur, :]
                dlt = dl_ref[h, cur, :]
                zp, zc = _dil_scores(qb, kph, kch, b_ref, h, b % seg > 0)
                pp = jnp.exp(zp - lse)
                pc = jnp.exp(zc - lse)
                dsp = pp * (_dot_nt(dob, jnp.where(sel, vp, 0.0)) - dlt)
                dsc = pc * (_dot_nt(dob, jnp.where(sel, vc, 0.0)) - dlt)
                g_ref[h, :, 0:BLK] += dsp
                g_ref[h, :, BLK:2 * BLK] += dsc
                dsp = dsp.astype(MXU_DTYPE)
                dsc = dsc.astype(MXU_DTYPE)
                dq = dq + _dot(dsp, kph) + _dot(dsc, kch)
                dk_ref[prev, :] += _dot_tn(dsp, qh)
                dk_ref[cur, :] += _dot_tn(dsc, qh)
                dv_ref[prev, :] += _dot_tn(pp, doh)
                dv_ref[cur, :] += _dot_tn(pc, doh)
            dq_ref[cur, :] = dq * 0.125
            return 0

        lax.fori_loop(0, nblk, block, 0, unroll=2)

    cols, stats = _one_pat_col_spec(s), _one_pat_pair_spec(s)
    return pl.pallas_call(
        body, name=name, grid=(nb, 2, len(DIL_PATTERNS)),
        in_specs=[_one_pat_col_spec(s, 0), _one_pat_col_spec(s, 2), _one_pat_col_spec(s, 4), cols, stats, stats,
                  _one_pat_bias_spec()],
        out_specs=[cols, cols, cols, pl.BlockSpec((None, 2, None, BLK, 2 * BLK), lambda b, p, t: (b, p, t, 0, 0))],
        out_shape=[jax.ShapeDtypeStruct((3, nb, s, 2 * BLK), F32)] * 3 + [jax.ShapeDtypeStruct((nb, N_HEADS, 3, BLK, 2 * BLK), F32)],
        compiler_params=_cparams("parallel", "parallel", "parallel"),
    )(qkvp, qkvp, qkvp, dop, lsep, deltap, bias)


def _bucket_reduce(gbias, table, name):
    nb = gbias.shape[0]

    def body(g_ref, t_ref, o_ref):
        row = _iota2((8, BLK), 0)
        lane = _iota2((8, BLK), 1)
        gsum = [[sum(g_ref[b, h, p] for b in range(nb)) for p in range(3)] for h in range(N_HEADS)]

        def bucket(k, acc):
            for h in range(N_HEADS):
                tot = sum(jnp.sum(jnp.where(t_ref[p] == k, gsum[h][p], 0.0)) for p in range(3))
                acc = acc + jnp.where((row == h) & (lane == k), tot, 0.0)
            return acc

        o_ref[...] = lax.fori_loop(0, REL_BUCKETS, bucket, jnp.zeros((8, BLK), F32))

    vm = pl.BlockSpec(memory_space=pltpu.VMEM)
    return pl.pallas_call(
        body, name=name, in_specs=[vm, vm], out_specs=vm, out_shape=jax.ShapeDtypeStruct((8, BLK), F32),
        compiler_params=pltpu.CompilerParams(vmem_limit_bytes=VMEM_LIMIT),
    )(gbias, table)


def _place():
    x, y, c = lax.axis_index("x"), lax.axis_index("y"), lax.axis_index("c")
    others = [(1 - x, y), (x, 1 - y), (1 - x, 1 - y)]
    return x, y, c, others


def _remote(src, dst, send_sem, recv_sem, to):
    return pltpu.make_async_remote_copy(src_ref=src, dst_ref=dst, send_sem=send_sem, recv_sem=recv_sem,
                                        device_id=to, device_id_type=MESH)


def _gather_weights(wp):
    nl, r, w = wp.shape

    def body(wp_ref, out_ref, send_sems, recv_sems, local_sem):
        x, y, c, others = _place()
        me = 2 * x + y
        sibling = (x, y, 1 - c)
        mine = pltpu.make_async_copy(wp_ref, out_ref.at[me], local_sem)
        mine.start()
        sends = [_remote(wp_ref.at[c], out_ref.at[me, c], send_sems.at[k], recv_sems.at[k], (ox, oy, c))
                 for k, (ox, oy) in enumerate(others)]
        for cp in sends:
            cp.start()
        passed = []
        for k, (ox, oy) in enumerate(others):
            landed = out_ref.at[2 * ox + oy, c]
            _remote(landed, landed, send_sems.at[k], recv_sems.at[k], (ox, oy, c)).wait_recv()
            cp = _remote(landed, landed, send_sems.at[3 + k], recv_sems.at[3 + k], sibling)
            cp.start()
            passed.append(cp)
        for k, (ox, oy) in enumerate(others):
            theirs = out_ref.at[2 * ox + oy, 1 - c]
            _remote(theirs, theirs, send_sems.at[3 + k], recv_sems.at[3 + k], sibling).wait_recv()
        for cp in sends + passed:
            cp.wait_send()
        mine.wait()

    hbm = pl.BlockSpec(memory_space=pl.ANY)
    return pl.pallas_call(
        body, name="gather_weights", in_specs=[hbm], out_specs=hbm,
        out_shape=jax.ShapeDtypeStruct((N_CHIPS, nl, r, w), wp.dtype),
        scratch_shapes=[pltpu.SemaphoreType.DMA((6,)), pltpu.SemaphoreType.DMA((6,)), pltpu.SemaphoreType.DMA],
    )(wp)


def _swap_layers(g):
    _, ns, r, w = g.shape

    def body(g_ref, out_ref, send_sem, recv_sem):
        x, y, c, _ = _place()
        cp = _remote(g_ref.at[1 - c], out_ref, send_sem, recv_sem, (x, y, 1 - c))
        cp.start()
        cp.wait()

    hbm = pl.BlockSpec(memory_space=pl.ANY)
    return pl.pallas_call(
        body, name="swap_layers", in_specs=[hbm], out_specs=hbm, out_shape=jax.ShapeDtypeStruct((ns, r, w), g.dtype),
        scratch_shapes=[pltpu.SemaphoreType.DMA, pltpu.SemaphoreType.DMA],
    )(g)


def _pair_sum(g, other, core):
    _, ns, r, w = g.shape
    tw = 256

    def body(core_ref, g_ref, o_ref, out_ref):
        out_ref[...] = (g_ref[...] + o_ref[...]).astype(out_ref.dtype)

    grid_spec = pltpu.PrefetchScalarGridSpec(
        num_scalar_prefetch=1, grid=(ns, w // tw),
        in_specs=[pl.BlockSpec((None, None, r, tw), lambda k, j, core_ref: (core_ref[0], k, 0, j)),
                  pl.BlockSpec((None, r, tw), lambda k, j, core_ref: (k, 0, j))],
        out_specs=pl.BlockSpec((None, r, tw), lambda k, j, core_ref: (k, 0, j)))
    return pl.pallas_call(
        body, name="pair_sum", grid_spec=grid_spec, out_shape=jax.ShapeDtypeStruct((ns, r, w), MXU_DTYPE),
        compiler_params=_cparams("parallel", "parallel"),
    )(core.reshape(1).astype(jnp.int32), g, other)


def _scatter_shards(p):
    ns, r, w = p.shape

    def body(p_ref, q_ref, send_sems, recv_sems, local_sem):
        x, y, c, others = _place()
        me = 2 * x + y
        mine = pltpu.make_async_copy(p_ref.at[me], q_ref.at[me], local_sem)
        mine.start()
        sends = [_remote(p_ref.at[2 * ox + oy], q_ref.at[me], send_sems.at[k], recv_sems.at[k], (ox, oy, c))
                 for k, (ox, oy) in enumerate(others)]
        for cp in sends:
            cp.start()
        for k, (ox, oy) in enumerate(others):
            slot = q_ref.at[2 * ox + oy]
            _remote(slot, slot, send_sems.at[k], recv_sems.at[k], (ox, oy, c)).wait_recv()
        for cp in sends:
            cp.wait_send()
        mine.wait()

    hbm = pl.BlockSpec(memory_space=pl.ANY)
    return pl.pallas_call(
        body, name="scatter_shards", in_specs=[hbm], out_specs=hbm, out_shape=jax.ShapeDtypeStruct(p.shape, p.dtype),
        scratch_shapes=[pltpu.SemaphoreType.DMA((3,)), pltpu.SemaphoreType.DMA((3,)), pltpu.SemaphoreType.DMA],
    )(p)


def _chip_sum(q):
    ns, r, w = q.shape

    def body(q_ref, out_ref):
        out_ref[...] = ((q_ref[0].astype(F32) + q_ref[1].astype(F32)) + q_ref[2].astype(F32)) + q_ref[3].astype(F32)

    tw = 128
    return pl.pallas_call(
        body, name="chip_sum", grid=(w // tw,), in_specs=[pl.BlockSpec((ns, r, tw), lambda j: (0, 0, j))],
        out_specs=pl.BlockSpec((r, tw), lambda j: (0, j)), out_shape=jax.ShapeDtypeStruct((r, w), F32),
        compiler_params=_cparams("parallel"),
    )(q)


def _share_layers(gl):
    r, w = gl.shape

    def body(gl_ref, out_ref, send_sem, recv_sem, local_sem):
        x, y, c, _ = _place()
        mine = pltpu.make_async_copy(gl_ref, out_ref.at[c], local_sem)
        mine.start()
        cp = _remote(gl_ref, out_ref.at[c], send_sem, recv_sem, (x, y, 1 - c))
        cp.start()
        cp.wait_send()
        theirs = out_ref.at[1 - c]
        _remote(theirs, theirs, send_sem, recv_sem, (x, y, 1 - c)).wait_recv()
        mine.wait()

    hbm = pl.BlockSpec(memory_space=pl.ANY)
    return pl.pallas_call(
        body, name="share_layers", in_specs=[hbm], out_specs=hbm, out_shape=jax.ShapeDtypeStruct((2, r, w), gl.dtype),
        scratch_shapes=[pltpu.SemaphoreType.DMA, pltpu.SemaphoreType.DMA, pltpu.SemaphoreType.DMA],
    )(gl)


def _gather_small(pk, name):
    rows, w = pk.shape

    def body(pk_ref, all_ref, sum_ref, send_sems, recv_sems):
        x, y, c, _ = _place()
        me = 4 * x + 2 * y + c
        all_ref[me] = pk_ref[...]
        flips = [(fx, fy, fc) for fx in (0, 1) for fy in (0, 1) for fc in (0, 1)][1:]
        peers = [(x ^ fx, y ^ fy, c ^ fc) for fx, fy, fc in flips]
        sends = [_remote(pk_ref, all_ref.at[me], send_sems.at[k], recv_sems.at[k], peer) for k, peer in enumerate(peers)]
        for cp in sends:
            cp.start()
        for k, (px, py, pc) in enumerate(peers):
            slot = all_ref.at[4 * px + 2 * py + pc]
            _remote(slot, slot, send_sems.at[k], recv_sems.at[k], (px, py, pc)).wait_recv()
        for cp in sends:
            cp.wait_send()
        total = all_ref[0]
        for d in range(1, N_DEV):
            total = total + all_ref[d]
        sum_ref[...] = total

    vm = pl.BlockSpec(memory_space=pltpu.VMEM)
    return pl.pallas_call(
        body, name=name, in_specs=[vm], out_specs=[vm, vm],
        out_shape=[jax.ShapeDtypeStruct((N_DEV, rows, w), F32), jax.ShapeDtypeStruct((rows, w), F32)],
        scratch_shapes=[pltpu.SemaphoreType.DMA((7,)), pltpu.SemaphoreType.DMA((7,))],
    )(pk)


def _row_layout(c, nb, s):
    ch = jnp.swapaxes(c[:, :N_HEADS].reshape(nb, s, N_HEADS), 1, 2)
    ccol = jnp.broadcast_to(ch[..., None], (nb, N_HEADS, s, ATT))
    crow = jnp.broadcast_to(ch.reshape(nb, N_HEADS, s // ATT, 1, ATT), (nb, N_HEADS, s // ATT, 8, ATT))
    return ccol, crow


def _dil_bias(rel_bias, name):
    def body(rel_ref, t_ref, o_ref):
        for p in range(len(DIL_PATTERNS)):
            table = t_ref[p]

            def bucket(k, accs, table=table):
                return tuple(jnp.where(table == k, rel_ref[k, h], acc) for h, acc in enumerate(accs))

            accs = lax.fori_loop(0, REL_BUCKETS, bucket, tuple(jnp.full((BLK, 2 * BLK), NEG, F32) for _ in range(N_HEADS)))
            for h in range(N_HEADS):
                o_ref[p, h] = accs[h]

    vm = pl.BlockSpec(memory_space=pltpu.VMEM)
    return pl.pallas_call(
        body, name=name, in_specs=[pl.BlockSpec(memory_space=pltpu.SMEM), vm], out_specs=vm,
        out_shape=jax.ShapeDtypeStruct((len(DIL_PATTERNS), N_HEADS, BLK, 2 * BLK), F32),
        compiler_params=pltpu.CompilerParams(vmem_limit_bytes=VMEM_LIMIT),
    )(rel_bias, jnp.asarray(_bucket_table()))


def _layer_forward(x, wts, small, nb, s, tag):
    proj = _matmul(x, wts["w_in"], tm=512, tn=640, tk=D_MODEL, name=f"proj_{tag}")

    o_sb = _sb_fwd(proj, nb, s, f"sb_fwd_{tag}")

    dl = proj[:, 3 * CONV_W:6 * CONV_W].reshape(nb, s, 3 * CONV_W).astype(MXU_DTYPE)
    qkvp = jnp.stack([_to_residue(dl, dil) for _, dil in DIL_PATTERNS])
    bias = _dil_bias(small["rel_bias"], f"dil_bias_{tag}")
    o_p, lse_p = _dil_fwd(qkvp, bias, f"dil_fwd_{tag}")
    o_nat = jnp.stack([_from_residue(o_p[p], dil) for p, (_, dil) in enumerate(DIL_PATTERNS)])
    lse_nat = jnp.stack([_from_residue(lse_p[p], dil) for p, (_, dil) in enumerate(DIL_PATTERNS)])
    o_dl, lse_dl = _dil_combine(o_nat, lse_nat, f"dil_mix_{tag}")

    fb = jnp.zeros((8, BLK), F32).at[0, :N_HEADS].set(small["f_bias"])
    csum = _fox_gates_fwd(proj, fb, nb, s, f"fox_gates_{tag}")
    ccol, crow = _row_layout(csum, nb, s)
    o_fx, lse_fx = _fox_fwd(proj, ccol, crow, nb, s, f"fox_fwd_{tag}")

    cw = jnp.zeros((8, CONV_W), F32).at[:3].set(small["conv_w"])
    o_cv = _conv_fwd(proj, cw, nb, s, f"conv_fwd_{tag}")

    mixed = jnp.concatenate([o_sb, o_dl, o_fx, o_cv], axis=-1).astype(MXU_DTYPE)
    mix = _matmul(mixed, wts["w_out"], tm=512, tn=1024, tk=D_MODEL, name=f"out_proj_{tag}")
    pre1, x1 = _ln_fwd(x, mix, small["ln1_g"], small["ln1_b"], f"ln1_fwd_{tag}")
    gu = _matmul(x1, wts["w_gu"], tm=512, tn=512, tk=D_MODEL, name=f"ffn_in_{tag}")
    hid = _swiglu_fwd(gu, f"swiglu_fwd_{tag}")
    ffn = _matmul(hid, wts["w_down"], tm=512, tn=1024, tk=D_FF, name=f"ffn_out_{tag}")
    pre2, x2 = _ln_fwd(x1, ffn, small["ln2_g"], small["ln2_b"], f"ln2_fwd_{tag}")
    saved = dict(x=x, proj=proj, qkvp=qkvp, bias=bias, o_dl=o_dl, lse_dl=lse_dl, fb=fb, ccol=ccol, crow=crow,
                 o_fx=o_fx, lse_fx=lse_fx, cw=cw, mixed=mixed, pre1=pre1, x1=x1, gu=gu, hid=hid, pre2=pre2)
    return x2, saved


def _layer_backward(dx2, sv, wts, small, nb, s, tag):
    t = nb * s
    dpre2, dgb2 = _ln_bwd(dx2, sv["pre2"], small["ln2_g"], f"ln2_bwd_{tag}")
    dpre2_b = dpre2.astype(MXU_DTYPE)
    dhid = _matmul(dpre2_b, wts["w_down"], tm=512, tn=1408, tk=D_MODEL, name=f"ffn_out_dx_{tag}", trans_b=True)
    dw_down = _matmul(sv["hid"].T, dpre2_b, tm=704, tn=1024, tk=1024, name=f"ffn_out_dw_{tag}")
    dgu = _swiglu_bwd(dhid, sv["gu"], f"swiglu_bwd_{tag}")
    dx1 = _matmul(dgu, wts["w_gu"], tm=512, tn=1024, tk=1408, name=f"ffn_in_dx_{tag}", add=dpre2, add_scale=ALPHA, trans_b=True)
    dw_gu = _matmul(sv["x1"].astype(MXU_DTYPE).T, dgu, tm=512, tn=512, tk=1024, name=f"ffn_in_dw_{tag}")

    dpre1, dgb1 = _ln_bwd(dx1, sv["pre1"], small["ln1_g"], f"ln1_bwd_{tag}")
    dpre1_b = dpre1.astype(MXU_DTYPE)
    dmixed = _matmul(dpre1_b, wts["w_out"], tm=512, tn=1024, tk=D_MODEL, name=f"out_proj_dx_{tag}", trans_b=True)
    dw_out = _matmul(sv["mixed"].T, dpre1_b, tm=512, tn=1024, tk=1024, name=f"out_proj_dw_{tag}")
    proj = sv["proj"]

    dq_sb, dk_sb, dv_sb = _sb_bwd(proj, dmixed, nb, s, f"sb_bwd_{tag}")

    delta_dl = _delta_kernel(dmixed, sv["o_dl"], nb, s, f"dil_delta_{tag}")
    do_dl = dmixed[:, CONV_W:2 * CONV_W].reshape(nb, s, CONV_W).astype(MXU_DTYPE)
    dop = jnp.stack([_to_residue(do_dl, dil) for _, dil in DIL_PATTERNS])
    lsep = jnp.stack([_to_residue(sv["lse_dl"], dil) for _, dil in DIL_PATTERNS])
    deltap = jnp.stack([_to_residue(delta_dl, dil) for _, dil in DIL_PATTERNS])
    dqp, dkp, dvp, gbias = _dil_bwd(sv["qkvp"], dop, lsep, deltap, sv["bias"], f"dil_bwd_{tag}")
    unperm = lambda tp: sum(_from_residue(tp[p], dil) for p, (_, dil) in enumerate(DIL_PATTERNS)).reshape(t, CONV_W)
    dq_dl, dk_dl, dv_dl = unperm(dqp), unperm(dkp), unperm(dvp)
    drel = _bucket_reduce(gbias, jnp.asarray(_bucket_table()), f"rel_bias_grad_{tag}")

    dq_fx, dk_fx, dv_fx, dcol = _fox_bwd(proj, dmixed, sv["lse_fx"], sv["ccol"], sv["crow"], nb, s, f"fox_bwd_{tag}")
    dcs = -jnp.swapaxes(dcol[:, :, :, 0, :].reshape(nb, N_HEADS, s), 1, 2).reshape(t, N_HEADS)
    dcs = jnp.pad(dcs, ((0, 0), (0, BLK - N_HEADS)))
    dfx, dfb = _fox_gates_bwd(dcs, proj, sv["fb"], nb, s, f"fox_gates_bwd_{tag}")

    dgates, dcw = _conv_bwd(dmixed, proj, sv["cw"], nb, s, f"conv_bwd_{tag}")

    dproj = jnp.concatenate([dq_sb, dk_sb, dv_sb, dq_dl, dk_dl, dv_dl, dq_fx, dk_fx, dv_fx, dgates, dfx],
                            axis=-1).astype(MXU_DTYPE)
    dx = _matmul(dproj, wts["w_in"], tm=512, tn=1024, tk=640, name=f"proj_dx_{tag}", add=dpre1, add_scale=ALPHA, trans_b=True)
    dw_in = _matmul(sv["x"].astype(MXU_DTYPE).T, dproj, tm=512, tn=640, tk=1024, name=f"proj_dw_{tag}")

    grads = dict(w_in=dw_in[:, :PROJ], w_out=dw_out, w_gate=dw_gu[:, :D_FF], w_up=dw_gu[:, D_FF:], w_down=dw_down,
                 ln1_g=dgb1[0], ln1_b=dgb1[1], ln2_g=dgb2[0], ln2_b=dgb2[1], conv_w=dcw[:3], f_bias=dfb[0, :N_HEADS],
                 rel_bias=drel[:N_HEADS, :REL_BUCKETS].T)
    return dx, grads


def _local_step(x, target, full, small_all):
    nb, s, d = x.shape
    h = x.reshape(nb * s, d)
    saved = []
    for layer in range(DEPTH):
        h, sv = _layer_forward(h, full[layer], small_all[layer], nb, s, f"l{layer}")
        saved.append(sv)
    dy, lossp = _loss_kernel(h, target.reshape(nb * s, d), "loss")
    grads = [None] * DEPTH
    for layer in reversed(range(DEPTH)):
        dy, grads[layer] = _layer_backward(dy, saved[layer], full[layer], small_all[layer], nb, s, f"l{layer}")
    return lossp, dy.reshape(nb, s, d), grads


_SHARD_SHAPES = (("w_in", (D_MODEL, PROJ // N_CHIPS)), ("w_out", (D_MODEL // N_CHIPS, D_MODEL)),
                 ("w_gate", (D_MODEL, D_FF // N_CHIPS)), ("w_up", (D_MODEL, D_FF // N_CHIPS)),
                 ("w_down", (D_FF // N_CHIPS, D_MODEL)))


def _pack_shards(parts, lead):
    flat = [parts[name].reshape(*lead, -1, D_MODEL) for name, _ in _SHARD_SHAPES]
    return jnp.concatenate(flat, axis=-2)


def _unpack_shards(slab):
    lead = slab.shape[:-2]
    out, row = {}, 0
    for name, (r, c) in _SHARD_SHAPES:
        n = r * c // D_MODEL
        out[name] = slab[..., row:row + n, :].reshape(*lead, r, c)
        row += n
    assert row == PACK_ROWS
    return out


def _full_weights(gathered):
    sh = _unpack_shards(gathered)
    cols = lambda t: jnp.moveaxis(t, 0, 2).reshape(DEPTH, t.shape[2], -1)
    rows = lambda t: jnp.moveaxis(t, 0, 1).reshape(DEPTH, -1, t.shape[3])
    w_in = jnp.pad(cols(sh["w_in"]), ((0, 0), (0, 0), (0, PROJ_PAD - PROJ)))
    w_gu = jnp.concatenate([cols(sh["w_gate"]), cols(sh["w_up"])], axis=-1)
    w_out, w_down = rows(sh["w_out"]), rows(sh["w_down"])
    return [dict(w_in=w_in[l], w_out=w_out[l], w_gu=w_gu[l], w_down=w_down[l]) for l in range(DEPTH)]


def _shard_major(grads):
    by_cols = lambda g: jnp.moveaxis(g.reshape(g.shape[0], N_CHIPS, -1), 1, 0)
    by_rows = lambda g: g.reshape(N_CHIPS, -1, g.shape[1])
    per_layer = []
    for g in grads:
        parts = dict(w_in=by_cols(g["w_in"]), w_out=by_rows(g["w_out"]), w_gate=by_cols(g["w_gate"]),
                     w_up=by_cols(g["w_up"]), w_down=by_rows(g["w_down"]))
        per_layer.append(_pack_shards(parts, (N_CHIPS,)))
    return jnp.stack(per_layer)


_SMALL_LAYOUT = (("ln1_g", 0), ("ln1_b", 2), ("ln2_g", 4), ("ln2_b", 6), ("conv_w", 8))
_ROW_MISC = 10
_ROW_LOSS = 11


def _pack_small(per_layer, rel_bias, loss=None):
    pk = jnp.zeros((SMALL_ROWS, D_MODEL), F32)
    for name, row in _SMALL_LAYOUT:
        for l in range(DEPTH):
            v = per_layer[l][name].reshape(-1)
            pk = pk.at[row + l, :v.shape[0]].set(v)
    fb = jnp.concatenate([per_layer[l]["f_bias"] for l in range(DEPTH)])
    pk = pk.at[_ROW_MISC, :2 * N_HEADS].set(fb)
    pk = pk.at[_ROW_MISC, BLK:BLK + REL_BUCKETS * N_HEADS].set(rel_bias.reshape(-1))
    if loss is not None:
        pk = pk.at[_ROW_LOSS, 0].set(loss)
    return pk


def _unpack_small(pk, conv_cols):
    out = {}
    for name, row in _SMALL_LAYOUT:
        n = 3 * conv_cols if name == "conv_w" else D_MODEL
        v = pk[row:row + DEPTH, :n]
        out[name] = v.reshape(DEPTH, 3, conv_cols) if name == "conv_w" else v
    out["f_bias"] = pk[_ROW_MISC, :2 * N_HEADS].reshape(DEPTH, N_HEADS)
    out["rel_bias"] = pk[_ROW_MISC, BLK:BLK + REL_BUCKETS * N_HEADS].reshape(REL_BUCKETS, N_HEADS)
    return out


_WEIGHTS = ("w_in", "f_bias", "conv_w", "w_out", "rel_bias", "ln1_g", "ln1_b", "w_gate", "w_up", "w_down", "ln2_g", "ln2_b")
_BIG = ("w_in", "w_out", "w_gate", "w_up", "w_down")


def kernel(x, w_in, f_bias, conv_w, w_out, rel_bias, ln1_g, ln1_b, w_gate, w_up, w_down, ln2_g, ln2_b, loss_target, m_w_in, m_f_bias, m_conv_w, m_w_out, m_rel_bias, m_ln1_g, m_ln1_b, m_w_gate, m_w_up, m_w_down, m_ln2_g, m_ln2_b, v_w_in, v_f_bias, v_conv_w, v_w_out, v_rel_bias, v_ln1_g, v_ln1_b, v_w_gate, v_w_up, v_w_down, v_ln2_g, v_ln2_b):
    w = dict(w_in=w_in, f_bias=f_bias, conv_w=conv_w, w_out=w_out, rel_bias=rel_bias, ln1_g=ln1_g, ln1_b=ln1_b,
             w_gate=w_gate, w_up=w_up, w_down=w_down, ln2_g=ln2_g, ln2_b=ln2_b)
    m = dict(w_in=m_w_in, f_bias=m_f_bias, conv_w=m_conv_w, w_out=m_w_out, rel_bias=m_rel_bias, ln1_g=m_ln1_g,
             ln1_b=m_ln1_b, w_gate=m_w_gate, w_up=m_w_up, w_down=m_w_down, ln2_g=m_ln2_g, ln2_b=m_ln2_b)
    v = dict(w_in=v_w_in, f_bias=v_f_bias, conv_w=v_conv_w, w_out=v_w_out, rel_bias=v_rel_bias, ln1_g=v_ln1_g,
             ln1_b=v_ln1_b, w_gate=v_w_gate, w_up=v_w_up, w_down=v_w_down, ln2_g=v_ln2_g, ln2_b=v_ln2_b)
    chip = 2 * lax.axis_index("x") + lax.axis_index("y")
    core = lax.axis_index("c")
    conv_shard = CONV_W // N_CHIPS

    slab = _pack_shards({name: w[name] for name in _BIG}, (DEPTH,)).astype(MXU_DTYPE)
    full = _full_weights(_gather_weights(slab))
    cw_pk = jnp.zeros((8, D_MODEL), F32).at[0, :DEPTH * 3 * conv_shard].set(conv_w.reshape(-1))
    cw_all, _ = _gather_small(cw_pk, "gather_conv_w")
    cw_chips = cw_all[0::2, 0, :DEPTH * 3 * conv_shard].reshape(N_CHIPS, DEPTH, 3, conv_shard)
    conv_full = jnp.moveaxis(cw_chips, 0, 2).reshape(DEPTH, 3, CONV_W)
    small_all = [dict(f_bias=f_bias[l], conv_w=conv_full[l], rel_bias=rel_bias, ln1_g=ln1_g[l], ln1_b=ln1_b[l],
                      ln2_g=ln2_g[l], ln2_b=ln2_b[l]) for l in range(DEPTH)]

    lossp, grad_x, grads = _local_step(x, loss_target, full, small_all)

    g = _shard_major(grads)
    pair = _pair_sum(g, _swap_layers(g), core)
    both = _share_layers(_chip_sum(_scatter_shards(pair)))
    big_g = _unpack_shards(both)

    drel = grads[0]["rel_bias"] + grads[1]["rel_bias"]
    small_pk = _pack_small(grads, drel, lossp[0, 0])
    _, small_sum = _gather_small(small_pk, "gather_small_grads")
    loss = small_sum[_ROW_LOSS, 0]
    small_g = _unpack_small(small_sum, CONV_W)
    small_g["conv_w"] = lax.dynamic_slice_in_dim(small_g["conv_w"], chip * conv_shard, conv_shard, axis=2)

    out_g, out_d, out_m, out_v = dict(small_g), {}, {}, {}
    for name in _BIG:
        out_g[name] = big_g[name]
        out_d[name], out_m[name], out_v[name] = _adamw(w[name], big_g[name], m[name], v[name], f"adamw_{name}")
    per_layer = lambda src: [{name: src[name][l] for name in ("ln1_g", "ln1_b", "ln2_g", "ln2_b", "conv_w", "f_bias")}
                             for l in range(DEPTH)]
    packs = [_pack_small(per_layer(src), src["rel_bias"])[None] for src in (w, small_g, m, v)]
    for dst, pk in zip((out_d, out_m, out_v), _adamw(*packs, "adamw_small")):
        dst.update(_unpack_small(pk[0], conv_shard))

    return (loss, grad_x, *[out_g[n] for n in _WEIGHTS], *[out_d[n] for n in _WEIGHTS],
            *[out_m[n] for n in _WEIGHTS], *[out_v[n] for n in _WEIGHTS])
```

```python
import functools
import math

import numpy as np
import jax
import jax.numpy as jnp
from jax import lax
from jax.experimental import pallas as pl
from jax.experimental.pallas import tpu as pltpu

F32 = jnp.float32
BF16 = jnp.bfloat16
MXU_DTYPE = BF16

D_MODEL = 1024
HEAD_DIM = 64
N_HEADS = 4
BLK = 128
ATT = 256
CONV_W = 256
PROJ = 3076
PROJ_PAD = 3200
D_FF = 2816
DEPTH = 2
ALPHA = (2 * DEPTH) ** 0.25
LN_EPS = 1e-5
NEG = -1e30
DIL_PATTERNS = ((128, 1), (512, 4), (2048, 16))
REL_BUCKETS = 32
N_CHIPS = 4
N_DEV = 8
PACK_ROWS = 3137
SMALL_ROWS = 16

ADAM_LR = 0.001
ADAM_B1 = 0.9
ADAM_B2 = 0.999
ADAM_EPS = 1e-08
ADAM_WD = 0.01
ADAM_STEP = 10

VMEM_LIMIT = 48 * 2 ** 20
MESH = pl.DeviceIdType.MESH


def _cparams(*sem):
    return pltpu.CompilerParams(dimension_semantics=tuple(sem), vmem_limit_bytes=VMEM_LIMIT)


def _dot(a, b):
    return jnp.dot(a.astype(MXU_DTYPE), b.astype(MXU_DTYPE), preferred_element_type=F32)


def _dot_nt(a, b):
    return lax.dot_general(a.astype(MXU_DTYPE), b.astype(MXU_DTYPE), (((1,), (1,)), ((), ())),
                           preferred_element_type=F32)


def _dot_tn(a, b):
    return lax.dot_general(a.astype(MXU_DTYPE), b.astype(MXU_DTYPE), (((0,), (0,)), ((), ())),
                           preferred_element_type=F32)


def _split_dot(x, ones, passes):
    acc, rest = None, x
    for p in range(passes):
        piece = rest.astype(MXU_DTYPE)
        part = jnp.dot(piece, ones, preferred_element_type=F32)
        acc = part if acc is None else acc + part
        if p + 1 < passes:
            rest = rest - piece.astype(F32)
    return acc


def _split_dot_lhs(ones, x, passes):
    acc, rest = None, x
    for p in range(passes):
        piece = rest.astype(MXU_DTYPE)
        part = jnp.dot(ones, piece, preferred_element_type=F32)
        acc = part if acc is None else acc + part
        if p + 1 < passes:
            rest = rest - piece.astype(F32)
    return acc


def _iota2(shape, axis):
    return lax.broadcasted_iota(jnp.int32, shape, axis)


_TILES = {"proj": (1024, 640, 1024), "out_proj": (1024, 1024, 1024), "ffn_in": (1024, 1408, 1024),
          "ffn_out": (1024, 1024, 2816), "ffn_out_dx": (1024, 1408, 1024), "ffn_out_dw": (1408, 1024, 2048),
          "ffn_in_dx": (1024, 1024, 1408), "ffn_in_dw": (1024, 1408, 2048), "out_proj_dx": (1024, 1024, 1024),
          "out_proj_dw": (1024, 1024, 2048), "proj_dx": (1024, 1024, 640), "proj_dw": (1024, 640, 2048)}


def _matmul(a, b, kind, tag, *, out_dtype=F32, add=None, add_scale=1.0, trans_b=False):
    m, k = a.shape
    n = b.shape[0] if trans_b else b.shape[1]
    tm, tn, tk = _TILES[kind]
    tm, tk, name = min(tm, m), min(tk, k), f"{kind}_{tag}"
    assert m % tm == 0 and n % tn == 0 and k % tk == 0, (a.shape, b.shape, tm, tn, tk)
    nk = k // tk

    def body(*refs):
        if add is None:
            a_ref, b_ref, o_ref = refs[:3]
            c_ref, scr = None, refs[3:]
        else:
            a_ref, b_ref, c_ref, o_ref = refs[:4]
            scr = refs[4:]
        part = _dot_nt(a_ref[...], b_ref[...]) if trans_b else _dot(a_ref[...], b_ref[...])

        def finish(acc):
            if c_ref is not None:
                acc = acc + add_scale * c_ref[...]
            o_ref[...] = acc.astype(out_dtype)

        if nk == 1:
            finish(part)
        else:
            acc_ref = scr[0]
            kk = pl.program_id(2)

            @pl.when(kk == 0)
            def _():
                acc_ref[...] = part

            @pl.when(kk > 0)
            def _():
                acc_ref[...] += part

            @pl.when(kk == nk - 1)
            def _():
                finish(acc_ref[...])

    b_spec = pl.BlockSpec((tn, tk), lambda i, j, kk: (j, kk)) if trans_b else pl.BlockSpec((tk, tn), lambda i, j, kk: (kk, j))
    in_specs = [pl.BlockSpec((tm, tk), lambda i, j, kk: (i, kk)), b_spec]
    operands = [a, b]
    if add is not None:
        in_specs.append(pl.BlockSpec((tm, tn), lambda i, j, kk: (i, j)))
        operands.append(add)
    return pl.pallas_call(
        body, name=name, grid=(m // tm, n // tn, nk), in_specs=in_specs,
        out_specs=pl.BlockSpec((tm, tn), lambda i, j, kk: (i, j)),
        out_shape=jax.ShapeDtypeStruct((m, n), out_dtype),
        scratch_shapes=[pltpu.VMEM((tm, tn), F32)] if nk > 1 else [],
        compiler_params=_cparams("parallel", "parallel", "arbitrary"),
    )(*operands)


def _ln_stats(pre):
    mu = jnp.mean(pre, axis=-1, keepdims=True)
    xc = pre - mu
    var = jnp.mean(xc * xc, axis=-1, keepdims=True)
    rstd = lax.rsqrt(var + LN_EPS)
    return xc * rstd, rstd


def _ln_fwd(xin, branch, g, b, name):
    t, d = xin.shape
    tile = 256

    def body(x_ref, br_ref, g_ref, b_ref, pre_ref, y_ref):
        pre = ALPHA * x_ref[...] + br_ref[...]
        xhat, _ = _ln_stats(pre)
        pre_ref[...] = pre
        y_ref[...] = xhat * g_ref[...] + b_ref[...]

    row = pl.BlockSpec((tile, d), lambda i: (i, 0))
    vec = pl.BlockSpec((1, d), lambda i: (0, 0))
    return pl.pallas_call(
        body, name=name, grid=(t // tile,), in_specs=[row, row, vec, vec], out_specs=[row, row],
        out_shape=[jax.ShapeDtypeStruct((t, d), F32)] * 2, compiler_params=_cparams("parallel"),
    )(xin, branch, g.reshape(1, d), b.reshape(1, d))


def _ln_bwd(dy, pre, g, name):
    t, d = dy.shape
    tile = 256

    def body(dy_ref, pre_ref, g_ref, dpre_ref, dgb_ref):
        dyv = dy_ref[...]
        xhat, rstd = _ln_stats(pre_ref[...])
        dxh = dyv * g_ref[...]
        m1 = jnp.mean(dxh, axis=-1, keepdims=True)
        m2 = jnp.mean(dxh * xhat, axis=-1, keepdims=True)
        dpre_ref[...] = rstd * (dxh - m1 - xhat * m2)

        @pl.when(pl.program_id(0) == 0)
        def _():
            dgb_ref[...] = jnp.zeros_like(dgb_ref)

        dgb_ref[0:1, :] += jnp.sum(dyv * xhat, axis=0, keepdims=True)
        dgb_ref[1:2, :] += jnp.sum(dyv, axis=0, keepdims=True)

    row = pl.BlockSpec((tile, d), lambda i: (i, 0))
    return pl.pallas_call(
        body, name=name, grid=(t // tile,), in_specs=[row, row, pl.BlockSpec((1, d), lambda i: (0, 0))],
        out_specs=[row, pl.BlockSpec((8, d), lambda i: (0, 0))],
        out_shape=[jax.ShapeDtypeStruct((t, d), F32), jax.ShapeDtypeStruct((8, d), F32)],
        compiler_params=_cparams("arbitrary"),
    )(dy, pre, g.reshape(1, d))


def _swiglu_fwd(gu, name):
    t = gu.shape[0]
    tile = 256

    def body(gu_ref, h_ref):
        gate = gu_ref[:, :D_FF]
        up = gu_ref[:, D_FF:]
        h_ref[...] = (gate * (1.0 / (1.0 + jnp.exp(-gate))) * up).astype(h_ref.dtype)

    return pl.pallas_call(
        body, name=name, grid=(t // tile,), in_specs=[pl.BlockSpec((tile, 2 * D_FF), lambda i: (i, 0))],
        out_specs=pl.BlockSpec((tile, D_FF), lambda i: (i, 0)),
        out_shape=jax.ShapeDtypeStruct((t, D_FF), MXU_DTYPE), compiler_params=_cparams("parallel"),
    )(gu)


def _swiglu_bwd(dh, gu, name):
    t = gu.shape[0]
    tile = 256

    def body(dh_ref, gu_ref, dgu_ref):
        gate = gu_ref[:, :D_FF]
        up = gu_ref[:, D_FF:]
        dhv = dh_ref[...]
        sig = 1.0 / (1.0 + jnp.exp(-gate))
        dgu_ref[:, :D_FF] = (dhv * up * sig * (1.0 + gate * (1.0 - sig))).astype(dgu_ref.dtype)
        dgu_ref[:, D_FF:] = (dhv * gate * sig).astype(dgu_ref.dtype)

    return pl.pallas_call(
        body, name=name, grid=(t // tile,),
        in_specs=[pl.BlockSpec((tile, D_FF), lambda i: (i, 0)), pl.BlockSpec((tile, 2 * D_FF), lambda i: (i, 0))],
        out_specs=pl.BlockSpec((tile, 2 * D_FF), lambda i: (i, 0)),
        out_shape=jax.ShapeDtypeStruct((t, 2 * D_FF), MXU_DTYPE), compiler_params=_cparams("parallel"),
    )(dh, gu)


def _loss_kernel(y, target, name):
    t, d = y.shape
    tile = 512

    def body(y_ref, t_ref, dy_ref, l_ref):
        err = y_ref[...] - t_ref[...]
        dy_ref[...] = err * (1.0 / d)

        @pl.when(pl.program_id(0) == 0)
        def _():
            l_ref[...] = jnp.zeros_like(l_ref)

        l_ref[...] += jnp.sum(err * err) * (0.5 / d)

    row = pl.BlockSpec((tile, d), lambda i: (i, 0))
    return pl.pallas_call(
        body, name=name, grid=(t // tile,), in_specs=[row, row],
        out_specs=[row, pl.BlockSpec((8, 128), lambda i: (0, 0))],
        out_shape=[jax.ShapeDtypeStruct((t, d), F32), jax.ShapeDtypeStruct((8, 128), F32)],
        compiler_params=_cparams("arbitrary"),
    )(y, target)


def _adamw(w, g, m, v, name):
    nl, r, c = w.shape
    tr = r
    for cand in (256, 352, 128, 64, 16, 8):
        if r % cand == 0:
            tr = cand
            break

    def body(w_ref, g_ref, m_ref, v_ref, d_ref, nm_ref, nv_ref):
        gv = g_ref[...]
        nm = ADAM_B1 * m_ref[...] + (1.0 - ADAM_B1) * gv
        nv = ADAM_B2 * v_ref[...] + (1.0 - ADAM_B2) * (gv * gv)
        m_hat = nm / (1.0 - ADAM_B1 ** ADAM_STEP)
        v_hat = nv / (1.0 - ADAM_B2 ** ADAM_STEP)
        d_ref[...] = -ADAM_LR * (m_hat / (jnp.sqrt(v_hat) + ADAM_EPS) + ADAM_WD * w_ref[...])
        nm_ref[...] = nm
        nv_ref[...] = nv

    blk = pl.BlockSpec((1, tr, c), lambda l, i: (l, i, 0))
    return pl.pallas_call(
        body, name=name, grid=(nl, r // tr), in_specs=[blk] * 4, out_specs=[blk] * 3,
        out_shape=[jax.ShapeDtypeStruct(w.shape, F32)] * 3, compiler_params=_cparams("parallel", "parallel"),
    )(w, g, m, v)


def _shift_down(u, k, rows):
    return jnp.where(rows >= k, pltpu.roll(u, k, 0), 0.0)


def _shift_up(u, k, rows, s):
    return jnp.where(rows < s - k, pltpu.roll(u, s - k, 0), 0.0)


def _conv_fwd(proj, conv_w, nb, s, name):
    def body(b_ref, c_ref, h_ref, w_ref, o_ref):
        rows = _iota2((s, CONV_W), 0)
        u = c_ref[...] * h_ref[...]
        y = w_ref[2:3, :] * u + w_ref[1:2, :] * _shift_down(u, 1, rows) + w_ref[0:1, :] * _shift_down(u, 2, rows)
        o_ref[...] = b_ref[...] * y

    col = lambda j: pl.BlockSpec((s, CONV_W), lambda b: (b, j))
    return pl.pallas_call(
        body, name=name, grid=(nb,),
        in_specs=[col(9), col(10), col(11), pl.BlockSpec((8, CONV_W), lambda b: (0, 0))],
        out_specs=pl.BlockSpec((s, CONV_W), lambda b: (b, 0)),
        out_shape=jax.ShapeDtypeStruct((nb * s, CONV_W), F32), compiler_params=_cparams("parallel"),
    )(proj, proj, proj, conv_w)


def _conv_bwd(dmixed, proj, conv_w, nb, s, name):
    def body(do_ref, b_ref, c_ref, h_ref, w_ref, dg_ref, dw_ref):
        rows = _iota2((s, CONV_W), 0)
        cg, hg, bg, dout = c_ref[...], h_ref[...], b_ref[...], do_ref[...]
        u = cg * hg
        u1 = _shift_down(u, 1, rows)
        u2 = _shift_down(u, 2, rows)
        y = w_ref[2:3, :] * u + w_ref[1:2, :] * u1 + w_ref[0:1, :] * u2
        dy = dout * bg
        du = w_ref[2:3, :] * dy + w_ref[1:2, :] * _shift_up(dy, 1, rows, s) + w_ref[0:1, :] * _shift_up(dy, 2, rows, s)
        dg_ref[:, 0:CONV_W] = dout * y
        dg_ref[:, CONV_W:2 * CONV_W] = du * hg
        dg_ref[:, 2 * CONV_W:3 * CONV_W] = du * cg

        @pl.when(pl.program_id(0) == 0)
        def _():
            dw_ref[...] = jnp.zeros_like(dw_ref)

        dw_ref[0:1, :] += jnp.sum(dy * u2, axis=0, keepdims=True)
        dw_ref[1:2, :] += jnp.sum(dy * u1, axis=0, keepdims=True)
        dw_ref[2:3, :] += jnp.sum(dy * u, axis=0, keepdims=True)

    col = lambda j: pl.BlockSpec((s, CONV_W), lambda b: (b, j))
    return pl.pallas_call(
        body, name=name, grid=(nb,),
        in_specs=[col(3), col(9), col(10), col(11), pl.BlockSpec((8, CONV_W), lambda b: (0, 0))],
        out_specs=[pl.BlockSpec((s, 3 * CONV_W), lambda b: (b, 0)), pl.BlockSpec((8, CONV_W), lambda b: (0, 0))],
        out_shape=[jax.ShapeDtypeStruct((nb * s, 3 * CONV_W), F32), jax.ShapeDtypeStruct((8, CONV_W), F32)],
        compiler_params=_cparams("arbitrary"),
    )(dmixed, proj, proj, proj, conv_w)


def _col_spec(s, base):
    return pl.BlockSpec((s, BLK), lambda b, p: (b, base + p))


def _rows(i):
    return pl.ds(pl.multiple_of(i * ATT, ATT), ATT)


def _rows128(i):
    return pl.ds(pl.multiple_of(i * BLK, BLK), BLK)


def _log_sigmoid_parts(z):
    e = jnp.exp(-jnp.abs(z))
    l1p = jnp.log(1.0 + e)
    lb = jnp.minimum(z, 0.0) - l1p
    return lb, lb - z, e


def _head_masks():
    lane = _iota2((1, BLK), 1)
    return [(lane >= h * HEAD_DIM) & (lane < (h + 1) * HEAD_DIM) for h in range(2)]


def _split_heads(ref, scr, sels):
    for h, sel in enumerate(sels):
        scr[h] = jnp.where(sel, ref[...], 0.0).astype(MXU_DTYPE)


def _sb_fwd(proj, nb, s, name):
    nblk = s // ATT

    def body(q_ref, k_ref, v_ref, o_ref, km, vm):
        sels = _head_masks()
        _split_heads(k_ref, km, sels)
        _split_heads(v_ref, vm, sels)
        rows = _iota2((ATT, ATT), 0)
        cols = _iota2((ATT, ATT), 1)
        later = (rows > cols).astype(MXU_DTYPE)

        def qblock(i, _):
            qi = (q_ref[_rows(i), :] * 0.125).astype(MXU_DTYPE)

            def kblock(t, state):
                carries, acc = state
                j = i - t
                strict = (cols + (j - i) * ATT) < rows
                out = []
                for h in range(2):
                    z = _dot_nt(qi, km[h, _rows(j), :])
                    lb, lr, _ = _log_sigmoid_parts(z)
                    lr = jnp.where(strict, lr, 0.0)
                    tail = _split_dot(lr, later, 2) + carries[h]
                    a = jnp.where(strict, jnp.exp(lb + tail), 0.0)
                    acc = acc + _dot(a, vm[h, _rows(j), :])
                    out.append(carries[h] + jnp.sum(lr, axis=-1, keepdims=True))
                return tuple(out), acc

            init = ((jnp.zeros((ATT, 1), F32),) * 2, jnp.zeros((ATT, BLK), F32))
            _, acc = lax.fori_loop(0, i + 1, kblock, init)
            o_ref[_rows(i), :] = acc
            return 0

        lax.fori_loop(0, nblk, qblock, 0)

    return pl.pallas_call(
        body, name=name, grid=(nb, 2), in_specs=[_col_spec(s, 0), _col_spec(s, 2), _col_spec(s, 4)],
        out_specs=_col_spec(s, 0), out_shape=jax.ShapeDtypeStruct((nb * s, 2 * BLK), F32),
        scratch_shapes=[pltpu.VMEM((2, s, BLK), MXU_DTYPE)] * 2, compiler_params=_cparams("parallel", "parallel"),
    )(proj, proj, proj)


def _sb_bwd(proj, dmixed, nb, s, name):
    nblk = s // ATT

    def body(q_ref, k_ref, v_ref, do_ref, dq_ref, dk_ref, dv_ref, km, vm, a_scr, dl_scr, beta_scr):
        sels = _head_masks()
        _split_heads(k_ref, km, sels)
        _split_heads(v_ref, vm, sels)
        rows = _iota2((ATT, ATT), 0)
        cols = _iota2((ATT, ATT), 1)
        later = (rows > cols).astype(MXU_DTYPE)
        earlier = (rows < cols).astype(MXU_DTYPE)
        dk_ref[...] = jnp.zeros_like(dk_ref)
        dv_ref[...] = jnp.zeros_like(dv_ref)

        def qblock(i, _):
            qi = (q_ref[_rows(i), :] * 0.125).astype(MXU_DTYPE)
            doi = do_ref[_rows(i), :].astype(MXU_DTYPE)
            qm = [jnp.where(sel, qi, 0.0) for sel in sels]
            dom = [jnp.where(sel, doi, 0.0) for sel in sels]

            def first(t, carries):
                j = i - t
                strict = (cols + (j - i) * ATT) < rows
                out = []
                for h in range(2):
                    z = _dot_nt(qi, km[h, _rows(j), :])
                    lb, lr, e = _log_sigmoid_parts(z)
                    lr = jnp.where(strict, lr, 0.0)
                    tail = _split_dot(lr, later, 2) + carries[h]
                    a = jnp.where(strict, jnp.exp(lb + tail), 0.0)
                    a_scr[h, j] = a
                    dl_scr[h, j] = a * _dot_nt(doi, vm[h, _rows(j), :])
                    beta_scr[h, j] = jnp.exp(lb)
                    out.append(carries[h] + jnp.sum(lr, axis=-1, keepdims=True))
                return tuple(out)

            lax.fori_loop(0, i + 1, first, (jnp.zeros((ATT, 1), F32),) * 2)

            def second(j, state):
                csums, dq = state
                strict = (cols + (j - i) * ATT) < rows
                out = []
                for h in range(2):
                    dl = dl_scr[h, j]
                    beta = beta_scr[h, j]
                    before = _split_dot(dl, earlier, 2) + csums[h]
                    dz = jnp.where(strict, dl * (1.0 - beta) - beta * before, 0.0).astype(MXU_DTYPE)
                    dq = dq + _dot(dz, km[h, _rows(j), :])
                    dk_ref[_rows(j), :] += _dot_tn(dz, qm[h])
                    dv_ref[_rows(j), :] += _dot_tn(a_scr[h, j], dom[h])
                    out.append(csums[h] + jnp.sum(dl, axis=-1, keepdims=True))
                return tuple(out), dq

            init = ((jnp.zeros((ATT, 1), F32),) * 2, jnp.zeros((ATT, BLK), F32))
            _, dq = lax.fori_loop(0, i + 1, second, init)
            dq_ref[_rows(i), :] = dq * 0.125
            return 0

        lax.fori_loop(0, nblk, qblock, 0)

    out = _col_spec(s, 0)
    return pl.pallas_call(
        body, name=name, grid=(nb, 2),
        in_specs=[_col_spec(s, 0), _col_spec(s, 2), _col_spec(s, 4), out], out_specs=[out] * 3,
        out_shape=[jax.ShapeDtypeStruct((nb * s, 2 * BLK), F32)] * 3,
        scratch_shapes=[pltpu.VMEM((2, s, BLK), MXU_DTYPE)] * 2 + [pltpu.VMEM((2, nblk, ATT, ATT), F32)] * 3,
        compiler_params=_cparams("parallel", "parallel"),
    )(proj, proj, proj, dmixed)


def _pair_spec(s, width):
    return pl.BlockSpec((None, 2, s, width), lambda b, p: (b, p, 0, 0))


def _fox_fwd(proj, ccol, crow, nb, s, name):
    nblk = s // ATT

    def body(q_ref, k_ref, v_ref, cc_ref, cr_ref, o_ref, lse_ref, km, vm):
        sels = _head_masks()
        _split_heads(k_ref, km, sels)
        _split_heads(v_ref, vm, sels)
        rows = _iota2((ATT, ATT), 0)
        cols = _iota2((ATT, ATT), 1)

        def qblock(i, _):
            qi = (q_ref[_rows(i), :] * 0.125).astype(MXU_DTYPE)
            ci = [cc_ref[h, _rows(i), :] for h in range(2)]

            def kblock(j, state):
                ms, ls, acc = state
                causal = (cols + (j - i) * ATT) <= rows
                new_m, new_l, scales, parts = [], [], [], []
                for h in range(2):
                    z = _dot_nt(qi, km[h, _rows(j), :]) + (ci[h] - cr_ref[h, j][0:1, :])
                    z = jnp.where(causal, z, NEG)
                    m_new = jnp.maximum(ms[h], jnp.max(z, axis=-1, keepdims=True))
                    p = jnp.exp(z - m_new)
                    scale = jnp.exp(ms[h] - m_new)
                    new_m.append(m_new)
                    new_l.append(scale * ls[h] + jnp.sum(p, axis=-1, keepdims=True))
                    scales.append(scale)
                    parts.append(_dot(p, vm[h, _rows(j), :]))
                acc = jnp.where(sels[0], scales[0], scales[1]) * acc + parts[0] + parts[1]
                return tuple(new_m), tuple(new_l), acc

            init = ((jnp.full((ATT, 1), NEG, F32),) * 2, (jnp.zeros((ATT, 1), F32),) * 2, jnp.zeros((ATT, BLK), F32))
            ms, ls, acc = lax.fori_loop(0, i + 1, kblock, init)
            o_ref[_rows(i), :] = acc / jnp.where(sels[0], ls[0], ls[1])
            for h in range(2):
                lse_ref[h, _rows(i), :] = jnp.broadcast_to(ms[h] + jnp.log(ls[h]), (ATT, ATT))
            return 0

        lax.fori_loop(0, nblk, qblock, 0)

    crow_spec = pl.BlockSpec((None, 2, nblk, 8, ATT), lambda b, p: (b, p, 0, 0, 0))
    return pl.pallas_call(
        body, name=name, grid=(nb, 2),
        in_specs=[_col_spec(s, 12), _col_spec(s, 14), _col_spec(s, 16), _pair_spec(s, ATT), crow_spec],
        out_specs=[_col_spec(s, 0), _pair_spec(s, ATT)],
        out_shape=[jax.ShapeDtypeStruct((nb * s, 2 * BLK), F32), jax.ShapeDtypeStruct((nb, N_HEADS, s, ATT), F32)],
        scratch_shapes=[pltpu.VMEM((2, s, BLK), MXU_DTYPE)] * 2, compiler_params=_cparams("parallel", "parallel"),
    )(proj, proj, proj, ccol, crow)


def _fox_bwd(proj, dmixed, lse, ccol, crow, nb, s, name):
    nblk = s // ATT

    def body(q_ref, k_ref, v_ref, do_ref, lse_ref, cc_ref, cr_ref, dq_ref, dk_ref, dv_ref, dc_ref, km, vm):
        sels = _head_masks()
        _split_heads(k_ref, km, sels)
        _split_heads(v_ref, vm, sels)
        rows = _iota2((ATT, ATT), 0)
        cols = _iota2((ATT, ATT), 1)
        dk_ref[...] = jnp.zeros_like(dk_ref)
        dv_ref[...] = jnp.zeros_like(dv_ref)
        dc_ref[...] = jnp.zeros_like(dc_ref)

        def qblock(i, _):
            qi = (q_ref[_rows(i), :] * 0.125).astype(MXU_DTYPE)
            doi = do_ref[_rows(i), :].astype(MXU_DTYPE)
            qm = [jnp.where(sel, qi, 0.0) for sel in sels]
            dom = [jnp.where(sel, doi, 0.0) for sel in sels]
            ci = [cc_ref[h, _rows(i), :] for h in range(2)]
            lsei = [lse_ref[h, _rows(i), :] for h in range(2)]

            def probs(j, h):
                z = _dot_nt(qi, km[h, _rows(j), :]) + (ci[h] - cr_ref[h, j][0:1, :])
                p = jnp.where((cols + (j - i) * ATT) <= rows, jnp.exp(z - lsei[h]), 0.0)
                return p, _dot_nt(doi, vm[h, _rows(j), :])

            def row_term(j, accs):
                out = []
                for h in range(2):
                    p, dp = probs(j, h)
                    out.append(accs[h] + jnp.sum(p * dp, axis=-1, keepdims=True))
                return tuple(out)

            di = lax.fori_loop(0, i + 1, row_term, (jnp.zeros((ATT, 1), F32),) * 2)

            def kblock(j, dq):
                for h in range(2):
                    p, dp = probs(j, h)
                    ds = p * (dp - di[h])
                    dc_ref[h, j] += jnp.broadcast_to(jnp.sum(ds, axis=0, keepdims=True), (8, ATT))
                    ds = ds.astype(MXU_DTYPE)
                    dk_ref[_rows(j), :] += _dot_tn(ds, qm[h])
                    dv_ref[_rows(j), :] += _dot_tn(p, dom[h])
                    dq = dq + _dot(ds, km[h, _rows(j), :])
                return dq

            dq = lax.fori_loop(0, i + 1, kblock, jnp.zeros((ATT, BLK), F32))
            dq_ref[_rows(i), :] = dq * 0.125
            return 0

        lax.fori_loop(0, nblk, qblock, 0)

    crow_spec = pl.BlockSpec((None, 2, nblk, 8, ATT), lambda b, p: (b, p, 0, 0, 0))
    wide, cols_out = _pair_spec(s, ATT), _col_spec(s, 0)
    return pl.pallas_call(
        body, name=name, grid=(nb, 2),
        in_specs=[_col_spec(s, 12), _col_spec(s, 14), _col_spec(s, 16), _col_spec(s, 4), wide, wide, crow_spec],
        out_specs=[cols_out, cols_out, cols_out, crow_spec],
        out_shape=[jax.ShapeDtypeStruct((nb * s, 2 * BLK), F32)] * 3 + [jax.ShapeDtypeStruct((nb, N_HEADS, nblk, 8, ATT), F32)],
        scratch_shapes=[pltpu.VMEM((2, s, BLK), MXU_DTYPE)] * 2, compiler_params=_cparams("parallel", "parallel"),
    )(proj, proj, proj, dmixed, lse, ccol, crow)


def _fox_gates_fwd(proj, f_bias, nb, s, name):
    chunk = 256

    def body(f_ref, b_ref, c_ref):
        lower = (_iota2((chunk, chunk), 0) >= _iota2((chunk, chunk), 1)).astype(MXU_DTYPE)
        carry = jnp.zeros((1, BLK), F32)
        for n in range(s // chunk):
            rows = pl.ds(n * chunk, chunk)
            lf, _, _ = _log_sigmoid_parts(f_ref[rows, :] + b_ref[0:1, :])
            c = _split_dot_lhs(lower, lf, 3) + carry
            c_ref[rows, :] = c
            carry = c[chunk - 1:chunk, :]

    return pl.pallas_call(
        body, name=name, grid=(nb,),
        in_specs=[pl.BlockSpec((s, BLK), lambda b: (b, (PROJ_PAD - BLK) // BLK)), pl.BlockSpec((8, BLK), lambda b: (0, 0))],
        out_specs=pl.BlockSpec((s, BLK), lambda b: (b, 0)),
        out_shape=jax.ShapeDtypeStruct((nb * s, BLK), F32), compiler_params=_cparams("parallel"),
    )(proj, f_bias)


def _fox_gates_bwd(dc, proj, f_bias, nb, s, name):
    chunk = 256

    def body(dc_ref, f_ref, b_ref, df_ref, db_ref):
        upper = (_iota2((chunk, chunk), 0) <= _iota2((chunk, chunk), 1)).astype(MXU_DTYPE)
        carry = jnp.zeros((1, BLK), F32)
        total = jnp.zeros((1, BLK), F32)
        for n in reversed(range(s // chunk)):
            rows = pl.ds(n * chunk, chunk)
            dlf = _split_dot_lhs(upper, dc_ref[rows, :], 3) + carry
            carry = dlf[0:1, :]
            pre = f_ref[rows, :] + b_ref[0:1, :]
            e = jnp.exp(-jnp.abs(pre))
            df = dlf * (jnp.where(pre >= 0.0, e, 1.0) / (1.0 + e))
            df_ref[rows, :] = df
            total = total + jnp.sum(df, axis=0, keepdims=True)

        @pl.when(pl.program_id(0) == 0)
        def _():
            db_ref[...] = jnp.zeros_like(db_ref)

        db_ref[0:1, :] += total

    return pl.pallas_call(
        body, name=name, grid=(nb,),
        in_specs=[pl.BlockSpec((s, BLK), lambda b: (b, 0)), pl.BlockSpec((s, BLK), lambda b: (b, (PROJ_PAD - BLK) // BLK)),
                  pl.BlockSpec((8, BLK), lambda b: (0, 0))],
        out_specs=[pl.BlockSpec((s, BLK), lambda b: (b, 0)), pl.BlockSpec((8, BLK), lambda b: (0, 0))],
        out_shape=[jax.ShapeDtypeStruct((nb * s, BLK), F32), jax.ShapeDtypeStruct((8, BLK), F32)],
        compiler_params=_cparams("arbitrary"),
    )(dc, proj, f_bias)


def _delta_kernel(dmixed, o, nb, s, name):
    def body(do_ref, o_ref, d_ref):
        prod = do_ref[...] * o_ref[...]
        for h, sel in enumerate(_head_masks()):
            d_ref[h] = jnp.broadcast_to(jnp.sum(jnp.where(sel, prod, 0.0), axis=-1, keepdims=True), (s, BLK))

    return pl.pallas_call(
        body, name=name, grid=(nb, 2), in_specs=[_col_spec(s, 2), _col_spec(s, 0)], out_specs=_pair_spec(s, BLK),
        out_shape=jax.ShapeDtypeStruct((nb, N_HEADS, s, BLK), F32), compiler_params=_cparams("parallel", "parallel"),
    )(dmixed, o)


def _t5_bucket_np(dist):
    max_exact = REL_BUCKETS // 2
    nf = np.maximum(dist, 1).astype(np.float32)
    large = max_exact + (np.log(nf / max_exact) / math.log(2048 / max_exact) * (REL_BUCKETS - max_exact)).astype(np.int32)
    large = np.minimum(large, REL_BUCKETS - 1)
    return np.where(dist < max_exact, dist, large)


def _bucket_table():
    qi = np.arange(BLK)[:, None]
    kj = np.arange(2 * BLK)[None, :]
    dist = qi + BLK - kj
    tables = []
    for window, dil in DIL_PATTERNS:
        in_band = (dist >= 0) & (dist <= window // dil)
        tables.append(np.where(in_band, _t5_bucket_np(np.maximum(dist, 0) * dil), -1).astype(np.int32))
    return np.stack(tables)


def _to_residue(t, dil):
    if dil == 1:
        return t
    *lead, s, e = t.shape
    return jnp.swapaxes(t.reshape(*lead, s // dil, dil, e), -3, -2).reshape(*lead, s, e)


def _from_residue(t, dil):
    if dil == 1:
        return t
    *lead, s, e = t.shape
    return jnp.swapaxes(t.reshape(*lead, dil, s // dil, e), -3, -2).reshape(*lead, s, e)


def _pat_col_spec(s, base=0):
    return pl.BlockSpec((3, None, s, BLK), lambda b, p: (0, b, 0, base + p))


def _pat_pair_spec(s):
    return pl.BlockSpec((3, None, 2, s, BLK), lambda b, p: (0, b, p, 0, 0))


def _one_pat_col_spec(s, base=0):
    return pl.BlockSpec((None, None, s, BLK), lambda b, p, t: (t, b, 0, base + p))


def _one_pat_pair_spec(s):
    return pl.BlockSpec((None, None, 2, s, BLK), lambda b, p, t: (t, b, p, 0, 0))


def _one_pat_bias_spec():
    return pl.BlockSpec((None, 2, BLK, 2 * BLK), lambda b, p, t: (t, p, 0, 0))


def _blocks_per_class(s):
    t = pl.program_id(2)
    segs = [s // dil // BLK for _, dil in DIL_PATTERNS]
    return jnp.where(t == 0, segs[0], jnp.where(t == 1, segs[1], segs[2]))


def _dil_scores(qb, kp, kc, b_ref, h, prev_valid):
    zp = _dot_nt(qb, kp) + b_ref[h, :, 0:BLK]
    zp = jnp.where(prev_valid, zp, NEG)
    zc = _dot_nt(qb, kc) + b_ref[h, :, BLK:2 * BLK]
    return zp, zc


def _dil_fwd(qkvp, bias, name):
    _, nb, s, _ = qkvp.shape
    nblk = s // BLK

    def body(q_ref, k_ref, v_ref, b_ref, o_ref, lse_ref):
        sels = _head_masks()
        seg = _blocks_per_class(s)

        def block(b, _):
            cur = _rows128(b)
            prev = _rows128(jnp.maximum(b - 1, 0))
            qb = q_ref[cur, :] * 0.125
            kp, kc, vp, vc = k_ref[prev, :], k_ref[cur, :], v_ref[prev, :], v_ref[cur, :]
            acc = jnp.zeros((BLK, BLK), F32)
            for h, sel in enumerate(sels):
                zp, zc = _dil_scores(qb, jnp.where(sel, kp, 0.0), jnp.where(sel, kc, 0.0), b_ref, h, b % seg > 0)
                m = jnp.maximum(jnp.max(zp, axis=-1, keepdims=True), jnp.max(zc, axis=-1, keepdims=True))
                pp = jnp.exp(zp - m)
                pc = jnp.exp(zc - m)
                den = jnp.sum(pp, axis=-1, keepdims=True) + jnp.sum(pc, axis=-1, keepdims=True)
                acc = acc + (_dot(pp, jnp.where(sel, vp, 0.0)) + _dot(pc, jnp.where(sel, vc, 0.0))) / den
                lse_ref[h, cur, :] = jnp.broadcast_to(m + jnp.log(den), (BLK, BLK))
            o_ref[cur, :] = acc
            return 0

        lax.fori_loop(0, nblk, block, 0, unroll=2)

    return pl.pallas_call(
        body, name=name, grid=(nb, 2, len(DIL_PATTERNS)),
        in_specs=[_one_pat_col_spec(s, 0), _one_pat_col_spec(s, 2), _one_pat_col_spec(s, 4), _one_pat_bias_spec()],
        out_specs=[_one_pat_col_spec(s), _one_pat_pair_spec(s)],
        out_shape=[jax.ShapeDtypeStruct((3, nb, s, 2 * BLK), F32), jax.ShapeDtypeStruct((3, nb, N_HEADS, s, BLK), F32)],
        compiler_params=_cparams("parallel", "parallel", "parallel"),
    )(qkvp, qkvp, qkvp, bias)


def _dil_combine(o, lse, name):
    _, nb, s, _ = o.shape

    def body(o_ref, l_ref, out_ref, lse_ref):
        sels = _head_masks()
        weights, dens = [], []
        for h in range(2):
            m = jnp.maximum(jnp.maximum(l_ref[0, h], l_ref[1, h]), l_ref[2, h])
            w = [jnp.exp(l_ref[p, h] - m) for p in range(3)]
            den = w[0] + w[1] + w[2]
            lse_ref[h] = m + jnp.log(den)
            weights.append(w)
            dens.append(den)
        num = sum(jnp.where(sels[0], weights[0][p], weights[1][p]) * o_ref[p] for p in range(3))
        out_ref[...] = num / jnp.where(sels[0], dens[0], dens[1])

    return pl.pallas_call(
        body, name=name, grid=(nb, 2), in_specs=[_pat_col_spec(s), _pat_pair_spec(s)],
        out_specs=[_col_spec(s, 0), _pair_spec(s, BLK)],
        out_shape=[jax.ShapeDtypeStruct((nb * s, 2 * BLK), F32), jax.ShapeDtypeStruct((nb, N_HEADS, s, BLK), F32)],
        compiler_params=_cparams("parallel", "parallel"),
    )(o, lse)


def _dil_bwd(qkvp, dop, lsep, deltap, bias, name):
    _, nb, s, _ = qkvp.shape
    nblk = s // BLK

    def body(q_ref, k_ref, v_ref, do_ref, lse_ref, dl_ref, b_ref, dq_ref, dk_ref, dv_ref, g_ref):
        sels = _head_masks()
        seg = _blocks_per_class(s)
        dk_ref[...] = jnp.zeros_like(dk_ref)
        dv_ref[...] = jnp.zeros_like(dv_ref)
        g_ref[...] = jnp.zeros_like(g_ref)

        def block(b, _):
            cur = _rows128(b)
            prev = _rows128(jnp.maximum(b - 1, 0))
            qb = q_ref[cur, :] * 0.125
            dob = do_ref[cur, :]
            kp, kc, vp, vc = k_ref[prev, :], k_ref[cur, :], v_ref[prev, :], v_ref[cur, :]
            dq = jnp.zeros((BLK, BLK), F32)
            for h, sel in enumerate(sels):
                kph, kch = jnp.where(sel, kp, 0.0), jnp.where(sel, kc, 0.0)
                qh, doh = jnp.where(sel, qb, 0.0), jnp.where(sel, dob, 0.0)
                lse = lse_ref[h, cur, :]
                dlt = dl_ref[h, cur, :]
                zp, zc = _dil_scores(qb, kph, kch, b_ref, h, b % seg > 0)
                pp = jnp.exp(zp - lse)
                pc = jnp.exp(zc - lse)
                dsp = pp * (_dot_nt(dob, jnp.where(sel, vp, 0.0)) - dlt)
                dsc = pc * (_dot_nt(dob, jnp.where(sel, vc, 0.0)) - dlt)
                g_ref[h, :, 0:BLK] += dsp
                g_ref[h, :, BLK:2 * BLK] += dsc
                dsp = dsp.astype(MXU_DTYPE)
                dsc = dsc.astype(MXU_DTYPE)
                dq = dq + _dot(dsp, kph) + _dot(dsc, kch)
                dk_ref[prev, :] += _dot_tn(dsp, qh)
                dk_ref[cur, :] += _dot_tn(dsc, qh)
                dv_ref[prev, :] += _dot_tn(pp, doh)
                dv_ref[cur, :] += _dot_tn(pc, doh)
            dq_ref[cur, :] = dq * 0.125
            return 0

        lax.fori_loop(0, nblk, block, 0, unroll=2)

    cols, stats = _one_pat_col_spec(s), _one_pat_pair_spec(s)
    return pl.pallas_call(
        body, name=name, grid=(nb, 2, len(DIL_PATTERNS)),
        in_specs=[_one_pat_col_spec(s, 0), _one_pat_col_spec(s, 2), _one_pat_col_spec(s, 4), cols, stats, stats,
                  _one_pat_bias_spec()],
        out_specs=[cols, cols, cols, pl.BlockSpec((None, 2, None, BLK, 2 * BLK), lambda b, p, t: (b, p, t, 0, 0))],
        out_shape=[jax.ShapeDtypeStruct((3, nb, s, 2 * BLK), F32)] * 3 + [jax.ShapeDtypeStruct((nb, N_HEADS, 3, BLK, 2 * BLK), F32)],
        compiler_params=_cparams("parallel", "parallel", "parallel"),
    )(qkvp, qkvp, qkvp, dop, lsep, deltap, bias)


def _bucket_reduce(gbias, table, name):
    nb = gbias.shape[0]

    def body(g_ref, t_ref, o_ref):
        row = _iota2((8, BLK), 0)
        lane = _iota2((8, BLK), 1)
        gsum = [[sum(g_ref[b, h, p] for b in range(nb)) for p in range(3)] for h in range(N_HEADS)]

        def bucket(k, acc):
            for h in range(N_HEADS):
                tot = sum(jnp.sum(jnp.where(t_ref[p] == k, gsum[h][p], 0.0)) for p in range(3))
                acc = acc + jnp.where((row == h) & (lane == k), tot, 0.0)
            return acc

        o_ref[...] = lax.fori_loop(0, REL_BUCKETS, bucket, jnp.zeros((8, BLK), F32))

    vm = pl.BlockSpec(memory_space=pltpu.VMEM)
    return pl.pallas_call(
        body, name=name, in_specs=[vm, vm], out_specs=vm, out_shape=jax.ShapeDtypeStruct((8, BLK), F32),
        compiler_params=pltpu.CompilerParams(vmem_limit_bytes=VMEM_LIMIT),
    )(gbias, table)


def _place():
    x, y, c = lax.axis_index("x"), lax.axis_index("y"), lax.axis_index("c")
    others = [(1 - x, y), (x, 1 - y), (1 - x, 1 - y)]
    return x, y, c, others


def _remote(src, dst, send_sem, recv_sem, to):
    return pltpu.make_async_remote_copy(src_ref=src, dst_ref=dst, send_sem=send_sem, recv_sem=recv_sem,
                                        device_id=to, device_id_type=MESH)


def _gather_weights(wp):
    nl, r, w = wp.shape

    def body(wp_ref, out_ref, send_sems, recv_sems):
        x, y, c, others = _place()
        me = 2 * x + y
        sibling = (x, y, 1 - c)
        sends = [_remote(wp_ref.at[c], out_ref.at[me, c], send_sems.at[k], recv_sems.at[k], (ox, oy, c))
                 for k, (ox, oy) in enumerate(others)]
        for cp in sends:
            cp.start()
        passed = []
        for k, (ox, oy) in enumerate(others):
            landed = out_ref.at[2 * ox + oy, c]
            _remote(landed, landed, send_sems.at[k], recv_sems.at[k], (ox, oy, c)).wait_recv()
            cp = _remote(landed, landed, send_sems.at[3 + k], recv_sems.at[3 + k], sibling)
            cp.start()
            passed.append(cp)
        for k, (ox, oy) in enumerate(others):
            theirs = out_ref.at[2 * ox + oy, 1 - c]
            _remote(theirs, theirs, send_sems.at[3 + k], recv_sems.at[3 + k], sibling).wait_recv()
        for cp in sends + passed:
            cp.wait_send()

    hbm = pl.BlockSpec(memory_space=pl.ANY)
    return pl.pallas_call(
        body, name="gather_weights", in_specs=[hbm], out_specs=hbm,
        out_shape=jax.ShapeDtypeStruct((N_CHIPS, nl, r, w), wp.dtype),
        scratch_shapes=[pltpu.SemaphoreType.DMA((6,)), pltpu.SemaphoreType.DMA((6,))],
    )(wp)


def _swap_layers(g):
    _, ns, r, w = g.shape

    def body(g_ref, out_ref, send_sem, recv_sem):
        x, y, c, _ = _place()
        cp = _remote(g_ref.at[1 - c], out_ref, send_sem, recv_sem, (x, y, 1 - c))
        cp.start()
        cp.wait()

    hbm = pl.BlockSpec(memory_space=pl.ANY)
    return pl.pallas_call(
        body, name="swap_layers", in_specs=[hbm], out_specs=hbm, out_shape=jax.ShapeDtypeStruct((ns, r, w), g.dtype),
        scratch_shapes=[pltpu.SemaphoreType.DMA, pltpu.SemaphoreType.DMA],
    )(g)


def _pair_sum(g, other, core):
    _, ns, r, w = g.shape
    tw = 256

    def body(core_ref, g_ref, o_ref, out_ref):
        out_ref[...] = (g_ref[...] + o_ref[...]).astype(out_ref.dtype)

    grid_spec = pltpu.PrefetchScalarGridSpec(
        num_scalar_prefetch=1, grid=(ns, w // tw),
        in_specs=[pl.BlockSpec((None, None, r, tw), lambda k, j, core_ref: (core_ref[0], k, 0, j)),
                  pl.BlockSpec((None, r, tw), lambda k, j, core_ref: (k, 0, j))],
        out_specs=pl.BlockSpec((None, r, tw), lambda k, j, core_ref: (k, 0, j)))
    return pl.pallas_call(
        body, name="pair_sum", grid_spec=grid_spec, out_shape=jax.ShapeDtypeStruct((ns, r, w), MXU_DTYPE),
        compiler_params=_cparams("parallel", "parallel"),
    )(core.reshape(1).astype(jnp.int32), g, other)


def _scatter_shards(p):
    ns, r, w = p.shape

    def body(p_ref, q_ref, send_sems, recv_sems):
        x, y, c, others = _place()
        me = 2 * x + y
        sends = [_remote(p_ref.at[2 * ox + oy], q_ref.at[me], send_sems.at[k], recv_sems.at[k], (ox, oy, c))
                 for k, (ox, oy) in enumerate(others)]
        for cp in sends:
            cp.start()
        for k, (ox, oy) in enumerate(others):
            slot = q_ref.at[2 * ox + oy]
            _remote(slot, slot, send_sems.at[k], recv_sems.at[k], (ox, oy, c)).wait_recv()
        for cp in sends:
            cp.wait_send()

    hbm = pl.BlockSpec(memory_space=pl.ANY)
    return pl.pallas_call(
        body, name="scatter_shards", in_specs=[hbm], out_specs=hbm, out_shape=jax.ShapeDtypeStruct(p.shape, p.dtype),
        scratch_shapes=[pltpu.SemaphoreType.DMA((3,)), pltpu.SemaphoreType.DMA((3,))],
    )(p)


def _chip_sum(q, p, chip):
    ns, r, w = q.shape

    def body(chip_ref, q_ref, own_ref, out_ref):
        me = chip_ref[0]
        own = own_ref[...].astype(F32)
        terms = [jnp.where(me == k, own, q_ref[k].astype(F32)) for k in range(ns)]
        out_ref[...] = ((terms[0] + terms[1]) + terms[2]) + terms[3]

    tw = 128
    grid_spec = pltpu.PrefetchScalarGridSpec(
        num_scalar_prefetch=1, grid=(w // tw,),
        in_specs=[pl.BlockSpec((ns, r, tw), lambda j, chip_ref: (0, 0, j)),
                  pl.BlockSpec((None, r, tw), lambda j, chip_ref: (chip_ref[0], 0, j))],
        out_specs=pl.BlockSpec((r, tw), lambda j, chip_ref: (0, j)))
    return pl.pallas_call(
        body, name="chip_sum", grid_spec=grid_spec, out_shape=jax.ShapeDtypeStruct((r, w), F32),
        compiler_params=_cparams("parallel"),
    )(chip.reshape(1).astype(jnp.int32), q, p)


def _share_layers(gl):
    r, w = gl.shape

    def body(gl_ref, out_ref, send_sem, recv_sem):
        x, y, c, _ = _place()
        cp = _remote(gl_ref, out_ref, send_sem, recv_sem, (x, y, 1 - c))
        cp.start()
        cp.wait()

    hbm = pl.BlockSpec(memory_space=pl.ANY)
    return pl.pallas_call(
        body, name="share_layers", in_specs=[hbm], out_specs=hbm, out_shape=jax.ShapeDtypeStruct((r, w), gl.dtype),
        scratch_shapes=[pltpu.SemaphoreType.DMA, pltpu.SemaphoreType.DMA],
    )(gl)


def _gather_small(pk, name):
    rows, w = pk.shape

    def body(pk_ref, all_ref, sum_ref, send_sems, recv_sems):
        x, y, c, _ = _place()
        me = 4 * x + 2 * y + c
        all_ref[me] = pk_ref[...]
        flips = [(fx, fy, fc) for fx in (0, 1) for fy in (0, 1) for fc in (0, 1)][1:]
        peers = [(x ^ fx, y ^ fy, c ^ fc) for fx, fy, fc in flips]
        sends = [_remote(pk_ref, all_ref.at[me], send_sems.at[k], recv_sems.at[k], peer) for k, peer in enumerate(peers)]
        for cp in sends:
            cp.start()
        for k, (px, py, pc) in enumerate(peers):
            slot = all_ref.at[4 * px + 2 * py + pc]
            _remote(slot, slot, send_sems.at[k], recv_sems.at[k], (px, py, pc)).wait_recv()
        for cp in sends:
            cp.wait_send()
        total = all_ref[0]
        for d in range(1, N_DEV):
            total = total + all_ref[d]
        sum_ref[...] = total

    vm = pl.BlockSpec(memory_space=pltpu.VMEM)
    return pl.pallas_call(
        body, name=name, in_specs=[vm], out_specs=[vm, vm],
        out_shape=[jax.ShapeDtypeStruct((N_DEV, rows, w), F32), jax.ShapeDtypeStruct((rows, w), F32)],
        scratch_shapes=[pltpu.SemaphoreType.DMA((7,)), pltpu.SemaphoreType.DMA((7,))],
    )(pk)


def _row_layout(c, nb, s):
    ch = jnp.swapaxes(c[:, :N_HEADS].reshape(nb, s, N_HEADS), 1, 2)
    ccol = jnp.broadcast_to(ch[..., None], (nb, N_HEADS, s, ATT))
    crow = jnp.broadcast_to(ch.reshape(nb, N_HEADS, s // ATT, 1, ATT), (nb, N_HEADS, s // ATT, 8, ATT))
    return ccol, crow


def _dil_bias(rel_bias, name):
    def body(rel_ref, t_ref, o_ref):
        for p in range(len(DIL_PATTERNS)):
            table = t_ref[p]

            def bucket(k, accs, table=table):
                return tuple(jnp.where(table == k, rel_ref[k, h], acc) for h, acc in enumerate(accs))

            accs = lax.fori_loop(0, REL_BUCKETS, bucket, tuple(jnp.full((BLK, 2 * BLK), NEG, F32) for _ in range(N_HEADS)))
            for h in range(N_HEADS):
                o_ref[p, h] = accs[h]

    vm = pl.BlockSpec(memory_space=pltpu.VMEM)
    return pl.pallas_call(
        body, name=name, in_specs=[pl.BlockSpec(memory_space=pltpu.SMEM), vm], out_specs=vm,
        out_shape=jax.ShapeDtypeStruct((len(DIL_PATTERNS), N_HEADS, BLK, 2 * BLK), F32),
        compiler_params=pltpu.CompilerParams(vmem_limit_bytes=VMEM_LIMIT),
    )(rel_bias, jnp.asarray(_bucket_table()))


def _layer_forward(x, wts, small, nb, s, tag):
    proj = _matmul(x, wts["w_in"], "proj", tag)

    o_sb = _sb_fwd(proj, nb, s, f"sb_fwd_{tag}")

    dl = proj[:, 3 * CONV_W:6 * CONV_W].reshape(nb, s, 3 * CONV_W).astype(MXU_DTYPE)
    qkvp = jnp.stack([_to_residue(dl, dil) for _, dil in DIL_PATTERNS])
    bias = _dil_bias(small["rel_bias"], f"dil_bias_{tag}")
    o_p, lse_p = _dil_fwd(qkvp, bias, f"dil_fwd_{tag}")
    o_nat = jnp.stack([_from_residue(o_p[p], dil) for p, (_, dil) in enumerate(DIL_PATTERNS)])
    lse_nat = jnp.stack([_from_residue(lse_p[p], dil) for p, (_, dil) in enumerate(DIL_PATTERNS)])
    o_dl, lse_dl = _dil_combine(o_nat, lse_nat, f"dil_mix_{tag}")

    fb = jnp.zeros((8, BLK), F32).at[0, :N_HEADS].set(small["f_bias"])
    csum = _fox_gates_fwd(proj, fb, nb, s, f"fox_gates_{tag}")
    ccol, crow = _row_layout(csum, nb, s)
    o_fx, lse_fx = _fox_fwd(proj, ccol, crow, nb, s, f"fox_fwd_{tag}")

    cw = jnp.zeros((8, CONV_W), F32).at[:3].set(small["conv_w"])
    o_cv = _conv_fwd(proj, cw, nb, s, f"conv_fwd_{tag}")

    mixed = jnp.concatenate([o_sb, o_dl, o_fx, o_cv], axis=-1).astype(MXU_DTYPE)
    mix = _matmul(mixed, wts["w_out"], "out_proj", tag)
    pre1, x1 = _ln_fwd(x, mix, small["ln1_g"], small["ln1_b"], f"ln1_fwd_{tag}")
    gu = _matmul(x1, wts["w_gu"], "ffn_in", tag)
    hid = _swiglu_fwd(gu, f"swiglu_fwd_{tag}")
    ffn = _matmul(hid, wts["w_down"], "ffn_out", tag)
    pre2, x2 = _ln_fwd(x1, ffn, small["ln2_g"], small["ln2_b"], f"ln2_fwd_{tag}")
    saved = dict(x=x, proj=proj, qkvp=qkvp, bias=bias, o_dl=o_dl, lse_dl=lse_dl, fb=fb, ccol=ccol, crow=crow,
                 o_fx=o_fx, lse_fx=lse_fx, cw=cw, mixed=mixed, pre1=pre1, x1=x1, gu=gu, hid=hid, pre2=pre2)
    return x2, saved


def _layer_backward(dx2, sv, wts, small, nb, s, tag):
    t = nb * s
    dpre2, dgb2 = _ln_bwd(dx2, sv["pre2"], small["ln2_g"], f"ln2_bwd_{tag}")
    dpre2_b = dpre2.astype(MXU_DTYPE)
    dhid = _matmul(dpre2_b, wts["w_down"], "ffn_out_dx", tag, trans_b=True)
    dw_down = _matmul(sv["hid"].T, dpre2_b, "ffn_out_dw", tag)
    dgu = _swiglu_bwd(dhid, sv["gu"], f"swiglu_bwd_{tag}")
    dx1 = _matmul(dgu, wts["w_gu"], "ffn_in_dx", tag, add=dpre2, add_scale=ALPHA, trans_b=True)
    dw_gu = _matmul(sv["x1"].astype(MXU_DTYPE).T, dgu, "ffn_in_dw", tag)

    dpre1, dgb1 = _ln_bwd(dx1, sv["pre1"], small["ln1_g"], f"ln1_bwd_{tag}")
    dpre1_b = dpre1.astype(MXU_DTYPE)
    dmixed = _matmul(dpre1_b, wts["w_out"], "out_proj_dx", tag, trans_b=True)
    dw_out = _matmul(sv["mixed"].T, dpre1_b, "out_proj_dw", tag)
    proj = sv["proj"]

    dq_sb, dk_sb, dv_sb = _sb_bwd(proj, dmixed, nb, s, f"sb_bwd_{tag}")

    delta_dl = _delta_kernel(dmixed, sv["o_dl"], nb, s, f"dil_delta_{tag}")
    do_dl = dmixed[:, CONV_W:2 * CONV_W].reshape(nb, s, CONV_W).astype(MXU_DTYPE)
    dop = jnp.stack([_to_residue(do_dl, dil) for _, dil in DIL_PATTERNS])
    lsep = jnp.stack([_to_residue(sv["lse_dl"], dil) for _, dil in DIL_PATTERNS])
    deltap = jnp.stack([_to_residue(delta_dl, dil) for _, dil in DIL_PATTERNS])
    dqp, dkp, dvp, gbias = _dil_bwd(sv["qkvp"], dop, lsep, deltap, sv["bias"], f"dil_bwd_{tag}")
    unperm = lambda tp: sum(_from_residue(tp[p], dil) for p, (_, dil) in enumerate(DIL_PATTERNS)).reshape(t, CONV_W)
    dq_dl, dk_dl, dv_dl = unperm(dqp), unperm(dkp), unperm(dvp)
    drel = _bucket_reduce(gbias, jnp.asarray(_bucket_table()), f"rel_bias_grad_{tag}")

    dq_fx, dk_fx, dv_fx, dcol = _fox_bwd(proj, dmixed, sv["lse_fx"], sv["ccol"], sv["crow"], nb, s, f"fox_bwd_{tag}")
    dcs = -jnp.swapaxes(dcol[:, :, :, 0, :].reshape(nb, N_HEADS, s), 1, 2).reshape(t, N_HEADS)
    dcs = jnp.pad(dcs, ((0, 0), (0, BLK - N_HEADS)))
    dfx, dfb = _fox_gates_bwd(dcs, proj, sv["fb"], nb, s, f"fox_gates_bwd_{tag}")

    dgates, dcw = _conv_bwd(dmixed, proj, sv["cw"], nb, s, f"conv_bwd_{tag}")

    dproj = jnp.concatenate([dq_sb, dk_sb, dv_sb, dq_dl, dk_dl, dv_dl, dq_fx, dk_fx, dv_fx, dgates, dfx],
                            axis=-1).astype(MXU_DTYPE)
    dx = _matmul(dproj, wts["w_in"], "proj_dx", tag, add=dpre1, add_scale=ALPHA, trans_b=True)
    dw_in = _matmul(sv["x"].astype(MXU_DTYPE).T, dproj, "proj_dw", tag)

    grads = dict(w_in=dw_in[:, :PROJ], w_out=dw_out, w_gate=dw_gu[:, :D_FF], w_up=dw_gu[:, D_FF:], w_down=dw_down,
                 ln1_g=dgb1[0], ln1_b=dgb1[1], ln2_g=dgb2[0], ln2_b=dgb2[1], conv_w=dcw[:3], f_bias=dfb[0, :N_HEADS],
                 rel_bias=drel[:N_HEADS, :REL_BUCKETS].T)
    return dx, grads


def _local_step(x, target, full, small_all):
    nb, s, d = x.shape
    h = x.reshape(nb * s, d)
    saved = []
    for layer in range(DEPTH):
        h, sv = _layer_forward(h, full[layer], small_all[layer], nb, s, f"l{layer}")
        saved.append(sv)
    dy, lossp = _loss_kernel(h, target.reshape(nb * s, d), "loss")
    grads = [None] * DEPTH
    for layer in reversed(range(DEPTH)):
        dy, grads[layer] = _layer_backward(dy, saved[layer], full[layer], small_all[layer], nb, s, f"l{layer}")
    return lossp, dy.reshape(nb, s, d), grads


_SHARD_SHAPES = (("w_in", (D_MODEL, PROJ // N_CHIPS)), ("w_out", (D_MODEL // N_CHIPS, D_MODEL)),
                 ("w_gate", (D_MODEL, D_FF // N_CHIPS)), ("w_up", (D_MODEL, D_FF // N_CHIPS)),
                 ("w_down", (D_FF // N_CHIPS, D_MODEL)))


def _pack_shards(parts, lead):
    flat = [parts[name].reshape(*lead, -1, D_MODEL) for name, _ in _SHARD_SHAPES]
    return jnp.concatenate(flat, axis=-2)


def _unpack_shards(slab):
    lead = slab.shape[:-2]
    out, row = {}, 0
    for name, (r, c) in _SHARD_SHAPES:
        n = r * c // D_MODEL
        out[name] = slab[..., row:row + n, :].reshape(*lead, r, c)
        row += n
    assert row == PACK_ROWS
    return out


def _full_weights(gathered):
    sh = _unpack_shards(gathered)
    cols = lambda t: jnp.moveaxis(t, 0, 2).reshape(DEPTH, t.shape[2], -1)
    rows = lambda t: jnp.moveaxis(t, 0, 1).reshape(DEPTH, -1, t.shape[3])
    w_in = jnp.pad(cols(sh["w_in"]), ((0, 0), (0, 0), (0, PROJ_PAD - PROJ)))
    w_gu = jnp.concatenate([cols(sh["w_gate"]), cols(sh["w_up"])], axis=-1)
    w_out, w_down = rows(sh["w_out"]), rows(sh["w_down"])
    return [dict(w_in=w_in[l], w_out=w_out[l], w_gu=w_gu[l], w_down=w_down[l]) for l in range(DEPTH)]


def _shard_major(grads):
    by_cols = lambda g: jnp.moveaxis(g.reshape(g.shape[0], N_CHIPS, -1), 1, 0)
    by_rows = lambda g: g.reshape(N_CHIPS, -1, g.shape[1])
    per_layer = []
    for g in grads:
        parts = dict(w_in=by_cols(g["w_in"]), w_out=by_rows(g["w_out"]), w_gate=by_cols(g["w_gate"]),
                     w_up=by_cols(g["w_up"]), w_down=by_rows(g["w_down"]))
        per_layer.append(_pack_shards(parts, (N_CHIPS,)))
    return jnp.stack(per_layer)


_SMALL_LAYOUT = (("ln1_g", 0), ("ln1_b", 2), ("ln2_g", 4), ("ln2_b", 6), ("conv_w", 8))
_ROW_MISC = 10
_ROW_LOSS = 11


def _pack_small(per_layer, rel_bias, loss=None):
    pk = jnp.zeros((SMALL_ROWS, D_MODEL), F32)
    for name, row in _SMALL_LAYOUT:
        for l in range(DEPTH):
            v = per_layer[l][name].reshape(-1)
            pk = pk.at[row + l, :v.shape[0]].set(v)
    fb = jnp.concatenate([per_layer[l]["f_bias"] for l in range(DEPTH)])
    pk = pk.at[_ROW_MISC, :2 * N_HEADS].set(fb)
    pk = pk.at[_ROW_MISC, BLK:BLK + REL_BUCKETS * N_HEADS].set(rel_bias.reshape(-1))
    if loss is not None:
        pk = pk.at[_ROW_LOSS, 0].set(loss)
    return pk


def _unpack_small(pk, conv_cols):
    out = {}
    for name, row in _SMALL_LAYOUT:
        n = 3 * conv_cols if name == "conv_w" else D_MODEL
        v = pk[row:row + DEPTH, :n]
        out[name] = v.reshape(DEPTH, 3, conv_cols) if name == "conv_w" else v
    out["f_bias"] = pk[_ROW_MISC, :2 * N_HEADS].reshape(DEPTH, N_HEADS)
    out["rel_bias"] = pk[_ROW_MISC, BLK:BLK + REL_BUCKETS * N_HEADS].reshape(REL_BUCKETS, N_HEADS)
    return out


_WEIGHTS = ("w_in", "f_bias", "conv_w", "w_out", "rel_bias", "ln1_g", "ln1_b", "w_gate", "w_up", "w_down", "ln2_g", "ln2_b")
_BIG = ("w_in", "w_out", "w_gate", "w_up", "w_down")


def kernel(x, w_in, f_bias, conv_w, w_out, rel_bias, ln1_g, ln1_b, w_gate, w_up, w_down, ln2_g, ln2_b, loss_target, m_w_in, m_f_bias, m_conv_w, m_w_out, m_rel_bias, m_ln1_g, m_ln1_b, m_w_gate, m_w_up, m_w_down, m_ln2_g, m_ln2_b, v_w_in, v_f_bias, v_conv_w, v_w_out, v_rel_bias, v_ln1_g, v_ln1_b, v_w_gate, v_w_up, v_w_down, v_ln2_g, v_ln2_b):
    w = dict(w_in=w_in, f_bias=f_bias, conv_w=conv_w, w_out=w_out, rel_bias=rel_bias, ln1_g=ln1_g, ln1_b=ln1_b,
             w_gate=w_gate, w_up=w_up, w_down=w_down, ln2_g=ln2_g, ln2_b=ln2_b)
    m = dict(w_in=m_w_in, f_bias=m_f_bias, conv_w=m_conv_w, w_out=m_w_out, rel_bias=m_rel_bias, ln1_g=m_ln1_g,
             ln1_b=m_ln1_b, w_gate=m_w_gate, w_up=m_w_up, w_down=m_w_down, ln2_g=m_ln2_g, ln2_b=m_ln2_b)
    v = dict(w_in=v_w_in, f_bias=v_f_bias, conv_w=v_conv_w, w_out=v_w_out, rel_bias=v_rel_bias, ln1_g=v_ln1_g,
             ln1_b=v_ln1_b, w_gate=v_w_gate, w_up=v_w_up, w_down=v_w_down, ln2_g=v_ln2_g, ln2_b=v_ln2_b)
    chip = 2 * lax.axis_index("x") + lax.axis_index("y")
    core = lax.axis_index("c")
    conv_shard = CONV_W // N_CHIPS

    slab = _pack_shards({name: w[name] for name in _BIG}, (DEPTH,)).astype(MXU_DTYPE)
    full = _full_weights(lax.dynamic_update_index_in_dim(_gather_weights(slab), slab, chip, 0))
    cw_pk = jnp.zeros((8, D_MODEL), F32).at[0, :DEPTH * 3 * conv_shard].set(conv_w.reshape(-1))
    cw_all, _ = _gather_small(cw_pk, "gather_conv_w")
    cw_chips = cw_all[0::2, 0, :DEPTH * 3 * conv_shard].reshape(N_CHIPS, DEPTH, 3, conv_shard)
    conv_full = jnp.moveaxis(cw_chips, 0, 2).reshape(DEPTH, 3, CONV_W)
    small_all = [dict(f_bias=f_bias[l], conv_w=conv_full[l], rel_bias=rel_bias, ln1_g=ln1_g[l], ln1_b=ln1_b[l],
                      ln2_g=ln2_g[l], ln2_b=ln2_b[l]) for l in range(DEPTH)]

    lossp, grad_x, grads = _local_step(x, loss_target, full, small_all)

    g = _shard_major(grads)
    pair = _pair_sum(g, _swap_layers(g), core)
    mine = _chip_sum(_scatter_shards(pair), pair, chip)
    theirs = _share_layers(mine)
    both = jnp.where(core == 0, jnp.stack([mine, theirs]), jnp.stack([theirs, mine]))
    big_g = _unpack_shards(both)

    drel = grads[0]["rel_bias"] + grads[1]["rel_bias"]
    small_pk = _pack_small(grads, drel, lossp[0, 0])
    _, small_sum = _gather_small(small_pk, "gather_small_grads")
    loss = small_sum[_ROW_LOSS, 0]
    small_g = _unpack_small(small_sum, CONV_W)
    small_g["conv_w"] = lax.dynamic_slice_in_dim(small_g["conv_w"], chip * conv_shard, conv_shard, axis=2)

    out_g, out_d, out_m, out_v = dict(small_g), {}, {}, {}
    for name in _BIG:
        out_g[name] = big_g[name]
        out_d[name], out_m[name], out_v[name] = _adamw(w[name], big_g[name], m[name], v[name], f"adamw_{name}")
    per_layer = lambda src: [{name: src[name][l] for name in ("ln1_g", "ln1_b", "ln2_g", "ln2_b", "conv_w", "f_bias")}
                             for l in range(DEPTH)]
    packs = [_pack_small(per_layer(src), src["rel_bias"])[None] for src in (w, small_g, m, v)]
    for dst, pk in zip((out_d, out_m, out_v), _adamw(*packs, "adamw_small")):
        dst.update(_unpack_small(pk[0], conv_shard))

    return (loss, grad_x, *[out_g[n] for n in _WEIGHTS], *[out_d[n] for n in _WEIGHTS],
            *[out_m[n] for n in _WEIGHTS], *[out_v[n] for n in _WEIGHTS])
```

```python
import functools
import math

import numpy as np
import jax
import jax.numpy as jnp
from jax import lax
from jax.experimental import pallas as pl
from jax.experimental.pallas import tpu as pltpu

F32 = jnp.float32
BF16 = jnp.bfloat16
MXU_DTYPE = BF16

D_MODEL = 1024
HEAD_DIM = 64
N_HEADS = 4
BLK = 128
ATT = 256
CONV_W = 256
PROJ = 3076
PROJ_PAD = 3200
D_FF = 2816
DEPTH = 2
ALPHA = (2 * DEPTH) ** 0.25
LN_EPS = 1e-5
NEG = -1e30
DIL_PATTERNS = ((128, 1), (512, 4), (2048, 16))
REL_BUCKETS = 32
N_CHIPS = 4
N_DEV = 8
SMALL_ROWS = 16

ADAM_LR = 0.001
ADAM_B1 = 0.9
ADAM_B2 = 0.999
ADAM_EPS = 1e-08
ADAM_WD = 0.01
ADAM_STEP = 10

VMEM_LIMIT = 48 * 2 ** 20
MESH = pl.DeviceIdType.MESH


def _cparams(*sem):
    return pltpu.CompilerParams(dimension_semantics=tuple(sem), vmem_limit_bytes=VMEM_LIMIT)


def _dot(a, b):
    return jnp.dot(a.astype(MXU_DTYPE), b.astype(MXU_DTYPE), preferred_element_type=F32)


def _dot_nt(a, b):
    return lax.dot_general(a.astype(MXU_DTYPE), b.astype(MXU_DTYPE), (((1,), (1,)), ((), ())),
                           preferred_element_type=F32)


def _dot_tn(a, b):
    return lax.dot_general(a.astype(MXU_DTYPE), b.astype(MXU_DTYPE), (((0,), (0,)), ((), ())),
                           preferred_element_type=F32)


def _split_dot(x, ones, passes):
    acc, rest = None, x
    for p in range(passes):
        piece = rest.astype(MXU_DTYPE)
        part = jnp.dot(piece, ones, preferred_element_type=F32)
        acc = part if acc is None else acc + part
        if p + 1 < passes:
            rest = rest - piece.astype(F32)
    return acc


def _split_dot_lhs(ones, x, passes):
    acc, rest = None, x
    for p in range(passes):
        piece = rest.astype(MXU_DTYPE)
        part = jnp.dot(ones, piece, preferred_element_type=F32)
        acc = part if acc is None else acc + part
        if p + 1 < passes:
            rest = rest - piece.astype(F32)
    return acc


def _iota2(shape, axis):
    return lax.broadcasted_iota(jnp.int32, shape, axis)


_TILES = {"proj": (1024, 640, 1024), "out_proj": (1024, 1024, 1024), "ffn_in": (1024, 1408, 1024),
          "ffn_out": (1024, 1024, 2816), "ffn_out_dx": (1024, 1408, 1024), "ffn_out_dw": (1408, 1024, 2048),
          "ffn_in_dx": (1024, 1024, 1408), "ffn_in_dw": (1024, 1408, 2048), "out_proj_dx": (1024, 1024, 1024),
          "out_proj_dw": (1024, 1024, 2048), "proj_dx": (1024, 1024, 640), "proj_dw": (1024, 640, 2048)}


def _matmul(a, b, kind, tag, *, out_dtype=F32, add=None, add_scale=1.0, trans_b=False):
    m, k = a.shape
    n = b.shape[0] if trans_b else b.shape[1]
    tm, tn, tk = _TILES[kind]
    tm, tk, name = min(tm, m), min(tk, k), f"{kind}_{tag}"
    assert m % tm == 0 and n % tn == 0 and k % tk == 0, (a.shape, b.shape, tm, tn, tk)
    nk = k // tk

    def body(*refs):
        if add is None:
            a_ref, b_ref, o_ref = refs[:3]
            c_ref, scr = None, refs[3:]
        else:
            a_ref, b_ref, c_ref, o_ref = refs[:4]
            scr = refs[4:]
        part = _dot_nt(a_ref[...], b_ref[...]) if trans_b else _dot(a_ref[...], b_ref[...])

        def finish(acc):
            if c_ref is not None:
                acc = acc + add_scale * c_ref[...]
            o_ref[...] = acc.astype(out_dtype)

        if nk == 1:
            finish(part)
        else:
            acc_ref = scr[0]
            kk = pl.program_id(2)

            @pl.when(kk == 0)
            def _():
                acc_ref[...] = part

            @pl.when(kk > 0)
            def _():
                acc_ref[...] += part

            @pl.when(kk == nk - 1)
            def _():
                finish(acc_ref[...])

    b_spec = pl.BlockSpec((tn, tk), lambda i, j, kk: (j, kk)) if trans_b else pl.BlockSpec((tk, tn), lambda i, j, kk: (kk, j))
    in_specs = [pl.BlockSpec((tm, tk), lambda i, j, kk: (i, kk)), b_spec]
    operands = [a, b]
    if add is not None:
        in_specs.append(pl.BlockSpec((tm, tn), lambda i, j, kk: (i, j)))
        operands.append(add)
    return pl.pallas_call(
        body, name=name, grid=(m // tm, n // tn, nk), in_specs=in_specs,
        out_specs=pl.BlockSpec((tm, tn), lambda i, j, kk: (i, j)),
        out_shape=jax.ShapeDtypeStruct((m, n), out_dtype),
        scratch_shapes=[pltpu.VMEM((tm, tn), F32)] if nk > 1 else [],
        compiler_params=_cparams("parallel", "parallel", "arbitrary"),
    )(*operands)


def _ln_stats(pre):
    mu = jnp.mean(pre, axis=-1, keepdims=True)
    xc = pre - mu
    var = jnp.mean(xc * xc, axis=-1, keepdims=True)
    rstd = lax.rsqrt(var + LN_EPS)
    return xc * rstd, rstd


def _ln_fwd(xin, branch, g, b, name):
    t, d = xin.shape
    tile = 256

    def body(x_ref, br_ref, g_ref, b_ref, pre_ref, y_ref):
        pre = ALPHA * x_ref[...] + br_ref[...]
        xhat, _ = _ln_stats(pre)
        pre_ref[...] = pre
        y_ref[...] = xhat * g_ref[...] + b_ref[...]

    row = pl.BlockSpec((tile, d), lambda i: (i, 0))
    vec = pl.BlockSpec((1, d), lambda i: (0, 0))
    return pl.pallas_call(
        body, name=name, grid=(t // tile,), in_specs=[row, row, vec, vec], out_specs=[row, row],
        out_shape=[jax.ShapeDtypeStruct((t, d), F32)] * 2, compiler_params=_cparams("parallel"),
    )(xin, branch, g.reshape(1, d), b.reshape(1, d))


def _ln_bwd(dy, pre, g, name):
    t, d = dy.shape
    tile = 256

    def body(dy_ref, pre_ref, g_ref, dpre_ref, dgb_ref):
        dyv = dy_ref[...]
        xhat, rstd = _ln_stats(pre_ref[...])
        dxh = dyv * g_ref[...]
        m1 = jnp.mean(dxh, axis=-1, keepdims=True)
        m2 = jnp.mean(dxh * xhat, axis=-1, keepdims=True)
        dpre_ref[...] = rstd * (dxh - m1 - xhat * m2)

        @pl.when(pl.program_id(0) == 0)
        def _():
            dgb_ref[...] = jnp.zeros_like(dgb_ref)

        dgb_ref[0:1, :] += jnp.sum(dyv * xhat, axis=0, keepdims=True)
        dgb_ref[1:2, :] += jnp.sum(dyv, axis=0, keepdims=True)

    row = pl.BlockSpec((tile, d), lambda i: (i, 0))
    return pl.pallas_call(
        body, name=name, grid=(t // tile,), in_specs=[row, row, pl.BlockSpec((1, d), lambda i: (0, 0))],
        out_specs=[row, pl.BlockSpec((8, d), lambda i: (0, 0))],
        out_shape=[jax.ShapeDtypeStruct((t, d), F32), jax.ShapeDtypeStruct((8, d), F32)],
        compiler_params=_cparams("arbitrary"),
    )(dy, pre, g.reshape(1, d))


def _swiglu_fwd(gu, name):
    t = gu.shape[0]
    tile = 256

    def body(gu_ref, h_ref):
        gate = gu_ref[:, :D_FF]
        up = gu_ref[:, D_FF:]
        h_ref[...] = (gate * (1.0 / (1.0 + jnp.exp(-gate))) * up).astype(h_ref.dtype)

    return pl.pallas_call(
        body, name=name, grid=(t // tile,), in_specs=[pl.BlockSpec((tile, 2 * D_FF), lambda i: (i, 0))],
        out_specs=pl.BlockSpec((tile, D_FF), lambda i: (i, 0)),
        out_shape=jax.ShapeDtypeStruct((t, D_FF), MXU_DTYPE), compiler_params=_cparams("parallel"),
    )(gu)


def _swiglu_bwd(dh, gu, name):
    t = gu.shape[0]
    tile = 256

    def body(dh_ref, gu_ref, dgu_ref):
        gate = gu_ref[:, :D_FF]
        up = gu_ref[:, D_FF:]
        dhv = dh_ref[...]
        sig = 1.0 / (1.0 + jnp.exp(-gate))
        dgu_ref[:, :D_FF] = (dhv * up * sig * (1.0 + gate * (1.0 - sig))).astype(dgu_ref.dtype)
        dgu_ref[:, D_FF:] = (dhv * gate * sig).astype(dgu_ref.dtype)

    return pl.pallas_call(
        body, name=name, grid=(t // tile,),
        in_specs=[pl.BlockSpec((tile, D_FF), lambda i: (i, 0)), pl.BlockSpec((tile, 2 * D_FF), lambda i: (i, 0))],
        out_specs=pl.BlockSpec((tile, 2 * D_FF), lambda i: (i, 0)),
        out_shape=jax.ShapeDtypeStruct((t, 2 * D_FF), MXU_DTYPE), compiler_params=_cparams("parallel"),
    )(dh, gu)


def _loss_kernel(y, target, name):
    t, d = y.shape
    tile = 512

    def body(y_ref, t_ref, dy_ref, l_ref):
        err = y_ref[...] - t_ref[...]
        dy_ref[...] = err * (1.0 / d)

        @pl.when(pl.program_id(0) == 0)
        def _():
            l_ref[...] = jnp.zeros_like(l_ref)

        l_ref[...] += jnp.sum(err * err) * (0.5 / d)

    row = pl.BlockSpec((tile, d), lambda i: (i, 0))
    return pl.pallas_call(
        body, name=name, grid=(t // tile,), in_specs=[row, row],
        out_specs=[row, pl.BlockSpec((8, 128), lambda i: (0, 0))],
        out_shape=[jax.ShapeDtypeStruct((t, d), F32), jax.ShapeDtypeStruct((8, 128), F32)],
        compiler_params=_cparams("arbitrary"),
    )(y, target)


def _adamw(w, g, m, v, name):
    nl, r, c = w.shape
    tr = r
    for cand in (256, 352, 128, 64, 16, 8):
        if r % cand == 0:
            tr = cand
            break

    def body(w_ref, g_ref, m_ref, v_ref, d_ref, nm_ref, nv_ref):
        gv = g_ref[...]
        nm = ADAM_B1 * m_ref[...] + (1.0 - ADAM_B1) * gv
        nv = ADAM_B2 * v_ref[...] + (1.0 - ADAM_B2) * (gv * gv)
        m_hat = nm / (1.0 - ADAM_B1 ** ADAM_STEP)
        v_hat = nv / (1.0 - ADAM_B2 ** ADAM_STEP)
        d_ref[...] = -ADAM_LR * (m_hat / (jnp.sqrt(v_hat) + ADAM_EPS) + ADAM_WD * w_ref[...])
        nm_ref[...] = nm
        nv_ref[...] = nv

    blk = pl.BlockSpec((1, tr, c), lambda l, i: (l, i, 0))
    return pl.pallas_call(
        body, name=name, grid=(nl, r // tr), in_specs=[blk] * 4, out_specs=[blk] * 3,
        out_shape=[jax.ShapeDtypeStruct(w.shape, F32)] * 3, compiler_params=_cparams("parallel", "parallel"),
    )(w, g, m, v)


def _shift_down(u, k, rows):
    return jnp.where(rows >= k, pltpu.roll(u, k, 0), 0.0)


def _shift_up(u, k, rows, s):
    return jnp.where(rows < s - k, pltpu.roll(u, s - k, 0), 0.0)


def _conv_fwd(proj, conv_w, nb, s, name):
    def body(b_ref, c_ref, h_ref, w_ref, o_ref):
        rows = _iota2((s, CONV_W), 0)
        u = c_ref[...] * h_ref[...]
        y = w_ref[2:3, :] * u + w_ref[1:2, :] * _shift_down(u, 1, rows) + w_ref[0:1, :] * _shift_down(u, 2, rows)
        o_ref[...] = b_ref[...] * y

    col = lambda j: pl.BlockSpec((s, CONV_W), lambda b: (b, j))
    return pl.pallas_call(
        body, name=name, grid=(nb,),
        in_specs=[col(9), col(10), col(11), pl.BlockSpec((8, CONV_W), lambda b: (0, 0))],
        out_specs=pl.BlockSpec((s, CONV_W), lambda b: (b, 0)),
        out_shape=jax.ShapeDtypeStruct((nb * s, CONV_W), F32), compiler_params=_cparams("parallel"),
    )(proj, proj, proj, conv_w)


def _conv_bwd(dmixed, proj, conv_w, nb, s, name):
    def body(do_ref, b_ref, c_ref, h_ref, w_ref, dg_ref, dw_ref):
        rows = _iota2((s, CONV_W), 0)
        cg, hg, bg, dout = c_ref[...], h_ref[...], b_ref[...], do_ref[...]
        u = cg * hg
        u1 = _shift_down(u, 1, rows)
        u2 = _shift_down(u, 2, rows)
        y = w_ref[2:3, :] * u + w_ref[1:2, :] * u1 + w_ref[0:1, :] * u2
        dy = dout * bg
        du = w_ref[2:3, :] * dy + w_ref[1:2, :] * _shift_up(dy, 1, rows, s) + w_ref[0:1, :] * _shift_up(dy, 2, rows, s)
        dg_ref[:, 0:CONV_W] = dout * y
        dg_ref[:, CONV_W:2 * CONV_W] = du * hg
        dg_ref[:, 2 * CONV_W:3 * CONV_W] = du * cg

        @pl.when(pl.program_id(0) == 0)
        def _():
            dw_ref[...] = jnp.zeros_like(dw_ref)

        dw_ref[0:1, :] += jnp.sum(dy * u2, axis=0, keepdims=True)
        dw_ref[1:2, :] += jnp.sum(dy * u1, axis=0, keepdims=True)
        dw_ref[2:3, :] += jnp.sum(dy * u, axis=0, keepdims=True)

    col = lambda j: pl.BlockSpec((s, CONV_W), lambda b: (b, j))
    return pl.pallas_call(
        body, name=name, grid=(nb,),
        in_specs=[col(3), col(9), col(10), col(11), pl.BlockSpec((8, CONV_W), lambda b: (0, 0))],
        out_specs=[pl.BlockSpec((s, 3 * CONV_W), lambda b: (b, 0)), pl.BlockSpec((8, CONV_W), lambda b: (0, 0))],
        out_shape=[jax.ShapeDtypeStruct((nb * s, 3 * CONV_W), F32), jax.ShapeDtypeStruct((8, CONV_W), F32)],
        compiler_params=_cparams("arbitrary"),
    )(dmixed, proj, proj, proj, conv_w)


def _col_spec(s, base):
    return pl.BlockSpec((s, BLK), lambda b, p: (b, base + p))


def _rows(i):
    return pl.ds(pl.multiple_of(i * ATT, ATT), ATT)


def _rows128(i):
    return pl.ds(pl.multiple_of(i * BLK, BLK), BLK)


def _log_sigmoid_parts(z):
    e = jnp.exp(-jnp.abs(z))
    l1p = jnp.log(1.0 + e)
    lb = jnp.minimum(z, 0.0) - l1p
    return lb, lb - z, e


def _head_masks():
    lane = _iota2((1, BLK), 1)
    return [(lane >= h * HEAD_DIM) & (lane < (h + 1) * HEAD_DIM) for h in range(2)]


def _split_heads(ref, scr, sels):
    for h, sel in enumerate(sels):
        scr[h] = jnp.where(sel, ref[...], 0.0).astype(MXU_DTYPE)


def _sb_fwd(proj, nb, s, name):
    nblk = s // ATT

    def body(q_ref, k_ref, v_ref, o_ref, km, vm):
        sels = _head_masks()
        _split_heads(k_ref, km, sels)
        _split_heads(v_ref, vm, sels)
        rows = _iota2((ATT, ATT), 0)
        cols = _iota2((ATT, ATT), 1)
        later = (rows > cols).astype(MXU_DTYPE)

        def qblock(i, _):
            qi = (q_ref[_rows(i), :] * 0.125).astype(MXU_DTYPE)

            def kblock(t, state):
                carries, acc = state
                j = i - t
                strict = (cols + (j - i) * ATT) < rows
                out = []
                for h in range(2):
                    z = _dot_nt(qi, km[h, _rows(j), :])
                    lb, lr, _ = _log_sigmoid_parts(z)
                    lr = jnp.where(strict, lr, 0.0)
                    tail = _split_dot(lr, later, 2) + carries[h]
                    a = jnp.where(strict, jnp.exp(lb + tail), 0.0)
                    acc = acc + _dot(a, vm[h, _rows(j), :])
                    out.append(carries[h] + jnp.sum(lr, axis=-1, keepdims=True))
                return tuple(out), acc

            init = ((jnp.zeros((ATT, 1), F32),) * 2, jnp.zeros((ATT, BLK), F32))
            _, acc = lax.fori_loop(0, i + 1, kblock, init)
            o_ref[_rows(i), :] = acc
            return 0

        lax.fori_loop(0, nblk, qblock, 0)

    return pl.pallas_call(
        body, name=name, grid=(nb, 2), in_specs=[_col_spec(s, 0), _col_spec(s, 2), _col_spec(s, 4)],
        out_specs=_col_spec(s, 0), out_shape=jax.ShapeDtypeStruct((nb * s, 2 * BLK), F32),
        scratch_shapes=[pltpu.VMEM((2, s, BLK), MXU_DTYPE)] * 2, compiler_params=_cparams("parallel", "parallel"),
    )(proj, proj, proj)


def _sb_bwd(proj, dmixed, nb, s, name):
    nblk = s // ATT

    def body(q_ref, k_ref, v_ref, do_ref, dq_ref, dk_ref, dv_ref, km, vm, a_scr, dl_scr, beta_scr):
        sels = _head_masks()
        _split_heads(k_ref, km, sels)
        _split_heads(v_ref, vm, sels)
        rows = _iota2((ATT, ATT), 0)
        cols = _iota2((ATT, ATT), 1)
        later = (rows > cols).astype(MXU_DTYPE)
        earlier = (rows < cols).astype(MXU_DTYPE)
        dk_ref[...] = jnp.zeros_like(dk_ref)
        dv_ref[...] = jnp.zeros_like(dv_ref)

        def qblock(i, _):
            qi = (q_ref[_rows(i), :] * 0.125).astype(MXU_DTYPE)
            doi = do_ref[_rows(i), :].astype(MXU_DTYPE)
            qm = [jnp.where(sel, qi, 0.0) for sel in sels]
            dom = [jnp.where(sel, doi, 0.0) for sel in sels]

            def first(t, carries):
                j = i - t
                strict = (cols + (j - i) * ATT) < rows
                out = []
                for h in range(2):
                    z = _dot_nt(qi, km[h, _rows(j), :])
                    lb, lr, e = _log_sigmoid_parts(z)
                    lr = jnp.where(strict, lr, 0.0)
                    tail = _split_dot(lr, later, 2) + carries[h]
                    a = jnp.where(strict, jnp.exp(lb + tail), 0.0)
                    a_scr[h, j] = a
                    dl_scr[h, j] = a * _dot_nt(doi, vm[h, _rows(j), :])
                    beta_scr[h, j] = jnp.exp(lb)
                    out.append(carries[h] + jnp.sum(lr, axis=-1, keepdims=True))
                return tuple(out)

            lax.fori_loop(0, i + 1, first, (jnp.zeros((ATT, 1), F32),) * 2)

            def second(j, state):
                csums, dq = state
                strict = (cols + (j - i) * ATT) < rows
                out = []
                for h in range(2):
                    dl = dl_scr[h, j]
                    beta = beta_scr[h, j]
                    before = _split_dot(dl, earlier, 2) + csums[h]
                    dz = jnp.where(strict, dl * (1.0 - beta) - beta * before, 0.0).astype(MXU_DTYPE)
                    dq = dq + _dot(dz, km[h, _rows(j), :])
                    dk_ref[_rows(j), :] += _dot_tn(dz, qm[h])
                    dv_ref[_rows(j), :] += _dot_tn(a_scr[h, j], dom[h])
                    out.append(csums[h] + jnp.sum(dl, axis=-1, keepdims=True))
                return tuple(out), dq

            init = ((jnp.zeros((ATT, 1), F32),) * 2, jnp.zeros((ATT, BLK), F32))
            _, dq = lax.fori_loop(0, i + 1, second, init)
            dq_ref[_rows(i), :] = dq * 0.125
            return 0

        lax.fori_loop(0, nblk, qblock, 0)

    out = _col_spec(s, 0)
    return pl.pallas_call(
        body, name=name, grid=(nb, 2),
        in_specs=[_col_spec(s, 0), _col_spec(s, 2), _col_spec(s, 4), out], out_specs=[out] * 3,
        out_shape=[jax.ShapeDtypeStruct((nb * s, 2 * BLK), F32)] * 3,
        scratch_shapes=[pltpu.VMEM((2, s, BLK), MXU_DTYPE)] * 2 + [pltpu.VMEM((2, nblk, ATT, ATT), F32)] * 3,
        compiler_params=_cparams("parallel", "parallel"),
    )(proj, proj, proj, dmixed)


def _pair_spec(s, width):
    return pl.BlockSpec((None, 2, s, width), lambda b, p: (b, p, 0, 0))


def _fox_fwd(proj, ccol, crow, nb, s, name):
    nblk = s // ATT

    def body(q_ref, k_ref, v_ref, cc_ref, cr_ref, o_ref, lse_ref, km, vm):
        sels = _head_masks()
        _split_heads(k_ref, km, sels)
        _split_heads(v_ref, vm, sels)
        rows = _iota2((ATT, ATT), 0)
        cols = _iota2((ATT, ATT), 1)

        def qblock(i, _):
            qi = (q_ref[_rows(i), :] * 0.125).astype(MXU_DTYPE)
            ci = [cc_ref[h, _rows(i), :] for h in range(2)]

            def kblock(j, state):
                ms, ls, acc = state
                causal = (cols + (j - i) * ATT) <= rows
                new_m, new_l, scales, parts = [], [], [], []
                for h in range(2):
                    z = _dot_nt(qi, km[h, _rows(j), :]) + (ci[h] - cr_ref[h, j][0:1, :])
                    z = jnp.where(causal, z, NEG)
                    m_new = jnp.maximum(ms[h], jnp.max(z, axis=-1, keepdims=True))
                    p = jnp.exp(z - m_new)
                    scale = jnp.exp(ms[h] - m_new)
                    new_m.append(m_new)
                    new_l.append(scale * ls[h] + jnp.sum(p, axis=-1, keepdims=True))
                    scales.append(scale)
                    parts.append(_dot(p, vm[h, _rows(j), :]))
                acc = jnp.where(sels[0], scales[0], scales[1]) * acc + parts[0] + parts[1]
                return tuple(new_m), tuple(new_l), acc

            init = ((jnp.full((ATT, 1), NEG, F32),) * 2, (jnp.zeros((ATT, 1), F32),) * 2, jnp.zeros((ATT, BLK), F32))
            ms, ls, acc = lax.fori_loop(0, i + 1, kblock, init)
            o_ref[_rows(i), :] = acc / jnp.where(sels[0], ls[0], ls[1])
            for h in range(2):
                lse_ref[h, _rows(i), :] = jnp.broadcast_to(ms[h] + jnp.log(ls[h]), (ATT, ATT))
            return 0

        lax.fori_loop(0, nblk, qblock, 0)

    crow_spec = pl.BlockSpec((None, 2, nblk, 8, ATT), lambda b, p: (b, p, 0, 0, 0))
    return pl.pallas_call(
        body, name=name, grid=(nb, 2),
        in_specs=[_col_spec(s, 12), _col_spec(s, 14), _col_spec(s, 16), _pair_spec(s, ATT), crow_spec],
        out_specs=[_col_spec(s, 0), _pair_spec(s, ATT)],
        out_shape=[jax.ShapeDtypeStruct((nb * s, 2 * BLK), F32), jax.ShapeDtypeStruct((nb, N_HEADS, s, ATT), F32)],
        scratch_shapes=[pltpu.VMEM((2, s, BLK), MXU_DTYPE)] * 2, compiler_params=_cparams("parallel", "parallel"),
    )(proj, proj, proj, ccol, crow)


def _fox_bwd(proj, dmixed, lse, ccol, crow, nb, s, name):
    nblk = s // ATT

    def body(q_ref, k_ref, v_ref, do_ref, lse_ref, cc_ref, cr_ref, dq_ref, dk_ref, dv_ref, dc_ref, km, vm):
        sels = _head_masks()
        _split_heads(k_ref, km, sels)
        _split_heads(v_ref, vm, sels)
        rows = _iota2((ATT, ATT), 0)
        cols = _iota2((ATT, ATT), 1)
        dk_ref[...] = jnp.zeros_like(dk_ref)
        dv_ref[...] = jnp.zeros_like(dv_ref)
        dc_ref[...] = jnp.zeros_like(dc_ref)

        def qblock(i, _):
            qi = (q_ref[_rows(i), :] * 0.125).astype(MXU_DTYPE)
            doi = do_ref[_rows(i), :].astype(MXU_DTYPE)
            qm = [jnp.where(sel, qi, 0.0) for sel in sels]
            dom = [jnp.where(sel, doi, 0.0) for sel in sels]
            ci = [cc_ref[h, _rows(i), :] for h in range(2)]
            lsei = [lse_ref[h, _rows(i), :] for h in range(2)]

            def probs(j, h):
                z = _dot_nt(qi, km[h, _rows(j), :]) + (ci[h] - cr_ref[h, j][0:1, :])
                p = jnp.where((cols + (j - i) * ATT) <= rows, jnp.exp(z - lsei[h]), 0.0)
                return p, _dot_nt(doi, vm[h, _rows(j), :])

            def row_term(j, accs):
                out = []
                for h in range(2):
                    p, dp = probs(j, h)
                    out.append(accs[h] + jnp.sum(p * dp, axis=-1, keepdims=True))
                return tuple(out)

            di = lax.fori_loop(0, i + 1, row_term, (jnp.zeros((ATT, 1), F32),) * 2)

            def kblock(j, dq):
                for h in range(2):
                    p, dp = probs(j, h)
                    ds = p * (dp - di[h])
                    dc_ref[h, j] += jnp.broadcast_to(jnp.sum(ds, axis=0, keepdims=True), (8, ATT))
                    ds = ds.astype(MXU_DTYPE)
                    dk_ref[_rows(j), :] += _dot_tn(ds, qm[h])
                    dv_ref[_rows(j), :] += _dot_tn(p, dom[h])
                    dq = dq + _dot(ds, km[h, _rows(j), :])
                return dq

            dq = lax.fori_loop(0, i + 1, kblock, jnp.zeros((ATT, BLK), F32))
            dq_ref[_rows(i), :] = dq * 0.125
            return 0

        lax.fori_loop(0, nblk, qblock, 0)

    crow_spec = pl.BlockSpec((None, 2, nblk, 8, ATT), lambda b, p: (b, p, 0, 0, 0))
    wide, cols_out = _pair_spec(s, ATT), _col_spec(s, 0)
    return pl.pallas_call(
        body, name=name, grid=(nb, 2),
        in_specs=[_col_spec(s, 12), _col_spec(s, 14), _col_spec(s, 16), _col_spec(s, 4), wide, wide, crow_spec],
        out_specs=[cols_out, cols_out, cols_out, crow_spec],
        out_shape=[jax.ShapeDtypeStruct((nb * s, 2 * BLK), F32)] * 3 + [jax.ShapeDtypeStruct((nb, N_HEADS, nblk, 8, ATT), F32)],
        scratch_shapes=[pltpu.VMEM((2, s, BLK), MXU_DTYPE)] * 2, compiler_params=_cparams("parallel", "parallel"),
    )(proj, proj, proj, dmixed, lse, ccol, crow)


def _fox_gates_fwd(proj, f_bias, nb, s, name):
    chunk = 256

    def body(f_ref, b_ref, c_ref):
        lower = (_iota2((chunk, chunk), 0) >= _iota2((chunk, chunk), 1)).astype(MXU_DTYPE)
        carry = jnp.zeros((1, BLK), F32)
        for n in range(s // chunk):
            rows = pl.ds(n * chunk, chunk)
            lf, _, _ = _log_sigmoid_parts(f_ref[rows, :] + b_ref[0:1, :])
            c = _split_dot_lhs(lower, lf, 3) + carry
            c_ref[rows, :] = c
            carry = c[chunk - 1:chunk, :]

    return pl.pallas_call(
        body, name=name, grid=(nb,),
        in_specs=[pl.BlockSpec((s, BLK), lambda b: (b, (PROJ_PAD - BLK) // BLK)), pl.BlockSpec((8, BLK), lambda b: (0, 0))],
        out_specs=pl.BlockSpec((s, BLK), lambda b: (b, 0)),
        out_shape=jax.ShapeDtypeStruct((nb * s, BLK), F32), compiler_params=_cparams("parallel"),
    )(proj, f_bias)


def _fox_gates_bwd(dc, proj, f_bias, nb, s, name):
    chunk = 256

    def body(dc_ref, f_ref, b_ref, df_ref, db_ref):
        upper = (_iota2((chunk, chunk), 0) <= _iota2((chunk, chunk), 1)).astype(MXU_DTYPE)
        carry = jnp.zeros((1, BLK), F32)
        total = jnp.zeros((1, BLK), F32)
        for n in reversed(range(s // chunk)):
            rows = pl.ds(n * chunk, chunk)
            dlf = _split_dot_lhs(upper, dc_ref[rows, :], 3) + carry
            carry = dlf[0:1, :]
            pre = f_ref[rows, :] + b_ref[0:1, :]
            e = jnp.exp(-jnp.abs(pre))
            df = dlf * (jnp.where(pre >= 0.0, e, 1.0) / (1.0 + e))
            df_ref[rows, :] = df
            total = total + jnp.sum(df, axis=0, keepdims=True)

        @pl.when(pl.program_id(0) == 0)
        def _():
            db_ref[...] = jnp.zeros_like(db_ref)

        db_ref[0:1, :] += total

    return pl.pallas_call(
        body, name=name, grid=(nb,),
        in_specs=[pl.BlockSpec((s, BLK), lambda b: (b, 0)), pl.BlockSpec((s, BLK), lambda b: (b, (PROJ_PAD - BLK) // BLK)),
                  pl.BlockSpec((8, BLK), lambda b: (0, 0))],
        out_specs=[pl.BlockSpec((s, BLK), lambda b: (b, 0)), pl.BlockSpec((8, BLK), lambda b: (0, 0))],
        out_shape=[jax.ShapeDtypeStruct((nb * s, BLK), F32), jax.ShapeDtypeStruct((8, BLK), F32)],
        compiler_params=_cparams("arbitrary"),
    )(dc, proj, f_bias)


def _delta_kernel(dmixed, o, nb, s, name):
    def body(do_ref, o_ref, d_ref):
        prod = do_ref[...] * o_ref[...]
        for h, sel in enumerate(_head_masks()):
            d_ref[h] = jnp.broadcast_to(jnp.sum(jnp.where(sel, prod, 0.0), axis=-1, keepdims=True), (s, BLK))

    return pl.pallas_call(
        body, name=name, grid=(nb, 2), in_specs=[_col_spec(s, 2), _col_spec(s, 0)], out_specs=_pair_spec(s, BLK),
        out_shape=jax.ShapeDtypeStruct((nb, N_HEADS, s, BLK), F32), compiler_params=_cparams("parallel", "parallel"),
    )(dmixed, o)


def _t5_bucket_np(dist):
    max_exact = REL_BUCKETS // 2
    nf = np.maximum(dist, 1).astype(np.float32)
    large = max_exact + (np.log(nf / max_exact) / math.log(2048 / max_exact) * (REL_BUCKETS - max_exact)).astype(np.int32)
    large = np.minimum(large, REL_BUCKETS - 1)
    return np.where(dist < max_exact, dist, large)


def _bucket_table():
    qi = np.arange(BLK)[:, None]
    kj = np.arange(2 * BLK)[None, :]
    dist = qi + BLK - kj
    tables = []
    for window, dil in DIL_PATTERNS:
        in_band = (dist >= 0) & (dist <= window // dil)
        tables.append(np.where(in_band, _t5_bucket_np(np.maximum(dist, 0) * dil), -1).astype(np.int32))
    return np.stack(tables)


def _to_residue(t, dil):
    if dil == 1:
        return t
    *lead, s, e = t.shape
    return jnp.swapaxes(t.reshape(*lead, s // dil, dil, e), -3, -2).reshape(*lead, s, e)


def _from_residue(t, dil):
    if dil == 1:
        return t
    *lead, s, e = t.shape
    return jnp.swapaxes(t.reshape(*lead, dil, s // dil, e), -3, -2).reshape(*lead, s, e)


def _pat_col_spec(s, base=0):
    return pl.BlockSpec((3, None, s, BLK), lambda b, p: (0, b, 0, base + p))


def _pat_pair_spec(s):
    return pl.BlockSpec((3, None, 2, s, BLK), lambda b, p: (0, b, p, 0, 0))


def _one_pat_col_spec(s, base=0):
    return pl.BlockSpec((None, None, s, BLK), lambda b, p, t: (t, b, 0, base + p))


def _one_pat_pair_spec(s):
    return pl.BlockSpec((None, None, 2, s, BLK), lambda b, p, t: (t, b, p, 0, 0))


def _one_pat_bias_spec():
    return pl.BlockSpec((None, 2, BLK, 2 * BLK), lambda b, p, t: (t, p, 0, 0))


def _blocks_per_class(s):
    t = pl.program_id(2)
    segs = [s // dil // BLK for _, dil in DIL_PATTERNS]
    return jnp.where(t == 0, segs[0], jnp.where(t == 1, segs[1], segs[2]))


def _dil_scores(qb, kp, kc, b_ref, h, prev_valid):
    zp = _dot_nt(qb, kp) + b_ref[h, :, 0:BLK]
    zp = jnp.where(prev_valid, zp, NEG)
    zc = _dot_nt(qb, kc) + b_ref[h, :, BLK:2 * BLK]
    return zp, zc


def _dil_fwd(qkvp, bias, name):
    _, nb, s, _ = qkvp.shape
    nblk = s // BLK

    def body(q_ref, k_ref, v_ref, b_ref, o_ref, lse_ref):
        sels = _head_masks()
        seg = _blocks_per_class(s)

        def block(b, _):
            cur = _rows128(b)
            prev = _rows128(jnp.maximum(b - 1, 0))
            qb = q_ref[cur, :] * 0.125
            kp, kc, vp, vc = k_ref[prev, :], k_ref[cur, :], v_ref[prev, :], v_ref[cur, :]
            acc = jnp.zeros((BLK, BLK), F32)
            for h, sel in enumerate(sels):
                zp, zc = _dil_scores(qb, jnp.where(sel, kp, 0.0), jnp.where(sel, kc, 0.0), b_ref, h, b % seg > 0)
                m = jnp.maximum(jnp.max(zp, axis=-1, keepdims=True), jnp.max(zc, axis=-1, keepdims=True))
                pp = jnp.exp(zp - m)
                pc = jnp.exp(zc - m)
                den = jnp.sum(pp, axis=-1, keepdims=True) + jnp.sum(pc, axis=-1, keepdims=True)
                acc = acc + (_dot(pp, jnp.where(sel, vp, 0.0)) + _dot(pc, jnp.where(sel, vc, 0.0))) / den
                lse_ref[h, cur, :] = jnp.broadcast_to(m + jnp.log(den), (BLK, BLK))
            o_ref[cur, :] = acc
            return 0

        lax.fori_loop(0, nblk, block, 0, unroll=2)

    return pl.pallas_call(
        body, name=name, grid=(nb, 2, len(DIL_PATTERNS)),
        in_specs=[_one_pat_col_spec(s, 0), _one_pat_col_spec(s, 2), _one_pat_col_spec(s, 4), _one_pat_bias_spec()],
        out_specs=[_one_pat_col_spec(s), _one_pat_pair_spec(s)],
        out_shape=[jax.ShapeDtypeStruct((3, nb, s, 2 * BLK), F32), jax.ShapeDtypeStruct((3, nb, N_HEADS, s, BLK), F32)],
        compiler_params=_cparams("parallel", "parallel", "parallel"),
    )(qkvp, qkvp, qkvp, bias)


def _dil_combine(o, lse, name):
    _, nb, s, _ = o.shape

    def body(o_ref, l_ref, out_ref, lse_ref):
        sels = _head_masks()
        weights, dens = [], []
        for h in range(2):
            m = jnp.maximum(jnp.maximum(l_ref[0, h], l_ref[1, h]), l_ref[2, h])
            w = [jnp.exp(l_ref[p, h] - m) for p in range(3)]
            den = w[0] + w[1] + w[2]
            lse_ref[h] = m + jnp.log(den)
            weights.append(w)
            dens.append(den)
        num = sum(jnp.where(sels[0], weights[0][p], weights[1][p]) * o_ref[p] for p in range(3))
        out_ref[...] = num / jnp.where(sels[0], dens[0], dens[1])

    return pl.pallas_call(
        body, name=name, grid=(nb, 2), in_specs=[_pat_col_spec(s), _pat_pair_spec(s)],
        out_specs=[_col_spec(s, 0), _pair_spec(s, BLK)],
        out_shape=[jax.ShapeDtypeStruct((nb * s, 2 * BLK), F32), jax.ShapeDtypeStruct((nb, N_HEADS, s, BLK), F32)],
        compiler_params=_cparams("parallel", "parallel"),
    )(o, lse)


def _dil_bwd(qkvp, dop, lsep, deltap, bias, name):
    _, nb, s, _ = qkvp.shape
    nblk = s // BLK

    def body(q_ref, k_ref, v_ref, do_ref, lse_ref, dl_ref, b_ref, dq_ref, dk_ref, dv_ref, g_ref):
        sels = _head_masks()
        seg = _blocks_per_class(s)
        dk_ref[...] = jnp.zeros_like(dk_ref)
        dv_ref[...] = jnp.zeros_like(dv_ref)
        g_ref[...] = jnp.zeros_like(g_ref)

        def block(b, _):
            cur = _rows128(b)
            prev = _rows128(jnp.maximum(b - 1, 0))
            qb = q_ref[cur, :] * 0.125
            dob = do_ref[cur, :]
            kp, kc, vp, vc = k_ref[prev, :], k_ref[cur, :], v_ref[prev, :], v_ref[cur, :]
            dq = jnp.zeros((BLK, BLK), F32)
            for h, sel in enumerate(sels):
                kph, kch = jnp.where(sel, kp, 0.0), jnp.where(sel, kc, 0.0)
                qh, doh = jnp.where(sel, qb, 0.0), jnp.where(sel, dob, 0.0)
                lse = lse_ref[h, cur, :]
                dlt = dl_ref[h, cur, :]
                zp, zc = _dil_scores(qb, kph, kch, b_ref, h, b % seg > 0)
                pp = jnp.exp(zp - lse)
                pc = jnp.exp(zc - lse)
                dsp = pp * (_dot_nt(dob, jnp.where(sel, vp, 0.0)) - dlt)
                dsc = pc * (_dot_nt(dob, jnp.where(sel, vc, 0.0)) - dlt)
                g_ref[h, :, 0:BLK] += dsp
                g_ref[h, :, BLK:2 * BLK] += dsc
                dsp = dsp.astype(MXU_DTYPE)
                dsc = dsc.astype(MXU_DTYPE)
                dq = dq + _dot(dsp, kph) + _dot(dsc, kch)
                dk_ref[prev, :] += _dot_tn(dsp, qh)
                dk_ref[cur, :] += _dot_tn(dsc, qh)
                dv_ref[prev, :] += _dot_tn(pp, doh)
                dv_ref[cur, :] += _dot_tn(pc, doh)
            dq_ref[cur, :] = dq * 0.125
            return 0

        lax.fori_loop(0, nblk, block, 0, unroll=2)

    cols, stats = _one_pat_col_spec(s), _one_pat_pair_spec(s)
    return pl.pallas_call(
        body, name=name, grid=(nb, 2, len(DIL_PATTERNS)),
        in_specs=[_one_pat_col_spec(s, 0), _one_pat_col_spec(s, 2), _one_pat_col_spec(s, 4), cols, stats, stats,
                  _one_pat_bias_spec()],
        out_specs=[cols, cols, cols, pl.BlockSpec((None, 2, None, BLK, 2 * BLK), lambda b, p, t: (b, p, t, 0, 0))],
        out_shape=[jax.ShapeDtypeStruct((3, nb, s, 2 * BLK), F32)] * 3 + [jax.ShapeDtypeStruct((nb, N_HEADS, 3, BLK, 2 * BLK), F32)],
        compiler_params=_cparams("parallel", "parallel", "parallel"),
    )(qkvp, qkvp, qkvp, dop, lsep, deltap, bias)


def _bucket_reduce(gbias, table, name):
    nb = gbias.shape[0]

    def body(g_ref, t_ref, o_ref):
        row = _iota2((8, BLK), 0)
        lane = _iota2((8, BLK), 1)
        gsum = [[sum(g_ref[b, h, p] for b in range(nb)) for p in range(3)] for h in range(N_HEADS)]

        def bucket(k, acc):
            for h in range(N_HEADS):
                tot = sum(jnp.sum(jnp.where(t_ref[p] == k, gsum[h][p], 0.0)) for p in range(3))
                acc = acc + jnp.where((row == h) & (lane == k), tot, 0.0)
            return acc

        o_ref[...] = lax.fori_loop(0, REL_BUCKETS, bucket, jnp.zeros((8, BLK), F32))

    vm = pl.BlockSpec(memory_space=pltpu.VMEM)
    return pl.pallas_call(
        body, name=name, in_specs=[vm, vm], out_specs=vm, out_shape=jax.ShapeDtypeStruct((8, BLK), F32),
        compiler_params=pltpu.CompilerParams(vmem_limit_bytes=VMEM_LIMIT),
    )(gbias, table)


def _place():
    x, y, c = lax.axis_index("x"), lax.axis_index("y"), lax.axis_index("c")
    others = [(1 - x, y), (x, 1 - y), (1 - x, 1 - y)]
    return x, y, c, others


def _remote(src, dst, send_sem, recv_sem, to):
    return pltpu.make_async_remote_copy(src_ref=src, dst_ref=dst, send_sem=send_sem, recv_sem=recv_sem,
                                        device_id=to, device_id_type=MESH)


_HBM = pl.BlockSpec(memory_space=pl.ANY)


def _gather_weights(shards):
    n = len(shards)

    def body(*refs):
        ins, outs, (send_sems, recv_sems) = refs[:n], refs[n:2 * n], refs[2 * n:]
        x, y, c, others = _place()
        me = 2 * x + y
        sibling = (x, y, 1 - c)
        sends = [_remote(ins[a].at[c], outs[a].at[c, me], send_sems.at[6 * a + k], recv_sems.at[6 * a + k], (ox, oy, c))
                 for a in range(n) for k, (ox, oy) in enumerate(others)]
        for cp in sends:
            cp.start()
        passed = []
        for k, (ox, oy) in enumerate(others):
            for a in range(n):
                landed = outs[a].at[c, 2 * ox + oy]
                _remote(landed, landed, send_sems.at[6 * a + k], recv_sems.at[6 * a + k], (ox, oy, c)).wait_recv()
                cp = _remote(landed, landed, send_sems.at[6 * a + 3 + k], recv_sems.at[6 * a + 3 + k], sibling)
                cp.start()
                passed.append(cp)
        for k, (ox, oy) in enumerate(others):
            for a in range(n):
                theirs = outs[a].at[1 - c, 2 * ox + oy]
                _remote(theirs, theirs, send_sems.at[6 * a + 3 + k], recv_sems.at[6 * a + 3 + k], sibling).wait_recv()
        for cp in sends + passed:
            cp.wait_send()

    return pl.pallas_call(
        body, name="gather_weights", in_specs=[_HBM] * n, out_specs=[_HBM] * n,
        out_shape=[jax.ShapeDtypeStruct((t.shape[0], N_CHIPS) + t.shape[1:], t.dtype) for t in shards],
        scratch_shapes=[pltpu.SemaphoreType.DMA((6 * n,)), pltpu.SemaphoreType.DMA((6 * n,))],
    )(*shards)


def _swap_layers(g0, g1):
    n = len(g0)

    def body(*refs):
        a0, a1, outs, (send_sems, recv_sems) = refs[:n], refs[n:2 * n], refs[2 * n:3 * n], refs[3 * n:]
        x, y, c, _ = _place()
        sibling = (x, y, 1 - c)

        @pl.when(c == 0)
        def _():
            for a in range(n):
                _remote(a1[a], outs[a], send_sems.at[a], recv_sems.at[a], sibling).start()

        @pl.when(c == 1)
        def _():
            for a in range(n):
                _remote(a0[a], outs[a], send_sems.at[a], recv_sems.at[a], sibling).start()

        for a in range(n):
            _remote(a0[a], outs[a], send_sems.at[a], recv_sems.at[a], sibling).wait()

    return pl.pallas_call(
        body, name="swap_layers", in_specs=[_HBM] * (2 * n), out_specs=[_HBM] * n,
        out_shape=[jax.ShapeDtypeStruct(t.shape, t.dtype) for t in g0],
        scratch_shapes=[pltpu.SemaphoreType.DMA((n,)), pltpu.SemaphoreType.DMA((n,))],
    )(*g0, *g1)


def _row_tile(r):
    for cand in (256, 352):
        if r % cand == 0:
            return cand
    return r


def _pair_sum(g0, g1, other, core, name):
    ns, r, w = other.shape
    tr = _row_tile(r)

    def body(core_ref, g0_ref, g1_ref, o_ref, out_ref):
        mine = jnp.where(core_ref[0] == 0, g0_ref[...], g1_ref[...])
        out_ref[...] = (mine + o_ref[...]).astype(out_ref.dtype)

    blk = pl.BlockSpec((None, tr, w), lambda k, i, core_ref: (k, i, 0))
    grid_spec = pltpu.PrefetchScalarGridSpec(num_scalar_prefetch=1, grid=(ns, r // tr), in_specs=[blk] * 3, out_specs=blk)
    return pl.pallas_call(
        body, name=name, grid_spec=grid_spec, out_shape=jax.ShapeDtypeStruct((ns, r, w), MXU_DTYPE),
        compiler_params=_cparams("parallel", "parallel"),
    )(core.reshape(1).astype(jnp.int32), g0, g1, other)


def _scatter_shards(ps):
    n = len(ps)

    def body(*refs):
        p_refs, q_refs, (send_sems, recv_sems) = refs[:n], refs[n:2 * n], refs[2 * n:]
        x, y, c, others = _place()
        me = 2 * x + y
        sends = [_remote(p_refs[a].at[2 * ox + oy], q_refs[a].at[me], send_sems.at[3 * a + k], recv_sems.at[3 * a + k],
                         (ox, oy, c)) for a in range(n) for k, (ox, oy) in enumerate(others)]
        for cp in sends:
            cp.start()
        for k, (ox, oy) in enumerate(others):
            for a in range(n):
                slot = q_refs[a].at[2 * ox + oy]
                _remote(slot, slot, send_sems.at[3 * a + k], recv_sems.at[3 * a + k], (ox, oy, c)).wait_recv()
        for cp in sends:
            cp.wait_send()

    return pl.pallas_call(
        body, name="scatter_shards", in_specs=[_HBM] * n, out_specs=[_HBM] * n,
        out_shape=[jax.ShapeDtypeStruct(t.shape, t.dtype) for t in ps],
        scratch_shapes=[pltpu.SemaphoreType.DMA((3 * n,)), pltpu.SemaphoreType.DMA((3 * n,))],
    )(*ps)


def _chip_sum(q, p, chip, name):
    ns, r, w = q.shape
    tr = _row_tile(r)

    def body(chip_ref, q_ref, own_ref, out_ref):
        me = chip_ref[0]
        own = own_ref[...].astype(F32)
        terms = [jnp.where(me == k, own, q_ref[k].astype(F32)) for k in range(ns)]
        out_ref[...] = ((terms[0] + terms[1]) + terms[2]) + terms[3]

    grid_spec = pltpu.PrefetchScalarGridSpec(
        num_scalar_prefetch=1, grid=(r // tr,),
        in_specs=[pl.BlockSpec((ns, tr, w), lambda i, chip_ref: (0, i, 0)),
                  pl.BlockSpec((None, tr, w), lambda i, chip_ref: (chip_ref[0], i, 0))],
        out_specs=pl.BlockSpec((tr, w), lambda i, chip_ref: (i, 0)))
    return pl.pallas_call(
        body, name=name, grid_spec=grid_spec, out_shape=jax.ShapeDtypeStruct((r, w), F32),
        compiler_params=_cparams("parallel"),
    )(chip.reshape(1).astype(jnp.int32), q, p)


def _share_layers(mine):
    n = len(mine)

    def body(*refs):
        ins, outs, (send_sems, recv_sems) = refs[:n], refs[n:2 * n], refs[2 * n:]
        x, y, c, _ = _place()
        copies = [_remote(ins[a], outs[a], send_sems.at[a], recv_sems.at[a], (x, y, 1 - c)) for a in range(n)]
        for cp in copies:
            cp.start()
        for cp in copies:
            cp.wait()

    return pl.pallas_call(
        body, name="share_layers", in_specs=[_HBM] * n, out_specs=[_HBM] * n,
        out_shape=[jax.ShapeDtypeStruct(t.shape, t.dtype) for t in mine],
        scratch_shapes=[pltpu.SemaphoreType.DMA((n,)), pltpu.SemaphoreType.DMA((n,))],
    )(*mine)


def _gather_small(pk, name):
    rows, w = pk.shape

    def body(pk_ref, all_ref, sum_ref, send_sems, recv_sems):
        x, y, c, _ = _place()
        me = 4 * x + 2 * y + c
        all_ref[me] = pk_ref[...]
        flips = [(fx, fy, fc) for fx in (0, 1) for fy in (0, 1) for fc in (0, 1)][1:]
        peers = [(x ^ fx, y ^ fy, c ^ fc) for fx, fy, fc in flips]
        sends = [_remote(pk_ref, all_ref.at[me], send_sems.at[k], recv_sems.at[k], peer) for k, peer in enumerate(peers)]
        for cp in sends:
            cp.start()
        for k, (px, py, pc) in enumerate(peers):
            slot = all_ref.at[4 * px + 2 * py + pc]
            _remote(slot, slot, send_sems.at[k], recv_sems.at[k], (px, py, pc)).wait_recv()
        for cp in sends:
            cp.wait_send()
        total = all_ref[0]
        for d in range(1, N_DEV):
            total = total + all_ref[d]
        sum_ref[...] = total

    vm = pl.BlockSpec(memory_space=pltpu.VMEM)
    return pl.pallas_call(
        body, name=name, in_specs=[vm], out_specs=[vm, vm],
        out_shape=[jax.ShapeDtypeStruct((N_DEV, rows, w), F32), jax.ShapeDtypeStruct((rows, w), F32)],
        scratch_shapes=[pltpu.SemaphoreType.DMA((7,)), pltpu.SemaphoreType.DMA((7,))],
    )(pk)


def _lanes(t):
    return jnp.broadcast_to(t, t.shape[:-1] + (BLK,))


def _row_layout(c, nb, s):
    ch = jnp.swapaxes(c[:, :N_HEADS].reshape(nb, s, N_HEADS), 1, 2)
    ccol = jnp.broadcast_to(ch[..., None], (nb, N_HEADS, s, ATT))
    crow = jnp.broadcast_to(ch.reshape(nb, N_HEADS, s // ATT, 1, ATT), (nb, N_HEADS, s // ATT, 8, ATT))
    return ccol, crow


def _dil_bias(rel_bias, name):
    def body(rel_ref, t_ref, o_ref):
        for p in range(len(DIL_PATTERNS)):
            table = t_ref[p]

            def bucket(k, accs, table=table):
                return tuple(jnp.where(table == k, rel_ref[k, h], acc) for h, acc in enumerate(accs))

            accs = lax.fori_loop(0, REL_BUCKETS, bucket, tuple(jnp.full((BLK, 2 * BLK), NEG, F32) for _ in range(N_HEADS)))
            for h in range(N_HEADS):
                o_ref[p, h] = accs[h]

    vm = pl.BlockSpec(memory_space=pltpu.VMEM)
    return pl.pallas_call(
        body, name=name, in_specs=[pl.BlockSpec(memory_space=pltpu.SMEM), vm], out_specs=vm,
        out_shape=jax.ShapeDtypeStruct((len(DIL_PATTERNS), N_HEADS, BLK, 2 * BLK), F32),
        compiler_params=pltpu.CompilerParams(vmem_limit_bytes=VMEM_LIMIT),
    )(rel_bias, jnp.asarray(_bucket_table()))


def _layer_forward(x, wts, small, nb, s, tag):
    proj = _matmul(x, wts["w_in"], "proj", tag)

    o_sb = _sb_fwd(proj, nb, s, f"sb_fwd_{tag}")

    dl = proj[:, 3 * CONV_W:6 * CONV_W].reshape(nb, s, 3 * CONV_W).astype(MXU_DTYPE)
    qkvp = jnp.stack([_to_residue(dl, dil) for _, dil in DIL_PATTERNS])
    bias = _dil_bias(small["rel_bias"], f"dil_bias_{tag}")
    o_p, lse_p = _dil_fwd(qkvp, bias, f"dil_fwd_{tag}")
    o_nat = jnp.stack([_from_residue(o_p[p], dil) for p, (_, dil) in enumerate(DIL_PATTERNS)])
    lse_nat = _lanes(jnp.stack([_from_residue(lse_p[p][..., :1], dil) for p, (_, dil) in enumerate(DIL_PATTERNS)]))
    o_dl, lse_dl = _dil_combine(o_nat, lse_nat, f"dil_mix_{tag}")

    fb = jnp.zeros((8, BLK), F32).at[0, :N_HEADS].set(small["f_bias"])
    csum = _fox_gates_fwd(proj, fb, nb, s, f"fox_gates_{tag}")
    ccol, crow = _row_layout(csum, nb, s)
    o_fx, lse_fx = _fox_fwd(proj, ccol, crow, nb, s, f"fox_fwd_{tag}")

    cw = jnp.zeros((8, CONV_W), F32).at[:3].set(small["conv_w"])
    o_cv = _conv_fwd(proj, cw, nb, s, f"conv_fwd_{tag}")

    mixed = jnp.concatenate([o_sb, o_dl, o_fx, o_cv], axis=-1).astype(MXU_DTYPE)
    mix = _matmul(mixed, wts["w_out"], "out_proj", tag)
    pre1, x1 = _ln_fwd(x, mix, small["ln1_g"], small["ln1_b"], f"ln1_fwd_{tag}")
    gu = _matmul(x1, wts["w_gu"], "ffn_in", tag)
    hid = _swiglu_fwd(gu, f"swiglu_fwd_{tag}")
    ffn = _matmul(hid, wts["w_down"], "ffn_out", tag)
    pre2, x2 = _ln_fwd(x1, ffn, small["ln2_g"], small["ln2_b"], f"ln2_fwd_{tag}")
    saved = dict(x=x, proj=proj, qkvp=qkvp, bias=bias, o_dl=o_dl, lse_dl=lse_dl, fb=fb, ccol=ccol, crow=crow,
                 o_fx=o_fx, lse_fx=lse_fx, cw=cw, mixed=mixed, pre1=pre1, x1=x1, gu=gu, hid=hid, pre2=pre2)
    return x2, saved


def _layer_backward(dx2, sv, wts, small, nb, s, tag):
    t = nb * s
    dpre2, dgb2 = _ln_bwd(dx2, sv["pre2"], small["ln2_g"], f"ln2_bwd_{tag}")
    dpre2_b = dpre2.astype(MXU_DTYPE)
    dhid = _matmul(dpre2_b, wts["w_down"], "ffn_out_dx", tag, trans_b=True)
    dw_down = _matmul(sv["hid"].T, dpre2_b, "ffn_out_dw", tag)
    dgu = _swiglu_bwd(dhid, sv["gu"], f"swiglu_bwd_{tag}")
    dx1 = _matmul(dgu, wts["w_gu"], "ffn_in_dx", tag, add=dpre2, add_scale=ALPHA, trans_b=True)
    dw_gu = _matmul(sv["x1"].astype(MXU_DTYPE).T, dgu, "ffn_in_dw", tag)

    dpre1, dgb1 = _ln_bwd(dx1, sv["pre1"], small["ln1_g"], f"ln1_bwd_{tag}")
    dpre1_b = dpre1.astype(MXU_DTYPE)
    dmixed = _matmul(dpre1_b, wts["w_out"], "out_proj_dx", tag, trans_b=True)
    dw_out = _matmul(sv["mixed"].T, dpre1_b, "out_proj_dw", tag)
    proj = sv["proj"]

    dq_sb, dk_sb, dv_sb = _sb_bwd(proj, dmixed, nb, s, f"sb_bwd_{tag}")

    delta_dl = _delta_kernel(dmixed, sv["o_dl"], nb, s, f"dil_delta_{tag}")
    do_dl = dmixed[:, CONV_W:2 * CONV_W].reshape(nb, s, CONV_W).astype(MXU_DTYPE)
    dop = jnp.stack([_to_residue(do_dl, dil) for _, dil in DIL_PATTERNS])
    lsep = _lanes(jnp.stack([_to_residue(sv["lse_dl"][..., :1], dil) for _, dil in DIL_PATTERNS]))
    deltap = _lanes(jnp.stack([_to_residue(delta_dl[..., :1], dil) for _, dil in DIL_PATTERNS]))
    dqp, dkp, dvp, gbias = _dil_bwd(sv["qkvp"], dop, lsep, deltap, sv["bias"], f"dil_bwd_{tag}")
    unperm = lambda tp: sum(_from_residue(tp[p], dil) for p, (_, dil) in enumerate(DIL_PATTERNS)).reshape(t, CONV_W)
    dq_dl, dk_dl, dv_dl = unperm(dqp), unperm(dkp), unperm(dvp)
    drel = _bucket_reduce(gbias, jnp.asarray(_bucket_table()), f"rel_bias_grad_{tag}")

    dq_fx, dk_fx, dv_fx, dcol = _fox_bwd(proj, dmixed, sv["lse_fx"], sv["ccol"], sv["crow"], nb, s, f"fox_bwd_{tag}")
    dcs = -jnp.swapaxes(dcol[:, :, :, 0, :].reshape(nb, N_HEADS, s), 1, 2).reshape(t, N_HEADS)
    dcs = jnp.pad(dcs, ((0, 0), (0, BLK - N_HEADS)))
    dfx, dfb = _fox_gates_bwd(dcs, proj, sv["fb"], nb, s, f"fox_gates_bwd_{tag}")

    dgates, dcw = _conv_bwd(dmixed, proj, sv["cw"], nb, s, f"conv_bwd_{tag}")

    dproj = jnp.concatenate([dq_sb, dk_sb, dv_sb, dq_dl, dk_dl, dv_dl, dq_fx, dk_fx, dv_fx, dgates, dfx],
                            axis=-1).astype(MXU_DTYPE)
    dx = _matmul(dproj, wts["w_in"], "proj_dx", tag, add=dpre1, add_scale=ALPHA, trans_b=True)
    dw_in = _matmul(sv["x"].astype(MXU_DTYPE).T, dproj, "proj_dw", tag)

    grads = dict(w_in=dw_in[:, :PROJ], w_out=dw_out, w_gate=dw_gu[:, :D_FF], w_up=dw_gu[:, D_FF:], w_down=dw_down,
                 ln1_g=dgb1[0], ln1_b=dgb1[1], ln2_g=dgb2[0], ln2_b=dgb2[1], conv_w=dcw[:3], f_bias=dfb[0, :N_HEADS],
                 rel_bias=drel[:N_HEADS, :REL_BUCKETS].T)
    return dx, grads


def _local_step(x, target, full, small_all):
    nb, s, d = x.shape
    h = x.reshape(nb * s, d)
    saved = []
    for layer in range(DEPTH):
        h, sv = _layer_forward(h, full[layer], small_all[layer], nb, s, f"l{layer}")
        saved.append(sv)
    dy, lossp = _loss_kernel(h, target.reshape(nb * s, d), "loss")
    grads = [None] * DEPTH
    for layer in reversed(range(DEPTH)):
        dy, grads[layer] = _layer_backward(dy, saved[layer], full[layer], small_all[layer], nb, s, f"l{layer}")
    return lossp, dy.reshape(nb, s, d), grads


_BIG = ("w_in", "w_out", "w_gate", "w_up", "w_down")
_COL_SHARDED = ("w_in", "w_gate", "w_up")


def _full_weights(gathered):
    cols = lambda t: jnp.swapaxes(t, 1, 2).reshape(DEPTH, t.shape[2], -1)
    rows = lambda t: t.reshape(DEPTH, -1, t.shape[3])
    w_in = jnp.pad(cols(gathered["w_in"]), ((0, 0), (0, 0), (0, PROJ_PAD - PROJ)))
    w_gu = jnp.concatenate([cols(gathered["w_gate"]), cols(gathered["w_up"])], axis=-1)
    w_out, w_down = rows(gathered["w_out"]), rows(gathered["w_down"])
    return [dict(w_in=w_in[l], w_out=w_out[l], w_gu=w_gu[l], w_down=w_down[l]) for l in range(DEPTH)]


def _by_chip(name, g):
    if name in _COL_SHARDED:
        return jnp.swapaxes(g.reshape(g.shape[0], N_CHIPS, -1), 0, 1)
    return g.reshape(N_CHIPS, -1, g.shape[1])


_SMALL_LAYOUT = (("ln1_g", 0), ("ln1_b", 2), ("ln2_g", 4), ("ln2_b", 6), ("conv_w", 8))
_ROW_MISC = 10
_ROW_LOSS = 11


def _pack_small(per_layer, rel_bias, loss=None):
    pk = jnp.zeros((SMALL_ROWS, D_MODEL), F32)
    for name, row in _SMALL_LAYOUT:
        for l in range(DEPTH):
            v = per_layer[l][name].reshape(-1)
            pk = pk.at[row + l, :v.shape[0]].set(v)
    fb = jnp.concatenate([per_layer[l]["f_bias"] for l in range(DEPTH)])
    pk = pk.at[_ROW_MISC, :2 * N_HEADS].set(fb)
    pk = pk.at[_ROW_MISC, BLK:BLK + REL_BUCKETS * N_HEADS].set(rel_bias.reshape(-1))
    if loss is not None:
        pk = pk.at[_ROW_LOSS, 0].set(loss)
    return pk


def _unpack_small(pk, conv_cols):
    out = {}
    for name, row in _SMALL_LAYOUT:
        n = 3 * conv_cols if name == "conv_w" else D_MODEL
        v = pk[row:row + DEPTH, :n]
        out[name] = v.reshape(DEPTH, 3, conv_cols) if name == "conv_w" else v
    out["f_bias"] = pk[_ROW_MISC, :2 * N_HEADS].reshape(DEPTH, N_HEADS)
    out["rel_bias"] = pk[_ROW_MISC, BLK:BLK + REL_BUCKETS * N_HEADS].reshape(REL_BUCKETS, N_HEADS)
    return out


_WEIGHTS = ("w_in", "f_bias", "conv_w", "w_out", "rel_bias", "ln1_g", "ln1_b", "w_gate", "w_up", "w_down", "ln2_g", "ln2_b")


def kernel(x, w_in, f_bias, conv_w, w_out, rel_bias, ln1_g, ln1_b, w_gate, w_up, w_down, ln2_g, ln2_b, loss_target, m_w_in, m_f_bias, m_conv_w, m_w_out, m_rel_bias, m_ln1_g, m_ln1_b, m_w_gate, m_w_up, m_w_down, m_ln2_g, m_ln2_b, v_w_in, v_f_bias, v_conv_w, v_w_out, v_rel_bias, v_ln1_g, v_ln1_b, v_w_gate, v_w_up, v_w_down, v_ln2_g, v_ln2_b):
    w = dict(w_in=w_in, f_bias=f_bias, conv_w=conv_w, w_out=w_out, rel_bias=rel_bias, ln1_g=ln1_g, ln1_b=ln1_b,
             w_gate=w_gate, w_up=w_up, w_down=w_down, ln2_g=ln2_g, ln2_b=ln2_b)
    m = dict(w_in=m_w_in, f_bias=m_f_bias, conv_w=m_conv_w, w_out=m_w_out, rel_bias=m_rel_bias, ln1_g=m_ln1_g,
             ln1_b=m_ln1_b, w_gate=m_w_gate, w_up=m_w_up, w_down=m_w_down, ln2_g=m_ln2_g, ln2_b=m_ln2_b)
    v = dict(w_in=v_w_in, f_bias=v_f_bias, conv_w=v_conv_w, w_out=v_w_out, rel_bias=v_rel_bias, ln1_g=v_ln1_g,
             ln1_b=v_ln1_b, w_gate=v_w_gate, w_up=v_w_up, w_down=v_w_down, ln2_g=v_ln2_g, ln2_b=v_ln2_b)
    chip = 2 * lax.axis_index("x") + lax.axis_index("y")
    core = lax.axis_index("c")
    conv_shard = CONV_W // N_CHIPS

    shards = [w[name].astype(MXU_DTYPE) for name in _BIG]
    gathered = [lax.dynamic_update_index_in_dim(got, own, chip, 1) for got, own in zip(_gather_weights(shards), shards)]
    full = _full_weights(dict(zip(_BIG, gathered)))
    cw_pk = jnp.zeros((8, D_MODEL), F32).at[0, :DEPTH * 3 * conv_shard].set(conv_w.reshape(-1))
    cw_all, _ = _gather_small(cw_pk, "gather_conv_w")
    cw_chips = cw_all[0::2, 0, :DEPTH * 3 * conv_shard].reshape(N_CHIPS, DEPTH, 3, conv_shard)
    conv_full = jnp.moveaxis(cw_chips, 0, 2).reshape(DEPTH, 3, CONV_W)
    small_all = [dict(f_bias=f_bias[l], conv_w=conv_full[l], rel_bias=rel_bias, ln1_g=ln1_g[l], ln1_b=ln1_b[l],
                      ln2_g=ln2_g[l], ln2_b=ln2_b[l]) for l in range(DEPTH)]

    lossp, grad_x, grads = _local_step(x, loss_target, full, small_all)

    g0, g1 = ([_by_chip(name, grads[l][name]) for name in _BIG] for l in range(DEPTH))
    pair = [_pair_sum(a0, a1, got, core, f"pair_sum_{name}")
            for name, a0, a1, got in zip(_BIG, g0, g1, _swap_layers(g0, g1))]
    mine = [_chip_sum(q, p, chip, f"chip_sum_{name}") for name, q, p in zip(_BIG, _scatter_shards(pair), pair)]
    theirs = _share_layers(mine)
    big_g = {name: jnp.where(core == 0, jnp.stack([a, b]), jnp.stack([b, a])) for name, a, b in zip(_BIG, mine, theirs)}

    drel = grads[0]["rel_bias"] + grads[1]["rel_bias"]
    small_pk = _pack_small(grads, drel, lossp[0, 0])
    _, small_sum = _gather_small(small_pk, "gather_small_grads")
    loss = small_sum[_ROW_LOSS, 0]
    small_g = _unpack_small(small_sum, CONV_W)
    small_g["conv_w"] = lax.dynamic_slice_in_dim(small_g["conv_w"], chip * conv_shard, conv_shard, axis=2)

    out_g, out_d, out_m, out_v = dict(small_g), {}, {}, {}
    for name in _BIG:
        out_g[name] = big_g[name]
        out_d[name], out_m[name], out_v[name] = _adamw(w[name], big_g[name], m[name], v[name], f"adamw_{name}")
    per_layer = lambda src: [{name: src[name][l] for name in ("ln1_g", "ln1_b", "ln2_g", "ln2_b", "conv_w", "f_bias")}
                             for l in range(DEPTH)]
    packs = [_pack_small(per_layer(src), src["rel_bias"])[None] for src in (w, small_g, m, v)]
    for dst, pk in zip((out_d, out_m, out_v), _adamw(*packs, "adamw_small")):
        dst.update(_unpack_small(pk[0], conv_shard))

    return (loss, grad_x, *[out_g[n] for n in _WEIGHTS], *[out_d[n] for n in _WEIGHTS],
            *[out_m[n] for n in _WEIGHTS], *[out_v[n] for n in _WEIGHTS])
```

```python
import functools
import math

import numpy as np
import jax
import jax.numpy as jnp
from jax import lax
from jax.experimental import pallas as pl
from jax.experimental.pallas import tpu as pltpu

F32 = jnp.float32
BF16 = jnp.bfloat16
MXU_DTYPE = BF16

D_MODEL = 1024
HEAD_DIM = 64
N_HEADS = 4
BLK = 128
ATT = 256
CONV_W = 256
PROJ = 3076
PROJ_PAD = 3200
D_FF = 2816
DEPTH = 2
ALPHA = (2 * DEPTH) ** 0.25
LN_EPS = 1e-5
NEG = -1e30
DIL_PATTERNS = ((128, 1), (512, 4), (2048, 16))
REL_BUCKETS = 32
N_CHIPS = 4
N_DEV = 8
SMALL_ROWS = 16

ADAM_LR = 0.001
ADAM_B1 = 0.9
ADAM_B2 = 0.999
ADAM_EPS = 1e-08
ADAM_WD = 0.01
ADAM_STEP = 10

VMEM_LIMIT = 48 * 2 ** 20
MESH = pl.DeviceIdType.MESH


def _cparams(*sem):
    return pltpu.CompilerParams(dimension_semantics=tuple(sem), vmem_limit_bytes=VMEM_LIMIT)


def _dot(a, b):
    return jnp.dot(a.astype(MXU_DTYPE), b.astype(MXU_DTYPE), preferred_element_type=F32)


def _dot_nt(a, b):
    return lax.dot_general(a.astype(MXU_DTYPE), b.astype(MXU_DTYPE), (((1,), (1,)), ((), ())),
                           preferred_element_type=F32)


def _dot_tn(a, b):
    return lax.dot_general(a.astype(MXU_DTYPE), b.astype(MXU_DTYPE), (((0,), (0,)), ((), ())),
                           preferred_element_type=F32)


def _split_dot(x, ones, passes):
    acc, rest = None, x
    for p in range(passes):
        piece = rest.astype(MXU_DTYPE)
        part = jnp.dot(piece, ones, preferred_element_type=F32)
        acc = part if acc is None else acc + part
        if p + 1 < passes:
            rest = rest - piece.astype(F32)
    return acc


def _split_dot_lhs(ones, x, passes):
    acc, rest = None, x
    for p in range(passes):
        piece = rest.astype(MXU_DTYPE)
        part = jnp.dot(ones, piece, preferred_element_type=F32)
        acc = part if acc is None else acc + part
        if p + 1 < passes:
            rest = rest - piece.astype(F32)
    return acc


def _iota2(shape, axis):
    return lax.broadcasted_iota(jnp.int32, shape, axis)


_TILES = {"proj": (1024, 640, 1024), "out_proj": (1024, 1024, 1024), "ffn_in": (1024, 1408, 1024),
          "ffn_out": (1024, 1024, 2816), "ffn_out_dx": (1024, 1408, 1024), "ffn_out_dw": (1408, 1024, 2048),
          "ffn_in_dx": (1024, 1024, 1408), "ffn_in_dw": (1024, 1408, 2048), "out_proj_dx": (1024, 1024, 1024),
          "out_proj_dw": (1024, 1024, 2048), "proj_dx": (1024, 1024, 640), "proj_dw": (1024, 640, 2048)}


def _matmul(a, b, kind, tag, *, out_dtype=F32, add=None, add_scale=1.0, trans_b=False):
    m, k = a.shape
    n = b.shape[0] if trans_b else b.shape[1]
    tm, tn, tk = _TILES[kind]
    tm, tk, name = min(tm, m), min(tk, k), f"{kind}_{tag}"
    assert m % tm == 0 and n % tn == 0 and k % tk == 0, (a.shape, b.shape, tm, tn, tk)
    nk = k // tk

    def body(*refs):
        if add is None:
            a_ref, b_ref, o_ref = refs[:3]
            c_ref, scr = None, refs[3:]
        else:
            a_ref, b_ref, c_ref, o_ref = refs[:4]
            scr = refs[4:]
        part = _dot_nt(a_ref[...], b_ref[...]) if trans_b else _dot(a_ref[...], b_ref[...])

        def finish(acc):
            if c_ref is not None:
                acc = acc + add_scale * c_ref[...]
            o_ref[...] = acc.astype(out_dtype)

        if nk == 1:
            finish(part)
        else:
            acc_ref = scr[0]
            kk = pl.program_id(2)

            @pl.when(kk == 0)
            def _():
                acc_ref[...] = part

            @pl.when(kk > 0)
            def _():
                acc_ref[...] += part

            @pl.when(kk == nk - 1)
            def _():
                finish(acc_ref[...])

    b_spec = pl.BlockSpec((tn, tk), lambda i, j, kk: (j, kk)) if trans_b else pl.BlockSpec((tk, tn), lambda i, j, kk: (kk, j))
    in_specs = [pl.BlockSpec((tm, tk), lambda i, j, kk: (i, kk)), b_spec]
    operands = [a, b]
    if add is not None:
        in_specs.append(pl.BlockSpec((tm, tn), lambda i, j, kk: (i, j)))
        operands.append(add)
    return pl.pallas_call(
        body, name=name, grid=(m // tm, n // tn, nk), in_specs=in_specs,
        out_specs=pl.BlockSpec((tm, tn), lambda i, j, kk: (i, j)),
        out_shape=jax.ShapeDtypeStruct((m, n), out_dtype),
        scratch_shapes=[pltpu.VMEM((tm, tn), F32)] if nk > 1 else [],
        compiler_params=_cparams("parallel", "parallel", "arbitrary"),
    )(*operands)


def _ln_stats(pre):
    mu = jnp.mean(pre, axis=-1, keepdims=True)
    xc = pre - mu
    var = jnp.mean(xc * xc, axis=-1, keepdims=True)
    rstd = lax.rsqrt(var + LN_EPS)
    return xc * rstd, rstd


def _ln_fwd(xin, branch, g, b, name):
    t, d = xin.shape
    tile = 256

    def body(x_ref, br_ref, g_ref, b_ref, pre_ref, y_ref):
        pre = ALPHA * x_ref[...] + br_ref[...]
        xhat, _ = _ln_stats(pre)
        pre_ref[...] = pre
        y_ref[...] = xhat * g_ref[...] + b_ref[...]

    row = pl.BlockSpec((tile, d), lambda i: (i, 0))
    vec = pl.BlockSpec((1, d), lambda i: (0, 0))
    return pl.pallas_call(
        body, name=name, grid=(t // tile,), in_specs=[row, row, vec, vec], out_specs=[row, row],
        out_shape=[jax.ShapeDtypeStruct((t, d), F32)] * 2, compiler_params=_cparams("parallel"),
    )(xin, branch, g.reshape(1, d), b.reshape(1, d))


def _ln_bwd(dy, pre, g, name):
    t, d = dy.shape
    tile = 256

    def body(dy_ref, pre_ref, g_ref, dpre_ref, dgb_ref):
        dyv = dy_ref[...]
        xhat, rstd = _ln_stats(pre_ref[...])
        dxh = dyv * g_ref[...]
        m1 = jnp.mean(dxh, axis=-1, keepdims=True)
        m2 = jnp.mean(dxh * xhat, axis=-1, keepdims=True)
        dpre_ref[...] = rstd * (dxh - m1 - xhat * m2)

        @pl.when(pl.program_id(0) == 0)
        def _():
            dgb_ref[...] = jnp.zeros_like(dgb_ref)

        dgb_ref[0:1, :] += jnp.sum(dyv * xhat, axis=0, keepdims=True)
        dgb_ref[1:2, :] += jnp.sum(dyv, axis=0, keepdims=True)

    row = pl.BlockSpec((tile, d), lambda i: (i, 0))
    return pl.pallas_call(
        body, name=name, grid=(t // tile,), in_specs=[row, row, pl.BlockSpec((1, d), lambda i: (0, 0))],
        out_specs=[row, pl.BlockSpec((8, d), lambda i: (0, 0))],
        out_shape=[jax.ShapeDtypeStruct((t, d), F32), jax.ShapeDtypeStruct((8, d), F32)],
        compiler_params=_cparams("arbitrary"),
    )(dy, pre, g.reshape(1, d))


def _swiglu_fwd(gu, name):
    t = gu.shape[0]
    tile = 256

    def body(gu_ref, h_ref):
        gate = gu_ref[:, :D_FF]
        up = gu_ref[:, D_FF:]
        h_ref[...] = (gate * (1.0 / (1.0 + jnp.exp(-gate))) * up).astype(h_ref.dtype)

    return pl.pallas_call(
        body, name=name, grid=(t // tile,), in_specs=[pl.BlockSpec((tile, 2 * D_FF), lambda i: (i, 0))],
        out_specs=pl.BlockSpec((tile, D_FF), lambda i: (i, 0)),
        out_shape=jax.ShapeDtypeStruct((t, D_FF), MXU_DTYPE), compiler_params=_cparams("parallel"),
    )(gu)


def _swiglu_bwd(dh, gu, name, carry=None):
    t = gu.shape[0]
    tile = 256

    def body(dh_ref, gu_ref, dgu_ref):
        gate = gu_ref[:, :D_FF]
        up = gu_ref[:, D_FF:]
        dhv = dh_ref[...]
        sig = 1.0 / (1.0 + jnp.exp(-gate))
        dgu_ref[:, :D_FF] = (dhv * up * sig * (1.0 + gate * (1.0 - sig))).astype(dgu_ref.dtype)
        dgu_ref[:, D_FF:] = (dhv * gate * sig).astype(dgu_ref.dtype)

    return _host_call(
        body, carry, name=name, grid=(t // tile,),
        in_specs=[pl.BlockSpec((tile, D_FF), lambda i: (i, 0)), pl.BlockSpec((tile, 2 * D_FF), lambda i: (i, 0))],
        out_specs=[pl.BlockSpec((tile, 2 * D_FF), lambda i: (i, 0))],
        out_shape=[jax.ShapeDtypeStruct((t, 2 * D_FF), MXU_DTYPE)], operands=(dh, gu))


def _loss_kernel(y, target, name):
    t, d = y.shape
    tile = 512

    def body(y_ref, t_ref, dy_ref, l_ref):
        err = y_ref[...] - t_ref[...]
        dy_ref[...] = err * (1.0 / d)

        @pl.when(pl.program_id(0) == 0)
        def _():
            l_ref[...] = jnp.zeros_like(l_ref)

        l_ref[...] += jnp.sum(err * err) * (0.5 / d)

    row = pl.BlockSpec((tile, d), lambda i: (i, 0))
    return pl.pallas_call(
        body, name=name, grid=(t // tile,), in_specs=[row, row],
        out_specs=[row, pl.BlockSpec((8, 128), lambda i: (0, 0))],
        out_shape=[jax.ShapeDtypeStruct((t, d), F32), jax.ShapeDtypeStruct((8, 128), F32)],
        compiler_params=_cparams("arbitrary"),
    )(y, target)


def _adamw(w, g, m, v, name):
    nl, r, c = w.shape
    tr = r
    for cand in (256, 352, 128, 64, 16, 8):
        if r % cand == 0:
            tr = cand
            break

    def body(w_ref, g_ref, m_ref, v_ref, d_ref, nm_ref, nv_ref):
        gv = g_ref[...]
        nm = ADAM_B1 * m_ref[...] + (1.0 - ADAM_B1) * gv
        nv = ADAM_B2 * v_ref[...] + (1.0 - ADAM_B2) * (gv * gv)
        m_hat = nm / (1.0 - ADAM_B1 ** ADAM_STEP)
        v_hat = nv / (1.0 - ADAM_B2 ** ADAM_STEP)
        d_ref[...] = -ADAM_LR * (m_hat / (jnp.sqrt(v_hat) + ADAM_EPS) + ADAM_WD * w_ref[...])
        nm_ref[...] = nm
        nv_ref[...] = nv

    blk = pl.BlockSpec((1, tr, c), lambda l, i: (l, i, 0))
    return pl.pallas_call(
        body, name=name, grid=(nl, r // tr), in_specs=[blk] * 4, out_specs=[blk] * 3,
        out_shape=[jax.ShapeDtypeStruct(w.shape, F32)] * 3, compiler_params=_cparams("parallel", "parallel"),
    )(w, g, m, v)


def _shift_down(u, k, rows):
    return jnp.where(rows >= k, pltpu.roll(u, k, 0), 0.0)


def _shift_up(u, k, rows, s):
    return jnp.where(rows < s - k, pltpu.roll(u, s - k, 0), 0.0)


def _conv_fwd(proj, conv_w, nb, s, name):
    def body(b_ref, c_ref, h_ref, w_ref, o_ref):
        rows = _iota2((s, CONV_W), 0)
        u = c_ref[...] * h_ref[...]
        y = w_ref[2:3, :] * u + w_ref[1:2, :] * _shift_down(u, 1, rows) + w_ref[0:1, :] * _shift_down(u, 2, rows)
        o_ref[...] = b_ref[...] * y

    col = lambda j: pl.BlockSpec((s, CONV_W), lambda b: (b, j))
    return pl.pallas_call(
        body, name=name, grid=(nb,),
        in_specs=[col(9), col(10), col(11), pl.BlockSpec((8, CONV_W), lambda b: (0, 0))],
        out_specs=pl.BlockSpec((s, CONV_W), lambda b: (b, 0)),
        out_shape=jax.ShapeDtypeStruct((nb * s, CONV_W), F32), compiler_params=_cparams("parallel"),
    )(proj, proj, proj, conv_w)


def _conv_bwd(dmixed, proj, conv_w, nb, s, name):
    def body(do_ref, b_ref, c_ref, h_ref, w_ref, dg_ref, dw_ref):
        rows = _iota2((s, CONV_W), 0)
        cg, hg, bg, dout = c_ref[...], h_ref[...], b_ref[...], do_ref[...]
        u = cg * hg
        u1 = _shift_down(u, 1, rows)
        u2 = _shift_down(u, 2, rows)
        y = w_ref[2:3, :] * u + w_ref[1:2, :] * u1 + w_ref[0:1, :] * u2
        dy = dout * bg
        du = w_ref[2:3, :] * dy + w_ref[1:2, :] * _shift_up(dy, 1, rows, s) + w_ref[0:1, :] * _shift_up(dy, 2, rows, s)
        dg_ref[:, 0:CONV_W] = dout * y
        dg_ref[:, CONV_W:2 * CONV_W] = du * hg
        dg_ref[:, 2 * CONV_W:3 * CONV_W] = du * cg

        @pl.when(pl.program_id(0) == 0)
        def _():
            dw_ref[...] = jnp.zeros_like(dw_ref)

        dw_ref[0:1, :] += jnp.sum(dy * u2, axis=0, keepdims=True)
        dw_ref[1:2, :] += jnp.sum(dy * u1, axis=0, keepdims=True)
        dw_ref[2:3, :] += jnp.sum(dy * u, axis=0, keepdims=True)

    col = lambda j: pl.BlockSpec((s, CONV_W), lambda b: (b, j))
    return pl.pallas_call(
        body, name=name, grid=(nb,),
        in_specs=[col(3), col(9), col(10), col(11), pl.BlockSpec((8, CONV_W), lambda b: (0, 0))],
        out_specs=[pl.BlockSpec((s, 3 * CONV_W), lambda b: (b, 0)), pl.BlockSpec((8, CONV_W), lambda b: (0, 0))],
        out_shape=[jax.ShapeDtypeStruct((nb * s, 3 * CONV_W), F32), jax.ShapeDtypeStruct((8, CONV_W), F32)],
        compiler_params=_cparams("arbitrary"),
    )(dmixed, proj, proj, proj, conv_w)


def _col_spec(s, base):
    return pl.BlockSpec((s, BLK), lambda b, p: (b, base + p))


def _rows(i):
    return pl.ds(pl.multiple_of(i * ATT, ATT), ATT)


def _rows128(i):
    return pl.ds(pl.multiple_of(i * BLK, BLK), BLK)


def _log_sigmoid_parts(z):
    e = jnp.exp(-jnp.abs(z))
    l1p = jnp.log(1.0 + e)
    lb = jnp.minimum(z, 0.0) - l1p
    return lb, lb - z, e


def _head_masks():
    lane = _iota2((1, BLK), 1)
    return [(lane >= h * HEAD_DIM) & (lane < (h + 1) * HEAD_DIM) for h in range(2)]


def _split_heads(ref, scr, sels):
    for h, sel in enumerate(sels):
        scr[h] = jnp.where(sel, ref[...], 0.0).astype(MXU_DTYPE)


def _sb_fwd(proj, nb, s, name, carry=None):
    nblk = s // ATT

    def body(q_ref, k_ref, v_ref, o_ref, km, vm):
        sels = _head_masks()
        _split_heads(k_ref, km, sels)
        _split_heads(v_ref, vm, sels)
        rows = _iota2((ATT, ATT), 0)
        cols = _iota2((ATT, ATT), 1)
        later = (rows > cols).astype(MXU_DTYPE)

        def qblock(i, _):
            qi = (q_ref[_rows(i), :] * 0.125).astype(MXU_DTYPE)

            def kblock(t, state):
                carries, acc = state
                j = i - t
                strict = (cols + (j - i) * ATT) < rows
                out = []
                for h in range(2):
                    z = _dot_nt(qi, km[h, _rows(j), :])
                    lb, lr, _ = _log_sigmoid_parts(z)
                    lr = jnp.where(strict, lr, 0.0)
                    tail = _split_dot(lr, later, 2) + carries[h]
                    a = jnp.where(strict, jnp.exp(lb + tail), 0.0)
                    acc = acc + _dot(a, vm[h, _rows(j), :])
                    out.append(carries[h] + jnp.sum(lr, axis=-1, keepdims=True))
                return tuple(out), acc

            init = ((jnp.zeros((ATT, 1), F32),) * 2, jnp.zeros((ATT, BLK), F32))
            _, acc = lax.fori_loop(0, i + 1, kblock, init)
            o_ref[_rows(i), :] = acc
            return 0

        lax.fori_loop(0, nblk, qblock, 0)

    (o,), extra = _host_call(
        body, carry, name=name, grid=(nb, 2), in_specs=[_col_spec(s, 0), _col_spec(s, 2), _col_spec(s, 4)],
        out_specs=[_col_spec(s, 0)], out_shape=[jax.ShapeDtypeStruct((nb * s, 2 * BLK), F32)],
        scratch_shapes=[pltpu.VMEM((2, s, BLK), MXU_DTYPE)] * 2, operands=(proj, proj, proj))
    return o, extra


def _sb_bwd(proj, dmixed, nb, s, name, carry=None):
    nblk = s // ATT

    def body(q_ref, k_ref, v_ref, do_ref, dq_ref, dk_ref, dv_ref, km, vm, a_scr, dl_scr, beta_scr):
        sels = _head_masks()
        _split_heads(k_ref, km, sels)
        _split_heads(v_ref, vm, sels)
        rows = _iota2((ATT, ATT), 0)
        cols = _iota2((ATT, ATT), 1)
        later = (rows > cols).astype(MXU_DTYPE)
        earlier = (rows < cols).astype(MXU_DTYPE)
        dk_ref[...] = jnp.zeros_like(dk_ref)
        dv_ref[...] = jnp.zeros_like(dv_ref)

        def qblock(i, _):
            qi = (q_ref[_rows(i), :] * 0.125).astype(MXU_DTYPE)
            doi = do_ref[_rows(i), :].astype(MXU_DTYPE)
            qm = [jnp.where(sel, qi, 0.0) for sel in sels]
            dom = [jnp.where(sel, doi, 0.0) for sel in sels]

            def first(t, carries):
                j = i - t
                strict = (cols + (j - i) * ATT) < rows
                out = []
                for h in range(2):
                    z = _dot_nt(qi, km[h, _rows(j), :])
                    lb, lr, e = _log_sigmoid_parts(z)
                    lr = jnp.where(strict, lr, 0.0)
                    tail = _split_dot(lr, later, 2) + carries[h]
                    a = jnp.where(strict, jnp.exp(lb + tail), 0.0)
                    a_scr[h, j] = a
                    dl_scr[h, j] = a * _dot_nt(doi, vm[h, _rows(j), :])
                    beta_scr[h, j] = jnp.exp(lb)
                    out.append(carries[h] + jnp.sum(lr, axis=-1, keepdims=True))
                return tuple(out)

            lax.fori_loop(0, i + 1, first, (jnp.zeros((ATT, 1), F32),) * 2)

            def second(j, state):
                csums, dq = state
                strict = (cols + (j - i) * ATT) < rows
                out = []
                for h in range(2):
                    dl = dl_scr[h, j]
                    beta = beta_scr[h, j]
                    before = _split_dot(dl, earlier, 2) + csums[h]
                    dz = jnp.where(strict, dl * (1.0 - beta) - beta * before, 0.0).astype(MXU_DTYPE)
                    dq = dq + _dot(dz, km[h, _rows(j), :])
                    dk_ref[_rows(j), :] += _dot_tn(dz, qm[h])
                    dv_ref[_rows(j), :] += _dot_tn(a_scr[h, j], dom[h])
                    out.append(csums[h] + jnp.sum(dl, axis=-1, keepdims=True))
                return tuple(out), dq

            init = ((jnp.zeros((ATT, 1), F32),) * 2, jnp.zeros((ATT, BLK), F32))
            _, dq = lax.fori_loop(0, i + 1, second, init)
            dq_ref[_rows(i), :] = dq * 0.125
            return 0

        lax.fori_loop(0, nblk, qblock, 0)

    out = _col_spec(s, 0)
    return _host_call(
        body, carry, name=name, grid=(nb, 2),
        in_specs=[_col_spec(s, 0), _col_spec(s, 2), _col_spec(s, 4), out], out_specs=[out] * 3,
        out_shape=[jax.ShapeDtypeStruct((nb * s, 2 * BLK), F32)] * 3,
        scratch_shapes=[pltpu.VMEM((2, s, BLK), MXU_DTYPE)] * 2 + [pltpu.VMEM((2, nblk, ATT, ATT), F32)] * 3,
        operands=(proj, proj, proj, dmixed))


def _pair_spec(s, width):
    return pl.BlockSpec((None, 2, s, width), lambda b, p: (b, p, 0, 0))


def _fox_fwd(proj, ccol, crow, nb, s, name, carry=None):
    nblk = s // ATT

    def body(q_ref, k_ref, v_ref, cc_ref, cr_ref, o_ref, lse_ref, km, vm):
        sels = _head_masks()
        _split_heads(k_ref, km, sels)
        _split_heads(v_ref, vm, sels)
        rows = _iota2((ATT, ATT), 0)
        cols = _iota2((ATT, ATT), 1)

        def qblock(i, _):
            qi = (q_ref[_rows(i), :] * 0.125).astype(MXU_DTYPE)
            ci = [cc_ref[h, _rows(i), :] for h in range(2)]

            def kblock(j, state):
                ms, ls, acc = state
                causal = (cols + (j - i) * ATT) <= rows
                new_m, new_l, scales, parts = [], [], [], []
                for h in range(2):
                    z = _dot_nt(qi, km[h, _rows(j), :]) + (ci[h] - cr_ref[h, j][0:1, :])
                    z = jnp.where(causal, z, NEG)
                    m_new = jnp.maximum(ms[h], jnp.max(z, axis=-1, keepdims=True))
                    p = jnp.exp(z - m_new)
                    scale = jnp.exp(ms[h] - m_new)
                    new_m.append(m_new)
                    new_l.append(scale * ls[h] + jnp.sum(p, axis=-1, keepdims=True))
                    scales.append(scale)
                    parts.append(_dot(p, vm[h, _rows(j), :]))
                acc = jnp.where(sels[0], scales[0], scales[1]) * acc + parts[0] + parts[1]
                return tuple(new_m), tuple(new_l), acc

            init = ((jnp.full((ATT, 1), NEG, F32),) * 2, (jnp.zeros((ATT, 1), F32),) * 2, jnp.zeros((ATT, BLK), F32))
            ms, ls, acc = lax.fori_loop(0, i + 1, kblock, init)
            o_ref[_rows(i), :] = acc / jnp.where(sels[0], ls[0], ls[1])
            for h in range(2):
                lse_ref[h, _rows(i), :] = jnp.broadcast_to(ms[h] + jnp.log(ls[h]), (ATT, ATT))
            return 0

        lax.fori_loop(0, nblk, qblock, 0)

    crow_spec = pl.BlockSpec((None, 2, nblk, 8, ATT), lambda b, p: (b, p, 0, 0, 0))
    return _host_call(
        body, carry, name=name, grid=(nb, 2),
        in_specs=[_col_spec(s, 12), _col_spec(s, 14), _col_spec(s, 16), _pair_spec(s, ATT), crow_spec],
        out_specs=[_col_spec(s, 0), _pair_spec(s, ATT)],
        out_shape=[jax.ShapeDtypeStruct((nb * s, 2 * BLK), F32), jax.ShapeDtypeStruct((nb, N_HEADS, s, ATT), F32)],
        scratch_shapes=[pltpu.VMEM((2, s, BLK), MXU_DTYPE)] * 2, operands=(proj, proj, proj, ccol, crow))


def _fox_bwd(proj, dmixed, lse, ccol, crow, nb, s, name, carry=None):
    nblk = s // ATT

    def body(q_ref, k_ref, v_ref, do_ref, lse_ref, cc_ref, cr_ref, dq_ref, dk_ref, dv_ref, dc_ref, km, vm):
        sels = _head_masks()
        _split_heads(k_ref, km, sels)
        _split_heads(v_ref, vm, sels)
        rows = _iota2((ATT, ATT), 0)
        cols = _iota2((ATT, ATT), 1)
        dk_ref[...] = jnp.zeros_like(dk_ref)
        dv_ref[...] = jnp.zeros_like(dv_ref)
        dc_ref[...] = jnp.zeros_like(dc_ref)

        def qblock(i, _):
            qi = (q_ref[_rows(i), :] * 0.125).astype(MXU_DTYPE)
            doi = do_ref[_rows(i), :].astype(MXU_DTYPE)
            qm = [jnp.where(sel, qi, 0.0) for sel in sels]
            dom = [jnp.where(sel, doi, 0.0) for sel in sels]
            ci = [cc_ref[h, _rows(i), :] for h in range(2)]
            lsei = [lse_ref[h, _rows(i), :] for h in range(2)]

            def probs(j, h):
                z = _dot_nt(qi, km[h, _rows(j), :]) + (ci[h] - cr_ref[h, j][0:1, :])
                p = jnp.where((cols + (j - i) * ATT) <= rows, jnp.exp(z - lsei[h]), 0.0)
                return p, _dot_nt(doi, vm[h, _rows(j), :])

            def row_term(j, accs):
                out = []
                for h in range(2):
                    p, dp = probs(j, h)
                    out.append(accs[h] + jnp.sum(p * dp, axis=-1, keepdims=True))
                return tuple(out)

            di = lax.fori_loop(0, i + 1, row_term, (jnp.zeros((ATT, 1), F32),) * 2)

            def kblock(j, dq):
                for h in range(2):
                    p, dp = probs(j, h)
                    ds = p * (dp - di[h])
                    dc_ref[h, j] += jnp.broadcast_to(jnp.sum(ds, axis=0, keepdims=True), (8, ATT))
                    ds = ds.astype(MXU_DTYPE)
                    dk_ref[_rows(j), :] += _dot_tn(ds, qm[h])
                    dv_ref[_rows(j), :] += _dot_tn(p, dom[h])
                    dq = dq + _dot(ds, km[h, _rows(j), :])
                return dq

            dq = lax.fori_loop(0, i + 1, kblock, jnp.zeros((ATT, BLK), F32))
            dq_ref[_rows(i), :] = dq * 0.125
            return 0

        lax.fori_loop(0, nblk, qblock, 0)

    crow_spec = pl.BlockSpec((None, 2, nblk, 8, ATT), lambda b, p: (b, p, 0, 0, 0))
    wide, cols_out = _pair_spec(s, ATT), _col_spec(s, 0)
    return _host_call(
        body, carry, name=name, grid=(nb, 2),
        in_specs=[_col_spec(s, 12), _col_spec(s, 14), _col_spec(s, 16), _col_spec(s, 4), wide, wide, crow_spec],
        out_specs=[cols_out, cols_out, cols_out, crow_spec],
        out_shape=[jax.ShapeDtypeStruct((nb * s, 2 * BLK), F32)] * 3 + [jax.ShapeDtypeStruct((nb, N_HEADS, nblk, 8, ATT), F32)],
        scratch_shapes=[pltpu.VMEM((2, s, BLK), MXU_DTYPE)] * 2, operands=(proj, proj, proj, dmixed, lse, ccol, crow))


def _fox_gates_fwd(proj, f_bias, nb, s, name):
    chunk = 256

    def body(f_ref, b_ref, c_ref):
        lower = (_iota2((chunk, chunk), 0) >= _iota2((chunk, chunk), 1)).astype(MXU_DTYPE)
        carry = jnp.zeros((1, BLK), F32)
        for n in range(s // chunk):
            rows = pl.ds(n * chunk, chunk)
            lf, _, _ = _log_sigmoid_parts(f_ref[rows, :] + b_ref[0:1, :])
            c = _split_dot_lhs(lower, lf, 3) + carry
            c_ref[rows, :] = c
            carry = c[chunk - 1:chunk, :]

    return pl.pallas_call(
        body, name=name, grid=(nb,),
        in_specs=[pl.BlockSpec((s, BLK), lambda b: (b, (PROJ_PAD - BLK) // BLK)), pl.BlockSpec((8, BLK), lambda b: (0, 0))],
        out_specs=pl.BlockSpec((s, BLK), lambda b: (b, 0)),
        out_shape=jax.ShapeDtypeStruct((nb * s, BLK), F32), compiler_params=_cparams("parallel"),
    )(proj, f_bias)


def _fox_gates_bwd(dc, proj, f_bias, nb, s, name):
    chunk = 256

    def body(dc_ref, f_ref, b_ref, df_ref, db_ref):
        upper = (_iota2((chunk, chunk), 0) <= _iota2((chunk, chunk), 1)).astype(MXU_DTYPE)
        carry = jnp.zeros((1, BLK), F32)
        total = jnp.zeros((1, BLK), F32)
        for n in reversed(range(s // chunk)):
            rows = pl.ds(n * chunk, chunk)
            dlf = _split_dot_lhs(upper, dc_ref[rows, :], 3) + carry
            carry = dlf[0:1, :]
            pre = f_ref[rows, :] + b_ref[0:1, :]
            e = jnp.exp(-jnp.abs(pre))
            df = dlf * (jnp.where(pre >= 0.0, e, 1.0) / (1.0 + e))
            df_ref[rows, :] = df
            total = total + jnp.sum(df, axis=0, keepdims=True)

        @pl.when(pl.program_id(0) == 0)
        def _():
            db_ref[...] = jnp.zeros_like(db_ref)

        db_ref[0:1, :] += total

    return pl.pallas_call(
        body, name=name, grid=(nb,),
        in_specs=[pl.BlockSpec((s, BLK), lambda b: (b, 0)), pl.BlockSpec((s, BLK), lambda b: (b, (PROJ_PAD - BLK) // BLK)),
                  pl.BlockSpec((8, BLK), lambda b: (0, 0))],
        out_specs=[pl.BlockSpec((s, BLK), lambda b: (b, 0)), pl.BlockSpec((8, BLK), lambda b: (0, 0))],
        out_shape=[jax.ShapeDtypeStruct((nb * s, BLK), F32), jax.ShapeDtypeStruct((8, BLK), F32)],
        compiler_params=_cparams("arbitrary"),
    )(dc, proj, f_bias)


def _delta_kernel(dmixed, o, nb, s, name):
    def body(do_ref, o_ref, d_ref):
        prod = do_ref[...] * o_ref[...]
        for h, sel in enumerate(_head_masks()):
            d_ref[h] = jnp.broadcast_to(jnp.sum(jnp.where(sel, prod, 0.0), axis=-1, keepdims=True), (s, BLK))

    return pl.pallas_call(
        body, name=name, grid=(nb, 2), in_specs=[_col_spec(s, 2), _col_spec(s, 0)], out_specs=_pair_spec(s, BLK),
        out_shape=jax.ShapeDtypeStruct((nb, N_HEADS, s, BLK), F32), compiler_params=_cparams("parallel", "parallel"),
    )(dmixed, o)


def _t5_bucket_np(dist):
    max_exact = REL_BUCKETS // 2
    nf = np.maximum(dist, 1).astype(np.float32)
    large = max_exact + (np.log(nf / max_exact) / math.log(2048 / max_exact) * (REL_BUCKETS - max_exact)).astype(np.int32)
    large = np.minimum(large, REL_BUCKETS - 1)
    return np.where(dist < max_exact, dist, large)


def _bucket_table():
    qi = np.arange(BLK)[:, None]
    kj = np.arange(2 * BLK)[None, :]
    dist = qi + BLK - kj
    tables = []
    for window, dil in DIL_PATTERNS:
        in_band = (dist >= 0) & (dist <= window // dil)
        tables.append(np.where(in_band, _t5_bucket_np(np.maximum(dist, 0) * dil), -1).astype(np.int32))
    return np.stack(tables)


def _to_residue(t, dil):
    if dil == 1:
        return t
    *lead, s, e = t.shape
    return jnp.swapaxes(t.reshape(*lead, s // dil, dil, e), -3, -2).reshape(*lead, s, e)


def _from_residue(t, dil):
    if dil == 1:
        return t
    *lead, s, e = t.shape
    return jnp.swapaxes(t.reshape(*lead, dil, s // dil, e), -3, -2).reshape(*lead, s, e)


def _pat_col_spec(s, base=0):
    return pl.BlockSpec((3, None, s, BLK), lambda b, p: (0, b, 0, base + p))


def _pat_pair_spec(s):
    return pl.BlockSpec((3, None, 2, s, BLK), lambda b, p: (0, b, p, 0, 0))


def _one_pat_col_spec(s, base=0):
    return pl.BlockSpec((None, None, s, BLK), lambda b, p, t: (t, b, 0, base + p))


def _one_pat_pair_spec(s):
    return pl.BlockSpec((None, None, 2, s, BLK), lambda b, p, t: (t, b, p, 0, 0))


def _one_pat_bias_spec():
    return pl.BlockSpec((None, 2, BLK, 2 * BLK), lambda b, p, t: (t, p, 0, 0))


def _blocks_per_class(s):
    t = pl.program_id(2)
    segs = [s // dil // BLK for _, dil in DIL_PATTERNS]
    return jnp.where(t == 0, segs[0], jnp.where(t == 1, segs[1], segs[2]))


def _dil_scores(qb, kp, kc, b_ref, h, prev_valid):
    zp = _dot_nt(qb, kp) + b_ref[h, :, 0:BLK]
    zp = jnp.where(prev_valid, zp, NEG)
    zc = _dot_nt(qb, kc) + b_ref[h, :, BLK:2 * BLK]
    return zp, zc


def _dil_fwd(qkvp, bias, name):
    _, nb, s, _ = qkvp.shape
    nblk = s // BLK

    def body(q_ref, k_ref, v_ref, b_ref, o_ref, lse_ref):
        sels = _head_masks()
        seg = _blocks_per_class(s)

        def block(b, _):
            cur = _rows128(b)
            prev = _rows128(jnp.maximum(b - 1, 0))
            qb = q_ref[cur, :] * 0.125
            kp, kc, vp, vc = k_ref[prev, :], k_ref[cur, :], v_ref[prev, :], v_ref[cur, :]
            acc = jnp.zeros((BLK, BLK), F32)
            for h, sel in enumerate(sels):
                zp, zc = _dil_scores(qb, jnp.where(sel, kp, 0.0), jnp.where(sel, kc, 0.0), b_ref, h, b % seg > 0)
                m = jnp.maximum(jnp.max(zp, axis=-1, keepdims=True), jnp.max(zc, axis=-1, keepdims=True))
                pp = jnp.exp(zp - m)
                pc = jnp.exp(zc - m)
                den = jnp.sum(pp, axis=-1, keepdims=True) + jnp.sum(pc, axis=-1, keepdims=True)
                acc = acc + (_dot(pp, jnp.where(sel, vp, 0.0)) + _dot(pc, jnp.where(sel, vc, 0.0))) / den
                lse_ref[h, cur, :] = jnp.broadcast_to(m + jnp.log(den), (BLK, BLK))
            o_ref[cur, :] = acc
            return 0

        lax.fori_loop(0, nblk, block, 0, unroll=2)

    return pl.pallas_call(
        body, name=name, grid=(nb, 2, len(DIL_PATTERNS)),
        in_specs=[_one_pat_col_spec(s, 0), _one_pat_col_spec(s, 2), _one_pat_col_spec(s, 4), _one_pat_bias_spec()],
        out_specs=[_one_pat_col_spec(s), _one_pat_pair_spec(s)],
        out_shape=[jax.ShapeDtypeStruct((3, nb, s, 2 * BLK), F32), jax.ShapeDtypeStruct((3, nb, N_HEADS, s, BLK), F32)],
        compiler_params=_cparams("parallel", "parallel", "parallel"),
    )(qkvp, qkvp, qkvp, bias)


def _dil_combine(o, lse, name):
    _, nb, s, _ = o.shape

    def body(o_ref, l_ref, out_ref, lse_ref):
        sels = _head_masks()
        weights, dens = [], []
        for h in range(2):
            m = jnp.maximum(jnp.maximum(l_ref[0, h], l_ref[1, h]), l_ref[2, h])
            w = [jnp.exp(l_ref[p, h] - m) for p in range(3)]
            den = w[0] + w[1] + w[2]
            lse_ref[h] = m + jnp.log(den)
            weights.append(w)
            dens.append(den)
        num = sum(jnp.where(sels[0], weights[0][p], weights[1][p]) * o_ref[p] for p in range(3))
        out_ref[...] = num / jnp.where(sels[0], dens[0], dens[1])

    return pl.pallas_call(
        body, name=name, grid=(nb, 2), in_specs=[_pat_col_spec(s), _pat_pair_spec(s)],
        out_specs=[_col_spec(s, 0), _pair_spec(s, BLK)],
        out_shape=[jax.ShapeDtypeStruct((nb * s, 2 * BLK), F32), jax.ShapeDtypeStruct((nb, N_HEADS, s, BLK), F32)],
        compiler_params=_cparams("parallel", "parallel"),
    )(o, lse)


def _dil_bwd(qkvp, dop, lsep, deltap, bias, name):
    _, nb, s, _ = qkvp.shape
    nblk = s // BLK

    def body(q_ref, k_ref, v_ref, do_ref, lse_ref, dl_ref, b_ref, dq_ref, dk_ref, dv_ref, g_ref):
        sels = _head_masks()
        seg = _blocks_per_class(s)
        dk_ref[...] = jnp.zeros_like(dk_ref)
        dv_ref[...] = jnp.zeros_like(dv_ref)
        g_ref[...] = jnp.zeros_like(g_ref)

        def block(b, _):
            cur = _rows128(b)
            prev = _rows128(jnp.maximum(b - 1, 0))
            qb = q_ref[cur, :] * 0.125
            dob = do_ref[cur, :]
            kp, kc, vp, vc = k_ref[prev, :], k_ref[cur, :], v_ref[prev, :], v_ref[cur, :]
            dq = jnp.zeros((BLK, BLK), F32)
            for h, sel in enumerate(sels):
                kph, kch = jnp.where(sel, kp, 0.0), jnp.where(sel, kc, 0.0)
                qh, doh = jnp.where(sel, qb, 0.0), jnp.where(sel, dob, 0.0)
                lse = lse_ref[h, cur, :]
                dlt = dl_ref[h, cur, :]
                zp, zc = _dil_scores(qb, kph, kch, b_ref, h, b % seg > 0)
                pp = jnp.exp(zp - lse)
                pc = jnp.exp(zc - lse)
                dsp = pp * (_dot_nt(dob, jnp.where(sel, vp, 0.0)) - dlt)
                dsc = pc * (_dot_nt(dob, jnp.where(sel, vc, 0.0)) - dlt)
                g_ref[h, :, 0:BLK] += dsp
                g_ref[h, :, BLK:2 * BLK] += dsc
                dsp = dsp.astype(MXU_DTYPE)
                dsc = dsc.astype(MXU_DTYPE)
                dq = dq + _dot(dsp, kph) + _dot(dsc, kch)
                dk_ref[prev, :] += _dot_tn(dsp, qh)
                dk_ref[cur, :] += _dot_tn(dsc, qh)
                dv_ref[prev, :] += _dot_tn(pp, doh)
                dv_ref[cur, :] += _dot_tn(pc, doh)
            dq_ref[cur, :] = dq * 0.125
            return 0

        lax.fori_loop(0, nblk, block, 0, unroll=2)

    cols, stats = _one_pat_col_spec(s), _one_pat_pair_spec(s)
    return pl.pallas_call(
        body, name=name, grid=(nb, 2, len(DIL_PATTERNS)),
        in_specs=[_one_pat_col_spec(s, 0), _one_pat_col_spec(s, 2), _one_pat_col_spec(s, 4), cols, stats, stats,
                  _one_pat_bias_spec()],
        out_specs=[cols, cols, cols, pl.BlockSpec((None, 2, None, BLK, 2 * BLK), lambda b, p, t: (b, p, t, 0, 0))],
        out_shape=[jax.ShapeDtypeStruct((3, nb, s, 2 * BLK), F32)] * 3 + [jax.ShapeDtypeStruct((nb, N_HEADS, 3, BLK, 2 * BLK), F32)],
        compiler_params=_cparams("parallel", "parallel", "parallel"),
    )(qkvp, qkvp, qkvp, dop, lsep, deltap, bias)


def _bucket_reduce(gbias, table, name):
    nb = gbias.shape[0]

    def body(g_ref, t_ref, o_ref):
        row = _iota2((8, BLK), 0)
        lane = _iota2((8, BLK), 1)
        gsum = [[sum(g_ref[b, h, p] for b in range(nb)) for p in range(3)] for h in range(N_HEADS)]

        def bucket(k, acc):
            for h in range(N_HEADS):
                tot = sum(jnp.sum(jnp.where(t_ref[p] == k, gsum[h][p], 0.0)) for p in range(3))
                acc = acc + jnp.where((row == h) & (lane == k), tot, 0.0)
            return acc

        o_ref[...] = lax.fori_loop(0, REL_BUCKETS, bucket, jnp.zeros((8, BLK), F32))

    vm = pl.BlockSpec(memory_space=pltpu.VMEM)
    return pl.pallas_call(
        body, name=name, in_specs=[vm, vm], out_specs=vm, out_shape=jax.ShapeDtypeStruct((8, BLK), F32),
        compiler_params=pltpu.CompilerParams(vmem_limit_bytes=VMEM_LIMIT),
    )(gbias, table)


def _place():
    x, y, c = lax.axis_index("x"), lax.axis_index("y"), lax.axis_index("c")
    others = [(1 - x, y), (x, 1 - y), (1 - x, 1 - y)]
    return x, y, c, others


def _remote(src, dst, send_sem, recv_sem, to):
    return pltpu.make_async_remote_copy(src_ref=src, dst_ref=dst, send_sem=send_sem, recv_sem=recv_sem,
                                        device_id=to, device_id_type=MESH)


_HBM = pl.BlockSpec(memory_space=pl.ANY)


class _Exchange:
    def __init__(self, operands, out_shape, n_copies, copies, aliases=None):
        self.operands, self.out_shape, self.n_copies, self.copies = list(operands), list(out_shape), n_copies, copies
        self.aliases = dict(aliases or {})

    def sem_shapes(self):
        return [pltpu.SemaphoreType.DMA((self.n_copies,)), pltpu.SemaphoreType.DMA((self.n_copies,))]


def _start_all(sends):
    for cp in sends:
        cp.start()


def _wait_all(sends, arrivals):
    for cp in arrivals:
        cp.wait_recv()
    for cp in sends:
        cp.wait_send()


def _run_exchange(ex, name):
    ni = len(ex.operands)

    def body(*refs):
        sends, arrivals = ex.copies(refs[:ni], refs[ni:-2], refs[-2], refs[-1])
        _start_all(sends)
        _wait_all(sends, arrivals)

    return list(pl.pallas_call(
        body, name=name, in_specs=[_HBM] * ni, out_specs=[_HBM] * len(ex.out_shape), out_shape=ex.out_shape,
        scratch_shapes=ex.sem_shapes(), input_output_aliases=ex.aliases)(*ex.operands))


def _host_call(body, carry, *, name, grid, in_specs, out_specs, out_shape, operands, scratch_shapes=()):
    in_specs, out_specs, out_shape, scratch_shapes = list(in_specs), list(out_specs), list(out_shape), list(scratch_shapes)
    if carry is None:
        res = pl.pallas_call(body, name=name, grid=grid, in_specs=in_specs, out_specs=out_specs, out_shape=out_shape,
                             scratch_shapes=scratch_shapes, compiler_params=_cparams(*["parallel"] * len(grid)))(*operands)
        return list(res), []
    n_in, n_out, n_scr, c_in, c_out = len(in_specs), len(out_specs), len(scratch_shapes), len(carry.operands), len(carry.out_shape)
    steps = math.prod(grid)

    def wrapped(*refs):
        ins, refs = refs[:n_in], refs[n_in:]
        c_ins, refs = refs[:c_in], refs[c_in:]
        outs, refs = refs[:n_out], refs[n_out:]
        c_outs, refs = refs[:c_out], refs[c_out:]
        scr, (send_sems, recv_sems) = refs[:n_scr], refs[n_scr:]
        step = 0
        for d, size in enumerate(grid):
            step = step * size + pl.program_id(d)

        @pl.when(step == 0)
        def _():
            _start_all(carry.copies(c_ins, c_outs, send_sems, recv_sems)[0])

        body(*ins, *outs, *scr)

        @pl.when(step == steps - 1)
        def _():
            _wait_all(*carry.copies(c_ins, c_outs, send_sems, recv_sems))

    res = pl.pallas_call(
        wrapped, name=name, grid=grid, in_specs=in_specs + [_HBM] * c_in, out_specs=out_specs + [_HBM] * c_out,
        out_shape=out_shape + carry.out_shape, scratch_shapes=scratch_shapes + carry.sem_shapes(),
        input_output_aliases={n_in + i: n_out + j for i, j in carry.aliases.items()},
        compiler_params=_cparams(*["arbitrary"] * len(grid)))(*operands, *carry.operands)
    return list(res[:n_out]), list(res[n_out:])


def _half(which, rows):
    h = rows // 2
    return pl.ds(pl.multiple_of(which * h, 16), h)


def _like(arrays, shape_of=lambda t: t.shape):
    return [jax.ShapeDtypeStruct(shape_of(t), t.dtype) for t in arrays]


def _gather_ici(shards, layer):
    n = len(shards)

    def copies(ins, outs, send_sems, recv_sems):
        x, y, c, others = _place()
        me = 2 * x + y
        sends, arrivals = [], []
        for a in range(n):
            rows = _half(c, shards[a].shape[1])
            for k, (ox, oy) in enumerate(others):
                sems = (send_sems.at[3 * a + k], recv_sems.at[3 * a + k], (ox, oy, c))
                sends.append(_remote(ins[a].at[layer, rows], outs[a].at[me, rows], *sems))
                landed = outs[a].at[2 * ox + oy, rows]
                arrivals.append(_remote(landed, landed, *sems))
        return sends, arrivals

    return _Exchange(shards, _like(shards, lambda t: (N_CHIPS,) + t.shape[1:]), 3 * n, copies)


def _gather_d2d(gathered):
    n = len(gathered)

    def copies(ins, outs, send_sems, recv_sems):
        x, y, c, others = _place()
        sends, arrivals = [], []
        for a in range(n):
            r = gathered[a].shape[1]
            for k, (ox, oy) in enumerate(others):
                sems = (send_sems.at[3 * a + k], recv_sems.at[3 * a + k], (x, y, 1 - c))
                mine, theirs = outs[a].at[2 * ox + oy, _half(c, r)], outs[a].at[2 * ox + oy, _half(1 - c, r)]
                sends.append(_remote(mine, mine, *sems))
                arrivals.append(_remote(theirs, theirs, *sems))
        return sends, arrivals

    return _Exchange(gathered, _like(gathered), 3 * n, copies, aliases={a: a for a in range(n)})


def _swap_halves(g):
    n = len(g)

    def copies(ins, outs, send_sems, recv_sems):
        x, y, c, _ = _place()
        sends, arrivals = [], []
        for a in range(n):
            sems = (send_sems.at[a], recv_sems.at[a], (x, y, 1 - c))
            sends.append(_remote(ins[a].at[:, _half(1 - c, g[a].shape[1])], outs[a], *sems))
            arrivals.append(_remote(outs[a], outs[a], *sems))
        return sends, arrivals

    return _Exchange(g, _like(g, lambda t: (t.shape[0], t.shape[1] // 2, t.shape[2])), n, copies)


def _scatter_shards(ps):
    n = len(ps)

    def copies(ins, outs, send_sems, recv_sems):
        x, y, c, others = _place()
        me = 2 * x + y
        sends, arrivals = [], []
        for a in range(n):
            for k, (ox, oy) in enumerate(others):
                sems = (send_sems.at[3 * a + k], recv_sems.at[3 * a + k], (ox, oy, c))
                sends.append(_remote(ins[a].at[2 * ox + oy], outs[a].at[me], *sems))
                slot = outs[a].at[2 * ox + oy]
                arrivals.append(_remote(slot, slot, *sems))
        return sends, arrivals

    return _Exchange(ps, _like(ps), 3 * n, copies)


def _share_halves(mine):
    n = len(mine)

    def copies(ins, outs, send_sems, recv_sems):
        x, y, c, _ = _place()
        sends, arrivals = [], []
        for a in range(n):
            sems = (send_sems.at[a], recv_sems.at[a], (x, y, 1 - c))
            sends.append(_remote(ins[a], outs[a], *sems))
            arrivals.append(_remote(outs[a], outs[a], *sems))
        return sends, arrivals

    return _Exchange(mine, _like(mine), n, copies)


def _row_tile(r):
    for cand in (256, 352, 128):
        if r % cand == 0:
            return cand
    return r


def _pair_sum(g, other, core, name):
    ns, h, w = other.shape
    tr = _row_tile(h)
    per_half = h // tr

    def body(core_ref, g_ref, o_ref, out_ref):
        out_ref[...] = (g_ref[...] + o_ref[...]).astype(out_ref.dtype)

    blk = pl.BlockSpec((None, tr, w), lambda k, i, core_ref: (k, i, 0))
    grid_spec = pltpu.PrefetchScalarGridSpec(
        num_scalar_prefetch=1, grid=(ns, per_half),
        in_specs=[pl.BlockSpec((None, tr, w), lambda k, i, core_ref: (k, core_ref[0] * per_half + i, 0)), blk], out_specs=blk)
    return pl.pallas_call(
        body, name=name, grid_spec=grid_spec, out_shape=jax.ShapeDtypeStruct((ns, h, w), MXU_DTYPE),
        compiler_params=_cparams("parallel", "parallel"),
    )(core.reshape(1).astype(jnp.int32), g, other)


def _chip_sum(q, p, chip, name):
    ns, r, w = q.shape
    tr = _row_tile(r)

    def body(chip_ref, q_ref, own_ref, out_ref):
        me = chip_ref[0]
        own = own_ref[...].astype(F32)
        terms = [jnp.where(me == k, own, q_ref[k].astype(F32)) for k in range(ns)]
        out_ref[...] = ((terms[0] + terms[1]) + terms[2]) + terms[3]

    grid_spec = pltpu.PrefetchScalarGridSpec(
        num_scalar_prefetch=1, grid=(r // tr,),
        in_specs=[pl.BlockSpec((ns, tr, w), lambda i, chip_ref: (0, i, 0)),
                  pl.BlockSpec((None, tr, w), lambda i, chip_ref: (chip_ref[0], i, 0))],
        out_specs=pl.BlockSpec((tr, w), lambda i, chip_ref: (i, 0)))
    return pl.pallas_call(
        body, name=name, grid_spec=grid_spec, out_shape=jax.ShapeDtypeStruct((r, w), F32),
        compiler_params=_cparams("parallel"),
    )(chip.reshape(1).astype(jnp.int32), q, p)


class _WeightPrefetch:
    def __init__(self, shards, layer, chip):
        self.shards, self.layer, self.chip, self.result = shards, layer, chip, None

    def first(self):
        return _gather_ici(self.shards, self.layer)

    def got_first(self, arrived):
        self.arrived = arrived

    def second(self):
        return _gather_d2d(self.arrived)

    def got_second(self, gathered):
        self.result = [lax.dynamic_update_index_in_dim(got, own[self.layer], self.chip, 0)
                       for got, own in zip(gathered, self.shards)]

    def run(self, tag):
        self.got_first(_run_exchange(self.first(), f"gather_ici_{tag}"))
        self.got_second(_run_exchange(self.second(), f"gather_d2d_{tag}"))
        return self.result


class _GradReduce:
    def __init__(self, g, chip, core, tag):
        self.g, self.chip, self.core, self.tag, self.result = g, chip, core, tag, None

    def swap(self):
        return _swap_halves(self.g)

    def got_swap(self, theirs):
        self.pair = [_pair_sum(g, t, self.core, f"pair_sum_{name}_{self.tag}") for name, g, t in zip(_BIG, self.g, theirs)]

    def scatter(self):
        return _scatter_shards(self.pair)

    def got_scatter(self, q):
        self.mine = [_chip_sum(qa, pa, self.chip, f"chip_sum_{name}_{self.tag}") for name, qa, pa in zip(_BIG, q, self.pair)]

    def share(self):
        return _share_halves(self.mine)

    def got_share(self, theirs):
        self.result = [jnp.where(self.core == 0, jnp.concatenate([a, b]), jnp.concatenate([b, a]))
                       for a, b in zip(self.mine, theirs)]

    def run(self):
        self.got_swap(_run_exchange(self.swap(), f"swap_halves_{self.tag}"))
        self.got_scatter(_run_exchange(self.scatter(), f"scatter_shards_{self.tag}"))
        self.got_share(_run_exchange(self.share(), f"share_halves_{self.tag}"))
        return self.result


def _gather_small(pk, name):
    rows, w = pk.shape

    def body(pk_ref, all_ref, sum_ref, send_sems, recv_sems):
        x, y, c, _ = _place()
        me = 4 * x + 2 * y + c
        all_ref[me] = pk_ref[...]
        flips = [(fx, fy, fc) for fx in (0, 1) for fy in (0, 1) for fc in (0, 1)][1:]
        peers = [(x ^ fx, y ^ fy, c ^ fc) for fx, fy, fc in flips]
        sends = [_remote(pk_ref, all_ref.at[me], send_sems.at[k], recv_sems.at[k], peer) for k, peer in enumerate(peers)]
        for cp in sends:
            cp.start()
        for k, (px, py, pc) in enumerate(peers):
            slot = all_ref.at[4 * px + 2 * py + pc]
            _remote(slot, slot, send_sems.at[k], recv_sems.at[k], (px, py, pc)).wait_recv()
        for cp in sends:
            cp.wait_send()
        total = all_ref[0]
        for d in range(1, N_DEV):
            total = total + all_ref[d]
        sum_ref[...] = total

    vm = pl.BlockSpec(memory_space=pltpu.VMEM)
    return pl.pallas_call(
        body, name=name, in_specs=[vm], out_specs=[vm, vm],
        out_shape=[jax.ShapeDtypeStruct((N_DEV, rows, w), F32), jax.ShapeDtypeStruct((rows, w), F32)],
        scratch_shapes=[pltpu.SemaphoreType.DMA((7,)), pltpu.SemaphoreType.DMA((7,))],
    )(pk)


def _lanes(t):
    return jnp.broadcast_to(t, t.shape[:-1] + (BLK,))


def _row_layout(c, nb, s):
    ch = jnp.swapaxes(c[:, :N_HEADS].reshape(nb, s, N_HEADS), 1, 2)
    ccol = jnp.broadcast_to(ch[..., None], (nb, N_HEADS, s, ATT))
    crow = jnp.broadcast_to(ch.reshape(nb, N_HEADS, s // ATT, 1, ATT), (nb, N_HEADS, s // ATT, 8, ATT))
    return ccol, crow


def _dil_bias(rel_bias, name):
    def body(rel_ref, t_ref, o_ref):
        for p in range(len(DIL_PATTERNS)):
            table = t_ref[p]

            def bucket(k, accs, table=table):
                return tuple(jnp.where(table == k, rel_ref[k, h], acc) for h, acc in enumerate(accs))

            accs = lax.fori_loop(0, REL_BUCKETS, bucket, tuple(jnp.full((BLK, 2 * BLK), NEG, F32) for _ in range(N_HEADS)))
            for h in range(N_HEADS):
                o_ref[p, h] = accs[h]

    vm = pl.BlockSpec(memory_space=pltpu.VMEM)
    return pl.pallas_call(
        body, name=name, in_specs=[pl.BlockSpec(memory_space=pltpu.SMEM), vm], out_specs=vm,
        out_shape=jax.ShapeDtypeStruct((len(DIL_PATTERNS), N_HEADS, BLK, 2 * BLK), F32),
        compiler_params=pltpu.CompilerParams(vmem_limit_bytes=VMEM_LIMIT),
    )(rel_bias, jnp.asarray(_bucket_table()))


def _layer_forward(x, wts, small, nb, s, tag, prefetch=None):
    proj = _matmul(x, wts["w_in"], "proj", tag)

    o_sb, carried = _sb_fwd(proj, nb, s, f"sb_fwd_{tag}", prefetch and prefetch.first())
    if prefetch:
        prefetch.got_first(carried)

    dl = proj[:, 3 * CONV_W:6 * CONV_W].reshape(nb, s, 3 * CONV_W).astype(MXU_DTYPE)
    qkvp = jnp.stack([_to_residue(dl, dil) for _, dil in DIL_PATTERNS])
    bias = _dil_bias(small["rel_bias"], f"dil_bias_{tag}")
    o_p, lse_p = _dil_fwd(qkvp, bias, f"dil_fwd_{tag}")
    o_nat = jnp.stack([_from_residue(o_p[p], dil) for p, (_, dil) in enumerate(DIL_PATTERNS)])
    lse_nat = _lanes(jnp.stack([_from_residue(lse_p[p][..., :1], dil) for p, (_, dil) in enumerate(DIL_PATTERNS)]))
    o_dl, lse_dl = _dil_combine(o_nat, lse_nat, f"dil_mix_{tag}")

    fb = jnp.zeros((8, BLK), F32).at[0, :N_HEADS].set(small["f_bias"])
    csum = _fox_gates_fwd(proj, fb, nb, s, f"fox_gates_{tag}")
    ccol, crow = _row_layout(csum, nb, s)
    (o_fx, lse_fx), carried = _fox_fwd(proj, ccol, crow, nb, s, f"fox_fwd_{tag}", prefetch and prefetch.second())
    if prefetch:
        prefetch.got_second(carried)

    cw = jnp.zeros((8, CONV_W), F32).at[:3].set(small["conv_w"])
    o_cv = _conv_fwd(proj, cw, nb, s, f"conv_fwd_{tag}")

    mixed = jnp.concatenate([o_sb, o_dl, o_fx, o_cv], axis=-1).astype(MXU_DTYPE)
    mix = _matmul(mixed, wts["w_out"], "out_proj", tag)
    pre1, x1 = _ln_fwd(x, mix, small["ln1_g"], small["ln1_b"], f"ln1_fwd_{tag}")
    gu = _matmul(x1, wts["w_gu"], "ffn_in", tag)
    hid = _swiglu_fwd(gu, f"swiglu_fwd_{tag}")
    ffn = _matmul(hid, wts["w_down"], "ffn_out", tag)
    pre2, x2 = _ln_fwd(x1, ffn, small["ln2_g"], small["ln2_b"], f"ln2_fwd_{tag}")
    saved = dict(x=x, proj=proj, qkvp=qkvp, bias=bias, o_dl=o_dl, lse_dl=lse_dl, fb=fb, ccol=ccol, crow=crow,
                 o_fx=o_fx, lse_fx=lse_fx, cw=cw, mixed=mixed, pre1=pre1, x1=x1, gu=gu, hid=hid, pre2=pre2)
    return x2, saved


def _layer_backward(dx2, sv, wts, small, nb, s, tag, reduce=None):
    t = nb * s
    dpre2, dgb2 = _ln_bwd(dx2, sv["pre2"], small["ln2_g"], f"ln2_bwd_{tag}")
    dpre2_b = dpre2.astype(MXU_DTYPE)
    dhid = _matmul(dpre2_b, wts["w_down"], "ffn_out_dx", tag, trans_b=True)
    dw_down = _matmul(sv["hid"].T, dpre2_b, "ffn_out_dw", tag)
    (dgu,), carried = _swiglu_bwd(dhid, sv["gu"], f"swiglu_bwd_{tag}", reduce and reduce.swap())
    if reduce:
        reduce.got_swap(carried)
    dx1 = _matmul(dgu, wts["w_gu"], "ffn_in_dx", tag, add=dpre2, add_scale=ALPHA, trans_b=True)
    dw_gu = _matmul(sv["x1"].astype(MXU_DTYPE).T, dgu, "ffn_in_dw", tag)

    dpre1, dgb1 = _ln_bwd(dx1, sv["pre1"], small["ln1_g"], f"ln1_bwd_{tag}")
    dpre1_b = dpre1.astype(MXU_DTYPE)
    dmixed = _matmul(dpre1_b, wts["w_out"], "out_proj_dx", tag, trans_b=True)
    dw_out = _matmul(sv["mixed"].T, dpre1_b, "out_proj_dw", tag)
    proj = sv["proj"]

    (dq_sb, dk_sb, dv_sb), carried = _sb_bwd(proj, dmixed, nb, s, f"sb_bwd_{tag}", reduce and reduce.scatter())
    if reduce:
        reduce.got_scatter(carried)

    delta_dl = _delta_kernel(dmixed, sv["o_dl"], nb, s, f"dil_delta_{tag}")
    do_dl = dmixed[:, CONV_W:2 * CONV_W].reshape(nb, s, CONV_W).astype(MXU_DTYPE)
    dop = jnp.stack([_to_residue(do_dl, dil) for _, dil in DIL_PATTERNS])
    lsep = _lanes(jnp.stack([_to_residue(sv["lse_dl"][..., :1], dil) for _, dil in DIL_PATTERNS]))
    deltap = _lanes(jnp.stack([_to_residue(delta_dl[..., :1], dil) for _, dil in DIL_PATTERNS]))
    dqp, dkp, dvp, gbias = _dil_bwd(sv["qkvp"], dop, lsep, deltap, sv["bias"], f"dil_bwd_{tag}")
    unperm = lambda tp: sum(_from_residue(tp[p], dil) for p, (_, dil) in enumerate(DIL_PATTERNS)).reshape(t, CONV_W)
    dq_dl, dk_dl, dv_dl = unperm(dqp), unperm(dkp), unperm(dvp)
    drel = _bucket_reduce(gbias, jnp.asarray(_bucket_table()), f"rel_bias_grad_{tag}")

    (dq_fx, dk_fx, dv_fx, dcol), carried = _fox_bwd(proj, dmixed, sv["lse_fx"], sv["ccol"], sv["crow"], nb, s,
                                                    f"fox_bwd_{tag}", reduce and reduce.share())
    if reduce:
        reduce.got_share(carried)
    dcs = -jnp.swapaxes(dcol[:, :, :, 0, :].reshape(nb, N_HEADS, s), 1, 2).reshape(t, N_HEADS)
    dcs = jnp.pad(dcs, ((0, 0), (0, BLK - N_HEADS)))
    dfx, dfb = _fox_gates_bwd(dcs, proj, sv["fb"], nb, s, f"fox_gates_bwd_{tag}")

    dgates, dcw = _conv_bwd(dmixed, proj, sv["cw"], nb, s, f"conv_bwd_{tag}")

    dproj = jnp.concatenate([dq_sb, dk_sb, dv_sb, dq_dl, dk_dl, dv_dl, dq_fx, dk_fx, dv_fx, dgates, dfx],
                            axis=-1).astype(MXU_DTYPE)
    dx = _matmul(dproj, wts["w_in"], "proj_dx", tag, add=dpre1, add_scale=ALPHA, trans_b=True)
    dw_in = _matmul(sv["x"].astype(MXU_DTYPE).T, dproj, "proj_dw", tag)

    grads = dict(w_in=dw_in[:, :PROJ], w_out=dw_out, w_gate=dw_gu[:, :D_FF], w_up=dw_gu[:, D_FF:], w_down=dw_down,
                 ln1_g=dgb1[0], ln1_b=dgb1[1], ln2_g=dgb2[0], ln2_b=dgb2[1], conv_w=dcw[:3], f_bias=dfb[0, :N_HEADS],
                 rel_bias=drel[:N_HEADS, :REL_BUCKETS].T)
    return dx, grads


def _local_step(x, target, weights_of, small_all, prefetch=None, make_reduce=None):
    nb, s, d = x.shape
    h = x.reshape(nb * s, d)
    saved = []
    for layer in range(DEPTH):
        wts = weights_of(layer)
        ahead = prefetch[layer + 1] if prefetch and layer + 1 < DEPTH else None
        h, sv = _layer_forward(h, wts, small_all[layer], nb, s, f"l{layer}", ahead)
        saved.append((sv, wts))
    dy, lossp = _loss_kernel(h, target.reshape(nb * s, d), "loss")
    grads, reduces, pending = [None] * DEPTH, [None] * DEPTH, None
    for layer in reversed(range(DEPTH)):
        sv, wts = saved[layer]
        dy, grads[layer] = _layer_backward(dy, sv, wts, small_all[layer], nb, s, f"l{layer}", pending)
        pending = reduces[layer] = make_reduce(layer, grads[layer]) if make_reduce else None
    if pending:
        pending.run()
    return lossp, dy.reshape(nb, s, d), grads, reduces


_BIG = ("w_in", "w_out", "w_gate", "w_up", "w_down")
_COL_SHARDED = ("w_in", "w_gate", "w_up")


def _full_weights(gathered):
    cols = lambda t: jnp.swapaxes(t, 0, 1).reshape(t.shape[1], -1)
    rows = lambda t: t.reshape(-1, t.shape[2])
    w_in = jnp.pad(cols(gathered["w_in"]), ((0, 0), (0, PROJ_PAD - PROJ)))
    w_gu = jnp.concatenate([cols(gathered["w_gate"]), cols(gathered["w_up"])], axis=-1)
    return dict(w_in=w_in, w_out=rows(gathered["w_out"]), w_gu=w_gu, w_down=rows(gathered["w_down"]))


def _by_chip(name, g):
    if name in _COL_SHARDED:
        return jnp.swapaxes(g.reshape(g.shape[0], N_CHIPS, -1), 0, 1)
    return g.reshape(N_CHIPS, -1, g.shape[1])


_SMALL_LAYOUT = (("ln1_g", 0), ("ln1_b", 2), ("ln2_g", 4), ("ln2_b", 6), ("conv_w", 8))
_ROW_MISC = 10
_ROW_LOSS = 11


def _pack_small(per_layer, rel_bias, loss=None):
    pk = jnp.zeros((SMALL_ROWS, D_MODEL), F32)
    for name, row in _SMALL_LAYOUT:
        for l in range(DEPTH):
            v = per_layer[l][name].reshape(-1)
            pk = pk.at[row + l, :v.shape[0]].set(v)
    fb = jnp.concatenate([per_layer[l]["f_bias"] for l in range(DEPTH)])
    pk = pk.at[_ROW_MISC, :2 * N_HEADS].set(fb)
    pk = pk.at[_ROW_MISC, BLK:BLK + REL_BUCKETS * N_HEADS].set(rel_bias.reshape(-1))
    if loss is not None:
        pk = pk.at[_ROW_LOSS, 0].set(loss)
    return pk


def _unpack_small(pk, conv_cols):
    out = {}
    for name, row in _SMALL_LAYOUT:
        n = 3 * conv_cols if name == "conv_w" else D_MODEL
        v = pk[row:row + DEPTH, :n]
        out[name] = v.reshape(DEPTH, 3, conv_cols) if name == "conv_w" else v
    out["f_bias"] = pk[_ROW_MISC, :2 * N_HEADS].reshape(DEPTH, N_HEADS)
    out["rel_bias"] = pk[_ROW_MISC, BLK:BLK + REL_BUCKETS * N_HEADS].reshape(REL_BUCKETS, N_HEADS)
    return out


_WEIGHTS = ("w_in", "f_bias", "conv_w", "w_out", "rel_bias", "ln1_g", "ln1_b", "w_gate", "w_up", "w_down", "ln2_g", "ln2_b")


def kernel(x, w_in, f_bias, conv_w, w_out, rel_bias, ln1_g, ln1_b, w_gate, w_up, w_down, ln2_g, ln2_b, loss_target, m_w_in, m_f_bias, m_conv_w, m_w_out, m_rel_bias, m_ln1_g, m_ln1_b, m_w_gate, m_w_up, m_w_down, m_ln2_g, m_ln2_b, v_w_in, v_f_bias, v_conv_w, v_w_out, v_rel_bias, v_ln1_g, v_ln1_b, v_w_gate, v_w_up, v_w_down, v_ln2_g, v_ln2_b):
    w = dict(w_in=w_in, f_bias=f_bias, conv_w=conv_w, w_out=w_out, rel_bias=rel_bias, ln1_g=ln1_g, ln1_b=ln1_b,
             w_gate=w_gate, w_up=w_up, w_down=w_down, ln2_g=ln2_g, ln2_b=ln2_b)
    m = dict(w_in=m_w_in, f_bias=m_f_bias, conv_w=m_conv_w, w_out=m_w_out, rel_bias=m_rel_bias, ln1_g=m_ln1_g,
             ln1_b=m_ln1_b, w_gate=m_w_gate, w_up=m_w_up, w_down=m_w_down, ln2_g=m_ln2_g, ln2_b=m_ln2_b)
    v = dict(w_in=v_w_in, f_bias=v_f_bias, conv_w=v_conv_w, w_out=v_w_out, rel_bias=v_rel_bias, ln1_g=v_ln1_g,
             ln1_b=v_ln1_b, w_gate=v_w_gate, w_up=v_w_up, w_down=v_w_down, ln2_g=v_ln2_g, ln2_b=v_ln2_b)
    chip = 2 * lax.axis_index("x") + lax.axis_index("y")
    core = lax.axis_index("c")
    conv_shard = CONV_W // N_CHIPS

    shards = [w[name].astype(MXU_DTYPE) for name in _BIG]
    fetch = [_WeightPrefetch(shards, l, chip) for l in range(DEPTH)]
    fetch[0].run("l0")
    cw_pk = jnp.zeros((8, D_MODEL), F32).at[0, :DEPTH * 3 * conv_shard].set(conv_w.reshape(-1))
    cw_all, _ = _gather_small(cw_pk, "gather_conv_w")
    cw_chips = cw_all[0::2, 0, :DEPTH * 3 * conv_shard].reshape(N_CHIPS, DEPTH, 3, conv_shard)
    conv_full = jnp.moveaxis(cw_chips, 0, 2).reshape(DEPTH, 3, CONV_W)
    small_all = [dict(f_bias=f_bias[l], conv_w=conv_full[l], rel_bias=rel_bias, ln1_g=ln1_g[l], ln1_b=ln1_b[l],
                      ln2_g=ln2_g[l], ln2_b=ln2_b[l]) for l in range(DEPTH)]

    lossp, grad_x, grads, reduces = _local_step(
        x, loss_target, lambda l: _full_weights(dict(zip(_BIG, fetch[l].result))), small_all, fetch,
        lambda l, g: _GradReduce([_by_chip(name, g[name]) for name in _BIG], chip, core, f"l{l}"))
    big_g = {name: jnp.stack([reduces[l].result[a] for l in range(DEPTH)]) for a, name in enumerate(_BIG)}

    drel = grads[0]["rel_bias"] + grads[1]["rel_bias"]
    small_pk = _pack_small(grads, drel, lossp[0, 0])
    _, small_sum = _gather_small(small_pk, "gather_small_grads")
    loss = small_sum[_ROW_LOSS, 0]
    small_g = _unpack_small(small_sum, CONV_W)
    small_g["conv_w"] = lax.dynamic_slice_in_dim(small_g["conv_w"], chip * conv_shard, conv_shard, axis=2)

    out_g, out_d, out_m, out_v = dict(small_g), {}, {}, {}
    for name in _BIG:
        out_g[name] = big_g[name]
        out_d[name], out_m[name], out_v[name] = _adamw(w[name], big_g[name], m[name], v[name], f"adamw_{name}")
    per_layer = lambda src: [{name: src[name][l] for name in ("ln1_g", "ln1_b", "ln2_g", "ln2_b", "conv_w", "f_bias")}
                             for l in range(DEPTH)]
    packs = [_pack_small(per_layer(src), src["rel_bias"])[None] for src in (w, small_g, m, v)]
    for dst, pk in zip((out_d, out_m, out_v), _adamw(*packs, "adamw_small")):
        dst.update(_unpack_small(pk[0], conv_shard))

    return (loss, grad_x, *[out_g[n] for n in _WEIGHTS], *[out_d[n] for n in _WEIGHTS],
            *[out_m[n] for n in _WEIGHTS], *[out_v[n] for n in _WEIGHTS])
```

```python
import functools
import math

import numpy as np
import jax
import jax.numpy as jnp
from jax import lax
from jax.experimental import pallas as pl
from jax.experimental.pallas import tpu as pltpu

F32 = jnp.float32
BF16 = jnp.bfloat16
MXU_DTYPE = BF16

D_MODEL = 1024
HEAD_DIM = 64
N_HEADS = 4
BLK = 128
ATT = 256
CONV_W = 256
PROJ = 3076
PROJ_PAD = 3200
D_FF = 2816
DEPTH = 2
ALPHA = (2 * DEPTH) ** 0.25
LN_EPS = 1e-5
NEG = -1e30
DIL_PATTERNS = ((128, 1), (512, 4), (2048, 16))
REL_BUCKETS = 32
N_CHIPS = 4
N_DEV = 8
SMALL_ROWS = 16

ADAM_LR = 0.001
ADAM_B1 = 0.9
ADAM_B2 = 0.999
ADAM_EPS = 1e-08
ADAM_WD = 0.01
ADAM_STEP = 10

VMEM_LIMIT = 48 * 2 ** 20
MESH = pl.DeviceIdType.MESH


def _cparams(*sem):
    return pltpu.CompilerParams(dimension_semantics=tuple(sem), vmem_limit_bytes=VMEM_LIMIT)


def _dot(a, b):
    return jnp.dot(a.astype(MXU_DTYPE), b.astype(MXU_DTYPE), preferred_element_type=F32)


def _dot_nt(a, b):
    return lax.dot_general(a.astype(MXU_DTYPE), b.astype(MXU_DTYPE), (((1,), (1,)), ((), ())),
                           preferred_element_type=F32)


def _dot_tn(a, b):
    return lax.dot_general(a.astype(MXU_DTYPE), b.astype(MXU_DTYPE), (((0,), (0,)), ((), ())),
                           preferred_element_type=F32)


def _split_dot(x, ones, passes):
    acc, rest = None, x
    for p in range(passes):
        piece = rest.astype(MXU_DTYPE)
        part = jnp.dot(piece, ones, preferred_element_type=F32)
        acc = part if acc is None else acc + part
        if p + 1 < passes:
            rest = rest - piece.astype(F32)
    return acc


def _split_dot_lhs(ones, x, passes):
    acc, rest = None, x
    for p in range(passes):
        piece = rest.astype(MXU_DTYPE)
        part = jnp.dot(ones, piece, preferred_element_type=F32)
        acc = part if acc is None else acc + part
        if p + 1 < passes:
            rest = rest - piece.astype(F32)
    return acc


def _iota2(shape, axis):
    return lax.broadcasted_iota(jnp.int32, shape, axis)


_TILES = {"proj": (1024, 640, 1024), "out_proj": (1024, 1024, 1024), "ffn_in": (1024, 1408, 1024),
          "ffn_out": (1024, 1024, 2816), "ffn_out_dx": (1024, 1408, 1024), "ffn_out_dw": (1408, 1024, 2048),
          "ffn_in_dx": (1024, 1024, 1408), "ffn_in_dw": (1024, 1408, 2048), "out_proj_dx": (1024, 1024, 1024),
          "out_proj_dw": (1024, 1024, 2048), "proj_dx": (1024, 1024, 640), "proj_dw": (1024, 640, 2048)}


def _matmul(a, b, kind, tag, *, out_dtype=F32, add=None, add_scale=1.0, trans_b=False):
    m, k = a.shape
    n = b.shape[0] if trans_b else b.shape[1]
    tm, tn, tk = _TILES[kind]
    tm, tk, name = min(tm, m), min(tk, k), f"{kind}_{tag}"
    assert m % tm == 0 and n % tn == 0 and k % tk == 0, (a.shape, b.shape, tm, tn, tk)
    nk = k // tk

    def body(*refs):
        if add is None:
            a_ref, b_ref, o_ref = refs[:3]
            c_ref, scr = None, refs[3:]
        else:
            a_ref, b_ref, c_ref, o_ref = refs[:4]
            scr = refs[4:]
        part = _dot_nt(a_ref[...], b_ref[...]) if trans_b else _dot(a_ref[...], b_ref[...])

        def finish(acc):
            if c_ref is not None:
                acc = acc + add_scale * c_ref[...]
            o_ref[...] = acc.astype(out_dtype)

        if nk == 1:
            finish(part)
        else:
            acc_ref = scr[0]
            kk = pl.program_id(2)

            @pl.when(kk == 0)
            def _():
                acc_ref[...] = part

            @pl.when(kk > 0)
            def _():
                acc_ref[...] += part

            @pl.when(kk == nk - 1)
            def _():
                finish(acc_ref[...])

    b_spec = pl.BlockSpec((tn, tk), lambda i, j, kk: (j, kk)) if trans_b else pl.BlockSpec((tk, tn), lambda i, j, kk: (kk, j))
    in_specs = [pl.BlockSpec((tm, tk), lambda i, j, kk: (i, kk)), b_spec]
    operands = [a, b]
    if add is not None:
        in_specs.append(pl.BlockSpec((tm, tn), lambda i, j, kk: (i, j)))
        operands.append(add)
    return pl.pallas_call(
        body, name=name, grid=(m // tm, n // tn, nk), in_specs=in_specs,
        out_specs=pl.BlockSpec((tm, tn), lambda i, j, kk: (i, j)),
        out_shape=jax.ShapeDtypeStruct((m, n), out_dtype),
        scratch_shapes=[pltpu.VMEM((tm, tn), F32)] if nk > 1 else [],
        compiler_params=_cparams("parallel", "parallel", "arbitrary"),
    )(*operands)


def _ln_stats(pre):
    mu = jnp.mean(pre, axis=-1, keepdims=True)
    xc = pre - mu
    var = jnp.mean(xc * xc, axis=-1, keepdims=True)
    rstd = lax.rsqrt(var + LN_EPS)
    return xc * rstd, rstd


def _ln_fwd(xin, branch, g, b, name):
    t, d = xin.shape
    tile = 256

    def body(x_ref, br_ref, g_ref, b_ref, pre_ref, y_ref):
        pre = ALPHA * x_ref[...] + br_ref[...]
        xhat, _ = _ln_stats(pre)
        pre_ref[...] = pre
        y_ref[...] = xhat * g_ref[...] + b_ref[...]

    row = pl.BlockSpec((tile, d), lambda i: (i, 0))
    vec = pl.BlockSpec((1, d), lambda i: (0, 0))
    return pl.pallas_call(
        body, name=name, grid=(t // tile,), in_specs=[row, row, vec, vec], out_specs=[row, row],
        out_shape=[jax.ShapeDtypeStruct((t, d), F32)] * 2, compiler_params=_cparams("parallel"),
    )(xin, branch, g.reshape(1, d), b.reshape(1, d))


def _ln_bwd(dy, pre, g, name):
    t, d = dy.shape
    tile = 256

    def body(dy_ref, pre_ref, g_ref, dpre_ref, dgb_ref):
        dyv = dy_ref[...]
        xhat, rstd = _ln_stats(pre_ref[...])
        dxh = dyv * g_ref[...]
        m1 = jnp.mean(dxh, axis=-1, keepdims=True)
        m2 = jnp.mean(dxh * xhat, axis=-1, keepdims=True)
        dpre_ref[...] = rstd * (dxh - m1 - xhat * m2)

        @pl.when(pl.program_id(0) == 0)
        def _():
            dgb_ref[...] = jnp.zeros_like(dgb_ref)

        dgb_ref[0:1, :] += jnp.sum(dyv * xhat, axis=0, keepdims=True)
        dgb_ref[1:2, :] += jnp.sum(dyv, axis=0, keepdims=True)

    row = pl.BlockSpec((tile, d), lambda i: (i, 0))
    return pl.pallas_call(
        body, name=name, grid=(t // tile,), in_specs=[row, row, pl.BlockSpec((1, d), lambda i: (0, 0))],
        out_specs=[row, pl.BlockSpec((8, d), lambda i: (0, 0))],
        out_shape=[jax.ShapeDtypeStruct((t, d), F32), jax.ShapeDtypeStruct((8, d), F32)],
        compiler_params=_cparams("arbitrary"),
    )(dy, pre, g.reshape(1, d))


def _swiglu_fwd(gu, name):
    t = gu.shape[0]
    tile = 256

    def body(gu_ref, h_ref):
        gate = gu_ref[:, :D_FF]
        up = gu_ref[:, D_FF:]
        h_ref[...] = (gate * (1.0 / (1.0 + jnp.exp(-gate))) * up).astype(h_ref.dtype)

    return pl.pallas_call(
        body, name=name, grid=(t // tile,), in_specs=[pl.BlockSpec((tile, 2 * D_FF), lambda i: (i, 0))],
        out_specs=pl.BlockSpec((tile, D_FF), lambda i: (i, 0)),
        out_shape=jax.ShapeDtypeStruct((t, D_FF), MXU_DTYPE), compiler_params=_cparams("parallel"),
    )(gu)


def _swiglu_bwd(dh, gu, name, carry=None):
    t = gu.shape[0]
    tile = 256

    def body(dh_ref, gu_ref, dgu_ref):
        gate = gu_ref[:, :D_FF]
        up = gu_ref[:, D_FF:]
        dhv = dh_ref[...]
        sig = 1.0 / (1.0 + jnp.exp(-gate))
        dgu_ref[:, :D_FF] = (dhv * up * sig * (1.0 + gate * (1.0 - sig))).astype(dgu_ref.dtype)
        dgu_ref[:, D_FF:] = (dhv * gate * sig).astype(dgu_ref.dtype)

    return _host_call(
        body, carry, name=name, grid=(t // tile,),
        in_specs=[pl.BlockSpec((tile, D_FF), lambda i: (i, 0)), pl.BlockSpec((tile, 2 * D_FF), lambda i: (i, 0))],
        out_specs=[pl.BlockSpec((tile, 2 * D_FF), lambda i: (i, 0))],
        out_shape=[jax.ShapeDtypeStruct((t, 2 * D_FF), MXU_DTYPE)], operands=(dh, gu))


def _loss_kernel(y, target, name):
    t, d = y.shape
    tile = 512

    def body(y_ref, t_ref, dy_ref, l_ref):
        err = y_ref[...] - t_ref[...]
        dy_ref[...] = err * (1.0 / d)

        @pl.when(pl.program_id(0) == 0)
        def _():
            l_ref[...] = jnp.zeros_like(l_ref)

        l_ref[...] += jnp.sum(err * err) * (0.5 / d)

    row = pl.BlockSpec((tile, d), lambda i: (i, 0))
    return pl.pallas_call(
        body, name=name, grid=(t // tile,), in_specs=[row, row],
        out_specs=[row, pl.BlockSpec((8, 128), lambda i: (0, 0))],
        out_shape=[jax.ShapeDtypeStruct((t, d), F32), jax.ShapeDtypeStruct((8, 128), F32)],
        compiler_params=_cparams("arbitrary"),
    )(y, target)


def _adamw(w, g, m, v, name):
    nl, r, c = w.shape
    tr = r
    for cand in (256, 352, 128, 64, 16, 8):
        if r % cand == 0:
            tr = cand
            break

    def body(w_ref, g_ref, m_ref, v_ref, d_ref, nm_ref, nv_ref):
        gv = g_ref[...]
        nm = ADAM_B1 * m_ref[...] + (1.0 - ADAM_B1) * gv
        nv = ADAM_B2 * v_ref[...] + (1.0 - ADAM_B2) * (gv * gv)
        m_hat = nm / (1.0 - ADAM_B1 ** ADAM_STEP)
        v_hat = nv / (1.0 - ADAM_B2 ** ADAM_STEP)
        d_ref[...] = -ADAM_LR * (m_hat / (jnp.sqrt(v_hat) + ADAM_EPS) + ADAM_WD * w_ref[...])
        nm_ref[...] = nm
        nv_ref[...] = nv

    blk = pl.BlockSpec((1, tr, c), lambda l, i: (l, i, 0))
    return pl.pallas_call(
        body, name=name, grid=(nl, r // tr), in_specs=[blk] * 4, out_specs=[blk] * 3,
        out_shape=[jax.ShapeDtypeStruct(w.shape, F32)] * 3, compiler_params=_cparams("parallel", "parallel"),
    )(w, g, m, v)


def _shift_down(u, k, rows):
    return jnp.where(rows >= k, pltpu.roll(u, k, 0), 0.0)


def _shift_up(u, k, rows, s):
    return jnp.where(rows < s - k, pltpu.roll(u, s - k, 0), 0.0)


def _conv_fwd(proj, conv_w, nb, s, name):
    def body(b_ref, c_ref, h_ref, w_ref, o_ref):
        rows = _iota2((s, CONV_W), 0)
        u = c_ref[...] * h_ref[...]
        y = w_ref[2:3, :] * u + w_ref[1:2, :] * _shift_down(u, 1, rows) + w_ref[0:1, :] * _shift_down(u, 2, rows)
        o_ref[...] = b_ref[...] * y

    col = lambda j: pl.BlockSpec((s, CONV_W), lambda b: (b, j))
    return pl.pallas_call(
        body, name=name, grid=(nb,),
        in_specs=[col(9), col(10), col(11), pl.BlockSpec((8, CONV_W), lambda b: (0, 0))],
        out_specs=pl.BlockSpec((s, CONV_W), lambda b: (b, 0)),
        out_shape=jax.ShapeDtypeStruct((nb * s, CONV_W), F32), compiler_params=_cparams("parallel"),
    )(proj, proj, proj, conv_w)


def _conv_bwd(dmixed, proj, conv_w, nb, s, name):
    def body(do_ref, b_ref, c_ref, h_ref, w_ref, dg_ref, dw_ref):
        rows = _iota2((s, CONV_W), 0)
        cg, hg, bg, dout = c_ref[...], h_ref[...], b_ref[...], do_ref[...]
        u = cg * hg
        u1 = _shift_down(u, 1, rows)
        u2 = _shift_down(u, 2, rows)
        y = w_ref[2:3, :] * u + w_ref[1:2, :] * u1 + w_ref[0:1, :] * u2
        dy = dout * bg
        du = w_ref[2:3, :] * dy + w_ref[1:2, :] * _shift_up(dy, 1, rows, s) + w_ref[0:1, :] * _shift_up(dy, 2, rows, s)
        dg_ref[:, 0:CONV_W] = dout * y
        dg_ref[:, CONV_W:2 * CONV_W] = du * hg
        dg_ref[:, 2 * CONV_W:3 * CONV_W] = du * cg

        @pl.when(pl.program_id(0) == 0)
        def _():
            dw_ref[...] = jnp.zeros_like(dw_ref)

        dw_ref[0:1, :] += jnp.sum(dy * u2, axis=0, keepdims=True)
        dw_ref[1:2, :] += jnp.sum(dy * u1, axis=0, keepdims=True)
        dw_ref[2:3, :] += jnp.sum(dy * u, axis=0, keepdims=True)

    col = lambda j: pl.BlockSpec((s, CONV_W), lambda b: (b, j))
    return pl.pallas_call(
        body, name=name, grid=(nb,),
        in_specs=[col(3), col(9), col(10), col(11), pl.BlockSpec((8, CONV_W), lambda b: (0, 0))],
        out_specs=[pl.BlockSpec((s, 3 * CONV_W), lambda b: (b, 0)), pl.BlockSpec((8, CONV_W), lambda b: (0, 0))],
        out_shape=[jax.ShapeDtypeStruct((nb * s, 3 * CONV_W), F32), jax.ShapeDtypeStruct((8, CONV_W), F32)],
        compiler_params=_cparams("arbitrary"),
    )(dmixed, proj, proj, proj, conv_w)


def _col_spec(s, base):
    return pl.BlockSpec((s, BLK), lambda b, p: (b, base + p))


def _rows(i):
    return pl.ds(pl.multiple_of(i * ATT, ATT), ATT)


def _rows128(i):
    return pl.ds(pl.multiple_of(i * BLK, BLK), BLK)


def _log_sigmoid_parts(z):
    e = jnp.exp(-jnp.abs(z))
    l1p = jnp.log(1.0 + e)
    lb = jnp.minimum(z, 0.0) - l1p
    return lb, lb - z, e


def _head_masks():
    lane = _iota2((1, BLK), 1)
    return [(lane >= h * HEAD_DIM) & (lane < (h + 1) * HEAD_DIM) for h in range(2)]


def _split_heads(ref, scr, sels):
    for h, sel in enumerate(sels):
        scr[h] = jnp.where(sel, ref[...], 0.0).astype(MXU_DTYPE)


def _sb_fwd(proj, nb, s, name, carry=None):
    nblk = s // ATT

    def body(q_ref, k_ref, v_ref, o_ref, km, vm):
        sels = _head_masks()
        _split_heads(k_ref, km, sels)
        _split_heads(v_ref, vm, sels)
        rows = _iota2((ATT, ATT), 0)
        cols = _iota2((ATT, ATT), 1)
        later = (rows > cols).astype(MXU_DTYPE)

        def qblock(i, _):
            qi = (q_ref[_rows(i), :] * 0.125).astype(MXU_DTYPE)

            def kblock(t, state):
                carries, acc = state
                j = i - t
                strict = (cols + (j - i) * ATT) < rows
                out = []
                for h in range(2):
                    z = _dot_nt(qi, km[h, _rows(j), :])
                    lb, lr, _ = _log_sigmoid_parts(z)
                    lr = jnp.where(strict, lr, 0.0)
                    tail = _split_dot(lr, later, 2) + carries[h]
                    a = jnp.where(strict, jnp.exp(lb + tail), 0.0)
                    acc = acc + _dot(a, vm[h, _rows(j), :])
                    out.append(carries[h] + jnp.sum(lr, axis=-1, keepdims=True))
                return tuple(out), acc

            init = ((jnp.zeros((ATT, 1), F32),) * 2, jnp.zeros((ATT, BLK), F32))
            _, acc = lax.fori_loop(0, i + 1, kblock, init)
            o_ref[_rows(i), :] = acc
            return 0

        lax.fori_loop(0, nblk, qblock, 0)

    (o,), extra = _host_call(
        body, carry, name=name, grid=(nb, 2), in_specs=[_col_spec(s, 0), _col_spec(s, 2), _col_spec(s, 4)],
        out_specs=[_col_spec(s, 0)], out_shape=[jax.ShapeDtypeStruct((nb * s, 2 * BLK), F32)],
        scratch_shapes=[pltpu.VMEM((2, s, BLK), MXU_DTYPE)] * 2, operands=(proj, proj, proj))
    return o, extra


def _sb_bwd(proj, dmixed, nb, s, name, carry=None):
    nblk = s // ATT

    def body(q_ref, k_ref, v_ref, do_ref, dq_ref, dk_ref, dv_ref, km, vm, a_scr, dl_scr, beta_scr):
        sels = _head_masks()
        _split_heads(k_ref, km, sels)
        _split_heads(v_ref, vm, sels)
        rows = _iota2((ATT, ATT), 0)
        cols = _iota2((ATT, ATT), 1)
        later = (rows > cols).astype(MXU_DTYPE)
        earlier = (rows < cols).astype(MXU_DTYPE)
        dk_ref[...] = jnp.zeros_like(dk_ref)
        dv_ref[...] = jnp.zeros_like(dv_ref)

        def qblock(i, _):
            qi = (q_ref[_rows(i), :] * 0.125).astype(MXU_DTYPE)
            doi = do_ref[_rows(i), :].astype(MXU_DTYPE)
            qm = [jnp.where(sel, qi, 0.0) for sel in sels]
            dom = [jnp.where(sel, doi, 0.0) for sel in sels]

            def first(t, carries):
                j = i - t
                strict = (cols + (j - i) * ATT) < rows
                out = []
                for h in range(2):
                    z = _dot_nt(qi, km[h, _rows(j), :])
                    lb, lr, e = _log_sigmoid_parts(z)
                    lr = jnp.where(strict, lr, 0.0)
                    tail = _split_dot(lr, later, 2) + carries[h]
                    a = jnp.where(strict, jnp.exp(lb + tail), 0.0)
                    a_scr[h, j] = a
                    dl_scr[h, j] = a * _dot_nt(doi, vm[h, _rows(j), :])
                    beta_scr[h, j] = jnp.exp(lb)
                    out.append(carries[h] + jnp.sum(lr, axis=-1, keepdims=True))
                return tuple(out)

            lax.fori_loop(0, i + 1, first, (jnp.zeros((ATT, 1), F32),) * 2)

            def second(j, state):
                csums, dq = state
                strict = (cols + (j - i) * ATT) < rows
                out = []
                for h in range(2):
                    dl = dl_scr[h, j]
                    beta = beta_scr[h, j]
                    before = _split_dot(dl, earlier, 2) + csums[h]
                    dz = jnp.where(strict, dl * (1.0 - beta) - beta * before, 0.0).astype(MXU_DTYPE)
                    dq = dq + _dot(dz, km[h, _rows(j), :])
                    dk_ref[_rows(j), :] += _dot_tn(dz, qm[h])
                    dv_ref[_rows(j), :] += _dot_tn(a_scr[h, j], dom[h])
                    out.append(csums[h] + jnp.sum(dl, axis=-1, keepdims=True))
                return tuple(out), dq

            init = ((jnp.zeros((ATT, 1), F32),) * 2, jnp.zeros((ATT, BLK), F32))
            _, dq = lax.fori_loop(0, i + 1, second, init)
            dq_ref[_rows(i), :] = dq * 0.125
            return 0

        lax.fori_loop(0, nblk, qblock, 0)

    out = _col_spec(s, 0)
    return _host_call(
        body, carry, name=name, grid=(nb, 2),
        in_specs=[_col_spec(s, 0), _col_spec(s, 2), _col_spec(s, 4), out], out_specs=[out] * 3,
        out_shape=[jax.ShapeDtypeStruct((nb * s, 2 * BLK), F32)] * 3,
        scratch_shapes=[pltpu.VMEM((2, s, BLK), MXU_DTYPE)] * 2 + [pltpu.VMEM((2, nblk, ATT, ATT), F32)] * 3,
        operands=(proj, proj, proj, dmixed))


def _pair_spec(s, width):
    return pl.BlockSpec((None, 2, s, width), lambda b, p: (b, p, 0, 0))


def _fox_fwd(proj, ccol, crow, nb, s, name, carry=None):
    nblk = s // ATT

    def body(q_ref, k_ref, v_ref, cc_ref, cr_ref, o_ref, lse_ref, km, vm):
        sels = _head_masks()
        _split_heads(k_ref, km, sels)
        _split_heads(v_ref, vm, sels)
        rows = _iota2((ATT, ATT), 0)
        cols = _iota2((ATT, ATT), 1)

        def qblock(i, _):
            qi = (q_ref[_rows(i), :] * 0.125).astype(MXU_DTYPE)
            ci = [cc_ref[h, _rows(i), :] for h in range(2)]

            def kblock(j, state):
                ms, ls, acc = state
                causal = (cols + (j - i) * ATT) <= rows
                new_m, new_l, scales, parts = [], [], [], []
                for h in range(2):
                    z = _dot_nt(qi, km[h, _rows(j), :]) + (ci[h] - cr_ref[h, j][0:1, :])
                    z = jnp.where(causal, z, NEG)
                    m_new = jnp.maximum(ms[h], jnp.max(z, axis=-1, keepdims=True))
                    p = jnp.exp(z - m_new)
                    scale = jnp.exp(ms[h] - m_new)
                    new_m.append(m_new)
                    new_l.append(scale * ls[h] + jnp.sum(p, axis=-1, keepdims=True))
                    scales.append(scale)
                    parts.append(_dot(p, vm[h, _rows(j), :]))
                acc = jnp.where(sels[0], scales[0], scales[1]) * acc + parts[0] + parts[1]
                return tuple(new_m), tuple(new_l), acc

            init = ((jnp.full((ATT, 1), NEG, F32),) * 2, (jnp.zeros((ATT, 1), F32),) * 2, jnp.zeros((ATT, BLK), F32))
            ms, ls, acc = lax.fori_loop(0, i + 1, kblock, init)
            o_ref[_rows(i), :] = acc / jnp.where(sels[0], ls[0], ls[1])
            for h in range(2):
                lse_ref[h, _rows(i), :] = jnp.broadcast_to(ms[h] + jnp.log(ls[h]), (ATT, ATT))
            return 0

        lax.fori_loop(0, nblk, qblock, 0)

    crow_spec = pl.BlockSpec((None, 2, nblk, 8, ATT), lambda b, p: (b, p, 0, 0, 0))
    return _host_call(
        body, carry, name=name, grid=(nb, 2),
        in_specs=[_col_spec(s, 12), _col_spec(s, 14), _col_spec(s, 16), _pair_spec(s, ATT), crow_spec],
        out_specs=[_col_spec(s, 0), _pair_spec(s, ATT)],
        out_shape=[jax.ShapeDtypeStruct((nb * s, 2 * BLK), F32), jax.ShapeDtypeStruct((nb, N_HEADS, s, ATT), F32)],
        scratch_shapes=[pltpu.VMEM((2, s, BLK), MXU_DTYPE)] * 2, operands=(proj, proj, proj, ccol, crow))


def _fox_bwd(proj, dmixed, lse, ccol, crow, nb, s, name, carry=None):
    nblk = s // ATT

    def body(q_ref, k_ref, v_ref, do_ref, lse_ref, cc_ref, cr_ref, dq_ref, dk_ref, dv_ref, dc_ref, km, vm):
        sels = _head_masks()
        _split_heads(k_ref, km, sels)
        _split_heads(v_ref, vm, sels)
        rows = _iota2((ATT, ATT), 0)
        cols = _iota2((ATT, ATT), 1)
        dk_ref[...] = jnp.zeros_like(dk_ref)
        dv_ref[...] = jnp.zeros_like(dv_ref)
        dc_ref[...] = jnp.zeros_like(dc_ref)

        def qblock(i, _):
            qi = (q_ref[_rows(i), :] * 0.125).astype(MXU_DTYPE)
            doi = do_ref[_rows(i), :].astype(MXU_DTYPE)
            qm = [jnp.where(sel, qi, 0.0) for sel in sels]
            dom = [jnp.where(sel, doi, 0.0) for sel in sels]
            ci = [cc_ref[h, _rows(i), :] for h in range(2)]
            lsei = [lse_ref[h, _rows(i), :] for h in range(2)]

            def probs(j, h):
                z = _dot_nt(qi, km[h, _rows(j), :]) + (ci[h] - cr_ref[h, j][0:1, :])
                p = jnp.where((cols + (j - i) * ATT) <= rows, jnp.exp(z - lsei[h]), 0.0)
                return p, _dot_nt(doi, vm[h, _rows(j), :])

            def row_term(j, accs):
                out = []
                for h in range(2):
                    p, dp = probs(j, h)
                    out.append(accs[h] + jnp.sum(p * dp, axis=-1, keepdims=True))
                return tuple(out)

            di = lax.fori_loop(0, i + 1, row_term, (jnp.zeros((ATT, 1), F32),) * 2)

            def kblock(j, dq):
                for h in range(2):
                    p, dp = probs(j, h)
                    ds = p * (dp - di[h])
                    dc_ref[h, j] += jnp.broadcast_to(jnp.sum(ds, axis=0, keepdims=True), (8, ATT))
                    ds = ds.astype(MXU_DTYPE)
                    dk_ref[_rows(j), :] += _dot_tn(ds, qm[h])
                    dv_ref[_rows(j), :] += _dot_tn(p, dom[h])
                    dq = dq + _dot(ds, km[h, _rows(j), :])
                return dq

            dq = lax.fori_loop(0, i + 1, kblock, jnp.zeros((ATT, BLK), F32))
            dq_ref[_rows(i), :] = dq * 0.125
            return 0

        lax.fori_loop(0, nblk, qblock, 0)

    crow_spec = pl.BlockSpec((None, 2, nblk, 8, ATT), lambda b, p: (b, p, 0, 0, 0))
    wide, cols_out = _pair_spec(s, ATT), _col_spec(s, 0)
    return _host_call(
        body, carry, name=name, grid=(nb, 2),
        in_specs=[_col_spec(s, 12), _col_spec(s, 14), _col_spec(s, 16), _col_spec(s, 4), wide, wide, crow_spec],
        out_specs=[cols_out, cols_out, cols_out, crow_spec],
        out_shape=[jax.ShapeDtypeStruct((nb * s, 2 * BLK), F32)] * 3 + [jax.ShapeDtypeStruct((nb, N_HEADS, nblk, 8, ATT), F32)],
        scratch_shapes=[pltpu.VMEM((2, s, BLK), MXU_DTYPE)] * 2, operands=(proj, proj, proj, dmixed, lse, ccol, crow))


def _fox_gates_fwd(proj, f_bias, nb, s, name):
    chunk = 256

    def body(f_ref, b_ref, c_ref):
        lower = (_iota2((chunk, chunk), 0) >= _iota2((chunk, chunk), 1)).astype(MXU_DTYPE)
        carry = jnp.zeros((1, BLK), F32)
        for n in range(s // chunk):
            rows = pl.ds(n * chunk, chunk)
            lf, _, _ = _log_sigmoid_parts(f_ref[rows, :] + b_ref[0:1, :])
            c = _split_dot_lhs(lower, lf, 3) + carry
            c_ref[rows, :] = c
            carry = c[chunk - 1:chunk, :]

    return pl.pallas_call(
        body, name=name, grid=(nb,),
        in_specs=[pl.BlockSpec((s, BLK), lambda b: (b, (PROJ_PAD - BLK) // BLK)), pl.BlockSpec((8, BLK), lambda b: (0, 0))],
        out_specs=pl.BlockSpec((s, BLK), lambda b: (b, 0)),
        out_shape=jax.ShapeDtypeStruct((nb * s, BLK), F32), compiler_params=_cparams("parallel"),
    )(proj, f_bias)


def _fox_gates_bwd(dc, proj, f_bias, nb, s, name):
    chunk = 256

    def body(dc_ref, f_ref, b_ref, df_ref, db_ref):
        upper = (_iota2((chunk, chunk), 0) <= _iota2((chunk, chunk), 1)).astype(MXU_DTYPE)
        carry = jnp.zeros((1, BLK), F32)
        total = jnp.zeros((1, BLK), F32)
        for n in reversed(range(s // chunk)):
            rows = pl.ds(n * chunk, chunk)
            dlf = _split_dot_lhs(upper, dc_ref[rows, :], 3) + carry
            carry = dlf[0:1, :]
            pre = f_ref[rows, :] + b_ref[0:1, :]
            e = jnp.exp(-jnp.abs(pre))
            df = dlf * (jnp.where(pre >= 0.0, e, 1.0) / (1.0 + e))
            df_ref[rows, :] = df
            total = total + jnp.sum(df, axis=0, keepdims=True)

        @pl.when(pl.program_id(0) == 0)
        def _():
            db_ref[...] = jnp.zeros_like(db_ref)

        db_ref[0:1, :] += total

    return pl.pallas_call(
        body, name=name, grid=(nb,),
        in_specs=[pl.BlockSpec((s, BLK), lambda b: (b, 0)), pl.BlockSpec((s, BLK), lambda b: (b, (PROJ_PAD - BLK) // BLK)),
                  pl.BlockSpec((8, BLK), lambda b: (0, 0))],
        out_specs=[pl.BlockSpec((s, BLK), lambda b: (b, 0)), pl.BlockSpec((8, BLK), lambda b: (0, 0))],
        out_shape=[jax.ShapeDtypeStruct((nb * s, BLK), F32), jax.ShapeDtypeStruct((8, BLK), F32)],
        compiler_params=_cparams("arbitrary"),
    )(dc, proj, f_bias)


def _delta_kernel(dmixed, o, nb, s, name):
    def body(do_ref, o_ref, d_ref):
        prod = do_ref[...] * o_ref[...]
        for h, sel in enumerate(_head_masks()):
            d_ref[h] = jnp.broadcast_to(jnp.sum(jnp.where(sel, prod, 0.0), axis=-1, keepdims=True), (s, BLK))

    return pl.pallas_call(
        body, name=name, grid=(nb, 2), in_specs=[_col_spec(s, 2), _col_spec(s, 0)], out_specs=_pair_spec(s, BLK),
        out_shape=jax.ShapeDtypeStruct((nb, N_HEADS, s, BLK), F32), compiler_params=_cparams("parallel", "parallel"),
    )(dmixed, o)


def _t5_bucket_np(dist):
    max_exact = REL_BUCKETS // 2
    nf = np.maximum(dist, 1).astype(np.float32)
    large = max_exact + (np.log(nf / max_exact) / math.log(2048 / max_exact) * (REL_BUCKETS - max_exact)).astype(np.int32)
    large = np.minimum(large, REL_BUCKETS - 1)
    return np.where(dist < max_exact, dist, large)


def _bucket_table():
    qi = np.arange(BLK)[:, None]
    kj = np.arange(2 * BLK)[None, :]
    dist = qi + BLK - kj
    tables = []
    for window, dil in DIL_PATTERNS:
        in_band = (dist >= 0) & (dist <= window // dil)
        tables.append(np.where(in_band, _t5_bucket_np(np.maximum(dist, 0) * dil), -1).astype(np.int32))
    return np.stack(tables)


def _dil_scores(qb, kp, kc, b_ref, h, prev_valid):
    zp = _dot_nt(qb, kp) + b_ref[h, :, 0:BLK]
    zp = jnp.where(prev_valid, zp, NEG)
    zc = _dot_nt(qb, kc) + b_ref[h, :, BLK:2 * BLK]
    return zp, zc


def _residue_rows(b, seg, dil):
    if dil == 1:
        return _rows128(b), _rows128(jnp.maximum(b - 1, 0)), b > 0
    r, n = b // seg, b % seg
    cur = pl.ds(r + dil * n * BLK, BLK, stride=dil)
    prev = pl.ds(r + dil * jnp.maximum(n - 1, 0) * BLK, BLK, stride=dil)
    return cur, prev, n > 0


def _dil_attention_fwd(proj, bias, nb, s, name):
    nblk = s // BLK

    def body(q_ref, k_ref, v_ref, b_ref, out_ref, lse_ref, o_scr, l_scr):
        sels = _head_masks()
        for p, (_, dil) in enumerate(DIL_PATTERNS):
            seg = s // dil // BLK

            def block(b, _, p=p, seg=seg, dil=dil):
                cur, prev, has_prev = _residue_rows(b, seg, dil)
                qb = (q_ref[cur, :] * 0.125).astype(MXU_DTYPE)
                kp, kc = k_ref[prev, :].astype(MXU_DTYPE), k_ref[cur, :].astype(MXU_DTYPE)
                vp, vc = v_ref[prev, :].astype(MXU_DTYPE), v_ref[cur, :].astype(MXU_DTYPE)
                acc = jnp.zeros((BLK, BLK), F32)
                for h, sel in enumerate(sels):
                    zp, zc = _dil_scores(qb, jnp.where(sel, kp, 0.0), jnp.where(sel, kc, 0.0), b_ref.at[p], h, has_prev)
                    m = jnp.maximum(jnp.max(zp, axis=-1, keepdims=True), jnp.max(zc, axis=-1, keepdims=True))
                    pp = jnp.exp(zp - m)
                    pc = jnp.exp(zc - m)
                    den = jnp.sum(pp, axis=-1, keepdims=True) + jnp.sum(pc, axis=-1, keepdims=True)
                    acc = acc + (_dot(pp, jnp.where(sel, vp, 0.0)) + _dot(pc, jnp.where(sel, vc, 0.0))) / den
                    l_scr[p, h, cur, :] = jnp.broadcast_to(m + jnp.log(den), (BLK, BLK))
                o_scr[p, cur, :] = acc
                return 0

            lax.fori_loop(0, nblk, block, 0, unroll=2)

        weights, dens = [], []
        for h in range(2):
            m = jnp.maximum(jnp.maximum(l_scr[0, h], l_scr[1, h]), l_scr[2, h])
            w = [jnp.exp(l_scr[p, h] - m) for p in range(3)]
            den = w[0] + w[1] + w[2]
            lse_ref[h] = m + jnp.log(den)
            weights.append(w)
            dens.append(den)
        num = sum(jnp.where(sels[0], weights[0][p], weights[1][p]) * o_scr[p] for p in range(3))
        out_ref[...] = num / jnp.where(sels[0], dens[0], dens[1])

    bias_spec = pl.BlockSpec((3, 2, BLK, 2 * BLK), lambda b, p: (0, p, 0, 0))
    return pl.pallas_call(
        body, name=name, grid=(nb, 2), in_specs=[_col_spec(s, 6), _col_spec(s, 8), _col_spec(s, 10), bias_spec],
        out_specs=[_col_spec(s, 0), _pair_spec(s, BLK)],
        out_shape=[jax.ShapeDtypeStruct((nb * s, 2 * BLK), F32), jax.ShapeDtypeStruct((nb, N_HEADS, s, BLK), F32)],
        scratch_shapes=[pltpu.VMEM((3, s, BLK), F32), pltpu.VMEM((3, 2, s, BLK), F32)],
        compiler_params=_cparams("parallel", "parallel"),
    )(proj, proj, proj, bias)


def _dil_attention_bwd(proj, dmixed, lse, delta, bias, nb, s, name):
    nblk = s // BLK

    def body(q_ref, k_ref, v_ref, do_ref, lse_ref, dl_ref, b_ref, dq_ref, dk_ref, dv_ref, g_ref):
        sels = _head_masks()
        dq_ref[...] = jnp.zeros_like(dq_ref)
        dk_ref[...] = jnp.zeros_like(dk_ref)
        dv_ref[...] = jnp.zeros_like(dv_ref)
        g_ref[...] = jnp.zeros_like(g_ref)
        for p, (_, dil) in enumerate(DIL_PATTERNS):
            seg = s // dil // BLK

            def block(b, _, p=p, seg=seg, dil=dil):
                cur, prev, has_prev = _residue_rows(b, seg, dil)
                qb = (q_ref[cur, :] * 0.125).astype(MXU_DTYPE)
                dob = do_ref[cur, :].astype(MXU_DTYPE)
                kp, kc = k_ref[prev, :].astype(MXU_DTYPE), k_ref[cur, :].astype(MXU_DTYPE)
                vp, vc = v_ref[prev, :].astype(MXU_DTYPE), v_ref[cur, :].astype(MXU_DTYPE)
                dq = jnp.zeros((BLK, BLK), F32)
                dkp, dkc, dvp, dvc = dq, dq, dq, dq
                for h, sel in enumerate(sels):
                    kph, kch = jnp.where(sel, kp, 0.0), jnp.where(sel, kc, 0.0)
                    qh, doh = jnp.where(sel, qb, 0.0), jnp.where(sel, dob, 0.0)
                    lse_h = lse_ref[h, cur, :]
                    dlt = dl_ref[h, cur, :]
                    zp, zc = _dil_scores(qb, kph, kch, b_ref.at[p], h, has_prev)
                    pp = jnp.exp(zp - lse_h)
                    pc = jnp.exp(zc - lse_h)
                    dsp = pp * (_dot_nt(dob, jnp.where(sel, vp, 0.0)) - dlt)
                    dsc = pc * (_dot_nt(dob, jnp.where(sel, vc, 0.0)) - dlt)
                    g_ref[h, p, :, 0:BLK] += dsp
                    g_ref[h, p, :, BLK:2 * BLK] += dsc
                    dsp = dsp.astype(MXU_DTYPE)
                    dsc = dsc.astype(MXU_DTYPE)
                    dq = dq + _dot(dsp, kph) + _dot(dsc, kch)
                    dkp, dkc = dkp + _dot_tn(dsp, qh), dkc + _dot_tn(dsc, qh)
                    dvp, dvc = dvp + _dot_tn(pp, doh), dvc + _dot_tn(pc, doh)
                dq_ref[cur, :] += dq * 0.125
                dk_ref[prev, :] += dkp
                dk_ref[cur, :] += dkc
                dv_ref[prev, :] += dvp
                dv_ref[cur, :] += dvc
                return 0

            lax.fori_loop(0, nblk, block, 0, unroll=2)

    bias_spec = pl.BlockSpec((3, 2, BLK, 2 * BLK), lambda b, p: (0, p, 0, 0))
    cols, stats = _col_spec(s, 0), _pair_spec(s, BLK)
    return pl.pallas_call(
        body, name=name, grid=(nb, 2),
        in_specs=[_col_spec(s, 6), _col_spec(s, 8), _col_spec(s, 10), _col_spec(s, 2), stats, stats, bias_spec],
        out_specs=[cols, cols, cols, pl.BlockSpec((None, 2, 3, BLK, 2 * BLK), lambda b, p: (b, p, 0, 0, 0))],
        out_shape=[jax.ShapeDtypeStruct((nb * s, 2 * BLK), F32)] * 3 + [jax.ShapeDtypeStruct((nb, N_HEADS, 3, BLK, 2 * BLK), F32)],
        compiler_params=_cparams("parallel", "parallel"),
    )(proj, proj, proj, dmixed, lse, delta, bias)


def _bucket_reduce(gbias, table, name):
    nb = gbias.shape[0]

    def body(g_ref, t_ref, o_ref):
        row = _iota2((8, BLK), 0)
        lane = _iota2((8, BLK), 1)
        gsum = [[sum(g_ref[b, h, p] for b in range(nb)) for p in range(3)] for h in range(N_HEADS)]

        def bucket(k, acc):
            for h in range(N_HEADS):
                tot = sum(jnp.sum(jnp.where(t_ref[p] == k, gsum[h][p], 0.0)) for p in range(3))
                acc = acc + jnp.where((row == h) & (lane == k), tot, 0.0)
            return acc

        o_ref[...] = lax.fori_loop(0, REL_BUCKETS, bucket, jnp.zeros((8, BLK), F32))

    vm = pl.BlockSpec(memory_space=pltpu.VMEM)
    return pl.pallas_call(
        body, name=name, in_specs=[vm, vm], out_specs=vm, out_shape=jax.ShapeDtypeStruct((8, BLK), F32),
        compiler_params=pltpu.CompilerParams(vmem_limit_bytes=VMEM_LIMIT),
    )(gbias, table)


def _place():
    x, y, c = lax.axis_index("x"), lax.axis_index("y"), lax.axis_index("c")
    others = [(1 - x, y), (x, 1 - y), (1 - x, 1 - y)]
    return x, y, c, others


def _remote(src, dst, send_sem, recv_sem, to):
    return pltpu.make_async_remote_copy(src_ref=src, dst_ref=dst, send_sem=send_sem, recv_sem=recv_sem,
                                        device_id=to, device_id_type=MESH)


_HBM = pl.BlockSpec(memory_space=pl.ANY)


class _Exchange:
    def __init__(self, operands, out_shape, n_copies, copies, aliases=None):
        self.operands, self.out_shape, self.n_copies, self.copies = list(operands), list(out_shape), n_copies, copies
        self.aliases = dict(aliases or {})

    def sem_shapes(self):
        return [pltpu.SemaphoreType.DMA((self.n_copies,)), pltpu.SemaphoreType.DMA((self.n_copies,))]


def _start_all(sends):
    for cp in sends:
        cp.start()


def _wait_all(sends, arrivals):
    for cp in arrivals:
        cp.wait_recv()
    for cp in sends:
        cp.wait_send()


def _run_exchange(ex, name):
    ni = len(ex.operands)

    def body(*refs):
        sends, arrivals = ex.copies(refs[:ni], refs[ni:-2], refs[-2], refs[-1])
        _start_all(sends)
        _wait_all(sends, arrivals)

    return list(pl.pallas_call(
        body, name=name, in_specs=[_HBM] * ni, out_specs=[_HBM] * len(ex.out_shape), out_shape=ex.out_shape,
        scratch_shapes=ex.sem_shapes(), input_output_aliases=ex.aliases)(*ex.operands))


def _host_call(body, carry, *, name, grid, in_specs, out_specs, out_shape, operands, scratch_shapes=()):
    in_specs, out_specs, out_shape, scratch_shapes = list(in_specs), list(out_specs), list(out_shape), list(scratch_shapes)
    if carry is None:
        res = pl.pallas_call(body, name=name, grid=grid, in_specs=in_specs, out_specs=out_specs, out_shape=out_shape,
                             scratch_shapes=scratch_shapes, compiler_params=_cparams(*["parallel"] * len(grid)))(*operands)
        return list(res), []
    n_in, n_out, n_scr, c_in, c_out = len(in_specs), len(out_specs), len(scratch_shapes), len(carry.operands), len(carry.out_shape)
    steps = math.prod(grid)

    def wrapped(*refs):
        ins, refs = refs[:n_in], refs[n_in:]
        c_ins, refs = refs[:c_in], refs[c_in:]
        outs, refs = refs[:n_out], refs[n_out:]
        c_outs, refs = refs[:c_out], refs[c_out:]
        scr, (send_sems, recv_sems) = refs[:n_scr], refs[n_scr:]
        step = 0
        for d, size in enumerate(grid):
            step = step * size + pl.program_id(d)

        @pl.when(step == 0)
        def _():
            _start_all(carry.copies(c_ins, c_outs, send_sems, recv_sems)[0])

        body(*ins, *outs, *scr)

        @pl.when(step == steps - 1)
        def _():
            _wait_all(*carry.copies(c_ins, c_outs, send_sems, recv_sems))

    res = pl.pallas_call(
        wrapped, name=name, grid=grid, in_specs=in_specs + [_HBM] * c_in, out_specs=out_specs + [_HBM] * c_out,
        out_shape=out_shape + carry.out_shape, scratch_shapes=scratch_shapes + carry.sem_shapes(),
        input_output_aliases={n_in + i: n_out + j for i, j in carry.aliases.items()},
        compiler_params=_cparams(*["arbitrary"] * len(grid)))(*operands, *carry.operands)
    return list(res[:n_out]), list(res[n_out:])


def _half(which, rows):
    h = rows // 2
    return pl.ds(pl.multiple_of(which * h, 16), h)


def _like(arrays, shape_of=lambda t: t.shape):
    return [jax.ShapeDtypeStruct(shape_of(t), t.dtype) for t in arrays]


def _gather_ici(shards, layer):
    n = len(shards)

    def copies(ins, outs, send_sems, recv_sems):
        x, y, c, others = _place()
        me = 2 * x + y
        sends, arrivals = [], []
        for a in range(n):
            rows = _half(c, shards[a].shape[1])
            for k, (ox, oy) in enumerate(others):
                sems = (send_sems.at[3 * a + k], recv_sems.at[3 * a + k], (ox, oy, c))
                sends.append(_remote(ins[a].at[layer, rows], outs[a].at[me, rows], *sems))
                landed = outs[a].at[2 * ox + oy, rows]
                arrivals.append(_remote(landed, landed, *sems))
        return sends, arrivals

    return _Exchange(shards, _like(shards, lambda t: (N_CHIPS,) + t.shape[1:]), 3 * n, copies)


def _gather_d2d(gathered):
    n = len(gathered)

    def copies(ins, outs, send_sems, recv_sems):
        x, y, c, others = _place()
        sends, arrivals = [], []
        for a in range(n):
            r = gathered[a].shape[1]
            for k, (ox, oy) in enumerate(others):
                sems = (send_sems.at[3 * a + k], recv_sems.at[3 * a + k], (x, y, 1 - c))
                mine, theirs = outs[a].at[2 * ox + oy, _half(c, r)], outs[a].at[2 * ox + oy, _half(1 - c, r)]
                sends.append(_remote(mine, mine, *sems))
                arrivals.append(_remote(theirs, theirs, *sems))
        return sends, arrivals

    return _Exchange(gathered, _like(gathered), 3 * n, copies, aliases={a: a for a in range(n)})


def _swap_halves(g):
    n = len(g)

    def copies(ins, outs, send_sems, recv_sems):
        x, y, c, _ = _place()
        sends, arrivals = [], []
        for a in range(n):
            sems = (send_sems.at[a], recv_sems.at[a], (x, y, 1 - c))
            sends.append(_remote(ins[a].at[:, _half(1 - c, g[a].shape[1])], outs[a], *sems))
            arrivals.append(_remote(outs[a], outs[a], *sems))
        return sends, arrivals

    return _Exchange(g, _like(g, lambda t: (t.shape[0], t.shape[1] // 2, t.shape[2])), n, copies)


def _scatter_shards(ps):
    n = len(ps)

    def copies(ins, outs, send_sems, recv_sems):
        x, y, c, others = _place()
        me = 2 * x + y
        sends, arrivals = [], []
        for a in range(n):
            for k, (ox, oy) in enumerate(others):
                sems = (send_sems.at[3 * a + k], recv_sems.at[3 * a + k], (ox, oy, c))
                sends.append(_remote(ins[a].at[2 * ox + oy], outs[a].at[me], *sems))
                slot = outs[a].at[2 * ox + oy]
                arrivals.append(_remote(slot, slot, *sems))
        return sends, arrivals

    return _Exchange(ps, _like(ps), 3 * n, copies)


def _share_halves(mine):
    n = len(mine)

    def copies(ins, outs, send_sems, recv_sems):
        x, y, c, _ = _place()
        sends, arrivals = [], []
        for a in range(n):
            sems = (send_sems.at[a], recv_sems.at[a], (x, y, 1 - c))
            sends.append(_remote(ins[a], outs[a], *sems))
            arrivals.append(_remote(outs[a], outs[a], *sems))
        return sends, arrivals

    return _Exchange(mine, _like(mine), n, copies)


def _row_tile(r):
    for cand in (256, 352, 128):
        if r % cand == 0:
            return cand
    return r


def _pair_sum(g, other, core, name):
    ns, h, w = other.shape
    tr = _row_tile(h)
    per_half = h // tr

    def body(core_ref, g_ref, o_ref, out_ref):
        out_ref[...] = (g_ref[...] + o_ref[...]).astype(out_ref.dtype)

    blk = pl.BlockSpec((None, tr, w), lambda k, i, core_ref: (k, i, 0))
    grid_spec = pltpu.PrefetchScalarGridSpec(
        num_scalar_prefetch=1, grid=(ns, per_half),
        in_specs=[pl.BlockSpec((None, tr, w), lambda k, i, core_ref: (k, core_ref[0] * per_half + i, 0)), blk], out_specs=blk)
    return pl.pallas_call(
        body, name=name, grid_spec=grid_spec, out_shape=jax.ShapeDtypeStruct((ns, h, w), MXU_DTYPE),
        compiler_params=_cparams("parallel", "parallel"),
    )(core.reshape(1).astype(jnp.int32), g, other)


def _chip_sum(q, p, chip, name):
    ns, r, w = q.shape
    tr = _row_tile(r)

    def body(chip_ref, q_ref, own_ref, out_ref):
        me = chip_ref[0]
        own = own_ref[...].astype(F32)
        terms = [jnp.where(me == k, own, q_ref[k].astype(F32)) for k in range(ns)]
        out_ref[...] = ((terms[0] + terms[1]) + terms[2]) + terms[3]

    grid_spec = pltpu.PrefetchScalarGridSpec(
        num_scalar_prefetch=1, grid=(r // tr,),
        in_specs=[pl.BlockSpec((ns, tr, w), lambda i, chip_ref: (0, i, 0)),
                  pl.BlockSpec((None, tr, w), lambda i, chip_ref: (chip_ref[0], i, 0))],
        out_specs=pl.BlockSpec((tr, w), lambda i, chip_ref: (i, 0)))
    return pl.pallas_call(
        body, name=name, grid_spec=grid_spec, out_shape=jax.ShapeDtypeStruct((r, w), F32),
        compiler_params=_cparams("parallel"),
    )(chip.reshape(1).astype(jnp.int32), q, p)


class _WeightPrefetch:
    def __init__(self, shards, layer, chip):
        self.shards, self.layer, self.chip, self.result = shards, layer, chip, None

    def first(self):
        return _gather_ici(self.shards, self.layer)

    def got_first(self, arrived):
        self.arrived = arrived

    def second(self):
        return _gather_d2d(self.arrived)

    def got_second(self, gathered):
        self.result = [lax.dynamic_update_index_in_dim(got, own[self.layer], self.chip, 0)
                       for got, own in zip(gathered, self.shards)]

    def run(self, tag):
        self.got_first(_run_exchange(self.first(), f"gather_ici_{tag}"))
        self.got_second(_run_exchange(self.second(), f"gather_d2d_{tag}"))
        return self.result


class _GradReduce:
    def __init__(self, g, chip, core, tag):
        self.g, self.chip, self.core, self.tag, self.result = g, chip, core, tag, None

    def swap(self):
        return _swap_halves(self.g)

    def got_swap(self, theirs):
        self.pair = [_pair_sum(g, t, self.core, f"pair_sum_{name}_{self.tag}") for name, g, t in zip(_BIG, self.g, theirs)]

    def scatter(self):
        return _scatter_shards(self.pair)

    def got_scatter(self, q):
        self.mine = [_chip_sum(qa, pa, self.chip, f"chip_sum_{name}_{self.tag}") for name, qa, pa in zip(_BIG, q, self.pair)]

    def share(self):
        return _share_halves(self.mine)

    def got_share(self, theirs):
        self.result = [jnp.where(self.core == 0, jnp.concatenate([a, b]), jnp.concatenate([b, a]))
                       for a, b in zip(self.mine, theirs)]

    def run(self):
        self.got_swap(_run_exchange(self.swap(), f"swap_halves_{self.tag}"))
        self.got_scatter(_run_exchange(self.scatter(), f"scatter_shards_{self.tag}"))
        self.got_share(_run_exchange(self.share(), f"share_halves_{self.tag}"))
        return self.result


def _gather_small(pk, name):
    rows, w = pk.shape

    def body(pk_ref, all_ref, sum_ref, send_sems, recv_sems):
        x, y, c, _ = _place()
        me = 4 * x + 2 * y + c
        all_ref[me] = pk_ref[...]
        flips = [(fx, fy, fc) for fx in (0, 1) for fy in (0, 1) for fc in (0, 1)][1:]
        peers = [(x ^ fx, y ^ fy, c ^ fc) for fx, fy, fc in flips]
        sends = [_remote(pk_ref, all_ref.at[me], send_sems.at[k], recv_sems.at[k], peer) for k, peer in enumerate(peers)]
        for cp in sends:
            cp.start()
        for k, (px, py, pc) in enumerate(peers):
            slot = all_ref.at[4 * px + 2 * py + pc]
            _remote(slot, slot, send_sems.at[k], recv_sems.at[k], (px, py, pc)).wait_recv()
        for cp in sends:
            cp.wait_send()
        total = all_ref[0]
        for d in range(1, N_DEV):
            total = total + all_ref[d]
        sum_ref[...] = total

    vm = pl.BlockSpec(memory_space=pltpu.VMEM)
    return pl.pallas_call(
        body, name=name, in_specs=[vm], out_specs=[vm, vm],
        out_shape=[jax.ShapeDtypeStruct((N_DEV, rows, w), F32), jax.ShapeDtypeStruct((rows, w), F32)],
        scratch_shapes=[pltpu.SemaphoreType.DMA((7,)), pltpu.SemaphoreType.DMA((7,))],
    )(pk)


def _row_layout(c, nb, s):
    ch = jnp.swapaxes(c[:, :N_HEADS].reshape(nb, s, N_HEADS), 1, 2)
    ccol = jnp.broadcast_to(ch[..., None], (nb, N_HEADS, s, ATT))
    crow = jnp.broadcast_to(ch.reshape(nb, N_HEADS, s // ATT, 1, ATT), (nb, N_HEADS, s // ATT, 8, ATT))
    return ccol, crow


def _dil_bias(rel_bias, name):
    def body(rel_ref, t_ref, o_ref):
        for p in range(len(DIL_PATTERNS)):
            table = t_ref[p]

            def bucket(k, accs, table=table):
                return tuple(jnp.where(table == k, rel_ref[k, h], acc) for h, acc in enumerate(accs))

            accs = lax.fori_loop(0, REL_BUCKETS, bucket, tuple(jnp.full((BLK, 2 * BLK), NEG, F32) for _ in range(N_HEADS)))
            for h in range(N_HEADS):
                o_ref[p, h] = accs[h]

    vm = pl.BlockSpec(memory_space=pltpu.VMEM)
    return pl.pallas_call(
        body, name=name, in_specs=[pl.BlockSpec(memory_space=pltpu.SMEM), vm], out_specs=vm,
        out_shape=jax.ShapeDtypeStruct((len(DIL_PATTERNS), N_HEADS, BLK, 2 * BLK), F32),
        compiler_params=pltpu.CompilerParams(vmem_limit_bytes=VMEM_LIMIT),
    )(rel_bias, jnp.asarray(_bucket_table()))


def _layer_forward(x, wts, small, nb, s, tag, prefetch=None):
    proj = _matmul(x, wts["w_in"], "proj", tag)

    o_sb, carried = _sb_fwd(proj, nb, s, f"sb_fwd_{tag}", prefetch and prefetch.first())
    if prefetch:
        prefetch.got_first(carried)

    bias = _dil_bias(small["rel_bias"], f"dil_bias_{tag}")
    o_dl, lse_dl = _dil_attention_fwd(proj, bias, nb, s, f"dil_fwd_{tag}")

    fb = jnp.zeros((8, BLK), F32).at[0, :N_HEADS].set(small["f_bias"])
    csum = _fox_gates_fwd(proj, fb, nb, s, f"fox_gates_{tag}")
    ccol, crow = _row_layout(csum, nb, s)
    (o_fx, lse_fx), carried = _fox_fwd(proj, ccol, crow, nb, s, f"fox_fwd_{tag}", prefetch and prefetch.second())
    if prefetch:
        prefetch.got_second(carried)

    cw = jnp.zeros((8, CONV_W), F32).at[:3].set(small["conv_w"])
    o_cv = _conv_fwd(proj, cw, nb, s, f"conv_fwd_{tag}")

    mixed = jnp.concatenate([o_sb, o_dl, o_fx, o_cv], axis=-1).astype(MXU_DTYPE)
    mix = _matmul(mixed, wts["w_out"], "out_proj", tag)
    pre1, x1 = _ln_fwd(x, mix, small["ln1_g"], small["ln1_b"], f"ln1_fwd_{tag}")
    gu = _matmul(x1, wts["w_gu"], "ffn_in", tag)
    hid = _swiglu_fwd(gu, f"swiglu_fwd_{tag}")
    ffn = _matmul(hid, wts["w_down"], "ffn_out", tag)
    pre2, x2 = _ln_fwd(x1, ffn, small["ln2_g"], small["ln2_b"], f"ln2_fwd_{tag}")
    saved = dict(x=x, proj=proj, bias=bias, o_dl=o_dl, lse_dl=lse_dl, fb=fb, ccol=ccol, crow=crow,
                 o_fx=o_fx, lse_fx=lse_fx, cw=cw, mixed=mixed, pre1=pre1, x1=x1, gu=gu, hid=hid, pre2=pre2)
    return x2, saved


def _layer_backward(dx2, sv, wts, small, nb, s, tag, reduce=None):
    t = nb * s
    dpre2, dgb2 = _ln_bwd(dx2, sv["pre2"], small["ln2_g"], f"ln2_bwd_{tag}")
    dpre2_b = dpre2.astype(MXU_DTYPE)
    dhid = _matmul(dpre2_b, wts["w_down"], "ffn_out_dx", tag, trans_b=True)
    dw_down = _matmul(sv["hid"].T, dpre2_b, "ffn_out_dw", tag)
    (dgu,), carried = _swiglu_bwd(dhid, sv["gu"], f"swiglu_bwd_{tag}", reduce and reduce.swap())
    if reduce:
        reduce.got_swap(carried)
    dx1 = _matmul(dgu, wts["w_gu"], "ffn_in_dx", tag, add=dpre2, add_scale=ALPHA, trans_b=True)
    dw_gu = _matmul(sv["x1"].astype(MXU_DTYPE).T, dgu, "ffn_in_dw", tag)

    dpre1, dgb1 = _ln_bwd(dx1, sv["pre1"], small["ln1_g"], f"ln1_bwd_{tag}")
    dpre1_b = dpre1.astype(MXU_DTYPE)
    dmixed = _matmul(dpre1_b, wts["w_out"], "out_proj_dx", tag, trans_b=True)
    dw_out = _matmul(sv["mixed"].T, dpre1_b, "out_proj_dw", tag)
    proj = sv["proj"]

    (dq_sb, dk_sb, dv_sb), carried = _sb_bwd(proj, dmixed, nb, s, f"sb_bwd_{tag}", reduce and reduce.scatter())
    if reduce:
        reduce.got_scatter(carried)

    delta_dl = _delta_kernel(dmixed, sv["o_dl"], nb, s, f"dil_delta_{tag}")
    dq_dl, dk_dl, dv_dl, gbias = _dil_attention_bwd(proj, dmixed, sv["lse_dl"], delta_dl, sv["bias"], nb, s, f"dil_bwd_{tag}")
    drel = _bucket_reduce(gbias, jnp.asarray(_bucket_table()), f"rel_bias_grad_{tag}")

    (dq_fx, dk_fx, dv_fx, dcol), carried = _fox_bwd(proj, dmixed, sv["lse_fx"], sv["ccol"], sv["crow"], nb, s,
                                                    f"fox_bwd_{tag}", reduce and reduce.share())
    if reduce:
        reduce.got_share(carried)
    dcs = -jnp.swapaxes(dcol[:, :, :, 0, :].reshape(nb, N_HEADS, s), 1, 2).reshape(t, N_HEADS)
    dcs = jnp.pad(dcs, ((0, 0), (0, BLK - N_HEADS)))
    dfx, dfb = _fox_gates_bwd(dcs, proj, sv["fb"], nb, s, f"fox_gates_bwd_{tag}")

    dgates, dcw = _conv_bwd(dmixed, proj, sv["cw"], nb, s, f"conv_bwd_{tag}")

    dproj = jnp.concatenate([dq_sb, dk_sb, dv_sb, dq_dl, dk_dl, dv_dl, dq_fx, dk_fx, dv_fx, dgates, dfx],
                            axis=-1).astype(MXU_DTYPE)
    dx = _matmul(dproj, wts["w_in"], "proj_dx", tag, add=dpre1, add_scale=ALPHA, trans_b=True)
    dw_in = _matmul(sv["x"].astype(MXU_DTYPE).T, dproj, "proj_dw", tag)

    grads = dict(w_in=dw_in[:, :PROJ], w_out=dw_out, w_gate=dw_gu[:, :D_FF], w_up=dw_gu[:, D_FF:], w_down=dw_down,
                 ln1_g=dgb1[0], ln1_b=dgb1[1], ln2_g=dgb2[0], ln2_b=dgb2[1], conv_w=dcw[:3], f_bias=dfb[0, :N_HEADS],
                 rel_bias=drel[:N_HEADS, :REL_BUCKETS].T)
    return dx, grads


def _local_step(x, target, weights_of, small_all, prefetch=None, make_reduce=None):
    nb, s, d = x.shape
    h = x.reshape(nb * s, d)
    saved = []
    for layer in range(DEPTH):
        wts = weights_of(layer)
        ahead = prefetch[layer + 1] if prefetch and layer + 1 < DEPTH else None
        h, sv = _layer_forward(h, wts, small_all[layer], nb, s, f"l{layer}", ahead)
        saved.append((sv, wts))
    dy, lossp = _loss_kernel(h, target.reshape(nb * s, d), "loss")
    grads, reduces, pending = [None] * DEPTH, [None] * DEPTH, None
    for layer in reversed(range(DEPTH)):
        sv, wts = saved[layer]
        dy, grads[layer] = _layer_backward(dy, sv, wts, small_all[layer], nb, s, f"l{layer}", pending)
        pending = reduces[layer] = make_reduce(layer, grads[layer]) if make_reduce else None
    if pending:
        pending.run()
    return lossp, dy.reshape(nb, s, d), grads, reduces


_BIG = ("w_in", "w_out", "w_gate", "w_up", "w_down")
_COL_SHARDED = ("w_in", "w_gate", "w_up")


def _full_weights(gathered):
    cols = lambda t: jnp.swapaxes(t, 0, 1).reshape(t.shape[1], -1)
    rows = lambda t: t.reshape(-1, t.shape[2])
    w_in = jnp.pad(cols(gathered["w_in"]), ((0, 0), (0, PROJ_PAD - PROJ)))
    w_gu = jnp.concatenate([cols(gathered["w_gate"]), cols(gathered["w_up"])], axis=-1)
    return dict(w_in=w_in, w_out=rows(gathered["w_out"]), w_gu=w_gu, w_down=rows(gathered["w_down"]))


def _by_chip(name, g):
    if name in _COL_SHARDED:
        return jnp.swapaxes(g.reshape(g.shape[0], N_CHIPS, -1), 0, 1)
    return g.reshape(N_CHIPS, -1, g.shape[1])


_SMALL_LAYOUT = (("ln1_g", 0), ("ln1_b", 2), ("ln2_g", 4), ("ln2_b", 6), ("conv_w", 8))
_ROW_MISC = 10
_ROW_LOSS = 11


def _pack_small(per_layer, rel_bias, loss=None):
    pk = jnp.zeros((SMALL_ROWS, D_MODEL), F32)
    for name, row in _SMALL_LAYOUT:
        for l in range(DEPTH):
            v = per_layer[l][name].reshape(-1)
            pk = pk.at[row + l, :v.shape[0]].set(v)
    fb = jnp.concatenate([per_layer[l]["f_bias"] for l in range(DEPTH)])
    pk = pk.at[_ROW_MISC, :2 * N_HEADS].set(fb)
    pk = pk.at[_ROW_MISC, BLK:BLK + REL_BUCKETS * N_HEADS].set(rel_bias.reshape(-1))
    if loss is not None:
        pk = pk.at[_ROW_LOSS, 0].set(loss)
    return pk


def _unpack_small(pk, conv_cols):
    out = {}
    for name, row in _SMALL_LAYOUT:
        n = 3 * conv_cols if name == "conv_w" else D_MODEL
        v = pk[row:row + DEPTH, :n]
        out[name] = v.reshape(DEPTH, 3, conv_cols) if name == "conv_w" else v
    out["f_bias"] = pk[_ROW_MISC, :2 * N_HEADS].reshape(DEPTH, N_HEADS)
    out["rel_bias"] = pk[_ROW_MISC, BLK:BLK + REL_BUCKETS * N_HEADS].reshape(REL_BUCKETS, N_HEADS)
    return out


_WEIGHTS = ("w_in", "f_bias", "conv_w", "w_out", "rel_bias", "ln1_g", "ln1_b", "w_gate", "w_up", "w_down", "ln2_g", "ln2_b")


def kernel(x, w_in, f_bias, conv_w, w_out, rel_bias, ln1_g, ln1_b, w_gate, w_up, w_down, ln2_g, ln2_b, loss_target, m_w_in, m_f_bias, m_conv_w, m_w_out, m_rel_bias, m_ln1_g, m_ln1_b, m_w_gate, m_w_up, m_w_down, m_ln2_g, m_ln2_b, v_w_in, v_f_bias, v_conv_w, v_w_out, v_rel_bias, v_ln1_g, v_ln1_b, v_w_gate, v_w_up, v_w_down, v_ln2_g, v_ln2_b):
    w = dict(w_in=w_in, f_bias=f_bias, conv_w=conv_w, w_out=w_out, rel_bias=rel_bias, ln1_g=ln1_g, ln1_b=ln1_b,
             w_gate=w_gate, w_up=w_up, w_down=w_down, ln2_g=ln2_g, ln2_b=ln2_b)
    m = dict(w_in=m_w_in, f_bias=m_f_bias, conv_w=m_conv_w, w_out=m_w_out, rel_bias=m_rel_bias, ln1_g=m_ln1_g,
             ln1_b=m_ln1_b, w_gate=m_w_gate, w_up=m_w_up, w_down=m_w_down, ln2_g=m_ln2_g, ln2_b=m_ln2_b)
    v = dict(w_in=v_w_in, f_bias=v_f_bias, conv_w=v_conv_w, w_out=v_w_out, rel_bias=v_rel_bias, ln1_g=v_ln1_g,
             ln1_b=v_ln1_b, w_gate=v_w_gate, w_up=v_w_up, w_down=v_w_down, ln2_g=v_ln2_g, ln2_b=v_ln2_b)
    chip = 2 * lax.axis_index("x") + lax.axis_index("y")
    core = lax.axis_index("c")
    conv_shard = CONV_W // N_CHIPS

    shards = [w[name].astype(MXU_DTYPE) for name in _BIG]
    fetch = [_WeightPrefetch(shards, l, chip) for l in range(DEPTH)]
    fetch[0].run("l0")
    cw_pk = jnp.zeros((8, D_MODEL), F32).at[0, :DEPTH * 3 * conv_shard].set(conv_w.reshape(-1))
    cw_all, _ = _gather_small(cw_pk, "gather_conv_w")
    cw_chips = cw_all[0::2, 0, :DEPTH * 3 * conv_shard].reshape(N_CHIPS, DEPTH, 3, conv_shard)
    conv_full = jnp.moveaxis(cw_chips, 0, 2).reshape(DEPTH, 3, CONV_W)
    small_all = [dict(f_bias=f_bias[l], conv_w=conv_full[l], rel_bias=rel_bias, ln1_g=ln1_g[l], ln1_b=ln1_b[l],
                      ln2_g=ln2_g[l], ln2_b=ln2_b[l]) for l in range(DEPTH)]

    lossp, grad_x, grads, reduces = _local_step(
        x, loss_target, lambda l: _full_weights(dict(zip(_BIG, fetch[l].result))), small_all, fetch,
        lambda l, g: _GradReduce([_by_chip(name, g[name]) for name in _BIG], chip, core, f"l{l}"))
    big_g = {name: jnp.stack([reduces[l].result[a] for l in range(DEPTH)]) for a, name in enumerate(_BIG)}

    drel = grads[0]["rel_bias"] + grads[1]["rel_bias"]
    small_pk = _pack_small(grads, drel, lossp[0, 0])
    _, small_sum = _gather_small(small_pk, "gather_small_grads")
    loss = small_sum[_ROW_LOSS, 0]
    small_g = _unpack_small(small_sum, CONV_W)
    small_g["conv_w"] = lax.dynamic_slice_in_dim(small_g["conv_w"], chip * conv_shard, conv_shard, axis=2)

    out_g, out_d, out_m, out_v = dict(small_g), {}, {}, {}
    for name in _BIG:
        out_g[name] = big_g[name]
        out_d[name], out_m[name], out_v[name] = _adamw(w[name], big_g[name], m[name], v[name], f"adamw_{name}")
    per_layer = lambda src: [{name: src[name][l] for name in ("ln1_g", "ln1_b", "ln2_g", "ln2_b", "conv_w", "f_bias")}
                             for l in range(DEPTH)]
    packs = [_pack_small(per_layer(src), src["rel_bias"])[None] for src in (w, small_g, m, v)]
    for dst, pk in zip((out_d, out_m, out_v), _adamw(*packs, "adamw_small")):
        dst.update(_unpack_small(pk[0], conv_shard))

    return (loss, grad_x, *[out_g[n] for n in _WEIGHTS], *[out_d[n] for n in _WEIGHTS],
            *[out_m[n] for n in _WEIGHTS], *[out_v[n] for n in _WEIGHTS])
```

```python
import functools
import math

import numpy as np
import jax
import jax.numpy as jnp
from jax import lax
from jax.experimental import pallas as pl
from jax.experimental.pallas import tpu as pltpu

F32 = jnp.float32
BF16 = jnp.bfloat16
MXU_DTYPE = BF16

D_MODEL = 1024
HEAD_DIM = 64
N_HEADS = 4
BLK = 128
ATT = 256
CONV_W = 256
PROJ = 3076
PROJ_PAD = 3200
D_FF = 2816
DEPTH = 2
ALPHA = (2 * DEPTH) ** 0.25
LN_EPS = 1e-5
NEG = -1e30
DIL_PATTERNS = ((128, 1), (512, 4), (2048, 16))
REL_BUCKETS = 32
N_CHIPS = 4
N_DEV = 8
SMALL_ROWS = 16

ADAM_LR = 0.001
ADAM_B1 = 0.9
ADAM_B2 = 0.999
ADAM_EPS = 1e-08
ADAM_WD = 0.01
ADAM_STEP = 10

VMEM_LIMIT = 48 * 2 ** 20
MESH = pl.DeviceIdType.MESH


def _cparams(*sem):
    return pltpu.CompilerParams(dimension_semantics=tuple(sem), vmem_limit_bytes=VMEM_LIMIT)


def _dot(a, b):
    return jnp.dot(a.astype(MXU_DTYPE), b.astype(MXU_DTYPE), preferred_element_type=F32)


def _dot_nt(a, b):
    return lax.dot_general(a.astype(MXU_DTYPE), b.astype(MXU_DTYPE), (((1,), (1,)), ((), ())),
                           preferred_element_type=F32)


def _dot_tn(a, b):
    return lax.dot_general(a.astype(MXU_DTYPE), b.astype(MXU_DTYPE), (((0,), (0,)), ((), ())),
                           preferred_element_type=F32)


def _split_dot(x, ones, passes):
    acc, rest = None, x
    for p in range(passes):
        piece = rest.astype(MXU_DTYPE)
        part = jnp.dot(piece, ones, preferred_element_type=F32)
        acc = part if acc is None else acc + part
        if p + 1 < passes:
            rest = rest - piece.astype(F32)
    return acc


def _split_dot_lhs(ones, x, passes):
    acc, rest = None, x
    for p in range(passes):
        piece = rest.astype(MXU_DTYPE)
        part = jnp.dot(ones, piece, preferred_element_type=F32)
        acc = part if acc is None else acc + part
        if p + 1 < passes:
            rest = rest - piece.astype(F32)
    return acc


def _iota2(shape, axis):
    return lax.broadcasted_iota(jnp.int32, shape, axis)


_TILES = {"proj": (1024, 640, 1024), "out_proj": (1024, 1024, 1024), "ffn_in": (1024, 1408, 1024),
          "ffn_out": (1024, 1024, 2816), "ffn_out_dx": (1024, 1408, 1024), "ffn_out_dw": (1408, 1024, 2048),
          "ffn_in_dx": (1024, 1024, 1408), "ffn_in_dw": (1024, 1408, 2048), "out_proj_dx": (1024, 1024, 1024),
          "out_proj_dw": (1024, 1024, 2048), "proj_dx": (1024, 512, 3200), "proj_dw": (1024, 640, 2048)}


def _matmul(a, b, kind, tag, *, out_dtype=F32, add=None, add_scale=1.0, trans_a=False, trans_b=False):
    k, m = a.shape if trans_a else a.shape[::-1]
    n = b.shape[0] if trans_b else b.shape[1]
    tm, tn, tk = _TILES[kind]
    tm, tk, name = min(tm, m), min(tk, k), f"{kind}_{tag}"
    assert m % tm == 0 and n % tn == 0 and k % tk == 0, (a.shape, b.shape, tm, tn, tk)
    nk = k // tk

    def body(*refs):
        if add is None:
            a_ref, b_ref, o_ref = refs[:3]
            c_ref, scr = None, refs[3:]
        else:
            a_ref, b_ref, c_ref, o_ref = refs[:4]
            scr = refs[4:]
        dot = _dot_tn if trans_a else _dot_nt if trans_b else _dot
        part = dot(a_ref[...], b_ref[...])

        def finish(acc):
            if c_ref is not None:
                acc = acc + add_scale * c_ref[...]
            o_ref[...] = acc.astype(out_dtype)

        if nk == 1:
            finish(part)
        else:
            acc_ref = scr[0]
            kk = pl.program_id(2)

            @pl.when(kk == 0)
            def _():
                acc_ref[...] = part

            @pl.when(kk > 0)
            def _():
                acc_ref[...] += part

            @pl.when(kk == nk - 1)
            def _():
                finish(acc_ref[...])

    b_spec = pl.BlockSpec((tn, tk), lambda i, j, kk: (j, kk)) if trans_b else pl.BlockSpec((tk, tn), lambda i, j, kk: (kk, j))
    a_spec = pl.BlockSpec((tk, tm), lambda i, j, kk: (kk, i)) if trans_a else pl.BlockSpec((tm, tk), lambda i, j, kk: (i, kk))
    in_specs = [a_spec, b_spec]
    operands = [a, b]
    if add is not None:
        in_specs.append(pl.BlockSpec((tm, tn), lambda i, j, kk: (i, j)))
        operands.append(add)
    return pl.pallas_call(
        body, name=name, grid=(m // tm, n // tn, nk), in_specs=in_specs,
        out_specs=pl.BlockSpec((tm, tn), lambda i, j, kk: (i, j)),
        out_shape=jax.ShapeDtypeStruct((m, n), out_dtype),
        scratch_shapes=[pltpu.VMEM((tm, tn), F32)] if nk > 1 else [],
        compiler_params=_cparams("parallel", "parallel", "arbitrary"),
    )(*operands)


def _ln_stats(pre):
    mu = jnp.mean(pre, axis=-1, keepdims=True)
    xc = pre - mu
    var = jnp.mean(xc * xc, axis=-1, keepdims=True)
    rstd = lax.rsqrt(var + LN_EPS)
    return xc * rstd, rstd


def _ln_fwd(xin, branch, g, b, name):
    t, d = xin.shape
    tile = 256

    def body(x_ref, br_ref, g_ref, b_ref, pre_ref, y_ref):
        pre = ALPHA * x_ref[...] + br_ref[...]
        xhat, _ = _ln_stats(pre)
        pre_ref[...] = pre
        y_ref[...] = xhat * g_ref[...] + b_ref[...]

    row = pl.BlockSpec((tile, d), lambda i: (i, 0))
    vec = pl.BlockSpec((1, d), lambda i: (0, 0))
    return pl.pallas_call(
        body, name=name, grid=(t // tile,), in_specs=[row, row, vec, vec], out_specs=[row, row],
        out_shape=[jax.ShapeDtypeStruct((t, d), F32)] * 2, compiler_params=_cparams("parallel"),
    )(xin, branch, g.reshape(1, d), b.reshape(1, d))


def _ln_bwd(dy, pre, g, name):
    t, d = dy.shape
    tile = 256

    def body(dy_ref, pre_ref, g_ref, dpre_ref, dgb_ref):
        dyv = dy_ref[...]
        xhat, rstd = _ln_stats(pre_ref[...])
        dxh = dyv * g_ref[...]
        m1 = jnp.mean(dxh, axis=-1, keepdims=True)
        m2 = jnp.mean(dxh * xhat, axis=-1, keepdims=True)
        dpre_ref[...] = rstd * (dxh - m1 - xhat * m2)

        @pl.when(pl.program_id(0) == 0)
        def _():
            dgb_ref[...] = jnp.zeros_like(dgb_ref)

        dgb_ref[0:1, :] += jnp.sum(dyv * xhat, axis=0, keepdims=True)
        dgb_ref[1:2, :] += jnp.sum(dyv, axis=0, keepdims=True)

    row = pl.BlockSpec((tile, d), lambda i: (i, 0))
    return pl.pallas_call(
        body, name=name, grid=(t // tile,), in_specs=[row, row, pl.BlockSpec((1, d), lambda i: (0, 0))],
        out_specs=[row, pl.BlockSpec((8, d), lambda i: (0, 0))],
        out_shape=[jax.ShapeDtypeStruct((t, d), F32), jax.ShapeDtypeStruct((8, d), F32)],
        compiler_params=_cparams("arbitrary"),
    )(dy, pre, g.reshape(1, d))


def _swiglu_fwd(gu, name):
    t = gu.shape[0]
    tile = 256

    def body(gu_ref, h_ref):
        gate = gu_ref[:, :D_FF]
        up = gu_ref[:, D_FF:]
        h_ref[...] = (gate * (1.0 / (1.0 + jnp.exp(-gate))) * up).astype(h_ref.dtype)

    return pl.pallas_call(
        body, name=name, grid=(t // tile,), in_specs=[pl.BlockSpec((tile, 2 * D_FF), lambda i: (i, 0))],
        out_specs=pl.BlockSpec((tile, D_FF), lambda i: (i, 0)),
        out_shape=jax.ShapeDtypeStruct((t, D_FF), MXU_DTYPE), compiler_params=_cparams("parallel"),
    )(gu)


def _swiglu_bwd(dh, gu, name, carry=None):
    t = gu.shape[0]
    tile = 256

    def body(dh_ref, gu_ref, dgu_ref):
        gate = gu_ref[:, :D_FF]
        up = gu_ref[:, D_FF:]
        dhv = dh_ref[...]
        sig = 1.0 / (1.0 + jnp.exp(-gate))
        dgu_ref[:, :D_FF] = (dhv * up * sig * (1.0 + gate * (1.0 - sig))).astype(dgu_ref.dtype)
        dgu_ref[:, D_FF:] = (dhv * gate * sig).astype(dgu_ref.dtype)

    return _host_call(
        body, carry, name=name, grid=(t // tile,),
        in_specs=[pl.BlockSpec((tile, D_FF), lambda i: (i, 0)), pl.BlockSpec((tile, 2 * D_FF), lambda i: (i, 0))],
        out_specs=[pl.BlockSpec((tile, 2 * D_FF), lambda i: (i, 0))],
        out_shape=[jax.ShapeDtypeStruct((t, 2 * D_FF), MXU_DTYPE)], operands=(dh, gu))


def _loss_kernel(y, target, name):
    t, d = y.shape
    tile = 512

    def body(y_ref, t_ref, dy_ref, l_ref):
        err = y_ref[...] - t_ref[...]
        dy_ref[...] = err * (1.0 / d)

        @pl.when(pl.program_id(0) == 0)
        def _():
            l_ref[...] = jnp.zeros_like(l_ref)

        l_ref[...] += jnp.sum(err * err) * (0.5 / d)

    row = pl.BlockSpec((tile, d), lambda i: (i, 0))
    return pl.pallas_call(
        body, name=name, grid=(t // tile,), in_specs=[row, row],
        out_specs=[row, pl.BlockSpec((8, 128), lambda i: (0, 0))],
        out_shape=[jax.ShapeDtypeStruct((t, d), F32), jax.ShapeDtypeStruct((8, 128), F32)],
        compiler_params=_cparams("arbitrary"),
    )(y, target)


def _adamw(w, g, m, v, name):
    nl, r, c = w.shape
    tr = r
    for cand in (256, 352, 128, 64, 16, 8):
        if r % cand == 0:
            tr = cand
            break

    def body(w_ref, g_ref, m_ref, v_ref, d_ref, nm_ref, nv_ref):
        gv = g_ref[...]
        nm = ADAM_B1 * m_ref[...] + (1.0 - ADAM_B1) * gv
        nv = ADAM_B2 * v_ref[...] + (1.0 - ADAM_B2) * (gv * gv)
        m_hat = nm / (1.0 - ADAM_B1 ** ADAM_STEP)
        v_hat = nv / (1.0 - ADAM_B2 ** ADAM_STEP)
        d_ref[...] = -ADAM_LR * (m_hat / (jnp.sqrt(v_hat) + ADAM_EPS) + ADAM_WD * w_ref[...])
        nm_ref[...] = nm
        nv_ref[...] = nv

    blk = pl.BlockSpec((1, tr, c), lambda l, i: (l, i, 0))
    return pl.pallas_call(
        body, name=name, grid=(nl, r // tr), in_specs=[blk] * 4, out_specs=[blk] * 3,
        out_shape=[jax.ShapeDtypeStruct(w.shape, F32)] * 3, compiler_params=_cparams("parallel", "parallel"),
    )(w, g, m, v)


def _shift_down(u, k, rows):
    return jnp.where(rows >= k, pltpu.roll(u, k, 0), 0.0)


def _shift_up(u, k, rows, s):
    return jnp.where(rows < s - k, pltpu.roll(u, s - k, 0), 0.0)


def _conv_fwd(proj, conv_w, nb, s, name):
    def body(b_ref, c_ref, h_ref, w_ref, o_ref):
        rows = _iota2((s, CONV_W), 0)
        u = c_ref[...] * h_ref[...]
        y = w_ref[2:3, :] * u + w_ref[1:2, :] * _shift_down(u, 1, rows) + w_ref[0:1, :] * _shift_down(u, 2, rows)
        o_ref[...] = b_ref[...] * y

    col = lambda j: pl.BlockSpec((s, CONV_W), lambda b: (b, j))
    return pl.pallas_call(
        body, name=name, grid=(nb,),
        in_specs=[col(9), col(10), col(11), pl.BlockSpec((8, CONV_W), lambda b: (0, 0))],
        out_specs=pl.BlockSpec((s, CONV_W), lambda b: (b, 0)),
        out_shape=jax.ShapeDtypeStruct((nb * s, CONV_W), F32), compiler_params=_cparams("parallel"),
    )(proj, proj, proj, conv_w)


def _conv_bwd(dmixed, proj, conv_w, nb, s, name):
    def body(do_ref, b_ref, c_ref, h_ref, w_ref, dg_ref, dw_ref):
        rows = _iota2((s, CONV_W), 0)
        cg, hg, bg, dout = c_ref[...], h_ref[...], b_ref[...], do_ref[...]
        u = cg * hg
        u1 = _shift_down(u, 1, rows)
        u2 = _shift_down(u, 2, rows)
        y = w_ref[2:3, :] * u + w_ref[1:2, :] * u1 + w_ref[0:1, :] * u2
        dy = dout * bg
        du = w_ref[2:3, :] * dy + w_ref[1:2, :] * _shift_up(dy, 1, rows, s) + w_ref[0:1, :] * _shift_up(dy, 2, rows, s)
        dg_ref[:, 0:CONV_W] = dout * y
        dg_ref[:, CONV_W:2 * CONV_W] = du * hg
        dg_ref[:, 2 * CONV_W:3 * CONV_W] = du * cg

        @pl.when(pl.program_id(0) == 0)
        def _():
            dw_ref[...] = jnp.zeros_like(dw_ref)

        dw_ref[0:1, :] += jnp.sum(dy * u2, axis=0, keepdims=True)
        dw_ref[1:2, :] += jnp.sum(dy * u1, axis=0, keepdims=True)
        dw_ref[2:3, :] += jnp.sum(dy * u, axis=0, keepdims=True)

    col = lambda j: pl.BlockSpec((s, CONV_W), lambda b: (b, j))
    return pl.pallas_call(
        body, name=name, grid=(nb,),
        in_specs=[col(3), col(9), col(10), col(11), pl.BlockSpec((8, CONV_W), lambda b: (0, 0))],
        out_specs=[pl.BlockSpec((s, 3 * CONV_W), lambda b: (b, 0)), pl.BlockSpec((8, CONV_W), lambda b: (0, 0))],
        out_shape=[jax.ShapeDtypeStruct((nb * s, 3 * CONV_W), F32), jax.ShapeDtypeStruct((8, CONV_W), F32)],
        compiler_params=_cparams("arbitrary"),
    )(dmixed, proj, proj, proj, conv_w)


def _col_spec(s, base):
    return pl.BlockSpec((s, BLK), lambda b, p: (b, base + p))


def _rows(i):
    return pl.ds(pl.multiple_of(i * ATT, ATT), ATT)


def _rows128(i):
    return pl.ds(pl.multiple_of(i * BLK, BLK), BLK)


def _log_sigmoid_parts(z):
    e = jnp.exp(-jnp.abs(z))
    l1p = jnp.log(1.0 + e)
    lb = jnp.minimum(z, 0.0) - l1p
    return lb, lb - z, e


def _head_masks():
    lane = _iota2((1, BLK), 1)
    return [(lane >= h * HEAD_DIM) & (lane < (h + 1) * HEAD_DIM) for h in range(2)]


def _split_heads(ref, scr, sels):
    for h, sel in enumerate(sels):
        scr[h] = jnp.where(sel, ref[...], 0.0).astype(MXU_DTYPE)


def _sb_fwd(proj, nb, s, name, carry=None):
    nblk = s // ATT

    def body(q_ref, k_ref, v_ref, o_ref, km, vm):
        sels = _head_masks()
        _split_heads(k_ref, km, sels)
        _split_heads(v_ref, vm, sels)
        rows = _iota2((ATT, ATT), 0)
        cols = _iota2((ATT, ATT), 1)
        later = (rows > cols).astype(MXU_DTYPE)

        def qblock(i, _):
            qi = (q_ref[_rows(i), :] * 0.125).astype(MXU_DTYPE)

            def kblock(t, state):
                carries, acc = state
                j = i - t
                strict = (cols + (j - i) * ATT) < rows
                out = []
                for h in range(2):
                    z = _dot_nt(qi, km[h, _rows(j), :])
                    lb, lr, _ = _log_sigmoid_parts(z)
                    lr = jnp.where(strict, lr, 0.0)
                    tail = _split_dot(lr, later, 2) + carries[h]
                    a = jnp.where(strict, jnp.exp(lb + tail), 0.0)
                    acc = acc + _dot(a, vm[h, _rows(j), :])
                    out.append(carries[h] + jnp.sum(lr, axis=-1, keepdims=True))
                return tuple(out), acc

            init = ((jnp.zeros((ATT, 1), F32),) * 2, jnp.zeros((ATT, BLK), F32))
            _, acc = lax.fori_loop(0, i + 1, kblock, init)
            o_ref[_rows(i), :] = acc
            return 0

        lax.fori_loop(0, nblk, qblock, 0)

    (o,), extra = _host_call(
        body, carry, name=name, grid=(nb, 2), in_specs=[_col_spec(s, 0), _col_spec(s, 2), _col_spec(s, 4)],
        out_specs=[_col_spec(s, 0)], out_shape=[jax.ShapeDtypeStruct((nb * s, 2 * BLK), F32)],
        scratch_shapes=[pltpu.VMEM((2, s, BLK), MXU_DTYPE)] * 2, operands=(proj, proj, proj))
    return o, extra


def _sb_bwd(proj, dmixed, nb, s, name, carry=None):
    nblk = s // ATT

    def body(q_ref, k_ref, v_ref, do_ref, dq_ref, dk_ref, dv_ref, km, vm, a_scr, dl_scr, beta_scr):
        sels = _head_masks()
        _split_heads(k_ref, km, sels)
        _split_heads(v_ref, vm, sels)
        rows = _iota2((ATT, ATT), 0)
        cols = _iota2((ATT, ATT), 1)
        later = (rows > cols).astype(MXU_DTYPE)
        earlier = (rows < cols).astype(MXU_DTYPE)
        dk_ref[...] = jnp.zeros_like(dk_ref)
        dv_ref[...] = jnp.zeros_like(dv_ref)

        def qblock(i, _):
            qi = (q_ref[_rows(i), :] * 0.125).astype(MXU_DTYPE)
            doi = do_ref[_rows(i), :].astype(MXU_DTYPE)
            qm = [jnp.where(sel, qi, 0.0) for sel in sels]
            dom = [jnp.where(sel, doi, 0.0) for sel in sels]

            def first(t, carries):
                j = i - t
                strict = (cols + (j - i) * ATT) < rows
                out = []
                for h in range(2):
                    z = _dot_nt(qi, km[h, _rows(j), :])
                    lb, lr, e = _log_sigmoid_parts(z)
                    lr = jnp.where(strict, lr, 0.0)
                    tail = _split_dot(lr, later, 2) + carries[h]
                    a = jnp.where(strict, jnp.exp(lb + tail), 0.0)
                    a_scr[h, j] = a
                    dl_scr[h, j] = a * _dot_nt(doi, vm[h, _rows(j), :])
                    beta_scr[h, j] = jnp.exp(lb)
                    out.append(carries[h] + jnp.sum(lr, axis=-1, keepdims=True))
                return tuple(out)

            lax.fori_loop(0, i + 1, first, (jnp.zeros((ATT, 1), F32),) * 2)

            def second(j, state):
                csums, dq = state
                strict = (cols + (j - i) * ATT) < rows
                out = []
                for h in range(2):
                    dl = dl_scr[h, j]
                    beta = beta_scr[h, j]
                    before = _split_dot(dl, earlier, 2) + csums[h]
                    dz = jnp.where(strict, dl * (1.0 - beta) - beta * before, 0.0).astype(MXU_DTYPE)
                    dq = dq + _dot(dz, km[h, _rows(j), :])
                    dk_ref[_rows(j), :] += _dot_tn(dz, qm[h])
                    dv_ref[_rows(j), :] += _dot_tn(a_scr[h, j], dom[h])
                    out.append(csums[h] + jnp.sum(dl, axis=-1, keepdims=True))
                return tuple(out), dq

            init = ((jnp.zeros((ATT, 1), F32),) * 2, jnp.zeros((ATT, BLK), F32))
            _, dq = lax.fori_loop(0, i + 1, second, init)
            dq_ref[_rows(i), :] = dq * 0.125
            return 0

        lax.fori_loop(0, nblk, qblock, 0)

    out = _col_spec(s, 0)
    return _host_call(
        body, carry, name=name, grid=(nb, 2),
        in_specs=[_col_spec(s, 0), _col_spec(s, 2), _col_spec(s, 4), out], out_specs=[out] * 3,
        out_shape=[jax.ShapeDtypeStruct((nb * s, 2 * BLK), F32)] * 3,
        scratch_shapes=[pltpu.VMEM((2, s, BLK), MXU_DTYPE)] * 2 + [pltpu.VMEM((2, nblk, ATT, ATT), F32)] * 3,
        operands=(proj, proj, proj, dmixed))


def _pair_spec(s, width):
    return pl.BlockSpec((None, 2, s, width), lambda b, p: (b, p, 0, 0))


def _fox_fwd(proj, ccol, crow, nb, s, name, carry=None):
    nblk = s // ATT

    def body(q_ref, k_ref, v_ref, cc_ref, cr_ref, o_ref, lse_ref, km, vm):
        sels = _head_masks()
        _split_heads(k_ref, km, sels)
        _split_heads(v_ref, vm, sels)
        rows = _iota2((ATT, ATT), 0)
        cols = _iota2((ATT, ATT), 1)

        def qblock(i, _):
            qi = (q_ref[_rows(i), :] * 0.125).astype(MXU_DTYPE)
            ci = [cc_ref[h, _rows(i), :] for h in range(2)]

            def kblock(j, state):
                ms, ls, acc = state
                causal = (cols + (j - i) * ATT) <= rows
                new_m, new_l, scales, parts = [], [], [], []
                for h in range(2):
                    z = _dot_nt(qi, km[h, _rows(j), :]) + (ci[h] - cr_ref[h, j][0:1, :])
                    z = jnp.where(causal, z, NEG)
                    m_new = jnp.maximum(ms[h], jnp.max(z, axis=-1, keepdims=True))
                    p = jnp.exp(z - m_new)
                    scale = jnp.exp(ms[h] - m_new)
                    new_m.append(m_new)
                    new_l.append(scale * ls[h] + jnp.sum(p, axis=-1, keepdims=True))
                    scales.append(scale)
                    parts.append(_dot(p, vm[h, _rows(j), :]))
                acc = jnp.where(sels[0], scales[0], scales[1]) * acc + parts[0] + parts[1]
                return tuple(new_m), tuple(new_l), acc

            init = ((jnp.full((ATT, 1), NEG, F32),) * 2, (jnp.zeros((ATT, 1), F32),) * 2, jnp.zeros((ATT, BLK), F32))
            ms, ls, acc = lax.fori_loop(0, i + 1, kblock, init)
            o_ref[_rows(i), :] = acc / jnp.where(sels[0], ls[0], ls[1])
            for h in range(2):
                lse_ref[h, _rows(i), :] = jnp.broadcast_to(ms[h] + jnp.log(ls[h]), (ATT, ATT))
            return 0

        lax.fori_loop(0, nblk, qblock, 0)

    crow_spec = pl.BlockSpec((None, 2, nblk, 8, ATT), lambda b, p: (b, p, 0, 0, 0))
    return _host_call(
        body, carry, name=name, grid=(nb, 2),
        in_specs=[_col_spec(s, 12), _col_spec(s, 14), _col_spec(s, 16), _pair_spec(s, ATT), crow_spec],
        out_specs=[_col_spec(s, 0), _pair_spec(s, ATT)],
        out_shape=[jax.ShapeDtypeStruct((nb * s, 2 * BLK), F32), jax.ShapeDtypeStruct((nb, N_HEADS, s, ATT), F32)],
        scratch_shapes=[pltpu.VMEM((2, s, BLK), MXU_DTYPE)] * 2, operands=(proj, proj, proj, ccol, crow))


def _fox_bwd(proj, dmixed, lse, ccol, crow, nb, s, name, carry=None):
    nblk = s // ATT

    def body(q_ref, k_ref, v_ref, do_ref, lse_ref, cc_ref, cr_ref, dq_ref, dk_ref, dv_ref, dc_ref, km, vm, p_scr, dp_scr):
        sels = _head_masks()
        _split_heads(k_ref, km, sels)
        _split_heads(v_ref, vm, sels)
        rows = _iota2((ATT, ATT), 0)
        cols = _iota2((ATT, ATT), 1)
        dk_ref[...] = jnp.zeros_like(dk_ref)
        dv_ref[...] = jnp.zeros_like(dv_ref)
        dc_ref[...] = jnp.zeros_like(dc_ref)

        def qblock(i, _):
            qi = (q_ref[_rows(i), :] * 0.125).astype(MXU_DTYPE)
            doi = do_ref[_rows(i), :].astype(MXU_DTYPE)
            qm = [jnp.where(sel, qi, 0.0) for sel in sels]
            dom = [jnp.where(sel, doi, 0.0) for sel in sels]
            ci = [cc_ref[h, _rows(i), :] for h in range(2)]
            lsei = [lse_ref[h, _rows(i), :] for h in range(2)]

            def probs(j, h):
                z = _dot_nt(qi, km[h, _rows(j), :]) + (ci[h] - cr_ref[h, j][0:1, :])
                p = jnp.where((cols + (j - i) * ATT) <= rows, jnp.exp(z - lsei[h]), 0.0)
                return p, _dot_nt(doi, vm[h, _rows(j), :])

            def row_term(j, accs):
                out = []
                for h in range(2):
                    p, dp = probs(j, h)
                    p_scr[h, j] = p
                    dp_scr[h, j] = dp
                    out.append(accs[h] + jnp.sum(p * dp, axis=-1, keepdims=True))
                return tuple(out)

            di = lax.fori_loop(0, i + 1, row_term, (jnp.zeros((ATT, 1), F32),) * 2)

            def kblock(j, dq):
                for h in range(2):
                    p = p_scr[h, j]
                    ds = p * (dp_scr[h, j] - di[h])
                    dc_ref[h, j] += jnp.broadcast_to(jnp.sum(ds, axis=0, keepdims=True), (8, ATT))
                    ds = ds.astype(MXU_DTYPE)
                    dk_ref[_rows(j), :] += _dot_tn(ds, qm[h])
                    dv_ref[_rows(j), :] += _dot_tn(p, dom[h])
                    dq = dq + _dot(ds, km[h, _rows(j), :])
                return dq

            dq = lax.fori_loop(0, i + 1, kblock, jnp.zeros((ATT, BLK), F32))
            dq_ref[_rows(i), :] = dq * 0.125
            return 0

        lax.fori_loop(0, nblk, qblock, 0)

    crow_spec = pl.BlockSpec((None, 2, nblk, 8, ATT), lambda b, p: (b, p, 0, 0, 0))
    wide, cols_out = _pair_spec(s, ATT), _col_spec(s, 0)
    return _host_call(
        body, carry, name=name, grid=(nb, 2),
        in_specs=[_col_spec(s, 12), _col_spec(s, 14), _col_spec(s, 16), _col_spec(s, 4), wide, wide, crow_spec],
        out_specs=[cols_out, cols_out, cols_out, crow_spec],
        out_shape=[jax.ShapeDtypeStruct((nb * s, 2 * BLK), F32)] * 3 + [jax.ShapeDtypeStruct((nb, N_HEADS, nblk, 8, ATT), F32)],
        scratch_shapes=[pltpu.VMEM((2, s, BLK), MXU_DTYPE)] * 2 + [pltpu.VMEM((2, nblk, ATT, ATT), F32)] * 2,
        operands=(proj, proj, proj, dmixed, lse, ccol, crow))


def _fox_gates_fwd(proj, f_bias, nb, s, name):
    chunk = 256

    def body(f_ref, b_ref, c_ref):
        lower = (_iota2((chunk, chunk), 0) >= _iota2((chunk, chunk), 1)).astype(MXU_DTYPE)
        carry = jnp.zeros((1, BLK), F32)
        for n in range(s // chunk):
            rows = pl.ds(n * chunk, chunk)
            lf, _, _ = _log_sigmoid_parts(f_ref[rows, :] + b_ref[0:1, :])
            c = _split_dot_lhs(lower, lf, 3) + carry
            c_ref[rows, :] = c
            carry = c[chunk - 1:chunk, :]

    return pl.pallas_call(
        body, name=name, grid=(nb,),
        in_specs=[pl.BlockSpec((s, BLK), lambda b: (b, (PROJ_PAD - BLK) // BLK)), pl.BlockSpec((8, BLK), lambda b: (0, 0))],
        out_specs=pl.BlockSpec((s, BLK), lambda b: (b, 0)),
        out_shape=jax.ShapeDtypeStruct((nb * s, BLK), F32), compiler_params=_cparams("parallel"),
    )(proj, f_bias)


def _fox_gates_bwd(dc, proj, f_bias, nb, s, name):
    chunk = 256

    def body(dc_ref, f_ref, b_ref, df_ref, db_ref):
        upper = (_iota2((chunk, chunk), 0) <= _iota2((chunk, chunk), 1)).astype(MXU_DTYPE)
        carry = jnp.zeros((1, BLK), F32)
        total = jnp.zeros((1, BLK), F32)
        for n in reversed(range(s // chunk)):
            rows = pl.ds(n * chunk, chunk)
            dlf = _split_dot_lhs(upper, dc_ref[rows, :], 3) + carry
            carry = dlf[0:1, :]
            pre = f_ref[rows, :] + b_ref[0:1, :]
            e = jnp.exp(-jnp.abs(pre))
            df = dlf * (jnp.where(pre >= 0.0, e, 1.0) / (1.0 + e))
            df_ref[rows, :] = df
            total = total + jnp.sum(df, axis=0, keepdims=True)

        @pl.when(pl.program_id(0) == 0)
        def _():
            db_ref[...] = jnp.zeros_like(db_ref)

        db_ref[0:1, :] += total

    return pl.pallas_call(
        body, name=name, grid=(nb,),
        in_specs=[pl.BlockSpec((s, BLK), lambda b: (b, 0)), pl.BlockSpec((s, BLK), lambda b: (b, (PROJ_PAD - BLK) // BLK)),
                  pl.BlockSpec((8, BLK), lambda b: (0, 0))],
        out_specs=[pl.BlockSpec((s, BLK), lambda b: (b, 0)), pl.BlockSpec((8, BLK), lambda b: (0, 0))],
        out_shape=[jax.ShapeDtypeStruct((nb * s, BLK), F32), jax.ShapeDtypeStruct((8, BLK), F32)],
        compiler_params=_cparams("arbitrary"),
    )(dc, proj, f_bias)


def _delta_kernel(dmixed, o, nb, s, name):
    def body(do_ref, o_ref, d_ref):
        prod = do_ref[...] * o_ref[...]
        for h, sel in enumerate(_head_masks()):
            d_ref[h] = jnp.broadcast_to(jnp.sum(jnp.where(sel, prod, 0.0), axis=-1, keepdims=True), (s, BLK))

    return pl.pallas_call(
        body, name=name, grid=(nb, 2), in_specs=[_col_spec(s, 2), _col_spec(s, 0)], out_specs=_pair_spec(s, BLK),
        out_shape=jax.ShapeDtypeStruct((nb, N_HEADS, s, BLK), F32), compiler_params=_cparams("parallel", "parallel"),
    )(dmixed, o)


def _t5_bucket_np(dist):
    max_exact = REL_BUCKETS // 2
    nf = np.maximum(dist, 1).astype(np.float32)
    large = max_exact + (np.log(nf / max_exact) / math.log(2048 / max_exact) * (REL_BUCKETS - max_exact)).astype(np.int32)
    large = np.minimum(large, REL_BUCKETS - 1)
    return np.where(dist < max_exact, dist, large)


def _bucket_table():
    qi = np.arange(BLK)[:, None]
    kj = np.arange(2 * BLK)[None, :]
    dist = qi + BLK - kj
    tables = []
    for window, dil in DIL_PATTERNS:
        in_band = (dist >= 0) & (dist <= window // dil)
        tables.append(np.where(in_band, _t5_bucket_np(np.maximum(dist, 0) * dil), -1).astype(np.int32))
    return np.stack(tables)


def _dil_scores(qb, kp, kc, b_ref, h, prev_valid):
    zp = _dot_nt(qb, kp) + b_ref[h, :, 0:BLK]
    zp = jnp.where(prev_valid, zp, NEG)
    zc = _dot_nt(qb, kc) + b_ref[h, :, BLK:2 * BLK]
    return zp, zc


def _residue_rows(b, seg, dil):
    if dil == 1:
        return _rows128(b), _rows128(jnp.maximum(b - 1, 0)), b > 0
    r, n = b // seg, b % seg
    cur = pl.ds(r + dil * n * BLK, BLK, stride=dil)
    prev = pl.ds(r + dil * jnp.maximum(n - 1, 0) * BLK, BLK, stride=dil)
    return cur, prev, n > 0


def _dil_attention_fwd(proj, bias, nb, s, name):
    nblk = s // BLK

    def body(q_ref, k_ref, v_ref, b_ref, out_ref, lse_ref, o_scr, l_scr):
        sels = _head_masks()
        for p, (_, dil) in enumerate(DIL_PATTERNS):
            seg = s // dil // BLK

            def block(b, _, p=p, seg=seg, dil=dil):
                cur, prev, has_prev = _residue_rows(b, seg, dil)
                qb = (q_ref[cur, :] * 0.125).astype(MXU_DTYPE)
                kp, kc = k_ref[prev, :].astype(MXU_DTYPE), k_ref[cur, :].astype(MXU_DTYPE)
                vp, vc = v_ref[prev, :].astype(MXU_DTYPE), v_ref[cur, :].astype(MXU_DTYPE)
                acc = jnp.zeros((BLK, BLK), F32)
                for h, sel in enumerate(sels):
                    zp, zc = _dil_scores(qb, jnp.where(sel, kp, 0.0), jnp.where(sel, kc, 0.0), b_ref.at[p], h, has_prev)
                    m = jnp.maximum(jnp.max(zp, axis=-1, keepdims=True), jnp.max(zc, axis=-1, keepdims=True))
                    pp = jnp.exp(zp - m)
                    pc = jnp.exp(zc - m)
                    den = jnp.sum(pp, axis=-1, keepdims=True) + jnp.sum(pc, axis=-1, keepdims=True)
                    acc = acc + (_dot(pp, jnp.where(sel, vp, 0.0)) + _dot(pc, jnp.where(sel, vc, 0.0))) / den
                    l_scr[p, h, cur, :] = jnp.broadcast_to(m + jnp.log(den), (BLK, BLK))
                o_scr[p, cur, :] = acc
                return 0

            lax.fori_loop(0, nblk, block, 0, unroll=2)

        weights, dens = [], []
        for h in range(2):
            m = jnp.maximum(jnp.maximum(l_scr[0, h], l_scr[1, h]), l_scr[2, h])
            w = [jnp.exp(l_scr[p, h] - m) for p in range(3)]
            den = w[0] + w[1] + w[2]
            lse_ref[h] = m + jnp.log(den)
            weights.append(w)
            dens.append(den)
        num = sum(jnp.where(sels[0], weights[0][p], weights[1][p]) * o_scr[p] for p in range(3))
        out_ref[...] = num / jnp.where(sels[0], dens[0], dens[1])

    bias_spec = pl.BlockSpec((3, 2, BLK, 2 * BLK), lambda b, p: (0, p, 0, 0))
    return pl.pallas_call(
        body, name=name, grid=(nb, 2), in_specs=[_col_spec(s, 6), _col_spec(s, 8), _col_spec(s, 10), bias_spec],
        out_specs=[_col_spec(s, 0), _pair_spec(s, BLK)],
        out_shape=[jax.ShapeDtypeStruct((nb * s, 2 * BLK), F32), jax.ShapeDtypeStruct((nb, N_HEADS, s, BLK), F32)],
        scratch_shapes=[pltpu.VMEM((3, s, BLK), F32), pltpu.VMEM((3, 2, s, BLK), F32)],
        compiler_params=_cparams("parallel", "parallel"),
    )(proj, proj, proj, bias)


def _dil_attention_bwd(proj, dmixed, lse, delta, bias, nb, s, name):
    nblk = s // BLK

    def body(q_ref, k_ref, v_ref, do_ref, lse_ref, dl_ref, b_ref, dq_ref, dk_ref, dv_ref, g_ref):
        sels = _head_masks()
        dq_ref[...] = jnp.zeros_like(dq_ref)
        dk_ref[...] = jnp.zeros_like(dk_ref)
        dv_ref[...] = jnp.zeros_like(dv_ref)
        g_ref[...] = jnp.zeros_like(g_ref)
        for p, (_, dil) in enumerate(DIL_PATTERNS):
            seg = s // dil // BLK

            def block(b, _, p=p, seg=seg, dil=dil):
                cur, prev, has_prev = _residue_rows(b, seg, dil)
                qb = (q_ref[cur, :] * 0.125).astype(MXU_DTYPE)
                dob = do_ref[cur, :].astype(MXU_DTYPE)
                kp, kc = k_ref[prev, :].astype(MXU_DTYPE), k_ref[cur, :].astype(MXU_DTYPE)
                vp, vc = v_ref[prev, :].astype(MXU_DTYPE), v_ref[cur, :].astype(MXU_DTYPE)
                dq = jnp.zeros((BLK, BLK), F32)
                dkp, dkc, dvp, dvc = dq, dq, dq, dq
                for h, sel in enumerate(sels):
                    kph, kch = jnp.where(sel, kp, 0.0), jnp.where(sel, kc, 0.0)
                    qh, doh = jnp.where(sel, qb, 0.0), jnp.where(sel, dob, 0.0)
                    lse_h = lse_ref[h, cur, :]
                    dlt = dl_ref[h, cur, :]
                    zp, zc = _dil_scores(qb, kph, kch, b_ref.at[p], h, has_prev)
                    pp = jnp.exp(zp - lse_h)
                    pc = jnp.exp(zc - lse_h)
                    dsp = pp * (_dot_nt(dob, jnp.where(sel, vp, 0.0)) - dlt)
                    dsc = pc * (_dot_nt(dob, jnp.where(sel, vc, 0.0)) - dlt)
                    g_ref[h, p, :, 0:BLK] += dsp
                    g_ref[h, p, :, BLK:2 * BLK] += dsc
                    dsp = dsp.astype(MXU_DTYPE)
                    dsc = dsc.astype(MXU_DTYPE)
                    dq = dq + _dot(dsp, kph) + _dot(dsc, kch)
                    dkp, dkc = dkp + _dot_tn(dsp, qh), dkc + _dot_tn(dsc, qh)
                    dvp, dvc = dvp + _dot_tn(pp, doh), dvc + _dot_tn(pc, doh)
                dq_ref[cur, :] += dq * 0.125
                dk_ref[prev, :] += dkp
                dk_ref[cur, :] += dkc
                dv_ref[prev, :] += dvp
                dv_ref[cur, :] += dvc
                return 0

            lax.fori_loop(0, nblk, block, 0, unroll=2)

    bias_spec = pl.BlockSpec((3, 2, BLK, 2 * BLK), lambda b, p: (0, p, 0, 0))
    cols, stats = _col_spec(s, 0), _pair_spec(s, BLK)
    return pl.pallas_call(
        body, name=name, grid=(nb, 2),
        in_specs=[_col_spec(s, 6), _col_spec(s, 8), _col_spec(s, 10), _col_spec(s, 2), stats, stats, bias_spec],
        out_specs=[cols, cols, cols, pl.BlockSpec((None, 2, 3, BLK, 2 * BLK), lambda b, p: (b, p, 0, 0, 0))],
        out_shape=[jax.ShapeDtypeStruct((nb * s, 2 * BLK), F32)] * 3 + [jax.ShapeDtypeStruct((nb, N_HEADS, 3, BLK, 2 * BLK), F32)],
        compiler_params=_cparams("parallel", "parallel"),
    )(proj, proj, proj, dmixed, lse, delta, bias)


def _bucket_reduce(gbias, table, name):
    nb = gbias.shape[0]

    def body(g_ref, t_ref, o_ref):
        row = _iota2((8, BLK), 0)
        lane = _iota2((8, BLK), 1)
        gsum = [[sum(g_ref[b, h, p] for b in range(nb)) for p in range(3)] for h in range(N_HEADS)]

        def bucket(k, acc):
            for h in range(N_HEADS):
                tot = sum(jnp.sum(jnp.where(t_ref[p] == k, gsum[h][p], 0.0)) for p in range(3))
                acc = acc + jnp.where((row == h) & (lane == k), tot, 0.0)
            return acc

        o_ref[...] = lax.fori_loop(0, REL_BUCKETS, bucket, jnp.zeros((8, BLK), F32))

    vm = pl.BlockSpec(memory_space=pltpu.VMEM)
    return pl.pallas_call(
        body, name=name, in_specs=[vm, vm], out_specs=vm, out_shape=jax.ShapeDtypeStruct((8, BLK), F32),
        compiler_params=pltpu.CompilerParams(vmem_limit_bytes=VMEM_LIMIT),
    )(gbias, table)


def _place():
    x, y, c = lax.axis_index("x"), lax.axis_index("y"), lax.axis_index("c")
    others = [(1 - x, y), (x, 1 - y), (1 - x, 1 - y)]
    return x, y, c, others


def _remote(src, dst, send_sem, recv_sem, to):
    return pltpu.make_async_remote_copy(src_ref=src, dst_ref=dst, send_sem=send_sem, recv_sem=recv_sem,
                                        device_id=to, device_id_type=MESH)


_HBM = pl.BlockSpec(memory_space=pl.ANY)


class _Exchange:
    def __init__(self, operands, out_shape, n_copies, copies, aliases=None):
        self.operands, self.out_shape, self.n_copies, self.copies = list(operands), list(out_shape), n_copies, copies
        self.aliases = dict(aliases or {})

    def sem_shapes(self):
        return [pltpu.SemaphoreType.DMA((self.n_copies,)), pltpu.SemaphoreType.DMA((self.n_copies,))]


def _start_all(sends):
    for cp in sends:
        cp.start()


def _wait_all(sends, arrivals):
    for cp in arrivals:
        cp.wait_recv()
    for cp in sends:
        cp.wait_send()


def _run_exchange(ex, name):
    ni = len(ex.operands)

    def body(*refs):
        sends, arrivals = ex.copies(refs[:ni], refs[ni:-2], refs[-2], refs[-1])
        _start_all(sends)
        _wait_all(sends, arrivals)

    return list(pl.pallas_call(
        body, name=name, in_specs=[_HBM] * ni, out_specs=[_HBM] * len(ex.out_shape), out_shape=ex.out_shape,
        scratch_shapes=ex.sem_shapes(), input_output_aliases=ex.aliases)(*ex.operands))


def _host_call(body, carry, *, name, grid, in_specs, out_specs, out_shape, operands, scratch_shapes=()):
    in_specs, out_specs, out_shape, scratch_shapes = list(in_specs), list(out_specs), list(out_shape), list(scratch_shapes)
    if carry is None:
        res = pl.pallas_call(body, name=name, grid=grid, in_specs=in_specs, out_specs=out_specs, out_shape=out_shape,
                             scratch_shapes=scratch_shapes, compiler_params=_cparams(*["parallel"] * len(grid)))(*operands)
        return list(res), []
    n_in, n_out, n_scr, c_in, c_out = len(in_specs), len(out_specs), len(scratch_shapes), len(carry.operands), len(carry.out_shape)
    steps = math.prod(grid)

    def wrapped(*refs):
        ins, refs = refs[:n_in], refs[n_in:]
        c_ins, refs = refs[:c_in], refs[c_in:]
        outs, refs = refs[:n_out], refs[n_out:]
        c_outs, refs = refs[:c_out], refs[c_out:]
        scr, (send_sems, recv_sems) = refs[:n_scr], refs[n_scr:]
        step = 0
        for d, size in enumerate(grid):
            step = step * size + pl.program_id(d)

        @pl.when(step == 0)
        def _():
            _start_all(carry.copies(c_ins, c_outs, send_sems, recv_sems)[0])

        body(*ins, *outs, *scr)

        @pl.when(step == steps - 1)
        def _():
            _wait_all(*carry.copies(c_ins, c_outs, send_sems, recv_sems))

    res = pl.pallas_call(
        wrapped, name=name, grid=grid, in_specs=in_specs + [_HBM] * c_in, out_specs=out_specs + [_HBM] * c_out,
        out_shape=out_shape + carry.out_shape, scratch_shapes=scratch_shapes + carry.sem_shapes(),
        input_output_aliases={n_in + i: n_out + j for i, j in carry.aliases.items()},
        compiler_params=_cparams(*["arbitrary"] * len(grid)))(*operands, *carry.operands)
    return list(res[:n_out]), list(res[n_out:])


def _half(which, rows):
    h = rows // 2
    return pl.ds(pl.multiple_of(which * h, 16), h)


def _like(arrays, shape_of=lambda t: t.shape):
    return [jax.ShapeDtypeStruct(shape_of(t), t.dtype) for t in arrays]


def _gather_ici(shards, layer):
    n = len(shards)

    def copies(ins, outs, send_sems, recv_sems):
        x, y, c, others = _place()
        me = 2 * x + y
        sends, arrivals = [], []
        for a in range(n):
            rows = _half(c, shards[a].shape[1])
            for k, (ox, oy) in enumerate(others):
                sems = (send_sems.at[3 * a + k], recv_sems.at[3 * a + k], (ox, oy, c))
                sends.append(_remote(ins[a].at[layer, rows], outs[a].at[me, rows], *sems))
                landed = outs[a].at[2 * ox + oy, rows]
                arrivals.append(_remote(landed, landed, *sems))
        return sends, arrivals

    return _Exchange(shards, _like(shards, lambda t: (N_CHIPS,) + t.shape[1:]), 3 * n, copies)


def _gather_d2d(gathered):
    n = len(gathered)

    def copies(ins, outs, send_sems, recv_sems):
        x, y, c, others = _place()
        sends, arrivals = [], []
        for a in range(n):
            r = gathered[a].shape[1]
            for k, (ox, oy) in enumerate(others):
                sems = (send_sems.at[3 * a + k], recv_sems.at[3 * a + k], (x, y, 1 - c))
                mine, theirs = outs[a].at[2 * ox + oy, _half(c, r)], outs[a].at[2 * ox + oy, _half(1 - c, r)]
                sends.append(_remote(mine, mine, *sems))
                arrivals.append(_remote(theirs, theirs, *sems))
        return sends, arrivals

    return _Exchange(gathered, _like(gathered), 3 * n, copies, aliases={a: a for a in range(n)})


def _swap_halves(g):
    n = len(g)

    def copies(ins, outs, send_sems, recv_sems):
        x, y, c, _ = _place()
        sends, arrivals = [], []
        for a in range(n):
            sems = (send_sems.at[a], recv_sems.at[a], (x, y, 1 - c))
            sends.append(_remote(ins[a].at[:, _half(1 - c, g[a].shape[1])], outs[a], *sems))
            arrivals.append(_remote(outs[a], outs[a], *sems))
        return sends, arrivals

    return _Exchange(g, _like(g, lambda t: (t.shape[0], t.shape[1] // 2, t.shape[2])), n, copies)


def _scatter_shards(ps):
    n = len(ps)

    def copies(ins, outs, send_sems, recv_sems):
        x, y, c, others = _place()
        me = 2 * x + y
        sends, arrivals = [], []
        for a in range(n):
            for k, (ox, oy) in enumerate(others):
                sems = (send_sems.at[3 * a + k], recv_sems.at[3 * a + k], (ox, oy, c))
                sends.append(_remote(ins[a].at[2 * ox + oy], outs[a].at[me], *sems))
                slot = outs[a].at[2 * ox + oy]
                arrivals.append(_remote(slot, slot, *sems))
        return sends, arrivals

    return _Exchange(ps, _like(ps), 3 * n, copies)


def _share_halves(mine):
    n = len(mine)

    def copies(ins, outs, send_sems, recv_sems):
        x, y, c, _ = _place()
        sends, arrivals = [], []
        for a in range(n):
            sems = (send_sems.at[a], recv_sems.at[a], (x, y, 1 - c))
            sends.append(_remote(ins[a], outs[a], *sems))
            arrivals.append(_remote(outs[a], outs[a], *sems))
        return sends, arrivals

    return _Exchange(mine, _like(mine), n, copies)


def _row_tile(r):
    for cand in (256, 352, 128):
        if r % cand == 0:
            return cand
    return r


def _pair_sum(g, other, core, name):
    ns, h, w = other.shape
    tr = _row_tile(h)
    per_half = h // tr

    def body(core_ref, g_ref, o_ref, out_ref):
        out_ref[...] = (g_ref[...] + o_ref[...]).astype(out_ref.dtype)

    blk = pl.BlockSpec((None, tr, w), lambda k, i, core_ref: (k, i, 0))
    grid_spec = pltpu.PrefetchScalarGridSpec(
        num_scalar_prefetch=1, grid=(ns, per_half),
        in_specs=[pl.BlockSpec((None, tr, w), lambda k, i, core_ref: (k, core_ref[0] * per_half + i, 0)), blk], out_specs=blk)
    return pl.pallas_call(
        body, name=name, grid_spec=grid_spec, out_shape=jax.ShapeDtypeStruct((ns, h, w), MXU_DTYPE),
        compiler_params=_cparams("parallel", "parallel"),
    )(core.reshape(1).astype(jnp.int32), g, other)


def _chip_sum(q, p, chip, name):
    ns, r, w = q.shape
    tr = _row_tile(r)

    def body(chip_ref, q_ref, own_ref, out_ref):
        me = chip_ref[0]
        own = own_ref[...].astype(F32)
        terms = [jnp.where(me == k, own, q_ref[k].astype(F32)) for k in range(ns)]
        out_ref[...] = ((terms[0] + terms[1]) + terms[2]) + terms[3]

    grid_spec = pltpu.PrefetchScalarGridSpec(
        num_scalar_prefetch=1, grid=(r // tr,),
        in_specs=[pl.BlockSpec((ns, tr, w), lambda i, chip_ref: (0, i, 0)),
                  pl.BlockSpec((None, tr, w), lambda i, chip_ref: (chip_ref[0], i, 0))],
        out_specs=pl.BlockSpec((tr, w), lambda i, chip_ref: (i, 0)))
    return pl.pallas_call(
        body, name=name, grid_spec=grid_spec, out_shape=jax.ShapeDtypeStruct((r, w), F32),
        compiler_params=_cparams("parallel"),
    )(chip.reshape(1).astype(jnp.int32), q, p)


class _WeightPrefetch:
    def __init__(self, shards, layer, chip):
        self.shards, self.layer, self.chip, self.result = shards, layer, chip, None

    def first(self):
        return _gather_ici(self.shards, self.layer)

    def got_first(self, arrived):
        self.arrived = arrived

    def second(self):
        return _gather_d2d(self.arrived)

    def got_second(self, gathered):
        self.result = [lax.dynamic_update_index_in_dim(got, own[self.layer], self.chip, 0)
                       for got, own in zip(gathered, self.shards)]

    def run(self, tag):
        self.got_first(_run_exchange(self.first(), f"gather_ici_{tag}"))
        self.got_second(_run_exchange(self.second(), f"gather_d2d_{tag}"))
        return self.result


class _GradReduce:
    def __init__(self, g, chip, core, tag):
        self.g, self.chip, self.core, self.tag, self.result = g, chip, core, tag, None

    def swap(self):
        return _swap_halves(self.g)

    def got_swap(self, theirs):
        self.pair = [_pair_sum(g, t, self.core, f"pair_sum_{name}_{self.tag}") for name, g, t in zip(_BIG, self.g, theirs)]

    def scatter(self):
        return _scatter_shards(self.pair)

    def got_scatter(self, q):
        self.mine = [_chip_sum(qa, pa, self.chip, f"chip_sum_{name}_{self.tag}") for name, qa, pa in zip(_BIG, q, self.pair)]

    def share(self):
        return _share_halves(self.mine)

    def got_share(self, theirs):
        self.result = [jnp.where(self.core == 0, jnp.concatenate([a, b]), jnp.concatenate([b, a]))
                       for a, b in zip(self.mine, theirs)]

    def run(self):
        self.got_swap(_run_exchange(self.swap(), f"swap_halves_{self.tag}"))
        self.got_scatter(_run_exchange(self.scatter(), f"scatter_shards_{self.tag}"))
        self.got_share(_run_exchange(self.share(), f"share_halves_{self.tag}"))
        return self.result


def _gather_small(pk, name):
    rows, w = pk.shape

    def body(pk_ref, all_ref, sum_ref, send_sems, recv_sems):
        x, y, c, _ = _place()
        me = 4 * x + 2 * y + c
        all_ref[me] = pk_ref[...]
        flips = [(fx, fy, fc) for fx in (0, 1) for fy in (0, 1) for fc in (0, 1)][1:]
        peers = [(x ^ fx, y ^ fy, c ^ fc) for fx, fy, fc in flips]
        sends = [_remote(pk_ref, all_ref.at[me], send_sems.at[k], recv_sems.at[k], peer) for k, peer in enumerate(peers)]
        for cp in sends:
            cp.start()
        for k, (px, py, pc) in enumerate(peers):
            slot = all_ref.at[4 * px + 2 * py + pc]
            _remote(slot, slot, send_sems.at[k], recv_sems.at[k], (px, py, pc)).wait_recv()
        for cp in sends:
            cp.wait_send()
        total = all_ref[0]
        for d in range(1, N_DEV):
            total = total + all_ref[d]
        sum_ref[...] = total

    vm = pl.BlockSpec(memory_space=pltpu.VMEM)
    return pl.pallas_call(
        body, name=name, in_specs=[vm], out_specs=[vm, vm],
        out_shape=[jax.ShapeDtypeStruct((N_DEV, rows, w), F32), jax.ShapeDtypeStruct((rows, w), F32)],
        scratch_shapes=[pltpu.SemaphoreType.DMA((7,)), pltpu.SemaphoreType.DMA((7,))],
    )(pk)


def _row_layout(c, nb, s):
    ch = jnp.swapaxes(c[:, :N_HEADS].reshape(nb, s, N_HEADS), 1, 2)
    ccol = jnp.broadcast_to(ch[..., None], (nb, N_HEADS, s, ATT))
    crow = jnp.broadcast_to(ch.reshape(nb, N_HEADS, s // ATT, 1, ATT), (nb, N_HEADS, s // ATT, 8, ATT))
    return ccol, crow


def _dil_bias(rel_bias, name):
    def body(rel_ref, t_ref, o_ref):
        for p in range(len(DIL_PATTERNS)):
            table = t_ref[p]

            def bucket(k, accs, table=table):
                return tuple(jnp.where(table == k, rel_ref[k, h], acc) for h, acc in enumerate(accs))

            accs = lax.fori_loop(0, REL_BUCKETS, bucket, tuple(jnp.full((BLK, 2 * BLK), NEG, F32) for _ in range(N_HEADS)))
            for h in range(N_HEADS):
                o_ref[p, h] = accs[h]

    vm = pl.BlockSpec(memory_space=pltpu.VMEM)
    return pl.pallas_call(
        body, name=name, in_specs=[pl.BlockSpec(memory_space=pltpu.SMEM), vm], out_specs=vm,
        out_shape=jax.ShapeDtypeStruct((len(DIL_PATTERNS), N_HEADS, BLK, 2 * BLK), F32),
        compiler_params=pltpu.CompilerParams(vmem_limit_bytes=VMEM_LIMIT),
    )(rel_bias, jnp.asarray(_bucket_table()))


def _layer_forward(x, wts, small, nb, s, tag, prefetch=None):
    proj = _matmul(x, wts["w_in"], "proj", tag)

    o_sb, carried = _sb_fwd(proj, nb, s, f"sb_fwd_{tag}", prefetch and prefetch.first())
    if prefetch:
        prefetch.got_first(carried)

    bias = _dil_bias(small["rel_bias"], f"dil_bias_{tag}")
    o_dl, lse_dl = _dil_attention_fwd(proj, bias, nb, s, f"dil_fwd_{tag}")

    fb = jnp.zeros((8, BLK), F32).at[0, :N_HEADS].set(small["f_bias"])
    csum = _fox_gates_fwd(proj, fb, nb, s, f"fox_gates_{tag}")
    ccol, crow = _row_layout(csum, nb, s)
    (o_fx, lse_fx), carried = _fox_fwd(proj, ccol, crow, nb, s, f"fox_fwd_{tag}", prefetch and prefetch.second())
    if prefetch:
        prefetch.got_second(carried)

    cw = jnp.zeros((8, CONV_W), F32).at[:3].set(small["conv_w"])
    o_cv = _conv_fwd(proj, cw, nb, s, f"conv_fwd_{tag}")

    mixed = jnp.concatenate([o_sb, o_dl, o_fx, o_cv], axis=-1).astype(MXU_DTYPE)
    mix = _matmul(mixed, wts["w_out"], "out_proj", tag)
    pre1, x1 = _ln_fwd(x, mix, small["ln1_g"], small["ln1_b"], f"ln1_fwd_{tag}")
    gu = _matmul(x1, wts["w_gu"], "ffn_in", tag)
    hid = _swiglu_fwd(gu, f"swiglu_fwd_{tag}")
    ffn = _matmul(hid, wts["w_down"], "ffn_out", tag)
    pre2, x2 = _ln_fwd(x1, ffn, small["ln2_g"], small["ln2_b"], f"ln2_fwd_{tag}")
    saved = dict(x=x, proj=proj, bias=bias, o_dl=o_dl, lse_dl=lse_dl, fb=fb, ccol=ccol, crow=crow,
                 o_fx=o_fx, lse_fx=lse_fx, cw=cw, mixed=mixed, pre1=pre1, x1=x1, gu=gu, hid=hid, pre2=pre2)
    return x2, saved


def _layer_backward(dx2, sv, wts, small, nb, s, tag, reduce=None):
    t = nb * s
    dpre2, dgb2 = _ln_bwd(dx2, sv["pre2"], small["ln2_g"], f"ln2_bwd_{tag}")
    dpre2_b = dpre2.astype(MXU_DTYPE)
    dhid = _matmul(dpre2_b, wts["w_down"], "ffn_out_dx", tag, trans_b=True)
    dw_down = _matmul(sv["hid"], dpre2_b, "ffn_out_dw", tag, trans_a=True)
    (dgu,), carried = _swiglu_bwd(dhid, sv["gu"], f"swiglu_bwd_{tag}", reduce and reduce.swap())
    if reduce:
        reduce.got_swap(carried)
    dx1 = _matmul(dgu, wts["w_gu"], "ffn_in_dx", tag, add=dpre2, add_scale=ALPHA, trans_b=True)
    dw_gu = _matmul(sv["x1"].astype(MXU_DTYPE), dgu, "ffn_in_dw", tag, trans_a=True)

    dpre1, dgb1 = _ln_bwd(dx1, sv["pre1"], small["ln1_g"], f"ln1_bwd_{tag}")
    dpre1_b = dpre1.astype(MXU_DTYPE)
    dmixed = _matmul(dpre1_b, wts["w_out"], "out_proj_dx", tag, trans_b=True)
    dw_out = _matmul(sv["mixed"], dpre1_b, "out_proj_dw", tag, trans_a=True)
    proj = sv["proj"]

    (dq_sb, dk_sb, dv_sb), carried = _sb_bwd(proj, dmixed, nb, s, f"sb_bwd_{tag}", reduce and reduce.scatter())
    if reduce:
        reduce.got_scatter(carried)

    delta_dl = _delta_kernel(dmixed, sv["o_dl"], nb, s, f"dil_delta_{tag}")
    dq_dl, dk_dl, dv_dl, gbias = _dil_attention_bwd(proj, dmixed, sv["lse_dl"], delta_dl, sv["bias"], nb, s, f"dil_bwd_{tag}")
    drel = _bucket_reduce(gbias, jnp.asarray(_bucket_table()), f"rel_bias_grad_{tag}")

    (dq_fx, dk_fx, dv_fx, dcol), carried = _fox_bwd(proj, dmixed, sv["lse_fx"], sv["ccol"], sv["crow"], nb, s,
                                                    f"fox_bwd_{tag}", reduce and reduce.share())
    if reduce:
        reduce.got_share(carried)
    dcs = -jnp.swapaxes(dcol[:, :, :, 0, :].reshape(nb, N_HEADS, s), 1, 2).reshape(t, N_HEADS)
    dcs = jnp.pad(dcs, ((0, 0), (0, BLK - N_HEADS)))
    dfx, dfb = _fox_gates_bwd(dcs, proj, sv["fb"], nb, s, f"fox_gates_bwd_{tag}")

    dgates, dcw = _conv_bwd(dmixed, proj, sv["cw"], nb, s, f"conv_bwd_{tag}")

    dproj = jnp.concatenate([dq_sb, dk_sb, dv_sb, dq_dl, dk_dl, dv_dl, dq_fx, dk_fx, dv_fx, dgates, dfx],
                            axis=-1).astype(MXU_DTYPE)
    dx = _matmul(dproj, wts["w_in"], "proj_dx", tag, add=dpre1, add_scale=ALPHA, trans_b=True)
    dw_in = _matmul(sv["x"].astype(MXU_DTYPE), dproj, "proj_dw", tag, trans_a=True)

    grads = dict(w_in=dw_in[:, :PROJ], w_out=dw_out, w_gate=dw_gu[:, :D_FF], w_up=dw_gu[:, D_FF:], w_down=dw_down,
                 ln1_g=dgb1[0], ln1_b=dgb1[1], ln2_g=dgb2[0], ln2_b=dgb2[1], conv_w=dcw[:3], f_bias=dfb[0, :N_HEADS],
                 rel_bias=drel[:N_HEADS, :REL_BUCKETS].T)
    return dx, grads


def _local_step(x, target, weights_of, small_all, prefetch=None, make_reduce=None):
    nb, s, d = x.shape
    h = x.reshape(nb * s, d)
    saved = []
    for layer in range(DEPTH):
        wts = weights_of(layer)
        ahead = prefetch[layer + 1] if prefetch and layer + 1 < DEPTH else None
        h, sv = _layer_forward(h, wts, small_all[layer], nb, s, f"l{layer}", ahead)
        saved.append((sv, wts))
    dy, lossp = _loss_kernel(h, target.reshape(nb * s, d), "loss")
    grads, reduces, pending = [None] * DEPTH, [None] * DEPTH, None
    for layer in reversed(range(DEPTH)):
        sv, wts = saved[layer]
        dy, grads[layer] = _layer_backward(dy, sv, wts, small_all[layer], nb, s, f"l{layer}", pending)
        pending = reduces[layer] = make_reduce(layer, grads[layer]) if make_reduce else None
    if pending:
        pending.run()
    return lossp, dy.reshape(nb, s, d), grads, reduces


_BIG = ("w_in", "w_out", "w_gate", "w_up", "w_down")
_COL_SHARDED = ("w_in", "w_gate", "w_up")


def _full_weights(gathered):
    cols = lambda t: jnp.swapaxes(t, 0, 1).reshape(t.shape[1], -1)
    rows = lambda t: t.reshape(-1, t.shape[2])
    w_in = jnp.pad(cols(gathered["w_in"]), ((0, 0), (0, PROJ_PAD - PROJ)))
    w_gu = jnp.concatenate([cols(gathered["w_gate"]), cols(gathered["w_up"])], axis=-1)
    return dict(w_in=w_in, w_out=rows(gathered["w_out"]), w_gu=w_gu, w_down=rows(gathered["w_down"]))


def _by_chip(name, g):
    if name in _COL_SHARDED:
        return jnp.swapaxes(g.reshape(g.shape[0], N_CHIPS, -1), 0, 1)
    return g.reshape(N_CHIPS, -1, g.shape[1])


_SMALL_LAYOUT = (("ln1_g", 0), ("ln1_b", 2), ("ln2_g", 4), ("ln2_b", 6), ("conv_w", 8))
_ROW_MISC = 10
_ROW_LOSS = 11


def _pack_small(per_layer, rel_bias, loss=None):
    pk = jnp.zeros((SMALL_ROWS, D_MODEL), F32)
    for name, row in _SMALL_LAYOUT:
        for l in range(DEPTH):
            v = per_layer[l][name].reshape(-1)
            pk = pk.at[row + l, :v.shape[0]].set(v)
    fb = jnp.concatenate([per_layer[l]["f_bias"] for l in range(DEPTH)])
    pk = pk.at[_ROW_MISC, :2 * N_HEADS].set(fb)
    pk = pk.at[_ROW_MISC, BLK:BLK + REL_BUCKETS * N_HEADS].set(rel_bias.reshape(-1))
    if loss is not None:
        pk = pk.at[_ROW_LOSS, 0].set(loss)
    return pk


def _unpack_small(pk, conv_cols):
    out = {}
    for name, row in _SMALL_LAYOUT:
        n = 3 * conv_cols if name == "conv_w" else D_MODEL
        v = pk[row:row + DEPTH, :n]
        out[name] = v.reshape(DEPTH, 3, conv_cols) if name == "conv_w" else v
    out["f_bias"] = pk[_ROW_MISC, :2 * N_HEADS].reshape(DEPTH, N_HEADS)
    out["rel_bias"] = pk[_ROW_MISC, BLK:BLK + REL_BUCKETS * N_HEADS].reshape(REL_BUCKETS, N_HEADS)
    return out


_WEIGHTS = ("w_in", "f_bias", "conv_w", "w_out", "rel_bias", "ln1_g", "ln1_b", "w_gate", "w_up", "w_down", "ln2_g", "ln2_b")


def kernel(x, w_in, f_bias, conv_w, w_out, rel_bias, ln1_g, ln1_b, w_gate, w_up, w_down, ln2_g, ln2_b, loss_target, m_w_in, m_f_bias, m_conv_w, m_w_out, m_rel_bias, m_ln1_g, m_ln1_b, m_w_gate, m_w_up, m_w_down, m_ln2_g, m_ln2_b, v_w_in, v_f_bias, v_conv_w, v_w_out, v_rel_bias, v_ln1_g, v_ln1_b, v_w_gate, v_w_up, v_w_down, v_ln2_g, v_ln2_b):
    w = dict(w_in=w_in, f_bias=f_bias, conv_w=conv_w, w_out=w_out, rel_bias=rel_bias, ln1_g=ln1_g, ln1_b=ln1_b,
             w_gate=w_gate, w_up=w_up, w_down=w_down, ln2_g=ln2_g, ln2_b=ln2_b)
    m = dict(w_in=m_w_in, f_bias=m_f_bias, conv_w=m_conv_w, w_out=m_w_out, rel_bias=m_rel_bias, ln1_g=m_ln1_g,
             ln1_b=m_ln1_b, w_gate=m_w_gate, w_up=m_w_up, w_down=m_w_down, ln2_g=m_ln2_g, ln2_b=m_ln2_b)
    v = dict(w_in=v_w_in, f_bias=v_f_bias, conv_w=v_conv_w, w_out=v_w_out, rel_bias=v_rel_bias, ln1_g=v_ln1_g,
             ln1_b=v_ln1_b, w_gate=v_w_gate, w_up=v_w_up, w_down=v_w_down, ln2_g=v_ln2_g, ln2_b=v_ln2_b)
    chip = 2 * lax.axis_index("x") + lax.axis_index("y")
    core = lax.axis_index("c")
    conv_shard = CONV_W // N_CHIPS

    shards = [w[name].astype(MXU_DTYPE) for name in _BIG]
    fetch = [_WeightPrefetch(shards, l, chip) for l in range(DEPTH)]
    fetch[0].run("l0")
    cw_pk = jnp.zeros((8, D_MODEL), F32).at[0, :DEPTH * 3 * conv_shard].set(conv_w.reshape(-1))
    cw_all, _ = _gather_small(cw_pk, "gather_conv_w")
    cw_chips = cw_all[0::2, 0, :DEPTH * 3 * conv_shard].reshape(N_CHIPS, DEPTH, 3, conv_shard)
    conv_full = jnp.moveaxis(cw_chips, 0, 2).reshape(DEPTH, 3, CONV_W)
    small_all = [dict(f_bias=f_bias[l], conv_w=conv_full[l], rel_bias=rel_bias, ln1_g=ln1_g[l], ln1_b=ln1_b[l],
                      ln2_g=ln2_g[l], ln2_b=ln2_b[l]) for l in range(DEPTH)]

    lossp, grad_x, grads, reduces = _local_step(
        x, loss_target, lambda l: _full_weights(dict(zip(_BIG, fetch[l].result))), small_all, fetch,
        lambda l, g: _GradReduce([_by_chip(name, g[name]) for name in _BIG], chip, core, f"l{l}"))
    big_g = {name: jnp.stack([reduces[l].result[a] for l in range(DEPTH)]) for a, name in enumerate(_BIG)}

    drel = grads[0]["rel_bias"] + grads[1]["rel_bias"]
    small_pk = _pack_small(grads, drel, lossp[0, 0])
    _, small_sum = _gather_small(small_pk, "gather_small_grads")
    loss = small_sum[_ROW_LOSS, 0]
    small_g = _unpack_small(small_sum, CONV_W)
    small_g["conv_w"] = lax.dynamic_slice_in_dim(small_g["conv_w"], chip * conv_shard, conv_shard, axis=2)

    out_g, out_d, out_m, out_v = dict(small_g), {}, {}, {}
    for name in _BIG:
        out_g[name] = big_g[name]
        out_d[name], out_m[name], out_v[name] = _adamw(w[name], big_g[name], m[name], v[name], f"adamw_{name}")
    per_layer = lambda src: [{name: src[name][l] for name in ("ln1_g", "ln1_b", "ln2_g", "ln2_b", "conv_w", "f_bias")}
                             for l in range(DEPTH)]
    packs = [_pack_small(per_layer(src), src["rel_bias"])[None] for src in (w, small_g, m, v)]
    for dst, pk in zip((out_d, out_m, out_v), _adamw(*packs, "adamw_small")):
        dst.update(_unpack_small(pk[0], conv_shard))

    return (loss, grad_x, *[out_g[n] for n in _WEIGHTS], *[out_d[n] for n in _WEIGHTS],
            *[out_m[n] for n in _WEIGHTS], *[out_v[n] for n in _WEIGHTS])
```

```python
import functools
import math

import numpy as np
import jax
import jax.numpy as jnp
from jax import lax
from jax.experimental import pallas as pl
from jax.experimental.pallas import tpu as pltpu

F32 = jnp.float32
BF16 = jnp.bfloat16
MXU_DTYPE = BF16

D_MODEL = 1024
HEAD_DIM = 64
N_HEADS = 4
BLK = 128
ATT = 256
QT = 512
CONV_W = 256
PROJ = 3076
PROJ_PAD = 3200
D_FF = 2816
DEPTH = 2
ALPHA = (2 * DEPTH) ** 0.25
LN_EPS = 1e-5
NEG = -1e30
DIL_PATTERNS = ((128, 1), (512, 4), (2048, 16))
REL_BUCKETS = 32
N_CHIPS = 4
N_DEV = 8
SMALL_ROWS = 16

ADAM_LR = 0.001
ADAM_B1 = 0.9
ADAM_B2 = 0.999
ADAM_EPS = 1e-08
ADAM_WD = 0.01
ADAM_STEP = 10

VMEM_LIMIT = 56 * 2 ** 20
MESH = pl.DeviceIdType.MESH


def _cparams(*sem):
    return pltpu.CompilerParams(dimension_semantics=tuple(sem), vmem_limit_bytes=VMEM_LIMIT)


def _dot(a, b):
    return jnp.dot(a.astype(MXU_DTYPE), b.astype(MXU_DTYPE), preferred_element_type=F32)


def _dot_nt(a, b):
    return lax.dot_general(a.astype(MXU_DTYPE), b.astype(MXU_DTYPE), (((1,), (1,)), ((), ())),
                           preferred_element_type=F32)


def _dot_tn(a, b):
    return lax.dot_general(a.astype(MXU_DTYPE), b.astype(MXU_DTYPE), (((0,), (0,)), ((), ())),
                           preferred_element_type=F32)


def _split_dot(x, ones, passes):
    acc, rest = None, x
    for p in range(passes):
        piece = rest.astype(MXU_DTYPE)
        part = jnp.dot(piece, ones, preferred_element_type=F32)
        acc = part if acc is None else acc + part
        if p + 1 < passes:
            rest = rest - piece.astype(F32)
    return acc


def _split_dot_lhs(ones, x, passes):
    acc, rest = None, x
    for p in range(passes):
        piece = rest.astype(MXU_DTYPE)
        part = jnp.dot(ones, piece, preferred_element_type=F32)
        acc = part if acc is None else acc + part
        if p + 1 < passes:
            rest = rest - piece.astype(F32)
    return acc


def _iota2(shape, axis):
    return lax.broadcasted_iota(jnp.int32, shape, axis)


_TILES = {"proj": (1024, 640, 1024), "out_proj": (1024, 1024, 1024), "ffn_in": (1024, 1408, 1024),
          "ffn_out": (1024, 1024, 2816), "ffn_out_dx": (1024, 1408, 1024), "ffn_out_dw": (1408, 1024, 2048),
          "ffn_in_dx": (1024, 1024, 1408), "ffn_in_dw": (1024, 1408, 2048), "out_proj_dx": (1024, 1024, 1024),
          "out_proj_dw": (1024, 1024, 2048), "proj_dx": (1024, 512, 3200), "proj_dw": (1024, 640, 2048)}


def _matmul(a, b, kind, tag, *, out_dtype=F32, add=None, add_scale=1.0, trans_a=False, trans_b=False):
    k, m = a.shape if trans_a else a.shape[::-1]
    n = b.shape[0] if trans_b else b.shape[1]
    tm, tn, tk = _TILES[kind]
    tm, tk, name = min(tm, m), min(tk, k), f"{kind}_{tag}"
    assert m % tm == 0 and n % tn == 0 and k % tk == 0, (a.shape, b.shape, tm, tn, tk)
    nk = k // tk

    def body(*refs):
        if add is None:
            a_ref, b_ref, o_ref = refs[:3]
            c_ref, scr = None, refs[3:]
        else:
            a_ref, b_ref, c_ref, o_ref = refs[:4]
            scr = refs[4:]
        dot = _dot_tn if trans_a else _dot_nt if trans_b else _dot
        part = dot(a_ref[...], b_ref[...])

        def finish(acc):
            if c_ref is not None:
                acc = acc + add_scale * c_ref[...]
            o_ref[...] = acc.astype(out_dtype)

        if nk == 1:
            finish(part)
        else:
            acc_ref = scr[0]
            kk = pl.program_id(2)

            @pl.when(kk == 0)
            def _():
                acc_ref[...] = part

            @pl.when(kk > 0)
            def _():
                acc_ref[...] += part

            @pl.when(kk == nk - 1)
            def _():
                finish(acc_ref[...])

    b_spec = pl.BlockSpec((tn, tk), lambda i, j, kk: (j, kk)) if trans_b else pl.BlockSpec((tk, tn), lambda i, j, kk: (kk, j))
    a_spec = pl.BlockSpec((tk, tm), lambda i, j, kk: (kk, i)) if trans_a else pl.BlockSpec((tm, tk), lambda i, j, kk: (i, kk))
    in_specs = [a_spec, b_spec]
    operands = [a, b]
    if add is not None:
        in_specs.append(pl.BlockSpec((tm, tn), lambda i, j, kk: (i, j)))
        operands.append(add)
    return pl.pallas_call(
        body, name=name, grid=(m // tm, n // tn, nk), in_specs=in_specs,
        out_specs=pl.BlockSpec((tm, tn), lambda i, j, kk: (i, j)),
        out_shape=jax.ShapeDtypeStruct((m, n), out_dtype),
        scratch_shapes=[pltpu.VMEM((tm, tn), F32)] if nk > 1 else [],
        compiler_params=_cparams("parallel", "parallel", "arbitrary"),
    )(*operands)


def _ln_stats(pre):
    mu = jnp.mean(pre, axis=-1, keepdims=True)
    xc = pre - mu
    var = jnp.mean(xc * xc, axis=-1, keepdims=True)
    rstd = lax.rsqrt(var + LN_EPS)
    return xc * rstd, rstd


def _ln_fwd(xin, branch, g, b, name):
    t, d = xin.shape
    tile = 256

    def body(x_ref, br_ref, g_ref, b_ref, pre_ref, y_ref):
        pre = ALPHA * x_ref[...] + br_ref[...]
        xhat, _ = _ln_stats(pre)
        pre_ref[...] = pre
        y_ref[...] = xhat * g_ref[...] + b_ref[...]

    row = pl.BlockSpec((tile, d), lambda i: (i, 0))
    vec = pl.BlockSpec((1, d), lambda i: (0, 0))
    return pl.pallas_call(
        body, name=name, grid=(t // tile,), in_specs=[row, row, vec, vec], out_specs=[row, row],
        out_shape=[jax.ShapeDtypeStruct((t, d), F32)] * 2, compiler_params=_cparams("parallel"),
    )(xin, branch, g.reshape(1, d), b.reshape(1, d))


def _ln_bwd(dy, pre, g, name):
    t, d = dy.shape
    tile = 256

    def body(dy_ref, pre_ref, g_ref, dpre_ref, dgb_ref):
        dyv = dy_ref[...]
        xhat, rstd = _ln_stats(pre_ref[...])
        dxh = dyv * g_ref[...]
        m1 = jnp.mean(dxh, axis=-1, keepdims=True)
        m2 = jnp.mean(dxh * xhat, axis=-1, keepdims=True)
        dpre_ref[...] = rstd * (dxh - m1 - xhat * m2)

        @pl.when(pl.program_id(0) == 0)
        def _():
            dgb_ref[...] = jnp.zeros_like(dgb_ref)

        dgb_ref[0:1, :] += jnp.sum(dyv * xhat, axis=0, keepdims=True)
        dgb_ref[1:2, :] += jnp.sum(dyv, axis=0, keepdims=True)

    row = pl.BlockSpec((tile, d), lambda i: (i, 0))
    return pl.pallas_call(
        body, name=name, grid=(t // tile,), in_specs=[row, row, pl.BlockSpec((1, d), lambda i: (0, 0))],
        out_specs=[row, pl.BlockSpec((8, d), lambda i: (0, 0))],
        out_shape=[jax.ShapeDtypeStruct((t, d), F32), jax.ShapeDtypeStruct((8, d), F32)],
        compiler_params=_cparams("arbitrary"),
    )(dy, pre, g.reshape(1, d))


def _swiglu_fwd(gu, name):
    t = gu.shape[0]
    tile = 256

    def body(gu_ref, h_ref):
        gate = gu_ref[:, :D_FF]
        up = gu_ref[:, D_FF:]
        h_ref[...] = (gate * (1.0 / (1.0 + jnp.exp(-gate))) * up).astype(h_ref.dtype)

    return pl.pallas_call(
        body, name=name, grid=(t // tile,), in_specs=[pl.BlockSpec((tile, 2 * D_FF), lambda i: (i, 0))],
        out_specs=pl.BlockSpec((tile, D_FF), lambda i: (i, 0)),
        out_shape=jax.ShapeDtypeStruct((t, D_FF), MXU_DTYPE), compiler_params=_cparams("parallel"),
    )(gu)


def _swiglu_bwd(dh, gu, name, carry=None):
    t = gu.shape[0]
    tile = 256

    def body(dh_ref, gu_ref, dgu_ref):
        gate = gu_ref[:, :D_FF]
        up = gu_ref[:, D_FF:]
        dhv = dh_ref[...]
        sig = 1.0 / (1.0 + jnp.exp(-gate))
        dgu_ref[:, :D_FF] = (dhv * up * sig * (1.0 + gate * (1.0 - sig))).astype(dgu_ref.dtype)
        dgu_ref[:, D_FF:] = (dhv * gate * sig).astype(dgu_ref.dtype)

    return _host_call(
        body, carry, name=name, grid=(t // tile,),
        in_specs=[pl.BlockSpec((tile, D_FF), lambda i: (i, 0)), pl.BlockSpec((tile, 2 * D_FF), lambda i: (i, 0))],
        out_specs=[pl.BlockSpec((tile, 2 * D_FF), lambda i: (i, 0))],
        out_shape=[jax.ShapeDtypeStruct((t, 2 * D_FF), MXU_DTYPE)], operands=(dh, gu))


def _loss_kernel(y, target, name):
    t, d = y.shape
    tile = 512

    def body(y_ref, t_ref, dy_ref, l_ref):
        err = y_ref[...] - t_ref[...]
        dy_ref[...] = err * (1.0 / d)

        @pl.when(pl.program_id(0) == 0)
        def _():
            l_ref[...] = jnp.zeros_like(l_ref)

        l_ref[...] += jnp.sum(err * err) * (0.5 / d)

    row = pl.BlockSpec((tile, d), lambda i: (i, 0))
    return pl.pallas_call(
        body, name=name, grid=(t // tile,), in_specs=[row, row],
        out_specs=[row, pl.BlockSpec((8, 128), lambda i: (0, 0))],
        out_shape=[jax.ShapeDtypeStruct((t, d), F32), jax.ShapeDtypeStruct((8, 128), F32)],
        compiler_params=_cparams("arbitrary"),
    )(y, target)


def _adamw(w, g, m, v, name):
    nl, r, c = w.shape
    tr = r
    for cand in (256, 352, 128, 64, 16, 8):
        if r % cand == 0:
            tr = cand
            break

    def body(w_ref, g_ref, m_ref, v_ref, d_ref, nm_ref, nv_ref):
        gv = g_ref[...]
        nm = ADAM_B1 * m_ref[...] + (1.0 - ADAM_B1) * gv
        nv = ADAM_B2 * v_ref[...] + (1.0 - ADAM_B2) * (gv * gv)
        m_hat = nm / (1.0 - ADAM_B1 ** ADAM_STEP)
        v_hat = nv / (1.0 - ADAM_B2 ** ADAM_STEP)
        d_ref[...] = -ADAM_LR * (m_hat / (jnp.sqrt(v_hat) + ADAM_EPS) + ADAM_WD * w_ref[...])
        nm_ref[...] = nm
        nv_ref[...] = nv

    blk = pl.BlockSpec((1, tr, c), lambda l, i: (l, i, 0))
    return pl.pallas_call(
        body, name=name, grid=(nl, r // tr), in_specs=[blk] * 4, out_specs=[blk] * 3,
        out_shape=[jax.ShapeDtypeStruct(w.shape, F32)] * 3, compiler_params=_cparams("parallel", "parallel"),
    )(w, g, m, v)


def _shift_down(u, k, rows):
    return jnp.where(rows >= k, pltpu.roll(u, k, 0), 0.0)


def _shift_up(u, k, rows, s):
    return jnp.where(rows < s - k, pltpu.roll(u, s - k, 0), 0.0)


def _conv_fwd(proj, conv_w, nb, s, name):
    def body(b_ref, c_ref, h_ref, w_ref, o_ref):
        rows = _iota2((s, CONV_W), 0)
        u = c_ref[...] * h_ref[...]
        y = w_ref[2:3, :] * u + w_ref[1:2, :] * _shift_down(u, 1, rows) + w_ref[0:1, :] * _shift_down(u, 2, rows)
        o_ref[...] = b_ref[...] * y

    col = lambda j: pl.BlockSpec((s, CONV_W), lambda b: (b, j))
    return pl.pallas_call(
        body, name=name, grid=(nb,),
        in_specs=[col(9), col(10), col(11), pl.BlockSpec((8, CONV_W), lambda b: (0, 0))],
        out_specs=pl.BlockSpec((s, CONV_W), lambda b: (b, 0)),
        out_shape=jax.ShapeDtypeStruct((nb * s, CONV_W), F32), compiler_params=_cparams("parallel"),
    )(proj, proj, proj, conv_w)


def _conv_bwd(dmixed, proj, conv_w, nb, s, name):
    def body(do_ref, b_ref, c_ref, h_ref, w_ref, dg_ref, dw_ref):
        rows = _iota2((s, CONV_W), 0)
        cg, hg, bg, dout = c_ref[...], h_ref[...], b_ref[...], do_ref[...]
        u = cg * hg
        u1 = _shift_down(u, 1, rows)
        u2 = _shift_down(u, 2, rows)
        y = w_ref[2:3, :] * u + w_ref[1:2, :] * u1 + w_ref[0:1, :] * u2
        dy = dout * bg
        du = w_ref[2:3, :] * dy + w_ref[1:2, :] * _shift_up(dy, 1, rows, s) + w_ref[0:1, :] * _shift_up(dy, 2, rows, s)
        dg_ref[:, 0:CONV_W] = dout * y
        dg_ref[:, CONV_W:2 * CONV_W] = du * hg
        dg_ref[:, 2 * CONV_W:3 * CONV_W] = du * cg

        @pl.when(pl.program_id(0) == 0)
        def _():
            dw_ref[...] = jnp.zeros_like(dw_ref)

        dw_ref[0:1, :] += jnp.sum(dy * u2, axis=0, keepdims=True)
        dw_ref[1:2, :] += jnp.sum(dy * u1, axis=0, keepdims=True)
        dw_ref[2:3, :] += jnp.sum(dy * u, axis=0, keepdims=True)

    col = lambda j: pl.BlockSpec((s, CONV_W), lambda b: (b, j))
    return pl.pallas_call(
        body, name=name, grid=(nb,),
        in_specs=[col(3), col(9), col(10), col(11), pl.BlockSpec((8, CONV_W), lambda b: (0, 0))],
        out_specs=[pl.BlockSpec((s, 3 * CONV_W), lambda b: (b, 0)), pl.BlockSpec((8, CONV_W), lambda b: (0, 0))],
        out_shape=[jax.ShapeDtypeStruct((nb * s, 3 * CONV_W), F32), jax.ShapeDtypeStruct((8, CONV_W), F32)],
        compiler_params=_cparams("arbitrary"),
    )(dmixed, proj, proj, proj, conv_w)


def _col_spec(s, base):
    return pl.BlockSpec((s, BLK), lambda b, p: (b, base + p))


def _qrows(i):
    return pl.ds(pl.multiple_of(i * QT, QT), QT)


def _rows(j):
    return pl.ds(pl.multiple_of(j * ATT, ATT), ATT)


def _keys_upto(i):
    return (i + 1) * (QT // ATT)


def _triangle(keep):
    return keep(_iota2((ATT, ATT), 0), _iota2((ATT, ATT), 1)).astype(MXU_DTYPE)


def _rows128(i):
    return pl.ds(pl.multiple_of(i * BLK, BLK), BLK)


def _log_sigmoid_parts(z):
    e = jnp.exp(-jnp.abs(z))
    l1p = jnp.log(1.0 + e)
    lb = jnp.minimum(z, 0.0) - l1p
    return lb, lb - z, e


def _head_masks():
    lane = _iota2((1, BLK), 1)
    return [(lane >= h * HEAD_DIM) & (lane < (h + 1) * HEAD_DIM) for h in range(2)]


def _split_heads(ref, scr, sels):
    for h, sel in enumerate(sels):
        scr[h] = jnp.where(sel, ref[...], 0.0).astype(MXU_DTYPE)


def _sb_fwd(proj, nb, s, name, carry=None):
    nblk = s // ATT

    def body(q_ref, k_ref, v_ref, o_ref, km, vm):
        sels = _head_masks()
        _split_heads(k_ref, km, sels)
        _split_heads(v_ref, vm, sels)
        rows = _iota2((QT, ATT), 0)
        cols = _iota2((QT, ATT), 1)
        later = _triangle(lambda r, c: r > c)

        def qblock(i, _):
            qi = (q_ref[_qrows(i), :] * 0.125).astype(MXU_DTYPE)

            def kblock(t, state):
                carries, acc = state
                j = _keys_upto(i) - 1 - t
                strict = (cols + (j * ATT - i * QT)) < rows
                out = []
                for h in range(2):
                    z = _dot_nt(qi, km[h, _rows(j), :])
                    lb, lr, _ = _log_sigmoid_parts(z)
                    lr = jnp.where(strict, lr, 0.0)
                    tail = _split_dot(lr, later, 2) + carries[h]
                    a = jnp.where(strict, jnp.exp(lb + tail), 0.0)
                    acc = acc + _dot(a, vm[h, _rows(j), :])
                    out.append(carries[h] + jnp.sum(lr, axis=-1, keepdims=True))
                return tuple(out), acc

            init = ((jnp.zeros((QT, 1), F32),) * 2, jnp.zeros((QT, BLK), F32))
            _, acc = lax.fori_loop(0, _keys_upto(i), kblock, init)
            o_ref[_qrows(i), :] = acc
            return 0

        lax.fori_loop(0, s // QT, qblock, 0)

    (o,), extra = _host_call(
        body, carry, name=name, grid=(nb, 2), in_specs=[_col_spec(s, 0), _col_spec(s, 2), _col_spec(s, 4)],
        out_specs=[_col_spec(s, 0)], out_shape=[jax.ShapeDtypeStruct((nb * s, 2 * BLK), F32)],
        scratch_shapes=[pltpu.VMEM((2, s, BLK), MXU_DTYPE)] * 2, operands=(proj, proj, proj))
    return o, extra


def _sb_bwd(proj, dmixed, nb, s, name, carry=None):
    nblk = s // ATT

    def body(q_ref, k_ref, v_ref, do_ref, dq_ref, dk_ref, dv_ref, km, vm, a_scr, dl_scr, beta_scr):
        sels = _head_masks()
        _split_heads(k_ref, km, sels)
        _split_heads(v_ref, vm, sels)
        rows = _iota2((QT, ATT), 0)
        cols = _iota2((QT, ATT), 1)
        later = _triangle(lambda r, c: r > c)
        earlier = _triangle(lambda r, c: r < c)
        dk_ref[...] = jnp.zeros_like(dk_ref)
        dv_ref[...] = jnp.zeros_like(dv_ref)

        def qblock(i, _):
            qi = (q_ref[_qrows(i), :] * 0.125).astype(MXU_DTYPE)
            doi = do_ref[_qrows(i), :].astype(MXU_DTYPE)
            qm = [jnp.where(sel, qi, 0.0) for sel in sels]
            dom = [jnp.where(sel, doi, 0.0) for sel in sels]

            def first(t, carries):
                j = _keys_upto(i) - 1 - t
                strict = (cols + (j * ATT - i * QT)) < rows
                out = []
                for h in range(2):
                    z = _dot_nt(qi, km[h, _rows(j), :])
                    lb, lr, e = _log_sigmoid_parts(z)
                    lr = jnp.where(strict, lr, 0.0)
                    tail = _split_dot(lr, later, 2) + carries[h]
                    a = jnp.where(strict, jnp.exp(lb + tail), 0.0)
                    a_scr[h, j] = a
                    dl_scr[h, j] = a * _dot_nt(doi, vm[h, _rows(j), :])
                    beta_scr[h, j] = jnp.exp(lb)
                    out.append(carries[h] + jnp.sum(lr, axis=-1, keepdims=True))
                return tuple(out)

            lax.fori_loop(0, _keys_upto(i), first, (jnp.zeros((QT, 1), F32),) * 2)

            def second(j, state):
                csums, dq = state
                strict = (cols + (j * ATT - i * QT)) < rows
                out = []
                for h in range(2):
                    dl = dl_scr[h, j]
                    beta = beta_scr[h, j]
                    before = _split_dot(dl, earlier, 2) + csums[h]
                    dz = jnp.where(strict, dl * (1.0 - beta) - beta * before, 0.0).astype(MXU_DTYPE)
                    dq = dq + _dot(dz, km[h, _rows(j), :])
                    dk_ref[_rows(j), :] += _dot_tn(dz, qm[h])
                    dv_ref[_rows(j), :] += _dot_tn(a_scr[h, j], dom[h])
                    out.append(csums[h] + jnp.sum(dl, axis=-1, keepdims=True))
                return tuple(out), dq

            init = ((jnp.zeros((QT, 1), F32),) * 2, jnp.zeros((QT, BLK), F32))
            _, dq = lax.fori_loop(0, _keys_upto(i), second, init)
            dq_ref[_qrows(i), :] = dq * 0.125
            return 0

        lax.fori_loop(0, s // QT, qblock, 0)

    out = _col_spec(s, 0)
    return _host_call(
        body, carry, name=name, grid=(nb, 2),
        in_specs=[_col_spec(s, 0), _col_spec(s, 2), _col_spec(s, 4), out], out_specs=[out] * 3,
        out_shape=[jax.ShapeDtypeStruct((nb * s, 2 * BLK), F32)] * 3,
        scratch_shapes=[pltpu.VMEM((2, s, BLK), MXU_DTYPE)] * 2 + [pltpu.VMEM((2, nblk, QT, ATT), F32)] * 3,
        operands=(proj, proj, proj, dmixed))


def _pair_spec(s, width):
    return pl.BlockSpec((None, 2, s, width), lambda b, p: (b, p, 0, 0))


def _fox_fwd(proj, ccol, crow, nb, s, name, carry=None):
    nblk = s // ATT

    def body(q_ref, k_ref, v_ref, cc_ref, cr_ref, o_ref, lse_ref, km, vm):
        sels = _head_masks()
        _split_heads(k_ref, km, sels)
        _split_heads(v_ref, vm, sels)
        rows = _iota2((QT, ATT), 0)
        cols = _iota2((QT, ATT), 1)

        def qblock(i, _):
            qi = (q_ref[_qrows(i), :] * 0.125).astype(MXU_DTYPE)
            ci = [cc_ref[h, _qrows(i), :] for h in range(2)]

            def kblock(j, state):
                ms, ls, acc = state
                causal = (cols + (j * ATT - i * QT)) <= rows
                new_m, new_l, scales, parts = [], [], [], []
                for h in range(2):
                    z = _dot_nt(qi, km[h, _rows(j), :]) + (ci[h] - cr_ref[h, j][0:1, :])
                    z = jnp.where(causal, z, NEG)
                    m_new = jnp.maximum(ms[h], jnp.max(z, axis=-1, keepdims=True))
                    p = jnp.exp(z - m_new)
                    scale = jnp.exp(ms[h] - m_new)
                    new_m.append(m_new)
                    new_l.append(scale * ls[h] + jnp.sum(p, axis=-1, keepdims=True))
                    scales.append(scale)
                    parts.append(_dot(p, vm[h, _rows(j), :]))
                acc = jnp.where(sels[0], scales[0], scales[1]) * acc + parts[0] + parts[1]
                return tuple(new_m), tuple(new_l), acc

            init = ((jnp.full((QT, 1), NEG, F32),) * 2, (jnp.zeros((QT, 1), F32),) * 2, jnp.zeros((QT, BLK), F32))
            ms, ls, acc = lax.fori_loop(0, _keys_upto(i), kblock, init)
            o_ref[_qrows(i), :] = acc / jnp.where(sels[0], ls[0], ls[1])
            for h in range(2):
                lse_ref[h, _qrows(i), :] = jnp.broadcast_to(ms[h] + jnp.log(ls[h]), (QT, ATT))
            return 0

        lax.fori_loop(0, s // QT, qblock, 0)

    crow_spec = pl.BlockSpec((None, 2, nblk, 8, ATT), lambda b, p: (b, p, 0, 0, 0))
    return _host_call(
        body, carry, name=name, grid=(nb, 2),
        in_specs=[_col_spec(s, 12), _col_spec(s, 14), _col_spec(s, 16), _pair_spec(s, ATT), crow_spec],
        out_specs=[_col_spec(s, 0), _pair_spec(s, ATT)],
        out_shape=[jax.ShapeDtypeStruct((nb * s, 2 * BLK), F32), jax.ShapeDtypeStruct((nb, N_HEADS, s, ATT), F32)],
        scratch_shapes=[pltpu.VMEM((2, s, BLK), MXU_DTYPE)] * 2, operands=(proj, proj, proj, ccol, crow))


def _fox_bwd(proj, dmixed, lse, ccol, crow, nb, s, name, carry=None):
    nblk = s // ATT

    def body(q_ref, k_ref, v_ref, do_ref, lse_ref, cc_ref, cr_ref, dq_ref, dk_ref, dv_ref, dc_ref, km, vm, p_scr, dp_scr):
        sels = _head_masks()
        _split_heads(k_ref, km, sels)
        _split_heads(v_ref, vm, sels)
        rows = _iota2((QT, ATT), 0)
        cols = _iota2((QT, ATT), 1)
        dk_ref[...] = jnp.zeros_like(dk_ref)
        dv_ref[...] = jnp.zeros_like(dv_ref)
        dc_ref[...] = jnp.zeros_like(dc_ref)

        def qblock(i, _):
            qi = (q_ref[_qrows(i), :] * 0.125).astype(MXU_DTYPE)
            doi = do_ref[_qrows(i), :].astype(MXU_DTYPE)
            qm = [jnp.where(sel, qi, 0.0) for sel in sels]
            dom = [jnp.where(sel, doi, 0.0) for sel in sels]
            ci = [cc_ref[h, _qrows(i), :] for h in range(2)]
            lsei = [lse_ref[h, _qrows(i), :] for h in range(2)]

            def probs(j, h):
                z = _dot_nt(qi, km[h, _rows(j), :]) + (ci[h] - cr_ref[h, j][0:1, :])
                p = jnp.where((cols + (j * ATT - i * QT)) <= rows, jnp.exp(z - lsei[h]), 0.0)
                return p, _dot_nt(doi, vm[h, _rows(j), :])

            def row_term(j, accs):
                out = []
                for h in range(2):
                    p, dp = probs(j, h)
                    p_scr[h, j] = p
                    dp_scr[h, j] = dp
                    out.append(accs[h] + jnp.sum(p * dp, axis=-1, keepdims=True))
                return tuple(out)

            di = lax.fori_loop(0, _keys_upto(i), row_term, (jnp.zeros((QT, 1), F32),) * 2)

            def kblock(j, dq):
                for h in range(2):
                    p = p_scr[h, j]
                    ds = p * (dp_scr[h, j] - di[h])
                    dc_ref[h, j] += jnp.broadcast_to(jnp.sum(ds, axis=0, keepdims=True), (8, ATT))
                    ds = ds.astype(MXU_DTYPE)
                    dk_ref[_rows(j), :] += _dot_tn(ds, qm[h])
                    dv_ref[_rows(j), :] += _dot_tn(p, dom[h])
                    dq = dq + _dot(ds, km[h, _rows(j), :])
                return dq

            dq = lax.fori_loop(0, _keys_upto(i), kblock, jnp.zeros((QT, BLK), F32))
            dq_ref[_qrows(i), :] = dq * 0.125
            return 0

        lax.fori_loop(0, s // QT, qblock, 0)

    crow_spec = pl.BlockSpec((None, 2, nblk, 8, ATT), lambda b, p: (b, p, 0, 0, 0))
    wide, cols_out = _pair_spec(s, ATT), _col_spec(s, 0)
    return _host_call(
        body, carry, name=name, grid=(nb, 2),
        in_specs=[_col_spec(s, 12), _col_spec(s, 14), _col_spec(s, 16), _col_spec(s, 4), wide, wide, crow_spec],
        out_specs=[cols_out, cols_out, cols_out, crow_spec],
        out_shape=[jax.ShapeDtypeStruct((nb * s, 2 * BLK), F32)] * 3 + [jax.ShapeDtypeStruct((nb, N_HEADS, nblk, 8, ATT), F32)],
        scratch_shapes=[pltpu.VMEM((2, s, BLK), MXU_DTYPE)] * 2 + [pltpu.VMEM((2, nblk, QT, ATT), F32)] * 2,
        operands=(proj, proj, proj, dmixed, lse, ccol, crow))


def _fox_gates_fwd(proj, f_bias, nb, s, name):
    chunk = 256

    def body(f_ref, b_ref, c_ref):
        lower = (_iota2((chunk, chunk), 0) >= _iota2((chunk, chunk), 1)).astype(MXU_DTYPE)
        carry = jnp.zeros((1, BLK), F32)
        for n in range(s // chunk):
            rows = pl.ds(n * chunk, chunk)
            lf, _, _ = _log_sigmoid_parts(f_ref[rows, :] + b_ref[0:1, :])
            c = _split_dot_lhs(lower, lf, 3) + carry
            c_ref[rows, :] = c
            carry = c[chunk - 1:chunk, :]

    return pl.pallas_call(
        body, name=name, grid=(nb,),
        in_specs=[pl.BlockSpec((s, BLK), lambda b: (b, (PROJ_PAD - BLK) // BLK)), pl.BlockSpec((8, BLK), lambda b: (0, 0))],
        out_specs=pl.BlockSpec((s, BLK), lambda b: (b, 0)),
        out_shape=jax.ShapeDtypeStruct((nb * s, BLK), F32), compiler_params=_cparams("parallel"),
    )(proj, f_bias)


def _fox_gates_bwd(dc, proj, f_bias, nb, s, name):
    chunk = 256

    def body(dc_ref, f_ref, b_ref, df_ref, db_ref):
        upper = (_iota2((chunk, chunk), 0) <= _iota2((chunk, chunk), 1)).astype(MXU_DTYPE)
        carry = jnp.zeros((1, BLK), F32)
        total = jnp.zeros((1, BLK), F32)
        for n in reversed(range(s // chunk)):
            rows = pl.ds(n * chunk, chunk)
            dlf = _split_dot_lhs(upper, dc_ref[rows, :], 3) + carry
            carry = dlf[0:1, :]
            pre = f_ref[rows, :] + b_ref[0:1, :]
            e = jnp.exp(-jnp.abs(pre))
            df = dlf * (jnp.where(pre >= 0.0, e, 1.0) / (1.0 + e))
            df_ref[rows, :] = df
            total = total + jnp.sum(df, axis=0, keepdims=True)

        @pl.when(pl.program_id(0) == 0)
        def _():
            db_ref[...] = jnp.zeros_like(db_ref)

        db_ref[0:1, :] += total

    return pl.pallas_call(
        body, name=name, grid=(nb,),
        in_specs=[pl.BlockSpec((s, BLK), lambda b: (b, 0)), pl.BlockSpec((s, BLK), lambda b: (b, (PROJ_PAD - BLK) // BLK)),
                  pl.BlockSpec((8, BLK), lambda b: (0, 0))],
        out_specs=[pl.BlockSpec((s, BLK), lambda b: (b, 0)), pl.BlockSpec((8, BLK), lambda b: (0, 0))],
        out_shape=[jax.ShapeDtypeStruct((nb * s, BLK), F32), jax.ShapeDtypeStruct((8, BLK), F32)],
        compiler_params=_cparams("arbitrary"),
    )(dc, proj, f_bias)


def _delta_kernel(dmixed, o, nb, s, name):
    def body(do_ref, o_ref, d_ref):
        prod = do_ref[...] * o_ref[...]
        for h, sel in enumerate(_head_masks()):
            d_ref[h] = jnp.broadcast_to(jnp.sum(jnp.where(sel, prod, 0.0), axis=-1, keepdims=True), (s, BLK))

    return pl.pallas_call(
        body, name=name, grid=(nb, 2), in_specs=[_col_spec(s, 2), _col_spec(s, 0)], out_specs=_pair_spec(s, BLK),
        out_shape=jax.ShapeDtypeStruct((nb, N_HEADS, s, BLK), F32), compiler_params=_cparams("parallel", "parallel"),
    )(dmixed, o)


def _t5_bucket_np(dist):
    max_exact = REL_BUCKETS // 2
    nf = np.maximum(dist, 1).astype(np.float32)
    large = max_exact + (np.log(nf / max_exact) / math.log(2048 / max_exact) * (REL_BUCKETS - max_exact)).astype(np.int32)
    large = np.minimum(large, REL_BUCKETS - 1)
    return np.where(dist < max_exact, dist, large)


def _bucket_table():
    qi = np.arange(BLK)[:, None]
    kj = np.arange(2 * BLK)[None, :]
    dist = qi + BLK - kj
    tables = []
    for window, dil in DIL_PATTERNS:
        in_band = (dist >= 0) & (dist <= window // dil)
        tables.append(np.where(in_band, _t5_bucket_np(np.maximum(dist, 0) * dil), -1).astype(np.int32))
    return np.stack(tables)


def _dil_scores(qb, kp, kc, b_ref, h, prev_valid):
    zp = _dot_nt(qb, kp) + b_ref[h, :, 0:BLK]
    zp = jnp.where(prev_valid, zp, NEG)
    zc = _dot_nt(qb, kc) + b_ref[h, :, BLK:2 * BLK]
    return zp, zc


def _residue_rows(b, seg, dil):
    if dil == 1:
        return _rows128(b), _rows128(jnp.maximum(b - 1, 0)), b > 0
    r, n = b // seg, b % seg
    cur = pl.ds(r + dil * n * BLK, BLK, stride=dil)
    prev = pl.ds(r + dil * jnp.maximum(n - 1, 0) * BLK, BLK, stride=dil)
    return cur, prev, n > 0


def _dil_attention_fwd(proj, bias, nb, s, name):
    nblk = s // BLK

    def body(q_ref, k_ref, v_ref, b_ref, out_ref, lse_ref, o_scr, l_scr):
        sels = _head_masks()
        for p, (_, dil) in enumerate(DIL_PATTERNS):
            seg = s // dil // BLK

            def block(b, _, p=p, seg=seg, dil=dil):
                cur, prev, has_prev = _residue_rows(b, seg, dil)
                qb = (q_ref[cur, :] * 0.125).astype(MXU_DTYPE)
                kp, kc = k_ref[prev, :].astype(MXU_DTYPE), k_ref[cur, :].astype(MXU_DTYPE)
                vp, vc = v_ref[prev, :].astype(MXU_DTYPE), v_ref[cur, :].astype(MXU_DTYPE)
                acc = jnp.zeros((BLK, BLK), F32)
                for h, sel in enumerate(sels):
                    zp, zc = _dil_scores(qb, jnp.where(sel, kp, 0.0), jnp.where(sel, kc, 0.0), b_ref.at[p], h, has_prev)
                    m = jnp.maximum(jnp.max(zp, axis=-1, keepdims=True), jnp.max(zc, axis=-1, keepdims=True))
                    pp = jnp.exp(zp - m)
                    pc = jnp.exp(zc - m)
                    den = jnp.sum(pp, axis=-1, keepdims=True) + jnp.sum(pc, axis=-1, keepdims=True)
                    acc = acc + (_dot(pp, jnp.where(sel, vp, 0.0)) + _dot(pc, jnp.where(sel, vc, 0.0))) / den
                    l_scr[p, h, cur, :] = jnp.broadcast_to(m + jnp.log(den), (BLK, BLK))
                o_scr[p, cur, :] = acc
                return 0

            lax.fori_loop(0, nblk, block, 0, unroll=2)

        weights, dens = [], []
        for h in range(2):
            m = jnp.maximum(jnp.maximum(l_scr[0, h], l_scr[1, h]), l_scr[2, h])
            w = [jnp.exp(l_scr[p, h] - m) for p in range(3)]
            den = w[0] + w[1] + w[2]
            lse_ref[h] = m + jnp.log(den)
            weights.append(w)
            dens.append(den)
        num = sum(jnp.where(sels[0], weights[0][p], weights[1][p]) * o_scr[p] for p in range(3))
        out_ref[...] = num / jnp.where(sels[0], dens[0], dens[1])

    bias_spec = pl.BlockSpec((3, 2, BLK, 2 * BLK), lambda b, p: (0, p, 0, 0))
    return pl.pallas_call(
        body, name=name, grid=(nb, 2), in_specs=[_col_spec(s, 6), _col_spec(s, 8), _col_spec(s, 10), bias_spec],
        out_specs=[_col_spec(s, 0), _pair_spec(s, BLK)],
        out_shape=[jax.ShapeDtypeStruct((nb * s, 2 * BLK), F32), jax.ShapeDtypeStruct((nb, N_HEADS, s, BLK), F32)],
        scratch_shapes=[pltpu.VMEM((3, s, BLK), F32), pltpu.VMEM((3, 2, s, BLK), F32)],
        compiler_params=_cparams("parallel", "parallel"),
    )(proj, proj, proj, bias)


def _dil_attention_bwd(proj, dmixed, lse, delta, bias, nb, s, name):
    nblk = s // BLK

    def body(q_ref, k_ref, v_ref, do_ref, lse_ref, dl_ref, b_ref, dq_ref, dk_ref, dv_ref, g_ref):
        sels = _head_masks()
        dq_ref[...] = jnp.zeros_like(dq_ref)
        dk_ref[...] = jnp.zeros_like(dk_ref)
        dv_ref[...] = jnp.zeros_like(dv_ref)
        g_ref[...] = jnp.zeros_like(g_ref)
        for p, (_, dil) in enumerate(DIL_PATTERNS):
            seg = s // dil // BLK

            def block(b, _, p=p, seg=seg, dil=dil):
                cur, prev, has_prev = _residue_rows(b, seg, dil)
                qb = (q_ref[cur, :] * 0.125).astype(MXU_DTYPE)
                dob = do_ref[cur, :].astype(MXU_DTYPE)
                kp, kc = k_ref[prev, :].astype(MXU_DTYPE), k_ref[cur, :].astype(MXU_DTYPE)
                vp, vc = v_ref[prev, :].astype(MXU_DTYPE), v_ref[cur, :].astype(MXU_DTYPE)
                dq = jnp.zeros((BLK, BLK), F32)
                dkp, dkc, dvp, dvc = dq, dq, dq, dq
                for h, sel in enumerate(sels):
                    kph, kch = jnp.where(sel, kp, 0.0), jnp.where(sel, kc, 0.0)
                    qh, doh = jnp.where(sel, qb, 0.0), jnp.where(sel, dob, 0.0)
                    lse_h = lse_ref[h, cur, :]
                    dlt = dl_ref[h, cur, :]
                    zp, zc = _dil_scores(qb, kph, kch, b_ref.at[p], h, has_prev)
                    pp = jnp.exp(zp - lse_h)
                    pc = jnp.exp(zc - lse_h)
                    dsp = pp * (_dot_nt(dob, jnp.where(sel, vp, 0.0)) - dlt)
                    dsc = pc * (_dot_nt(dob, jnp.where(sel, vc, 0.0)) - dlt)
                    g_ref[h, p, :, 0:BLK] += dsp
                    g_ref[h, p, :, BLK:2 * BLK] += dsc
                    dsp = dsp.astype(MXU_DTYPE)
                    dsc = dsc.astype(MXU_DTYPE)
                    dq = dq + _dot(dsp, kph) + _dot(dsc, kch)
                    dkp, dkc = dkp + _dot_tn(dsp, qh), dkc + _dot_tn(dsc, qh)
                    dvp, dvc = dvp + _dot_tn(pp, doh), dvc + _dot_tn(pc, doh)
                dq_ref[cur, :] += dq * 0.125
                dk_ref[prev, :] += dkp
                dk_ref[cur, :] += dkc
                dv_ref[prev, :] += dvp
                dv_ref[cur, :] += dvc
                return 0

            lax.fori_loop(0, nblk, block, 0, unroll=2)

    bias_spec = pl.BlockSpec((3, 2, BLK, 2 * BLK), lambda b, p: (0, p, 0, 0))
    cols, stats = _col_spec(s, 0), _pair_spec(s, BLK)
    return pl.pallas_call(
        body, name=name, grid=(nb, 2),
        in_specs=[_col_spec(s, 6), _col_spec(s, 8), _col_spec(s, 10), _col_spec(s, 2), stats, stats, bias_spec],
        out_specs=[cols, cols, cols, pl.BlockSpec((None, 2, 3, BLK, 2 * BLK), lambda b, p: (b, p, 0, 0, 0))],
        out_shape=[jax.ShapeDtypeStruct((nb * s, 2 * BLK), F32)] * 3 + [jax.ShapeDtypeStruct((nb, N_HEADS, 3, BLK, 2 * BLK), F32)],
        compiler_params=_cparams("parallel", "parallel"),
    )(proj, proj, proj, dmixed, lse, delta, bias)


def _bucket_reduce(gbias, table, name):
    nb = gbias.shape[0]

    def body(g_ref, t_ref, o_ref):
        row = _iota2((8, BLK), 0)
        lane = _iota2((8, BLK), 1)
        gsum = [[sum(g_ref[b, h, p] for b in range(nb)) for p in range(3)] for h in range(N_HEADS)]

        def bucket(k, acc):
            for h in range(N_HEADS):
                tot = sum(jnp.sum(jnp.where(t_ref[p] == k, gsum[h][p], 0.0)) for p in range(3))
                acc = acc + jnp.where((row == h) & (lane == k), tot, 0.0)
            return acc

        o_ref[...] = lax.fori_loop(0, REL_BUCKETS, bucket, jnp.zeros((8, BLK), F32))

    vm = pl.BlockSpec(memory_space=pltpu.VMEM)
    return pl.pallas_call(
        body, name=name, in_specs=[vm, vm], out_specs=vm, out_shape=jax.ShapeDtypeStruct((8, BLK), F32),
        compiler_params=pltpu.CompilerParams(vmem_limit_bytes=VMEM_LIMIT),
    )(gbias, table)


def _place():
    x, y, c = lax.axis_index("x"), lax.axis_index("y"), lax.axis_index("c")
    others = [(1 - x, y), (x, 1 - y), (1 - x, 1 - y)]
    return x, y, c, others


def _remote(src, dst, send_sem, recv_sem, to):
    return pltpu.make_async_remote_copy(src_ref=src, dst_ref=dst, send_sem=send_sem, recv_sem=recv_sem,
                                        device_id=to, device_id_type=MESH)


_HBM = pl.BlockSpec(memory_space=pl.ANY)


class _Exchange:
    def __init__(self, operands, out_shape, n_copies, copies, aliases=None):
        self.operands, self.out_shape, self.n_copies, self.copies = list(operands), list(out_shape), n_copies, copies
        self.aliases = dict(aliases or {})

    def sem_shapes(self):
        return [pltpu.SemaphoreType.DMA((self.n_copies,)), pltpu.SemaphoreType.DMA((self.n_copies,))]


def _start_all(sends):
    for cp in sends:
        cp.start()


def _wait_all(sends, arrivals):
    for cp in arrivals:
        cp.wait_recv()
    for cp in sends:
        cp.wait_send()


def _run_exchange(ex, name):
    ni = len(ex.operands)

    def body(*refs):
        sends, arrivals = ex.copies(refs[:ni], refs[ni:-2], refs[-2], refs[-1])
        _start_all(sends)
        _wait_all(sends, arrivals)

    return list(pl.pallas_call(
        body, name=name, in_specs=[_HBM] * ni, out_specs=[_HBM] * len(ex.out_shape), out_shape=ex.out_shape,
        scratch_shapes=ex.sem_shapes(), input_output_aliases=ex.aliases)(*ex.operands))


def _host_call(body, carry, *, name, grid, in_specs, out_specs, out_shape, operands, scratch_shapes=()):
    in_specs, out_specs, out_shape, scratch_shapes = list(in_specs), list(out_specs), list(out_shape), list(scratch_shapes)
    if carry is None:
        res = pl.pallas_call(body, name=name, grid=grid, in_specs=in_specs, out_specs=out_specs, out_shape=out_shape,
                             scratch_shapes=scratch_shapes, compiler_params=_cparams(*["parallel"] * len(grid)))(*operands)
        return list(res), []
    n_in, n_out, n_scr, c_in, c_out = len(in_specs), len(out_specs), len(scratch_shapes), len(carry.operands), len(carry.out_shape)
    steps = math.prod(grid)

    def wrapped(*refs):
        ins, refs = refs[:n_in], refs[n_in:]
        c_ins, refs = refs[:c_in], refs[c_in:]
        outs, refs = refs[:n_out], refs[n_out:]
        c_outs, refs = refs[:c_out], refs[c_out:]
        scr, (send_sems, recv_sems) = refs[:n_scr], refs[n_scr:]
        step = 0
        for d, size in enumerate(grid):
            step = step * size + pl.program_id(d)

        @pl.when(step == 0)
        def _():
            _start_all(carry.copies(c_ins, c_outs, send_sems, recv_sems)[0])

        body(*ins, *outs, *scr)

        @pl.when(step == steps - 1)
        def _():
            _wait_all(*carry.copies(c_ins, c_outs, send_sems, recv_sems))

    res = pl.pallas_call(
        wrapped, name=name, grid=grid, in_specs=in_specs + [_HBM] * c_in, out_specs=out_specs + [_HBM] * c_out,
        out_shape=out_shape + carry.out_shape, scratch_shapes=scratch_shapes + carry.sem_shapes(),
        input_output_aliases={n_in + i: n_out + j for i, j in carry.aliases.items()},
        compiler_params=_cparams(*["arbitrary"] * len(grid)))(*operands, *carry.operands)
    return list(res[:n_out]), list(res[n_out:])


def _half(which, rows):
    h = rows // 2
    return pl.ds(pl.multiple_of(which * h, 16), h)


def _like(arrays, shape_of=lambda t: t.shape):
    return [jax.ShapeDtypeStruct(shape_of(t), t.dtype) for t in arrays]


def _gather_ici(shards, layer):
    n = len(shards)

    def copies(ins, outs, send_sems, recv_sems):
        x, y, c, others = _place()
        me = 2 * x + y
        sends, arrivals = [], []
        for a in range(n):
            rows = _half(c, shards[a].shape[1])
            for k, (ox, oy) in enumerate(others):
                sems = (send_sems.at[3 * a + k], recv_sems.at[3 * a + k], (ox, oy, c))
                sends.append(_remote(ins[a].at[layer, rows], outs[a].at[me, rows], *sems))
                landed = outs[a].at[2 * ox + oy, rows]
                arrivals.append(_remote(landed, landed, *sems))
        return sends, arrivals

    return _Exchange(shards, _like(shards, lambda t: (N_CHIPS,) + t.shape[1:]), 3 * n, copies)


def _gather_d2d(gathered):
    n = len(gathered)

    def copies(ins, outs, send_sems, recv_sems):
        x, y, c, others = _place()
        sends, arrivals = [], []
        for a in range(n):
            r = gathered[a].shape[1]
            for k, (ox, oy) in enumerate(others):
                sems = (send_sems.at[3 * a + k], recv_sems.at[3 * a + k], (x, y, 1 - c))
                mine, theirs = outs[a].at[2 * ox + oy, _half(c, r)], outs[a].at[2 * ox + oy, _half(1 - c, r)]
                sends.append(_remote(mine, mine, *sems))
                arrivals.append(_remote(theirs, theirs, *sems))
        return sends, arrivals

    return _Exchange(gathered, _like(gathered), 3 * n, copies, aliases={a: a for a in range(n)})


def _swap_halves(g):
    n = len(g)

    def copies(ins, outs, send_sems, recv_sems):
        x, y, c, _ = _place()
        sends, arrivals = [], []
        for a in range(n):
            sems = (send_sems.at[a], recv_sems.at[a], (x, y, 1 - c))
            sends.append(_remote(ins[a].at[:, _half(1 - c, g[a].shape[1])], outs[a], *sems))
            arrivals.append(_remote(outs[a], outs[a], *sems))
        return sends, arrivals

    return _Exchange(g, _like(g, lambda t: (t.shape[0], t.shape[1] // 2, t.shape[2])), n, copies)


def _scatter_shards(ps):
    n = len(ps)

    def copies(ins, outs, send_sems, recv_sems):
        x, y, c, others = _place()
        me = 2 * x + y
        sends, arrivals = [], []
        for a in range(n):
            for k, (ox, oy) in enumerate(others):
                sems = (send_sems.at[3 * a + k], recv_sems.at[3 * a + k], (ox, oy, c))
                sends.append(_remote(ins[a].at[2 * ox + oy], outs[a].at[me], *sems))
                slot = outs[a].at[2 * ox + oy]
                arrivals.append(_remote(slot, slot, *sems))
        return sends, arrivals

    return _Exchange(ps, _like(ps), 3 * n, copies)


def _share_halves(mine):
    n = len(mine)

    def copies(ins, outs, send_sems, recv_sems):
        x, y, c, _ = _place()
        sends, arrivals = [], []
        for a in range(n):
            sems = (send_sems.at[a], recv_sems.at[a], (x, y, 1 - c))
            sends.append(_remote(ins[a], outs[a], *sems))
            arrivals.append(_remote(outs[a], outs[a], *sems))
        return sends, arrivals

    return _Exchange(mine, _like(mine), n, copies)


def _row_tile(r):
    for cand in (256, 352, 128):
        if r % cand == 0:
            return cand
    return r


def _pair_sum(g, other, core, name):
    ns, h, w = other.shape
    tr = _row_tile(h)
    per_half = h // tr

    def body(core_ref, g_ref, o_ref, out_ref):
        out_ref[...] = (g_ref[...] + o_ref[...]).astype(out_ref.dtype)

    blk = pl.BlockSpec((None, tr, w), lambda k, i, core_ref: (k, i, 0))
    grid_spec = pltpu.PrefetchScalarGridSpec(
        num_scalar_prefetch=1, grid=(ns, per_half),
        in_specs=[pl.BlockSpec((None, tr, w), lambda k, i, core_ref: (k, core_ref[0] * per_half + i, 0)), blk], out_specs=blk)
    return pl.pallas_call(
        body, name=name, grid_spec=grid_spec, out_shape=jax.ShapeDtypeStruct((ns, h, w), MXU_DTYPE),
        compiler_params=_cparams("parallel", "parallel"),
    )(core.reshape(1).astype(jnp.int32), g, other)


def _chip_sum(q, p, chip, name):
    ns, r, w = q.shape
    tr = _row_tile(r)

    def body(chip_ref, q_ref, own_ref, out_ref):
        me = chip_ref[0]
        own = own_ref[...].astype(F32)
        terms = [jnp.where(me == k, own, q_ref[k].astype(F32)) for k in range(ns)]
        out_ref[...] = ((terms[0] + terms[1]) + terms[2]) + terms[3]

    grid_spec = pltpu.PrefetchScalarGridSpec(
        num_scalar_prefetch=1, grid=(r // tr,),
        in_specs=[pl.BlockSpec((ns, tr, w), lambda i, chip_ref: (0, i, 0)),
                  pl.BlockSpec((None, tr, w), lambda i, chip_ref: (chip_ref[0], i, 0))],
        out_specs=pl.BlockSpec((tr, w), lambda i, chip_ref: (i, 0)))
    return pl.pallas_call(
        body, name=name, grid_spec=grid_spec, out_shape=jax.ShapeDtypeStruct((r, w), F32),
        compiler_params=_cparams("parallel"),
    )(chip.reshape(1).astype(jnp.int32), q, p)


class _WeightPrefetch:
    def __init__(self, shards, layer, chip):
        self.shards, self.layer, self.chip, self.result = shards, layer, chip, None

    def first(self):
        return _gather_ici(self.shards, self.layer)

    def got_first(self, arrived):
        self.arrived = arrived

    def second(self):
        return _gather_d2d(self.arrived)

    def got_second(self, gathered):
        self.result = [lax.dynamic_update_index_in_dim(got, own[self.layer], self.chip, 0)
                       for got, own in zip(gathered, self.shards)]

    def run(self, tag):
        self.got_first(_run_exchange(self.first(), f"gather_ici_{tag}"))
        self.got_second(_run_exchange(self.second(), f"gather_d2d_{tag}"))
        return self.result


class _GradReduce:
    def __init__(self, g, chip, core, tag):
        self.g, self.chip, self.core, self.tag, self.result = g, chip, core, tag, None

    def swap(self):
        return _swap_halves(self.g)

    def got_swap(self, theirs):
        self.pair = [_pair_sum(g, t, self.core, f"pair_sum_{name}_{self.tag}") for name, g, t in zip(_BIG, self.g, theirs)]

    def scatter(self):
        return _scatter_shards(self.pair)

    def got_scatter(self, q):
        self.mine = [_chip_sum(qa, pa, self.chip, f"chip_sum_{name}_{self.tag}") for name, qa, pa in zip(_BIG, q, self.pair)]

    def share(self):
        return _share_halves(self.mine)

    def got_share(self, theirs):
        self.result = [jnp.where(self.core == 0, jnp.concatenate([a, b]), jnp.concatenate([b, a]))
                       for a, b in zip(self.mine, theirs)]

    def run(self):
        self.got_swap(_run_exchange(self.swap(), f"swap_halves_{self.tag}"))
        self.got_scatter(_run_exchange(self.scatter(), f"scatter_shards_{self.tag}"))
        self.got_share(_run_exchange(self.share(), f"share_halves_{self.tag}"))
        return self.result


def _gather_small(pk, name):
    rows, w = pk.shape

    def body(pk_ref, all_ref, sum_ref, send_sems, recv_sems):
        x, y, c, _ = _place()
        me = 4 * x + 2 * y + c
        all_ref[me] = pk_ref[...]
        flips = [(fx, fy, fc) for fx in (0, 1) for fy in (0, 1) for fc in (0, 1)][1:]
        peers = [(x ^ fx, y ^ fy, c ^ fc) for fx, fy, fc in flips]
        sends = [_remote(pk_ref, all_ref.at[me], send_sems.at[k], recv_sems.at[k], peer) for k, peer in enumerate(peers)]
        for cp in sends:
            cp.start()
        for k, (px, py, pc) in enumerate(peers):
            slot = all_ref.at[4 * px + 2 * py + pc]
            _remote(slot, slot, send_sems.at[k], recv_sems.at[k], (px, py, pc)).wait_recv()
        for cp in sends:
            cp.wait_send()
        total = all_ref[0]
        for d in range(1, N_DEV):
            total = total + all_ref[d]
        sum_ref[...] = total

    vm = pl.BlockSpec(memory_space=pltpu.VMEM)
    return pl.pallas_call(
        body, name=name, in_specs=[vm], out_specs=[vm, vm],
        out_shape=[jax.ShapeDtypeStruct((N_DEV, rows, w), F32), jax.ShapeDtypeStruct((rows, w), F32)],
        scratch_shapes=[pltpu.SemaphoreType.DMA((7,)), pltpu.SemaphoreType.DMA((7,))],
    )(pk)


def _row_layout(c, nb, s):
    ch = jnp.swapaxes(c[:, :N_HEADS].reshape(nb, s, N_HEADS), 1, 2)
    ccol = jnp.broadcast_to(ch[..., None], (nb, N_HEADS, s, ATT))
    crow = jnp.broadcast_to(ch.reshape(nb, N_HEADS, s // ATT, 1, ATT), (nb, N_HEADS, s // ATT, 8, ATT))
    return ccol, crow


def _dil_bias(rel_bias, name):
    def body(rel_ref, t_ref, o_ref):
        for p in range(len(DIL_PATTERNS)):
            table = t_ref[p]

            def bucket(k, accs, table=table):
                return tuple(jnp.where(table == k, rel_ref[k, h], acc) for h, acc in enumerate(accs))

            accs = lax.fori_loop(0, REL_BUCKETS, bucket, tuple(jnp.full((BLK, 2 * BLK), NEG, F32) for _ in range(N_HEADS)))
            for h in range(N_HEADS):
                o_ref[p, h] = accs[h]

    vm = pl.BlockSpec(memory_space=pltpu.VMEM)
    return pl.pallas_call(
        body, name=name, in_specs=[pl.BlockSpec(memory_space=pltpu.SMEM), vm], out_specs=vm,
        out_shape=jax.ShapeDtypeStruct((len(DIL_PATTERNS), N_HEADS, BLK, 2 * BLK), F32),
        compiler_params=pltpu.CompilerParams(vmem_limit_bytes=VMEM_LIMIT),
    )(rel_bias, jnp.asarray(_bucket_table()))


def _layer_forward(x, wts, small, nb, s, tag, prefetch=None):
    proj = _matmul(x, wts["w_in"], "proj", tag)

    o_sb, carried = _sb_fwd(proj, nb, s, f"sb_fwd_{tag}", prefetch and prefetch.first())
    if prefetch:
        prefetch.got_first(carried)

    bias = _dil_bias(small["rel_bias"], f"dil_bias_{tag}")
    o_dl, lse_dl = _dil_attention_fwd(proj, bias, nb, s, f"dil_fwd_{tag}")

    fb = jnp.zeros((8, BLK), F32).at[0, :N_HEADS].set(small["f_bias"])
    csum = _fox_gates_fwd(proj, fb, nb, s, f"fox_gates_{tag}")
    ccol, crow = _row_layout(csum, nb, s)
    (o_fx, lse_fx), carried = _fox_fwd(proj, ccol, crow, nb, s, f"fox_fwd_{tag}", prefetch and prefetch.second())
    if prefetch:
        prefetch.got_second(carried)

    cw = jnp.zeros((8, CONV_W), F32).at[:3].set(small["conv_w"])
    o_cv = _conv_fwd(proj, cw, nb, s, f"conv_fwd_{tag}")

    mixed = jnp.concatenate([o_sb, o_dl, o_fx, o_cv], axis=-1).astype(MXU_DTYPE)
    mix = _matmul(mixed, wts["w_out"], "out_proj", tag)
    pre1, x1 = _ln_fwd(x, mix, small["ln1_g"], small["ln1_b"], f"ln1_fwd_{tag}")
    gu = _matmul(x1, wts["w_gu"], "ffn_in", tag)
    hid = _swiglu_fwd(gu, f"swiglu_fwd_{tag}")
    ffn = _matmul(hid, wts["w_down"], "ffn_out", tag)
    pre2, x2 = _ln_fwd(x1, ffn, small["ln2_g"], small["ln2_b"], f"ln2_fwd_{tag}")
    saved = dict(x=x, proj=proj, bias=bias, o_dl=o_dl, lse_dl=lse_dl, fb=fb, ccol=ccol, crow=crow,
                 o_fx=o_fx, lse_fx=lse_fx, cw=cw, mixed=mixed, pre1=pre1, x1=x1, gu=gu, hid=hid, pre2=pre2)
    return x2, saved


def _layer_backward(dx2, sv, wts, small, nb, s, tag, reduce=None):
    t = nb * s
    dpre2, dgb2 = _ln_bwd(dx2, sv["pre2"], small["ln2_g"], f"ln2_bwd_{tag}")
    dpre2_b = dpre2.astype(MXU_DTYPE)
    dhid = _matmul(dpre2_b, wts["w_down"], "ffn_out_dx", tag, trans_b=True)
    dw_down = _matmul(sv["hid"], dpre2_b, "ffn_out_dw", tag, trans_a=True)
    (dgu,), carried = _swiglu_bwd(dhid, sv["gu"], f"swiglu_bwd_{tag}", reduce and reduce.swap())
    if reduce:
        reduce.got_swap(carried)
    dx1 = _matmul(dgu, wts["w_gu"], "ffn_in_dx", tag, add=dpre2, add_scale=ALPHA, trans_b=True)
    dw_gu = _matmul(sv["x1"].astype(MXU_DTYPE), dgu, "ffn_in_dw", tag, trans_a=True)

    dpre1, dgb1 = _ln_bwd(dx1, sv["pre1"], small["ln1_g"], f"ln1_bwd_{tag}")
    dpre1_b = dpre1.astype(MXU_DTYPE)
    dmixed = _matmul(dpre1_b, wts["w_out"], "out_proj_dx", tag, trans_b=True)
    dw_out = _matmul(sv["mixed"], dpre1_b, "out_proj_dw", tag, trans_a=True)
    proj = sv["proj"]

    (dq_sb, dk_sb, dv_sb), carried = _sb_bwd(proj, dmixed, nb, s, f"sb_bwd_{tag}", reduce and reduce.scatter())
    if reduce:
        reduce.got_scatter(carried)

    delta_dl = _delta_kernel(dmixed, sv["o_dl"], nb, s, f"dil_delta_{tag}")
    dq_dl, dk_dl, dv_dl, gbias = _dil_attention_bwd(proj, dmixed, sv["lse_dl"], delta_dl, sv["bias"], nb, s, f"dil_bwd_{tag}")
    drel = _bucket_reduce(gbias, jnp.asarray(_bucket_table()), f"rel_bias_grad_{tag}")

    (dq_fx, dk_fx, dv_fx, dcol), carried = _fox_bwd(proj, dmixed, sv["lse_fx"], sv["ccol"], sv["crow"], nb, s,
                                                    f"fox_bwd_{tag}", reduce and reduce.share())
    if reduce:
        reduce.got_share(carried)
    dcs = -jnp.swapaxes(dcol[:, :, :, 0, :].reshape(nb, N_HEADS, s), 1, 2).reshape(t, N_HEADS)
    dcs = jnp.pad(dcs, ((0, 0), (0, BLK - N_HEADS)))
    dfx, dfb = _fox_gates_bwd(dcs, proj, sv["fb"], nb, s, f"fox_gates_bwd_{tag}")

    dgates, dcw = _conv_bwd(dmixed, proj, sv["cw"], nb, s, f"conv_bwd_{tag}")

    dproj = jnp.concatenate([dq_sb, dk_sb, dv_sb, dq_dl, dk_dl, dv_dl, dq_fx, dk_fx, dv_fx, dgates, dfx],
                            axis=-1).astype(MXU_DTYPE)
    dx = _matmul(dproj, wts["w_in"], "proj_dx", tag, add=dpre1, add_scale=ALPHA, trans_b=True)
    dw_in = _matmul(sv["x"].astype(MXU_DTYPE), dproj, "proj_dw", tag, trans_a=True)

    grads = dict(w_in=dw_in[:, :PROJ], w_out=dw_out, w_gate=dw_gu[:, :D_FF], w_up=dw_gu[:, D_FF:], w_down=dw_down,
                 ln1_g=dgb1[0], ln1_b=dgb1[1], ln2_g=dgb2[0], ln2_b=dgb2[1], conv_w=dcw[:3], f_bias=dfb[0, :N_HEADS],
                 rel_bias=drel[:N_HEADS, :REL_BUCKETS].T)
    return dx, grads


def _local_step(x, target, weights_of, small_all, prefetch=None, make_reduce=None):
    nb, s, d = x.shape
    h = x.reshape(nb * s, d)
    saved = []
    for layer in range(DEPTH):
        wts = weights_of(layer)
        ahead = prefetch[layer + 1] if prefetch and layer + 1 < DEPTH else None
        h, sv = _layer_forward(h, wts, small_all[layer], nb, s, f"l{layer}", ahead)
        saved.append((sv, wts))
    dy, lossp = _loss_kernel(h, target.reshape(nb * s, d), "loss")
    grads, reduces, pending = [None] * DEPTH, [None] * DEPTH, None
    for layer in reversed(range(DEPTH)):
        sv, wts = saved[layer]
        dy, grads[layer] = _layer_backward(dy, sv, wts, small_all[layer], nb, s, f"l{layer}", pending)
        pending = reduces[layer] = make_reduce(layer, grads[layer]) if make_reduce else None
    if pending:
        pending.run()
    return lossp, dy.reshape(nb, s, d), grads, reduces


_BIG = ("w_in", "w_out", "w_gate", "w_up", "w_down")
_COL_SHARDED = ("w_in", "w_gate", "w_up")


def _full_weights(gathered):
    cols = lambda t: jnp.swapaxes(t, 0, 1).reshape(t.shape[1], -1)
    rows = lambda t: t.reshape(-1, t.shape[2])
    w_in = jnp.pad(cols(gathered["w_in"]), ((0, 0), (0, PROJ_PAD - PROJ)))
    w_gu = jnp.concatenate([cols(gathered["w_gate"]), cols(gathered["w_up"])], axis=-1)
    return dict(w_in=w_in, w_out=rows(gathered["w_out"]), w_gu=w_gu, w_down=rows(gathered["w_down"]))


def _by_chip(name, g):
    if name in _COL_SHARDED:
        return jnp.swapaxes(g.reshape(g.shape[0], N_CHIPS, -1), 0, 1)
    return g.reshape(N_CHIPS, -1, g.shape[1])


_SMALL_LAYOUT = (("ln1_g", 0), ("ln1_b", 2), ("ln2_g", 4), ("ln2_b", 6), ("conv_w", 8))
_ROW_MISC = 10
_ROW_LOSS = 11


def _pack_small(per_layer, rel_bias, loss=None):
    pk = jnp.zeros((SMALL_ROWS, D_MODEL), F32)
    for name, row in _SMALL_LAYOUT:
        for l in range(DEPTH):
            v = per_layer[l][name].reshape(-1)
            pk = pk.at[row + l, :v.shape[0]].set(v)
    fb = jnp.concatenate([per_layer[l]["f_bias"] for l in range(DEPTH)])
    pk = pk.at[_ROW_MISC, :2 * N_HEADS].set(fb)
    pk = pk.at[_ROW_MISC, BLK:BLK + REL_BUCKETS * N_HEADS].set(rel_bias.reshape(-1))
    if loss is not None:
        pk = pk.at[_ROW_LOSS, 0].set(loss)
    return pk


def _unpack_small(pk, conv_cols):
    out = {}
    for name, row in _SMALL_LAYOUT:
        n = 3 * conv_cols if name == "conv_w" else D_MODEL
        v = pk[row:row + DEPTH, :n]
        out[name] = v.reshape(DEPTH, 3, conv_cols) if name == "conv_w" else v
    out["f_bias"] = pk[_ROW_MISC, :2 * N_HEADS].reshape(DEPTH, N_HEADS)
    out["rel_bias"] = pk[_ROW_MISC, BLK:BLK + REL_BUCKETS * N_HEADS].reshape(REL_BUCKETS, N_HEADS)
    return out


_WEIGHTS = ("w_in", "f_bias", "conv_w", "w_out", "rel_bias", "ln1_g", "ln1_b", "w_gate", "w_up", "w_down", "ln2_g", "ln2_b")


def kernel(x, w_in, f_bias, conv_w, w_out, rel_bias, ln1_g, ln1_b, w_gate, w_up, w_down, ln2_g, ln2_b, loss_target, m_w_in, m_f_bias, m_conv_w, m_w_out, m_rel_bias, m_ln1_g, m_ln1_b, m_w_gate, m_w_up, m_w_down, m_ln2_g, m_ln2_b, v_w_in, v_f_bias, v_conv_w, v_w_out, v_rel_bias, v_ln1_g, v_ln1_b, v_w_gate, v_w_up, v_w_down, v_ln2_g, v_ln2_b):
    w = dict(w_in=w_in, f_bias=f_bias, conv_w=conv_w, w_out=w_out, rel_bias=rel_bias, ln1_g=ln1_g, ln1_b=ln1_b,
             w_gate=w_gate, w_up=w_up, w_down=w_down, ln2_g=ln2_g, ln2_b=ln2_b)
    m = dict(w_in=m_w_in, f_bias=m_f_bias, conv_w=m_conv_w, w_out=m_w_out, rel_bias=m_rel_bias, ln1_g=m_ln1_g,
             ln1_b=m_ln1_b, w_gate=m_w_gate, w_up=m_w_up, w_down=m_w_down, ln2_g=m_ln2_g, ln2_b=m_ln2_b)
    v = dict(w_in=v_w_in, f_bias=v_f_bias, conv_w=v_conv_w, w_out=v_w_out, rel_bias=v_rel_bias, ln1_g=v_ln1_g,
             ln1_b=v_ln1_b, w_gate=v_w_gate, w_up=v_w_up, w_down=v_w_down, ln2_g=v_ln2_g, ln2_b=v_ln2_b)
    chip = 2 * lax.axis_index("x") + lax.axis_index("y")
    core = lax.axis_index("c")
    conv_shard = CONV_W // N_CHIPS

    shards = [w[name].astype(MXU_DTYPE) for name in _BIG]
    fetch = [_WeightPrefetch(shards, l, chip) for l in range(DEPTH)]
    fetch[0].run("l0")
    cw_pk = jnp.zeros((8, D_MODEL), F32).at[0, :DEPTH * 3 * conv_shard].set(conv_w.reshape(-1))
    cw_all, _ = _gather_small(cw_pk, "gather_conv_w")
    cw_chips = cw_all[0::2, 0, :DEPTH * 3 * conv_shard].reshape(N_CHIPS, DEPTH, 3, conv_shard)
    conv_full = jnp.moveaxis(cw_chips, 0, 2).reshape(DEPTH, 3, CONV_W)
    small_all = [dict(f_bias=f_bias[l], conv_w=conv_full[l], rel_bias=rel_bias, ln1_g=ln1_g[l], ln1_b=ln1_b[l],
                      ln2_g=ln2_g[l], ln2_b=ln2_b[l]) for l in range(DEPTH)]

    lossp, grad_x, grads, reduces = _local_step(
        x, loss_target, lambda l: _full_weights(dict(zip(_BIG, fetch[l].result))), small_all, fetch,
        lambda l, g: _GradReduce([_by_chip(name, g[name]) for name in _BIG], chip, core, f"l{l}"))
    big_g = {name: jnp.stack([reduces[l].result[a] for l in range(DEPTH)]) for a, name in enumerate(_BIG)}

    drel = grads[0]["rel_bias"] + grads[1]["rel_bias"]
    small_pk = _pack_small(grads, drel, lossp[0, 0])
    _, small_sum = _gather_small(small_pk, "gather_small_grads")
    loss = small_sum[_ROW_LOSS, 0]
    small_g = _unpack_small(small_sum, CONV_W)
    small_g["conv_w"] = lax.dynamic_slice_in_dim(small_g["conv_w"], chip * conv_shard, conv_shard, axis=2)

    out_g, out_d, out_m, out_v = dict(small_g), {}, {}, {}
    for name in _BIG:
        out_g[name] = big_g[name]
        out_d[name], out_m[name], out_v[name] = _adamw(w[name], big_g[name], m[name], v[name], f"adamw_{name}")
    per_layer = lambda src: [{name: src[name][l] for name in ("ln1_g", "ln1_b", "ln2_g", "ln2_b", "conv_w", "f_bias")}
                             for l in range(DEPTH)]
    packs = [_pack_small(per_layer(src), src["rel_bias"])[None] for src in (w, small_g, m, v)]
    for dst, pk in zip((out_d, out_m, out_v), _adamw(*packs, "adamw_small")):
        dst.update(_unpack_small(pk[0], conv_shard))

    return (loss, grad_x, *[out_g[n] for n in _WEIGHTS], *[out_d[n] for n in _WEIGHTS],
            *[out_m[n] for n in _WEIGHTS], *[out_v[n] for n in _WEIGHTS])
```

```python
import functools
import math

import numpy as np
import jax
import jax.numpy as jnp
from jax import lax
from jax.experimental import pallas as pl
from jax.experimental.pallas import tpu as pltpu

F32 = jnp.float32
BF16 = jnp.bfloat16
MXU_DTYPE = BF16

D_MODEL = 1024
HEAD_DIM = 64
N_HEADS = 4
BLK = 128
ATT = 256
QT = 512
CONV_W = 256
PROJ = 3076
PROJ_PAD = 3200
D_FF = 2816
DEPTH = 2
ALPHA = (2 * DEPTH) ** 0.25
LN_EPS = 1e-5
NEG = -1e30
DIL_PATTERNS = ((128, 1), (512, 4), (2048, 16))
REL_BUCKETS = 32
N_CHIPS = 4
N_DEV = 8
SMALL_ROWS = 16

ADAM_LR = 0.001
ADAM_B1 = 0.9
ADAM_B2 = 0.999
ADAM_EPS = 1e-08
ADAM_WD = 0.01
ADAM_STEP = 10

VMEM_LIMIT = 56 * 2 ** 20
MESH = pl.DeviceIdType.MESH


def _cparams(*sem):
    return pltpu.CompilerParams(dimension_semantics=tuple(sem), vmem_limit_bytes=VMEM_LIMIT)


def _dot(a, b):
    return jnp.dot(a.astype(MXU_DTYPE), b.astype(MXU_DTYPE), preferred_element_type=F32)


def _dot_nt(a, b):
    return lax.dot_general(a.astype(MXU_DTYPE), b.astype(MXU_DTYPE), (((1,), (1,)), ((), ())),
                           preferred_element_type=F32)


def _dot_tn(a, b):
    return lax.dot_general(a.astype(MXU_DTYPE), b.astype(MXU_DTYPE), (((0,), (0,)), ((), ())),
                           preferred_element_type=F32)


def _split_dot(x, ones, passes):
    acc, rest = None, x
    for p in range(passes):
        piece = rest.astype(MXU_DTYPE)
        part = jnp.dot(piece, ones, preferred_element_type=F32)
        acc = part if acc is None else acc + part
        if p + 1 < passes:
            rest = rest - piece.astype(F32)
    return acc


def _split_dot_lhs(ones, x, passes):
    acc, rest = None, x
    for p in range(passes):
        piece = rest.astype(MXU_DTYPE)
        part = jnp.dot(ones, piece, preferred_element_type=F32)
        acc = part if acc is None else acc + part
        if p + 1 < passes:
            rest = rest - piece.astype(F32)
    return acc


def _iota2(shape, axis):
    return lax.broadcasted_iota(jnp.int32, shape, axis)


_TILES = {"proj": (1024, 640, 1024), "out_proj": (1024, 1024, 1024), "ffn_in": (1024, 1408, 1024),
          "ffn_out": (1024, 1024, 2816), "ffn_out_dx": (1024, 1408, 1024), "ffn_out_dw": (1408, 1024, 2048),
          "ffn_in_dx": (1024, 1024, 1408), "ffn_in_dw": (1024, 1408, 2048), "out_proj_dx": (1024, 1024, 1024),
          "out_proj_dw": (1024, 1024, 2048), "proj_dx": (1024, 512, 3200), "proj_dw": (1024, 640, 2048)}


def _matmul(a, b, kind, tag, *, out_dtype=F32, add=None, add_scale=1.0, trans_a=False, trans_b=False):
    k, m = a.shape if trans_a else a.shape[::-1]
    n = b.shape[0] if trans_b else b.shape[1]
    tm, tn, tk = _TILES[kind]
    tm, tk, name = min(tm, m), min(tk, k), f"{kind}_{tag}"
    assert m % tm == 0 and n % tn == 0 and k % tk == 0, (a.shape, b.shape, tm, tn, tk)
    nk = k // tk

    def body(*refs):
        if add is None:
            a_ref, b_ref, o_ref = refs[:3]
            c_ref, scr = None, refs[3:]
        else:
            a_ref, b_ref, c_ref, o_ref = refs[:4]
            scr = refs[4:]
        dot = _dot_tn if trans_a else _dot_nt if trans_b else _dot
        part = dot(a_ref[...], b_ref[...])

        def finish(acc):
            if c_ref is not None:
                acc = acc + add_scale * c_ref[...]
            o_ref[...] = acc.astype(out_dtype)

        if nk == 1:
            finish(part)
        else:
            acc_ref = scr[0]
            kk = pl.program_id(2)

            @pl.when(kk == 0)
            def _():
                acc_ref[...] = part

            @pl.when(kk > 0)
            def _():
                acc_ref[...] += part

            @pl.when(kk == nk - 1)
            def _():
                finish(acc_ref[...])

    b_spec = pl.BlockSpec((tn, tk), lambda i, j, kk: (j, kk)) if trans_b else pl.BlockSpec((tk, tn), lambda i, j, kk: (kk, j))
    a_spec = pl.BlockSpec((tk, tm), lambda i, j, kk: (kk, i)) if trans_a else pl.BlockSpec((tm, tk), lambda i, j, kk: (i, kk))
    in_specs = [a_spec, b_spec]
    operands = [a, b]
    if add is not None:
        in_specs.append(pl.BlockSpec((tm, tn), lambda i, j, kk: (i, j)))
        operands.append(add)
    return pl.pallas_call(
        body, name=name, grid=(m // tm, n // tn, nk), in_specs=in_specs,
        out_specs=pl.BlockSpec((tm, tn), lambda i, j, kk: (i, j)),
        out_shape=jax.ShapeDtypeStruct((m, n), out_dtype),
        scratch_shapes=[pltpu.VMEM((tm, tn), F32)] if nk > 1 else [],
        compiler_params=_cparams("parallel", "parallel", "arbitrary"),
    )(*operands)


def _ln_stats(pre):
    mu = jnp.mean(pre, axis=-1, keepdims=True)
    xc = pre - mu
    var = jnp.mean(xc * xc, axis=-1, keepdims=True)
    rstd = lax.rsqrt(var + LN_EPS)
    return xc * rstd, rstd


def _ln_fwd(xin, branch, g, b, name):
    t, d = xin.shape
    tile = 256

    def body(x_ref, br_ref, g_ref, b_ref, pre_ref, y_ref):
        pre = ALPHA * x_ref[...] + br_ref[...]
        xhat, _ = _ln_stats(pre)
        pre_ref[...] = pre
        y_ref[...] = xhat * g_ref[...] + b_ref[...]

    row = pl.BlockSpec((tile, d), lambda i: (i, 0))
    vec = pl.BlockSpec((1, d), lambda i: (0, 0))
    return pl.pallas_call(
        body, name=name, grid=(t // tile,), in_specs=[row, row, vec, vec], out_specs=[row, row],
        out_shape=[jax.ShapeDtypeStruct((t, d), F32)] * 2, compiler_params=_cparams("parallel"),
    )(xin, branch, g.reshape(1, d), b.reshape(1, d))


def _ln_bwd(dy, pre, g, name):
    t, d = dy.shape
    tile = 256

    def body(dy_ref, pre_ref, g_ref, dpre_ref, dgb_ref):
        dyv = dy_ref[...]
        xhat, rstd = _ln_stats(pre_ref[...])
        dxh = dyv * g_ref[...]
        m1 = jnp.mean(dxh, axis=-1, keepdims=True)
        m2 = jnp.mean(dxh * xhat, axis=-1, keepdims=True)
        dpre_ref[...] = rstd * (dxh - m1 - xhat * m2)

        @pl.when(pl.program_id(0) == 0)
        def _():
            dgb_ref[...] = jnp.zeros_like(dgb_ref)

        dgb_ref[0:1, :] += jnp.sum(dyv * xhat, axis=0, keepdims=True)
        dgb_ref[1:2, :] += jnp.sum(dyv, axis=0, keepdims=True)

    row = pl.BlockSpec((tile, d), lambda i: (i, 0))
    return pl.pallas_call(
        body, name=name, grid=(t // tile,), in_specs=[row, row, pl.BlockSpec((1, d), lambda i: (0, 0))],
        out_specs=[row, pl.BlockSpec((8, d), lambda i: (0, 0))],
        out_shape=[jax.ShapeDtypeStruct((t, d), F32), jax.ShapeDtypeStruct((8, d), F32)],
        compiler_params=_cparams("arbitrary"),
    )(dy, pre, g.reshape(1, d))


def _swiglu_fwd(gu, name, carry=None):
    t = gu.shape[0]
    tile = 256

    def body(gu_ref, h_ref):
        gate = gu_ref[:, :D_FF]
        up = gu_ref[:, D_FF:]
        h_ref[...] = (gate * (1.0 / (1.0 + jnp.exp(-gate))) * up).astype(h_ref.dtype)

    return _host_call(
        body, carry, name=name, grid=(t // tile,), in_specs=[pl.BlockSpec((tile, 2 * D_FF), lambda i: (i, 0))],
        out_specs=[pl.BlockSpec((tile, D_FF), lambda i: (i, 0))],
        out_shape=[jax.ShapeDtypeStruct((t, D_FF), MXU_DTYPE)], operands=(gu,))


def _swiglu_bwd(dh, gu, name, carry=None):
    t = gu.shape[0]
    tile = 256

    def body(dh_ref, gu_ref, dgu_ref):
        gate = gu_ref[:, :D_FF]
        up = gu_ref[:, D_FF:]
        dhv = dh_ref[...]
        sig = 1.0 / (1.0 + jnp.exp(-gate))
        dgu_ref[:, :D_FF] = (dhv * up * sig * (1.0 + gate * (1.0 - sig))).astype(dgu_ref.dtype)
        dgu_ref[:, D_FF:] = (dhv * gate * sig).astype(dgu_ref.dtype)

    return _host_call(
        body, carry, name=name, grid=(t // tile,),
        in_specs=[pl.BlockSpec((tile, D_FF), lambda i: (i, 0)), pl.BlockSpec((tile, 2 * D_FF), lambda i: (i, 0))],
        out_specs=[pl.BlockSpec((tile, 2 * D_FF), lambda i: (i, 0))],
        out_shape=[jax.ShapeDtypeStruct((t, 2 * D_FF), MXU_DTYPE)], operands=(dh, gu))


def _loss_kernel(y, target, name):
    t, d = y.shape
    tile = 512

    def body(y_ref, t_ref, dy_ref, l_ref):
        err = y_ref[...] - t_ref[...]
        dy_ref[...] = err * (1.0 / d)

        @pl.when(pl.program_id(0) == 0)
        def _():
            l_ref[...] = jnp.zeros_like(l_ref)

        l_ref[...] += jnp.sum(err * err) * (0.5 / d)

    row = pl.BlockSpec((tile, d), lambda i: (i, 0))
    return pl.pallas_call(
        body, name=name, grid=(t // tile,), in_specs=[row, row],
        out_specs=[row, pl.BlockSpec((8, 128), lambda i: (0, 0))],
        out_shape=[jax.ShapeDtypeStruct((t, d), F32), jax.ShapeDtypeStruct((8, 128), F32)],
        compiler_params=_cparams("arbitrary"),
    )(y, target)


def _adamw(w, g, m, v, name):
    nl, r, c = w.shape
    tr = r
    for cand in (256, 352, 128, 64, 16, 8):
        if r % cand == 0:
            tr = cand
            break

    def body(w_ref, g_ref, m_ref, v_ref, d_ref, nm_ref, nv_ref):
        gv = g_ref[...]
        nm = ADAM_B1 * m_ref[...] + (1.0 - ADAM_B1) * gv
        nv = ADAM_B2 * v_ref[...] + (1.0 - ADAM_B2) * (gv * gv)
        m_hat = nm / (1.0 - ADAM_B1 ** ADAM_STEP)
        v_hat = nv / (1.0 - ADAM_B2 ** ADAM_STEP)
        d_ref[...] = -ADAM_LR * (m_hat / (jnp.sqrt(v_hat) + ADAM_EPS) + ADAM_WD * w_ref[...])
        nm_ref[...] = nm
        nv_ref[...] = nv

    blk = pl.BlockSpec((1, tr, c), lambda l, i: (l, i, 0))
    return pl.pallas_call(
        body, name=name, grid=(nl, r // tr), in_specs=[blk] * 4, out_specs=[blk] * 3,
        out_shape=[jax.ShapeDtypeStruct(w.shape, F32)] * 3, compiler_params=_cparams("parallel", "parallel"),
    )(w, g, m, v)


def _shift_down(u, k, rows):
    return jnp.where(rows >= k, pltpu.roll(u, k, 0), 0.0)


def _shift_up(u, k, rows, s):
    return jnp.where(rows < s - k, pltpu.roll(u, s - k, 0), 0.0)


def _conv_fwd(proj, conv_w, nb, s, name):
    def body(b_ref, c_ref, h_ref, w_ref, o_ref):
        rows = _iota2((s, CONV_W), 0)
        u = c_ref[...] * h_ref[...]
        y = w_ref[2:3, :] * u + w_ref[1:2, :] * _shift_down(u, 1, rows) + w_ref[0:1, :] * _shift_down(u, 2, rows)
        o_ref[...] = b_ref[...] * y

    col = lambda j: pl.BlockSpec((s, CONV_W), lambda b: (b, j))
    return pl.pallas_call(
        body, name=name, grid=(nb,),
        in_specs=[col(9), col(10), col(11), pl.BlockSpec((8, CONV_W), lambda b: (0, 0))],
        out_specs=pl.BlockSpec((s, CONV_W), lambda b: (b, 0)),
        out_shape=jax.ShapeDtypeStruct((nb * s, CONV_W), F32), compiler_params=_cparams("parallel"),
    )(proj, proj, proj, conv_w)


def _conv_bwd(dmixed, proj, conv_w, nb, s, name):
    def body(do_ref, b_ref, c_ref, h_ref, w_ref, dg_ref, dw_ref):
        rows = _iota2((s, CONV_W), 0)
        cg, hg, bg, dout = c_ref[...], h_ref[...], b_ref[...], do_ref[...]
        u = cg * hg
        u1 = _shift_down(u, 1, rows)
        u2 = _shift_down(u, 2, rows)
        y = w_ref[2:3, :] * u + w_ref[1:2, :] * u1 + w_ref[0:1, :] * u2
        dy = dout * bg
        du = w_ref[2:3, :] * dy + w_ref[1:2, :] * _shift_up(dy, 1, rows, s) + w_ref[0:1, :] * _shift_up(dy, 2, rows, s)
        dg_ref[:, 0:CONV_W] = dout * y
        dg_ref[:, CONV_W:2 * CONV_W] = du * hg
        dg_ref[:, 2 * CONV_W:3 * CONV_W] = du * cg

        @pl.when(pl.program_id(0) == 0)
        def _():
            dw_ref[...] = jnp.zeros_like(dw_ref)

        dw_ref[0:1, :] += jnp.sum(dy * u2, axis=0, keepdims=True)
        dw_ref[1:2, :] += jnp.sum(dy * u1, axis=0, keepdims=True)
        dw_ref[2:3, :] += jnp.sum(dy * u, axis=0, keepdims=True)

    col = lambda j: pl.BlockSpec((s, CONV_W), lambda b: (b, j))
    return pl.pallas_call(
        body, name=name, grid=(nb,),
        in_specs=[col(3), col(9), col(10), col(11), pl.BlockSpec((8, CONV_W), lambda b: (0, 0))],
        out_specs=[pl.BlockSpec((s, 3 * CONV_W), lambda b: (b, 0)), pl.BlockSpec((8, CONV_W), lambda b: (0, 0))],
        out_shape=[jax.ShapeDtypeStruct((nb * s, 3 * CONV_W), F32), jax.ShapeDtypeStruct((8, CONV_W), F32)],
        compiler_params=_cparams("arbitrary"),
    )(dmixed, proj, proj, proj, conv_w)


def _col_spec(s, base):
    return pl.BlockSpec((s, BLK), lambda b, p: (b, base + p))


def _qrows(i):
    return pl.ds(pl.multiple_of(i * QT, QT), QT)


def _rows(j):
    return pl.ds(pl.multiple_of(j * ATT, ATT), ATT)


def _keys_upto(i):
    return (i + 1) * (QT // ATT)


def _triangle(keep):
    return keep(_iota2((ATT, ATT), 0), _iota2((ATT, ATT), 1)).astype(MXU_DTYPE)


def _rows128(i):
    return pl.ds(pl.multiple_of(i * BLK, BLK), BLK)


def _log_sigmoid_parts(z):
    e = jnp.exp(-jnp.abs(z))
    l1p = jnp.log(1.0 + e)
    lb = jnp.minimum(z, 0.0) - l1p
    return lb, lb - z, e


def _head_masks():
    lane = _iota2((1, BLK), 1)
    return [(lane >= h * HEAD_DIM) & (lane < (h + 1) * HEAD_DIM) for h in range(2)]


def _split_heads(ref, scr, sels):
    for h, sel in enumerate(sels):
        scr[h] = jnp.where(sel, ref[...], 0.0).astype(MXU_DTYPE)


def _sb_fwd(proj, nb, s, name, carry=None):
    nblk = s // ATT

    def body(q_ref, k_ref, v_ref, o_ref, km, vm):
        sels = _head_masks()
        _split_heads(k_ref, km, sels)
        _split_heads(v_ref, vm, sels)
        rows = _iota2((QT, ATT), 0)
        cols = _iota2((QT, ATT), 1)
        later = _triangle(lambda r, c: r > c)

        def qblock(i, _):
            qi = (q_ref[_qrows(i), :] * 0.125).astype(MXU_DTYPE)

            def kblock(t, state):
                carries, acc = state
                j = _keys_upto(i) - 1 - t
                strict = (cols + (j * ATT - i * QT)) < rows
                out = []
                for h in range(2):
                    z = _dot_nt(qi, km[h, _rows(j), :])
                    lb, lr, _ = _log_sigmoid_parts(z)
                    lr = jnp.where(strict, lr, 0.0)
                    tail = _split_dot(lr, later, 2) + carries[h]
                    a = jnp.where(strict, jnp.exp(lb + tail), 0.0)
                    acc = acc + _dot(a, vm[h, _rows(j), :])
                    out.append(carries[h] + jnp.sum(lr, axis=-1, keepdims=True))
                return tuple(out), acc

            init = ((jnp.zeros((QT, 1), F32),) * 2, jnp.zeros((QT, BLK), F32))
            _, acc = lax.fori_loop(0, _keys_upto(i), kblock, init)
            o_ref[_qrows(i), :] = acc
            return 0

        lax.fori_loop(0, s // QT, qblock, 0)

    (o,), extra = _host_call(
        body, carry, name=name, grid=(nb, 2), in_specs=[_col_spec(s, 0), _col_spec(s, 2), _col_spec(s, 4)],
        out_specs=[_col_spec(s, 0)], out_shape=[jax.ShapeDtypeStruct((nb * s, 2 * BLK), F32)],
        scratch_shapes=[pltpu.VMEM((2, s, BLK), MXU_DTYPE)] * 2, operands=(proj, proj, proj))
    return o, extra


def _sb_bwd(proj, dmixed, nb, s, name, carry=None):
    nblk = s // ATT

    def body(q_ref, k_ref, v_ref, do_ref, dq_ref, dk_ref, dv_ref, km, vm, a_scr, dl_scr, beta_scr):
        sels = _head_masks()
        _split_heads(k_ref, km, sels)
        _split_heads(v_ref, vm, sels)
        rows = _iota2((QT, ATT), 0)
        cols = _iota2((QT, ATT), 1)
        later = _triangle(lambda r, c: r > c)
        earlier = _triangle(lambda r, c: r < c)
        dk_ref[...] = jnp.zeros_like(dk_ref)
        dv_ref[...] = jnp.zeros_like(dv_ref)

        def qblock(i, _):
            qi = (q_ref[_qrows(i), :] * 0.125).astype(MXU_DTYPE)
            doi = do_ref[_qrows(i), :].astype(MXU_DTYPE)
            qm = [jnp.where(sel, qi, 0.0) for sel in sels]
            dom = [jnp.where(sel, doi, 0.0) for sel in sels]

            def first(t, carries):
                j = _keys_upto(i) - 1 - t
                strict = (cols + (j * ATT - i * QT)) < rows
                out = []
                for h in range(2):
                    z = _dot_nt(qi, km[h, _rows(j), :])
                    lb, lr, e = _log_sigmoid_parts(z)
                    lr = jnp.where(strict, lr, 0.0)
                    tail = _split_dot(lr, later, 2) + carries[h]
                    a = jnp.where(strict, jnp.exp(lb + tail), 0.0)
                    a_scr[h, j] = a
                    dl_scr[h, j] = a * _dot_nt(doi, vm[h, _rows(j), :])
                    beta_scr[h, j] = jnp.exp(lb)
                    out.append(carries[h] + jnp.sum(lr, axis=-1, keepdims=True))
                return tuple(out)

            lax.fori_loop(0, _keys_upto(i), first, (jnp.zeros((QT, 1), F32),) * 2)

            def second(j, state):
                csums, dq = state
                strict = (cols + (j * ATT - i * QT)) < rows
                out = []
                for h in range(2):
                    dl = dl_scr[h, j]
                    beta = beta_scr[h, j]
                    before = _split_dot(dl, earlier, 2) + csums[h]
                    dz = jnp.where(strict, dl * (1.0 - beta) - beta * before, 0.0).astype(MXU_DTYPE)
                    dq = dq + _dot(dz, km[h, _rows(j), :])
                    dk_ref[_rows(j), :] += _dot_tn(dz, qm[h])
                    dv_ref[_rows(j), :] += _dot_tn(a_scr[h, j], dom[h])
                    out.append(csums[h] + jnp.sum(dl, axis=-1, keepdims=True))
                return tuple(out), dq

            init = ((jnp.zeros((QT, 1), F32),) * 2, jnp.zeros((QT, BLK), F32))
            _, dq = lax.fori_loop(0, _keys_upto(i), second, init)
            dq_ref[_qrows(i), :] = dq * 0.125
            return 0

        lax.fori_loop(0, s // QT, qblock, 0)

    out = _col_spec(s, 0)
    return _host_call(
        body, carry, name=name, grid=(nb, 2),
        in_specs=[_col_spec(s, 0), _col_spec(s, 2), _col_spec(s, 4), out], out_specs=[out] * 3,
        out_shape=[jax.ShapeDtypeStruct((nb * s, 2 * BLK), F32)] * 3,
        scratch_shapes=[pltpu.VMEM((2, s, BLK), MXU_DTYPE)] * 2 + [pltpu.VMEM((2, nblk, QT, ATT), F32)] * 3,
        operands=(proj, proj, proj, dmixed))


def _pair_spec(s, width):
    return pl.BlockSpec((None, 2, s, width), lambda b, p: (b, p, 0, 0))


def _fox_fwd(proj, ccol, crow, nb, s, name, carry=None):
    nblk = s // ATT

    def body(q_ref, k_ref, v_ref, cc_ref, cr_ref, o_ref, lse_ref, km, vm):
        sels = _head_masks()
        _split_heads(k_ref, km, sels)
        _split_heads(v_ref, vm, sels)
        rows = _iota2((QT, ATT), 0)
        cols = _iota2((QT, ATT), 1)

        def qblock(i, _):
            qi = (q_ref[_qrows(i), :] * 0.125).astype(MXU_DTYPE)
            ci = [cc_ref[h, _qrows(i), :] for h in range(2)]

            def kblock(j, state):
                ms, ls, acc = state
                causal = (cols + (j * ATT - i * QT)) <= rows
                new_m, new_l, scales, parts = [], [], [], []
                for h in range(2):
                    z = _dot_nt(qi, km[h, _rows(j), :]) + (ci[h] - cr_ref[h, j][0:1, :])
                    z = jnp.where(causal, z, NEG)
                    m_new = jnp.maximum(ms[h], jnp.max(z, axis=-1, keepdims=True))
                    p = jnp.exp(z - m_new)
                    scale = jnp.exp(ms[h] - m_new)
                    new_m.append(m_new)
                    new_l.append(scale * ls[h] + jnp.sum(p, axis=-1, keepdims=True))
                    scales.append(scale)
                    parts.append(_dot(p, vm[h, _rows(j), :]))
                acc = jnp.where(sels[0], scales[0], scales[1]) * acc + parts[0] + parts[1]
                return tuple(new_m), tuple(new_l), acc

            init = ((jnp.full((QT, 1), NEG, F32),) * 2, (jnp.zeros((QT, 1), F32),) * 2, jnp.zeros((QT, BLK), F32))
            ms, ls, acc = lax.fori_loop(0, _keys_upto(i), kblock, init)
            o_ref[_qrows(i), :] = acc / jnp.where(sels[0], ls[0], ls[1])
            for h in range(2):
                lse_ref[h, _qrows(i), :] = jnp.broadcast_to(ms[h] + jnp.log(ls[h]), (QT, ATT))
            return 0

        lax.fori_loop(0, s // QT, qblock, 0)

    crow_spec = pl.BlockSpec((None, 2, nblk, 8, ATT), lambda b, p: (b, p, 0, 0, 0))
    return _host_call(
        body, carry, name=name, grid=(nb, 2),
        in_specs=[_col_spec(s, 12), _col_spec(s, 14), _col_spec(s, 16), _pair_spec(s, ATT), crow_spec],
        out_specs=[_col_spec(s, 0), _pair_spec(s, ATT)],
        out_shape=[jax.ShapeDtypeStruct((nb * s, 2 * BLK), F32), jax.ShapeDtypeStruct((nb, N_HEADS, s, ATT), F32)],
        scratch_shapes=[pltpu.VMEM((2, s, BLK), MXU_DTYPE)] * 2, operands=(proj, proj, proj, ccol, crow))


def _fox_bwd(proj, dmixed, lse, ccol, crow, nb, s, name, carry=None):
    nblk = s // ATT

    def body(q_ref, k_ref, v_ref, do_ref, lse_ref, cc_ref, cr_ref, dq_ref, dk_ref, dv_ref, dc_ref, km, vm, p_scr, dp_scr):
        sels = _head_masks()
        _split_heads(k_ref, km, sels)
        _split_heads(v_ref, vm, sels)
        rows = _iota2((QT, ATT), 0)
        cols = _iota2((QT, ATT), 1)
        dk_ref[...] = jnp.zeros_like(dk_ref)
        dv_ref[...] = jnp.zeros_like(dv_ref)
        dc_ref[...] = jnp.zeros_like(dc_ref)

        def qblock(i, _):
            qi = (q_ref[_qrows(i), :] * 0.125).astype(MXU_DTYPE)
            doi = do_ref[_qrows(i), :].astype(MXU_DTYPE)
            qm = [jnp.where(sel, qi, 0.0) for sel in sels]
            dom = [jnp.where(sel, doi, 0.0) for sel in sels]
            ci = [cc_ref[h, _qrows(i), :] for h in range(2)]
            lsei = [lse_ref[h, _qrows(i), :] for h in range(2)]

            def probs(j, h):
                z = _dot_nt(qi, km[h, _rows(j), :]) + (ci[h] - cr_ref[h, j][0:1, :])
                p = jnp.where((cols + (j * ATT - i * QT)) <= rows, jnp.exp(z - lsei[h]), 0.0)
                return p, _dot_nt(doi, vm[h, _rows(j), :])

            def row_term(j, accs):
                out = []
                for h in range(2):
                    p, dp = probs(j, h)
                    p_scr[h, j] = p
                    dp_scr[h, j] = dp
                    out.append(accs[h] + jnp.sum(p * dp, axis=-1, keepdims=True))
                return tuple(out)

            di = lax.fori_loop(0, _keys_upto(i), row_term, (jnp.zeros((QT, 1), F32),) * 2)

            def kblock(j, dq):
                for h in range(2):
                    p = p_scr[h, j]
                    ds = p * (dp_scr[h, j] - di[h])
                    dc_ref[h, j] += jnp.broadcast_to(jnp.sum(ds, axis=0, keepdims=True), (8, ATT))
                    ds = ds.astype(MXU_DTYPE)
                    dk_ref[_rows(j), :] += _dot_tn(ds, qm[h])
                    dv_ref[_rows(j), :] += _dot_tn(p, dom[h])
                    dq = dq + _dot(ds, km[h, _rows(j), :])
                return dq

            dq = lax.fori_loop(0, _keys_upto(i), kblock, jnp.zeros((QT, BLK), F32))
            dq_ref[_qrows(i), :] = dq * 0.125
            return 0

        lax.fori_loop(0, s // QT, qblock, 0)

    crow_spec = pl.BlockSpec((None, 2, nblk, 8, ATT), lambda b, p: (b, p, 0, 0, 0))
    wide, cols_out = _pair_spec(s, ATT), _col_spec(s, 0)
    return _host_call(
        body, carry, name=name, grid=(nb, 2),
        in_specs=[_col_spec(s, 12), _col_spec(s, 14), _col_spec(s, 16), _col_spec(s, 4), wide, wide, crow_spec],
        out_specs=[cols_out, cols_out, cols_out, crow_spec],
        out_shape=[jax.ShapeDtypeStruct((nb * s, 2 * BLK), F32)] * 3 + [jax.ShapeDtypeStruct((nb, N_HEADS, nblk, 8, ATT), F32)],
        scratch_shapes=[pltpu.VMEM((2, s, BLK), MXU_DTYPE)] * 2 + [pltpu.VMEM((2, nblk, QT, ATT), F32)] * 2,
        operands=(proj, proj, proj, dmixed, lse, ccol, crow))


def _fox_gates_fwd(proj, f_bias, nb, s, name):
    chunk = 256

    def body(f_ref, b_ref, c_ref):
        lower = (_iota2((chunk, chunk), 0) >= _iota2((chunk, chunk), 1)).astype(MXU_DTYPE)
        carry = jnp.zeros((1, BLK), F32)
        for n in range(s // chunk):
            rows = pl.ds(n * chunk, chunk)
            lf, _, _ = _log_sigmoid_parts(f_ref[rows, :] + b_ref[0:1, :])
            c = _split_dot_lhs(lower, lf, 3) + carry
            c_ref[rows, :] = c
            carry = c[chunk - 1:chunk, :]

    return pl.pallas_call(
        body, name=name, grid=(nb,),
        in_specs=[pl.BlockSpec((s, BLK), lambda b: (b, (PROJ_PAD - BLK) // BLK)), pl.BlockSpec((8, BLK), lambda b: (0, 0))],
        out_specs=pl.BlockSpec((s, BLK), lambda b: (b, 0)),
        out_shape=jax.ShapeDtypeStruct((nb * s, BLK), F32), compiler_params=_cparams("parallel"),
    )(proj, f_bias)


def _fox_gates_bwd(dc, proj, f_bias, nb, s, name):
    chunk = 256

    def body(dc_ref, f_ref, b_ref, df_ref, db_ref):
        upper = (_iota2((chunk, chunk), 0) <= _iota2((chunk, chunk), 1)).astype(MXU_DTYPE)
        carry = jnp.zeros((1, BLK), F32)
        total = jnp.zeros((1, BLK), F32)
        for n in reversed(range(s // chunk)):
            rows = pl.ds(n * chunk, chunk)
            dlf = _split_dot_lhs(upper, dc_ref[rows, :], 3) + carry
            carry = dlf[0:1, :]
            pre = f_ref[rows, :] + b_ref[0:1, :]
            e = jnp.exp(-jnp.abs(pre))
            df = dlf * (jnp.where(pre >= 0.0, e, 1.0) / (1.0 + e))
            df_ref[rows, :] = df
            total = total + jnp.sum(df, axis=0, keepdims=True)

        @pl.when(pl.program_id(0) == 0)
        def _():
            db_ref[...] = jnp.zeros_like(db_ref)

        db_ref[0:1, :] += total

    return pl.pallas_call(
        body, name=name, grid=(nb,),
        in_specs=[pl.BlockSpec((s, BLK), lambda b: (b, 0)), pl.BlockSpec((s, BLK), lambda b: (b, (PROJ_PAD - BLK) // BLK)),
                  pl.BlockSpec((8, BLK), lambda b: (0, 0))],
        out_specs=[pl.BlockSpec((s, BLK), lambda b: (b, 0)), pl.BlockSpec((8, BLK), lambda b: (0, 0))],
        out_shape=[jax.ShapeDtypeStruct((nb * s, BLK), F32), jax.ShapeDtypeStruct((8, BLK), F32)],
        compiler_params=_cparams("arbitrary"),
    )(dc, proj, f_bias)


def _delta_kernel(dmixed, o, nb, s, name):
    def body(do_ref, o_ref, d_ref):
        prod = do_ref[...] * o_ref[...]
        for h, sel in enumerate(_head_masks()):
            d_ref[h] = jnp.broadcast_to(jnp.sum(jnp.where(sel, prod, 0.0), axis=-1, keepdims=True), (s, BLK))

    return pl.pallas_call(
        body, name=name, grid=(nb, 2), in_specs=[_col_spec(s, 2), _col_spec(s, 0)], out_specs=_pair_spec(s, BLK),
        out_shape=jax.ShapeDtypeStruct((nb, N_HEADS, s, BLK), F32), compiler_params=_cparams("parallel", "parallel"),
    )(dmixed, o)


def _t5_bucket_np(dist):
    max_exact = REL_BUCKETS // 2
    nf = np.maximum(dist, 1).astype(np.float32)
    large = max_exact + (np.log(nf / max_exact) / math.log(2048 / max_exact) * (REL_BUCKETS - max_exact)).astype(np.int32)
    large = np.minimum(large, REL_BUCKETS - 1)
    return np.where(dist < max_exact, dist, large)


def _bucket_table():
    qi = np.arange(BLK)[:, None]
    kj = np.arange(2 * BLK)[None, :]
    dist = qi + BLK - kj
    tables = []
    for window, dil in DIL_PATTERNS:
        in_band = (dist >= 0) & (dist <= window // dil)
        tables.append(np.where(in_band, _t5_bucket_np(np.maximum(dist, 0) * dil), -1).astype(np.int32))
    return np.stack(tables)


def _dil_scores(qb, kp, kc, b_ref, h, prev_valid):
    zp = _dot_nt(qb, kp) + b_ref[h, :, 0:BLK]
    zp = jnp.where(prev_valid, zp, NEG)
    zc = _dot_nt(qb, kc) + b_ref[h, :, BLK:2 * BLK]
    return zp, zc


def _residue_rows(b, seg, dil):
    if dil == 1:
        return _rows128(b), _rows128(jnp.maximum(b - 1, 0)), b > 0
    r, n = b // seg, b % seg
    cur = pl.ds(r + dil * n * BLK, BLK, stride=dil)
    prev = pl.ds(r + dil * jnp.maximum(n - 1, 0) * BLK, BLK, stride=dil)
    return cur, prev, n > 0


def _dil_attention_fwd(proj, bias, nb, s, name, carry=None):
    nblk = s // BLK

    def body(q_ref, k_ref, v_ref, b_ref, out_ref, lse_ref, o_scr, l_scr):
        sels = _head_masks()
        for p, (_, dil) in enumerate(DIL_PATTERNS):
            seg = s // dil // BLK

            def block(b, _, p=p, seg=seg, dil=dil):
                cur, prev, has_prev = _residue_rows(b, seg, dil)
                qb = (q_ref[cur, :] * 0.125).astype(MXU_DTYPE)
                kp, kc = k_ref[prev, :].astype(MXU_DTYPE), k_ref[cur, :].astype(MXU_DTYPE)
                vp, vc = v_ref[prev, :].astype(MXU_DTYPE), v_ref[cur, :].astype(MXU_DTYPE)
                acc = jnp.zeros((BLK, BLK), F32)
                for h, sel in enumerate(sels):
                    zp, zc = _dil_scores(qb, jnp.where(sel, kp, 0.0), jnp.where(sel, kc, 0.0), b_ref.at[p], h, has_prev)
                    m = jnp.maximum(jnp.max(zp, axis=-1, keepdims=True), jnp.max(zc, axis=-1, keepdims=True))
                    pp = jnp.exp(zp - m)
                    pc = jnp.exp(zc - m)
                    den = jnp.sum(pp, axis=-1, keepdims=True) + jnp.sum(pc, axis=-1, keepdims=True)
                    acc = acc + (_dot(pp, jnp.where(sel, vp, 0.0)) + _dot(pc, jnp.where(sel, vc, 0.0))) / den
                    l_scr[p, h, cur, :] = jnp.broadcast_to(m + jnp.log(den), (BLK, BLK))
                o_scr[p, cur, :] = acc
                return 0

            lax.fori_loop(0, nblk, block, 0, unroll=4)

        weights, dens = [], []
        for h in range(2):
            m = jnp.maximum(jnp.maximum(l_scr[0, h], l_scr[1, h]), l_scr[2, h])
            w = [jnp.exp(l_scr[p, h] - m) for p in range(3)]
            den = w[0] + w[1] + w[2]
            lse_ref[h] = m + jnp.log(den)
            weights.append(w)
            dens.append(den)
        num = sum(jnp.where(sels[0], weights[0][p], weights[1][p]) * o_scr[p] for p in range(3))
        out_ref[...] = num / jnp.where(sels[0], dens[0], dens[1])

    bias_spec = pl.BlockSpec((3, 2, BLK, 2 * BLK), lambda b, p: (0, p, 0, 0))
    return _host_call(
        body, carry, name=name, grid=(nb, 2), in_specs=[_col_spec(s, 6), _col_spec(s, 8), _col_spec(s, 10), bias_spec],
        out_specs=[_col_spec(s, 0), _pair_spec(s, BLK)],
        out_shape=[jax.ShapeDtypeStruct((nb * s, 2 * BLK), F32), jax.ShapeDtypeStruct((nb, N_HEADS, s, BLK), F32)],
        scratch_shapes=[pltpu.VMEM((3, s, BLK), F32), pltpu.VMEM((3, 2, s, BLK), F32)], operands=(proj, proj, proj, bias))


def _dil_attention_bwd(proj, dmixed, lse, delta, bias, nb, s, name, carry=None):
    nblk = s // BLK

    def body(q_ref, k_ref, v_ref, do_ref, lse_ref, dl_ref, b_ref, dq_ref, dk_ref, dv_ref, g_ref):
        sels = _head_masks()
        dq_ref[...] = jnp.zeros_like(dq_ref)
        dk_ref[...] = jnp.zeros_like(dk_ref)
        dv_ref[...] = jnp.zeros_like(dv_ref)
        g_ref[...] = jnp.zeros_like(g_ref)
        for p, (_, dil) in enumerate(DIL_PATTERNS):
            seg = s // dil // BLK

            def block(b, _, p=p, seg=seg, dil=dil):
                cur, prev, has_prev = _residue_rows(b, seg, dil)
                qb = (q_ref[cur, :] * 0.125).astype(MXU_DTYPE)
                dob = do_ref[cur, :].astype(MXU_DTYPE)
                kp, kc = k_ref[prev, :].astype(MXU_DTYPE), k_ref[cur, :].astype(MXU_DTYPE)
                vp, vc = v_ref[prev, :].astype(MXU_DTYPE), v_ref[cur, :].astype(MXU_DTYPE)
                dq = jnp.zeros((BLK, BLK), F32)
                dkp, dkc, dvp, dvc = dq, dq, dq, dq
                for h, sel in enumerate(sels):
                    kph, kch = jnp.where(sel, kp, 0.0), jnp.where(sel, kc, 0.0)
                    qh, doh = jnp.where(sel, qb, 0.0), jnp.where(sel, dob, 0.0)
                    lse_h = lse_ref[h, cur, :]
                    dlt = dl_ref[h, cur, :]
                    zp, zc = _dil_scores(qb, kph, kch, b_ref.at[p], h, has_prev)
                    pp = jnp.exp(zp - lse_h)
                    pc = jnp.exp(zc - lse_h)
                    dsp = pp * (_dot_nt(dob, jnp.where(sel, vp, 0.0)) - dlt)
                    dsc = pc * (_dot_nt(dob, jnp.where(sel, vc, 0.0)) - dlt)
                    g_ref[h, p, :, 0:BLK] += dsp
                    g_ref[h, p, :, BLK:2 * BLK] += dsc
                    dsp = dsp.astype(MXU_DTYPE)
                    dsc = dsc.astype(MXU_DTYPE)
                    dq = dq + _dot(dsp, kph) + _dot(dsc, kch)
                    dkp, dkc = dkp + _dot_tn(dsp, qh), dkc + _dot_tn(dsc, qh)
                    dvp, dvc = dvp + _dot_tn(pp, doh), dvc + _dot_tn(pc, doh)
                dq_ref[cur, :] += dq * 0.125
                dk_ref[prev, :] += dkp
                dk_ref[cur, :] += dkc
                dv_ref[prev, :] += dvp
                dv_ref[cur, :] += dvc
                return 0

            lax.fori_loop(0, nblk, block, 0, unroll=4)

    bias_spec = pl.BlockSpec((3, 2, BLK, 2 * BLK), lambda b, p: (0, p, 0, 0))
    cols, stats = _col_spec(s, 0), _pair_spec(s, BLK)
    return _host_call(
        body, carry, name=name, grid=(nb, 2),
        in_specs=[_col_spec(s, 6), _col_spec(s, 8), _col_spec(s, 10), _col_spec(s, 2), stats, stats, bias_spec],
        out_specs=[cols, cols, cols, pl.BlockSpec((None, 2, 3, BLK, 2 * BLK), lambda b, p: (b, p, 0, 0, 0))],
        out_shape=[jax.ShapeDtypeStruct((nb * s, 2 * BLK), F32)] * 3 + [jax.ShapeDtypeStruct((nb, N_HEADS, 3, BLK, 2 * BLK), F32)],
        operands=(proj, proj, proj, dmixed, lse, delta, bias))


def _bucket_reduce(gbias, table, name):
    nb = gbias.shape[0]

    def body(g_ref, t_ref, o_ref):
        row = _iota2((8, BLK), 0)
        lane = _iota2((8, BLK), 1)
        gsum = [[sum(g_ref[b, h, p] for b in range(nb)) for p in range(3)] for h in range(N_HEADS)]

        def bucket(k, acc):
            for h in range(N_HEADS):
                tot = sum(jnp.sum(jnp.where(t_ref[p] == k, gsum[h][p], 0.0)) for p in range(3))
                acc = acc + jnp.where((row == h) & (lane == k), tot, 0.0)
            return acc

        o_ref[...] = lax.fori_loop(0, REL_BUCKETS, bucket, jnp.zeros((8, BLK), F32))

    vm = pl.BlockSpec(memory_space=pltpu.VMEM)
    return pl.pallas_call(
        body, name=name, in_specs=[vm, vm], out_specs=vm, out_shape=jax.ShapeDtypeStruct((8, BLK), F32),
        compiler_params=pltpu.CompilerParams(vmem_limit_bytes=VMEM_LIMIT),
    )(gbias, table)


def _place():
    x, y, c = lax.axis_index("x"), lax.axis_index("y"), lax.axis_index("c")
    others = [(1 - x, y), (x, 1 - y), (1 - x, 1 - y)]
    return x, y, c, others


def _remote(src, dst, send_sem, recv_sem, to):
    return pltpu.make_async_remote_copy(src_ref=src, dst_ref=dst, send_sem=send_sem, recv_sem=recv_sem,
                                        device_id=to, device_id_type=MESH)


_HBM = pl.BlockSpec(memory_space=pl.ANY)


class _Exchange:
    def __init__(self, operands, out_shape, n_copies, copies, aliases=None):
        self.operands, self.out_shape, self.n_copies, self.copies = list(operands), list(out_shape), n_copies, copies
        self.aliases = dict(aliases or {})

    def sem_shapes(self):
        return [pltpu.SemaphoreType.DMA((self.n_copies,)), pltpu.SemaphoreType.DMA((self.n_copies,))]


def _start_all(sends):
    for cp in sends:
        cp.start()


def _wait_all(sends, arrivals):
    for cp in arrivals:
        cp.wait_recv()
    for cp in sends:
        cp.wait_send()


def _run_exchange(ex, name):
    ni = len(ex.operands)

    def body(*refs):
        sends, arrivals = ex.copies(refs[:ni], refs[ni:-2], refs[-2], refs[-1])
        _start_all(sends)
        _wait_all(sends, arrivals)

    return list(pl.pallas_call(
        body, name=name, in_specs=[_HBM] * ni, out_specs=[_HBM] * len(ex.out_shape), out_shape=ex.out_shape,
        scratch_shapes=ex.sem_shapes(), input_output_aliases=ex.aliases)(*ex.operands))


def _host_call(body, carry, *, name, grid, in_specs, out_specs, out_shape, operands, scratch_shapes=()):
    in_specs, out_specs, out_shape, scratch_shapes = list(in_specs), list(out_specs), list(out_shape), list(scratch_shapes)
    if carry is None:
        res = pl.pallas_call(body, name=name, grid=grid, in_specs=in_specs, out_specs=out_specs, out_shape=out_shape,
                             scratch_shapes=scratch_shapes, compiler_params=_cparams(*["parallel"] * len(grid)))(*operands)
        return list(res), []
    n_in, n_out, n_scr, c_in, c_out = len(in_specs), len(out_specs), len(scratch_shapes), len(carry.operands), len(carry.out_shape)
    steps = math.prod(grid)

    def wrapped(*refs):
        ins, refs = refs[:n_in], refs[n_in:]
        c_ins, refs = refs[:c_in], refs[c_in:]
        outs, refs = refs[:n_out], refs[n_out:]
        c_outs, refs = refs[:c_out], refs[c_out:]
        scr, (send_sems, recv_sems) = refs[:n_scr], refs[n_scr:]
        step = 0
        for d, size in enumerate(grid):
            step = step * size + pl.program_id(d)

        @pl.when(step == 0)
        def _():
            _start_all(carry.copies(c_ins, c_outs, send_sems, recv_sems)[0])

        body(*ins, *outs, *scr)

        @pl.when(step == steps - 1)
        def _():
            _wait_all(*carry.copies(c_ins, c_outs, send_sems, recv_sems))

    res = pl.pallas_call(
        wrapped, name=name, grid=grid, in_specs=in_specs + [_HBM] * c_in, out_specs=out_specs + [_HBM] * c_out,
        out_shape=out_shape + carry.out_shape, scratch_shapes=scratch_shapes + carry.sem_shapes(),
        input_output_aliases={n_in + i: n_out + j for i, j in carry.aliases.items()},
        compiler_params=_cparams(*["arbitrary"] * len(grid)))(*operands, *carry.operands)
    return list(res[:n_out]), list(res[n_out:])


def _half(which, rows):
    h = rows // 2
    return pl.ds(pl.multiple_of(which * h, 16), h)


def _like(arrays, shape_of=lambda t: t.shape):
    return [jax.ShapeDtypeStruct(shape_of(t), t.dtype) for t in arrays]


def _gather_ici(shards, layer):
    n = len(shards)

    def copies(ins, outs, send_sems, recv_sems, base=0):
        x, y, c, others = _place()
        me = 2 * x + y
        sends, arrivals = [], []
        for a in range(n):
            rows = _half(c, shards[a].shape[1])
            for k, (ox, oy) in enumerate(others):
                sems = (send_sems.at[base + 3 * a + k], recv_sems.at[base + 3 * a + k],(ox, oy, c))
                sends.append(_remote(ins[a].at[layer, rows], outs[a].at[me, rows], *sems))
                landed = outs[a].at[2 * ox + oy, rows]
                arrivals.append(_remote(landed, landed, *sems))
        return sends, arrivals

    return _Exchange(shards, _like(shards, lambda t: (N_CHIPS,) + t.shape[1:]), 3 * n, copies)


def _gather_d2d(gathered):
    n = len(gathered)

    def copies(ins, outs, send_sems, recv_sems, base=0):
        x, y, c, others = _place()
        sends, arrivals = [], []
        for a in range(n):
            r = gathered[a].shape[1]
            for k, (ox, oy) in enumerate(others):
                sems = (send_sems.at[base + 3 * a + k], recv_sems.at[base + 3 * a + k],(x, y, 1 - c))
                mine, theirs = outs[a].at[2 * ox + oy, _half(c, r)], outs[a].at[2 * ox + oy, _half(1 - c, r)]
                sends.append(_remote(mine, mine, *sems))
                arrivals.append(_remote(theirs, theirs, *sems))
        return sends, arrivals

    return _Exchange(gathered, _like(gathered), 3 * n, copies, aliases={a: a for a in range(n)})


def _swap_halves(g):
    n = len(g)

    def copies(ins, outs, send_sems, recv_sems, base=0):
        x, y, c, _ = _place()
        sends, arrivals = [], []
        for a in range(n):
            sems = (send_sems.at[base + a], recv_sems.at[base + a], (x, y, 1 - c))
            sends.append(_remote(ins[a].at[:, _half(1 - c, g[a].shape[1])], outs[a], *sems))
            arrivals.append(_remote(outs[a], outs[a], *sems))
        return sends, arrivals

    return _Exchange(g, _like(g, lambda t: (t.shape[0], t.shape[1] // 2, t.shape[2])), n, copies)


def _scatter_shards(ps):
    n = len(ps)

    def copies(ins, outs, send_sems, recv_sems, base=0):
        x, y, c, others = _place()
        me = 2 * x + y
        sends, arrivals = [], []
        for a in range(n):
            for k, (ox, oy) in enumerate(others):
                sems = (send_sems.at[base + 3 * a + k], recv_sems.at[base + 3 * a + k],(ox, oy, c))
                sends.append(_remote(ins[a].at[2 * ox + oy], outs[a].at[me], *sems))
                slot = outs[a].at[2 * ox + oy]
                arrivals.append(_remote(slot, slot, *sems))
        return sends, arrivals

    return _Exchange(ps, _like(ps), 3 * n, copies)


def _share_halves(mine):
    n = len(mine)

    def copies(ins, outs, send_sems, recv_sems, base=0):
        x, y, c, _ = _place()
        sends, arrivals = [], []
        for a in range(n):
            sems = (send_sems.at[base + a], recv_sems.at[base + a], (x, y, 1 - c))
            sends.append(_remote(ins[a], outs[a], *sems))
            arrivals.append(_remote(outs[a], outs[a], *sems))
        return sends, arrivals

    return _Exchange(mine, _like(mine), n, copies)


def _row_tile(r):
    for cand in (256, 352, 128):
        if r % cand == 0:
            return cand
    return r


def _pair_sum(g, other, core, name):
    ns, h, w = other.shape
    tr = _row_tile(h)
    per_half = h // tr

    def body(core_ref, g_ref, o_ref, out_ref):
        out_ref[...] = (g_ref[...] + o_ref[...]).astype(out_ref.dtype)

    blk = pl.BlockSpec((None, tr, w), lambda k, i, core_ref: (k, i, 0))
    grid_spec = pltpu.PrefetchScalarGridSpec(
        num_scalar_prefetch=1, grid=(ns, per_half),
        in_specs=[pl.BlockSpec((None, tr, w), lambda k, i, core_ref: (k, core_ref[0] * per_half + i, 0)), blk], out_specs=blk)
    return pl.pallas_call(
        body, name=name, grid_spec=grid_spec, out_shape=jax.ShapeDtypeStruct((ns, h, w), MXU_DTYPE),
        compiler_params=_cparams("parallel", "parallel"),
    )(core.reshape(1).astype(jnp.int32), g, other)


def _chip_sum(q, p, chip, name):
    ns, r, w = q.shape
    tr = _row_tile(r)

    def body(chip_ref, q_ref, own_ref, out_ref):
        me = chip_ref[0]
        own = own_ref[...].astype(F32)
        terms = [jnp.where(me == k, own, q_ref[k].astype(F32)) for k in range(ns)]
        out_ref[...] = ((terms[0] + terms[1]) + terms[2]) + terms[3]

    grid_spec = pltpu.PrefetchScalarGridSpec(
        num_scalar_prefetch=1, grid=(r // tr,),
        in_specs=[pl.BlockSpec((ns, tr, w), lambda i, chip_ref: (0, i, 0)),
                  pl.BlockSpec((None, tr, w), lambda i, chip_ref: (chip_ref[0], i, 0))],
        out_specs=pl.BlockSpec((tr, w), lambda i, chip_ref: (i, 0)))
    return pl.pallas_call(
        body, name=name, grid_spec=grid_spec, out_shape=jax.ShapeDtypeStruct((r, w), F32),
        compiler_params=_cparams("parallel"),
    )(chip.reshape(1).astype(jnp.int32), q, p)


def _merge(exchanges):
    if len(exchanges) <= 1:
        return exchanges[0] if exchanges else None
    operands, out_shape, aliases, spans, n = [], [], {}, [], 0
    for ex in exchanges:
        spans.append((len(operands), len(out_shape), n))
        aliases.update({len(operands) + i: len(out_shape) + j for i, j in ex.aliases.items()})
        operands += ex.operands
        out_shape += ex.out_shape
        n += ex.n_copies

    def copies(ins, outs, send_sems, recv_sems, base=0):
        sends, arrivals = [], []
        for ex, (i0, o0, s0) in zip(exchanges, spans):
            s, a = ex.copies(ins[i0:i0 + len(ex.operands)], outs[o0:o0 + len(ex.out_shape)], send_sems, recv_sems, base + s0)
            sends += s
            arrivals += a
        return sends, arrivals

    return _Exchange(operands, out_shape, n, copies, aliases)


def _take(hooks, host):
    stages = (hooks or {}).pop(host, [])
    exchanges = [make() for make, _ in stages]

    def finish(results):
        for (_, done), ex in zip(stages, exchanges):
            done(results[:len(ex.out_shape)])
            results = results[len(ex.out_shape):]

    return _merge(exchanges), finish


def _hook(hooks, host, make, done):
    hooks.setdefault(host, []).append((make, done))


class _WeightPrefetch:
    def __init__(self, names, shards, layer, chip):
        self.names, self.shards, self.layer, self.chip, self.result = names, [shards[n] for n in names], layer, chip, None

    def first(self):
        return _gather_ici(self.shards, self.layer)

    def got_first(self, arrived):
        self.arrived = arrived

    def second(self):
        return _gather_d2d(self.arrived)

    def got_second(self, gathered):
        self.result = {name: lax.dynamic_update_index_in_dim(got, own[self.layer], self.chip, 0)
                       for name, got, own in zip(self.names, gathered, self.shards)}

    def ride(self, hooks, first_host, second_host):
        _hook(hooks, first_host, self.first, self.got_first)
        _hook(hooks, second_host, self.second, self.got_second)

    def run(self, tag):
        self.got_first(_run_exchange(self.first(), f"gather_ici_{tag}"))
        self.got_second(_run_exchange(self.second(), f"gather_d2d_{tag}"))


class _GradReduce:
    def __init__(self, g, chip, core, tag):
        self.names, self.g, self.chip, self.core, self.tag, self.result = list(g), list(g.values()), chip, core, tag, None

    def swap(self):
        return _swap_halves(self.g)

    def got_swap(self, theirs):
        self.pair = [_pair_sum(g, t, self.core, f"pair_sum_{n}_{self.tag}") for n, g, t in zip(self.names, self.g, theirs)]

    def scatter(self):
        return _scatter_shards(self.pair)

    def got_scatter(self, q):
        self.mine = [_chip_sum(qa, pa, self.chip, f"chip_sum_{n}_{self.tag}") for n, qa, pa in zip(self.names, q, self.pair)]

    def share(self):
        return _share_halves(self.mine)

    def got_share(self, theirs):
        self.result = {n: jnp.where(self.core == 0, jnp.concatenate([a, b]), jnp.concatenate([b, a]))
                       for n, a, b in zip(self.names, self.mine, theirs)}

    def ride(self, hooks, swap_host, scatter_host, share_host):
        _hook(hooks, swap_host, self.swap, self.got_swap)
        _hook(hooks, scatter_host, self.scatter, self.got_scatter)
        _hook(hooks, share_host, self.share, self.got_share)

    def run(self):
        self.got_swap(_run_exchange(self.swap(), f"swap_halves_{self.tag}"))
        self.got_scatter(_run_exchange(self.scatter(), f"scatter_shards_{self.tag}"))
        self.got_share(_run_exchange(self.share(), f"share_halves_{self.tag}"))


class _LayerWeights:
    def __init__(self, gathered):
        self.gathered, self.made = gathered, {}

    def __getitem__(self, key):
        if key not in self.made:
            cols = lambda t: jnp.swapaxes(t, 0, 1).reshape(t.shape[1], -1)
            rows = lambda t: t.reshape(-1, t.shape[2])
            if key == "w_in":
                made = jnp.pad(cols(self.gathered("w_in")), ((0, 0), (0, PROJ_PAD - PROJ)))
            elif key == "w_gu":
                made = jnp.concatenate([cols(self.gathered("w_gate")), cols(self.gathered("w_up"))], axis=-1)
            else:
                made = rows(self.gathered(key))
            self.made[key] = made
        return self.made[key]


def _gather_small(pk, name):
    rows, w = pk.shape

    def body(pk_ref, all_ref, sum_ref, send_sems, recv_sems):
        x, y, c, _ = _place()
        me = 4 * x + 2 * y + c
        all_ref[me] = pk_ref[...]
        flips = [(fx, fy, fc) for fx in (0, 1) for fy in (0, 1) for fc in (0, 1)][1:]
        peers = [(x ^ fx, y ^ fy, c ^ fc) for fx, fy, fc in flips]
        sends = [_remote(pk_ref, all_ref.at[me], send_sems.at[k], recv_sems.at[k], peer) for k, peer in enumerate(peers)]
        for cp in sends:
            cp.start()
        for k, (px, py, pc) in enumerate(peers):
            slot = all_ref.at[4 * px + 2 * py + pc]
            _remote(slot, slot, send_sems.at[k], recv_sems.at[k], (px, py, pc)).wait_recv()
        for cp in sends:
            cp.wait_send()
        total = all_ref[0]
        for d in range(1, N_DEV):
            total = total + all_ref[d]
        sum_ref[...] = total

    vm = pl.BlockSpec(memory_space=pltpu.VMEM)
    return pl.pallas_call(
        body, name=name, in_specs=[vm], out_specs=[vm, vm],
        out_shape=[jax.ShapeDtypeStruct((N_DEV, rows, w), F32), jax.ShapeDtypeStruct((rows, w), F32)],
        scratch_shapes=[pltpu.SemaphoreType.DMA((7,)), pltpu.SemaphoreType.DMA((7,))],
    )(pk)


def _row_layout(c, nb, s):
    ch = jnp.swapaxes(c[:, :N_HEADS].reshape(nb, s, N_HEADS), 1, 2)
    ccol = jnp.broadcast_to(ch[..., None], (nb, N_HEADS, s, ATT))
    crow = jnp.broadcast_to(ch.reshape(nb, N_HEADS, s // ATT, 1, ATT), (nb, N_HEADS, s // ATT, 8, ATT))
    return ccol, crow


def _dil_bias(rel_bias, name):
    def body(rel_ref, t_ref, o_ref):
        for p in range(len(DIL_PATTERNS)):
            table = t_ref[p]

            def bucket(k, accs, table=table):
                return tuple(jnp.where(table == k, rel_ref[k, h], acc) for h, acc in enumerate(accs))

            accs = lax.fori_loop(0, REL_BUCKETS, bucket, tuple(jnp.full((BLK, 2 * BLK), NEG, F32) for _ in range(N_HEADS)))
            for h in range(N_HEADS):
                o_ref[p, h] = accs[h]

    vm = pl.BlockSpec(memory_space=pltpu.VMEM)
    return pl.pallas_call(
        body, name=name, in_specs=[pl.BlockSpec(memory_space=pltpu.SMEM), vm], out_specs=vm,
        out_shape=jax.ShapeDtypeStruct((len(DIL_PATTERNS), N_HEADS, BLK, 2 * BLK), F32),
        compiler_params=pltpu.CompilerParams(vmem_limit_bytes=VMEM_LIMIT),
    )(rel_bias, jnp.asarray(_bucket_table()))


def _layer_forward(x, wts, small, nb, s, tag, hooks=None):
    proj = _matmul(x, wts["w_in"], "proj", tag)

    carry, finish = _take(hooks, "sb_fwd")
    o_sb, carried = _sb_fwd(proj, nb, s, f"sb_fwd_{tag}", carry)
    finish(carried)

    bias = _dil_bias(small["rel_bias"], f"dil_bias_{tag}")
    carry, finish = _take(hooks, "dil_fwd")
    (o_dl, lse_dl), carried = _dil_attention_fwd(proj, bias, nb, s, f"dil_fwd_{tag}", carry)
    finish(carried)

    fb = jnp.zeros((8, BLK), F32).at[0, :N_HEADS].set(small["f_bias"])
    csum = _fox_gates_fwd(proj, fb, nb, s, f"fox_gates_{tag}")
    ccol, crow = _row_layout(csum, nb, s)
    carry, finish = _take(hooks, "fox_fwd")
    (o_fx, lse_fx), carried = _fox_fwd(proj, ccol, crow, nb, s, f"fox_fwd_{tag}", carry)
    finish(carried)

    cw = jnp.zeros((8, CONV_W), F32).at[:3].set(small["conv_w"])
    o_cv = _conv_fwd(proj, cw, nb, s, f"conv_fwd_{tag}")

    mixed = jnp.concatenate([o_sb, o_dl, o_fx, o_cv], axis=-1).astype(MXU_DTYPE)
    mix = _matmul(mixed, wts["w_out"], "out_proj", tag)
    pre1, x1 = _ln_fwd(x, mix, small["ln1_g"], small["ln1_b"], f"ln1_fwd_{tag}")
    gu = _matmul(x1, wts["w_gu"], "ffn_in", tag)
    carry, finish = _take(hooks, "swiglu_fwd")
    (hid,), carried = _swiglu_fwd(gu, f"swiglu_fwd_{tag}", carry)
    finish(carried)
    ffn = _matmul(hid, wts["w_down"], "ffn_out", tag)
    pre2, x2 = _ln_fwd(x1, ffn, small["ln2_g"], small["ln2_b"], f"ln2_fwd_{tag}")
    saved = dict(x=x, proj=proj, bias=bias, o_dl=o_dl, lse_dl=lse_dl, fb=fb, ccol=ccol, crow=crow,
                 o_fx=o_fx, lse_fx=lse_fx, cw=cw, mixed=mixed, pre1=pre1, x1=x1, gu=gu, hid=hid, pre2=pre2)
    return x2, saved


def _layer_backward(dx2, sv, wts, small, nb, s, tag, hooks=None, ffn_grads_ready=None):
    t = nb * s
    dpre2, dgb2 = _ln_bwd(dx2, sv["pre2"], small["ln2_g"], f"ln2_bwd_{tag}")
    dpre2_b = dpre2.astype(MXU_DTYPE)
    dhid = _matmul(dpre2_b, wts["w_down"], "ffn_out_dx", tag, trans_b=True)
    dw_down = _matmul(sv["hid"], dpre2_b, "ffn_out_dw", tag, trans_a=True)
    carry, finish = _take(hooks, "swiglu_bwd")
    (dgu,), carried = _swiglu_bwd(dhid, sv["gu"], f"swiglu_bwd_{tag}", carry)
    finish(carried)
    dx1 = _matmul(dgu, wts["w_gu"], "ffn_in_dx", tag, add=dpre2, add_scale=ALPHA, trans_b=True)
    dw_gu = _matmul(sv["x1"].astype(MXU_DTYPE), dgu, "ffn_in_dw", tag, trans_a=True)
    if ffn_grads_ready:
        ffn_grads_ready(dw_down, dw_gu[:, :D_FF], dw_gu[:, D_FF:])

    dpre1, dgb1 = _ln_bwd(dx1, sv["pre1"], small["ln1_g"], f"ln1_bwd_{tag}")
    dpre1_b = dpre1.astype(MXU_DTYPE)
    dmixed = _matmul(dpre1_b, wts["w_out"], "out_proj_dx", tag, trans_b=True)
    dw_out = _matmul(sv["mixed"], dpre1_b, "out_proj_dw", tag, trans_a=True)
    proj = sv["proj"]

    carry, finish = _take(hooks, "sb_bwd")
    (dq_sb, dk_sb, dv_sb), carried = _sb_bwd(proj, dmixed, nb, s, f"sb_bwd_{tag}", carry)
    finish(carried)

    delta_dl = _delta_kernel(dmixed, sv["o_dl"], nb, s, f"dil_delta_{tag}")
    carry, finish = _take(hooks, "dil_bwd")
    (dq_dl, dk_dl, dv_dl, gbias), carried = _dil_attention_bwd(proj, dmixed, sv["lse_dl"], delta_dl, sv["bias"], nb, s,
                                                               f"dil_bwd_{tag}", carry)
    finish(carried)
    drel = _bucket_reduce(gbias, jnp.asarray(_bucket_table()), f"rel_bias_grad_{tag}")

    carry, finish = _take(hooks, "fox_bwd")
    (dq_fx, dk_fx, dv_fx, dcol), carried = _fox_bwd(proj, dmixed, sv["lse_fx"], sv["ccol"], sv["crow"], nb, s,
                                                    f"fox_bwd_{tag}", carry)
    finish(carried)
    dcs = -jnp.swapaxes(dcol[:, :, :, 0, :].reshape(nb, N_HEADS, s), 1, 2).reshape(t, N_HEADS)
    dcs = jnp.pad(dcs, ((0, 0), (0, BLK - N_HEADS)))
    dfx, dfb = _fox_gates_bwd(dcs, proj, sv["fb"], nb, s, f"fox_gates_bwd_{tag}")

    dgates, dcw = _conv_bwd(dmixed, proj, sv["cw"], nb, s, f"conv_bwd_{tag}")

    dproj = jnp.concatenate([dq_sb, dk_sb, dv_sb, dq_dl, dk_dl, dv_dl, dq_fx, dk_fx, dv_fx, dgates, dfx],
                            axis=-1).astype(MXU_DTYPE)
    dx = _matmul(dproj, wts["w_in"], "proj_dx", tag, add=dpre1, add_scale=ALPHA, trans_b=True)
    dw_in = _matmul(sv["x"].astype(MXU_DTYPE), dproj, "proj_dw", tag, trans_a=True)

    grads = dict(w_in=dw_in[:, :PROJ], w_out=dw_out, w_gate=dw_gu[:, :D_FF], w_up=dw_gu[:, D_FF:], w_down=dw_down,
                 ln1_g=dgb1[0], ln1_b=dgb1[1], ln2_g=dgb2[0], ln2_b=dgb2[1], conv_w=dcw[:3], f_bias=dfb[0, :N_HEADS],
                 rel_bias=drel[:N_HEADS, :REL_BUCKETS].T)
    return dx, grads


class _NoExchanges:
    def forward_hooks(self, layer):
        return None

    def backward_hooks(self, layer):
        return None

    def ffn_grads_ready(self, layer):
        return None

    def layer_done(self, layer, grads):
        pass


def _local_step(x, target, weights_of, small_all, schedule=None):
    schedule = schedule or _NoExchanges()
    nb, s, d = x.shape
    h = x.reshape(nb * s, d)
    saved = []
    for layer in range(DEPTH):
        wts = weights_of(layer)
        h, sv = _layer_forward(h, wts, small_all[layer], nb, s, f"l{layer}", schedule.forward_hooks(layer))
        saved.append((sv, wts))
    dy, lossp = _loss_kernel(h, target.reshape(nb * s, d), "loss")
    grads = [None] * DEPTH
    for layer in reversed(range(DEPTH)):
        sv, wts = saved[layer]
        dy, grads[layer] = _layer_backward(dy, sv, wts, small_all[layer], nb, s, f"l{layer}",
                                           schedule.backward_hooks(layer), schedule.ffn_grads_ready(layer))
        schedule.layer_done(layer, grads[layer])
    return lossp, dy.reshape(nb, s, d), grads


_BIG = ("w_in", "w_out", "w_gate", "w_up", "w_down")
_COL_SHARDED = ("w_in", "w_gate", "w_up")


class _Schedule:
    def __init__(self, shards, chip, core):
        self.chip, self.core, self.reduces = chip, core, [[] for _ in range(DEPTH)]
        first = _WeightPrefetch(["w_in"], shards, 0, chip)
        first.run("l0_w_in")
        rest = _WeightPrefetch(["w_out", "w_gate", "w_up", "w_down"], shards, 0, chip)
        ahead_a = _WeightPrefetch(["w_in", "w_out", "w_down"], shards, 1, chip)
        ahead_b = _WeightPrefetch(["w_gate", "w_up"], shards, 1, chip)
        self.fetches = [[first, rest], [ahead_a, ahead_b]]
        self.forward, self.backward = [{} for _ in range(DEPTH)], [{} for _ in range(DEPTH)]
        rest.ride(self.forward[0], "sb_fwd", "fox_fwd")
        ahead_a.ride(self.forward[0], "dil_fwd", "swiglu_fwd")
        ahead_b.ride(self.forward[0], "fox_fwd", "swiglu_fwd")

    def weights(self, layer):
        def gathered(name):
            return next(f.result[name] for f in self.fetches[layer] if name in f.names)
        return _LayerWeights(gathered)

    def forward_hooks(self, layer):
        return self.forward[layer]

    def backward_hooks(self, layer):
        return self.backward[layer]

    def _reduce(self, layer, grads, tag):
        red = _GradReduce({name: _by_chip(name, g) for name, g in grads.items()}, self.chip, self.core, tag)
        self.reduces[layer].append(red)
        return red

    def ffn_grads_ready(self, layer):
        if layer != 0:
            return None

        def ready(dw_down, dw_gate, dw_up):
            red = self._reduce(0, dict(w_gate=dw_gate, w_up=dw_up, w_down=dw_down), "l0_ffn")
            red.ride(self.backward[0], "sb_bwd", "dil_bwd", "fox_bwd")

        return ready

    def layer_done(self, layer, grads):
        if layer == 1:
            self._reduce(1, {name: grads[name] for name in _BIG}, "l1").ride(self.backward[0], "swiglu_bwd", "sb_bwd", "fox_bwd")
        else:
            self._reduce(0, dict(w_in=grads["w_in"], w_out=grads["w_out"]), "l0_attn").run()

    def reduced(self, layer, name):
        return next(r.result[name] for r in self.reduces[layer] if name in r.names)


def _by_chip(name, g):
    if name in _COL_SHARDED:
        return jnp.swapaxes(g.reshape(g.shape[0], N_CHIPS, -1), 0, 1)
    return g.reshape(N_CHIPS, -1, g.shape[1])


_SMALL_LAYOUT = (("ln1_g", 0), ("ln1_b", 2), ("ln2_g", 4), ("ln2_b", 6), ("conv_w", 8))
_ROW_MISC = 10
_ROW_LOSS = 11


def _pack_small(per_layer, rel_bias, loss=None):
    pk = jnp.zeros((SMALL_ROWS, D_MODEL), F32)
    for name, row in _SMALL_LAYOUT:
        for l in range(DEPTH):
            v = per_layer[l][name].reshape(-1)
            pk = pk.at[row + l, :v.shape[0]].set(v)
    fb = jnp.concatenate([per_layer[l]["f_bias"] for l in range(DEPTH)])
    pk = pk.at[_ROW_MISC, :2 * N_HEADS].set(fb)
    pk = pk.at[_ROW_MISC, BLK:BLK + REL_BUCKETS * N_HEADS].set(rel_bias.reshape(-1))
    if loss is not None:
        pk = pk.at[_ROW_LOSS, 0].set(loss)
    return pk


def _unpack_small(pk, conv_cols):
    out = {}
    for name, row in _SMALL_LAYOUT:
        n = 3 * conv_cols if name == "conv_w" else D_MODEL
        v = pk[row:row + DEPTH, :n]
        out[name] = v.reshape(DEPTH, 3, conv_cols) if name == "conv_w" else v
    out["f_bias"] = pk[_ROW_MISC, :2 * N_HEADS].reshape(DEPTH, N_HEADS)
    out["rel_bias"] = pk[_ROW_MISC, BLK:BLK + REL_BUCKETS * N_HEADS].reshape(REL_BUCKETS, N_HEADS)
    return out


_WEIGHTS = ("w_in", "f_bias", "conv_w", "w_out", "rel_bias", "ln1_g", "ln1_b", "w_gate", "w_up", "w_down", "ln2_g", "ln2_b")


def kernel(x, w_in, f_bias, conv_w, w_out, rel_bias, ln1_g, ln1_b, w_gate, w_up, w_down, ln2_g, ln2_b, loss_target, m_w_in, m_f_bias, m_conv_w, m_w_out, m_rel_bias, m_ln1_g, m_ln1_b, m_w_gate, m_w_up, m_w_down, m_ln2_g, m_ln2_b, v_w_in, v_f_bias, v_conv_w, v_w_out, v_rel_bias, v_ln1_g, v_ln1_b, v_w_gate, v_w_up, v_w_down, v_ln2_g, v_ln2_b):
    w = dict(w_in=w_in, f_bias=f_bias, conv_w=conv_w, w_out=w_out, rel_bias=rel_bias, ln1_g=ln1_g, ln1_b=ln1_b,
             w_gate=w_gate, w_up=w_up, w_down=w_down, ln2_g=ln2_g, ln2_b=ln2_b)
    m = dict(w_in=m_w_in, f_bias=m_f_bias, conv_w=m_conv_w, w_out=m_w_out, rel_bias=m_rel_bias, ln1_g=m_ln1_g,
             ln1_b=m_ln1_b, w_gate=m_w_gate, w_up=m_w_up, w_down=m_w_down, ln2_g=m_ln2_g, ln2_b=m_ln2_b)
    v = dict(w_in=v_w_in, f_bias=v_f_bias, conv_w=v_conv_w, w_out=v_w_out, rel_bias=v_rel_bias, ln1_g=v_ln1_g,
             ln1_b=v_ln1_b, w_gate=v_w_gate, w_up=v_w_up, w_down=v_w_down, ln2_g=v_ln2_g, ln2_b=v_ln2_b)
    chip = 2 * lax.axis_index("x") + lax.axis_index("y")
    core = lax.axis_index("c")
    conv_shard = CONV_W // N_CHIPS

    schedule = _Schedule({name: w[name].astype(MXU_DTYPE) for name in _BIG}, chip, core)
    cw_pk = jnp.zeros((8, D_MODEL), F32).at[0, :DEPTH * 3 * conv_shard].set(conv_w.reshape(-1))
    cw_all, _ = _gather_small(cw_pk, "gather_conv_w")
    cw_chips = cw_all[0::2, 0, :DEPTH * 3 * conv_shard].reshape(N_CHIPS, DEPTH, 3, conv_shard)
    conv_full = jnp.moveaxis(cw_chips, 0, 2).reshape(DEPTH, 3, CONV_W)
    small_all = [dict(f_bias=f_bias[l], conv_w=conv_full[l], rel_bias=rel_bias, ln1_g=ln1_g[l], ln1_b=ln1_b[l],
                      ln2_g=ln2_g[l], ln2_b=ln2_b[l]) for l in range(DEPTH)]

    lossp, grad_x, grads = _local_step(x, loss_target, schedule.weights, small_all, schedule)
    big_g = {name: jnp.stack([schedule.reduced(l, name) for l in range(DEPTH)]) for name in _BIG}

    drel = grads[0]["rel_bias"] + grads[1]["rel_bias"]
    small_pk = _pack_small(grads, drel, lossp[0, 0])
    _, small_sum = _gather_small(small_pk, "gather_small_grads")
    loss = small_sum[_ROW_LOSS, 0]
    small_g = _unpack_small(small_sum, CONV_W)
    small_g["conv_w"] = lax.dynamic_slice_in_dim(small_g["conv_w"], chip * conv_shard, conv_shard, axis=2)

    out_g, out_d, out_m, out_v = dict(small_g), {}, {}, {}
    for name in _BIG:
        out_g[name] = big_g[name]
        out_d[name], out_m[name], out_v[name] = _adamw(w[name], big_g[name], m[name], v[name], f"adamw_{name}")
    per_layer = lambda src: [{name: src[name][l] for name in ("ln1_g", "ln1_b", "ln2_g", "ln2_b", "conv_w", "f_bias")}
                             for l in range(DEPTH)]
    packs = [_pack_small(per_layer(src), src["rel_bias"])[None] for src in (w, small_g, m, v)]
    for dst, pk in zip((out_d, out_m, out_v), _adamw(*packs, "adamw_small")):
        dst.update(_unpack_small(pk[0], conv_shard))

    return (loss, grad_x, *[out_g[n] for n in _WEIGHTS], *[out_d[n] for n in _WEIGHTS],
            *[out_m[n] for n in _WEIGHTS], *[out_v[n] for n in _WEIGHTS])
```

```python
import functools
import math

import numpy as np
import jax
import jax.numpy as jnp
from jax import lax
from jax.experimental import pallas as pl
from jax.experimental.pallas import tpu as pltpu

F32 = jnp.float32
BF16 = jnp.bfloat16
MXU_DTYPE = BF16

D_MODEL = 1024
HEAD_DIM = 64
N_HEADS = 4
BLK = 128
ATT = 256
QT = 512
CONV_W = 256
PROJ = 3076
PROJ_PAD = 3200
D_FF = 2816
DEPTH = 2
ALPHA = (2 * DEPTH) ** 0.25
LN_EPS = 1e-5
NEG = -1e30
DIL_PATTERNS = ((128, 1), (512, 4), (2048, 16))
REL_BUCKETS = 32
N_CHIPS = 4
N_DEV = 8
SMALL_ROWS = 16

ADAM_LR = 0.001
ADAM_B1 = 0.9
ADAM_B2 = 0.999
ADAM_EPS = 1e-08
ADAM_WD = 0.01
ADAM_STEP = 10

VMEM_LIMIT = 56 * 2 ** 20
MESH = pl.DeviceIdType.MESH


def _cparams(*sem):
    return pltpu.CompilerParams(dimension_semantics=tuple(sem), vmem_limit_bytes=VMEM_LIMIT)


def _dot(a, b):
    return jnp.dot(a.astype(MXU_DTYPE), b.astype(MXU_DTYPE), preferred_element_type=F32)


def _dot_nt(a, b):
    return lax.dot_general(a.astype(MXU_DTYPE), b.astype(MXU_DTYPE), (((1,), (1,)), ((), ())),
                           preferred_element_type=F32)


def _dot_tn(a, b):
    return lax.dot_general(a.astype(MXU_DTYPE), b.astype(MXU_DTYPE), (((0,), (0,)), ((), ())),
                           preferred_element_type=F32)


def _split_dot(x, ones, passes):
    acc, rest = None, x
    for p in range(passes):
        piece = rest.astype(MXU_DTYPE)
        part = jnp.dot(piece, ones, preferred_element_type=F32)
        acc = part if acc is None else acc + part
        if p + 1 < passes:
            rest = rest - piece.astype(F32)
    return acc


def _split_dot_lhs(ones, x, passes):
    acc, rest = None, x
    for p in range(passes):
        piece = rest.astype(MXU_DTYPE)
        part = jnp.dot(ones, piece, preferred_element_type=F32)
        acc = part if acc is None else acc + part
        if p + 1 < passes:
            rest = rest - piece.astype(F32)
    return acc


def _iota2(shape, axis):
    return lax.broadcasted_iota(jnp.int32, shape, axis)


_TILES = {"proj": (1024, 640, 1024), "ffn_out_dx": (1024, 1408, 1024), "ffn_out_dw": (1408, 1024, 2048),
          "ffn_in_dx": (1024, 1024, 1408), "ffn_in_dw": (1024, 1408, 2048), "out_proj_dx": (1024, 1024, 1024),
          "out_proj_dw": (1024, 1024, 2048), "proj_dx": (1024, 512, 3200), "proj_dw": (1024, 640, 2048)}


def _matmul(a, b, kind, tag, *, out_dtype=F32, add=None, add_scale=1.0, trans_a=False, trans_b=False):
    k, m = a.shape if trans_a else a.shape[::-1]
    n = b.shape[0] if trans_b else b.shape[1]
    tm, tn, tk = _TILES[kind]
    tm, tk, name = min(tm, m), min(tk, k), f"{kind}_{tag}"
    assert m % tm == 0 and n % tn == 0 and k % tk == 0, (a.shape, b.shape, tm, tn, tk)
    nk = k // tk

    def body(*refs):
        if add is None:
            a_ref, b_ref, o_ref = refs[:3]
            c_ref, scr = None, refs[3:]
        else:
            a_ref, b_ref, c_ref, o_ref = refs[:4]
            scr = refs[4:]
        dot = _dot_tn if trans_a else _dot_nt if trans_b else _dot
        part = dot(a_ref[...], b_ref[...])

        def finish(acc):
            if c_ref is not None:
                acc = acc + add_scale * c_ref[...]
            o_ref[...] = acc.astype(out_dtype)

        if nk == 1:
            finish(part)
        else:
            acc_ref = scr[0]
            kk = pl.program_id(2)

            @pl.when(kk == 0)
            def _():
                acc_ref[...] = part

            @pl.when(kk > 0)
            def _():
                acc_ref[...] += part

            @pl.when(kk == nk - 1)
            def _():
                finish(acc_ref[...])

    b_spec = pl.BlockSpec((tn, tk), lambda i, j, kk: (j, kk)) if trans_b else pl.BlockSpec((tk, tn), lambda i, j, kk: (kk, j))
    a_spec = pl.BlockSpec((tk, tm), lambda i, j, kk: (kk, i)) if trans_a else pl.BlockSpec((tm, tk), lambda i, j, kk: (i, kk))
    in_specs = [a_spec, b_spec]
    operands = [a, b]
    if add is not None:
        in_specs.append(pl.BlockSpec((tm, tn), lambda i, j, kk: (i, j)))
        operands.append(add)
    return pl.pallas_call(
        body, name=name, grid=(m // tm, n // tn, nk), in_specs=in_specs,
        out_specs=pl.BlockSpec((tm, tn), lambda i, j, kk: (i, j)),
        out_shape=jax.ShapeDtypeStruct((m, n), out_dtype),
        scratch_shapes=[pltpu.VMEM((tm, tn), F32)] if nk > 1 else [],
        compiler_params=_cparams("parallel", "parallel", "arbitrary"),
    )(*operands)


def _matmul_post_norm(a, b, xin, g, beta, name):
    t, k = a.shape
    d = b.shape[1]
    tm = 512

    def body(a_ref, b_ref, x_ref, g_ref, beta_ref, pre_ref, y_ref):
        pre = ALPHA * x_ref[...] + _dot(a_ref[...], b_ref[...])
        xhat, _ = _ln_stats(pre)
        pre_ref[...] = pre
        y_ref[...] = xhat * g_ref[...] + beta_ref[...]

    row = pl.BlockSpec((tm, d), lambda i: (i, 0))
    vec = pl.BlockSpec((1, d), lambda i: (0, 0))
    return pl.pallas_call(
        body, name=name, grid=(t // tm,),
        in_specs=[pl.BlockSpec((tm, k), lambda i: (i, 0)), pl.BlockSpec((k, d), lambda i: (0, 0)), row, vec, vec],
        out_specs=[row, row], out_shape=[jax.ShapeDtypeStruct((t, d), F32)] * 2, compiler_params=_cparams("parallel"),
    )(a, b, xin, g.reshape(1, d), beta.reshape(1, d))


def _ffn_in(x1, w_gu, name, carry=None):
    t, d = x1.shape
    tm, tn = 512, D_FF // 2
    nj = D_FF // tn

    def body(x_ref, wg_ref, wu_ref, gate_ref, up_ref, h_ref):
        xb = x_ref[...].astype(MXU_DTYPE)
        gate = _dot(xb, wg_ref[...])
        up = _dot(xb, wu_ref[...])
        gate_ref[...] = gate
        up_ref[...] = up
        h_ref[...] = (gate * (1.0 / (1.0 + jnp.exp(-gate))) * up).astype(h_ref.dtype)

    out = pl.BlockSpec((tm, tn), lambda i, j: (i, j))
    return _host_call(
        body, carry, name=name, grid=(t // tm, nj),
        in_specs=[pl.BlockSpec((tm, d), lambda i, j: (i, 0)), pl.BlockSpec((d, tn), lambda i, j: (0, j)),
                  pl.BlockSpec((d, tn), lambda i, j: (0, nj + j))],
        out_specs=[out, out, out],
        out_shape=[jax.ShapeDtypeStruct((t, D_FF), F32)] * 2 + [jax.ShapeDtypeStruct((t, D_FF), MXU_DTYPE)],
        operands=(x1, w_gu, w_gu))


def _ln_stats(pre):
    mu = jnp.mean(pre, axis=-1, keepdims=True)
    xc = pre - mu
    var = jnp.mean(xc * xc, axis=-1, keepdims=True)
    rstd = lax.rsqrt(var + LN_EPS)
    return xc * rstd, rstd


def _ln_bwd(dy, pre, g, name):
    t, d = dy.shape
    tile = 256

    def body(dy_ref, pre_ref, g_ref, dpre_ref, dgb_ref):
        dyv = dy_ref[...]
        xhat, rstd = _ln_stats(pre_ref[...])
        dxh = dyv * g_ref[...]
        m1 = jnp.mean(dxh, axis=-1, keepdims=True)
        m2 = jnp.mean(dxh * xhat, axis=-1, keepdims=True)
        dpre_ref[...] = rstd * (dxh - m1 - xhat * m2)

        @pl.when(pl.program_id(0) == 0)
        def _():
            dgb_ref[...] = jnp.zeros_like(dgb_ref)

        dgb_ref[0:1, :] += jnp.sum(dyv * xhat, axis=0, keepdims=True)
        dgb_ref[1:2, :] += jnp.sum(dyv, axis=0, keepdims=True)

    row = pl.BlockSpec((tile, d), lambda i: (i, 0))
    return pl.pallas_call(
        body, name=name, grid=(t // tile,), in_specs=[row, row, pl.BlockSpec((1, d), lambda i: (0, 0))],
        out_specs=[row, pl.BlockSpec((8, d), lambda i: (0, 0))],
        out_shape=[jax.ShapeDtypeStruct((t, d), F32), jax.ShapeDtypeStruct((8, d), F32)],
        compiler_params=_cparams("arbitrary"),
    )(dy, pre, g.reshape(1, d))


def _swiglu_bwd(dh, gate, up, name, carry=None):
    t = dh.shape[0]
    tile = 256

    def body(dh_ref, gate_ref, up_ref, dgu_ref):
        gate = gate_ref[...]
        up = up_ref[...]
        dhv = dh_ref[...]
        sig = 1.0 / (1.0 + jnp.exp(-gate))
        dgu_ref[:, :D_FF] = (dhv * up * sig * (1.0 + gate * (1.0 - sig))).astype(dgu_ref.dtype)
        dgu_ref[:, D_FF:] = (dhv * gate * sig).astype(dgu_ref.dtype)

    return _host_call(
        body, carry, name=name, grid=(t // tile,),
        in_specs=[pl.BlockSpec((tile, D_FF), lambda i: (i, 0))] * 3,
        out_specs=[pl.BlockSpec((tile, 2 * D_FF), lambda i: (i, 0))],
        out_shape=[jax.ShapeDtypeStruct((t, 2 * D_FF), MXU_DTYPE)], operands=(dh, gate, up))


def _loss_kernel(y, target, name):
    t, d = y.shape
    tile = 512

    def body(y_ref, t_ref, dy_ref, l_ref):
        err = y_ref[...] - t_ref[...]
        dy_ref[...] = err * (1.0 / d)

        @pl.when(pl.program_id(0) == 0)
        def _():
            l_ref[...] = jnp.zeros_like(l_ref)

        l_ref[...] += jnp.sum(err * err) * (0.5 / d)

    row = pl.BlockSpec((tile, d), lambda i: (i, 0))
    return pl.pallas_call(
        body, name=name, grid=(t // tile,), in_specs=[row, row],
        out_specs=[row, pl.BlockSpec((8, 128), lambda i: (0, 0))],
        out_shape=[jax.ShapeDtypeStruct((t, d), F32), jax.ShapeDtypeStruct((8, 128), F32)],
        compiler_params=_cparams("arbitrary"),
    )(y, target)


def _adamw(w, g, m, v, name):
    nl, r, c = w.shape
    tr = r
    for cand in (256, 352, 128, 64, 16, 8):
        if r % cand == 0:
            tr = cand
            break

    def body(w_ref, g_ref, m_ref, v_ref, d_ref, nm_ref, nv_ref):
        gv = g_ref[...]
        nm = ADAM_B1 * m_ref[...] + (1.0 - ADAM_B1) * gv
        nv = ADAM_B2 * v_ref[...] + (1.0 - ADAM_B2) * (gv * gv)
        m_hat = nm / (1.0 - ADAM_B1 ** ADAM_STEP)
        v_hat = nv / (1.0 - ADAM_B2 ** ADAM_STEP)
        d_ref[...] = -ADAM_LR * (m_hat / (jnp.sqrt(v_hat) + ADAM_EPS) + ADAM_WD * w_ref[...])
        nm_ref[...] = nm
        nv_ref[...] = nv

    blk = pl.BlockSpec((1, tr, c), lambda l, i: (l, i, 0))
    return pl.pallas_call(
        body, name=name, grid=(nl, r // tr), in_specs=[blk] * 4, out_specs=[blk] * 3,
        out_shape=[jax.ShapeDtypeStruct(w.shape, F32)] * 3, compiler_params=_cparams("parallel", "parallel"),
    )(w, g, m, v)


def _shift_down(u, k, rows):
    return jnp.where(rows >= k, pltpu.roll(u, k, 0), 0.0)


def _shift_up(u, k, rows, s):
    return jnp.where(rows < s - k, pltpu.roll(u, s - k, 0), 0.0)


def _conv_fwd(proj, conv_w, nb, s, name):
    def body(b_ref, c_ref, h_ref, w_ref, o_ref):
        rows = _iota2((s, CONV_W), 0)
        u = c_ref[...] * h_ref[...]
        y = w_ref[2:3, :] * u + w_ref[1:2, :] * _shift_down(u, 1, rows) + w_ref[0:1, :] * _shift_down(u, 2, rows)
        o_ref[...] = b_ref[...] * y

    col = lambda j: pl.BlockSpec((s, CONV_W), lambda b: (b, j))
    return pl.pallas_call(
        body, name=name, grid=(nb,),
        in_specs=[col(9), col(10), col(11), pl.BlockSpec((8, CONV_W), lambda b: (0, 0))],
        out_specs=pl.BlockSpec((s, CONV_W), lambda b: (b, 0)),
        out_shape=jax.ShapeDtypeStruct((nb * s, CONV_W), F32), compiler_params=_cparams("parallel"),
    )(proj, proj, proj, conv_w)


def _conv_bwd(dmixed, proj, conv_w, nb, s, name):
    def body(do_ref, b_ref, c_ref, h_ref, w_ref, dg_ref, dw_ref):
        rows = _iota2((s, CONV_W), 0)
        cg, hg, bg, dout = c_ref[...], h_ref[...], b_ref[...], do_ref[...]
        u = cg * hg
        u1 = _shift_down(u, 1, rows)
        u2 = _shift_down(u, 2, rows)
        y = w_ref[2:3, :] * u + w_ref[1:2, :] * u1 + w_ref[0:1, :] * u2
        dy = dout * bg
        du = w_ref[2:3, :] * dy + w_ref[1:2, :] * _shift_up(dy, 1, rows, s) + w_ref[0:1, :] * _shift_up(dy, 2, rows, s)
        dg_ref[:, 0:CONV_W] = dout * y
        dg_ref[:, CONV_W:2 * CONV_W] = du * hg
        dg_ref[:, 2 * CONV_W:3 * CONV_W] = du * cg

        @pl.when(pl.program_id(0) == 0)
        def _():
            dw_ref[...] = jnp.zeros_like(dw_ref)

        dw_ref[0:1, :] += jnp.sum(dy * u2, axis=0, keepdims=True)
        dw_ref[1:2, :] += jnp.sum(dy * u1, axis=0, keepdims=True)
        dw_ref[2:3, :] += jnp.sum(dy * u, axis=0, keepdims=True)

    col = lambda j: pl.BlockSpec((s, CONV_W), lambda b: (b, j))
    return pl.pallas_call(
        body, name=name, grid=(nb,),
        in_specs=[col(3), col(9), col(10), col(11), pl.BlockSpec((8, CONV_W), lambda b: (0, 0))],
        out_specs=[pl.BlockSpec((s, 3 * CONV_W), lambda b: (b, 0)), pl.BlockSpec((8, CONV_W), lambda b: (0, 0))],
        out_shape=[jax.ShapeDtypeStruct((nb * s, 3 * CONV_W), F32), jax.ShapeDtypeStruct((8, CONV_W), F32)],
        compiler_params=_cparams("arbitrary"),
    )(dmixed, proj, proj, proj, conv_w)


def _col_spec(s, base):
    return pl.BlockSpec((s, BLK), lambda b, p: (b, base + p))


def _qrows(i):
    return pl.ds(pl.multiple_of(i * QT, QT), QT)


def _rows(j):
    return pl.ds(pl.multiple_of(j * ATT, ATT), ATT)


def _keys_upto(i):
    return (i + 1) * (QT // ATT)


def _triangle(keep):
    return keep(_iota2((ATT, ATT), 0), _iota2((ATT, ATT), 1)).astype(MXU_DTYPE)


def _rows128(i):
    return pl.ds(pl.multiple_of(i * BLK, BLK), BLK)


def _log_sigmoid_parts(z):
    e = jnp.exp(-jnp.abs(z))
    l1p = jnp.log(1.0 + e)
    lb = jnp.minimum(z, 0.0) - l1p
    return lb, lb - z, e


def _head_masks():
    lane = _iota2((1, BLK), 1)
    return [(lane >= h * HEAD_DIM) & (lane < (h + 1) * HEAD_DIM) for h in range(2)]


def _split_heads(ref, scr, sels):
    for h, sel in enumerate(sels):
        scr[h] = jnp.where(sel, ref[...], 0.0).astype(MXU_DTYPE)


def _sb_fwd(proj, nb, s, name, carry=None):
    nblk = s // ATT

    def body(q_ref, k_ref, v_ref, o_ref, km, vm):
        sels = _head_masks()
        _split_heads(k_ref, km, sels)
        _split_heads(v_ref, vm, sels)
        rows = _iota2((QT, ATT), 0)
        cols = _iota2((QT, ATT), 1)
        later = _triangle(lambda r, c: r > c)

        def qblock(i, _):
            qi = (q_ref[_qrows(i), :] * 0.125).astype(MXU_DTYPE)

            def kblock(t, state):
                carries, acc = state
                j = _keys_upto(i) - 1 - t
                strict = (cols + (j * ATT - i * QT)) < rows
                out = []
                for h in range(2):
                    z = _dot_nt(qi, km[h, _rows(j), :])
                    lb, lr, _ = _log_sigmoid_parts(z)
                    lr = jnp.where(strict, lr, 0.0)
                    tail = _split_dot(lr, later, 2) + carries[h]
                    a = jnp.where(strict, jnp.exp(lb + tail), 0.0)
                    acc = acc + _dot(a, vm[h, _rows(j), :])
                    out.append(carries[h] + jnp.sum(lr, axis=-1, keepdims=True))
                return tuple(out), acc

            init = ((jnp.zeros((QT, 1), F32),) * 2, jnp.zeros((QT, BLK), F32))
            _, acc = lax.fori_loop(0, _keys_upto(i), kblock, init)
            o_ref[_qrows(i), :] = acc
            return 0

        lax.fori_loop(0, s // QT, qblock, 0)

    (o,), extra = _host_call(
        body, carry, name=name, grid=(nb, 2), in_specs=[_col_spec(s, 0), _col_spec(s, 2), _col_spec(s, 4)],
        out_specs=[_col_spec(s, 0)], out_shape=[jax.ShapeDtypeStruct((nb * s, 2 * BLK), F32)],
        scratch_shapes=[pltpu.VMEM((2, s, BLK), MXU_DTYPE)] * 2, operands=(proj, proj, proj))
    return o, extra


def _sb_bwd(proj, dmixed, nb, s, name, carry=None):
    nblk = s // ATT

    def body(q_ref, k_ref, v_ref, do_ref, dq_ref, dk_ref, dv_ref, km, vm, a_scr, dl_scr, beta_scr):
        sels = _head_masks()
        _split_heads(k_ref, km, sels)
        _split_heads(v_ref, vm, sels)
        rows = _iota2((QT, ATT), 0)
        cols = _iota2((QT, ATT), 1)
        later = _triangle(lambda r, c: r > c)
        earlier = _triangle(lambda r, c: r < c)
        dk_ref[...] = jnp.zeros_like(dk_ref)
        dv_ref[...] = jnp.zeros_like(dv_ref)

        def qblock(i, _):
            qi = (q_ref[_qrows(i), :] * 0.125).astype(MXU_DTYPE)
            doi = do_ref[_qrows(i), :].astype(MXU_DTYPE)
            qm = [jnp.where(sel, qi, 0.0) for sel in sels]
            dom = [jnp.where(sel, doi, 0.0) for sel in sels]

            def first(t, carries):
                j = _keys_upto(i) - 1 - t
                strict = (cols + (j * ATT - i * QT)) < rows
                out = []
                for h in range(2):
                    z = _dot_nt(qi, km[h, _rows(j), :])
                    lb, lr, e = _log_sigmoid_parts(z)
                    lr = jnp.where(strict, lr, 0.0)
                    tail = _split_dot(lr, later, 2) + carries[h]
                    a = jnp.where(strict, jnp.exp(lb + tail), 0.0)
                    a_scr[h, j] = a
                    dl_scr[h, j] = a * _dot_nt(doi, vm[h, _rows(j), :])
                    beta_scr[h, j] = jnp.exp(lb)
                    out.append(carries[h] + jnp.sum(lr, axis=-1, keepdims=True))
                return tuple(out)

            lax.fori_loop(0, _keys_upto(i), first, (jnp.zeros((QT, 1), F32),) * 2)

            def second(j, state):
                csums, dq = state
                strict = (cols + (j * ATT - i * QT)) < rows
                out = []
                for h in range(2):
                    dl = dl_scr[h, j]
                    beta = beta_scr[h, j]
                    before = _split_dot(dl, earlier, 2) + csums[h]
                    dz = jnp.where(strict, dl * (1.0 - beta) - beta * before, 0.0).astype(MXU_DTYPE)
                    dq = dq + _dot(dz, km[h, _rows(j), :])
                    dk_ref[_rows(j), :] += _dot_tn(dz, qm[h])
                    dv_ref[_rows(j), :] += _dot_tn(a_scr[h, j], dom[h])
                    out.append(csums[h] + jnp.sum(dl, axis=-1, keepdims=True))
                return tuple(out), dq

            init = ((jnp.zeros((QT, 1), F32),) * 2, jnp.zeros((QT, BLK), F32))
            _, dq = lax.fori_loop(0, _keys_upto(i), second, init)
            dq_ref[_qrows(i), :] = dq * 0.125
            return 0

        lax.fori_loop(0, s // QT, qblock, 0)

    out = _col_spec(s, 0)
    return _host_call(
        body, carry, name=name, grid=(nb, 2),
        in_specs=[_col_spec(s, 0), _col_spec(s, 2), _col_spec(s, 4), out], out_specs=[out] * 3,
        out_shape=[jax.ShapeDtypeStruct((nb * s, 2 * BLK), F32)] * 3,
        scratch_shapes=[pltpu.VMEM((2, s, BLK), MXU_DTYPE)] * 2 + [pltpu.VMEM((2, nblk, QT, ATT), F32)] * 3,
        operands=(proj, proj, proj, dmixed))


def _pair_spec(s, width):
    return pl.BlockSpec((None, 2, s, width), lambda b, p: (b, p, 0, 0))


def _fox_fwd(proj, ccol, crow, nb, s, name, carry=None):
    nblk = s // ATT

    def body(q_ref, k_ref, v_ref, cc_ref, cr_ref, o_ref, lse_ref, km, vm):
        sels = _head_masks()
        _split_heads(k_ref, km, sels)
        _split_heads(v_ref, vm, sels)
        rows = _iota2((QT, ATT), 0)
        cols = _iota2((QT, ATT), 1)

        def qblock(i, _):
            qi = (q_ref[_qrows(i), :] * 0.125).astype(MXU_DTYPE)
            ci = [cc_ref[h, _qrows(i), :] for h in range(2)]

            def kblock(j, state):
                ms, ls, acc = state
                causal = (cols + (j * ATT - i * QT)) <= rows
                new_m, new_l, scales, parts = [], [], [], []
                for h in range(2):
                    z = _dot_nt(qi, km[h, _rows(j), :]) + (ci[h] - cr_ref[h, j][0:1, :])
                    z = jnp.where(causal, z, NEG)
                    m_new = jnp.maximum(ms[h], jnp.max(z, axis=-1, keepdims=True))
                    p = jnp.exp(z - m_new)
                    scale = jnp.exp(ms[h] - m_new)
                    new_m.append(m_new)
                    new_l.append(scale * ls[h] + jnp.sum(p, axis=-1, keepdims=True))
                    scales.append(scale)
                    parts.append(_dot(p, vm[h, _rows(j), :]))
                acc = jnp.where(sels[0], scales[0], scales[1]) * acc + parts[0] + parts[1]
                return tuple(new_m), tuple(new_l), acc

            init = ((jnp.full((QT, 1), NEG, F32),) * 2, (jnp.zeros((QT, 1), F32),) * 2, jnp.zeros((QT, BLK), F32))
            ms, ls, acc = lax.fori_loop(0, _keys_upto(i), kblock, init)
            o_ref[_qrows(i), :] = acc / jnp.where(sels[0], ls[0], ls[1])
            for h in range(2):
                lse_ref[h, _qrows(i), :] = jnp.broadcast_to(ms[h] + jnp.log(ls[h]), (QT, ATT))
            return 0

        lax.fori_loop(0, s // QT, qblock, 0)

    crow_spec = pl.BlockSpec((None, 2, nblk, 8, ATT), lambda b, p: (b, p, 0, 0, 0))
    return _host_call(
        body, carry, name=name, grid=(nb, 2),
        in_specs=[_col_spec(s, 12), _col_spec(s, 14), _col_spec(s, 16), _pair_spec(s, ATT), crow_spec],
        out_specs=[_col_spec(s, 0), _pair_spec(s, ATT)],
        out_shape=[jax.ShapeDtypeStruct((nb * s, 2 * BLK), F32), jax.ShapeDtypeStruct((nb, N_HEADS, s, ATT), F32)],
        scratch_shapes=[pltpu.VMEM((2, s, BLK), MXU_DTYPE)] * 2, operands=(proj, proj, proj, ccol, crow))


def _fox_bwd(proj, dmixed, lse, ccol, crow, nb, s, name, carry=None):
    nblk = s // ATT

    def body(q_ref, k_ref, v_ref, do_ref, lse_ref, cc_ref, cr_ref, dq_ref, dk_ref, dv_ref, dc_ref, km, vm, p_scr, dp_scr):
        sels = _head_masks()
        _split_heads(k_ref, km, sels)
        _split_heads(v_ref, vm, sels)
        rows = _iota2((QT, ATT), 0)
        cols = _iota2((QT, ATT), 1)
        dk_ref[...] = jnp.zeros_like(dk_ref)
        dv_ref[...] = jnp.zeros_like(dv_ref)
        dc_ref[...] = jnp.zeros_like(dc_ref)

        def qblock(i, _):
            qi = (q_ref[_qrows(i), :] * 0.125).astype(MXU_DTYPE)
            doi = do_ref[_qrows(i), :].astype(MXU_DTYPE)
            qm = [jnp.where(sel, qi, 0.0) for sel in sels]
            dom = [jnp.where(sel, doi, 0.0) for sel in sels]
            ci = [cc_ref[h, _qrows(i), :] for h in range(2)]
            lsei = [lse_ref[h, _qrows(i), :] for h in range(2)]

            def probs(j, h):
                z = _dot_nt(qi, km[h, _rows(j), :]) + (ci[h] - cr_ref[h, j][0:1, :])
                p = jnp.where((cols + (j * ATT - i * QT)) <= rows, jnp.exp(z - lsei[h]), 0.0)
                return p, _dot_nt(doi, vm[h, _rows(j), :])

            def row_term(j, accs):
                out = []
                for h in range(2):
                    p, dp = probs(j, h)
                    p_scr[h, j] = p
                    dp_scr[h, j] = dp
                    out.append(accs[h] + jnp.sum(p * dp, axis=-1, keepdims=True))
                return tuple(out)

            di = lax.fori_loop(0, _keys_upto(i), row_term, (jnp.zeros((QT, 1), F32),) * 2)

            def kblock(j, dq):
                for h in range(2):
                    p = p_scr[h, j]
                    ds = p * (dp_scr[h, j] - di[h])
                    dc_ref[h, j] += jnp.broadcast_to(jnp.sum(ds, axis=0, keepdims=True), (8, ATT))
                    ds = ds.astype(MXU_DTYPE)
                    dk_ref[_rows(j), :] += _dot_tn(ds, qm[h])
                    dv_ref[_rows(j), :] += _dot_tn(p, dom[h])
                    dq = dq + _dot(ds, km[h, _rows(j), :])
                return dq

            dq = lax.fori_loop(0, _keys_upto(i), kblock, jnp.zeros((QT, BLK), F32))
            dq_ref[_qrows(i), :] = dq * 0.125
            return 0

        lax.fori_loop(0, s // QT, qblock, 0)

    crow_spec = pl.BlockSpec((None, 2, nblk, 8, ATT), lambda b, p: (b, p, 0, 0, 0))
    wide, cols_out = _pair_spec(s, ATT), _col_spec(s, 0)
    return _host_call(
        body, carry, name=name, grid=(nb, 2),
        in_specs=[_col_spec(s, 12), _col_spec(s, 14), _col_spec(s, 16), _col_spec(s, 4), wide, wide, crow_spec],
        out_specs=[cols_out, cols_out, cols_out, crow_spec],
        out_shape=[jax.ShapeDtypeStruct((nb * s, 2 * BLK), F32)] * 3 + [jax.ShapeDtypeStruct((nb, N_HEADS, nblk, 8, ATT), F32)],
        scratch_shapes=[pltpu.VMEM((2, s, BLK), MXU_DTYPE)] * 2 + [pltpu.VMEM((2, nblk, QT, ATT), F32)] * 2,
        operands=(proj, proj, proj, dmixed, lse, ccol, crow))


def _fox_gates_fwd(proj, f_bias, nb, s, name):
    chunk = 256

    def body(f_ref, b_ref, c_ref):
        lower = (_iota2((chunk, chunk), 0) >= _iota2((chunk, chunk), 1)).astype(MXU_DTYPE)
        carry = jnp.zeros((1, BLK), F32)
        for n in range(s // chunk):
            rows = pl.ds(n * chunk, chunk)
            lf, _, _ = _log_sigmoid_parts(f_ref[rows, :] + b_ref[0:1, :])
            c = _split_dot_lhs(lower, lf, 3) + carry
            c_ref[rows, :] = c
            carry = c[chunk - 1:chunk, :]

    return pl.pallas_call(
        body, name=name, grid=(nb,),
        in_specs=[pl.BlockSpec((s, BLK), lambda b: (b, (PROJ_PAD - BLK) // BLK)), pl.BlockSpec((8, BLK), lambda b: (0, 0))],
        out_specs=pl.BlockSpec((s, BLK), lambda b: (b, 0)),
        out_shape=jax.ShapeDtypeStruct((nb * s, BLK), F32), compiler_params=_cparams("parallel"),
    )(proj, f_bias)


def _fox_gates_bwd(dc, proj, f_bias, nb, s, name):
    chunk = 256

    def body(dc_ref, f_ref, b_ref, df_ref, db_ref):
        upper = (_iota2((chunk, chunk), 0) <= _iota2((chunk, chunk), 1)).astype(MXU_DTYPE)
        carry = jnp.zeros((1, BLK), F32)
        total = jnp.zeros((1, BLK), F32)
        for n in reversed(range(s // chunk)):
            rows = pl.ds(n * chunk, chunk)
            dlf = _split_dot_lhs(upper, dc_ref[rows, :], 3) + carry
            carry = dlf[0:1, :]
            pre = f_ref[rows, :] + b_ref[0:1, :]
            e = jnp.exp(-jnp.abs(pre))
            df = dlf * (jnp.where(pre >= 0.0, e, 1.0) / (1.0 + e))
            df_ref[rows, :] = df
            total = total + jnp.sum(df, axis=0, keepdims=True)

        @pl.when(pl.program_id(0) == 0)
        def _():
            db_ref[...] = jnp.zeros_like(db_ref)

        db_ref[0:1, :] += total

    return pl.pallas_call(
        body, name=name, grid=(nb,),
        in_specs=[pl.BlockSpec((s, BLK), lambda b: (b, 0)), pl.BlockSpec((s, BLK), lambda b: (b, (PROJ_PAD - BLK) // BLK)),
                  pl.BlockSpec((8, BLK), lambda b: (0, 0))],
        out_specs=[pl.BlockSpec((s, BLK), lambda b: (b, 0)), pl.BlockSpec((8, BLK), lambda b: (0, 0))],
        out_shape=[jax.ShapeDtypeStruct((nb * s, BLK), F32), jax.ShapeDtypeStruct((8, BLK), F32)],
        compiler_params=_cparams("arbitrary"),
    )(dc, proj, f_bias)


def _delta_kernel(dmixed, o, nb, s, name):
    def body(do_ref, o_ref, d_ref):
        prod = do_ref[...] * o_ref[...]
        for h, sel in enumerate(_head_masks()):
            d_ref[h] = jnp.broadcast_to(jnp.sum(jnp.where(sel, prod, 0.0), axis=-1, keepdims=True), (s, BLK))

    return pl.pallas_call(
        body, name=name, grid=(nb, 2), in_specs=[_col_spec(s, 2), _col_spec(s, 0)], out_specs=_pair_spec(s, BLK),
        out_shape=jax.ShapeDtypeStruct((nb, N_HEADS, s, BLK), F32), compiler_params=_cparams("parallel", "parallel"),
    )(dmixed, o)


def _t5_bucket_np(dist):
    max_exact = REL_BUCKETS // 2
    nf = np.maximum(dist, 1).astype(np.float32)
    large = max_exact + (np.log(nf / max_exact) / math.log(2048 / max_exact) * (REL_BUCKETS - max_exact)).astype(np.int32)
    large = np.minimum(large, REL_BUCKETS - 1)
    return np.where(dist < max_exact, dist, large)


def _bucket_table():
    qi = np.arange(BLK)[:, None]
    kj = np.arange(2 * BLK)[None, :]
    dist = qi + BLK - kj
    tables = []
    for window, dil in DIL_PATTERNS:
        in_band = (dist >= 0) & (dist <= window // dil)
        tables.append(np.where(in_band, _t5_bucket_np(np.maximum(dist, 0) * dil), -1).astype(np.int32))
    return np.stack(tables)


def _dil_scores(qb, kp, kc, b_ref, h, prev_valid):
    zp = _dot_nt(qb, kp) + b_ref[h, :, 0:BLK]
    zp = jnp.where(prev_valid, zp, NEG)
    zc = _dot_nt(qb, kc) + b_ref[h, :, BLK:2 * BLK]
    return zp, zc


def _residue_rows(b, seg, dil):
    if dil == 1:
        return _rows128(b), _rows128(jnp.maximum(b - 1, 0)), b > 0
    r, n = b // seg, b % seg
    cur = pl.ds(r + dil * n * BLK, BLK, stride=dil)
    prev = pl.ds(r + dil * jnp.maximum(n - 1, 0) * BLK, BLK, stride=dil)
    return cur, prev, n > 0


def _dil_attention_fwd(proj, bias, nb, s, name, carry=None):
    nblk = s // BLK

    def body(q_ref, k_ref, v_ref, b_ref, out_ref, lse_ref, o_scr, l_scr):
        sels = _head_masks()
        for p, (_, dil) in enumerate(DIL_PATTERNS):
            seg = s // dil // BLK

            def block(b, _, p=p, seg=seg, dil=dil):
                cur, prev, has_prev = _residue_rows(b, seg, dil)
                qb = (q_ref[cur, :] * 0.125).astype(MXU_DTYPE)
                kp, kc = k_ref[prev, :].astype(MXU_DTYPE), k_ref[cur, :].astype(MXU_DTYPE)
                vp, vc = v_ref[prev, :].astype(MXU_DTYPE), v_ref[cur, :].astype(MXU_DTYPE)
                acc = jnp.zeros((BLK, BLK), F32)
                for h, sel in enumerate(sels):
                    zp, zc = _dil_scores(qb, jnp.where(sel, kp, 0.0), jnp.where(sel, kc, 0.0), b_ref.at[p], h, has_prev)
                    m = jnp.maximum(jnp.max(zp, axis=-1, keepdims=True), jnp.max(zc, axis=-1, keepdims=True))
                    pp = jnp.exp(zp - m)
                    pc = jnp.exp(zc - m)
                    den = jnp.sum(pp, axis=-1, keepdims=True) + jnp.sum(pc, axis=-1, keepdims=True)
                    acc = acc + (_dot(pp, jnp.where(sel, vp, 0.0)) + _dot(pc, jnp.where(sel, vc, 0.0))) / den
                    l_scr[p, h, cur, :] = jnp.broadcast_to(m + jnp.log(den), (BLK, BLK))
                o_scr[p, cur, :] = acc
                return 0

            lax.fori_loop(0, nblk, block, 0, unroll=4)

        weights, dens = [], []
        for h in range(2):
            m = jnp.maximum(jnp.maximum(l_scr[0, h], l_scr[1, h]), l_scr[2, h])
            w = [jnp.exp(l_scr[p, h] - m) for p in range(3)]
            den = w[0] + w[1] + w[2]
            lse_ref[h] = m + jnp.log(den)
            weights.append(w)
            dens.append(den)
        num = sum(jnp.where(sels[0], weights[0][p], weights[1][p]) * o_scr[p] for p in range(3))
        out_ref[...] = num / jnp.where(sels[0], dens[0], dens[1])

    bias_spec = pl.BlockSpec((3, 2, BLK, 2 * BLK), lambda b, p: (0, p, 0, 0))
    return _host_call(
        body, carry, name=name, grid=(nb, 2), in_specs=[_col_spec(s, 6), _col_spec(s, 8), _col_spec(s, 10), bias_spec],
        out_specs=[_col_spec(s, 0), _pair_spec(s, BLK)],
        out_shape=[jax.ShapeDtypeStruct((nb * s, 2 * BLK), F32), jax.ShapeDtypeStruct((nb, N_HEADS, s, BLK), F32)],
        scratch_shapes=[pltpu.VMEM((3, s, BLK), F32), pltpu.VMEM((3, 2, s, BLK), F32)], operands=(proj, proj, proj, bias))


def _dil_attention_bwd(proj, dmixed, lse, delta, bias, nb, s, name, carry=None):
    nblk = s // BLK

    def body(q_ref, k_ref, v_ref, do_ref, lse_ref, dl_ref, b_ref, dq_ref, dk_ref, dv_ref, g_ref):
        sels = _head_masks()
        dq_ref[...] = jnp.zeros_like(dq_ref)
        dk_ref[...] = jnp.zeros_like(dk_ref)
        dv_ref[...] = jnp.zeros_like(dv_ref)
        g_ref[...] = jnp.zeros_like(g_ref)
        for p, (_, dil) in enumerate(DIL_PATTERNS):
            seg = s // dil // BLK

            def block(b, _, p=p, seg=seg, dil=dil):
                cur, prev, has_prev = _residue_rows(b, seg, dil)
                qb = (q_ref[cur, :] * 0.125).astype(MXU_DTYPE)
                dob = do_ref[cur, :].astype(MXU_DTYPE)
                kp, kc = k_ref[prev, :].astype(MXU_DTYPE), k_ref[cur, :].astype(MXU_DTYPE)
                vp, vc = v_ref[prev, :].astype(MXU_DTYPE), v_ref[cur, :].astype(MXU_DTYPE)
                dq = jnp.zeros((BLK, BLK), F32)
                dkp, dkc, dvp, dvc = dq, dq, dq, dq
                for h, sel in enumerate(sels):
                    kph, kch = jnp.where(sel, kp, 0.0), jnp.where(sel, kc, 0.0)
                    qh, doh = jnp.where(sel, qb, 0.0), jnp.where(sel, dob, 0.0)
                    lse_h = lse_ref[h, cur, :]
                    dlt = dl_ref[h, cur, :]
                    zp, zc = _dil_scores(qb, kph, kch, b_ref.at[p], h, has_prev)
                    pp = jnp.exp(zp - lse_h)
                    pc = jnp.exp(zc - lse_h)
                    dsp = pp * (_dot_nt(dob, jnp.where(sel, vp, 0.0)) - dlt)
                    dsc = pc * (_dot_nt(dob, jnp.where(sel, vc, 0.0)) - dlt)
                    g_ref[h, p, :, 0:BLK] += dsp
                    g_ref[h, p, :, BLK:2 * BLK] += dsc
                    dsp = dsp.astype(MXU_DTYPE)
                    dsc = dsc.astype(MXU_DTYPE)
                    dq = dq + _dot(dsp, kph) + _dot(dsc, kch)
                    dkp, dkc = dkp + _dot_tn(dsp, qh), dkc + _dot_tn(dsc, qh)
                    dvp, dvc = dvp + _dot_tn(pp, doh), dvc + _dot_tn(pc, doh)
                dq_ref[cur, :] += dq * 0.125
                dk_ref[prev, :] += dkp
                dk_ref[cur, :] += dkc
                dv_ref[prev, :] += dvp
                dv_ref[cur, :] += dvc
                return 0

            lax.fori_loop(0, nblk, block, 0, unroll=4)

    bias_spec = pl.BlockSpec((3, 2, BLK, 2 * BLK), lambda b, p: (0, p, 0, 0))
    cols, stats = _col_spec(s, 0), _pair_spec(s, BLK)
    return _host_call(
        body, carry, name=name, grid=(nb, 2),
        in_specs=[_col_spec(s, 6), _col_spec(s, 8), _col_spec(s, 10), _col_spec(s, 2), stats, stats, bias_spec],
        out_specs=[cols, cols, cols, pl.BlockSpec((None, 2, 3, BLK, 2 * BLK), lambda b, p: (b, p, 0, 0, 0))],
        out_shape=[jax.ShapeDtypeStruct((nb * s, 2 * BLK), F32)] * 3 + [jax.ShapeDtypeStruct((nb, N_HEADS, 3, BLK, 2 * BLK), F32)],
        operands=(proj, proj, proj, dmixed, lse, delta, bias))


def _bucket_reduce(gbias, table, name):
    nb = gbias.shape[0]

    def body(g_ref, t_ref, o_ref):
        row = _iota2((8, BLK), 0)
        lane = _iota2((8, BLK), 1)
        gsum = [[sum(g_ref[b, h, p] for b in range(nb)) for p in range(3)] for h in range(N_HEADS)]

        def bucket(k, acc):
            for h in range(N_HEADS):
                tot = sum(jnp.sum(jnp.where(t_ref[p] == k, gsum[h][p], 0.0)) for p in range(3))
                acc = acc + jnp.where((row == h) & (lane == k), tot, 0.0)
            return acc

        o_ref[...] = lax.fori_loop(0, REL_BUCKETS, bucket, jnp.zeros((8, BLK), F32))

    vm = pl.BlockSpec(memory_space=pltpu.VMEM)
    return pl.pallas_call(
        body, name=name, in_specs=[vm, vm], out_specs=vm, out_shape=jax.ShapeDtypeStruct((8, BLK), F32),
        compiler_params=pltpu.CompilerParams(vmem_limit_bytes=VMEM_LIMIT),
    )(gbias, table)


def _place():
    x, y, c = lax.axis_index("x"), lax.axis_index("y"), lax.axis_index("c")
    others = [(1 - x, y), (x, 1 - y), (1 - x, 1 - y)]
    return x, y, c, others


def _remote(src, dst, send_sem, recv_sem, to):
    return pltpu.make_async_remote_copy(src_ref=src, dst_ref=dst, send_sem=send_sem, recv_sem=recv_sem,
                                        device_id=to, device_id_type=MESH)


_HBM = pl.BlockSpec(memory_space=pl.ANY)


class _Exchange:
    def __init__(self, operands, out_shape, n_copies, copies, aliases=None):
        self.operands, self.out_shape, self.n_copies, self.copies = list(operands), list(out_shape), n_copies, copies
        self.aliases = dict(aliases or {})

    def sem_shapes(self):
        return [pltpu.SemaphoreType.DMA((self.n_copies,)), pltpu.SemaphoreType.DMA((self.n_copies,))]


def _start_all(sends):
    for cp in sends:
        cp.start()


def _wait_all(sends, arrivals):
    for cp in arrivals:
        cp.wait_recv()
    for cp in sends:
        cp.wait_send()


def _run_exchange(ex, name):
    ni = len(ex.operands)

    def body(*refs):
        sends, arrivals = ex.copies(refs[:ni], refs[ni:-2], refs[-2], refs[-1])
        _start_all(sends)
        _wait_all(sends, arrivals)

    return list(pl.pallas_call(
        body, name=name, in_specs=[_HBM] * ni, out_specs=[_HBM] * len(ex.out_shape), out_shape=ex.out_shape,
        scratch_shapes=ex.sem_shapes(), input_output_aliases=ex.aliases)(*ex.operands))


def _host_call(body, carry, *, name, grid, in_specs, out_specs, out_shape, operands, scratch_shapes=()):
    in_specs, out_specs, out_shape, scratch_shapes = list(in_specs), list(out_specs), list(out_shape), list(scratch_shapes)
    if carry is None:
        res = pl.pallas_call(body, name=name, grid=grid, in_specs=in_specs, out_specs=out_specs, out_shape=out_shape,
                             scratch_shapes=scratch_shapes, compiler_params=_cparams(*["parallel"] * len(grid)))(*operands)
        return list(res), []
    n_in, n_out, n_scr, c_in, c_out = len(in_specs), len(out_specs), len(scratch_shapes), len(carry.operands), len(carry.out_shape)
    steps = math.prod(grid)

    def wrapped(*refs):
        ins, refs = refs[:n_in], refs[n_in:]
        c_ins, refs = refs[:c_in], refs[c_in:]
        outs, refs = refs[:n_out], refs[n_out:]
        c_outs, refs = refs[:c_out], refs[c_out:]
        scr, (send_sems, recv_sems) = refs[:n_scr], refs[n_scr:]
        step = 0
        for d, size in enumerate(grid):
            step = step * size + pl.program_id(d)

        @pl.when(step == 0)
        def _():
            _start_all(carry.copies(c_ins, c_outs, send_sems, recv_sems)[0])

        body(*ins, *outs, *scr)

        @pl.when(step == steps - 1)
        def _():
            _wait_all(*carry.copies(c_ins, c_outs, send_sems, recv_sems))

    res = pl.pallas_call(
        wrapped, name=name, grid=grid, in_specs=in_specs + [_HBM] * c_in, out_specs=out_specs + [_HBM] * c_out,
        out_shape=out_shape + carry.out_shape, scratch_shapes=scratch_shapes + carry.sem_shapes(),
        input_output_aliases={n_in + i: n_out + j for i, j in carry.aliases.items()},
        compiler_params=_cparams(*["arbitrary"] * len(grid)))(*operands, *carry.operands)
    return list(res[:n_out]), list(res[n_out:])


def _half(which, rows):
    h = rows // 2
    return pl.ds(pl.multiple_of(which * h, 16), h)


def _like(arrays, shape_of=lambda t: t.shape):
    return [jax.ShapeDtypeStruct(shape_of(t), t.dtype) for t in arrays]


def _gather_ici(shards, layer):
    n = len(shards)

    def copies(ins, outs, send_sems, recv_sems, base=0):
        x, y, c, others = _place()
        me = 2 * x + y
        sends, arrivals = [], []
        for a in range(n):
            rows = _half(c, shards[a].shape[1])
            for k, (ox, oy) in enumerate(others):
                sems = (send_sems.at[base + 3 * a + k], recv_sems.at[base + 3 * a + k],(ox, oy, c))
                sends.append(_remote(ins[a].at[layer, rows], outs[a].at[me, rows], *sems))
                landed = outs[a].at[2 * ox + oy, rows]
                arrivals.append(_remote(landed, landed, *sems))
        return sends, arrivals

    return _Exchange(shards, _like(shards, lambda t: (N_CHIPS,) + t.shape[1:]), 3 * n, copies)


def _gather_d2d(gathered):
    n = len(gathered)

    def copies(ins, outs, send_sems, recv_sems, base=0):
        x, y, c, others = _place()
        sends, arrivals = [], []
        for a in range(n):
            r = gathered[a].shape[1]
            for k, (ox, oy) in enumerate(others):
                sems = (send_sems.at[base + 3 * a + k], recv_sems.at[base + 3 * a + k],(x, y, 1 - c))
                mine, theirs = outs[a].at[2 * ox + oy, _half(c, r)], outs[a].at[2 * ox + oy, _half(1 - c, r)]
                sends.append(_remote(mine, mine, *sems))
                arrivals.append(_remote(theirs, theirs, *sems))
        return sends, arrivals

    return _Exchange(gathered, _like(gathered), 3 * n, copies, aliases={a: a for a in range(n)})


def _swap_halves(g):
    n = len(g)

    def copies(ins, outs, send_sems, recv_sems, base=0):
        x, y, c, _ = _place()
        sends, arrivals = [], []
        for a in range(n):
            sems = (send_sems.at[base + a], recv_sems.at[base + a], (x, y, 1 - c))
            sends.append(_remote(ins[a].at[:, _half(1 - c, g[a].shape[1])], outs[a], *sems))
            arrivals.append(_remote(outs[a], outs[a], *sems))
        return sends, arrivals

    return _Exchange(g, _like(g, lambda t: (t.shape[0], t.shape[1] // 2, t.shape[2])), n, copies)


def _scatter_shards(ps):
    n = len(ps)

    def copies(ins, outs, send_sems, recv_sems, base=0):
        x, y, c, others = _place()
        me = 2 * x + y
        sends, arrivals = [], []
        for a in range(n):
            for k, (ox, oy) in enumerate(others):
                sems = (send_sems.at[base + 3 * a + k], recv_sems.at[base + 3 * a + k],(ox, oy, c))
                sends.append(_remote(ins[a].at[2 * ox + oy], outs[a].at[me], *sems))
                slot = outs[a].at[2 * ox + oy]
                arrivals.append(_remote(slot, slot, *sems))
        return sends, arrivals

    return _Exchange(ps, _like(ps), 3 * n, copies)


def _share_halves(mine):
    n = len(mine)

    def copies(ins, outs, send_sems, recv_sems, base=0):
        x, y, c, _ = _place()
        sends, arrivals = [], []
        for a in range(n):
            sems = (send_sems.at[base + a], recv_sems.at[base + a], (x, y, 1 - c))
            sends.append(_remote(ins[a], outs[a], *sems))
            arrivals.append(_remote(outs[a], outs[a], *sems))
        return sends, arrivals

    return _Exchange(mine, _like(mine), n, copies)


def _row_tile(r):
    for cand in (256, 352, 128):
        if r % cand == 0:
            return cand
    return r


def _pair_sum(g, other, core, name):
    ns, h, w = other.shape
    tr = _row_tile(h)
    per_half = h // tr

    def body(core_ref, g_ref, o_ref, out_ref):
        out_ref[...] = (g_ref[...] + o_ref[...]).astype(out_ref.dtype)

    blk = pl.BlockSpec((None, tr, w), lambda k, i, core_ref: (k, i, 0))
    grid_spec = pltpu.PrefetchScalarGridSpec(
        num_scalar_prefetch=1, grid=(ns, per_half),
        in_specs=[pl.BlockSpec((None, tr, w), lambda k, i, core_ref: (k, core_ref[0] * per_half + i, 0)), blk], out_specs=blk)
    return pl.pallas_call(
        body, name=name, grid_spec=grid_spec, out_shape=jax.ShapeDtypeStruct((ns, h, w), MXU_DTYPE),
        compiler_params=_cparams("parallel", "parallel"),
    )(core.reshape(1).astype(jnp.int32), g, other)


def _chip_sum(q, p, chip, name):
    ns, r, w = q.shape
    tr = _row_tile(r)

    def body(chip_ref, q_ref, own_ref, out_ref):
        me = chip_ref[0]
        own = own_ref[...].astype(F32)
        terms = [jnp.where(me == k, own, q_ref[k].astype(F32)) for k in range(ns)]
        out_ref[...] = ((terms[0] + terms[1]) + terms[2]) + terms[3]

    grid_spec = pltpu.PrefetchScalarGridSpec(
        num_scalar_prefetch=1, grid=(r // tr,),
        in_specs=[pl.BlockSpec((ns, tr, w), lambda i, chip_ref: (0, i, 0)),
                  pl.BlockSpec((None, tr, w), lambda i, chip_ref: (chip_ref[0], i, 0))],
        out_specs=pl.BlockSpec((tr, w), lambda i, chip_ref: (i, 0)))
    return pl.pallas_call(
        body, name=name, grid_spec=grid_spec, out_shape=jax.ShapeDtypeStruct((r, w), F32),
        compiler_params=_cparams("parallel"),
    )(chip.reshape(1).astype(jnp.int32), q, p)


def _merge(exchanges):
    if len(exchanges) <= 1:
        return exchanges[0] if exchanges else None
    operands, out_shape, aliases, spans, n = [], [], {}, [], 0
    for ex in exchanges:
        spans.append((len(operands), len(out_shape), n))
        aliases.update({len(operands) + i: len(out_shape) + j for i, j in ex.aliases.items()})
        operands += ex.operands
        out_shape += ex.out_shape
        n += ex.n_copies

    def copies(ins, outs, send_sems, recv_sems, base=0):
        sends, arrivals = [], []
        for ex, (i0, o0, s0) in zip(exchanges, spans):
            s, a = ex.copies(ins[i0:i0 + len(ex.operands)], outs[o0:o0 + len(ex.out_shape)], send_sems, recv_sems, base + s0)
            sends += s
            arrivals += a
        return sends, arrivals

    return _Exchange(operands, out_shape, n, copies, aliases)


def _take(hooks, host):
    stages = (hooks or {}).pop(host, [])
    exchanges = [make() for make, _ in stages]

    def finish(results):
        for (_, done), ex in zip(stages, exchanges):
            done(results[:len(ex.out_shape)])
            results = results[len(ex.out_shape):]

    return _merge(exchanges), finish


def _hook(hooks, host, make, done):
    hooks.setdefault(host, []).append((make, done))


class _WeightPrefetch:
    def __init__(self, names, shards, layer, chip):
        self.names, self.shards, self.layer, self.chip, self.result = names, [shards[n] for n in names], layer, chip, None

    def first(self):
        return _gather_ici(self.shards, self.layer)

    def got_first(self, arrived):
        self.arrived = arrived

    def second(self):
        return _gather_d2d(self.arrived)

    def got_second(self, gathered):
        self.result = {name: lax.dynamic_update_index_in_dim(got, own[self.layer], self.chip, 0)
                       for name, got, own in zip(self.names, gathered, self.shards)}

    def ride(self, hooks, first_host, second_host):
        _hook(hooks, first_host, self.first, self.got_first)
        _hook(hooks, second_host, self.second, self.got_second)

    def run(self, tag):
        self.got_first(_run_exchange(self.first(), f"gather_ici_{tag}"))
        self.got_second(_run_exchange(self.second(), f"gather_d2d_{tag}"))


class _GradReduce:
    def __init__(self, g, chip, core, tag):
        self.names, self.g, self.chip, self.core, self.tag, self.result = list(g), list(g.values()), chip, core, tag, None

    def swap(self):
        return _swap_halves(self.g)

    def got_swap(self, theirs):
        self.pair = [_pair_sum(g, t, self.core, f"pair_sum_{n}_{self.tag}") for n, g, t in zip(self.names, self.g, theirs)]

    def scatter(self):
        return _scatter_shards(self.pair)

    def got_scatter(self, q):
        self.mine = [_chip_sum(qa, pa, self.chip, f"chip_sum_{n}_{self.tag}") for n, qa, pa in zip(self.names, q, self.pair)]

    def share(self):
        return _share_halves(self.mine)

    def got_share(self, theirs):
        self.result = {n: jnp.where(self.core == 0, jnp.concatenate([a, b]), jnp.concatenate([b, a]))
                       for n, a, b in zip(self.names, self.mine, theirs)}

    def ride(self, hooks, swap_host, scatter_host, share_host):
        _hook(hooks, swap_host, self.swap, self.got_swap)
        _hook(hooks, scatter_host, self.scatter, self.got_scatter)
        _hook(hooks, share_host, self.share, self.got_share)

    def run(self):
        self.got_swap(_run_exchange(self.swap(), f"swap_halves_{self.tag}"))
        self.got_scatter(_run_exchange(self.scatter(), f"scatter_shards_{self.tag}"))
        self.got_share(_run_exchange(self.share(), f"share_halves_{self.tag}"))


class _LayerWeights:
    def __init__(self, gathered):
        self.gathered, self.made = gathered, {}

    def __getitem__(self, key):
        if key not in self.made:
            cols = lambda t: jnp.swapaxes(t, 0, 1).reshape(t.shape[1], -1)
            rows = lambda t: t.reshape(-1, t.shape[2])
            if key == "w_in":
                made = jnp.pad(cols(self.gathered("w_in")), ((0, 0), (0, PROJ_PAD - PROJ)))
            elif key == "w_gu":
                made = jnp.concatenate([cols(self.gathered("w_gate")), cols(self.gathered("w_up"))], axis=-1)
            else:
                made = rows(self.gathered(key))
            self.made[key] = made
        return self.made[key]


def _gather_small(pk, name):
    rows, w = pk.shape

    def body(pk_ref, all_ref, sum_ref, send_sems, recv_sems):
        x, y, c, _ = _place()
        me = 4 * x + 2 * y + c
        all_ref[me] = pk_ref[...]
        flips = [(fx, fy, fc) for fx in (0, 1) for fy in (0, 1) for fc in (0, 1)][1:]
        peers = [(x ^ fx, y ^ fy, c ^ fc) for fx, fy, fc in flips]
        sends = [_remote(pk_ref, all_ref.at[me], send_sems.at[k], recv_sems.at[k], peer) for k, peer in enumerate(peers)]
        for cp in sends:
            cp.start()
        for k, (px, py, pc) in enumerate(peers):
            slot = all_ref.at[4 * px + 2 * py + pc]
            _remote(slot, slot, send_sems.at[k], recv_sems.at[k], (px, py, pc)).wait_recv()
        for cp in sends:
            cp.wait_send()
        total = all_ref[0]
        for d in range(1, N_DEV):
            total = total + all_ref[d]
        sum_ref[...] = total

    vm = pl.BlockSpec(memory_space=pltpu.VMEM)
    return pl.pallas_call(
        body, name=name, in_specs=[vm], out_specs=[vm, vm],
        out_shape=[jax.ShapeDtypeStruct((N_DEV, rows, w), F32), jax.ShapeDtypeStruct((rows, w), F32)],
        scratch_shapes=[pltpu.SemaphoreType.DMA((7,)), pltpu.SemaphoreType.DMA((7,))],
    )(pk)


def _row_layout(c, nb, s):
    ch = jnp.swapaxes(c[:, :N_HEADS].reshape(nb, s, N_HEADS), 1, 2)
    ccol = jnp.broadcast_to(ch[..., None], (nb, N_HEADS, s, ATT))
    crow = jnp.broadcast_to(ch.reshape(nb, N_HEADS, s // ATT, 1, ATT), (nb, N_HEADS, s // ATT, 8, ATT))
    return ccol, crow


def _dil_bias(rel_bias, name):
    def body(rel_ref, t_ref, o_ref):
        for p in range(len(DIL_PATTERNS)):
            table = t_ref[p]

            def bucket(k, accs, table=table):
                return tuple(jnp.where(table == k, rel_ref[k, h], acc) for h, acc in enumerate(accs))

            accs = lax.fori_loop(0, REL_BUCKETS, bucket, tuple(jnp.full((BLK, 2 * BLK), NEG, F32) for _ in range(N_HEADS)))
            for h in range(N_HEADS):
                o_ref[p, h] = accs[h]

    vm = pl.BlockSpec(memory_space=pltpu.VMEM)
    return pl.pallas_call(
        body, name=name, in_specs=[pl.BlockSpec(memory_space=pltpu.SMEM), vm], out_specs=vm,
        out_shape=jax.ShapeDtypeStruct((len(DIL_PATTERNS), N_HEADS, BLK, 2 * BLK), F32),
        compiler_params=pltpu.CompilerParams(vmem_limit_bytes=VMEM_LIMIT),
    )(rel_bias, jnp.asarray(_bucket_table()))


def _layer_forward(x, wts, small, nb, s, tag, hooks=None):
    proj = _matmul(x, wts["w_in"], "proj", tag)

    carry, finish = _take(hooks, "sb_fwd")
    o_sb, carried = _sb_fwd(proj, nb, s, f"sb_fwd_{tag}", carry)
    finish(carried)

    bias = _dil_bias(small["rel_bias"], f"dil_bias_{tag}")
    carry, finish = _take(hooks, "dil_fwd")
    (o_dl, lse_dl), carried = _dil_attention_fwd(proj, bias, nb, s, f"dil_fwd_{tag}", carry)
    finish(carried)

    fb = jnp.zeros((8, BLK), F32).at[0, :N_HEADS].set(small["f_bias"])
    csum = _fox_gates_fwd(proj, fb, nb, s, f"fox_gates_{tag}")
    ccol, crow = _row_layout(csum, nb, s)
    carry, finish = _take(hooks, "fox_fwd")
    (o_fx, lse_fx), carried = _fox_fwd(proj, ccol, crow, nb, s, f"fox_fwd_{tag}", carry)
    finish(carried)

    cw = jnp.zeros((8, CONV_W), F32).at[:3].set(small["conv_w"])
    o_cv = _conv_fwd(proj, cw, nb, s, f"conv_fwd_{tag}")

    mixed = jnp.concatenate([o_sb, o_dl, o_fx, o_cv], axis=-1).astype(MXU_DTYPE)
    pre1, x1 = _matmul_post_norm(mixed, wts["w_out"], x, small["ln1_g"], small["ln1_b"], f"out_proj_ln1_{tag}")
    carry, finish = _take(hooks, "ffn_in")
    (gate, up, hid), carried = _ffn_in(x1, wts["w_gu"], f"ffn_in_{tag}", carry)
    finish(carried)
    pre2, x2 = _matmul_post_norm(hid, wts["w_down"], x1, small["ln2_g"], small["ln2_b"], f"ffn_out_ln2_{tag}")
    saved = dict(x=x, proj=proj, bias=bias, o_dl=o_dl, lse_dl=lse_dl, fb=fb, ccol=ccol, crow=crow, o_fx=o_fx,
                 lse_fx=lse_fx, cw=cw, mixed=mixed, pre1=pre1, x1=x1, gate=gate, up=up, hid=hid, pre2=pre2)
    return x2, saved


def _layer_backward(dx2, sv, wts, small, nb, s, tag, hooks=None, ffn_grads_ready=None):
    t = nb * s
    dpre2, dgb2 = _ln_bwd(dx2, sv["pre2"], small["ln2_g"], f"ln2_bwd_{tag}")
    dpre2_b = dpre2.astype(MXU_DTYPE)
    dhid = _matmul(dpre2_b, wts["w_down"], "ffn_out_dx", tag, trans_b=True)
    dw_down = _matmul(sv["hid"], dpre2_b, "ffn_out_dw", tag, trans_a=True)
    carry, finish = _take(hooks, "swiglu_bwd")
    (dgu,), carried = _swiglu_bwd(dhid, sv["gate"], sv["up"], f"swiglu_bwd_{tag}", carry)
    finish(carried)
    dx1 = _matmul(dgu, wts["w_gu"], "ffn_in_dx", tag, add=dpre2, add_scale=ALPHA, trans_b=True)
    dw_gu = _matmul(sv["x1"].astype(MXU_DTYPE), dgu, "ffn_in_dw", tag, trans_a=True)
    if ffn_grads_ready:
        ffn_grads_ready(dw_down, dw_gu[:, :D_FF], dw_gu[:, D_FF:])

    dpre1, dgb1 = _ln_bwd(dx1, sv["pre1"], small["ln1_g"], f"ln1_bwd_{tag}")
    dpre1_b = dpre1.astype(MXU_DTYPE)
    dmixed = _matmul(dpre1_b, wts["w_out"], "out_proj_dx", tag, trans_b=True)
    dw_out = _matmul(sv["mixed"], dpre1_b, "out_proj_dw", tag, trans_a=True)
    proj = sv["proj"]

    carry, finish = _take(hooks, "sb_bwd")
    (dq_sb, dk_sb, dv_sb), carried = _sb_bwd(proj, dmixed, nb, s, f"sb_bwd_{tag}", carry)
    finish(carried)

    delta_dl = _delta_kernel(dmixed, sv["o_dl"], nb, s, f"dil_delta_{tag}")
    carry, finish = _take(hooks, "dil_bwd")
    (dq_dl, dk_dl, dv_dl, gbias), carried = _dil_attention_bwd(proj, dmixed, sv["lse_dl"], delta_dl, sv["bias"], nb, s,
                                                               f"dil_bwd_{tag}", carry)
    finish(carried)
    drel = _bucket_reduce(gbias, jnp.asarray(_bucket_table()), f"rel_bias_grad_{tag}")

    carry, finish = _take(hooks, "fox_bwd")
    (dq_fx, dk_fx, dv_fx, dcol), carried = _fox_bwd(proj, dmixed, sv["lse_fx"], sv["ccol"], sv["crow"], nb, s,
                                                    f"fox_bwd_{tag}", carry)
    finish(carried)
    dcs = -jnp.swapaxes(dcol[:, :, :, 0, :].reshape(nb, N_HEADS, s), 1, 2).reshape(t, N_HEADS)
    dcs = jnp.pad(dcs, ((0, 0), (0, BLK - N_HEADS)))
    dfx, dfb = _fox_gates_bwd(dcs, proj, sv["fb"], nb, s, f"fox_gates_bwd_{tag}")

    dgates, dcw = _conv_bwd(dmixed, proj, sv["cw"], nb, s, f"conv_bwd_{tag}")

    dproj = jnp.concatenate([dq_sb, dk_sb, dv_sb, dq_dl, dk_dl, dv_dl, dq_fx, dk_fx, dv_fx, dgates, dfx],
                            axis=-1).astype(MXU_DTYPE)
    dx = _matmul(dproj, wts["w_in"], "proj_dx", tag, add=dpre1, add_scale=ALPHA, trans_b=True)
    dw_in = _matmul(sv["x"].astype(MXU_DTYPE), dproj, "proj_dw", tag, trans_a=True)

    grads = dict(w_in=dw_in[:, :PROJ], w_out=dw_out, w_gate=dw_gu[:, :D_FF], w_up=dw_gu[:, D_FF:], w_down=dw_down,
                 ln1_g=dgb1[0], ln1_b=dgb1[1], ln2_g=dgb2[0], ln2_b=dgb2[1], conv_w=dcw[:3], f_bias=dfb[0, :N_HEADS],
                 rel_bias=drel[:N_HEADS, :REL_BUCKETS].T)
    return dx, grads


class _NoExchanges:
    def forward_hooks(self, layer):
        return None

    def backward_hooks(self, layer):
        return None

    def ffn_grads_ready(self, layer):
        return None

    def layer_done(self, layer, grads):
        pass


def _local_step(x, target, weights_of, small_all, schedule=None):
    schedule = schedule or _NoExchanges()
    nb, s, d = x.shape
    h = x.reshape(nb * s, d)
    saved = []
    for layer in range(DEPTH):
        wts = weights_of(layer)
        h, sv = _layer_forward(h, wts, small_all[layer], nb, s, f"l{layer}", schedule.forward_hooks(layer))
        saved.append((sv, wts))
    dy, lossp = _loss_kernel(h, target.reshape(nb * s, d), "loss")
    grads = [None] * DEPTH
    for layer in reversed(range(DEPTH)):
        sv, wts = saved[layer]
        dy, grads[layer] = _layer_backward(dy, sv, wts, small_all[layer], nb, s, f"l{layer}",
                                           schedule.backward_hooks(layer), schedule.ffn_grads_ready(layer))
        schedule.layer_done(layer, grads[layer])
    return lossp, dy.reshape(nb, s, d), grads


_BIG = ("w_in", "w_out", "w_gate", "w_up", "w_down")
_COL_SHARDED = ("w_in", "w_gate", "w_up")


class _Schedule:
    def __init__(self, shards, chip, core):
        self.chip, self.core, self.reduces = chip, core, [[] for _ in range(DEPTH)]
        first = _WeightPrefetch(["w_in"], shards, 0, chip)
        first.run("l0_w_in")
        rest = _WeightPrefetch(["w_out", "w_gate", "w_up", "w_down"], shards, 0, chip)
        ahead_a = _WeightPrefetch(["w_in", "w_out", "w_down"], shards, 1, chip)
        ahead_b = _WeightPrefetch(["w_gate", "w_up"], shards, 1, chip)
        self.fetches = [[first, rest], [ahead_a, ahead_b]]
        self.forward, self.backward = [{} for _ in range(DEPTH)], [{} for _ in range(DEPTH)]
        rest.ride(self.forward[0], "sb_fwd", "fox_fwd")
        ahead_a.ride(self.forward[0], "dil_fwd", "ffn_in")
        ahead_b.ride(self.forward[0], "fox_fwd", "ffn_in")

    def weights(self, layer):
        def gathered(name):
            return next(f.result[name] for f in self.fetches[layer] if name in f.names)
        return _LayerWeights(gathered)

    def forward_hooks(self, layer):
        return self.forward[layer]

    def backward_hooks(self, layer):
        return self.backward[layer]

    def _reduce(self, layer, grads, tag):
        red = _GradReduce({name: _by_chip(name, g) for name, g in grads.items()}, self.chip, self.core, tag)
        self.reduces[layer].append(red)
        return red

    def ffn_grads_ready(self, layer):
        if layer != 0:
            return None

        def ready(dw_down, dw_gate, dw_up):
            red = self._reduce(0, dict(w_gate=dw_gate, w_up=dw_up, w_down=dw_down), "l0_ffn")
            red.ride(self.backward[0], "sb_bwd", "dil_bwd", "fox_bwd")

        return ready

    def layer_done(self, layer, grads):
        if layer == 1:
            self._reduce(1, {name: grads[name] for name in _BIG}, "l1").ride(self.backward[0], "swiglu_bwd", "sb_bwd", "fox_bwd")
        else:
            self._reduce(0, dict(w_in=grads["w_in"], w_out=grads["w_out"]), "l0_attn").run()

    def reduced(self, layer, name):
        return next(r.result[name] for r in self.reduces[layer] if name in r.names)


def _by_chip(name, g):
    if name in _COL_SHARDED:
        return jnp.swapaxes(g.reshape(g.shape[0], N_CHIPS, -1), 0, 1)
    return g.reshape(N_CHIPS, -1, g.shape[1])


_SMALL_LAYOUT = (("ln1_g", 0), ("ln1_b", 2), ("ln2_g", 4), ("ln2_b", 6), ("conv_w", 8))
_ROW_MISC = 10
_ROW_LOSS = 11


def _pack_small(per_layer, rel_bias, loss=None):
    pk = jnp.zeros((SMALL_ROWS, D_MODEL), F32)
    for name, row in _SMALL_LAYOUT:
        for l in range(DEPTH):
            v = per_layer[l][name].reshape(-1)
            pk = pk.at[row + l, :v.shape[0]].set(v)
    fb = jnp.concatenate([per_layer[l]["f_bias"] for l in range(DEPTH)])
    pk = pk.at[_ROW_MISC, :2 * N_HEADS].set(fb)
    pk = pk.at[_ROW_MISC, BLK:BLK + REL_BUCKETS * N_HEADS].set(rel_bias.reshape(-1))
    if loss is not None:
        pk = pk.at[_ROW_LOSS, 0].set(loss)
    return pk


def _unpack_small(pk, conv_cols):
    out = {}
    for name, row in _SMALL_LAYOUT:
        n = 3 * conv_cols if name == "conv_w" else D_MODEL
        v = pk[row:row + DEPTH, :n]
        out[name] = v.reshape(DEPTH, 3, conv_cols) if name == "conv_w" else v
    out["f_bias"] = pk[_ROW_MISC, :2 * N_HEADS].reshape(DEPTH, N_HEADS)
    out["rel_bias"] = pk[_ROW_MISC, BLK:BLK + REL_BUCKETS * N_HEADS].reshape(REL_BUCKETS, N_HEADS)
    return out


_WEIGHTS = ("w_in", "f_bias", "conv_w", "w_out", "rel_bias", "ln1_g", "ln1_b", "w_gate", "w_up", "w_down", "ln2_g", "ln2_b")


def kernel(x, w_in, f_bias, conv_w, w_out, rel_bias, ln1_g, ln1_b, w_gate, w_up, w_down, ln2_g, ln2_b, loss_target, m_w_in, m_f_bias, m_conv_w, m_w_out, m_rel_bias, m_ln1_g, m_ln1_b, m_w_gate, m_w_up, m_w_down, m_ln2_g, m_ln2_b, v_w_in, v_f_bias, v_conv_w, v_w_out, v_rel_bias, v_ln1_g, v_ln1_b, v_w_gate, v_w_up, v_w_down, v_ln2_g, v_ln2_b):
    w = dict(w_in=w_in, f_bias=f_bias, conv_w=conv_w, w_out=w_out, rel_bias=rel_bias, ln1_g=ln1_g, ln1_b=ln1_b,
             w_gate=w_gate, w_up=w_up, w_down=w_down, ln2_g=ln2_g, ln2_b=ln2_b)
    m = dict(w_in=m_w_in, f_bias=m_f_bias, conv_w=m_conv_w, w_out=m_w_out, rel_bias=m_rel_bias, ln1_g=m_ln1_g,
             ln1_b=m_ln1_b, w_gate=m_w_gate, w_up=m_w_up, w_down=m_w_down, ln2_g=m_ln2_g, ln2_b=m_ln2_b)
    v = dict(w_in=v_w_in, f_bias=v_f_bias, conv_w=v_conv_w, w_out=v_w_out, rel_bias=v_rel_bias, ln1_g=v_ln1_g,
             ln1_b=v_ln1_b, w_gate=v_w_gate, w_up=v_w_up, w_down=v_w_down, ln2_g=v_ln2_g, ln2_b=v_ln2_b)
    chip = 2 * lax.axis_index("x") + lax.axis_index("y")
    core = lax.axis_index("c")
    conv_shard = CONV_W // N_CHIPS

    schedule = _Schedule({name: w[name].astype(MXU_DTYPE) for name in _BIG}, chip, core)
    cw_pk = jnp.zeros((8, D_MODEL), F32).at[0, :DEPTH * 3 * conv_shard].set(conv_w.reshape(-1))
    cw_all, _ = _gather_small(cw_pk, "gather_conv_w")
    cw_chips = cw_all[0::2, 0, :DEPTH * 3 * conv_shard].reshape(N_CHIPS, DEPTH, 3, conv_shard)
    conv_full = jnp.moveaxis(cw_chips, 0, 2).reshape(DEPTH, 3, CONV_W)
    small_all = [dict(f_bias=f_bias[l], conv_w=conv_full[l], rel_bias=rel_bias, ln1_g=ln1_g[l], ln1_b=ln1_b[l],
                      ln2_g=ln2_g[l], ln2_b=ln2_b[l]) for l in range(DEPTH)]

    lossp, grad_x, grads = _local_step(x, loss_target, schedule.weights, small_all, schedule)
    big_g = {name: jnp.stack([schedule.reduced(l, name) for l in range(DEPTH)]) for name in _BIG}

    drel = grads[0]["rel_bias"] + grads[1]["rel_bias"]
    small_pk = _pack_small(grads, drel, lossp[0, 0])
    _, small_sum = _gather_small(small_pk, "gather_small_grads")
    loss = small_sum[_ROW_LOSS, 0]
    small_g = _unpack_small(small_sum, CONV_W)
    small_g["conv_w"] = lax.dynamic_slice_in_dim(small_g["conv_w"], chip * conv_shard, conv_shard, axis=2)

    out_g, out_d, out_m, out_v = dict(small_g), {}, {}, {}
    for name in _BIG:
        out_g[name] = big_g[name]
        out_d[name], out_m[name], out_v[name] = _adamw(w[name], big_g[name], m[name], v[name], f"adamw_{name}")
    per_layer = lambda src: [{name: src[name][l] for name in ("ln1_g", "ln1_b", "ln2_g", "ln2_b", "conv_w", "f_bias")}
                             for l in range(DEPTH)]
    packs = [_pack_small(per_layer(src), src["rel_bias"])[None] for src in (w, small_g, m, v)]
    for dst, pk in zip((out_d, out_m, out_v), _adamw(*packs, "adamw_small")):
        dst.update(_unpack_small(pk[0], conv_shard))

    return (loss, grad_x, *[out_g[n] for n in _WEIGHTS], *[out_d[n] for n in _WEIGHTS],
            *[out_m[n] for n in _WEIGHTS], *[out_v[n] for n in _WEIGHTS])
```

```python
import functools
import math

import numpy as np
import jax
import jax.numpy as jnp
from jax import lax
from jax.experimental import pallas as pl
from jax.experimental.pallas import tpu as pltpu

F32 = jnp.float32
BF16 = jnp.bfloat16
MXU_DTYPE = BF16

D_MODEL = 1024
HEAD_DIM = 64
N_HEADS = 4
BLK = 128
ATT = 256
QT = 512
CONV_W = 256
PROJ = 3076
PROJ_PAD = 3200
D_FF = 2816
DEPTH = 2
ALPHA = (2 * DEPTH) ** 0.25
LN_EPS = 1e-5
NEG = -1e30
DIL_PATTERNS = ((128, 1), (512, 4), (2048, 16))
REL_BUCKETS = 32
N_CHIPS = 4
N_DEV = 8
SMALL_ROWS = 16

ADAM_LR = 0.001
ADAM_B1 = 0.9
ADAM_B2 = 0.999
ADAM_EPS = 1e-08
ADAM_WD = 0.01
ADAM_STEP = 10

VMEM_LIMIT = 56 * 2 ** 20
MESH = pl.DeviceIdType.MESH


def _cparams(*sem):
    return pltpu.CompilerParams(dimension_semantics=tuple(sem), vmem_limit_bytes=VMEM_LIMIT)


def _dot(a, b):
    return jnp.dot(a.astype(MXU_DTYPE), b.astype(MXU_DTYPE), preferred_element_type=F32)


def _dot_nt(a, b):
    return lax.dot_general(a.astype(MXU_DTYPE), b.astype(MXU_DTYPE), (((1,), (1,)), ((), ())),
                           preferred_element_type=F32)


def _dot_tn(a, b):
    return lax.dot_general(a.astype(MXU_DTYPE), b.astype(MXU_DTYPE), (((0,), (0,)), ((), ())),
                           preferred_element_type=F32)


def _split_dot(x, ones, passes):
    acc, rest = None, x
    for p in range(passes):
        piece = rest.astype(MXU_DTYPE)
        part = jnp.dot(piece, ones, preferred_element_type=F32)
        acc = part if acc is None else acc + part
        if p + 1 < passes:
            rest = rest - piece.astype(F32)
    return acc


def _split_dot_lhs(ones, x, passes):
    acc, rest = None, x
    for p in range(passes):
        piece = rest.astype(MXU_DTYPE)
        part = jnp.dot(ones, piece, preferred_element_type=F32)
        acc = part if acc is None else acc + part
        if p + 1 < passes:
            rest = rest - piece.astype(F32)
    return acc


def _iota2(shape, axis):
    return lax.broadcasted_iota(jnp.int32, shape, axis)


_TILES = {"proj": (1024, 640, 1024), "ffn_out_dw": (1408, 1024, 2048),
          "ffn_in_dw": (1024, 1408, 2048), "out_proj_dx": (1024, 1024, 1024),
          "out_proj_dw": (1024, 1024, 2048), "proj_dx": (1024, 512, 3200), "proj_dw": (1024, 640, 2048)}


def _matmul(a, b, kind, tag, *, out_dtype=F32, add=None, add_scale=1.0, trans_a=False, trans_b=False):
    k, m = a.shape if trans_a else a.shape[::-1]
    n = b.shape[0] if trans_b else b.shape[1]
    tm, tn, tk = _TILES[kind]
    tm, tk, name = min(tm, m), min(tk, k), f"{kind}_{tag}"
    assert m % tm == 0 and n % tn == 0 and k % tk == 0, (a.shape, b.shape, tm, tn, tk)
    nk = k // tk

    def body(*refs):
        if add is None:
            a_ref, b_ref, o_ref = refs[:3]
            c_ref, scr = None, refs[3:]
        else:
            a_ref, b_ref, c_ref, o_ref = refs[:4]
            scr = refs[4:]
        dot = _dot_tn if trans_a else _dot_nt if trans_b else _dot
        part = dot(a_ref[...], b_ref[...])

        def finish(acc):
            if c_ref is not None:
                acc = acc + add_scale * c_ref[...]
            o_ref[...] = acc.astype(out_dtype)

        if nk == 1:
            finish(part)
        else:
            acc_ref = scr[0]
            kk = pl.program_id(2)

            @pl.when(kk == 0)
            def _():
                acc_ref[...] = part

            @pl.when(kk > 0)
            def _():
                acc_ref[...] += part

            @pl.when(kk == nk - 1)
            def _():
                finish(acc_ref[...])

    b_spec = pl.BlockSpec((tn, tk), lambda i, j, kk: (j, kk)) if trans_b else pl.BlockSpec((tk, tn), lambda i, j, kk: (kk, j))
    a_spec = pl.BlockSpec((tk, tm), lambda i, j, kk: (kk, i)) if trans_a else pl.BlockSpec((tm, tk), lambda i, j, kk: (i, kk))
    in_specs = [a_spec, b_spec]
    operands = [a, b]
    if add is not None:
        in_specs.append(pl.BlockSpec((tm, tn), lambda i, j, kk: (i, j)))
        operands.append(add)
    return pl.pallas_call(
        body, name=name, grid=(m // tm, n // tn, nk), in_specs=in_specs,
        out_specs=pl.BlockSpec((tm, tn), lambda i, j, kk: (i, j)),
        out_shape=jax.ShapeDtypeStruct((m, n), out_dtype),
        scratch_shapes=[pltpu.VMEM((tm, tn), F32)] if nk > 1 else [],
        compiler_params=_cparams("parallel", "parallel", "arbitrary"),
    )(*operands)


def _matmul_post_norm(a, b, xin, g, beta, name):
    t, k = a.shape
    d = b.shape[1]
    tm = 512

    def body(a_ref, b_ref, x_ref, g_ref, beta_ref, pre_ref, y_ref):
        pre = ALPHA * x_ref[...] + _dot(a_ref[...], b_ref[...])
        xhat, _ = _ln_stats(pre)
        pre_ref[...] = pre
        y_ref[...] = xhat * g_ref[...] + beta_ref[...]

    row = pl.BlockSpec((tm, d), lambda i: (i, 0))
    vec = pl.BlockSpec((1, d), lambda i: (0, 0))
    return pl.pallas_call(
        body, name=name, grid=(t // tm,),
        in_specs=[pl.BlockSpec((tm, k), lambda i: (i, 0)), pl.BlockSpec((k, d), lambda i: (0, 0)), row, vec, vec],
        out_specs=[row, row], out_shape=[jax.ShapeDtypeStruct((t, d), F32)] * 2, compiler_params=_cparams("parallel"),
    )(a, b, xin, g.reshape(1, d), beta.reshape(1, d))


def _ffn_in(x1, w_gu, name, carry=None):
    t, d = x1.shape
    tm, tn = 512, D_FF // 2
    nj = D_FF // tn

    def body(x_ref, wg_ref, wu_ref, gate_ref, up_ref, h_ref):
        xb = x_ref[...].astype(MXU_DTYPE)
        gate = _dot(xb, wg_ref[...])
        up = _dot(xb, wu_ref[...])
        gate_ref[...] = gate
        up_ref[...] = up
        h_ref[...] = (gate * (1.0 / (1.0 + jnp.exp(-gate))) * up).astype(h_ref.dtype)

    out = pl.BlockSpec((tm, tn), lambda i, j: (i, j))
    return _host_call(
        body, carry, name=name, grid=(t // tm, nj),
        in_specs=[pl.BlockSpec((tm, d), lambda i, j: (i, 0)), pl.BlockSpec((d, tn), lambda i, j: (0, j)),
                  pl.BlockSpec((d, tn), lambda i, j: (0, nj + j))],
        out_specs=[out, out, out],
        out_shape=[jax.ShapeDtypeStruct((t, D_FF), F32)] * 2 + [jax.ShapeDtypeStruct((t, D_FF), MXU_DTYPE)],
        operands=(x1, w_gu, w_gu))


def _ffn_out_dx(dy, w_down, gate, up, name, carry=None):
    t, d = dy.shape
    tm, tn = 512, D_FF // 2

    def body(dy_ref, w_ref, gate_ref, up_ref, dg_ref, du_ref):
        dh = _dot_nt(dy_ref[...], w_ref[...])
        gate = gate_ref[...]
        sig = 1.0 / (1.0 + jnp.exp(-gate))
        dg_ref[...] = (dh * up_ref[...] * sig * (1.0 + gate * (1.0 - sig))).astype(dg_ref.dtype)
        du_ref[...] = (dh * gate * sig).astype(du_ref.dtype)

    tile = pl.BlockSpec((tm, tn), lambda i, j: (i, j))
    return _host_call(
        body, carry, name=name, grid=(t // tm, D_FF // tn),
        in_specs=[pl.BlockSpec((tm, d), lambda i, j: (i, 0)), pl.BlockSpec((tn, d), lambda i, j: (j, 0)), tile, tile],
        out_specs=[tile, tile], out_shape=[jax.ShapeDtypeStruct((t, D_FF), MXU_DTYPE)] * 2,
        operands=(dy, w_down, gate, up))


def _ffn_in_dx(dgate, dup, w_gu, add, name):
    t = dgate.shape[0]
    d = w_gu.shape[0]
    tm, tk = 1024, D_FF // 2
    nk = D_FF // tk

    def body(dg_ref, du_ref, wg_ref, wu_ref, add_ref, o_ref, acc_ref):
        kk = pl.program_id(1)
        part = _dot_nt(dg_ref[...], wg_ref[...]) + _dot_nt(du_ref[...], wu_ref[...])

        @pl.when(kk == 0)
        def _():
            acc_ref[...] = part

        @pl.when(kk > 0)
        def _():
            acc_ref[...] += part

        @pl.when(kk == nk - 1)
        def _():
            o_ref[...] = acc_ref[...] + ALPHA * add_ref[...]

    act = pl.BlockSpec((tm, tk), lambda i, kk: (i, kk))
    row = pl.BlockSpec((tm, d), lambda i, kk: (i, 0))
    return pl.pallas_call(
        body, name=name, grid=(t // tm, nk),
        in_specs=[act, act, pl.BlockSpec((d, tk), lambda i, kk: (0, kk)), pl.BlockSpec((d, tk), lambda i, kk: (0, nk + kk)), row],
        out_specs=row, out_shape=jax.ShapeDtypeStruct((t, d), F32), scratch_shapes=[pltpu.VMEM((tm, d), F32)],
        compiler_params=_cparams("parallel", "arbitrary"),
    )(dgate, dup, w_gu, w_gu, add)


def _ln_stats(pre):
    mu = jnp.mean(pre, axis=-1, keepdims=True)
    xc = pre - mu
    var = jnp.mean(xc * xc, axis=-1, keepdims=True)
    rstd = lax.rsqrt(var + LN_EPS)
    return xc * rstd, rstd


def _ln_bwd(dy, pre, g, name):
    t, d = dy.shape
    tile = 256

    def body(dy_ref, pre_ref, g_ref, dpre_ref, dgb_ref):
        dyv = dy_ref[...]
        xhat, rstd = _ln_stats(pre_ref[...])
        dxh = dyv * g_ref[...]
        m1 = jnp.mean(dxh, axis=-1, keepdims=True)
        m2 = jnp.mean(dxh * xhat, axis=-1, keepdims=True)
        dpre_ref[...] = rstd * (dxh - m1 - xhat * m2)

        @pl.when(pl.program_id(0) == 0)
        def _():
            dgb_ref[...] = jnp.zeros_like(dgb_ref)

        dgb_ref[0:1, :] += jnp.sum(dyv * xhat, axis=0, keepdims=True)
        dgb_ref[1:2, :] += jnp.sum(dyv, axis=0, keepdims=True)

    row = pl.BlockSpec((tile, d), lambda i: (i, 0))
    return pl.pallas_call(
        body, name=name, grid=(t // tile,), in_specs=[row, row, pl.BlockSpec((1, d), lambda i: (0, 0))],
        out_specs=[row, pl.BlockSpec((8, d), lambda i: (0, 0))],
        out_shape=[jax.ShapeDtypeStruct((t, d), F32), jax.ShapeDtypeStruct((8, d), F32)],
        compiler_params=_cparams("arbitrary"),
    )(dy, pre, g.reshape(1, d))


def _loss_kernel(y, target, name):
    t, d = y.shape
    tile = 512

    def body(y_ref, t_ref, dy_ref, l_ref):
        err = y_ref[...] - t_ref[...]
        dy_ref[...] = err * (1.0 / d)

        @pl.when(pl.program_id(0) == 0)
        def _():
            l_ref[...] = jnp.zeros_like(l_ref)

        l_ref[...] += jnp.sum(err * err) * (0.5 / d)

    row = pl.BlockSpec((tile, d), lambda i: (i, 0))
    return pl.pallas_call(
        body, name=name, grid=(t // tile,), in_specs=[row, row],
        out_specs=[row, pl.BlockSpec((8, 128), lambda i: (0, 0))],
        out_shape=[jax.ShapeDtypeStruct((t, d), F32), jax.ShapeDtypeStruct((8, 128), F32)],
        compiler_params=_cparams("arbitrary"),
    )(y, target)


def _adamw(w, g, m, v, name):
    nl, r, c = w.shape
    tr = r
    for cand in (256, 352, 128, 64, 16, 8):
        if r % cand == 0:
            tr = cand
            break

    def body(w_ref, g_ref, m_ref, v_ref, d_ref, nm_ref, nv_ref):
        gv = g_ref[...]
        nm = ADAM_B1 * m_ref[...] + (1.0 - ADAM_B1) * gv
        nv = ADAM_B2 * v_ref[...] + (1.0 - ADAM_B2) * (gv * gv)
        m_hat = nm / (1.0 - ADAM_B1 ** ADAM_STEP)
        v_hat = nv / (1.0 - ADAM_B2 ** ADAM_STEP)
        d_ref[...] = -ADAM_LR * (m_hat / (jnp.sqrt(v_hat) + ADAM_EPS) + ADAM_WD * w_ref[...])
        nm_ref[...] = nm
        nv_ref[...] = nv

    blk = pl.BlockSpec((1, tr, c), lambda l, i: (l, i, 0))
    return pl.pallas_call(
        body, name=name, grid=(nl, r // tr), in_specs=[blk] * 4, out_specs=[blk] * 3,
        out_shape=[jax.ShapeDtypeStruct(w.shape, F32)] * 3, compiler_params=_cparams("parallel", "parallel"),
    )(w, g, m, v)


def _shift_down(u, k, rows):
    return jnp.where(rows >= k, pltpu.roll(u, k, 0), 0.0)


def _shift_up(u, k, rows, s):
    return jnp.where(rows < s - k, pltpu.roll(u, s - k, 0), 0.0)


def _conv_fwd(proj, conv_w, nb, s, name):
    def body(b_ref, c_ref, h_ref, w_ref, o_ref):
        rows = _iota2((s, CONV_W), 0)
        u = c_ref[...] * h_ref[...]
        y = w_ref[2:3, :] * u + w_ref[1:2, :] * _shift_down(u, 1, rows) + w_ref[0:1, :] * _shift_down(u, 2, rows)
        o_ref[...] = b_ref[...] * y

    col = lambda j: pl.BlockSpec((s, CONV_W), lambda b: (b, j))
    return pl.pallas_call(
        body, name=name, grid=(nb,),
        in_specs=[col(9), col(10), col(11), pl.BlockSpec((8, CONV_W), lambda b: (0, 0))],
        out_specs=pl.BlockSpec((s, CONV_W), lambda b: (b, 0)),
        out_shape=jax.ShapeDtypeStruct((nb * s, CONV_W), F32), compiler_params=_cparams("parallel"),
    )(proj, proj, proj, conv_w)


def _conv_bwd(dmixed, proj, conv_w, nb, s, name):
    def body(do_ref, b_ref, c_ref, h_ref, w_ref, dg_ref, dw_ref):
        rows = _iota2((s, CONV_W), 0)
        cg, hg, bg, dout = c_ref[...], h_ref[...], b_ref[...], do_ref[...]
        u = cg * hg
        u1 = _shift_down(u, 1, rows)
        u2 = _shift_down(u, 2, rows)
        y = w_ref[2:3, :] * u + w_ref[1:2, :] * u1 + w_ref[0:1, :] * u2
        dy = dout * bg
        du = w_ref[2:3, :] * dy + w_ref[1:2, :] * _shift_up(dy, 1, rows, s) + w_ref[0:1, :] * _shift_up(dy, 2, rows, s)
        dg_ref[:, 0:CONV_W] = dout * y
        dg_ref[:, CONV_W:2 * CONV_W] = du * hg
        dg_ref[:, 2 * CONV_W:3 * CONV_W] = du * cg

        @pl.when(pl.program_id(0) == 0)
        def _():
            dw_ref[...] = jnp.zeros_like(dw_ref)

        dw_ref[0:1, :] += jnp.sum(dy * u2, axis=0, keepdims=True)
        dw_ref[1:2, :] += jnp.sum(dy * u1, axis=0, keepdims=True)
        dw_ref[2:3, :] += jnp.sum(dy * u, axis=0, keepdims=True)

    col = lambda j: pl.BlockSpec((s, CONV_W), lambda b: (b, j))
    return pl.pallas_call(
        body, name=name, grid=(nb,),
        in_specs=[col(3), col(9), col(10), col(11), pl.BlockSpec((8, CONV_W), lambda b: (0, 0))],
        out_specs=[pl.BlockSpec((s, 3 * CONV_W), lambda b: (b, 0)), pl.BlockSpec((8, CONV_W), lambda b: (0, 0))],
        out_shape=[jax.ShapeDtypeStruct((nb * s, 3 * CONV_W), F32), jax.ShapeDtypeStruct((8, CONV_W), F32)],
        compiler_params=_cparams("arbitrary"),
    )(dmixed, proj, proj, proj, conv_w)


def _col_spec(s, base):
    return pl.BlockSpec((s, BLK), lambda b, p: (b, base + p))


def _qrows(i):
    return pl.ds(pl.multiple_of(i * QT, QT), QT)


def _rows(j):
    return pl.ds(pl.multiple_of(j * ATT, ATT), ATT)


def _keys_upto(i):
    return (i + 1) * (QT // ATT)


def _triangle(keep):
    return keep(_iota2((ATT, ATT), 0), _iota2((ATT, ATT), 1)).astype(MXU_DTYPE)


def _rows128(i):
    return pl.ds(pl.multiple_of(i * BLK, BLK), BLK)


def _log_sigmoid_parts(z):
    e = jnp.exp(-jnp.abs(z))
    l1p = jnp.log(1.0 + e)
    lb = jnp.minimum(z, 0.0) - l1p
    return lb, lb - z, e


def _head_masks():
    lane = _iota2((1, BLK), 1)
    return [(lane >= h * HEAD_DIM) & (lane < (h + 1) * HEAD_DIM) for h in range(2)]


def _split_heads(ref, scr, sels):
    for h, sel in enumerate(sels):
        scr[h] = jnp.where(sel, ref[...], 0.0).astype(MXU_DTYPE)


def _sb_fwd(proj, nb, s, name, carry=None):
    nblk = s // ATT

    def body(q_ref, k_ref, v_ref, o_ref, km, vm):
        sels = _head_masks()
        _split_heads(k_ref, km, sels)
        _split_heads(v_ref, vm, sels)
        rows = _iota2((QT, ATT), 0)
        cols = _iota2((QT, ATT), 1)
        later = _triangle(lambda r, c: r > c)

        def qblock(i, _):
            qi = (q_ref[_qrows(i), :] * 0.125).astype(MXU_DTYPE)

            def kblock(t, state):
                carries, acc = state
                j = _keys_upto(i) - 1 - t
                strict = (cols + (j * ATT - i * QT)) < rows
                out = []
                for h in range(2):
                    z = _dot_nt(qi, km[h, _rows(j), :])
                    lb, lr, _ = _log_sigmoid_parts(z)
                    lr = jnp.where(strict, lr, 0.0)
                    tail = _split_dot(lr, later, 2) + carries[h]
                    a = jnp.where(strict, jnp.exp(lb + tail), 0.0)
                    acc = acc + _dot(a, vm[h, _rows(j), :])
                    out.append(carries[h] + jnp.sum(lr, axis=-1, keepdims=True))
                return tuple(out), acc

            init = ((jnp.zeros((QT, 1), F32),) * 2, jnp.zeros((QT, BLK), F32))
            _, acc = lax.fori_loop(0, _keys_upto(i), kblock, init)
            o_ref[_qrows(i), :] = acc
            return 0

        lax.fori_loop(0, s // QT, qblock, 0)

    (o,), extra = _host_call(
        body, carry, name=name, grid=(nb, 2), in_specs=[_col_spec(s, 0), _col_spec(s, 2), _col_spec(s, 4)],
        out_specs=[_col_spec(s, 0)], out_shape=[jax.ShapeDtypeStruct((nb * s, 2 * BLK), F32)],
        scratch_shapes=[pltpu.VMEM((2, s, BLK), MXU_DTYPE)] * 2, operands=(proj, proj, proj))
    return o, extra


def _sb_bwd(proj, dmixed, nb, s, name, carry=None):
    nblk = s // ATT

    def body(q_ref, k_ref, v_ref, do_ref, dq_ref, dk_ref, dv_ref, km, vm, a_scr, dl_scr, beta_scr):
        sels = _head_masks()
        _split_heads(k_ref, km, sels)
        _split_heads(v_ref, vm, sels)
        rows = _iota2((QT, ATT), 0)
        cols = _iota2((QT, ATT), 1)
        later = _triangle(lambda r, c: r > c)
        earlier = _triangle(lambda r, c: r < c)
        dk_ref[...] = jnp.zeros_like(dk_ref)
        dv_ref[...] = jnp.zeros_like(dv_ref)

        def qblock(i, _):
            qi = (q_ref[_qrows(i), :] * 0.125).astype(MXU_DTYPE)
            doi = do_ref[_qrows(i), :].astype(MXU_DTYPE)
            qm = [jnp.where(sel, qi, 0.0) for sel in sels]
            dom = [jnp.where(sel, doi, 0.0) for sel in sels]

            def first(t, carries):
                j = _keys_upto(i) - 1 - t
                strict = (cols + (j * ATT - i * QT)) < rows
                out = []
                for h in range(2):
                    z = _dot_nt(qi, km[h, _rows(j), :])
                    lb, lr, e = _log_sigmoid_parts(z)
                    lr = jnp.where(strict, lr, 0.0)
                    tail = _split_dot(lr, later, 2) + carries[h]
                    a = jnp.where(strict, jnp.exp(lb + tail), 0.0)
                    a_scr[h, j] = a
                    dl_scr[h, j] = a * _dot_nt(doi, vm[h, _rows(j), :])
                    beta_scr[h, j] = jnp.exp(lb)
                    out.append(carries[h] + jnp.sum(lr, axis=-1, keepdims=True))
                return tuple(out)

            lax.fori_loop(0, _keys_upto(i), first, (jnp.zeros((QT, 1), F32),) * 2)

            def second(j, state):
                csums, dq = state
                strict = (cols + (j * ATT - i * QT)) < rows
                out = []
                for h in range(2):
                    dl = dl_scr[h, j]
                    beta = beta_scr[h, j]
                    before = _split_dot(dl, earlier, 2) + csums[h]
                    dz = jnp.where(strict, dl * (1.0 - beta) - beta * before, 0.0).astype(MXU_DTYPE)
                    dq = dq + _dot(dz, km[h, _rows(j), :])
                    dk_ref[_rows(j), :] += _dot_tn(dz, qm[h])
                    dv_ref[_rows(j), :] += _dot_tn(a_scr[h, j], dom[h])
                    out.append(csums[h] + jnp.sum(dl, axis=-1, keepdims=True))
                return tuple(out), dq

            init = ((jnp.zeros((QT, 1), F32),) * 2, jnp.zeros((QT, BLK), F32))
            _, dq = lax.fori_loop(0, _keys_upto(i), second, init)
            dq_ref[_qrows(i), :] = dq * 0.125
            return 0

        lax.fori_loop(0, s // QT, qblock, 0)

    out = _col_spec(s, 0)
    return _host_call(
        body, carry, name=name, grid=(nb, 2),
        in_specs=[_col_spec(s, 0), _col_spec(s, 2), _col_spec(s, 4), out], out_specs=[out] * 3,
        out_shape=[jax.ShapeDtypeStruct((nb * s, 2 * BLK), F32)] * 3,
        scratch_shapes=[pltpu.VMEM((2, s, BLK), MXU_DTYPE)] * 2 + [pltpu.VMEM((2, nblk, QT, ATT), F32)] * 3,
        operands=(proj, proj, proj, dmixed))


def _pair_spec(s, width):
    return pl.BlockSpec((None, 2, s, width), lambda b, p: (b, p, 0, 0))


def _fox_fwd(proj, ccol, crow, nb, s, name, carry=None):
    nblk = s // ATT

    def body(q_ref, k_ref, v_ref, cc_ref, cr_ref, o_ref, lse_ref, km, vm):
        sels = _head_masks()
        _split_heads(k_ref, km, sels)
        _split_heads(v_ref, vm, sels)
        rows = _iota2((QT, ATT), 0)
        cols = _iota2((QT, ATT), 1)

        def qblock(i, _):
            qi = (q_ref[_qrows(i), :] * 0.125).astype(MXU_DTYPE)
            ci = [cc_ref[h, _qrows(i), :] for h in range(2)]

            def kblock(j, state):
                ms, ls, acc = state
                causal = (cols + (j * ATT - i * QT)) <= rows
                new_m, new_l, scales, parts = [], [], [], []
                for h in range(2):
                    z = _dot_nt(qi, km[h, _rows(j), :]) + (ci[h] - cr_ref[h, j][0:1, :])
                    z = jnp.where(causal, z, NEG)
                    m_new = jnp.maximum(ms[h], jnp.max(z, axis=-1, keepdims=True))
                    p = jnp.exp(z - m_new)
                    scale = jnp.exp(ms[h] - m_new)
                    new_m.append(m_new)
                    new_l.append(scale * ls[h] + jnp.sum(p, axis=-1, keepdims=True))
                    scales.append(scale)
                    parts.append(_dot(p, vm[h, _rows(j), :]))
                acc = jnp.where(sels[0], scales[0], scales[1]) * acc + parts[0] + parts[1]
                return tuple(new_m), tuple(new_l), acc

            init = ((jnp.full((QT, 1), NEG, F32),) * 2, (jnp.zeros((QT, 1), F32),) * 2, jnp.zeros((QT, BLK), F32))
            ms, ls, acc = lax.fori_loop(0, _keys_upto(i), kblock, init)
            o_ref[_qrows(i), :] = acc / jnp.where(sels[0], ls[0], ls[1])
            for h in range(2):
                lse_ref[h, _qrows(i), :] = jnp.broadcast_to(ms[h] + jnp.log(ls[h]), (QT, ATT))
            return 0

        lax.fori_loop(0, s // QT, qblock, 0)

    crow_spec = pl.BlockSpec((None, 2, nblk, 8, ATT), lambda b, p: (b, p, 0, 0, 0))
    return _host_call(
        body, carry, name=name, grid=(nb, 2),
        in_specs=[_col_spec(s, 12), _col_spec(s, 14), _col_spec(s, 16), _pair_spec(s, ATT), crow_spec],
        out_specs=[_col_spec(s, 0), _pair_spec(s, ATT)],
        out_shape=[jax.ShapeDtypeStruct((nb * s, 2 * BLK), F32), jax.ShapeDtypeStruct((nb, N_HEADS, s, ATT), F32)],
        scratch_shapes=[pltpu.VMEM((2, s, BLK), MXU_DTYPE)] * 2, operands=(proj, proj, proj, ccol, crow))


def _fox_bwd(proj, dmixed, lse, ccol, crow, nb, s, name, carry=None):
    nblk = s // ATT

    def body(q_ref, k_ref, v_ref, do_ref, lse_ref, cc_ref, cr_ref, dq_ref, dk_ref, dv_ref, dc_ref, km, vm, p_scr, dp_scr):
        sels = _head_masks()
        _split_heads(k_ref, km, sels)
        _split_heads(v_ref, vm, sels)
        rows = _iota2((QT, ATT), 0)
        cols = _iota2((QT, ATT), 1)
        dk_ref[...] = jnp.zeros_like(dk_ref)
        dv_ref[...] = jnp.zeros_like(dv_ref)
        dc_ref[...] = jnp.zeros_like(dc_ref)

        def qblock(i, _):
            qi = (q_ref[_qrows(i), :] * 0.125).astype(MXU_DTYPE)
            doi = do_ref[_qrows(i), :].astype(MXU_DTYPE)
            qm = [jnp.where(sel, qi, 0.0) for sel in sels]
            dom = [jnp.where(sel, doi, 0.0) for sel in sels]
            ci = [cc_ref[h, _qrows(i), :] for h in range(2)]
            lsei = [lse_ref[h, _qrows(i), :] for h in range(2)]

            def probs(j, h):
                z = _dot_nt(qi, km[h, _rows(j), :]) + (ci[h] - cr_ref[h, j][0:1, :])
                p = jnp.where((cols + (j * ATT - i * QT)) <= rows, jnp.exp(z - lsei[h]), 0.0)
                return p, _dot_nt(doi, vm[h, _rows(j), :])

            def row_term(j, accs):
                out = []
                for h in range(2):
                    p, dp = probs(j, h)
                    p_scr[h, j] = p
                    dp_scr[h, j] = dp
                    out.append(accs[h] + jnp.sum(p * dp, axis=-1, keepdims=True))
                return tuple(out)

            di = lax.fori_loop(0, _keys_upto(i), row_term, (jnp.zeros((QT, 1), F32),) * 2)

            def kblock(j, dq):
                for h in range(2):
                    p = p_scr[h, j]
                    ds = p * (dp_scr[h, j] - di[h])
                    dc_ref[h, j] += jnp.broadcast_to(jnp.sum(ds, axis=0, keepdims=True), (8, ATT))
                    ds = ds.astype(MXU_DTYPE)
                    dk_ref[_rows(j), :] += _dot_tn(ds, qm[h])
                    dv_ref[_rows(j), :] += _dot_tn(p, dom[h])
                    dq = dq + _dot(ds, km[h, _rows(j), :])
                return dq

            dq = lax.fori_loop(0, _keys_upto(i), kblock, jnp.zeros((QT, BLK), F32))
            dq_ref[_qrows(i), :] = dq * 0.125
            return 0

        lax.fori_loop(0, s // QT, qblock, 0)

    crow_spec = pl.BlockSpec((None, 2, nblk, 8, ATT), lambda b, p: (b, p, 0, 0, 0))
    wide, cols_out = _pair_spec(s, ATT), _col_spec(s, 0)
    return _host_call(
        body, carry, name=name, grid=(nb, 2),
        in_specs=[_col_spec(s, 12), _col_spec(s, 14), _col_spec(s, 16), _col_spec(s, 4), wide, wide, crow_spec],
        out_specs=[cols_out, cols_out, cols_out, crow_spec],
        out_shape=[jax.ShapeDtypeStruct((nb * s, 2 * BLK), F32)] * 3 + [jax.ShapeDtypeStruct((nb, N_HEADS, nblk, 8, ATT), F32)],
        scratch_shapes=[pltpu.VMEM((2, s, BLK), MXU_DTYPE)] * 2 + [pltpu.VMEM((2, nblk, QT, ATT), F32)] * 2,
        operands=(proj, proj, proj, dmixed, lse, ccol, crow))


def _fox_gates_fwd(proj, f_bias, nb, s, name):
    chunk = 256

    def body(f_ref, b_ref, c_ref):
        lower = (_iota2((chunk, chunk), 0) >= _iota2((chunk, chunk), 1)).astype(MXU_DTYPE)
        carry = jnp.zeros((1, BLK), F32)
        for n in range(s // chunk):
            rows = pl.ds(n * chunk, chunk)
            lf, _, _ = _log_sigmoid_parts(f_ref[rows, :] + b_ref[0:1, :])
            c = _split_dot_lhs(lower, lf, 3) + carry
            c_ref[rows, :] = c
            carry = c[chunk - 1:chunk, :]

    return pl.pallas_call(
        body, name=name, grid=(nb,),
        in_specs=[pl.BlockSpec((s, BLK), lambda b: (b, (PROJ_PAD - BLK) // BLK)), pl.BlockSpec((8, BLK), lambda b: (0, 0))],
        out_specs=pl.BlockSpec((s, BLK), lambda b: (b, 0)),
        out_shape=jax.ShapeDtypeStruct((nb * s, BLK), F32), compiler_params=_cparams("parallel"),
    )(proj, f_bias)


def _fox_gates_bwd(dc, proj, f_bias, nb, s, name):
    chunk = 256

    def body(dc_ref, f_ref, b_ref, df_ref, db_ref):
        upper = (_iota2((chunk, chunk), 0) <= _iota2((chunk, chunk), 1)).astype(MXU_DTYPE)
        carry = jnp.zeros((1, BLK), F32)
        total = jnp.zeros((1, BLK), F32)
        for n in reversed(range(s // chunk)):
            rows = pl.ds(n * chunk, chunk)
            dlf = _split_dot_lhs(upper, dc_ref[rows, :], 3) + carry
            carry = dlf[0:1, :]
            pre = f_ref[rows, :] + b_ref[0:1, :]
            e = jnp.exp(-jnp.abs(pre))
            df = dlf * (jnp.where(pre >= 0.0, e, 1.0) / (1.0 + e))
            df_ref[rows, :] = df
            total = total + jnp.sum(df, axis=0, keepdims=True)

        @pl.when(pl.program_id(0) == 0)
        def _():
            db_ref[...] = jnp.zeros_like(db_ref)

        db_ref[0:1, :] += total

    return pl.pallas_call(
        body, name=name, grid=(nb,),
        in_specs=[pl.BlockSpec((s, BLK), lambda b: (b, 0)), pl.BlockSpec((s, BLK), lambda b: (b, (PROJ_PAD - BLK) // BLK)),
                  pl.BlockSpec((8, BLK), lambda b: (0, 0))],
        out_specs=[pl.BlockSpec((s, BLK), lambda b: (b, 0)), pl.BlockSpec((8, BLK), lambda b: (0, 0))],
        out_shape=[jax.ShapeDtypeStruct((nb * s, BLK), F32), jax.ShapeDtypeStruct((8, BLK), F32)],
        compiler_params=_cparams("arbitrary"),
    )(dc, proj, f_bias)


def _delta_kernel(dmixed, o, nb, s, name):
    def body(do_ref, o_ref, d_ref):
        prod = do_ref[...] * o_ref[...]
        for h, sel in enumerate(_head_masks()):
            d_ref[h] = jnp.broadcast_to(jnp.sum(jnp.where(sel, prod, 0.0), axis=-1, keepdims=True), (s, BLK))

    return pl.pallas_call(
        body, name=name, grid=(nb, 2), in_specs=[_col_spec(s, 2), _col_spec(s, 0)], out_specs=_pair_spec(s, BLK),
        out_shape=jax.ShapeDtypeStruct((nb, N_HEADS, s, BLK), F32), compiler_params=_cparams("parallel", "parallel"),
    )(dmixed, o)


def _t5_bucket_np(dist):
    max_exact = REL_BUCKETS // 2
    nf = np.maximum(dist, 1).astype(np.float32)
    large = max_exact + (np.log(nf / max_exact) / math.log(2048 / max_exact) * (REL_BUCKETS - max_exact)).astype(np.int32)
    large = np.minimum(large, REL_BUCKETS - 1)
    return np.where(dist < max_exact, dist, large)


def _bucket_table():
    qi = np.arange(BLK)[:, None]
    kj = np.arange(2 * BLK)[None, :]
    dist = qi + BLK - kj
    tables = []
    for window, dil in DIL_PATTERNS:
        in_band = (dist >= 0) & (dist <= window // dil)
        tables.append(np.where(in_band, _t5_bucket_np(np.maximum(dist, 0) * dil), -1).astype(np.int32))
    return np.stack(tables)


def _dil_scores(qb, kp, kc, b_ref, h, prev_valid):
    zp = _dot_nt(qb, kp) + b_ref[h, :, 0:BLK]
    zp = jnp.where(prev_valid, zp, NEG)
    zc = _dot_nt(qb, kc) + b_ref[h, :, BLK:2 * BLK]
    return zp, zc


def _residue_rows(b, seg, dil):
    if dil == 1:
        return _rows128(b), _rows128(jnp.maximum(b - 1, 0)), b > 0
    r, n = b // seg, b % seg
    cur = pl.ds(r + dil * n * BLK, BLK, stride=dil)
    prev = pl.ds(r + dil * jnp.maximum(n - 1, 0) * BLK, BLK, stride=dil)
    return cur, prev, n > 0


def _dil_attention_fwd(proj, bias, nb, s, name, carry=None):
    nblk = s // BLK

    def body(q_ref, k_ref, v_ref, b_ref, out_ref, lse_ref, o_scr, l_scr):
        sels = _head_masks()
        for p, (_, dil) in enumerate(DIL_PATTERNS):
            seg = s // dil // BLK

            def block(b, _, p=p, seg=seg, dil=dil):
                cur, prev, has_prev = _residue_rows(b, seg, dil)
                qb = (q_ref[cur, :] * 0.125).astype(MXU_DTYPE)
                kp, kc = k_ref[prev, :].astype(MXU_DTYPE), k_ref[cur, :].astype(MXU_DTYPE)
                vp, vc = v_ref[prev, :].astype(MXU_DTYPE), v_ref[cur, :].astype(MXU_DTYPE)
                acc = jnp.zeros((BLK, BLK), F32)
                for h, sel in enumerate(sels):
                    zp, zc = _dil_scores(qb, jnp.where(sel, kp, 0.0), jnp.where(sel, kc, 0.0), b_ref.at[p], h, has_prev)
                    m = jnp.maximum(jnp.max(zp, axis=-1, keepdims=True), jnp.max(zc, axis=-1, keepdims=True))
                    pp = jnp.exp(zp - m)
                    pc = jnp.exp(zc - m)
                    den = jnp.sum(pp, axis=-1, keepdims=True) + jnp.sum(pc, axis=-1, keepdims=True)
                    acc = acc + (_dot(pp, jnp.where(sel, vp, 0.0)) + _dot(pc, jnp.where(sel, vc, 0.0))) / den
                    l_scr[p, h, cur, :] = jnp.broadcast_to(m + jnp.log(den), (BLK, BLK))
                o_scr[p, cur, :] = acc
                return 0

            lax.fori_loop(0, nblk, block, 0, unroll=4)

        weights, dens = [], []
        for h in range(2):
            m = jnp.maximum(jnp.maximum(l_scr[0, h], l_scr[1, h]), l_scr[2, h])
            w = [jnp.exp(l_scr[p, h] - m) for p in range(3)]
            den = w[0] + w[1] + w[2]
            lse_ref[h] = m + jnp.log(den)
            weights.append(w)
            dens.append(den)
        num = sum(jnp.where(sels[0], weights[0][p], weights[1][p]) * o_scr[p] for p in range(3))
        out_ref[...] = num / jnp.where(sels[0], dens[0], dens[1])

    bias_spec = pl.BlockSpec((3, 2, BLK, 2 * BLK), lambda b, p: (0, p, 0, 0))
    return _host_call(
        body, carry, name=name, grid=(nb, 2), in_specs=[_col_spec(s, 6), _col_spec(s, 8), _col_spec(s, 10), bias_spec],
        out_specs=[_col_spec(s, 0), _pair_spec(s, BLK)],
        out_shape=[jax.ShapeDtypeStruct((nb * s, 2 * BLK), F32), jax.ShapeDtypeStruct((nb, N_HEADS, s, BLK), F32)],
        scratch_shapes=[pltpu.VMEM((3, s, BLK), F32), pltpu.VMEM((3, 2, s, BLK), F32)], operands=(proj, proj, proj, bias))


def _dil_attention_bwd(proj, dmixed, lse, delta, bias, nb, s, name, carry=None):
    nblk = s // BLK

    def body(q_ref, k_ref, v_ref, do_ref, lse_ref, dl_ref, b_ref, dq_ref, dk_ref, dv_ref, g_ref):
        sels = _head_masks()
        dq_ref[...] = jnp.zeros_like(dq_ref)
        dk_ref[...] = jnp.zeros_like(dk_ref)
        dv_ref[...] = jnp.zeros_like(dv_ref)
        g_ref[...] = jnp.zeros_like(g_ref)
        for p, (_, dil) in enumerate(DIL_PATTERNS):
            seg = s // dil // BLK

            def block(b, _, p=p, seg=seg, dil=dil):
                cur, prev, has_prev = _residue_rows(b, seg, dil)
                qb = (q_ref[cur, :] * 0.125).astype(MXU_DTYPE)
                dob = do_ref[cur, :].astype(MXU_DTYPE)
                kp, kc = k_ref[prev, :].astype(MXU_DTYPE), k_ref[cur, :].astype(MXU_DTYPE)
                vp, vc = v_ref[prev, :].astype(MXU_DTYPE), v_ref[cur, :].astype(MXU_DTYPE)
                dq = jnp.zeros((BLK, BLK), F32)
                dkp, dkc, dvp, dvc = dq, dq, dq, dq
                for h, sel in enumerate(sels):
                    kph, kch = jnp.where(sel, kp, 0.0), jnp.where(sel, kc, 0.0)
                    qh, doh = jnp.where(sel, qb, 0.0), jnp.where(sel, dob, 0.0)
                    lse_h = lse_ref[h, cur, :]
                    dlt = dl_ref[h, cur, :]
                    zp, zc = _dil_scores(qb, kph, kch, b_ref.at[p], h, has_prev)
                    pp = jnp.exp(zp - lse_h)
                    pc = jnp.exp(zc - lse_h)
                    dsp = pp * (_dot_nt(dob, jnp.where(sel, vp, 0.0)) - dlt)
                    dsc = pc * (_dot_nt(dob, jnp.where(sel, vc, 0.0)) - dlt)
                    g_ref[h, p, :, 0:BLK] += dsp
                    g_ref[h, p, :, BLK:2 * BLK] += dsc
                    dsp = dsp.astype(MXU_DTYPE)
                    dsc = dsc.astype(MXU_DTYPE)
                    dq = dq + _dot(dsp, kph) + _dot(dsc, kch)
                    dkp, dkc = dkp + _dot_tn(dsp, qh), dkc + _dot_tn(dsc, qh)
                    dvp, dvc = dvp + _dot_tn(pp, doh), dvc + _dot_tn(pc, doh)
                dq_ref[cur, :] += dq * 0.125
                dk_ref[prev, :] += dkp
                dk_ref[cur, :] += dkc
                dv_ref[prev, :] += dvp
                dv_ref[cur, :] += dvc
                return 0

            lax.fori_loop(0, nblk, block, 0, unroll=4)

    bias_spec = pl.BlockSpec((3, 2, BLK, 2 * BLK), lambda b, p: (0, p, 0, 0))
    cols, stats = _col_spec(s, 0), _pair_spec(s, BLK)
    return _host_call(
        body, carry, name=name, grid=(nb, 2),
        in_specs=[_col_spec(s, 6), _col_spec(s, 8), _col_spec(s, 10), _col_spec(s, 2), stats, stats, bias_spec],
        out_specs=[cols, cols, cols, pl.BlockSpec((None, 2, 3, BLK, 2 * BLK), lambda b, p: (b, p, 0, 0, 0))],
        out_shape=[jax.ShapeDtypeStruct((nb * s, 2 * BLK), F32)] * 3 + [jax.ShapeDtypeStruct((nb, N_HEADS, 3, BLK, 2 * BLK), F32)],
        operands=(proj, proj, proj, dmixed, lse, delta, bias))


def _bucket_reduce(gbias, table, name):
    nb = gbias.shape[0]

    def body(g_ref, t_ref, o_ref):
        row = _iota2((8, BLK), 0)
        lane = _iota2((8, BLK), 1)
        gsum = [[sum(g_ref[b, h, p] for b in range(nb)) for p in range(3)] for h in range(N_HEADS)]

        def bucket(k, acc):
            for h in range(N_HEADS):
                tot = sum(jnp.sum(jnp.where(t_ref[p] == k, gsum[h][p], 0.0)) for p in range(3))
                acc = acc + jnp.where((row == h) & (lane == k), tot, 0.0)
            return acc

        o_ref[...] = lax.fori_loop(0, REL_BUCKETS, bucket, jnp.zeros((8, BLK), F32))

    vm = pl.BlockSpec(memory_space=pltpu.VMEM)
    return pl.pallas_call(
        body, name=name, in_specs=[vm, vm], out_specs=vm, out_shape=jax.ShapeDtypeStruct((8, BLK), F32),
        compiler_params=pltpu.CompilerParams(vmem_limit_bytes=VMEM_LIMIT),
    )(gbias, table)


def _place():
    x, y, c = lax.axis_index("x"), lax.axis_index("y"), lax.axis_index("c")
    others = [(1 - x, y), (x, 1 - y), (1 - x, 1 - y)]
    return x, y, c, others


def _remote(src, dst, send_sem, recv_sem, to):
    return pltpu.make_async_remote_copy(src_ref=src, dst_ref=dst, send_sem=send_sem, recv_sem=recv_sem,
                                        device_id=to, device_id_type=MESH)


_HBM = pl.BlockSpec(memory_space=pl.ANY)


class _Exchange:
    def __init__(self, operands, out_shape, n_copies, copies, aliases=None):
        self.operands, self.out_shape, self.n_copies, self.copies = list(operands), list(out_shape), n_copies, copies
        self.aliases = dict(aliases or {})

    def sem_shapes(self):
        return [pltpu.SemaphoreType.DMA((self.n_copies,)), pltpu.SemaphoreType.DMA((self.n_copies,))]


def _start_all(sends):
    for cp in sends:
        cp.start()


def _wait_all(sends, arrivals):
    for cp in arrivals:
        cp.wait_recv()
    for cp in sends:
        cp.wait_send()


def _run_exchange(ex, name):
    ni = len(ex.operands)

    def body(*refs):
        sends, arrivals = ex.copies(refs[:ni], refs[ni:-2], refs[-2], refs[-1])
        _start_all(sends)
        _wait_all(sends, arrivals)

    return list(pl.pallas_call(
        body, name=name, in_specs=[_HBM] * ni, out_specs=[_HBM] * len(ex.out_shape), out_shape=ex.out_shape,
        scratch_shapes=ex.sem_shapes(), input_output_aliases=ex.aliases)(*ex.operands))


def _host_call(body, carry, *, name, grid, in_specs, out_specs, out_shape, operands, scratch_shapes=()):
    in_specs, out_specs, out_shape, scratch_shapes = list(in_specs), list(out_specs), list(out_shape), list(scratch_shapes)
    if carry is None:
        res = pl.pallas_call(body, name=name, grid=grid, in_specs=in_specs, out_specs=out_specs, out_shape=out_shape,
                             scratch_shapes=scratch_shapes, compiler_params=_cparams(*["parallel"] * len(grid)))(*operands)
        return list(res), []
    n_in, n_out, n_scr, c_in, c_out = len(in_specs), len(out_specs), len(scratch_shapes), len(carry.operands), len(carry.out_shape)
    steps = math.prod(grid)

    def wrapped(*refs):
        ins, refs = refs[:n_in], refs[n_in:]
        c_ins, refs = refs[:c_in], refs[c_in:]
        outs, refs = refs[:n_out], refs[n_out:]
        c_outs, refs = refs[:c_out], refs[c_out:]
        scr, (send_sems, recv_sems) = refs[:n_scr], refs[n_scr:]
        step = 0
        for d, size in enumerate(grid):
            step = step * size + pl.program_id(d)

        @pl.when(step == 0)
        def _():
            _start_all(carry.copies(c_ins, c_outs, send_sems, recv_sems)[0])

        body(*ins, *outs, *scr)

        @pl.when(step == steps - 1)
        def _():
            _wait_all(*carry.copies(c_ins, c_outs, send_sems, recv_sems))

    res = pl.pallas_call(
        wrapped, name=name, grid=grid, in_specs=in_specs + [_HBM] * c_in, out_specs=out_specs + [_HBM] * c_out,
        out_shape=out_shape + carry.out_shape, scratch_shapes=scratch_shapes + carry.sem_shapes(),
        input_output_aliases={n_in + i: n_out + j for i, j in carry.aliases.items()},
        compiler_params=_cparams(*["arbitrary"] * len(grid)))(*operands, *carry.operands)
    return list(res[:n_out]), list(res[n_out:])


def _half(which, rows):
    h = rows // 2
    return pl.ds(pl.multiple_of(which * h, 16), h)


def _like(arrays, shape_of=lambda t: t.shape):
    return [jax.ShapeDtypeStruct(shape_of(t), t.dtype) for t in arrays]


def _gather_ici(shards, layer):
    n = len(shards)

    def copies(ins, outs, send_sems, recv_sems, base=0):
        x, y, c, others = _place()
        me = 2 * x + y
        sends, arrivals = [], []
        for a in range(n):
            rows = _half(c, shards[a].shape[1])
            for k, (ox, oy) in enumerate(others):
                sems = (send_sems.at[base + 3 * a + k], recv_sems.at[base + 3 * a + k],(ox, oy, c))
                sends.append(_remote(ins[a].at[layer, rows], outs[a].at[me, rows], *sems))
                landed = outs[a].at[2 * ox + oy, rows]
                arrivals.append(_remote(landed, landed, *sems))
        return sends, arrivals

    return _Exchange(shards, _like(shards, lambda t: (N_CHIPS,) + t.shape[1:]), 3 * n, copies)


def _gather_d2d(gathered):
    n = len(gathered)

    def copies(ins, outs, send_sems, recv_sems, base=0):
        x, y, c, others = _place()
        sends, arrivals = [], []
        for a in range(n):
            r = gathered[a].shape[1]
            for k, (ox, oy) in enumerate(others):
                sems = (send_sems.at[base + 3 * a + k], recv_sems.at[base + 3 * a + k],(x, y, 1 - c))
                mine, theirs = outs[a].at[2 * ox + oy, _half(c, r)], outs[a].at[2 * ox + oy, _half(1 - c, r)]
                sends.append(_remote(mine, mine, *sems))
                arrivals.append(_remote(theirs, theirs, *sems))
        return sends, arrivals

    return _Exchange(gathered, _like(gathered), 3 * n, copies, aliases={a: a for a in range(n)})


def _swap_halves(g):
    n = len(g)

    def copies(ins, outs, send_sems, recv_sems, base=0):
        x, y, c, _ = _place()
        sends, arrivals = [], []
        for a in range(n):
            sems = (send_sems.at[base + a], recv_sems.at[base + a], (x, y, 1 - c))
            sends.append(_remote(ins[a].at[:, _half(1 - c, g[a].shape[1])], outs[a], *sems))
            arrivals.append(_remote(outs[a], outs[a], *sems))
        return sends, arrivals

    return _Exchange(g, _like(g, lambda t: (t.shape[0], t.shape[1] // 2, t.shape[2])), n, copies)


def _scatter_shards(ps):
    n = len(ps)

    def copies(ins, outs, send_sems, recv_sems, base=0):
        x, y, c, others = _place()
        me = 2 * x + y
        sends, arrivals = [], []
        for a in range(n):
            for k, (ox, oy) in enumerate(others):
                sems = (send_sems.at[base + 3 * a + k], recv_sems.at[base + 3 * a + k],(ox, oy, c))
                sends.append(_remote(ins[a].at[2 * ox + oy], outs[a].at[me], *sems))
                slot = outs[a].at[2 * ox + oy]
                arrivals.append(_remote(slot, slot, *sems))
        return sends, arrivals

    return _Exchange(ps, _like(ps), 3 * n, copies)


def _share_halves(mine):
    n = len(mine)

    def copies(ins, outs, send_sems, recv_sems, base=0):
        x, y, c, _ = _place()
        sends, arrivals = [], []
        for a in range(n):
            sems = (send_sems.at[base + a], recv_sems.at[base + a], (x, y, 1 - c))
            sends.append(_remote(ins[a], outs[a], *sems))
            arrivals.append(_remote(outs[a], outs[a], *sems))
        return sends, arrivals

    return _Exchange(mine, _like(mine), n, copies)


def _row_tile(r):
    for cand in (256, 352, 128):
        if r % cand == 0:
            return cand
    return r


def _pair_sum(g, other, core, name):
    ns, h, w = other.shape
    tr = _row_tile(h)
    per_half = h // tr

    def body(core_ref, g_ref, o_ref, out_ref):
        out_ref[...] = (g_ref[...] + o_ref[...]).astype(out_ref.dtype)

    blk = pl.BlockSpec((None, tr, w), lambda k, i, core_ref: (k, i, 0))
    grid_spec = pltpu.PrefetchScalarGridSpec(
        num_scalar_prefetch=1, grid=(ns, per_half),
        in_specs=[pl.BlockSpec((None, tr, w), lambda k, i, core_ref: (k, core_ref[0] * per_half + i, 0)), blk], out_specs=blk)
    return pl.pallas_call(
        body, name=name, grid_spec=grid_spec, out_shape=jax.ShapeDtypeStruct((ns, h, w), MXU_DTYPE),
        compiler_params=_cparams("parallel", "parallel"),
    )(core.reshape(1).astype(jnp.int32), g, other)


def _chip_sum(q, p, chip, name):
    ns, r, w = q.shape
    tr = _row_tile(r)

    def body(chip_ref, q_ref, own_ref, out_ref):
        me = chip_ref[0]
        own = own_ref[...].astype(F32)
        terms = [jnp.where(me == k, own, q_ref[k].astype(F32)) for k in range(ns)]
        out_ref[...] = ((terms[0] + terms[1]) + terms[2]) + terms[3]

    grid_spec = pltpu.PrefetchScalarGridSpec(
        num_scalar_prefetch=1, grid=(r // tr,),
        in_specs=[pl.BlockSpec((ns, tr, w), lambda i, chip_ref: (0, i, 0)),
                  pl.BlockSpec((None, tr, w), lambda i, chip_ref: (chip_ref[0], i, 0))],
        out_specs=pl.BlockSpec((tr, w), lambda i, chip_ref: (i, 0)))
    return pl.pallas_call(
        body, name=name, grid_spec=grid_spec, out_shape=jax.ShapeDtypeStruct((r, w), F32),
        compiler_params=_cparams("parallel"),
    )(chip.reshape(1).astype(jnp.int32), q, p)


def _merge(exchanges):
    if len(exchanges) <= 1:
        return exchanges[0] if exchanges else None
    operands, out_shape, aliases, spans, n = [], [], {}, [], 0
    for ex in exchanges:
        spans.append((len(operands), len(out_shape), n))
        aliases.update({len(operands) + i: len(out_shape) + j for i, j in ex.aliases.items()})
        operands += ex.operands
        out_shape += ex.out_shape
        n += ex.n_copies

    def copies(ins, outs, send_sems, recv_sems, base=0):
        sends, arrivals = [], []
        for ex, (i0, o0, s0) in zip(exchanges, spans):
            s, a = ex.copies(ins[i0:i0 + len(ex.operands)], outs[o0:o0 + len(ex.out_shape)], send_sems, recv_sems, base + s0)
            sends += s
            arrivals += a
        return sends, arrivals

    return _Exchange(operands, out_shape, n, copies, aliases)


def _take(hooks, host):
    stages = (hooks or {}).pop(host, [])
    exchanges = [make() for make, _ in stages]

    def finish(results):
        for (_, done), ex in zip(stages, exchanges):
            done(results[:len(ex.out_shape)])
            results = results[len(ex.out_shape):]

    return _merge(exchanges), finish


def _hook(hooks, host, make, done):
    hooks.setdefault(host, []).append((make, done))


class _WeightPrefetch:
    def __init__(self, names, shards, layer, chip):
        self.names, self.shards, self.layer, self.chip, self.result = names, [shards[n] for n in names], layer, chip, None

    def first(self):
        return _gather_ici(self.shards, self.layer)

    def got_first(self, arrived):
        self.arrived = arrived

    def second(self):
        return _gather_d2d(self.arrived)

    def got_second(self, gathered):
        self.result = {name: lax.dynamic_update_index_in_dim(got, own[self.layer], self.chip, 0)
                       for name, got, own in zip(self.names, gathered, self.shards)}

    def ride(self, hooks, first_host, second_host):
        _hook(hooks, first_host, self.first, self.got_first)
        _hook(hooks, second_host, self.second, self.got_second)

    def run(self, tag):
        self.got_first(_run_exchange(self.first(), f"gather_ici_{tag}"))
        self.got_second(_run_exchange(self.second(), f"gather_d2d_{tag}"))


class _GradReduce:
    def __init__(self, g, chip, core, tag):
        self.names, self.g, self.chip, self.core, self.tag, self.result = list(g), list(g.values()), chip, core, tag, None

    def swap(self):
        return _swap_halves(self.g)

    def got_swap(self, theirs):
        self.pair = [_pair_sum(g, t, self.core, f"pair_sum_{n}_{self.tag}") for n, g, t in zip(self.names, self.g, theirs)]

    def scatter(self):
        return _scatter_shards(self.pair)

    def got_scatter(self, q):
        self.mine = [_chip_sum(qa, pa, self.chip, f"chip_sum_{n}_{self.tag}") for n, qa, pa in zip(self.names, q, self.pair)]

    def share(self):
        return _share_halves(self.mine)

    def got_share(self, theirs):
        self.result = {n: jnp.where(self.core == 0, jnp.concatenate([a, b]), jnp.concatenate([b, a]))
                       for n, a, b in zip(self.names, self.mine, theirs)}

    def ride(self, hooks, swap_host, scatter_host, share_host):
        _hook(hooks, swap_host, self.swap, self.got_swap)
        _hook(hooks, scatter_host, self.scatter, self.got_scatter)
        _hook(hooks, share_host, self.share, self.got_share)

    def run(self):
        self.got_swap(_run_exchange(self.swap(), f"swap_halves_{self.tag}"))
        self.got_scatter(_run_exchange(self.scatter(), f"scatter_shards_{self.tag}"))
        self.got_share(_run_exchange(self.share(), f"share_halves_{self.tag}"))


class _LayerWeights:
    def __init__(self, gathered):
        self.gathered, self.made = gathered, {}

    def __getitem__(self, key):
        if key not in self.made:
            cols = lambda t: jnp.swapaxes(t, 0, 1).reshape(t.shape[1], -1)
            rows = lambda t: t.reshape(-1, t.shape[2])
            if key == "w_in":
                made = jnp.pad(cols(self.gathered("w_in")), ((0, 0), (0, PROJ_PAD - PROJ)))
            elif key == "w_gu":
                made = jnp.concatenate([cols(self.gathered("w_gate")), cols(self.gathered("w_up"))], axis=-1)
            else:
                made = rows(self.gathered(key))
            self.made[key] = made
        return self.made[key]


def _gather_small(pk, name):
    rows, w = pk.shape

    def body(pk_ref, all_ref, sum_ref, send_sems, recv_sems):
        x, y, c, _ = _place()
        me = 4 * x + 2 * y + c
        all_ref[me] = pk_ref[...]
        flips = [(fx, fy, fc) for fx in (0, 1) for fy in (0, 1) for fc in (0, 1)][1:]
        peers = [(x ^ fx, y ^ fy, c ^ fc) for fx, fy, fc in flips]
        sends = [_remote(pk_ref, all_ref.at[me], send_sems.at[k], recv_sems.at[k], peer) for k, peer in enumerate(peers)]
        for cp in sends:
            cp.start()
        for k, (px, py, pc) in enumerate(peers):
            slot = all_ref.at[4 * px + 2 * py + pc]
            _remote(slot, slot, send_sems.at[k], recv_sems.at[k], (px, py, pc)).wait_recv()
        for cp in sends:
            cp.wait_send()
        total = all_ref[0]
        for d in range(1, N_DEV):
            total = total + all_ref[d]
        sum_ref[...] = total

    vm = pl.BlockSpec(memory_space=pltpu.VMEM)
    return pl.pallas_call(
        body, name=name, in_specs=[vm], out_specs=[vm, vm],
        out_shape=[jax.ShapeDtypeStruct((N_DEV, rows, w), F32), jax.ShapeDtypeStruct((rows, w), F32)],
        scratch_shapes=[pltpu.SemaphoreType.DMA((7,)), pltpu.SemaphoreType.DMA((7,))],
    )(pk)


def _row_layout(c, nb, s):
    ch = jnp.swapaxes(c[:, :N_HEADS].reshape(nb, s, N_HEADS), 1, 2)
    ccol = jnp.broadcast_to(ch[..., None], (nb, N_HEADS, s, ATT))
    crow = jnp.broadcast_to(ch.reshape(nb, N_HEADS, s // ATT, 1, ATT), (nb, N_HEADS, s // ATT, 8, ATT))
    return ccol, crow


def _dil_bias(rel_bias, name):
    def body(rel_ref, t_ref, o_ref):
        for p in range(len(DIL_PATTERNS)):
            table = t_ref[p]

            def bucket(k, accs, table=table):
                return tuple(jnp.where(table == k, rel_ref[k, h], acc) for h, acc in enumerate(accs))

            accs = lax.fori_loop(0, REL_BUCKETS, bucket, tuple(jnp.full((BLK, 2 * BLK), NEG, F32) for _ in range(N_HEADS)))
            for h in range(N_HEADS):
                o_ref[p, h] = accs[h]

    vm = pl.BlockSpec(memory_space=pltpu.VMEM)
    return pl.pallas_call(
        body, name=name, in_specs=[pl.BlockSpec(memory_space=pltpu.SMEM), vm], out_specs=vm,
        out_shape=jax.ShapeDtypeStruct((len(DIL_PATTERNS), N_HEADS, BLK, 2 * BLK), F32),
        compiler_params=pltpu.CompilerParams(vmem_limit_bytes=VMEM_LIMIT),
    )(rel_bias, jnp.asarray(_bucket_table()))


def _layer_forward(x, wts, small, nb, s, tag, hooks=None):
    proj = _matmul(x, wts["w_in"], "proj", tag)

    carry, finish = _take(hooks, "sb_fwd")
    o_sb, carried = _sb_fwd(proj, nb, s, f"sb_fwd_{tag}", carry)
    finish(carried)

    bias = _dil_bias(small["rel_bias"], f"dil_bias_{tag}")
    carry, finish = _take(hooks, "dil_fwd")
    (o_dl, lse_dl), carried = _dil_attention_fwd(proj, bias, nb, s, f"dil_fwd_{tag}", carry)
    finish(carried)

    fb = jnp.zeros((8, BLK), F32).at[0, :N_HEADS].set(small["f_bias"])
    csum = _fox_gates_fwd(proj, fb, nb, s, f"fox_gates_{tag}")
    ccol, crow = _row_layout(csum, nb, s)
    carry, finish = _take(hooks, "fox_fwd")
    (o_fx, lse_fx), carried = _fox_fwd(proj, ccol, crow, nb, s, f"fox_fwd_{tag}", carry)
    finish(carried)

    cw = jnp.zeros((8, CONV_W), F32).at[:3].set(small["conv_w"])
    o_cv = _conv_fwd(proj, cw, nb, s, f"conv_fwd_{tag}")

    mixed = jnp.concatenate([o_sb, o_dl, o_fx, o_cv], axis=-1).astype(MXU_DTYPE)
    pre1, x1 = _matmul_post_norm(mixed, wts["w_out"], x, small["ln1_g"], small["ln1_b"], f"out_proj_ln1_{tag}")
    carry, finish = _take(hooks, "ffn_in")
    (gate, up, hid), carried = _ffn_in(x1, wts["w_gu"], f"ffn_in_{tag}", carry)
    finish(carried)
    pre2, x2 = _matmul_post_norm(hid, wts["w_down"], x1, small["ln2_g"], small["ln2_b"], f"ffn_out_ln2_{tag}")
    saved = dict(x=x, proj=proj, bias=bias, o_dl=o_dl, lse_dl=lse_dl, fb=fb, ccol=ccol, crow=crow, o_fx=o_fx,
                 lse_fx=lse_fx, cw=cw, mixed=mixed, pre1=pre1, x1=x1, gate=gate, up=up, hid=hid, pre2=pre2)
    return x2, saved


def _layer_backward(dx2, sv, wts, small, nb, s, tag, hooks=None, ffn_grads_ready=None):
    t = nb * s
    dpre2, dgb2 = _ln_bwd(dx2, sv["pre2"], small["ln2_g"], f"ln2_bwd_{tag}")
    dpre2_b = dpre2.astype(MXU_DTYPE)
    carry, finish = _take(hooks, "ffn_out_dx")
    (dgate, dup), carried = _ffn_out_dx(dpre2_b, wts["w_down"], sv["gate"], sv["up"], f"ffn_out_dx_{tag}", carry)
    finish(carried)
    dw_down = _matmul(sv["hid"], dpre2_b, "ffn_out_dw", tag, trans_a=True)
    dx1 = _ffn_in_dx(dgate, dup, wts["w_gu"], dpre2, f"ffn_in_dx_{tag}")
    x1_b = sv["x1"].astype(MXU_DTYPE)
    dw_gate = _matmul(x1_b, dgate, "ffn_in_dw", f"{tag}_gate", trans_a=True)
    dw_up = _matmul(x1_b, dup, "ffn_in_dw", f"{tag}_up", trans_a=True)
    if ffn_grads_ready:
        ffn_grads_ready(dw_down, dw_gate, dw_up)

    dpre1, dgb1 = _ln_bwd(dx1, sv["pre1"], small["ln1_g"], f"ln1_bwd_{tag}")
    dpre1_b = dpre1.astype(MXU_DTYPE)
    dmixed = _matmul(dpre1_b, wts["w_out"], "out_proj_dx", tag, trans_b=True)
    dw_out = _matmul(sv["mixed"], dpre1_b, "out_proj_dw", tag, trans_a=True)
    proj = sv["proj"]

    carry, finish = _take(hooks, "sb_bwd")
    (dq_sb, dk_sb, dv_sb), carried = _sb_bwd(proj, dmixed, nb, s, f"sb_bwd_{tag}", carry)
    finish(carried)

    delta_dl = _delta_kernel(dmixed, sv["o_dl"], nb, s, f"dil_delta_{tag}")
    carry, finish = _take(hooks, "dil_bwd")
    (dq_dl, dk_dl, dv_dl, gbias), carried = _dil_attention_bwd(proj, dmixed, sv["lse_dl"], delta_dl, sv["bias"], nb, s,
                                                               f"dil_bwd_{tag}", carry)
    finish(carried)
    drel = _bucket_reduce(gbias, jnp.asarray(_bucket_table()), f"rel_bias_grad_{tag}")

    carry, finish = _take(hooks, "fox_bwd")
    (dq_fx, dk_fx, dv_fx, dcol), carried = _fox_bwd(proj, dmixed, sv["lse_fx"], sv["ccol"], sv["crow"], nb, s,
                                                    f"fox_bwd_{tag}", carry)
    finish(carried)
    dcs = -jnp.swapaxes(dcol[:, :, :, 0, :].reshape(nb, N_HEADS, s), 1, 2).reshape(t, N_HEADS)
    dcs = jnp.pad(dcs, ((0, 0), (0, BLK - N_HEADS)))
    dfx, dfb = _fox_gates_bwd(dcs, proj, sv["fb"], nb, s, f"fox_gates_bwd_{tag}")

    dgates, dcw = _conv_bwd(dmixed, proj, sv["cw"], nb, s, f"conv_bwd_{tag}")

    dproj = jnp.concatenate([dq_sb, dk_sb, dv_sb, dq_dl, dk_dl, dv_dl, dq_fx, dk_fx, dv_fx, dgates, dfx],
                            axis=-1).astype(MXU_DTYPE)
    dx = _matmul(dproj, wts["w_in"], "proj_dx", tag, add=dpre1, add_scale=ALPHA, trans_b=True)
    dw_in = _matmul(sv["x"].astype(MXU_DTYPE), dproj, "proj_dw", tag, trans_a=True)

    grads = dict(w_in=dw_in[:, :PROJ], w_out=dw_out, w_gate=dw_gate, w_up=dw_up, w_down=dw_down,
                 ln1_g=dgb1[0], ln1_b=dgb1[1], ln2_g=dgb2[0], ln2_b=dgb2[1], conv_w=dcw[:3], f_bias=dfb[0, :N_HEADS],
                 rel_bias=drel[:N_HEADS, :REL_BUCKETS].T)
    return dx, grads


class _NoExchanges:
    def forward_hooks(self, layer):
        return None

    def backward_hooks(self, layer):
        return None

    def ffn_grads_ready(self, layer):
        return None

    def layer_done(self, layer, grads):
        pass


def _local_step(x, target, weights_of, small_all, schedule=None):
    schedule = schedule or _NoExchanges()
    nb, s, d = x.shape
    h = x.reshape(nb * s, d)
    saved = []
    for layer in range(DEPTH):
        wts = weights_of(layer)
        h, sv = _layer_forward(h, wts, small_all[layer], nb, s, f"l{layer}", schedule.forward_hooks(layer))
        saved.append((sv, wts))
    dy, lossp = _loss_kernel(h, target.reshape(nb * s, d), "loss")
    grads = [None] * DEPTH
    for layer in reversed(range(DEPTH)):
        sv, wts = saved[layer]
        dy, grads[layer] = _layer_backward(dy, sv, wts, small_all[layer], nb, s, f"l{layer}",
                                           schedule.backward_hooks(layer), schedule.ffn_grads_ready(layer))
        schedule.layer_done(layer, grads[layer])
    return lossp, dy.reshape(nb, s, d), grads


_BIG = ("w_in", "w_out", "w_gate", "w_up", "w_down")
_COL_SHARDED = ("w_in", "w_gate", "w_up")


class _Schedule:
    def __init__(self, shards, chip, core):
        self.chip, self.core, self.reduces = chip, core, [[] for _ in range(DEPTH)]
        first = _WeightPrefetch(["w_in"], shards, 0, chip)
        first.run("l0_w_in")
        rest = _WeightPrefetch(["w_out", "w_gate", "w_up", "w_down"], shards, 0, chip)
        ahead_a = _WeightPrefetch(["w_in", "w_out", "w_down"], shards, 1, chip)
        ahead_b = _WeightPrefetch(["w_gate", "w_up"], shards, 1, chip)
        self.fetches = [[first, rest], [ahead_a, ahead_b]]
        self.forward, self.backward = [{} for _ in range(DEPTH)], [{} for _ in range(DEPTH)]
        rest.ride(self.forward[0], "sb_fwd", "fox_fwd")
        ahead_a.ride(self.forward[0], "dil_fwd", "ffn_in")
        ahead_b.ride(self.forward[0], "fox_fwd", "ffn_in")

    def weights(self, layer):
        def gathered(name):
            return next(f.result[name] for f in self.fetches[layer] if name in f.names)
        return _LayerWeights(gathered)

    def forward_hooks(self, layer):
        return self.forward[layer]

    def backward_hooks(self, layer):
        return self.backward[layer]

    def _reduce(self, layer, grads, tag):
        red = _GradReduce({name: _by_chip(name, g) for name, g in grads.items()}, self.chip, self.core, tag)
        self.reduces[layer].append(red)
        return red

    def ffn_grads_ready(self, layer):
        if layer != 0:
            return None

        def ready(dw_down, dw_gate, dw_up):
            red = self._reduce(0, dict(w_gate=dw_gate, w_up=dw_up, w_down=dw_down), "l0_ffn")
            red.ride(self.backward[0], "sb_bwd", "dil_bwd", "fox_bwd")

        return ready

    def layer_done(self, layer, grads):
        if layer == 1:
            self._reduce(1, {name: grads[name] for name in _BIG}, "l1").ride(self.backward[0], "ffn_out_dx", "sb_bwd", "fox_bwd")
        else:
            self._reduce(0, dict(w_in=grads["w_in"], w_out=grads["w_out"]), "l0_attn").run()

    def reduced(self, layer, name):
        return next(r.result[name] for r in self.reduces[layer] if name in r.names)


def _by_chip(name, g):
    if name in _COL_SHARDED:
        return jnp.swapaxes(g.reshape(g.shape[0], N_CHIPS, -1), 0, 1)
    return g.reshape(N_CHIPS, -1, g.shape[1])


_SMALL_LAYOUT = (("ln1_g", 0), ("ln1_b", 2), ("ln2_g", 4), ("ln2_b", 6), ("conv_w", 8))
_ROW_MISC = 10
_ROW_LOSS = 11


def _pack_small(per_layer, rel_bias, loss=None):
    pk = jnp.zeros((SMALL_ROWS, D_MODEL), F32)
    for name, row in _SMALL_LAYOUT:
        for l in range(DEPTH):
            v = per_layer[l][name].reshape(-1)
            pk = pk.at[row + l, :v.shape[0]].set(v)
    fb = jnp.concatenate([per_layer[l]["f_bias"] for l in range(DEPTH)])
    pk = pk.at[_ROW_MISC, :2 * N_HEADS].set(fb)
    pk = pk.at[_ROW_MISC, BLK:BLK + REL_BUCKETS * N_HEADS].set(rel_bias.reshape(-1))
    if loss is not None:
        pk = pk.at[_ROW_LOSS, 0].set(loss)
    return pk


def _unpack_small(pk, conv_cols):
    out = {}
    for name, row in _SMALL_LAYOUT:
        n = 3 * conv_cols if name == "conv_w" else D_MODEL
        v = pk[row:row + DEPTH, :n]
        out[name] = v.reshape(DEPTH, 3, conv_cols) if name == "conv_w" else v
    out["f_bias"] = pk[_ROW_MISC, :2 * N_HEADS].reshape(DEPTH, N_HEADS)
    out["rel_bias"] = pk[_ROW_MISC, BLK:BLK + REL_BUCKETS * N_HEADS].reshape(REL_BUCKETS, N_HEADS)
    return out


_WEIGHTS = ("w_in", "f_bias", "conv_w", "w_out", "rel_bias", "ln1_g", "ln1_b", "w_gate", "w_up", "w_down", "ln2_g", "ln2_b")


def kernel(x, w_in, f_bias, conv_w, w_out, rel_bias, ln1_g, ln1_b, w_gate, w_up, w_down, ln2_g, ln2_b, loss_target, m_w_in, m_f_bias, m_conv_w, m_w_out, m_rel_bias, m_ln1_g, m_ln1_b, m_w_gate, m_w_up, m_w_down, m_ln2_g, m_ln2_b, v_w_in, v_f_bias, v_conv_w, v_w_out, v_rel_bias, v_ln1_g, v_ln1_b, v_w_gate, v_w_up, v_w_down, v_ln2_g, v_ln2_b):
    w = dict(w_in=w_in, f_bias=f_bias, conv_w=conv_w, w_out=w_out, rel_bias=rel_bias, ln1_g=ln1_g, ln1_b=ln1_b,
             w_gate=w_gate, w_up=w_up, w_down=w_down, ln2_g=ln2_g, ln2_b=ln2_b)
    m = dict(w_in=m_w_in, f_bias=m_f_bias, conv_w=m_conv_w, w_out=m_w_out, rel_bias=m_rel_bias, ln1_g=m_ln1_g,
             ln1_b=m_ln1_b, w_gate=m_w_gate, w_up=m_w_up, w_down=m_w_down, ln2_g=m_ln2_g, ln2_b=m_ln2_b)
    v = dict(w_in=v_w_in, f_bias=v_f_bias, conv_w=v_conv_w, w_out=v_w_out, rel_bias=v_rel_bias, ln1_g=v_ln1_g,
             ln1_b=v_ln1_b, w_gate=v_w_gate, w_up=v_w_up, w_down=v_w_down, ln2_g=v_ln2_g, ln2_b=v_ln2_b)
    chip = 2 * lax.axis_index("x") + lax.axis_index("y")
    core = lax.axis_index("c")
    conv_shard = CONV_W // N_CHIPS

    schedule = _Schedule({name: w[name].astype(MXU_DTYPE) for name in _BIG}, chip, core)
    cw_pk = jnp.zeros((8, D_MODEL), F32).at[0, :DEPTH * 3 * conv_shard].set(conv_w.reshape(-1))
    cw_all, _ = _gather_small(cw_pk, "gather_conv_w")
    cw_chips = cw_all[0::2, 0, :DEPTH * 3 * conv_shard].reshape(N_CHIPS, DEPTH, 3, conv_shard)
    conv_full = jnp.moveaxis(cw_chips, 0, 2).reshape(DEPTH, 3, CONV_W)
    small_all = [dict(f_bias=f_bias[l], conv_w=conv_full[l], rel_bias=rel_bias, ln1_g=ln1_g[l], ln1_b=ln1_b[l],
                      ln2_g=ln2_g[l], ln2_b=ln2_b[l]) for l in range(DEPTH)]

    lossp, grad_x, grads = _local_step(x, loss_target, schedule.weights, small_all, schedule)
    big_g = {name: jnp.stack([schedule.reduced(l, name) for l in range(DEPTH)]) for name in _BIG}

    drel = grads[0]["rel_bias"] + grads[1]["rel_bias"]
    small_pk = _pack_small(grads, drel, lossp[0, 0])
    _, small_sum = _gather_small(small_pk, "gather_small_grads")
    loss = small_sum[_ROW_LOSS, 0]
    small_g = _unpack_small(small_sum, CONV_W)
    small_g["conv_w"] = lax.dynamic_slice_in_dim(small_g["conv_w"], chip * conv_shard, conv_shard, axis=2)

    out_g, out_d, out_m, out_v = dict(small_g), {}, {}, {}
    for name in _BIG:
        out_g[name] = big_g[name]
        out_d[name], out_m[name], out_v[name] = _adamw(w[name], big_g[name], m[name], v[name], f"adamw_{name}")
    per_layer = lambda src: [{name: src[name][l] for name in ("ln1_g", "ln1_b", "ln2_g", "ln2_b", "conv_w", "f_bias")}
                             for l in range(DEPTH)]
    packs = [_pack_small(per_layer(src), src["rel_bias"])[None] for src in (w, small_g, m, v)]
    for dst, pk in zip((out_d, out_m, out_v), _adamw(*packs, "adamw_small")):
        dst.update(_unpack_small(pk[0], conv_shard))

    return (loss, grad_x, *[out_g[n] for n in _WEIGHTS], *[out_d[n] for n in _WEIGHTS],
            *[out_m[n] for n in _WEIGHTS], *[out_v[n] for n in _WEIGHTS])
```

```python
import functools
import math

import numpy as np
import jax
import jax.numpy as jnp
from jax import lax
from jax.experimental import pallas as pl
from jax.experimental.pallas import tpu as pltpu

F32 = jnp.float32
BF16 = jnp.bfloat16
MXU_DTYPE = BF16

D_MODEL = 1024
HEAD_DIM = 64
N_HEADS = 4
BLK = 128
ATT = 256
QT = 512
CONV_W = 256
PROJ = 3076
PROJ_PAD = 3200
D_FF = 2816
DEPTH = 2
ALPHA = (2 * DEPTH) ** 0.25
LN_EPS = 1e-5
NEG = -1e30
DIL_PATTERNS = ((128, 1), (512, 4), (2048, 16))
REL_BUCKETS = 32
N_CHIPS = 4
N_DEV = 8
SMALL_ROWS = 16

ADAM_LR = 0.001
ADAM_B1 = 0.9
ADAM_B2 = 0.999
ADAM_EPS = 1e-08
ADAM_WD = 0.01
ADAM_STEP = 10

VMEM_LIMIT = 56 * 2 ** 20
MESH = pl.DeviceIdType.MESH


def _cparams(*sem):
    return pltpu.CompilerParams(dimension_semantics=tuple(sem), vmem_limit_bytes=VMEM_LIMIT)


def _dot(a, b):
    return jnp.dot(a.astype(MXU_DTYPE), b.astype(MXU_DTYPE), preferred_element_type=F32)


def _dot_nt(a, b):
    return lax.dot_general(a.astype(MXU_DTYPE), b.astype(MXU_DTYPE), (((1,), (1,)), ((), ())),
                           preferred_element_type=F32)


def _dot_tn(a, b):
    return lax.dot_general(a.astype(MXU_DTYPE), b.astype(MXU_DTYPE), (((0,), (0,)), ((), ())),
                           preferred_element_type=F32)


def _split_dot(x, ones, passes):
    acc, rest = None, x
    for p in range(passes):
        piece = rest.astype(MXU_DTYPE)
        part = jnp.dot(piece, ones, preferred_element_type=F32)
        acc = part if acc is None else acc + part
        if p + 1 < passes:
            rest = rest - piece.astype(F32)
    return acc


def _split_dot_lhs(ones, x, passes):
    acc, rest = None, x
    for p in range(passes):
        piece = rest.astype(MXU_DTYPE)
        part = jnp.dot(ones, piece, preferred_element_type=F32)
        acc = part if acc is None else acc + part
        if p + 1 < passes:
            rest = rest - piece.astype(F32)
    return acc


def _iota2(shape, axis):
    return lax.broadcasted_iota(jnp.int32, shape, axis)


_TILES = {"proj": (1024, 640, 1024), "ffn_out_dw": (1408, 1024, 2048),
          "ffn_in_dw": (1024, 1408, 2048), "out_proj_dx": (1024, 1024, 1024),
          "out_proj_dw": (1024, 1024, 2048), "proj_dx": (1024, 512, 3200), "proj_dw": (1024, 640, 2048)}


def _matmul(a, b, kind, tag, *, out_dtype=F32, add=None, add_scale=1.0, trans_a=False, trans_b=False):
    k, m = a.shape if trans_a else a.shape[::-1]
    n = b.shape[0] if trans_b else b.shape[1]
    tm, tn, tk = _TILES[kind]
    tm, tk, name = min(tm, m), min(tk, k), f"{kind}_{tag}"
    assert m % tm == 0 and n % tn == 0 and k % tk == 0, (a.shape, b.shape, tm, tn, tk)
    nk = k // tk

    def body(*refs):
        if add is None:
            a_ref, b_ref, o_ref = refs[:3]
            c_ref, scr = None, refs[3:]
        else:
            a_ref, b_ref, c_ref, o_ref = refs[:4]
            scr = refs[4:]
        dot = _dot_tn if trans_a else _dot_nt if trans_b else _dot
        part = dot(a_ref[...], b_ref[...])

        def finish(acc):
            if c_ref is not None:
                acc = acc + add_scale * c_ref[...]
            o_ref[...] = acc.astype(out_dtype)

        if nk == 1:
            finish(part)
        else:
            acc_ref = scr[0]
            kk = pl.program_id(2)

            @pl.when(kk == 0)
            def _():
                acc_ref[...] = part

            @pl.when(kk > 0)
            def _():
                acc_ref[...] += part

            @pl.when(kk == nk - 1)
            def _():
                finish(acc_ref[...])

    b_spec = pl.BlockSpec((tn, tk), lambda i, j, kk: (j, kk)) if trans_b else pl.BlockSpec((tk, tn), lambda i, j, kk: (kk, j))
    a_spec = pl.BlockSpec((tk, tm), lambda i, j, kk: (kk, i)) if trans_a else pl.BlockSpec((tm, tk), lambda i, j, kk: (i, kk))
    in_specs = [a_spec, b_spec]
    operands = [a, b]
    if add is not None:
        in_specs.append(pl.BlockSpec((tm, tn), lambda i, j, kk: (i, j)))
        operands.append(add)
    return pl.pallas_call(
        body, name=name, grid=(m // tm, n // tn, nk), in_specs=in_specs,
        out_specs=pl.BlockSpec((tm, tn), lambda i, j, kk: (i, j)),
        out_shape=jax.ShapeDtypeStruct((m, n), out_dtype),
        scratch_shapes=[pltpu.VMEM((tm, tn), F32)] if nk > 1 else [],
        compiler_params=_cparams("parallel", "parallel", "arbitrary"),
    )(*operands)


def _matmul_post_norm(a, b, xin, g, beta, name):
    t, k = a.shape
    d = b.shape[1]
    tm = 512

    def body(a_ref, b_ref, x_ref, g_ref, beta_ref, pre_ref, y_ref, yb_ref):
        pre = ALPHA * x_ref[...] + _dot(a_ref[...], b_ref[...])
        xhat, _ = _ln_stats(pre)
        y = xhat * g_ref[...] + beta_ref[...]
        pre_ref[...] = pre
        y_ref[...] = y
        yb_ref[...] = y.astype(yb_ref.dtype)

    row = pl.BlockSpec((tm, d), lambda i: (i, 0))
    vec = pl.BlockSpec((1, d), lambda i: (0, 0))
    return pl.pallas_call(
        body, name=name, grid=(t // tm,),
        in_specs=[pl.BlockSpec((tm, k), lambda i: (i, 0)), pl.BlockSpec((k, d), lambda i: (0, 0)), row, vec, vec],
        out_specs=[row, row, row],
        out_shape=[jax.ShapeDtypeStruct((t, d), F32)] * 2 + [jax.ShapeDtypeStruct((t, d), MXU_DTYPE)],
        compiler_params=_cparams("parallel"),
    )(a, b, xin, g.reshape(1, d), beta.reshape(1, d))


def _ffn_in(x1, w_gu, name, carry=None):
    t, d = x1.shape
    tm, tn = 512, D_FF // 2
    nj = D_FF // tn

    def body(x_ref, wg_ref, wu_ref, gate_ref, up_ref, h_ref):
        xb = x_ref[...].astype(MXU_DTYPE)
        gate = _dot(xb, wg_ref[...])
        up = _dot(xb, wu_ref[...])
        gate_ref[...] = gate
        up_ref[...] = up
        h_ref[...] = (gate * (1.0 / (1.0 + jnp.exp(-gate))) * up).astype(h_ref.dtype)

    out = pl.BlockSpec((tm, tn), lambda i, j: (i, j))
    return _host_call(
        body, carry, name=name, grid=(t // tm, nj),
        in_specs=[pl.BlockSpec((tm, d), lambda i, j: (i, 0)), pl.BlockSpec((d, tn), lambda i, j: (0, j)),
                  pl.BlockSpec((d, tn), lambda i, j: (0, nj + j))],
        out_specs=[out, out, out],
        out_shape=[jax.ShapeDtypeStruct((t, D_FF), F32)] * 2 + [jax.ShapeDtypeStruct((t, D_FF), MXU_DTYPE)],
        operands=(x1, w_gu, w_gu))


def _ffn_out_dx(dy, w_down, gate, up, name, carry=None):
    t, d = dy.shape
    tm, tn = 512, D_FF // 2

    def body(dy_ref, w_ref, gate_ref, up_ref, dg_ref, du_ref):
        dh = _dot_nt(dy_ref[...], w_ref[...])
        gate = gate_ref[...]
        sig = 1.0 / (1.0 + jnp.exp(-gate))
        dg_ref[...] = (dh * up_ref[...] * sig * (1.0 + gate * (1.0 - sig))).astype(dg_ref.dtype)
        du_ref[...] = (dh * gate * sig).astype(du_ref.dtype)

    tile = pl.BlockSpec((tm, tn), lambda i, j: (i, j))
    return _host_call(
        body, carry, name=name, grid=(t // tm, D_FF // tn),
        in_specs=[pl.BlockSpec((tm, d), lambda i, j: (i, 0)), pl.BlockSpec((tn, d), lambda i, j: (j, 0)), tile, tile],
        out_specs=[tile, tile], out_shape=[jax.ShapeDtypeStruct((t, D_FF), MXU_DTYPE)] * 2,
        operands=(dy, w_down, gate, up))


def _ffn_in_dx(dgate, dup, w_gu, add, name):
    t = dgate.shape[0]
    d = w_gu.shape[0]
    tm, tk = 1024, D_FF // 2
    nk = D_FF // tk

    def body(dg_ref, du_ref, wg_ref, wu_ref, add_ref, o_ref, acc_ref):
        kk = pl.program_id(1)
        part = _dot_nt(dg_ref[...], wg_ref[...]) + _dot_nt(du_ref[...], wu_ref[...])

        @pl.when(kk == 0)
        def _():
            acc_ref[...] = part

        @pl.when(kk > 0)
        def _():
            acc_ref[...] += part

        @pl.when(kk == nk - 1)
        def _():
            o_ref[...] = acc_ref[...] + ALPHA * add_ref[...]

    act = pl.BlockSpec((tm, tk), lambda i, kk: (i, kk))
    row = pl.BlockSpec((tm, d), lambda i, kk: (i, 0))
    return pl.pallas_call(
        body, name=name, grid=(t // tm, nk),
        in_specs=[act, act, pl.BlockSpec((d, tk), lambda i, kk: (0, kk)), pl.BlockSpec((d, tk), lambda i, kk: (0, nk + kk)), row],
        out_specs=row, out_shape=jax.ShapeDtypeStruct((t, d), F32), scratch_shapes=[pltpu.VMEM((tm, d), F32)],
        compiler_params=_cparams("parallel", "arbitrary"),
    )(dgate, dup, w_gu, w_gu, add)


def _ln_stats(pre):
    mu = jnp.mean(pre, axis=-1, keepdims=True)
    xc = pre - mu
    var = jnp.mean(xc * xc, axis=-1, keepdims=True)
    rstd = lax.rsqrt(var + LN_EPS)
    return xc * rstd, rstd


def _ln_bwd(dy, pre, g, name):
    t, d = dy.shape
    tile = 256

    def body(dy_ref, pre_ref, g_ref, dpre_ref, dgb_ref):
        dyv = dy_ref[...]
        xhat, rstd = _ln_stats(pre_ref[...])
        dxh = dyv * g_ref[...]
        m1 = jnp.mean(dxh, axis=-1, keepdims=True)
        m2 = jnp.mean(dxh * xhat, axis=-1, keepdims=True)
        dpre_ref[...] = rstd * (dxh - m1 - xhat * m2)

        @pl.when(pl.program_id(0) == 0)
        def _():
            dgb_ref[...] = jnp.zeros_like(dgb_ref)

        dgb_ref[0:1, :] += jnp.sum(dyv * xhat, axis=0, keepdims=True)
        dgb_ref[1:2, :] += jnp.sum(dyv, axis=0, keepdims=True)

    row = pl.BlockSpec((tile, d), lambda i: (i, 0))
    return pl.pallas_call(
        body, name=name, grid=(t // tile,), in_specs=[row, row, pl.BlockSpec((1, d), lambda i: (0, 0))],
        out_specs=[row, pl.BlockSpec((8, d), lambda i: (0, 0))],
        out_shape=[jax.ShapeDtypeStruct((t, d), F32), jax.ShapeDtypeStruct((8, d), F32)],
        compiler_params=_cparams("arbitrary"),
    )(dy, pre, g.reshape(1, d))


def _loss_kernel(y, target, name):
    t, d = y.shape
    tile = 512

    def body(y_ref, t_ref, dy_ref, l_ref):
        err = y_ref[...] - t_ref[...]
        dy_ref[...] = err * (1.0 / d)

        @pl.when(pl.program_id(0) == 0)
        def _():
            l_ref[...] = jnp.zeros_like(l_ref)

        l_ref[...] += jnp.sum(err * err) * (0.5 / d)

    row = pl.BlockSpec((tile, d), lambda i: (i, 0))
    return pl.pallas_call(
        body, name=name, grid=(t // tile,), in_specs=[row, row],
        out_specs=[row, pl.BlockSpec((8, 128), lambda i: (0, 0))],
        out_shape=[jax.ShapeDtypeStruct((t, d), F32), jax.ShapeDtypeStruct((8, 128), F32)],
        compiler_params=_cparams("arbitrary"),
    )(y, target)


def _adamw(w, g, m, v, name):
    nl, r, c = w.shape
    tr = r
    for cand in (256, 352, 128, 64, 16, 8):
        if r % cand == 0:
            tr = cand
            break

    def body(w_ref, g_ref, m_ref, v_ref, d_ref, nm_ref, nv_ref):
        gv = g_ref[...]
        nm = ADAM_B1 * m_ref[...] + (1.0 - ADAM_B1) * gv
        nv = ADAM_B2 * v_ref[...] + (1.0 - ADAM_B2) * (gv * gv)
        m_hat = nm / (1.0 - ADAM_B1 ** ADAM_STEP)
        v_hat = nv / (1.0 - ADAM_B2 ** ADAM_STEP)
        d_ref[...] = -ADAM_LR * (m_hat / (jnp.sqrt(v_hat) + ADAM_EPS) + ADAM_WD * w_ref[...])
        nm_ref[...] = nm
        nv_ref[...] = nv

    blk = pl.BlockSpec((1, tr, c), lambda l, i: (l, i, 0))
    return pl.pallas_call(
        body, name=name, grid=(nl, r // tr), in_specs=[blk] * 4, out_specs=[blk] * 3,
        out_shape=[jax.ShapeDtypeStruct(w.shape, F32)] * 3, compiler_params=_cparams("parallel", "parallel"),
    )(w, g, m, v)


def _shift_down(u, k, rows):
    return jnp.where(rows >= k, pltpu.roll(u, k, 0), 0.0)


def _shift_up(u, k, rows, s):
    return jnp.where(rows < s - k, pltpu.roll(u, s - k, 0), 0.0)


def _conv_fwd(proj, conv_w, nb, s, name):
    def body(b_ref, c_ref, h_ref, w_ref, o_ref):
        rows = _iota2((s, CONV_W), 0)
        u = c_ref[...] * h_ref[...]
        y = w_ref[2:3, :] * u + w_ref[1:2, :] * _shift_down(u, 1, rows) + w_ref[0:1, :] * _shift_down(u, 2, rows)
        o_ref[...] = b_ref[...] * y

    col = lambda j: pl.BlockSpec((s, CONV_W), lambda b: (b, j))
    return pl.pallas_call(
        body, name=name, grid=(nb,),
        in_specs=[col(9), col(10), col(11), pl.BlockSpec((8, CONV_W), lambda b: (0, 0))],
        out_specs=pl.BlockSpec((s, CONV_W), lambda b: (b, 0)),
        out_shape=jax.ShapeDtypeStruct((nb * s, CONV_W), F32), compiler_params=_cparams("parallel"),
    )(proj, proj, proj, conv_w)


def _conv_bwd(dmixed, proj, conv_w, nb, s, name):
    def body(do_ref, b_ref, c_ref, h_ref, w_ref, dg_ref, dw_ref):
        rows = _iota2((s, CONV_W), 0)
        cg, hg, bg, dout = c_ref[...], h_ref[...], b_ref[...], do_ref[...]
        u = cg * hg
        u1 = _shift_down(u, 1, rows)
        u2 = _shift_down(u, 2, rows)
        y = w_ref[2:3, :] * u + w_ref[1:2, :] * u1 + w_ref[0:1, :] * u2
        dy = dout * bg
        du = w_ref[2:3, :] * dy + w_ref[1:2, :] * _shift_up(dy, 1, rows, s) + w_ref[0:1, :] * _shift_up(dy, 2, rows, s)
        dg_ref[:, 0:CONV_W] = dout * y
        dg_ref[:, CONV_W:2 * CONV_W] = du * hg
        dg_ref[:, 2 * CONV_W:3 * CONV_W] = du * cg

        @pl.when(pl.program_id(0) == 0)
        def _():
            dw_ref[...] = jnp.zeros_like(dw_ref)

        dw_ref[0:1, :] += jnp.sum(dy * u2, axis=0, keepdims=True)
        dw_ref[1:2, :] += jnp.sum(dy * u1, axis=0, keepdims=True)
        dw_ref[2:3, :] += jnp.sum(dy * u, axis=0, keepdims=True)

    col = lambda j: pl.BlockSpec((s, CONV_W), lambda b: (b, j))
    return pl.pallas_call(
        body, name=name, grid=(nb,),
        in_specs=[col(3), col(9), col(10), col(11), pl.BlockSpec((8, CONV_W), lambda b: (0, 0))],
        out_specs=[pl.BlockSpec((s, 3 * CONV_W), lambda b: (b, 0)), pl.BlockSpec((8, CONV_W), lambda b: (0, 0))],
        out_shape=[jax.ShapeDtypeStruct((nb * s, 3 * CONV_W), F32), jax.ShapeDtypeStruct((8, CONV_W), F32)],
        compiler_params=_cparams("arbitrary"),
    )(dmixed, proj, proj, proj, conv_w)


def _col_spec(s, base):
    return pl.BlockSpec((s, BLK), lambda b, p: (b, base + p))


def _qrows(i):
    return pl.ds(pl.multiple_of(i * QT, QT), QT)


def _rows(j):
    return pl.ds(pl.multiple_of(j * ATT, ATT), ATT)


def _keys_upto(i):
    return (i + 1) * (QT // ATT)


def _triangle(keep):
    return keep(_iota2((ATT, ATT), 0), _iota2((ATT, ATT), 1)).astype(MXU_DTYPE)


def _rows128(i):
    return pl.ds(pl.multiple_of(i * BLK, BLK), BLK)


def _log_sigmoid_parts(z):
    e = jnp.exp(-jnp.abs(z))
    l1p = jnp.log(1.0 + e)
    lb = jnp.minimum(z, 0.0) - l1p
    return lb, lb - z, e


def _head_masks():
    lane = _iota2((1, BLK), 1)
    return [(lane >= h * HEAD_DIM) & (lane < (h + 1) * HEAD_DIM) for h in range(2)]


def _split_heads(ref, scr, sels):
    for h, sel in enumerate(sels):
        scr[h] = jnp.where(sel, ref[...], 0.0).astype(MXU_DTYPE)


def _sb_fwd(proj, nb, s, name, carry=None):
    nblk = s // ATT

    def body(q_ref, k_ref, v_ref, o_ref, km, vm):
        sels = _head_masks()
        _split_heads(k_ref, km, sels)
        _split_heads(v_ref, vm, sels)
        rows = _iota2((QT, ATT), 0)
        cols = _iota2((QT, ATT), 1)
        later = _triangle(lambda r, c: r > c)

        def qblock(i, _):
            qi = (q_ref[_qrows(i), :] * 0.125).astype(MXU_DTYPE)

            def kblock(t, state):
                carries, acc = state
                j = _keys_upto(i) - 1 - t
                strict = (cols + (j * ATT - i * QT)) < rows
                out = []
                for h in range(2):
                    z = _dot_nt(qi, km[h, _rows(j), :])
                    lb, lr, _ = _log_sigmoid_parts(z)
                    lr = jnp.where(strict, lr, 0.0)
                    tail = _split_dot(lr, later, 2) + carries[h]
                    a = jnp.where(strict, jnp.exp(lb + tail), 0.0)
                    acc = acc + _dot(a, vm[h, _rows(j), :])
                    out.append(carries[h] + jnp.sum(lr, axis=-1, keepdims=True))
                return tuple(out), acc

            init = ((jnp.zeros((QT, 1), F32),) * 2, jnp.zeros((QT, BLK), F32))
            _, acc = lax.fori_loop(0, _keys_upto(i), kblock, init)
            o_ref[_qrows(i), :] = acc
            return 0

        lax.fori_loop(0, s // QT, qblock, 0)

    (o,), extra = _host_call(
        body, carry, name=name, grid=(nb, 2), in_specs=[_col_spec(s, 0), _col_spec(s, 2), _col_spec(s, 4)],
        out_specs=[_col_spec(s, 0)], out_shape=[jax.ShapeDtypeStruct((nb * s, 2 * BLK), F32)],
        scratch_shapes=[pltpu.VMEM((2, s, BLK), MXU_DTYPE)] * 2, operands=(proj, proj, proj))
    return o, extra


def _sb_bwd(proj, dmixed, nb, s, name, carry=None):
    nblk = s // ATT

    def body(q_ref, k_ref, v_ref, do_ref, dq_ref, dk_ref, dv_ref, km, vm, a_scr, dl_scr, beta_scr):
        sels = _head_masks()
        _split_heads(k_ref, km, sels)
        _split_heads(v_ref, vm, sels)
        rows = _iota2((QT, ATT), 0)
        cols = _iota2((QT, ATT), 1)
        later = _triangle(lambda r, c: r > c)
        earlier = _triangle(lambda r, c: r < c)
        dk_ref[...] = jnp.zeros_like(dk_ref)
        dv_ref[...] = jnp.zeros_like(dv_ref)

        def qblock(i, _):
            qi = (q_ref[_qrows(i), :] * 0.125).astype(MXU_DTYPE)
            doi = do_ref[_qrows(i), :].astype(MXU_DTYPE)
            qm = [jnp.where(sel, qi, 0.0) for sel in sels]
            dom = [jnp.where(sel, doi, 0.0) for sel in sels]

            def first(t, carries):
                j = _keys_upto(i) - 1 - t
                strict = (cols + (j * ATT - i * QT)) < rows
                out = []
                for h in range(2):
                    z = _dot_nt(qi, km[h, _rows(j), :])
                    lb, lr, e = _log_sigmoid_parts(z)
                    lr = jnp.where(strict, lr, 0.0)
                    tail = _split_dot(lr, later, 2) + carries[h]
                    a = jnp.where(strict, jnp.exp(lb + tail), 0.0)
                    a_scr[h, j] = a
                    dl_scr[h, j] = a * _dot_nt(doi, vm[h, _rows(j), :])
                    beta_scr[h, j] = jnp.exp(lb)
                    out.append(carries[h] + jnp.sum(lr, axis=-1, keepdims=True))
                return tuple(out)

            lax.fori_loop(0, _keys_upto(i), first, (jnp.zeros((QT, 1), F32),) * 2)

            def second(j, state):
                csums, dq = state
                strict = (cols + (j * ATT - i * QT)) < rows
                out = []
                for h in range(2):
                    dl = dl_scr[h, j]
                    beta = beta_scr[h, j]
                    before = _split_dot(dl, earlier, 2) + csums[h]
                    dz = jnp.where(strict, dl * (1.0 - beta) - beta * before, 0.0).astype(MXU_DTYPE)
                    dq = dq + _dot(dz, km[h, _rows(j), :])
                    dk_ref[_rows(j), :] += _dot_tn(dz, qm[h])
                    dv_ref[_rows(j), :] += _dot_tn(a_scr[h, j], dom[h])
                    out.append(csums[h] + jnp.sum(dl, axis=-1, keepdims=True))
                return tuple(out), dq

            init = ((jnp.zeros((QT, 1), F32),) * 2, jnp.zeros((QT, BLK), F32))
            _, dq = lax.fori_loop(0, _keys_upto(i), second, init)
            dq_ref[_qrows(i), :] = dq * 0.125
            return 0

        lax.fori_loop(0, s // QT, qblock, 0)

    out = _col_spec(s, 0)
    return _host_call(
        body, carry, name=name, grid=(nb, 2),
        in_specs=[_col_spec(s, 0), _col_spec(s, 2), _col_spec(s, 4), out], out_specs=[out] * 3,
        out_shape=[jax.ShapeDtypeStruct((nb * s, 2 * BLK), F32)] * 3,
        scratch_shapes=[pltpu.VMEM((2, s, BLK), MXU_DTYPE)] * 2 + [pltpu.VMEM((2, nblk, QT, ATT), F32)] * 3,
        operands=(proj, proj, proj, dmixed))


def _pair_spec(s, width):
    return pl.BlockSpec((None, 2, s, width), lambda b, p: (b, p, 0, 0))


def _fox_fwd(proj, ccol, crow, nb, s, name, carry=None):
    nblk = s // ATT

    def body(q_ref, k_ref, v_ref, cc_ref, cr_ref, o_ref, lse_ref, km, vm):
        sels = _head_masks()
        _split_heads(k_ref, km, sels)
        _split_heads(v_ref, vm, sels)
        rows = _iota2((QT, ATT), 0)
        cols = _iota2((QT, ATT), 1)

        def qblock(i, _):
            qi = (q_ref[_qrows(i), :] * 0.125).astype(MXU_DTYPE)
            ci = [cc_ref[h, _qrows(i), :] for h in range(2)]

            def kblock(j, state):
                ms, ls, acc = state
                causal = (cols + (j * ATT - i * QT)) <= rows
                new_m, new_l, scales, parts = [], [], [], []
                for h in range(2):
                    z = _dot_nt(qi, km[h, _rows(j), :]) + (ci[h] - cr_ref[h, j][0:1, :])
                    z = jnp.where(causal, z, NEG)
                    m_new = jnp.maximum(ms[h], jnp.max(z, axis=-1, keepdims=True))
                    p = jnp.exp(z - m_new)
                    scale = jnp.exp(ms[h] - m_new)
                    new_m.append(m_new)
                    new_l.append(scale * ls[h] + jnp.sum(p, axis=-1, keepdims=True))
                    scales.append(scale)
                    parts.append(_dot(p, vm[h, _rows(j), :]))
                acc = jnp.where(sels[0], scales[0], scales[1]) * acc + parts[0] + parts[1]
                return tuple(new_m), tuple(new_l), acc

            init = ((jnp.full((QT, 1), NEG, F32),) * 2, (jnp.zeros((QT, 1), F32),) * 2, jnp.zeros((QT, BLK), F32))
            ms, ls, acc = lax.fori_loop(0, _keys_upto(i), kblock, init)
            o_ref[_qrows(i), :] = acc / jnp.where(sels[0], ls[0], ls[1])
            for h in range(2):
                lse_ref[h, _qrows(i), :] = jnp.broadcast_to(ms[h] + jnp.log(ls[h]), (QT, ATT))
            return 0

        lax.fori_loop(0, s // QT, qblock, 0)

    crow_spec = pl.BlockSpec((None, 2, nblk, 8, ATT), lambda b, p: (b, p, 0, 0, 0))
    return _host_call(
        body, carry, name=name, grid=(nb, 2),
        in_specs=[_col_spec(s, 12), _col_spec(s, 14), _col_spec(s, 16), _pair_spec(s, ATT), crow_spec],
        out_specs=[_col_spec(s, 0), _pair_spec(s, ATT)],
        out_shape=[jax.ShapeDtypeStruct((nb * s, 2 * BLK), F32), jax.ShapeDtypeStruct((nb, N_HEADS, s, ATT), F32)],
        scratch_shapes=[pltpu.VMEM((2, s, BLK), MXU_DTYPE)] * 2, operands=(proj, proj, proj, ccol, crow))


def _fox_bwd(proj, dmixed, lse, ccol, crow, nb, s, name, carry=None):
    nblk = s // ATT

    def body(q_ref, k_ref, v_ref, do_ref, lse_ref, cc_ref, cr_ref, dq_ref, dk_ref, dv_ref, dc_ref, km, vm, p_scr, dp_scr):
        sels = _head_masks()
        _split_heads(k_ref, km, sels)
        _split_heads(v_ref, vm, sels)
        rows = _iota2((QT, ATT), 0)
        cols = _iota2((QT, ATT), 1)
        dk_ref[...] = jnp.zeros_like(dk_ref)
        dv_ref[...] = jnp.zeros_like(dv_ref)
        dc_ref[...] = jnp.zeros_like(dc_ref)

        def qblock(i, _):
            qi = (q_ref[_qrows(i), :] * 0.125).astype(MXU_DTYPE)
            doi = do_ref[_qrows(i), :].astype(MXU_DTYPE)
            qm = [jnp.where(sel, qi, 0.0) for sel in sels]
            dom = [jnp.where(sel, doi, 0.0) for sel in sels]
            ci = [cc_ref[h, _qrows(i), :] for h in range(2)]
            lsei = [lse_ref[h, _qrows(i), :] for h in range(2)]

            def probs(j, h):
                z = _dot_nt(qi, km[h, _rows(j), :]) + (ci[h] - cr_ref[h, j][0:1, :])
                p = jnp.where((cols + (j * ATT - i * QT)) <= rows, jnp.exp(z - lsei[h]), 0.0)
                return p, _dot_nt(doi, vm[h, _rows(j), :])

            def row_term(j, accs):
                out = []
                for h in range(2):
                    p, dp = probs(j, h)
                    p_scr[h, j] = p
                    dp_scr[h, j] = dp
                    out.append(accs[h] + jnp.sum(p * dp, axis=-1, keepdims=True))
                return tuple(out)

            di = lax.fori_loop(0, _keys_upto(i), row_term, (jnp.zeros((QT, 1), F32),) * 2)

            def kblock(j, dq):
                for h in range(2):
                    p = p_scr[h, j]
                    ds = p * (dp_scr[h, j] - di[h])
                    dc_ref[h, j] += jnp.broadcast_to(jnp.sum(ds, axis=0, keepdims=True), (8, ATT))
                    ds = ds.astype(MXU_DTYPE)
                    dk_ref[_rows(j), :] += _dot_tn(ds, qm[h])
                    dv_ref[_rows(j), :] += _dot_tn(p, dom[h])
                    dq = dq + _dot(ds, km[h, _rows(j), :])
                return dq

            dq = lax.fori_loop(0, _keys_upto(i), kblock, jnp.zeros((QT, BLK), F32))
            dq_ref[_qrows(i), :] = dq * 0.125
            return 0

        lax.fori_loop(0, s // QT, qblock, 0)

    crow_spec = pl.BlockSpec((None, 2, nblk, 8, ATT), lambda b, p: (b, p, 0, 0, 0))
    wide, cols_out = _pair_spec(s, ATT), _col_spec(s, 0)
    return _host_call(
        body, carry, name=name, grid=(nb, 2),
        in_specs=[_col_spec(s, 12), _col_spec(s, 14), _col_spec(s, 16), _col_spec(s, 4), wide, wide, crow_spec],
        out_specs=[cols_out, cols_out, cols_out, crow_spec],
        out_shape=[jax.ShapeDtypeStruct((nb * s, 2 * BLK), F32)] * 3 + [jax.ShapeDtypeStruct((nb, N_HEADS, nblk, 8, ATT), F32)],
        scratch_shapes=[pltpu.VMEM((2, s, BLK), MXU_DTYPE)] * 2 + [pltpu.VMEM((2, nblk, QT, ATT), F32)] * 2,
        operands=(proj, proj, proj, dmixed, lse, ccol, crow))


def _fox_gates_fwd(proj, f_bias, nb, s, name):
    chunk = 256

    def body(f_ref, b_ref, c_ref):
        lower = (_iota2((chunk, chunk), 0) >= _iota2((chunk, chunk), 1)).astype(MXU_DTYPE)
        carry = jnp.zeros((1, BLK), F32)
        for n in range(s // chunk):
            rows = pl.ds(n * chunk, chunk)
            lf, _, _ = _log_sigmoid_parts(f_ref[rows, :] + b_ref[0:1, :])
            c = _split_dot_lhs(lower, lf, 3) + carry
            c_ref[rows, :] = c
            carry = c[chunk - 1:chunk, :]

    return pl.pallas_call(
        body, name=name, grid=(nb,),
        in_specs=[pl.BlockSpec((s, BLK), lambda b: (b, (PROJ_PAD - BLK) // BLK)), pl.BlockSpec((8, BLK), lambda b: (0, 0))],
        out_specs=pl.BlockSpec((s, BLK), lambda b: (b, 0)),
        out_shape=jax.ShapeDtypeStruct((nb * s, BLK), F32), compiler_params=_cparams("parallel"),
    )(proj, f_bias)


def _fox_gates_bwd(dc, proj, f_bias, nb, s, name):
    chunk = 256

    def body(dc_ref, f_ref, b_ref, df_ref, db_ref):
        upper = (_iota2((chunk, chunk), 0) <= _iota2((chunk, chunk), 1)).astype(MXU_DTYPE)
        carry = jnp.zeros((1, BLK), F32)
        total = jnp.zeros((1, BLK), F32)
        for n in reversed(range(s // chunk)):
            rows = pl.ds(n * chunk, chunk)
            dlf = _split_dot_lhs(upper, dc_ref[rows, :], 3) + carry
            carry = dlf[0:1, :]
            pre = f_ref[rows, :] + b_ref[0:1, :]
            e = jnp.exp(-jnp.abs(pre))
            df = dlf * (jnp.where(pre >= 0.0, e, 1.0) / (1.0 + e))
            df_ref[rows, :] = df
            total = total + jnp.sum(df, axis=0, keepdims=True)

        @pl.when(pl.program_id(0) == 0)
        def _():
            db_ref[...] = jnp.zeros_like(db_ref)

        db_ref[0:1, :] += total

    return pl.pallas_call(
        body, name=name, grid=(nb,),
        in_specs=[pl.BlockSpec((s, BLK), lambda b: (b, 0)), pl.BlockSpec((s, BLK), lambda b: (b, (PROJ_PAD - BLK) // BLK)),
                  pl.BlockSpec((8, BLK), lambda b: (0, 0))],
        out_specs=[pl.BlockSpec((s, BLK), lambda b: (b, 0)), pl.BlockSpec((8, BLK), lambda b: (0, 0))],
        out_shape=[jax.ShapeDtypeStruct((nb * s, BLK), F32), jax.ShapeDtypeStruct((8, BLK), F32)],
        compiler_params=_cparams("arbitrary"),
    )(dc, proj, f_bias)


def _delta_kernel(dmixed, o, nb, s, name):
    def body(do_ref, o_ref, d_ref):
        prod = do_ref[...] * o_ref[...]
        for h, sel in enumerate(_head_masks()):
            d_ref[h] = jnp.broadcast_to(jnp.sum(jnp.where(sel, prod, 0.0), axis=-1, keepdims=True), (s, BLK))

    return pl.pallas_call(
        body, name=name, grid=(nb, 2), in_specs=[_col_spec(s, 2), _col_spec(s, 0)], out_specs=_pair_spec(s, BLK),
        out_shape=jax.ShapeDtypeStruct((nb, N_HEADS, s, BLK), F32), compiler_params=_cparams("parallel", "parallel"),
    )(dmixed, o)


def _t5_bucket_np(dist):
    max_exact = REL_BUCKETS // 2
    nf = np.maximum(dist, 1).astype(np.float32)
    large = max_exact + (np.log(nf / max_exact) / math.log(2048 / max_exact) * (REL_BUCKETS - max_exact)).astype(np.int32)
    large = np.minimum(large, REL_BUCKETS - 1)
    return np.where(dist < max_exact, dist, large)


def _bucket_table():
    qi = np.arange(BLK)[:, None]
    kj = np.arange(2 * BLK)[None, :]
    dist = qi + BLK - kj
    tables = []
    for window, dil in DIL_PATTERNS:
        in_band = (dist >= 0) & (dist <= window // dil)
        tables.append(np.where(in_band, _t5_bucket_np(np.maximum(dist, 0) * dil), -1).astype(np.int32))
    return np.stack(tables)


def _dil_scores(qb, kp, kc, b_ref, h, prev_valid):
    zp = _dot_nt(qb, kp) + b_ref[h, :, 0:BLK]
    zp = jnp.where(prev_valid, zp, NEG)
    zc = _dot_nt(qb, kc) + b_ref[h, :, BLK:2 * BLK]
    return zp, zc


def _residue_rows(b, seg, dil):
    if dil == 1:
        return _rows128(b), _rows128(jnp.maximum(b - 1, 0)), b > 0
    r, n = b // seg, b % seg
    cur = pl.ds(r + dil * n * BLK, BLK, stride=dil)
    prev = pl.ds(r + dil * jnp.maximum(n - 1, 0) * BLK, BLK, stride=dil)
    return cur, prev, n > 0


def _dil_attention_fwd(proj, bias, nb, s, name, carry=None):
    nblk = s // BLK

    def body(q_ref, k_ref, v_ref, b_ref, out_ref, lse_ref, o_scr, l_scr):
        sels = _head_masks()
        for p, (_, dil) in enumerate(DIL_PATTERNS):
            seg = s // dil // BLK

            def block(b, _, p=p, seg=seg, dil=dil):
                cur, prev, has_prev = _residue_rows(b, seg, dil)
                qb = (q_ref[cur, :] * 0.125).astype(MXU_DTYPE)
                kp, kc = k_ref[prev, :].astype(MXU_DTYPE), k_ref[cur, :].astype(MXU_DTYPE)
                vp, vc = v_ref[prev, :].astype(MXU_DTYPE), v_ref[cur, :].astype(MXU_DTYPE)
                acc = jnp.zeros((BLK, BLK), F32)
                for h, sel in enumerate(sels):
                    zp, zc = _dil_scores(qb, jnp.where(sel, kp, 0.0), jnp.where(sel, kc, 0.0), b_ref.at[p], h, has_prev)
                    m = jnp.maximum(jnp.max(zp, axis=-1, keepdims=True), jnp.max(zc, axis=-1, keepdims=True))
                    pp = jnp.exp(zp - m)
                    pc = jnp.exp(zc - m)
                    den = jnp.sum(pp, axis=-1, keepdims=True) + jnp.sum(pc, axis=-1, keepdims=True)
                    acc = acc + (_dot(pp, jnp.where(sel, vp, 0.0)) + _dot(pc, jnp.where(sel, vc, 0.0))) / den
                    l_scr[p, h, cur, :] = jnp.broadcast_to(m + jnp.log(den), (BLK, BLK))
                o_scr[p, cur, :] = acc
                return 0

            lax.fori_loop(0, nblk, block, 0, unroll=4)

        weights, dens = [], []
        for h in range(2):
            m = jnp.maximum(jnp.maximum(l_scr[0, h], l_scr[1, h]), l_scr[2, h])
            w = [jnp.exp(l_scr[p, h] - m) for p in range(3)]
            den = w[0] + w[1] + w[2]
            lse_ref[h] = m + jnp.log(den)
            weights.append(w)
            dens.append(den)
        num = sum(jnp.where(sels[0], weights[0][p], weights[1][p]) * o_scr[p] for p in range(3))
        out_ref[...] = num / jnp.where(sels[0], dens[0], dens[1])

    bias_spec = pl.BlockSpec((3, 2, BLK, 2 * BLK), lambda b, p: (0, p, 0, 0))
    return _host_call(
        body, carry, name=name, grid=(nb, 2), in_specs=[_col_spec(s, 6), _col_spec(s, 8), _col_spec(s, 10), bias_spec],
        out_specs=[_col_spec(s, 0), _pair_spec(s, BLK)],
        out_shape=[jax.ShapeDtypeStruct((nb * s, 2 * BLK), F32), jax.ShapeDtypeStruct((nb, N_HEADS, s, BLK), F32)],
        scratch_shapes=[pltpu.VMEM((3, s, BLK), F32), pltpu.VMEM((3, 2, s, BLK), F32)], operands=(proj, proj, proj, bias))


def _dil_attention_bwd(proj, dmixed, lse, delta, bias, nb, s, name, carry=None):
    nblk = s // BLK

    def body(q_ref, k_ref, v_ref, do_ref, lse_ref, dl_ref, b_ref, dq_ref, dk_ref, dv_ref, g_ref):
        sels = _head_masks()
        dq_ref[...] = jnp.zeros_like(dq_ref)
        dk_ref[...] = jnp.zeros_like(dk_ref)
        dv_ref[...] = jnp.zeros_like(dv_ref)
        g_ref[...] = jnp.zeros_like(g_ref)
        for p, (_, dil) in enumerate(DIL_PATTERNS):
            seg = s // dil // BLK

            def block(b, _, p=p, seg=seg, dil=dil):
                cur, prev, has_prev = _residue_rows(b, seg, dil)
                qb = (q_ref[cur, :] * 0.125).astype(MXU_DTYPE)
                dob = do_ref[cur, :].astype(MXU_DTYPE)
                kp, kc = k_ref[prev, :].astype(MXU_DTYPE), k_ref[cur, :].astype(MXU_DTYPE)
                vp, vc = v_ref[prev, :].astype(MXU_DTYPE), v_ref[cur, :].astype(MXU_DTYPE)
                dq = jnp.zeros((BLK, BLK), F32)
                dkp, dkc, dvp, dvc = dq, dq, dq, dq
                for h, sel in enumerate(sels):
                    kph, kch = jnp.where(sel, kp, 0.0), jnp.where(sel, kc, 0.0)
                    qh, doh = jnp.where(sel, qb, 0.0), jnp.where(sel, dob, 0.0)
                    lse_h = lse_ref[h, cur, :]
                    dlt = dl_ref[h, cur, :]
                    zp, zc = _dil_scores(qb, kph, kch, b_ref.at[p], h, has_prev)
                    pp = jnp.exp(zp - lse_h)
                    pc = jnp.exp(zc - lse_h)
                    dsp = pp * (_dot_nt(dob, jnp.where(sel, vp, 0.0)) - dlt)
                    dsc = pc * (_dot_nt(dob, jnp.where(sel, vc, 0.0)) - dlt)
                    g_ref[h, p, :, 0:BLK] += dsp
                    g_ref[h, p, :, BLK:2 * BLK] += dsc
                    dsp = dsp.astype(MXU_DTYPE)
                    dsc = dsc.astype(MXU_DTYPE)
                    dq = dq + _dot(dsp, kph) + _dot(dsc, kch)
                    dkp, dkc = dkp + _dot_tn(dsp, qh), dkc + _dot_tn(dsc, qh)
                    dvp, dvc = dvp + _dot_tn(pp, doh), dvc + _dot_tn(pc, doh)
                dq_ref[cur, :] += dq * 0.125
                dk_ref[prev, :] += dkp
                dk_ref[cur, :] += dkc
                dv_ref[prev, :] += dvp
                dv_ref[cur, :] += dvc
                return 0

            lax.fori_loop(0, nblk, block, 0, unroll=4)

    bias_spec = pl.BlockSpec((3, 2, BLK, 2 * BLK), lambda b, p: (0, p, 0, 0))
    cols, stats = _col_spec(s, 0), _pair_spec(s, BLK)
    return _host_call(
        body, carry, name=name, grid=(nb, 2),
        in_specs=[_col_spec(s, 6), _col_spec(s, 8), _col_spec(s, 10), _col_spec(s, 2), stats, stats, bias_spec],
        out_specs=[cols, cols, cols, pl.BlockSpec((None, 2, 3, BLK, 2 * BLK), lambda b, p: (b, p, 0, 0, 0))],
        out_shape=[jax.ShapeDtypeStruct((nb * s, 2 * BLK), F32)] * 3 + [jax.ShapeDtypeStruct((nb, N_HEADS, 3, BLK, 2 * BLK), F32)],
        operands=(proj, proj, proj, dmixed, lse, delta, bias))


def _bucket_reduce(gbias, table, name):
    nb = gbias.shape[0]

    def body(g_ref, t_ref, o_ref):
        row = _iota2((8, BLK), 0)
        lane = _iota2((8, BLK), 1)
        gsum = [[sum(g_ref[b, h, p] for b in range(nb)) for p in range(3)] for h in range(N_HEADS)]

        def bucket(k, acc):
            for h in range(N_HEADS):
                tot = sum(jnp.sum(jnp.where(t_ref[p] == k, gsum[h][p], 0.0)) for p in range(3))
                acc = acc + jnp.where((row == h) & (lane == k), tot, 0.0)
            return acc

        o_ref[...] = lax.fori_loop(0, REL_BUCKETS, bucket, jnp.zeros((8, BLK), F32))

    vm = pl.BlockSpec(memory_space=pltpu.VMEM)
    return pl.pallas_call(
        body, name=name, in_specs=[vm, vm], out_specs=vm, out_shape=jax.ShapeDtypeStruct((8, BLK), F32),
        compiler_params=pltpu.CompilerParams(vmem_limit_bytes=VMEM_LIMIT),
    )(gbias, table)


def _place():
    x, y, c = lax.axis_index("x"), lax.axis_index("y"), lax.axis_index("c")
    others = [(1 - x, y), (x, 1 - y), (1 - x, 1 - y)]
    return x, y, c, others


def _remote(src, dst, send_sem, recv_sem, to):
    return pltpu.make_async_remote_copy(src_ref=src, dst_ref=dst, send_sem=send_sem, recv_sem=recv_sem,
                                        device_id=to, device_id_type=MESH)


_HBM = pl.BlockSpec(memory_space=pl.ANY)


class _Exchange:
    def __init__(self, operands, out_shape, n_copies, copies, aliases=None):
        self.operands, self.out_shape, self.n_copies, self.copies = list(operands), list(out_shape), n_copies, copies
        self.aliases = dict(aliases or {})

    def sem_shapes(self):
        return [pltpu.SemaphoreType.DMA((self.n_copies,)), pltpu.SemaphoreType.DMA((self.n_copies,))]


def _start_all(sends):
    for cp in sends:
        cp.start()


def _wait_all(sends, arrivals):
    for cp in arrivals:
        cp.wait_recv()
    for cp in sends:
        cp.wait_send()


def _run_exchange(ex, name):
    ni = len(ex.operands)

    def body(*refs):
        sends, arrivals = ex.copies(refs[:ni], refs[ni:-2], refs[-2], refs[-1])
        _start_all(sends)
        _wait_all(sends, arrivals)

    return list(pl.pallas_call(
        body, name=name, in_specs=[_HBM] * ni, out_specs=[_HBM] * len(ex.out_shape), out_shape=ex.out_shape,
        scratch_shapes=ex.sem_shapes(), input_output_aliases=ex.aliases)(*ex.operands))


def _host_call(body, carry, *, name, grid, in_specs, out_specs, out_shape, operands, scratch_shapes=()):
    in_specs, out_specs, out_shape, scratch_shapes = list(in_specs), list(out_specs), list(out_shape), list(scratch_shapes)
    if carry is None:
        res = pl.pallas_call(body, name=name, grid=grid, in_specs=in_specs, out_specs=out_specs, out_shape=out_shape,
                             scratch_shapes=scratch_shapes, compiler_params=_cparams(*["parallel"] * len(grid)))(*operands)
        return list(res), []
    n_in, n_out, n_scr, c_in, c_out = len(in_specs), len(out_specs), len(scratch_shapes), len(carry.operands), len(carry.out_shape)
    steps = math.prod(grid)

    def wrapped(*refs):
        ins, refs = refs[:n_in], refs[n_in:]
        c_ins, refs = refs[:c_in], refs[c_in:]
        outs, refs = refs[:n_out], refs[n_out:]
        c_outs, refs = refs[:c_out], refs[c_out:]
        scr, (send_sems, recv_sems) = refs[:n_scr], refs[n_scr:]
        step = 0
        for d, size in enumerate(grid):
            step = step * size + pl.program_id(d)

        @pl.when(step == 0)
        def _():
            _start_all(carry.copies(c_ins, c_outs, send_sems, recv_sems)[0])

        body(*ins, *outs, *scr)

        @pl.when(step == steps - 1)
        def _():
            _wait_all(*carry.copies(c_ins, c_outs, send_sems, recv_sems))

    res = pl.pallas_call(
        wrapped, name=name, grid=grid, in_specs=in_specs + [_HBM] * c_in, out_specs=out_specs + [_HBM] * c_out,
        out_shape=out_shape + carry.out_shape, scratch_shapes=scratch_shapes + carry.sem_shapes(),
        input_output_aliases={n_in + i: n_out + j for i, j in carry.aliases.items()},
        compiler_params=_cparams(*["arbitrary"] * len(grid)))(*operands, *carry.operands)
    return list(res[:n_out]), list(res[n_out:])


def _half(which, rows):
    h = rows // 2
    return pl.ds(pl.multiple_of(which * h, 16), h)


def _like(arrays, shape_of=lambda t: t.shape):
    return [jax.ShapeDtypeStruct(shape_of(t), t.dtype) for t in arrays]


def _gather_ici(shards, layer):
    n = len(shards)

    def copies(ins, outs, send_sems, recv_sems, base=0):
        x, y, c, others = _place()
        me = 2 * x + y
        sends, arrivals = [], []
        for a in range(n):
            rows = _half(c, shards[a].shape[1])
            for k, (ox, oy) in enumerate(others):
                sems = (send_sems.at[base + 3 * a + k], recv_sems.at[base + 3 * a + k],(ox, oy, c))
                sends.append(_remote(ins[a].at[layer, rows], outs[a].at[me, rows], *sems))
                landed = outs[a].at[2 * ox + oy, rows]
                arrivals.append(_remote(landed, landed, *sems))
        return sends, arrivals

    return _Exchange(shards, _like(shards, lambda t: (N_CHIPS,) + t.shape[1:]), 3 * n, copies)


def _gather_d2d(gathered):
    n = len(gathered)

    def copies(ins, outs, send_sems, recv_sems, base=0):
        x, y, c, others = _place()
        sends, arrivals = [], []
        for a in range(n):
            r = gathered[a].shape[1]
            for k, (ox, oy) in enumerate(others):
                sems = (send_sems.at[base + 3 * a + k], recv_sems.at[base + 3 * a + k],(x, y, 1 - c))
                mine, theirs = outs[a].at[2 * ox + oy, _half(c, r)], outs[a].at[2 * ox + oy, _half(1 - c, r)]
                sends.append(_remote(mine, mine, *sems))
                arrivals.append(_remote(theirs, theirs, *sems))
        return sends, arrivals

    return _Exchange(gathered, _like(gathered), 3 * n, copies, aliases={a: a for a in range(n)})


def _swap_halves(g):
    n = len(g)

    def copies(ins, outs, send_sems, recv_sems, base=0):
        x, y, c, _ = _place()
        sends, arrivals = [], []
        for a in range(n):
            sems = (send_sems.at[base + a], recv_sems.at[base + a], (x, y, 1 - c))
            sends.append(_remote(ins[a].at[:, _half(1 - c, g[a].shape[1])], outs[a], *sems))
            arrivals.append(_remote(outs[a], outs[a], *sems))
        return sends, arrivals

    return _Exchange(g, _like(g, lambda t: (t.shape[0], t.shape[1] // 2, t.shape[2])), n, copies)


def _scatter_shards(ps):
    n = len(ps)

    def copies(ins, outs, send_sems, recv_sems, base=0):
        x, y, c, others = _place()
        me = 2 * x + y
        sends, arrivals = [], []
        for a in range(n):
            for k, (ox, oy) in enumerate(others):
                sems = (send_sems.at[base + 3 * a + k], recv_sems.at[base + 3 * a + k],(ox, oy, c))
                sends.append(_remote(ins[a].at[2 * ox + oy], outs[a].at[me], *sems))
                slot = outs[a].at[2 * ox + oy]
                arrivals.append(_remote(slot, slot, *sems))
        return sends, arrivals

    return _Exchange(ps, _like(ps), 3 * n, copies)


def _share_halves(mine):
    n = len(mine)

    def copies(ins, outs, send_sems, recv_sems, base=0):
        x, y, c, _ = _place()
        sends, arrivals = [], []
        for a in range(n):
            sems = (send_sems.at[base + a], recv_sems.at[base + a], (x, y, 1 - c))
            sends.append(_remote(ins[a], outs[a], *sems))
            arrivals.append(_remote(outs[a], outs[a], *sems))
        return sends, arrivals

    return _Exchange(mine, _like(mine), n, copies)


def _row_tile(r):
    for cand in (256, 352, 128):
        if r % cand == 0:
            return cand
    return r


def _pair_sum(g, other, core, name):
    ns, h, w = other.shape
    tr = _row_tile(h)
    per_half = h // tr

    def body(core_ref, g_ref, o_ref, out_ref):
        out_ref[...] = (g_ref[...] + o_ref[...]).astype(out_ref.dtype)

    blk = pl.BlockSpec((None, tr, w), lambda k, i, core_ref: (k, i, 0))
    grid_spec = pltpu.PrefetchScalarGridSpec(
        num_scalar_prefetch=1, grid=(ns, per_half),
        in_specs=[pl.BlockSpec((None, tr, w), lambda k, i, core_ref: (k, core_ref[0] * per_half + i, 0)), blk], out_specs=blk)
    return pl.pallas_call(
        body, name=name, grid_spec=grid_spec, out_shape=jax.ShapeDtypeStruct((ns, h, w), MXU_DTYPE),
        compiler_params=_cparams("parallel", "parallel"),
    )(core.reshape(1).astype(jnp.int32), g, other)


def _chip_sum(q, p, chip, name):
    ns, r, w = q.shape
    tr = _row_tile(r)

    def body(chip_ref, q_ref, own_ref, out_ref):
        me = chip_ref[0]
        own = own_ref[...].astype(F32)
        terms = [jnp.where(me == k, own, q_ref[k].astype(F32)) for k in range(ns)]
        out_ref[...] = ((terms[0] + terms[1]) + terms[2]) + terms[3]

    grid_spec = pltpu.PrefetchScalarGridSpec(
        num_scalar_prefetch=1, grid=(r // tr,),
        in_specs=[pl.BlockSpec((ns, tr, w), lambda i, chip_ref: (0, i, 0)),
                  pl.BlockSpec((None, tr, w), lambda i, chip_ref: (chip_ref[0], i, 0))],
        out_specs=pl.BlockSpec((tr, w), lambda i, chip_ref: (i, 0)))
    return pl.pallas_call(
        body, name=name, grid_spec=grid_spec, out_shape=jax.ShapeDtypeStruct((r, w), F32),
        compiler_params=_cparams("parallel"),
    )(chip.reshape(1).astype(jnp.int32), q, p)


def _merge(exchanges):
    if len(exchanges) <= 1:
        return exchanges[0] if exchanges else None
    operands, out_shape, aliases, spans, n = [], [], {}, [], 0
    for ex in exchanges:
        spans.append((len(operands), len(out_shape), n))
        aliases.update({len(operands) + i: len(out_shape) + j for i, j in ex.aliases.items()})
        operands += ex.operands
        out_shape += ex.out_shape
        n += ex.n_copies

    def copies(ins, outs, send_sems, recv_sems, base=0):
        sends, arrivals = [], []
        for ex, (i0, o0, s0) in zip(exchanges, spans):
            s, a = ex.copies(ins[i0:i0 + len(ex.operands)], outs[o0:o0 + len(ex.out_shape)], send_sems, recv_sems, base + s0)
            sends += s
            arrivals += a
        return sends, arrivals

    return _Exchange(operands, out_shape, n, copies, aliases)


def _take(hooks, host):
    stages = (hooks or {}).pop(host, [])
    exchanges = [make() for make, _ in stages]

    def finish(results):
        for (_, done), ex in zip(stages, exchanges):
            done(results[:len(ex.out_shape)])
            results = results[len(ex.out_shape):]

    return _merge(exchanges), finish


def _hook(hooks, host, make, done):
    hooks.setdefault(host, []).append((make, done))


class _WeightPrefetch:
    def __init__(self, names, shards, layer, chip):
        self.names, self.shards, self.layer, self.chip, self.result = names, [shards[n] for n in names], layer, chip, None

    def first(self):
        return _gather_ici(self.shards, self.layer)

    def got_first(self, arrived):
        self.arrived = arrived

    def second(self):
        return _gather_d2d(self.arrived)

    def got_second(self, gathered):
        self.result = {name: lax.dynamic_update_index_in_dim(got, own[self.layer], self.chip, 0)
                       for name, got, own in zip(self.names, gathered, self.shards)}

    def ride(self, hooks, first_host, second_host):
        _hook(hooks, first_host, self.first, self.got_first)
        _hook(hooks, second_host, self.second, self.got_second)

    def run(self, tag):
        self.got_first(_run_exchange(self.first(), f"gather_ici_{tag}"))
        self.got_second(_run_exchange(self.second(), f"gather_d2d_{tag}"))


class _GradReduce:
    def __init__(self, g, chip, core, tag):
        self.names, self.g, self.chip, self.core, self.tag, self.result = list(g), list(g.values()), chip, core, tag, None

    def swap(self):
        return _swap_halves(self.g)

    def got_swap(self, theirs):
        self.pair = [_pair_sum(g, t, self.core, f"pair_sum_{n}_{self.tag}") for n, g, t in zip(self.names, self.g, theirs)]

    def scatter(self):
        return _scatter_shards(self.pair)

    def got_scatter(self, q):
        self.mine = [_chip_sum(qa, pa, self.chip, f"chip_sum_{n}_{self.tag}") for n, qa, pa in zip(self.names, q, self.pair)]

    def share(self):
        return _share_halves(self.mine)

    def got_share(self, theirs):
        self.result = {n: jnp.where(self.core == 0, jnp.concatenate([a, b]), jnp.concatenate([b, a]))
                       for n, a, b in zip(self.names, self.mine, theirs)}

    def ride(self, hooks, swap_host, scatter_host, share_host):
        _hook(hooks, swap_host, self.swap, self.got_swap)
        _hook(hooks, scatter_host, self.scatter, self.got_scatter)
        _hook(hooks, share_host, self.share, self.got_share)

    def run(self):
        self.got_swap(_run_exchange(self.swap(), f"swap_halves_{self.tag}"))
        self.got_scatter(_run_exchange(self.scatter(), f"scatter_shards_{self.tag}"))
        self.got_share(_run_exchange(self.share(), f"share_halves_{self.tag}"))


class _LayerWeights:
    def __init__(self, gathered):
        self.gathered, self.made = gathered, {}

    def __getitem__(self, key):
        if key not in self.made:
            cols = lambda t: jnp.swapaxes(t, 0, 1).reshape(t.shape[1], -1)
            rows = lambda t: t.reshape(-1, t.shape[2])
            if key == "w_in":
                made = jnp.pad(cols(self.gathered("w_in")), ((0, 0), (0, PROJ_PAD - PROJ)))
            elif key == "w_gu":
                made = jnp.concatenate([cols(self.gathered("w_gate")), cols(self.gathered("w_up"))], axis=-1)
            else:
                made = rows(self.gathered(key))
            self.made[key] = made
        return self.made[key]


def _gather_small(pk, name):
    rows, w = pk.shape

    def body(pk_ref, all_ref, sum_ref, send_sems, recv_sems):
        x, y, c, _ = _place()
        me = 4 * x + 2 * y + c
        all_ref[me] = pk_ref[...]
        flips = [(fx, fy, fc) for fx in (0, 1) for fy in (0, 1) for fc in (0, 1)][1:]
        peers = [(x ^ fx, y ^ fy, c ^ fc) for fx, fy, fc in flips]
        sends = [_remote(pk_ref, all_ref.at[me], send_sems.at[k], recv_sems.at[k], peer) for k, peer in enumerate(peers)]
        for cp in sends:
            cp.start()
        for k, (px, py, pc) in enumerate(peers):
            slot = all_ref.at[4 * px + 2 * py + pc]
            _remote(slot, slot, send_sems.at[k], recv_sems.at[k], (px, py, pc)).wait_recv()
        for cp in sends:
            cp.wait_send()
        total = all_ref[0]
        for d in range(1, N_DEV):
            total = total + all_ref[d]
        sum_ref[...] = total

    vm = pl.BlockSpec(memory_space=pltpu.VMEM)
    return pl.pallas_call(
        body, name=name, in_specs=[vm], out_specs=[vm, vm],
        out_shape=[jax.ShapeDtypeStruct((N_DEV, rows, w), F32), jax.ShapeDtypeStruct((rows, w), F32)],
        scratch_shapes=[pltpu.SemaphoreType.DMA((7,)), pltpu.SemaphoreType.DMA((7,))],
    )(pk)


def _row_layout(c, nb, s):
    ch = jnp.swapaxes(c[:, :N_HEADS].reshape(nb, s, N_HEADS), 1, 2)
    ccol = jnp.broadcast_to(ch[..., None], (nb, N_HEADS, s, ATT))
    crow = jnp.broadcast_to(ch.reshape(nb, N_HEADS, s // ATT, 1, ATT), (nb, N_HEADS, s // ATT, 8, ATT))
    return ccol, crow


def _dil_bias(rel_bias, name):
    def body(rel_ref, t_ref, o_ref):
        for p in range(len(DIL_PATTERNS)):
            table = t_ref[p]

            def bucket(k, accs, table=table):
                return tuple(jnp.where(table == k, rel_ref[k, h], acc) for h, acc in enumerate(accs))

            accs = lax.fori_loop(0, REL_BUCKETS, bucket, tuple(jnp.full((BLK, 2 * BLK), NEG, F32) for _ in range(N_HEADS)))
            for h in range(N_HEADS):
                o_ref[p, h] = accs[h]

    vm = pl.BlockSpec(memory_space=pltpu.VMEM)
    return pl.pallas_call(
        body, name=name, in_specs=[pl.BlockSpec(memory_space=pltpu.SMEM), vm], out_specs=vm,
        out_shape=jax.ShapeDtypeStruct((len(DIL_PATTERNS), N_HEADS, BLK, 2 * BLK), F32),
        compiler_params=pltpu.CompilerParams(vmem_limit_bytes=VMEM_LIMIT),
    )(rel_bias, jnp.asarray(_bucket_table()))


def _layer_forward(x, x_b, wts, small, nb, s, tag, hooks=None):
    proj = _matmul(x_b, wts["w_in"], "proj", tag)

    carry, finish = _take(hooks, "sb_fwd")
    o_sb, carried = _sb_fwd(proj, nb, s, f"sb_fwd_{tag}", carry)
    finish(carried)

    bias = _dil_bias(small["rel_bias"], f"dil_bias_{tag}")
    carry, finish = _take(hooks, "dil_fwd")
    (o_dl, lse_dl), carried = _dil_attention_fwd(proj, bias, nb, s, f"dil_fwd_{tag}", carry)
    finish(carried)

    fb = jnp.zeros((8, BLK), F32).at[0, :N_HEADS].set(small["f_bias"])
    csum = _fox_gates_fwd(proj, fb, nb, s, f"fox_gates_{tag}")
    ccol, crow = _row_layout(csum, nb, s)
    carry, finish = _take(hooks, "fox_fwd")
    (o_fx, lse_fx), carried = _fox_fwd(proj, ccol, crow, nb, s, f"fox_fwd_{tag}", carry)
    finish(carried)

    cw = jnp.zeros((8, CONV_W), F32).at[:3].set(small["conv_w"])
    o_cv = _conv_fwd(proj, cw, nb, s, f"conv_fwd_{tag}")

    mixed = jnp.concatenate([o_sb, o_dl, o_fx, o_cv], axis=-1).astype(MXU_DTYPE)
    pre1, x1, x1_b = _matmul_post_norm(mixed, wts["w_out"], x, small["ln1_g"], small["ln1_b"], f"out_proj_ln1_{tag}")
    carry, finish = _take(hooks, "ffn_in")
    (gate, up, hid), carried = _ffn_in(x1_b, wts["w_gu"], f"ffn_in_{tag}", carry)
    finish(carried)
    pre2, x2, x2_b = _matmul_post_norm(hid, wts["w_down"], x1, small["ln2_g"], small["ln2_b"], f"ffn_out_ln2_{tag}")
    saved = dict(x_b=x_b, proj=proj, bias=bias, o_dl=o_dl, lse_dl=lse_dl, fb=fb, ccol=ccol, crow=crow, o_fx=o_fx,
                 lse_fx=lse_fx, cw=cw, mixed=mixed, pre1=pre1, x1_b=x1_b, gate=gate, up=up, hid=hid, pre2=pre2)
    return (x2, x2_b), saved


def _layer_backward(dx2, sv, wts, small, nb, s, tag, hooks=None, ffn_grads_ready=None):
    t = nb * s
    dpre2, dgb2 = _ln_bwd(dx2, sv["pre2"], small["ln2_g"], f"ln2_bwd_{tag}")
    dpre2_b = dpre2.astype(MXU_DTYPE)
    carry, finish = _take(hooks, "ffn_out_dx")
    (dgate, dup), carried = _ffn_out_dx(dpre2_b, wts["w_down"], sv["gate"], sv["up"], f"ffn_out_dx_{tag}", carry)
    finish(carried)
    dw_down = _matmul(sv["hid"], dpre2_b, "ffn_out_dw", tag, trans_a=True)
    dx1 = _ffn_in_dx(dgate, dup, wts["w_gu"], dpre2, f"ffn_in_dx_{tag}")
    x1_b = sv["x1_b"]
    dw_gate = _matmul(x1_b, dgate, "ffn_in_dw", f"{tag}_gate", trans_a=True)
    dw_up = _matmul(x1_b, dup, "ffn_in_dw", f"{tag}_up", trans_a=True)
    if ffn_grads_ready:
        ffn_grads_ready(dw_down, dw_gate, dw_up)

    dpre1, dgb1 = _ln_bwd(dx1, sv["pre1"], small["ln1_g"], f"ln1_bwd_{tag}")
    dpre1_b = dpre1.astype(MXU_DTYPE)
    dmixed = _matmul(dpre1_b, wts["w_out"], "out_proj_dx", tag, trans_b=True)
    dw_out = _matmul(sv["mixed"], dpre1_b, "out_proj_dw", tag, trans_a=True)
    proj = sv["proj"]

    carry, finish = _take(hooks, "sb_bwd")
    (dq_sb, dk_sb, dv_sb), carried = _sb_bwd(proj, dmixed, nb, s, f"sb_bwd_{tag}", carry)
    finish(carried)

    delta_dl = _delta_kernel(dmixed, sv["o_dl"], nb, s, f"dil_delta_{tag}")
    carry, finish = _take(hooks, "dil_bwd")
    (dq_dl, dk_dl, dv_dl, gbias), carried = _dil_attention_bwd(proj, dmixed, sv["lse_dl"], delta_dl, sv["bias"], nb, s,
                                                               f"dil_bwd_{tag}", carry)
    finish(carried)
    drel = _bucket_reduce(gbias, jnp.asarray(_bucket_table()), f"rel_bias_grad_{tag}")

    carry, finish = _take(hooks, "fox_bwd")
    (dq_fx, dk_fx, dv_fx, dcol), carried = _fox_bwd(proj, dmixed, sv["lse_fx"], sv["ccol"], sv["crow"], nb, s,
                                                    f"fox_bwd_{tag}", carry)
    finish(carried)
    dcs = -jnp.swapaxes(dcol[:, :, :, 0, :].reshape(nb, N_HEADS, s), 1, 2).reshape(t, N_HEADS)
    dcs = jnp.pad(dcs, ((0, 0), (0, BLK - N_HEADS)))
    dfx, dfb = _fox_gates_bwd(dcs, proj, sv["fb"], nb, s, f"fox_gates_bwd_{tag}")

    dgates, dcw = _conv_bwd(dmixed, proj, sv["cw"], nb, s, f"conv_bwd_{tag}")

    dproj = jnp.concatenate([dq_sb, dk_sb, dv_sb, dq_dl, dk_dl, dv_dl, dq_fx, dk_fx, dv_fx, dgates, dfx],
                            axis=-1).astype(MXU_DTYPE)
    dx = _matmul(dproj, wts["w_in"], "proj_dx", tag, add=dpre1, add_scale=ALPHA, trans_b=True)
    dw_in = _matmul(sv["x_b"], dproj, "proj_dw", tag, trans_a=True)

    grads = dict(w_in=dw_in[:, :PROJ], w_out=dw_out, w_gate=dw_gate, w_up=dw_up, w_down=dw_down,
                 ln1_g=dgb1[0], ln1_b=dgb1[1], ln2_g=dgb2[0], ln2_b=dgb2[1], conv_w=dcw[:3], f_bias=dfb[0, :N_HEADS],
                 rel_bias=drel[:N_HEADS, :REL_BUCKETS].T)
    return dx, grads


class _NoExchanges:
    def forward_hooks(self, layer):
        return None

    def backward_hooks(self, layer):
        return None

    def ffn_grads_ready(self, layer):
        return None

    def layer_done(self, layer, grads):
        pass


def _local_step(x, target, weights_of, small_all, schedule=None):
    schedule = schedule or _NoExchanges()
    nb, s, d = x.shape
    h = x.reshape(nb * s, d)
    h_b = h.astype(MXU_DTYPE)
    saved = []
    for layer in range(DEPTH):
        wts = weights_of(layer)
        (h, h_b), sv = _layer_forward(h, h_b, wts, small_all[layer], nb, s, f"l{layer}", schedule.forward_hooks(layer))
        saved.append((sv, wts))
    dy, lossp = _loss_kernel(h, target.reshape(nb * s, d), "loss")
    grads = [None] * DEPTH
    for layer in reversed(range(DEPTH)):
        sv, wts = saved[layer]
        dy, grads[layer] = _layer_backward(dy, sv, wts, small_all[layer], nb, s, f"l{layer}",
                                           schedule.backward_hooks(layer), schedule.ffn_grads_ready(layer))
        schedule.layer_done(layer, grads[layer])
    return lossp, dy.reshape(nb, s, d), grads


_BIG = ("w_in", "w_out", "w_gate", "w_up", "w_down")
_COL_SHARDED = ("w_in", "w_gate", "w_up")


class _Schedule:
    def __init__(self, shards, chip, core):
        self.chip, self.core, self.reduces = chip, core, [[] for _ in range(DEPTH)]
        first = _WeightPrefetch(["w_in"], shards, 0, chip)
        first.run("l0_w_in")
        rest = _WeightPrefetch(["w_out", "w_gate", "w_up", "w_down"], shards, 0, chip)
        ahead_a = _WeightPrefetch(["w_in", "w_out", "w_down"], shards, 1, chip)
        ahead_b = _WeightPrefetch(["w_gate", "w_up"], shards, 1, chip)
        self.fetches = [[first, rest], [ahead_a, ahead_b]]
        self.forward, self.backward = [{} for _ in range(DEPTH)], [{} for _ in range(DEPTH)]
        rest.ride(self.forward[0], "sb_fwd", "fox_fwd")
        ahead_a.ride(self.forward[0], "dil_fwd", "ffn_in")
        ahead_b.ride(self.forward[0], "fox_fwd", "ffn_in")

    def weights(self, layer):
        def gathered(name):
            return next(f.result[name] for f in self.fetches[layer] if name in f.names)
        return _LayerWeights(gathered)

    def forward_hooks(self, layer):
        return self.forward[layer]

    def backward_hooks(self, layer):
        return self.backward[layer]

    def _reduce(self, layer, grads, tag):
        red = _GradReduce({name: _by_chip(name, g) for name, g in grads.items()}, self.chip, self.core, tag)
        self.reduces[layer].append(red)
        return red

    def ffn_grads_ready(self, layer):
        if layer != 0:
            return None

        def ready(dw_down, dw_gate, dw_up):
            red = self._reduce(0, dict(w_gate=dw_gate, w_up=dw_up, w_down=dw_down), "l0_ffn")
            red.ride(self.backward[0], "sb_bwd", "dil_bwd", "fox_bwd")

        return ready

    def layer_done(self, layer, grads):
        if layer == 1:
            self._reduce(1, {name: grads[name] for name in _BIG}, "l1").ride(self.backward[0], "ffn_out_dx", "sb_bwd", "fox_bwd")
        else:
            self._reduce(0, dict(w_in=grads["w_in"], w_out=grads["w_out"]), "l0_attn").run()

    def reduced(self, layer, name):
        return next(r.result[name] for r in self.reduces[layer] if name in r.names)


def _by_chip(name, g):
    if name in _COL_SHARDED:
        return jnp.swapaxes(g.reshape(g.shape[0], N_CHIPS, -1), 0, 1)
    return g.reshape(N_CHIPS, -1, g.shape[1])


_SMALL_LAYOUT = (("ln1_g", 0), ("ln1_b", 2), ("ln2_g", 4), ("ln2_b", 6), ("conv_w", 8))
_ROW_MISC = 10
_ROW_LOSS = 11


def _pack_small(per_layer, rel_bias, loss=None):
    pk = jnp.zeros((SMALL_ROWS, D_MODEL), F32)
    for name, row in _SMALL_LAYOUT:
        for l in range(DEPTH):
            v = per_layer[l][name].reshape(-1)
            pk = pk.at[row + l, :v.shape[0]].set(v)
    fb = jnp.concatenate([per_layer[l]["f_bias"] for l in range(DEPTH)])
    pk = pk.at[_ROW_MISC, :2 * N_HEADS].set(fb)
    pk = pk.at[_ROW_MISC, BLK:BLK + REL_BUCKETS * N_HEADS].set(rel_bias.reshape(-1))
    if loss is not None:
        pk = pk.at[_ROW_LOSS, 0].set(loss)
    return pk


def _unpack_small(pk, conv_cols):
    out = {}
    for name, row in _SMALL_LAYOUT:
        n = 3 * conv_cols if name == "conv_w" else D_MODEL
        v = pk[row:row + DEPTH, :n]
        out[name] = v.reshape(DEPTH, 3, conv_cols) if name == "conv_w" else v
    out["f_bias"] = pk[_ROW_MISC, :2 * N_HEADS].reshape(DEPTH, N_HEADS)
    out["rel_bias"] = pk[_ROW_MISC, BLK:BLK + REL_BUCKETS * N_HEADS].reshape(REL_BUCKETS, N_HEADS)
    return out


_WEIGHTS = ("w_in", "f_bias", "conv_w", "w_out", "rel_bias", "ln1_g", "ln1_b", "w_gate", "w_up", "w_down", "ln2_g", "ln2_b")


def kernel(x, w_in, f_bias, conv_w, w_out, rel_bias, ln1_g, ln1_b, w_gate, w_up, w_down, ln2_g, ln2_b, loss_target, m_w_in, m_f_bias, m_conv_w, m_w_out, m_rel_bias, m_ln1_g, m_ln1_b, m_w_gate, m_w_up, m_w_down, m_ln2_g, m_ln2_b, v_w_in, v_f_bias, v_conv_w, v_w_out, v_rel_bias, v_ln1_g, v_ln1_b, v_w_gate, v_w_up, v_w_down, v_ln2_g, v_ln2_b):
    w = dict(w_in=w_in, f_bias=f_bias, conv_w=conv_w, w_out=w_out, rel_bias=rel_bias, ln1_g=ln1_g, ln1_b=ln1_b,
             w_gate=w_gate, w_up=w_up, w_down=w_down, ln2_g=ln2_g, ln2_b=ln2_b)
    m = dict(w_in=m_w_in, f_bias=m_f_bias, conv_w=m_conv_w, w_out=m_w_out, rel_bias=m_rel_bias, ln1_g=m_ln1_g,
             ln1_b=m_ln1_b, w_gate=m_w_gate, w_up=m_w_up, w_down=m_w_down, ln2_g=m_ln2_g, ln2_b=m_ln2_b)
    v = dict(w_in=v_w_in, f_bias=v_f_bias, conv_w=v_conv_w, w_out=v_w_out, rel_bias=v_rel_bias, ln1_g=v_ln1_g,
             ln1_b=v_ln1_b, w_gate=v_w_gate, w_up=v_w_up, w_down=v_w_down, ln2_g=v_ln2_g, ln2_b=v_ln2_b)
    chip = 2 * lax.axis_index("x") + lax.axis_index("y")
    core = lax.axis_index("c")
    conv_shard = CONV_W // N_CHIPS

    schedule = _Schedule({name: w[name].astype(MXU_DTYPE) for name in _BIG}, chip, core)
    cw_pk = jnp.zeros((8, D_MODEL), F32).at[0, :DEPTH * 3 * conv_shard].set(conv_w.reshape(-1))
    cw_all, _ = _gather_small(cw_pk, "gather_conv_w")
    cw_chips = cw_all[0::2, 0, :DEPTH * 3 * conv_shard].reshape(N_CHIPS, DEPTH, 3, conv_shard)
    conv_full = jnp.moveaxis(cw_chips, 0, 2).reshape(DEPTH, 3, CONV_W)
    small_all = [dict(f_bias=f_bias[l], conv_w=conv_full[l], rel_bias=rel_bias, ln1_g=ln1_g[l], ln1_b=ln1_b[l],
                      ln2_g=ln2_g[l], ln2_b=ln2_b[l]) for l in range(DEPTH)]

    lossp, grad_x, grads = _local_step(x, loss_target, schedule.weights, small_all, schedule)
    big_g = {name: jnp.stack([schedule.reduced(l, name) for l in range(DEPTH)]) for name in _BIG}

    drel = grads[0]["rel_bias"] + grads[1]["rel_bias"]
    small_pk = _pack_small(grads, drel, lossp[0, 0])
    _, small_sum = _gather_small(small_pk, "gather_small_grads")
    loss = small_sum[_ROW_LOSS, 0]
    small_g = _unpack_small(small_sum, CONV_W)
    small_g["conv_w"] = lax.dynamic_slice_in_dim(small_g["conv_w"], chip * conv_shard, conv_shard, axis=2)

    out_g, out_d, out_m, out_v = dict(small_g), {}, {}, {}
    for name in _BIG:
        out_g[name] = big_g[name]
        out_d[name], out_m[name], out_v[name] = _adamw(w[name], big_g[name], m[name], v[name], f"adamw_{name}")
    as_3d = lambda t: t if t.ndim == 3 else t[None]
    for name in _WEIGHTS:
        if name not in _BIG:
            stepped = _adamw(as_3d(w[name]), as_3d(small_g[name]), as_3d(m[name]), as_3d(v[name]), f"adamw_{name}")
            out_d[name], out_m[name], out_v[name] = (t.reshape(w[name].shape) for t in stepped)

    return (loss, grad_x, *[out_g[n] for n in _WEIGHTS], *[out_d[n] for n in _WEIGHTS],
            *[out_m[n] for n in _WEIGHTS], *[out_v[n] for n in _WEIGHTS])
```

```python
import functools
import math

import numpy as np
import jax
import jax.numpy as jnp
from jax import lax
from jax.experimental import pallas as pl
from jax.experimental.pallas import tpu as pltpu

F32 = jnp.float32
BF16 = jnp.bfloat16
MXU_DTYPE = BF16

D_MODEL = 1024
HEAD_DIM = 64
N_HEADS = 4
BLK = 128
ATT = 256
QT = 512
CONV_W = 256
PROJ = 3076
PROJ_PAD = 3200
D_FF = 2816
DEPTH = 2
ALPHA = (2 * DEPTH) ** 0.25
LN_EPS = 1e-5
NEG = -1e30
DIL_PATTERNS = ((128, 1), (512, 4), (2048, 16))
REL_BUCKETS = 32
N_CHIPS = 4
N_DEV = 8
SMALL_ROWS = 16

ADAM_LR = 0.001
ADAM_B1 = 0.9
ADAM_B2 = 0.999
ADAM_EPS = 1e-08
ADAM_WD = 0.01
ADAM_STEP = 10

VMEM_LIMIT = 56 * 2 ** 20
MESH = pl.DeviceIdType.MESH


def _cparams(*sem):
    return pltpu.CompilerParams(dimension_semantics=tuple(sem), vmem_limit_bytes=VMEM_LIMIT)


def _dot(a, b):
    return jnp.dot(a.astype(MXU_DTYPE), b.astype(MXU_DTYPE), preferred_element_type=F32)


def _dot_nt(a, b):
    return lax.dot_general(a.astype(MXU_DTYPE), b.astype(MXU_DTYPE), (((1,), (1,)), ((), ())),
                           preferred_element_type=F32)


def _dot_tn(a, b):
    return lax.dot_general(a.astype(MXU_DTYPE), b.astype(MXU_DTYPE), (((0,), (0,)), ((), ())),
                           preferred_element_type=F32)


def _split_dot(x, ones, passes):
    acc, rest = None, x
    for p in range(passes):
        piece = rest.astype(MXU_DTYPE)
        part = jnp.dot(piece, ones, preferred_element_type=F32)
        acc = part if acc is None else acc + part
        if p + 1 < passes:
            rest = rest - piece.astype(F32)
    return acc


def _split_dot_lhs(ones, x, passes):
    acc, rest = None, x
    for p in range(passes):
        piece = rest.astype(MXU_DTYPE)
        part = jnp.dot(ones, piece, preferred_element_type=F32)
        acc = part if acc is None else acc + part
        if p + 1 < passes:
            rest = rest - piece.astype(F32)
    return acc


def _iota2(shape, axis):
    return lax.broadcasted_iota(jnp.int32, shape, axis)


_TILES = {"proj": (1024, 640, 1024), "ffn_out_dw": (1408, 1024, 2048),
          "ffn_in_dw": (1024, 1408, 2048), "out_proj_dx": (1024, 1024, 1024),
          "out_proj_dw": (1024, 1024, 2048), "proj_dx": (1024, 512, 3200), "proj_dw": (1024, 640, 2048)}


def _matmul(a, b, kind, tag, *, out_dtype=F32, add=None, add_scale=1.0, trans_a=False, trans_b=False):
    k, m = a.shape if trans_a else a.shape[::-1]
    n = b.shape[0] if trans_b else b.shape[1]
    tm, tn, tk = _TILES[kind]
    tm, tk, name = min(tm, m), min(tk, k), f"{kind}_{tag}"
    assert m % tm == 0 and n % tn == 0 and k % tk == 0, (a.shape, b.shape, tm, tn, tk)
    nk = k // tk

    def body(*refs):
        if add is None:
            a_ref, b_ref, o_ref = refs[:3]
            c_ref, scr = None, refs[3:]
        else:
            a_ref, b_ref, c_ref, o_ref = refs[:4]
            scr = refs[4:]
        dot = _dot_tn if trans_a else _dot_nt if trans_b else _dot
        part = dot(a_ref[...], b_ref[...])

        def finish(acc):
            if c_ref is not None:
                acc = acc + add_scale * c_ref[...]
            o_ref[...] = acc.astype(out_dtype)

        if nk == 1:
            finish(part)
        else:
            acc_ref = scr[0]
            kk = pl.program_id(2)

            @pl.when(kk == 0)
            def _():
                acc_ref[...] = part

            @pl.when(kk > 0)
            def _():
                acc_ref[...] += part

            @pl.when(kk == nk - 1)
            def _():
                finish(acc_ref[...])

    b_spec = pl.BlockSpec((tn, tk), lambda i, j, kk: (j, kk)) if trans_b else pl.BlockSpec((tk, tn), lambda i, j, kk: (kk, j))
    a_spec = pl.BlockSpec((tk, tm), lambda i, j, kk: (kk, i)) if trans_a else pl.BlockSpec((tm, tk), lambda i, j, kk: (i, kk))
    in_specs = [a_spec, b_spec]
    operands = [a, b]
    if add is not None:
        in_specs.append(pl.BlockSpec((tm, tn), lambda i, j, kk: (i, j)))
        operands.append(add)
    return pl.pallas_call(
        body, name=name, grid=(m // tm, n // tn, nk), in_specs=in_specs,
        out_specs=pl.BlockSpec((tm, tn), lambda i, j, kk: (i, j)),
        out_shape=jax.ShapeDtypeStruct((m, n), out_dtype),
        scratch_shapes=[pltpu.VMEM((tm, tn), F32)] if nk > 1 else [],
        compiler_params=_cparams("parallel", "parallel", "arbitrary"),
    )(*operands)


def _matmul_post_norm(a, b, xin, g, beta, name):
    t, k = a.shape
    d = b.shape[1]
    tm = 512

    def body(a_ref, b_ref, x_ref, g_ref, beta_ref, pre_ref, y_ref, yb_ref):
        pre = ALPHA * x_ref[...] + _dot(a_ref[...], b_ref[...])
        xhat, _ = _ln_stats(pre)
        y = xhat * g_ref[...] + beta_ref[...]
        pre_ref[...] = pre
        y_ref[...] = y
        yb_ref[...] = y.astype(yb_ref.dtype)

    row = pl.BlockSpec((tm, d), lambda i: (i, 0))
    vec = pl.BlockSpec((1, d), lambda i: (0, 0))
    return pl.pallas_call(
        body, name=name, grid=(t // tm,),
        in_specs=[pl.BlockSpec((tm, k), lambda i: (i, 0)), pl.BlockSpec((k, d), lambda i: (0, 0)), row, vec, vec],
        out_specs=[row, row, row],
        out_shape=[jax.ShapeDtypeStruct((t, d), F32)] * 2 + [jax.ShapeDtypeStruct((t, d), MXU_DTYPE)],
        compiler_params=_cparams("parallel"),
    )(a, b, xin, g.reshape(1, d), beta.reshape(1, d))


def _ffn_in(x1, w_gu, name, carry=None):
    t, d = x1.shape
    tm, tn = 512, D_FF // 2
    nj = D_FF // tn

    def body(x_ref, wg_ref, wu_ref, gate_ref, up_ref, h_ref):
        xb = x_ref[...].astype(MXU_DTYPE)
        gate = _dot(xb, wg_ref[...])
        up = _dot(xb, wu_ref[...])
        gate_ref[...] = gate
        up_ref[...] = up
        h_ref[...] = (gate * (1.0 / (1.0 + jnp.exp(-gate))) * up).astype(h_ref.dtype)

    out = pl.BlockSpec((tm, tn), lambda i, j: (i, j))
    return _host_call(
        body, carry, name=name, grid=(t // tm, nj),
        in_specs=[pl.BlockSpec((tm, d), lambda i, j: (i, 0)), pl.BlockSpec((d, tn), lambda i, j: (0, j)),
                  pl.BlockSpec((d, tn), lambda i, j: (0, nj + j))],
        out_specs=[out, out, out],
        out_shape=[jax.ShapeDtypeStruct((t, D_FF), F32)] * 2 + [jax.ShapeDtypeStruct((t, D_FF), MXU_DTYPE)],
        operands=(x1, w_gu, w_gu))


def _ffn_out_dx(dy, w_down, gate, up, name, carry=None):
    t, d = dy.shape
    tm, tn = 512, D_FF // 2

    def body(dy_ref, w_ref, gate_ref, up_ref, dg_ref, du_ref):
        dh = _dot_nt(dy_ref[...], w_ref[...])
        gate = gate_ref[...]
        sig = 1.0 / (1.0 + jnp.exp(-gate))
        dg_ref[...] = (dh * up_ref[...] * sig * (1.0 + gate * (1.0 - sig))).astype(dg_ref.dtype)
        du_ref[...] = (dh * gate * sig).astype(du_ref.dtype)

    tile = pl.BlockSpec((tm, tn), lambda i, j: (i, j))
    return _host_call(
        body, carry, name=name, grid=(t // tm, D_FF // tn),
        in_specs=[pl.BlockSpec((tm, d), lambda i, j: (i, 0)), pl.BlockSpec((tn, d), lambda i, j: (j, 0)), tile, tile],
        out_specs=[tile, tile], out_shape=[jax.ShapeDtypeStruct((t, D_FF), MXU_DTYPE)] * 2,
        operands=(dy, w_down, gate, up))


def _ffn_in_dx(dgate, dup, w_gu, add, name):
    t = dgate.shape[0]
    d = w_gu.shape[0]
    tm, tk = 1024, D_FF // 2
    nk = D_FF // tk

    def body(dg_ref, du_ref, wg_ref, wu_ref, add_ref, o_ref, acc_ref):
        kk = pl.program_id(1)
        part = _dot_nt(dg_ref[...], wg_ref[...]) + _dot_nt(du_ref[...], wu_ref[...])

        @pl.when(kk == 0)
        def _():
            acc_ref[...] = part

        @pl.when(kk > 0)
        def _():
            acc_ref[...] += part

        @pl.when(kk == nk - 1)
        def _():
            o_ref[...] = acc_ref[...] + ALPHA * add_ref[...]

    act = pl.BlockSpec((tm, tk), lambda i, kk: (i, kk))
    row = pl.BlockSpec((tm, d), lambda i, kk: (i, 0))
    return pl.pallas_call(
        body, name=name, grid=(t // tm, nk),
        in_specs=[act, act, pl.BlockSpec((d, tk), lambda i, kk: (0, kk)), pl.BlockSpec((d, tk), lambda i, kk: (0, nk + kk)), row],
        out_specs=row, out_shape=jax.ShapeDtypeStruct((t, d), F32), scratch_shapes=[pltpu.VMEM((tm, d), F32)],
        compiler_params=_cparams("parallel", "arbitrary"),
    )(dgate, dup, w_gu, w_gu, add)


def _ln_stats(pre):
    mu = jnp.mean(pre, axis=-1, keepdims=True)
    xc = pre - mu
    var = jnp.mean(xc * xc, axis=-1, keepdims=True)
    rstd = lax.rsqrt(var + LN_EPS)
    return xc * rstd, rstd


def _ln_bwd(dy, pre, g, name):
    t, d = dy.shape
    tile = 256

    def body(dy_ref, pre_ref, g_ref, dpre_ref, dpre_b_ref, dgb_ref):
        dyv = dy_ref[...]
        xhat, rstd = _ln_stats(pre_ref[...])
        dxh = dyv * g_ref[...]
        m1 = jnp.mean(dxh, axis=-1, keepdims=True)
        m2 = jnp.mean(dxh * xhat, axis=-1, keepdims=True)
        dpre = rstd * (dxh - m1 - xhat * m2)
        dpre_ref[...] = dpre
        dpre_b_ref[...] = dpre.astype(dpre_b_ref.dtype)

        @pl.when(pl.program_id(0) == 0)
        def _():
            dgb_ref[...] = jnp.zeros_like(dgb_ref)

        dgb_ref[0:1, :] += jnp.sum(dyv * xhat, axis=0, keepdims=True)
        dgb_ref[1:2, :] += jnp.sum(dyv, axis=0, keepdims=True)

    row = pl.BlockSpec((tile, d), lambda i: (i, 0))
    return pl.pallas_call(
        body, name=name, grid=(t // tile,), in_specs=[row, row, pl.BlockSpec((1, d), lambda i: (0, 0))],
        out_specs=[row, row, pl.BlockSpec((8, d), lambda i: (0, 0))],
        out_shape=[jax.ShapeDtypeStruct((t, d), F32), jax.ShapeDtypeStruct((t, d), MXU_DTYPE), jax.ShapeDtypeStruct((8, d), F32)],
        compiler_params=_cparams("arbitrary"),
    )(dy, pre, g.reshape(1, d))


def _loss_kernel(y, target, name):
    t, d = y.shape
    tile = 512

    def body(y_ref, t_ref, dy_ref, l_ref):
        err = y_ref[...] - t_ref[...]
        dy_ref[...] = err * (1.0 / d)

        @pl.when(pl.program_id(0) == 0)
        def _():
            l_ref[...] = jnp.zeros_like(l_ref)

        l_ref[...] += jnp.sum(err * err) * (0.5 / d)

    row = pl.BlockSpec((tile, d), lambda i: (i, 0))
    return pl.pallas_call(
        body, name=name, grid=(t // tile,), in_specs=[row, row],
        out_specs=[row, pl.BlockSpec((8, 128), lambda i: (0, 0))],
        out_shape=[jax.ShapeDtypeStruct((t, d), F32), jax.ShapeDtypeStruct((8, 128), F32)],
        compiler_params=_cparams("arbitrary"),
    )(y, target)


def _adamw(w, g, m, v, name):
    nl, r, c = w.shape
    tr = r
    for cand in (256, 352, 128, 64, 16, 8):
        if r % cand == 0:
            tr = cand
            break

    def body(w_ref, g_ref, m_ref, v_ref, d_ref, nm_ref, nv_ref):
        gv = g_ref[...]
        nm = ADAM_B1 * m_ref[...] + (1.0 - ADAM_B1) * gv
        nv = ADAM_B2 * v_ref[...] + (1.0 - ADAM_B2) * (gv * gv)
        m_hat = nm / (1.0 - ADAM_B1 ** ADAM_STEP)
        v_hat = nv / (1.0 - ADAM_B2 ** ADAM_STEP)
        d_ref[...] = -ADAM_LR * (m_hat / (jnp.sqrt(v_hat) + ADAM_EPS) + ADAM_WD * w_ref[...])
        nm_ref[...] = nm
        nv_ref[...] = nv

    blk = pl.BlockSpec((1, tr, c), lambda l, i: (l, i, 0))
    return pl.pallas_call(
        body, name=name, grid=(nl, r // tr), in_specs=[blk] * 4, out_specs=[blk] * 3,
        out_shape=[jax.ShapeDtypeStruct(w.shape, F32)] * 3, compiler_params=_cparams("parallel", "parallel"),
    )(w, g, m, v)


def _shift_down(u, k, rows):
    return jnp.where(rows >= k, pltpu.roll(u, k, 0), 0.0)


def _shift_up(u, k, rows, s):
    return jnp.where(rows < s - k, pltpu.roll(u, s - k, 0), 0.0)


def _conv_fwd(proj, conv_w, nb, s, name):
    def body(b_ref, c_ref, h_ref, w_ref, o_ref):
        rows = _iota2((s, CONV_W), 0)
        u = c_ref[...] * h_ref[...]
        y = w_ref[2:3, :] * u + w_ref[1:2, :] * _shift_down(u, 1, rows) + w_ref[0:1, :] * _shift_down(u, 2, rows)
        o_ref[...] = b_ref[...] * y

    col = lambda j: pl.BlockSpec((s, CONV_W), lambda b: (b, j))
    return pl.pallas_call(
        body, name=name, grid=(nb,),
        in_specs=[col(9), col(10), col(11), pl.BlockSpec((8, CONV_W), lambda b: (0, 0))],
        out_specs=pl.BlockSpec((s, CONV_W), lambda b: (b, 0)),
        out_shape=jax.ShapeDtypeStruct((nb * s, CONV_W), F32), compiler_params=_cparams("parallel"),
    )(proj, proj, proj, conv_w)


def _conv_bwd(dmixed, proj, conv_w, nb, s, name):
    def body(do_ref, b_ref, c_ref, h_ref, w_ref, dg_ref, dw_ref):
        rows = _iota2((s, CONV_W), 0)
        cg, hg, bg, dout = c_ref[...], h_ref[...], b_ref[...], do_ref[...]
        u = cg * hg
        u1 = _shift_down(u, 1, rows)
        u2 = _shift_down(u, 2, rows)
        y = w_ref[2:3, :] * u + w_ref[1:2, :] * u1 + w_ref[0:1, :] * u2
        dy = dout * bg
        du = w_ref[2:3, :] * dy + w_ref[1:2, :] * _shift_up(dy, 1, rows, s) + w_ref[0:1, :] * _shift_up(dy, 2, rows, s)
        dg_ref[:, 0:CONV_W] = dout * y
        dg_ref[:, CONV_W:2 * CONV_W] = du * hg
        dg_ref[:, 2 * CONV_W:3 * CONV_W] = du * cg

        @pl.when(pl.program_id(0) == 0)
        def _():
            dw_ref[...] = jnp.zeros_like(dw_ref)

        dw_ref[0:1, :] += jnp.sum(dy * u2, axis=0, keepdims=True)
        dw_ref[1:2, :] += jnp.sum(dy * u1, axis=0, keepdims=True)
        dw_ref[2:3, :] += jnp.sum(dy * u, axis=0, keepdims=True)

    col = lambda j: pl.BlockSpec((s, CONV_W), lambda b: (b, j))
    return pl.pallas_call(
        body, name=name, grid=(nb,),
        in_specs=[col(3), col(9), col(10), col(11), pl.BlockSpec((8, CONV_W), lambda b: (0, 0))],
        out_specs=[pl.BlockSpec((s, 3 * CONV_W), lambda b: (b, 0)), pl.BlockSpec((8, CONV_W), lambda b: (0, 0))],
        out_shape=[jax.ShapeDtypeStruct((nb * s, 3 * CONV_W), F32), jax.ShapeDtypeStruct((8, CONV_W), F32)],
        compiler_params=_cparams("arbitrary"),
    )(dmixed, proj, proj, proj, conv_w)


def _col_spec(s, base):
    return pl.BlockSpec((s, BLK), lambda b, p: (b, base + p))


def _qrows(i):
    return pl.ds(pl.multiple_of(i * QT, QT), QT)


def _rows(j):
    return pl.ds(pl.multiple_of(j * ATT, ATT), ATT)


def _keys_upto(i):
    return (i + 1) * (QT // ATT)


def _triangle(keep):
    return keep(_iota2((ATT, ATT), 0), _iota2((ATT, ATT), 1)).astype(MXU_DTYPE)


def _rows128(i):
    return pl.ds(pl.multiple_of(i * BLK, BLK), BLK)


def _log_sigmoid_parts(z):
    e = jnp.exp(-jnp.abs(z))
    l1p = jnp.log(1.0 + e)
    lb = jnp.minimum(z, 0.0) - l1p
    return lb, lb - z, e


def _head_masks():
    lane = _iota2((1, BLK), 1)
    return [(lane >= h * HEAD_DIM) & (lane < (h + 1) * HEAD_DIM) for h in range(2)]


def _split_heads(ref, scr, sels):
    for h, sel in enumerate(sels):
        scr[h] = jnp.where(sel, ref[...], 0.0).astype(MXU_DTYPE)


def _sb_fwd(proj, nb, s, name, carry=None):
    nblk = s // ATT

    def body(q_ref, k_ref, v_ref, o_ref, km, vm):
        sels = _head_masks()
        _split_heads(k_ref, km, sels)
        _split_heads(v_ref, vm, sels)
        rows = _iota2((QT, ATT), 0)
        cols = _iota2((QT, ATT), 1)
        later = _triangle(lambda r, c: r > c)

        def qblock(i, _):
            qi = (q_ref[_qrows(i), :] * 0.125).astype(MXU_DTYPE)

            def kblock(t, state):
                carries, acc = state
                j = _keys_upto(i) - 1 - t
                strict = (cols + (j * ATT - i * QT)) < rows
                out = []
                for h in range(2):
                    z = _dot_nt(qi, km[h, _rows(j), :])
                    lb, lr, _ = _log_sigmoid_parts(z)
                    lr = jnp.where(strict, lr, 0.0)
                    tail = _split_dot(lr, later, 2) + carries[h]
                    a = jnp.where(strict, jnp.exp(lb + tail), 0.0)
                    acc = acc + _dot(a, vm[h, _rows(j), :])
                    out.append(carries[h] + jnp.sum(lr, axis=-1, keepdims=True))
                return tuple(out), acc

            init = ((jnp.zeros((QT, 1), F32),) * 2, jnp.zeros((QT, BLK), F32))
            _, acc = lax.fori_loop(0, _keys_upto(i), kblock, init)
            o_ref[_qrows(i), :] = acc
            return 0

        lax.fori_loop(0, s // QT, qblock, 0)

    (o,), extra = _host_call(
        body, carry, name=name, grid=(nb, 2), in_specs=[_col_spec(s, 0), _col_spec(s, 2), _col_spec(s, 4)],
        out_specs=[_col_spec(s, 0)], out_shape=[jax.ShapeDtypeStruct((nb * s, 2 * BLK), F32)],
        scratch_shapes=[pltpu.VMEM((2, s, BLK), MXU_DTYPE)] * 2, operands=(proj, proj, proj))
    return o, extra


def _sb_bwd(proj, dmixed, nb, s, name, carry=None):
    nblk = s // ATT

    def body(q_ref, k_ref, v_ref, do_ref, dq_ref, dk_ref, dv_ref, km, vm, a_scr, dl_scr, beta_scr):
        sels = _head_masks()
        _split_heads(k_ref, km, sels)
        _split_heads(v_ref, vm, sels)
        rows = _iota2((QT, ATT), 0)
        cols = _iota2((QT, ATT), 1)
        later = _triangle(lambda r, c: r > c)
        earlier = _triangle(lambda r, c: r < c)
        dk_ref[...] = jnp.zeros_like(dk_ref)
        dv_ref[...] = jnp.zeros_like(dv_ref)

        def qblock(i, _):
            qi = (q_ref[_qrows(i), :] * 0.125).astype(MXU_DTYPE)
            doi = do_ref[_qrows(i), :].astype(MXU_DTYPE)
            qm = [jnp.where(sel, qi, 0.0) for sel in sels]
            dom = [jnp.where(sel, doi, 0.0) for sel in sels]

            def first(t, carries):
                j = _keys_upto(i) - 1 - t
                strict = (cols + (j * ATT - i * QT)) < rows
                out = []
                for h in range(2):
                    z = _dot_nt(qi, km[h, _rows(j), :])
                    lb, lr, e = _log_sigmoid_parts(z)
                    lr = jnp.where(strict, lr, 0.0)
                    tail = _split_dot(lr, later, 2) + carries[h]
                    a = jnp.where(strict, jnp.exp(lb + tail), 0.0)
                    a_scr[h, j] = a
                    dl_scr[h, j] = a * _dot_nt(doi, vm[h, _rows(j), :])
                    beta_scr[h, j] = jnp.exp(lb)
                    out.append(carries[h] + jnp.sum(lr, axis=-1, keepdims=True))
                return tuple(out)

            lax.fori_loop(0, _keys_upto(i), first, (jnp.zeros((QT, 1), F32),) * 2)

            def second(j, state):
                csums, dq = state
                strict = (cols + (j * ATT - i * QT)) < rows
                out = []
                for h in range(2):
                    dl = dl_scr[h, j]
                    beta = beta_scr[h, j]
                    before = _split_dot(dl, earlier, 2) + csums[h]
                    dz = jnp.where(strict, dl * (1.0 - beta) - beta * before, 0.0).astype(MXU_DTYPE)
                    dq = dq + _dot(dz, km[h, _rows(j), :])
                    dk_ref[_rows(j), :] += _dot_tn(dz, qm[h])
                    dv_ref[_rows(j), :] += _dot_tn(a_scr[h, j], dom[h])
                    out.append(csums[h] + jnp.sum(dl, axis=-1, keepdims=True))
                return tuple(out), dq

            init = ((jnp.zeros((QT, 1), F32),) * 2, jnp.zeros((QT, BLK), F32))
            _, dq = lax.fori_loop(0, _keys_upto(i), second, init)
            dq_ref[_qrows(i), :] = dq * 0.125
            return 0

        lax.fori_loop(0, s // QT, qblock, 0)

    out = _col_spec(s, 0)
    return _host_call(
        body, carry, name=name, grid=(nb, 2),
        in_specs=[_col_spec(s, 0), _col_spec(s, 2), _col_spec(s, 4), out], out_specs=[out] * 3,
        out_shape=[jax.ShapeDtypeStruct((nb * s, 2 * BLK), F32)] * 3,
        scratch_shapes=[pltpu.VMEM((2, s, BLK), MXU_DTYPE)] * 2 + [pltpu.VMEM((2, nblk, QT, ATT), F32)] * 3,
        operands=(proj, proj, proj, dmixed))


def _pair_spec(s, width):
    return pl.BlockSpec((None, 2, s, width), lambda b, p: (b, p, 0, 0))


def _fox_fwd(proj, ccol, crow, nb, s, name, carry=None):
    nblk = s // ATT

    def body(q_ref, k_ref, v_ref, cc_ref, cr_ref, o_ref, lse_ref, km, vm):
        sels = _head_masks()
        _split_heads(k_ref, km, sels)
        _split_heads(v_ref, vm, sels)
        rows = _iota2((QT, ATT), 0)
        cols = _iota2((QT, ATT), 1)

        def qblock(i, _):
            qi = (q_ref[_qrows(i), :] * 0.125).astype(MXU_DTYPE)
            ci = [cc_ref[h, _qrows(i), :] for h in range(2)]

            def kblock(j, state):
                ms, ls, acc = state
                causal = (cols + (j * ATT - i * QT)) <= rows
                new_m, new_l, scales, parts = [], [], [], []
                for h in range(2):
                    z = _dot_nt(qi, km[h, _rows(j), :]) + (ci[h] - cr_ref[h, j][0:1, :])
                    z = jnp.where(causal, z, NEG)
                    m_new = jnp.maximum(ms[h], jnp.max(z, axis=-1, keepdims=True))
                    p = jnp.exp(z - m_new)
                    scale = jnp.exp(ms[h] - m_new)
                    new_m.append(m_new)
                    new_l.append(scale * ls[h] + jnp.sum(p, axis=-1, keepdims=True))
                    scales.append(scale)
                    parts.append(_dot(p, vm[h, _rows(j), :]))
                acc = jnp.where(sels[0], scales[0], scales[1]) * acc + parts[0] + parts[1]
                return tuple(new_m), tuple(new_l), acc

            init = ((jnp.full((QT, 1), NEG, F32),) * 2, (jnp.zeros((QT, 1), F32),) * 2, jnp.zeros((QT, BLK), F32))
            ms, ls, acc = lax.fori_loop(0, _keys_upto(i), kblock, init)
            o_ref[_qrows(i), :] = acc / jnp.where(sels[0], ls[0], ls[1])
            for h in range(2):
                lse_ref[h, _qrows(i), :] = jnp.broadcast_to(ms[h] + jnp.log(ls[h]), (QT, ATT))
            return 0

        lax.fori_loop(0, s // QT, qblock, 0)

    crow_spec = pl.BlockSpec((None, 2, nblk, 8, ATT), lambda b, p: (b, p, 0, 0, 0))
    return _host_call(
        body, carry, name=name, grid=(nb, 2),
        in_specs=[_col_spec(s, 12), _col_spec(s, 14), _col_spec(s, 16), _pair_spec(s, ATT), crow_spec],
        out_specs=[_col_spec(s, 0), _pair_spec(s, ATT)],
        out_shape=[jax.ShapeDtypeStruct((nb * s, 2 * BLK), F32), jax.ShapeDtypeStruct((nb, N_HEADS, s, ATT), F32)],
        scratch_shapes=[pltpu.VMEM((2, s, BLK), MXU_DTYPE)] * 2, operands=(proj, proj, proj, ccol, crow))


def _fox_bwd(proj, dmixed, lse, ccol, crow, nb, s, name, carry=None):
    nblk = s // ATT

    def body(q_ref, k_ref, v_ref, do_ref, lse_ref, cc_ref, cr_ref, dq_ref, dk_ref, dv_ref, dc_ref, km, vm, p_scr, dp_scr):
        sels = _head_masks()
        _split_heads(k_ref, km, sels)
        _split_heads(v_ref, vm, sels)
        rows = _iota2((QT, ATT), 0)
        cols = _iota2((QT, ATT), 1)
        dk_ref[...] = jnp.zeros_like(dk_ref)
        dv_ref[...] = jnp.zeros_like(dv_ref)
        dc_ref[...] = jnp.zeros_like(dc_ref)

        def qblock(i, _):
            qi = (q_ref[_qrows(i), :] * 0.125).astype(MXU_DTYPE)
            doi = do_ref[_qrows(i), :].astype(MXU_DTYPE)
            qm = [jnp.where(sel, qi, 0.0) for sel in sels]
            dom = [jnp.where(sel, doi, 0.0) for sel in sels]
            ci = [cc_ref[h, _qrows(i), :] for h in range(2)]
            lsei = [lse_ref[h, _qrows(i), :] for h in range(2)]

            def probs(j, h):
                z = _dot_nt(qi, km[h, _rows(j), :]) + (ci[h] - cr_ref[h, j][0:1, :])
                p = jnp.where((cols + (j * ATT - i * QT)) <= rows, jnp.exp(z - lsei[h]), 0.0)
                return p, _dot_nt(doi, vm[h, _rows(j), :])

            def row_term(j, accs):
                out = []
                for h in range(2):
                    p, dp = probs(j, h)
                    p_scr[h, j] = p
                    dp_scr[h, j] = dp
                    out.append(accs[h] + jnp.sum(p * dp, axis=-1, keepdims=True))
                return tuple(out)

            di = lax.fori_loop(0, _keys_upto(i), row_term, (jnp.zeros((QT, 1), F32),) * 2)

            def kblock(j, dq):
                for h in range(2):
                    p = p_scr[h, j]
                    ds = p * (dp_scr[h, j] - di[h])
                    dc_ref[h, j] += jnp.broadcast_to(jnp.sum(ds, axis=0, keepdims=True), (8, ATT))
                    ds = ds.astype(MXU_DTYPE)
                    dk_ref[_rows(j), :] += _dot_tn(ds, qm[h])
                    dv_ref[_rows(j), :] += _dot_tn(p, dom[h])
                    dq = dq + _dot(ds, km[h, _rows(j), :])
                return dq

            dq = lax.fori_loop(0, _keys_upto(i), kblock, jnp.zeros((QT, BLK), F32))
            dq_ref[_qrows(i), :] = dq * 0.125
            return 0

        lax.fori_loop(0, s // QT, qblock, 0)

    crow_spec = pl.BlockSpec((None, 2, nblk, 8, ATT), lambda b, p: (b, p, 0, 0, 0))
    wide, cols_out = _pair_spec(s, ATT), _col_spec(s, 0)
    return _host_call(
        body, carry, name=name, grid=(nb, 2),
        in_specs=[_col_spec(s, 12), _col_spec(s, 14), _col_spec(s, 16), _col_spec(s, 4), wide, wide, crow_spec],
        out_specs=[cols_out, cols_out, cols_out, crow_spec],
        out_shape=[jax.ShapeDtypeStruct((nb * s, 2 * BLK), F32)] * 3 + [jax.ShapeDtypeStruct((nb, N_HEADS, nblk, 8, ATT), F32)],
        scratch_shapes=[pltpu.VMEM((2, s, BLK), MXU_DTYPE)] * 2 + [pltpu.VMEM((2, nblk, QT, ATT), F32)] * 2,
        operands=(proj, proj, proj, dmixed, lse, ccol, crow))


def _fox_gates_fwd(proj, f_bias, nb, s, name):
    chunk = 256

    def body(f_ref, b_ref, c_ref):
        lower = (_iota2((chunk, chunk), 0) >= _iota2((chunk, chunk), 1)).astype(MXU_DTYPE)
        carry = jnp.zeros((1, BLK), F32)
        for n in range(s // chunk):
            rows = pl.ds(n * chunk, chunk)
            lf, _, _ = _log_sigmoid_parts(f_ref[rows, :] + b_ref[0:1, :])
            c = _split_dot_lhs(lower, lf, 3) + carry
            c_ref[rows, :] = c
            carry = c[chunk - 1:chunk, :]

    return pl.pallas_call(
        body, name=name, grid=(nb,),
        in_specs=[pl.BlockSpec((s, BLK), lambda b: (b, (PROJ_PAD - BLK) // BLK)), pl.BlockSpec((8, BLK), lambda b: (0, 0))],
        out_specs=pl.BlockSpec((s, BLK), lambda b: (b, 0)),
        out_shape=jax.ShapeDtypeStruct((nb * s, BLK), F32), compiler_params=_cparams("parallel"),
    )(proj, f_bias)


def _fox_gates_bwd(dc, proj, f_bias, nb, s, name):
    chunk = 256

    def body(dc_ref, f_ref, b_ref, df_ref, db_ref):
        upper = (_iota2((chunk, chunk), 0) <= _iota2((chunk, chunk), 1)).astype(MXU_DTYPE)
        carry = jnp.zeros((1, BLK), F32)
        total = jnp.zeros((1, BLK), F32)
        for n in reversed(range(s // chunk)):
            rows = pl.ds(n * chunk, chunk)
            dlf = _split_dot_lhs(upper, dc_ref[rows, :], 3) + carry
            carry = dlf[0:1, :]
            pre = f_ref[rows, :] + b_ref[0:1, :]
            e = jnp.exp(-jnp.abs(pre))
            df = dlf * (jnp.where(pre >= 0.0, e, 1.0) / (1.0 + e))
            df_ref[rows, :] = df
            total = total + jnp.sum(df, axis=0, keepdims=True)

        @pl.when(pl.program_id(0) == 0)
        def _():
            db_ref[...] = jnp.zeros_like(db_ref)

        db_ref[0:1, :] += total

    return pl.pallas_call(
        body, name=name, grid=(nb,),
        in_specs=[pl.BlockSpec((s, BLK), lambda b: (b, 0)), pl.BlockSpec((s, BLK), lambda b: (b, (PROJ_PAD - BLK) // BLK)),
                  pl.BlockSpec((8, BLK), lambda b: (0, 0))],
        out_specs=[pl.BlockSpec((s, BLK), lambda b: (b, 0)), pl.BlockSpec((8, BLK), lambda b: (0, 0))],
        out_shape=[jax.ShapeDtypeStruct((nb * s, BLK), F32), jax.ShapeDtypeStruct((8, BLK), F32)],
        compiler_params=_cparams("arbitrary"),
    )(dc, proj, f_bias)


def _delta_kernel(dmixed, o, nb, s, name):
    def body(do_ref, o_ref, d_ref):
        prod = do_ref[...] * o_ref[...]
        for h, sel in enumerate(_head_masks()):
            d_ref[h] = jnp.broadcast_to(jnp.sum(jnp.where(sel, prod, 0.0), axis=-1, keepdims=True), (s, BLK))

    return pl.pallas_call(
        body, name=name, grid=(nb, 2), in_specs=[_col_spec(s, 2), _col_spec(s, 0)], out_specs=_pair_spec(s, BLK),
        out_shape=jax.ShapeDtypeStruct((nb, N_HEADS, s, BLK), F32), compiler_params=_cparams("parallel", "parallel"),
    )(dmixed, o)


def _t5_bucket_np(dist):
    max_exact = REL_BUCKETS // 2
    nf = np.maximum(dist, 1).astype(np.float32)
    large = max_exact + (np.log(nf / max_exact) / math.log(2048 / max_exact) * (REL_BUCKETS - max_exact)).astype(np.int32)
    large = np.minimum(large, REL_BUCKETS - 1)
    return np.where(dist < max_exact, dist, large)


def _bucket_table():
    qi = np.arange(BLK)[:, None]
    kj = np.arange(2 * BLK)[None, :]
    dist = qi + BLK - kj
    tables = []
    for window, dil in DIL_PATTERNS:
        in_band = (dist >= 0) & (dist <= window // dil)
        tables.append(np.where(in_band, _t5_bucket_np(np.maximum(dist, 0) * dil), -1).astype(np.int32))
    return np.stack(tables)


def _dil_scores(qb, kp, kc, b_ref, h, prev_valid):
    zp = _dot_nt(qb, kp) + b_ref[h, :, 0:BLK]
    zp = jnp.where(prev_valid, zp, NEG)
    zc = _dot_nt(qb, kc) + b_ref[h, :, BLK:2 * BLK]
    return zp, zc


def _residue_rows(b, seg, dil):
    if dil == 1:
        return _rows128(b), _rows128(jnp.maximum(b - 1, 0)), b > 0
    r, n = b // seg, b % seg
    cur = pl.ds(r + dil * n * BLK, BLK, stride=dil)
    prev = pl.ds(r + dil * jnp.maximum(n - 1, 0) * BLK, BLK, stride=dil)
    return cur, prev, n > 0


def _dil_attention_fwd(proj, bias, nb, s, name, carry=None):
    nblk = s // BLK

    def body(q_ref, k_ref, v_ref, b_ref, out_ref, lse_ref, o_scr, l_scr):
        sels = _head_masks()
        for p, (_, dil) in enumerate(DIL_PATTERNS):
            seg = s // dil // BLK

            def block(b, _, p=p, seg=seg, dil=dil):
                cur, prev, has_prev = _residue_rows(b, seg, dil)
                qb = (q_ref[cur, :] * 0.125).astype(MXU_DTYPE)
                kp, kc = k_ref[prev, :].astype(MXU_DTYPE), k_ref[cur, :].astype(MXU_DTYPE)
                vp, vc = v_ref[prev, :].astype(MXU_DTYPE), v_ref[cur, :].astype(MXU_DTYPE)
                acc = jnp.zeros((BLK, BLK), F32)
                for h, sel in enumerate(sels):
                    zp, zc = _dil_scores(qb, jnp.where(sel, kp, 0.0), jnp.where(sel, kc, 0.0), b_ref.at[p], h, has_prev)
                    m = jnp.maximum(jnp.max(zp, axis=-1, keepdims=True), jnp.max(zc, axis=-1, keepdims=True))
                    pp = jnp.exp(zp - m)
                    pc = jnp.exp(zc - m)
                    den = jnp.sum(pp, axis=-1, keepdims=True) + jnp.sum(pc, axis=-1, keepdims=True)
                    acc = acc + (_dot(pp, jnp.where(sel, vp, 0.0)) + _dot(pc, jnp.where(sel, vc, 0.0))) / den
                    l_scr[p, h, cur, :] = jnp.broadcast_to(m + jnp.log(den), (BLK, BLK))
                o_scr[p, cur, :] = acc
                return 0

            lax.fori_loop(0, nblk, block, 0, unroll=4)

        weights, dens = [], []
        for h in range(2):
            m = jnp.maximum(jnp.maximum(l_scr[0, h], l_scr[1, h]), l_scr[2, h])
            w = [jnp.exp(l_scr[p, h] - m) for p in range(3)]
            den = w[0] + w[1] + w[2]
            lse_ref[h] = m + jnp.log(den)
            weights.append(w)
            dens.append(den)
        num = sum(jnp.where(sels[0], weights[0][p], weights[1][p]) * o_scr[p] for p in range(3))
        out_ref[...] = num / jnp.where(sels[0], dens[0], dens[1])

    bias_spec = pl.BlockSpec((3, 2, BLK, 2 * BLK), lambda b, p: (0, p, 0, 0))
    return _host_call(
        body, carry, name=name, grid=(nb, 2), in_specs=[_col_spec(s, 6), _col_spec(s, 8), _col_spec(s, 10), bias_spec],
        out_specs=[_col_spec(s, 0), _pair_spec(s, BLK)],
        out_shape=[jax.ShapeDtypeStruct((nb * s, 2 * BLK), F32), jax.ShapeDtypeStruct((nb, N_HEADS, s, BLK), F32)],
        scratch_shapes=[pltpu.VMEM((3, s, BLK), F32), pltpu.VMEM((3, 2, s, BLK), F32)], operands=(proj, proj, proj, bias))


def _dil_attention_bwd(proj, dmixed, lse, delta, bias, nb, s, name, carry=None):
    nblk = s // BLK

    def body(q_ref, k_ref, v_ref, do_ref, lse_ref, dl_ref, b_ref, dq_ref, dk_ref, dv_ref, g_ref):
        sels = _head_masks()
        dq_ref[...] = jnp.zeros_like(dq_ref)
        dk_ref[...] = jnp.zeros_like(dk_ref)
        dv_ref[...] = jnp.zeros_like(dv_ref)
        g_ref[...] = jnp.zeros_like(g_ref)
        for p, (_, dil) in enumerate(DIL_PATTERNS):
            seg = s // dil // BLK

            def block(b, _, p=p, seg=seg, dil=dil):
                cur, prev, has_prev = _residue_rows(b, seg, dil)
                qb = (q_ref[cur, :] * 0.125).astype(MXU_DTYPE)
                dob = do_ref[cur, :].astype(MXU_DTYPE)
                kp, kc = k_ref[prev, :].astype(MXU_DTYPE), k_ref[cur, :].astype(MXU_DTYPE)
                vp, vc = v_ref[prev, :].astype(MXU_DTYPE), v_ref[cur, :].astype(MXU_DTYPE)
                dq = jnp.zeros((BLK, BLK), F32)
                dkp, dkc, dvp, dvc = dq, dq, dq, dq
                for h, sel in enumerate(sels):
                    kph, kch = jnp.where(sel, kp, 0.0), jnp.where(sel, kc, 0.0)
                    qh, doh = jnp.where(sel, qb, 0.0), jnp.where(sel, dob, 0.0)
                    lse_h = lse_ref[h, cur, :]
                    dlt = dl_ref[h, cur, :]
                    zp, zc = _dil_scores(qb, kph, kch, b_ref.at[p], h, has_prev)
                    pp = jnp.exp(zp - lse_h)
                    pc = jnp.exp(zc - lse_h)
                    dsp = pp * (_dot_nt(dob, jnp.where(sel, vp, 0.0)) - dlt)
                    dsc = pc * (_dot_nt(dob, jnp.where(sel, vc, 0.0)) - dlt)
                    g_ref[h, p, :, 0:BLK] += dsp
                    g_ref[h, p, :, BLK:2 * BLK] += dsc
                    dsp = dsp.astype(MXU_DTYPE)
                    dsc = dsc.astype(MXU_DTYPE)
                    dq = dq + _dot(dsp, kph) + _dot(dsc, kch)
                    dkp, dkc = dkp + _dot_tn(dsp, qh), dkc + _dot_tn(dsc, qh)
                    dvp, dvc = dvp + _dot_tn(pp, doh), dvc + _dot_tn(pc, doh)
                dq_ref[cur, :] += dq * 0.125
                dk_ref[prev, :] += dkp
                dk_ref[cur, :] += dkc
                dv_ref[prev, :] += dvp
                dv_ref[cur, :] += dvc
                return 0

            lax.fori_loop(0, nblk, block, 0, unroll=4)

    bias_spec = pl.BlockSpec((3, 2, BLK, 2 * BLK), lambda b, p: (0, p, 0, 0))
    cols, stats = _col_spec(s, 0), _pair_spec(s, BLK)
    return _host_call(
        body, carry, name=name, grid=(nb, 2),
        in_specs=[_col_spec(s, 6), _col_spec(s, 8), _col_spec(s, 10), _col_spec(s, 2), stats, stats, bias_spec],
        out_specs=[cols, cols, cols, pl.BlockSpec((None, 2, 3, BLK, 2 * BLK), lambda b, p: (b, p, 0, 0, 0))],
        out_shape=[jax.ShapeDtypeStruct((nb * s, 2 * BLK), F32)] * 3 + [jax.ShapeDtypeStruct((nb, N_HEADS, 3, BLK, 2 * BLK), F32)],
        operands=(proj, proj, proj, dmixed, lse, delta, bias))


def _bucket_reduce(gbias, table, name):
    nb = gbias.shape[0]

    def body(g_ref, t_ref, o_ref):
        row = _iota2((8, BLK), 0)
        lane = _iota2((8, BLK), 1)
        gsum = [[sum(g_ref[b, h, p] for b in range(nb)) for p in range(3)] for h in range(N_HEADS)]

        def bucket(k, acc):
            for h in range(N_HEADS):
                tot = sum(jnp.sum(jnp.where(t_ref[p] == k, gsum[h][p], 0.0)) for p in range(3))
                acc = acc + jnp.where((row == h) & (lane == k), tot, 0.0)
            return acc

        o_ref[...] = lax.fori_loop(0, REL_BUCKETS, bucket, jnp.zeros((8, BLK), F32))

    vm = pl.BlockSpec(memory_space=pltpu.VMEM)
    return pl.pallas_call(
        body, name=name, in_specs=[vm, vm], out_specs=vm, out_shape=jax.ShapeDtypeStruct((8, BLK), F32),
        compiler_params=pltpu.CompilerParams(vmem_limit_bytes=VMEM_LIMIT),
    )(gbias, table)


def _place():
    x, y, c = lax.axis_index("x"), lax.axis_index("y"), lax.axis_index("c")
    others = [(1 - x, y), (x, 1 - y), (1 - x, 1 - y)]
    return x, y, c, others


def _remote(src, dst, send_sem, recv_sem, to):
    return pltpu.make_async_remote_copy(src_ref=src, dst_ref=dst, send_sem=send_sem, recv_sem=recv_sem,
                                        device_id=to, device_id_type=MESH)


_HBM = pl.BlockSpec(memory_space=pl.ANY)


class _Exchange:
    def __init__(self, operands, out_shape, n_copies, copies, aliases=None):
        self.operands, self.out_shape, self.n_copies, self.copies = list(operands), list(out_shape), n_copies, copies
        self.aliases = dict(aliases or {})

    def sem_shapes(self):
        return [pltpu.SemaphoreType.DMA((self.n_copies,)), pltpu.SemaphoreType.DMA((self.n_copies,))]


def _start_all(sends):
    for cp in sends:
        cp.start()


def _wait_all(sends, arrivals):
    for cp in arrivals:
        cp.wait_recv()
    for cp in sends:
        cp.wait_send()


def _run_exchange(ex, name):
    ni = len(ex.operands)

    def body(*refs):
        sends, arrivals = ex.copies(refs[:ni], refs[ni:-2], refs[-2], refs[-1])
        _start_all(sends)
        _wait_all(sends, arrivals)

    return list(pl.pallas_call(
        body, name=name, in_specs=[_HBM] * ni, out_specs=[_HBM] * len(ex.out_shape), out_shape=ex.out_shape,
        scratch_shapes=ex.sem_shapes(), input_output_aliases=ex.aliases)(*ex.operands))


def _host_call(body, carry, *, name, grid, in_specs, out_specs, out_shape, operands, scratch_shapes=()):
    in_specs, out_specs, out_shape, scratch_shapes = list(in_specs), list(out_specs), list(out_shape), list(scratch_shapes)
    if carry is None:
        res = pl.pallas_call(body, name=name, grid=grid, in_specs=in_specs, out_specs=out_specs, out_shape=out_shape,
                             scratch_shapes=scratch_shapes, compiler_params=_cparams(*["parallel"] * len(grid)))(*operands)
        return list(res), []
    n_in, n_out, n_scr, c_in, c_out = len(in_specs), len(out_specs), len(scratch_shapes), len(carry.operands), len(carry.out_shape)
    steps = math.prod(grid)

    def wrapped(*refs):
        ins, refs = refs[:n_in], refs[n_in:]
        c_ins, refs = refs[:c_in], refs[c_in:]
        outs, refs = refs[:n_out], refs[n_out:]
        c_outs, refs = refs[:c_out], refs[c_out:]
        scr, (send_sems, recv_sems) = refs[:n_scr], refs[n_scr:]
        step = 0
        for d, size in enumerate(grid):
            step = step * size + pl.program_id(d)

        @pl.when(step == 0)
        def _():
            _start_all(carry.copies(c_ins, c_outs, send_sems, recv_sems)[0])

        body(*ins, *outs, *scr)

        @pl.when(step == steps - 1)
        def _():
            _wait_all(*carry.copies(c_ins, c_outs, send_sems, recv_sems))

    res = pl.pallas_call(
        wrapped, name=name, grid=grid, in_specs=in_specs + [_HBM] * c_in, out_specs=out_specs + [_HBM] * c_out,
        out_shape=out_shape + carry.out_shape, scratch_shapes=scratch_shapes + carry.sem_shapes(),
        input_output_aliases={n_in + i: n_out + j for i, j in carry.aliases.items()},
        compiler_params=_cparams(*["arbitrary"] * len(grid)))(*operands, *carry.operands)
    return list(res[:n_out]), list(res[n_out:])


def _half(which, rows):
    h = rows // 2
    return pl.ds(pl.multiple_of(which * h, 16), h)


def _like(arrays, shape_of=lambda t: t.shape):
    return [jax.ShapeDtypeStruct(shape_of(t), t.dtype) for t in arrays]


def _gather_ici(shards, layer):
    n = len(shards)

    def copies(ins, outs, send_sems, recv_sems, base=0):
        x, y, c, others = _place()
        me = 2 * x + y
        sends, arrivals = [], []
        for a in range(n):
            rows = _half(c, shards[a].shape[1])
            for k, (ox, oy) in enumerate(others):
                sems = (send_sems.at[base + 3 * a + k], recv_sems.at[base + 3 * a + k],(ox, oy, c))
                sends.append(_remote(ins[a].at[layer, rows], outs[a].at[me, rows], *sems))
                landed = outs[a].at[2 * ox + oy, rows]
                arrivals.append(_remote(landed, landed, *sems))
        return sends, arrivals

    return _Exchange(shards, _like(shards, lambda t: (N_CHIPS,) + t.shape[1:]), 3 * n, copies)


def _gather_d2d(gathered):
    n = len(gathered)

    def copies(ins, outs, send_sems, recv_sems, base=0):
        x, y, c, others = _place()
        sends, arrivals = [], []
        for a in range(n):
            r = gathered[a].shape[1]
            for k, (ox, oy) in enumerate(others):
                sems = (send_sems.at[base + 3 * a + k], recv_sems.at[base + 3 * a + k],(x, y, 1 - c))
                mine, theirs = outs[a].at[2 * ox + oy, _half(c, r)], outs[a].at[2 * ox + oy, _half(1 - c, r)]
                sends.append(_remote(mine, mine, *sems))
                arrivals.append(_remote(theirs, theirs, *sems))
        return sends, arrivals

    return _Exchange(gathered, _like(gathered), 3 * n, copies, aliases={a: a for a in range(n)})


def _swap_halves(g):
    n = len(g)

    def copies(ins, outs, send_sems, recv_sems, base=0):
        x, y, c, _ = _place()
        sends, arrivals = [], []
        for a in range(n):
            sems = (send_sems.at[base + a], recv_sems.at[base + a], (x, y, 1 - c))
            sends.append(_remote(ins[a].at[:, _half(1 - c, g[a].shape[1])], outs[a], *sems))
            arrivals.append(_remote(outs[a], outs[a], *sems))
        return sends, arrivals

    return _Exchange(g, _like(g, lambda t: (t.shape[0], t.shape[1] // 2, t.shape[2])), n, copies)


def _scatter_shards(ps):
    n = len(ps)

    def copies(ins, outs, send_sems, recv_sems, base=0):
        x, y, c, others = _place()
        me = 2 * x + y
        sends, arrivals = [], []
        for a in range(n):
            for k, (ox, oy) in enumerate(others):
                sems = (send_sems.at[base + 3 * a + k], recv_sems.at[base + 3 * a + k],(ox, oy, c))
                sends.append(_remote(ins[a].at[2 * ox + oy], outs[a].at[me], *sems))
                slot = outs[a].at[2 * ox + oy]
                arrivals.append(_remote(slot, slot, *sems))
        return sends, arrivals

    return _Exchange(ps, _like(ps), 3 * n, copies)


def _share_halves(mine):
    n = len(mine)

    def copies(ins, outs, send_sems, recv_sems, base=0):
        x, y, c, _ = _place()
        sends, arrivals = [], []
        for a in range(n):
            sems = (send_sems.at[base + a], recv_sems.at[base + a], (x, y, 1 - c))
            sends.append(_remote(ins[a], outs[a], *sems))
            arrivals.append(_remote(outs[a], outs[a], *sems))
        return sends, arrivals

    return _Exchange(mine, _like(mine), n, copies)


def _row_tile(r):
    for cand in (256, 352, 128):
        if r % cand == 0:
            return cand
    return r


def _pair_sum(g, other, core, name):
    ns, h, w = other.shape
    tr = _row_tile(h)
    per_half = h // tr

    def body(core_ref, g_ref, o_ref, out_ref):
        out_ref[...] = (g_ref[...] + o_ref[...]).astype(out_ref.dtype)

    blk = pl.BlockSpec((None, tr, w), lambda k, i, core_ref: (k, i, 0))
    grid_spec = pltpu.PrefetchScalarGridSpec(
        num_scalar_prefetch=1, grid=(ns, per_half),
        in_specs=[pl.BlockSpec((None, tr, w), lambda k, i, core_ref: (k, core_ref[0] * per_half + i, 0)), blk], out_specs=blk)
    return pl.pallas_call(
        body, name=name, grid_spec=grid_spec, out_shape=jax.ShapeDtypeStruct((ns, h, w), MXU_DTYPE),
        compiler_params=_cparams("parallel", "parallel"),
    )(core.reshape(1).astype(jnp.int32), g, other)


def _chip_sum(q, p, chip, name):
    ns, r, w = q.shape
    tr = _row_tile(r)

    def body(chip_ref, q_ref, own_ref, out_ref):
        me = chip_ref[0]
        own = own_ref[...].astype(F32)
        terms = [jnp.where(me == k, own, q_ref[k].astype(F32)) for k in range(ns)]
        out_ref[...] = ((terms[0] + terms[1]) + terms[2]) + terms[3]

    grid_spec = pltpu.PrefetchScalarGridSpec(
        num_scalar_prefetch=1, grid=(r // tr,),
        in_specs=[pl.BlockSpec((ns, tr, w), lambda i, chip_ref: (0, i, 0)),
                  pl.BlockSpec((None, tr, w), lambda i, chip_ref: (chip_ref[0], i, 0))],
        out_specs=pl.BlockSpec((tr, w), lambda i, chip_ref: (i, 0)))
    return pl.pallas_call(
        body, name=name, grid_spec=grid_spec, out_shape=jax.ShapeDtypeStruct((r, w), F32),
        compiler_params=_cparams("parallel"),
    )(chip.reshape(1).astype(jnp.int32), q, p)


def _merge(exchanges):
    if len(exchanges) <= 1:
        return exchanges[0] if exchanges else None
    operands, out_shape, aliases, spans, n = [], [], {}, [], 0
    for ex in exchanges:
        spans.append((len(operands), len(out_shape), n))
        aliases.update({len(operands) + i: len(out_shape) + j for i, j in ex.aliases.items()})
        operands += ex.operands
        out_shape += ex.out_shape
        n += ex.n_copies

    def copies(ins, outs, send_sems, recv_sems, base=0):
        sends, arrivals = [], []
        for ex, (i0, o0, s0) in zip(exchanges, spans):
            s, a = ex.copies(ins[i0:i0 + len(ex.operands)], outs[o0:o0 + len(ex.out_shape)], send_sems, recv_sems, base + s0)
            sends += s
            arrivals += a
        return sends, arrivals

    return _Exchange(operands, out_shape, n, copies, aliases)


def _take(hooks, host):
    stages = (hooks or {}).pop(host, [])
    exchanges = [make() for make, _ in stages]

    def finish(results):
        for (_, done), ex in zip(stages, exchanges):
            done(results[:len(ex.out_shape)])
            results = results[len(ex.out_shape):]

    return _merge(exchanges), finish


def _hook(hooks, host, make, done):
    hooks.setdefault(host, []).append((make, done))


class _WeightPrefetch:
    def __init__(self, names, shards, layer, chip):
        self.names, self.shards, self.layer, self.chip, self.result = names, [shards[n] for n in names], layer, chip, None

    def first(self):
        return _gather_ici(self.shards, self.layer)

    def got_first(self, arrived):
        self.arrived = arrived

    def second(self):
        return _gather_d2d(self.arrived)

    def got_second(self, gathered):
        self.result = {name: lax.dynamic_update_index_in_dim(got, own[self.layer], self.chip, 0)
                       for name, got, own in zip(self.names, gathered, self.shards)}

    def ride(self, hooks, first_host, second_host):
        _hook(hooks, first_host, self.first, self.got_first)
        _hook(hooks, second_host, self.second, self.got_second)

    def run(self, tag):
        self.got_first(_run_exchange(self.first(), f"gather_ici_{tag}"))
        self.got_second(_run_exchange(self.second(), f"gather_d2d_{tag}"))


class _GradReduce:
    def __init__(self, g, chip, core, tag):
        self.names, self.g, self.chip, self.core, self.tag, self.result = list(g), list(g.values()), chip, core, tag, None

    def swap(self):
        return _swap_halves(self.g)

    def got_swap(self, theirs):
        self.pair = [_pair_sum(g, t, self.core, f"pair_sum_{n}_{self.tag}") for n, g, t in zip(self.names, self.g, theirs)]

    def scatter(self):
        return _scatter_shards(self.pair)

    def got_scatter(self, q):
        self.mine = [_chip_sum(qa, pa, self.chip, f"chip_sum_{n}_{self.tag}") for n, qa, pa in zip(self.names, q, self.pair)]

    def share(self):
        return _share_halves(self.mine)

    def got_share(self, theirs):
        self.result = {n: jnp.where(self.core == 0, jnp.concatenate([a, b]), jnp.concatenate([b, a]))
                       for n, a, b in zip(self.names, self.mine, theirs)}

    def ride(self, hooks, swap_host, scatter_host, share_host):
        _hook(hooks, swap_host, self.swap, self.got_swap)
        _hook(hooks, scatter_host, self.scatter, self.got_scatter)
        _hook(hooks, share_host, self.share, self.got_share)

    def run(self):
        self.got_swap(_run_exchange(self.swap(), f"swap_halves_{self.tag}"))
        self.got_scatter(_run_exchange(self.scatter(), f"scatter_shards_{self.tag}"))
        self.got_share(_run_exchange(self.share(), f"share_halves_{self.tag}"))


class _LayerWeights:
    def __init__(self, gathered):
        self.gathered, self.made = gathered, {}

    def __getitem__(self, key):
        if key not in self.made:
            cols = lambda t: jnp.swapaxes(t, 0, 1).reshape(t.shape[1], -1)
            rows = lambda t: t.reshape(-1, t.shape[2])
            if key == "w_in":
                made = jnp.pad(cols(self.gathered("w_in")), ((0, 0), (0, PROJ_PAD - PROJ)))
            elif key == "w_gu":
                made = jnp.concatenate([cols(self.gathered("w_gate")), cols(self.gathered("w_up"))], axis=-1)
            else:
                made = rows(self.gathered(key))
            self.made[key] = made
        return self.made[key]


def _gather_small(pk, name):
    rows, w = pk.shape

    def body(pk_ref, all_ref, sum_ref, send_sems, recv_sems):
        x, y, c, _ = _place()
        me = 4 * x + 2 * y + c
        all_ref[me] = pk_ref[...]
        flips = [(fx, fy, fc) for fx in (0, 1) for fy in (0, 1) for fc in (0, 1)][1:]
        peers = [(x ^ fx, y ^ fy, c ^ fc) for fx, fy, fc in flips]
        sends = [_remote(pk_ref, all_ref.at[me], send_sems.at[k], recv_sems.at[k], peer) for k, peer in enumerate(peers)]
        for cp in sends:
            cp.start()
        for k, (px, py, pc) in enumerate(peers):
            slot = all_ref.at[4 * px + 2 * py + pc]
            _remote(slot, slot, send_sems.at[k], recv_sems.at[k], (px, py, pc)).wait_recv()
        for cp in sends:
            cp.wait_send()
        total = all_ref[0]
        for d in range(1, N_DEV):
            total = total + all_ref[d]
        sum_ref[...] = total

    vm = pl.BlockSpec(memory_space=pltpu.VMEM)
    return pl.pallas_call(
        body, name=name, in_specs=[vm], out_specs=[vm, vm],
        out_shape=[jax.ShapeDtypeStruct((N_DEV, rows, w), F32), jax.ShapeDtypeStruct((rows, w), F32)],
        scratch_shapes=[pltpu.SemaphoreType.DMA((7,)), pltpu.SemaphoreType.DMA((7,))],
    )(pk)


def _row_layout(c, nb, s):
    ch = jnp.swapaxes(c[:, :N_HEADS].reshape(nb, s, N_HEADS), 1, 2)
    ccol = jnp.broadcast_to(ch[..., None], (nb, N_HEADS, s, ATT))
    crow = jnp.broadcast_to(ch.reshape(nb, N_HEADS, s // ATT, 1, ATT), (nb, N_HEADS, s // ATT, 8, ATT))
    return ccol, crow


def _dil_bias(rel_bias, name):
    def body(rel_ref, t_ref, o_ref):
        for p in range(len(DIL_PATTERNS)):
            table = t_ref[p]

            def bucket(k, accs, table=table):
                return tuple(jnp.where(table == k, rel_ref[k, h], acc) for h, acc in enumerate(accs))

            accs = lax.fori_loop(0, REL_BUCKETS, bucket, tuple(jnp.full((BLK, 2 * BLK), NEG, F32) for _ in range(N_HEADS)))
            for h in range(N_HEADS):
                o_ref[p, h] = accs[h]

    vm = pl.BlockSpec(memory_space=pltpu.VMEM)
    return pl.pallas_call(
        body, name=name, in_specs=[pl.BlockSpec(memory_space=pltpu.SMEM), vm], out_specs=vm,
        out_shape=jax.ShapeDtypeStruct((len(DIL_PATTERNS), N_HEADS, BLK, 2 * BLK), F32),
        compiler_params=pltpu.CompilerParams(vmem_limit_bytes=VMEM_LIMIT),
    )(rel_bias, jnp.asarray(_bucket_table()))


def _layer_forward(x, x_b, wts, small, nb, s, tag, hooks=None):
    proj = _matmul(x_b, wts["w_in"], "proj", tag)

    carry, finish = _take(hooks, "sb_fwd")
    o_sb, carried = _sb_fwd(proj, nb, s, f"sb_fwd_{tag}", carry)
    finish(carried)

    bias = _dil_bias(small["rel_bias"], f"dil_bias_{tag}")
    carry, finish = _take(hooks, "dil_fwd")
    (o_dl, lse_dl), carried = _dil_attention_fwd(proj, bias, nb, s, f"dil_fwd_{tag}", carry)
    finish(carried)

    fb = jnp.zeros((8, BLK), F32).at[0, :N_HEADS].set(small["f_bias"])
    csum = _fox_gates_fwd(proj, fb, nb, s, f"fox_gates_{tag}")
    ccol, crow = _row_layout(csum, nb, s)
    carry, finish = _take(hooks, "fox_fwd")
    (o_fx, lse_fx), carried = _fox_fwd(proj, ccol, crow, nb, s, f"fox_fwd_{tag}", carry)
    finish(carried)

    cw = jnp.zeros((8, CONV_W), F32).at[:3].set(small["conv_w"])
    o_cv = _conv_fwd(proj, cw, nb, s, f"conv_fwd_{tag}")

    mixed = jnp.concatenate([o_sb, o_dl, o_fx, o_cv], axis=-1).astype(MXU_DTYPE)
    pre1, x1, x1_b = _matmul_post_norm(mixed, wts["w_out"], x, small["ln1_g"], small["ln1_b"], f"out_proj_ln1_{tag}")
    carry, finish = _take(hooks, "ffn_in")
    (gate, up, hid), carried = _ffn_in(x1_b, wts["w_gu"], f"ffn_in_{tag}", carry)
    finish(carried)
    pre2, x2, x2_b = _matmul_post_norm(hid, wts["w_down"], x1, small["ln2_g"], small["ln2_b"], f"ffn_out_ln2_{tag}")
    saved = dict(x_b=x_b, proj=proj, bias=bias, o_dl=o_dl, lse_dl=lse_dl, fb=fb, ccol=ccol, crow=crow, o_fx=o_fx,
                 lse_fx=lse_fx, cw=cw, mixed=mixed, pre1=pre1, x1_b=x1_b, gate=gate, up=up, hid=hid, pre2=pre2)
    return (x2, x2_b), saved


def _layer_backward(dx2, sv, wts, small, nb, s, tag, hooks=None, ffn_grads_ready=None):
    t = nb * s
    dpre2, dpre2_b, dgb2 = _ln_bwd(dx2, sv["pre2"], small["ln2_g"], f"ln2_bwd_{tag}")
    carry, finish = _take(hooks, "ffn_out_dx")
    (dgate, dup), carried = _ffn_out_dx(dpre2_b, wts["w_down"], sv["gate"], sv["up"], f"ffn_out_dx_{tag}", carry)
    finish(carried)
    dw_down = _matmul(sv["hid"], dpre2_b, "ffn_out_dw", tag, trans_a=True)
    dx1 = _ffn_in_dx(dgate, dup, wts["w_gu"], dpre2, f"ffn_in_dx_{tag}")
    x1_b = sv["x1_b"]
    dw_gate = _matmul(x1_b, dgate, "ffn_in_dw", f"{tag}_gate", trans_a=True)
    dw_up = _matmul(x1_b, dup, "ffn_in_dw", f"{tag}_up", trans_a=True)

    dpre1, dpre1_b, dgb1 = _ln_bwd(dx1, sv["pre1"], small["ln1_g"], f"ln1_bwd_{tag}")
    dmixed = _matmul(dpre1_b, wts["w_out"], "out_proj_dx", tag, trans_b=True)
    dw_out = _matmul(sv["mixed"], dpre1_b, "out_proj_dw", tag, trans_a=True)
    if ffn_grads_ready:
        ffn_grads_ready(dict(w_down=dw_down, w_gate=dw_gate, w_up=dw_up, w_out=dw_out))
    proj = sv["proj"]

    carry, finish = _take(hooks, "sb_bwd")
    (dq_sb, dk_sb, dv_sb), carried = _sb_bwd(proj, dmixed, nb, s, f"sb_bwd_{tag}", carry)
    finish(carried)

    delta_dl = _delta_kernel(dmixed, sv["o_dl"], nb, s, f"dil_delta_{tag}")
    carry, finish = _take(hooks, "dil_bwd")
    (dq_dl, dk_dl, dv_dl, gbias), carried = _dil_attention_bwd(proj, dmixed, sv["lse_dl"], delta_dl, sv["bias"], nb, s,
                                                               f"dil_bwd_{tag}", carry)
    finish(carried)
    drel = _bucket_reduce(gbias, jnp.asarray(_bucket_table()), f"rel_bias_grad_{tag}")

    carry, finish = _take(hooks, "fox_bwd")
    (dq_fx, dk_fx, dv_fx, dcol), carried = _fox_bwd(proj, dmixed, sv["lse_fx"], sv["ccol"], sv["crow"], nb, s,
                                                    f"fox_bwd_{tag}", carry)
    finish(carried)
    dcs = -jnp.swapaxes(dcol[:, :, :, 0, :].reshape(nb, N_HEADS, s), 1, 2).reshape(t, N_HEADS)
    dcs = jnp.pad(dcs, ((0, 0), (0, BLK - N_HEADS)))
    dfx, dfb = _fox_gates_bwd(dcs, proj, sv["fb"], nb, s, f"fox_gates_bwd_{tag}")

    dgates, dcw = _conv_bwd(dmixed, proj, sv["cw"], nb, s, f"conv_bwd_{tag}")

    dproj = jnp.concatenate([dq_sb, dk_sb, dv_sb, dq_dl, dk_dl, dv_dl, dq_fx, dk_fx, dv_fx, dgates, dfx],
                            axis=-1).astype(MXU_DTYPE)
    dx = _matmul(dproj, wts["w_in"], "proj_dx", tag, add=dpre1, add_scale=ALPHA, trans_b=True)
    dw_in = _matmul(sv["x_b"], dproj, "proj_dw", tag, trans_a=True)

    grads = dict(w_in=dw_in[:, :PROJ], w_out=dw_out, w_gate=dw_gate, w_up=dw_up, w_down=dw_down,
                 ln1_g=dgb1[0], ln1_b=dgb1[1], ln2_g=dgb2[0], ln2_b=dgb2[1], conv_w=dcw[:3], f_bias=dfb[0, :N_HEADS],
                 rel_bias=drel[:N_HEADS, :REL_BUCKETS].T)
    return dx, grads


class _NoExchanges:
    def forward_hooks(self, layer):
        return None

    def backward_hooks(self, layer):
        return None

    def ffn_grads_ready(self, layer):
        return None

    def layer_done(self, layer, grads):
        pass


def _local_step(x, target, weights_of, small_all, schedule=None):
    schedule = schedule or _NoExchanges()
    nb, s, d = x.shape
    h = x.reshape(nb * s, d)
    h_b = h.astype(MXU_DTYPE)
    saved = []
    for layer in range(DEPTH):
        wts = weights_of(layer)
        (h, h_b), sv = _layer_forward(h, h_b, wts, small_all[layer], nb, s, f"l{layer}", schedule.forward_hooks(layer))
        saved.append((sv, wts))
    dy, lossp = _loss_kernel(h, target.reshape(nb * s, d), "loss")
    grads = [None] * DEPTH
    for layer in reversed(range(DEPTH)):
        sv, wts = saved[layer]
        dy, grads[layer] = _layer_backward(dy, sv, wts, small_all[layer], nb, s, f"l{layer}",
                                           schedule.backward_hooks(layer), schedule.ffn_grads_ready(layer))
        schedule.layer_done(layer, grads[layer])
    return lossp, dy.reshape(nb, s, d), grads


_BIG = ("w_in", "w_out", "w_gate", "w_up", "w_down")
_COL_SHARDED = ("w_in", "w_gate", "w_up")


class _Schedule:
    def __init__(self, shards, chip, core):
        self.chip, self.core, self.reduces = chip, core, [[] for _ in range(DEPTH)]
        first = _WeightPrefetch(["w_in"], shards, 0, chip)
        first.run("l0_w_in")
        rest = _WeightPrefetch(["w_out", "w_gate", "w_up", "w_down"], shards, 0, chip)
        ahead_a = _WeightPrefetch(["w_in", "w_out", "w_down"], shards, 1, chip)
        ahead_b = _WeightPrefetch(["w_gate", "w_up"], shards, 1, chip)
        self.fetches = [[first, rest], [ahead_a, ahead_b]]
        self.forward, self.backward = [{} for _ in range(DEPTH)], [{} for _ in range(DEPTH)]
        rest.ride(self.forward[0], "sb_fwd", "fox_fwd")
        ahead_a.ride(self.forward[0], "dil_fwd", "ffn_in")
        ahead_b.ride(self.forward[0], "fox_fwd", "ffn_in")

    def weights(self, layer):
        def gathered(name):
            return next(f.result[name] for f in self.fetches[layer] if name in f.names)
        return _LayerWeights(gathered)

    def forward_hooks(self, layer):
        return self.forward[layer]

    def backward_hooks(self, layer):
        return self.backward[layer]

    def _reduce(self, layer, grads, tag):
        red = _GradReduce({name: _by_chip(name, g) for name, g in grads.items()}, self.chip, self.core, tag)
        self.reduces[layer].append(red)
        return red

    def ffn_grads_ready(self, layer):
        if layer != 0:
            return None

        def ready(early):
            self._reduce(0, early, "l0_early").ride(self.backward[0], "sb_bwd", "dil_bwd", "fox_bwd")

        return ready

    def layer_done(self, layer, grads):
        if layer == 1:
            self._reduce(1, {name: grads[name] for name in _BIG}, "l1").ride(self.backward[0], "ffn_out_dx", "sb_bwd", "fox_bwd")
        else:
            self._reduce(0, dict(w_in=grads["w_in"]), "l0_w_in").run()

    def reduced(self, layer, name):
        return next(r.result[name] for r in self.reduces[layer] if name in r.names)


def _by_chip(name, g):
    if name in _COL_SHARDED:
        return jnp.swapaxes(g.reshape(g.shape[0], N_CHIPS, -1), 0, 1)
    return g.reshape(N_CHIPS, -1, g.shape[1])


_SMALL_LAYOUT = (("ln1_g", 0), ("ln1_b", 2), ("ln2_g", 4), ("ln2_b", 6), ("conv_w", 8))
_ROW_MISC = 10
_ROW_LOSS = 11


def _pack_small(per_layer, rel_bias, loss=None):
    pk = jnp.zeros((SMALL_ROWS, D_MODEL), F32)
    for name, row in _SMALL_LAYOUT:
        for l in range(DEPTH):
            v = per_layer[l][name].reshape(-1)
            pk = pk.at[row + l, :v.shape[0]].set(v)
    fb = jnp.concatenate([per_layer[l]["f_bias"] for l in range(DEPTH)])
    pk = pk.at[_ROW_MISC, :2 * N_HEADS].set(fb)
    pk = pk.at[_ROW_MISC, BLK:BLK + REL_BUCKETS * N_HEADS].set(rel_bias.reshape(-1))
    if loss is not None:
        pk = pk.at[_ROW_LOSS, 0].set(loss)
    return pk


def _unpack_small(pk, conv_cols):
    out = {}
    for name, row in _SMALL_LAYOUT:
        n = 3 * conv_cols if name == "conv_w" else D_MODEL
        v = pk[row:row + DEPTH, :n]
        out[name] = v.reshape(DEPTH, 3, conv_cols) if name == "conv_w" else v
    out["f_bias"] = pk[_ROW_MISC, :2 * N_HEADS].reshape(DEPTH, N_HEADS)
    out["rel_bias"] = pk[_ROW_MISC, BLK:BLK + REL_BUCKETS * N_HEADS].reshape(REL_BUCKETS, N_HEADS)
    return out


_WEIGHTS = ("w_in", "f_bias", "conv_w", "w_out", "rel_bias", "ln1_g", "ln1_b", "w_gate", "w_up", "w_down", "ln2_g", "ln2_b")


def kernel(x, w_in, f_bias, conv_w, w_out, rel_bias, ln1_g, ln1_b, w_gate, w_up, w_down, ln2_g, ln2_b, loss_target, m_w_in, m_f_bias, m_conv_w, m_w_out, m_rel_bias, m_ln1_g, m_ln1_b, m_w_gate, m_w_up, m_w_down, m_ln2_g, m_ln2_b, v_w_in, v_f_bias, v_conv_w, v_w_out, v_rel_bias, v_ln1_g, v_ln1_b, v_w_gate, v_w_up, v_w_down, v_ln2_g, v_ln2_b):
    w = dict(w_in=w_in, f_bias=f_bias, conv_w=conv_w, w_out=w_out, rel_bias=rel_bias, ln1_g=ln1_g, ln1_b=ln1_b,
             w_gate=w_gate, w_up=w_up, w_down=w_down, ln2_g=ln2_g, ln2_b=ln2_b)
    m = dict(w_in=m_w_in, f_bias=m_f_bias, conv_w=m_conv_w, w_out=m_w_out, rel_bias=m_rel_bias, ln1_g=m_ln1_g,
             ln1_b=m_ln1_b, w_gate=m_w_gate, w_up=m_w_up, w_down=m_w_down, ln2_g=m_ln2_g, ln2_b=m_ln2_b)
    v = dict(w_in=v_w_in, f_bias=v_f_bias, conv_w=v_conv_w, w_out=v_w_out, rel_bias=v_rel_bias, ln1_g=v_ln1_g,
             ln1_b=v_ln1_b, w_gate=v_w_gate, w_up=v_w_up, w_down=v_w_down, ln2_g=v_ln2_g, ln2_b=v_ln2_b)
    chip = 2 * lax.axis_index("x") + lax.axis_index("y")
    core = lax.axis_index("c")
    conv_shard = CONV_W // N_CHIPS

    schedule = _Schedule({name: w[name].astype(MXU_DTYPE) for name in _BIG}, chip, core)
    cw_pk = jnp.zeros((8, D_MODEL), F32).at[0, :DEPTH * 3 * conv_shard].set(conv_w.reshape(-1))
    cw_all, _ = _gather_small(cw_pk, "gather_conv_w")
    cw_chips = cw_all[0::2, 0, :DEPTH * 3 * conv_shard].reshape(N_CHIPS, DEPTH, 3, conv_shard)
    conv_full = jnp.moveaxis(cw_chips, 0, 2).reshape(DEPTH, 3, CONV_W)
    small_all = [dict(f_bias=f_bias[l], conv_w=conv_full[l], rel_bias=rel_bias, ln1_g=ln1_g[l], ln1_b=ln1_b[l],
                      ln2_g=ln2_g[l], ln2_b=ln2_b[l]) for l in range(DEPTH)]

    lossp, grad_x, grads = _local_step(x, loss_target, schedule.weights, small_all, schedule)
    big_g = {name: jnp.stack([schedule.reduced(l, name) for l in range(DEPTH)]) for name in _BIG}

    drel = grads[0]["rel_bias"] + grads[1]["rel_bias"]
    small_pk = _pack_small(grads, drel, lossp[0, 0])
    _, small_sum = _gather_small(small_pk, "gather_small_grads")
    loss = small_sum[_ROW_LOSS, 0]
    small_g = _unpack_small(small_sum, CONV_W)
    small_g["conv_w"] = lax.dynamic_slice_in_dim(small_g["conv_w"], chip * conv_shard, conv_shard, axis=2)

    out_g, out_d, out_m, out_v = dict(small_g), {}, {}, {}
    for name in _BIG:
        out_g[name] = big_g[name]
        out_d[name], out_m[name], out_v[name] = _adamw(w[name], big_g[name], m[name], v[name], f"adamw_{name}")
    as_3d = lambda t: t if t.ndim == 3 else t[None]
    for name in _WEIGHTS:
        if name not in _BIG:
            stepped = _adamw(as_3d(w[name]), as_3d(small_g[name]), as_3d(m[name]), as_3d(v[name]), f"adamw_{name}")
            out_d[name], out_m[name], out_v[name] = (t.reshape(w[name].shape) for t in stepped)

    return (loss, grad_x, *[out_g[n] for n in _WEIGHTS], *[out_d[n] for n in _WEIGHTS],
            *[out_m[n] for n in _WEIGHTS], *[out_v[n] for n in _WEIGHTS])
```

```python
import functools
import math

import numpy as np
import jax
import jax.numpy as jnp
from jax import lax
from jax.experimental import pallas as pl
from jax.experimental.pallas import tpu as pltpu

F32 = jnp.float32
BF16 = jnp.bfloat16
MXU_DTYPE = BF16

D_MODEL = 1024
HEAD_DIM = 64
N_HEADS = 4
BLK = 128
ATT = 256
QT = 512
CONV_W = 256
PROJ = 3076
PROJ_PAD = 3200
D_FF = 2816
DEPTH = 2
ALPHA = (2 * DEPTH) ** 0.25
LN_EPS = 1e-5
NEG = -1e30
DIL_PATTERNS = ((128, 1), (512, 4), (2048, 16))
REL_BUCKETS = 32
N_CHIPS = 4
N_DEV = 8
SMALL_ROWS = 16

ADAM_LR = 0.001
ADAM_B1 = 0.9
ADAM_B2 = 0.999
ADAM_EPS = 1e-08
ADAM_WD = 0.01
ADAM_STEP = 10

VMEM_LIMIT = 56 * 2 ** 20
MESH = pl.DeviceIdType.MESH


def _cparams(*sem):
    return pltpu.CompilerParams(dimension_semantics=tuple(sem), vmem_limit_bytes=VMEM_LIMIT)


def _dot(a, b):
    return jnp.dot(a.astype(MXU_DTYPE), b.astype(MXU_DTYPE), preferred_element_type=F32)


def _dot_nt(a, b):
    return lax.dot_general(a.astype(MXU_DTYPE), b.astype(MXU_DTYPE), (((1,), (1,)), ((), ())),
                           preferred_element_type=F32)


def _dot_tn(a, b):
    return lax.dot_general(a.astype(MXU_DTYPE), b.astype(MXU_DTYPE), (((0,), (0,)), ((), ())),
                           preferred_element_type=F32)


def _split_dot(x, ones, passes):
    acc, rest = None, x
    for p in range(passes):
        piece = rest.astype(MXU_DTYPE)
        part = jnp.dot(piece, ones, preferred_element_type=F32)
        acc = part if acc is None else acc + part
        if p + 1 < passes:
            rest = rest - piece.astype(F32)
    return acc


def _split_dot_lhs(ones, x, passes):
    acc, rest = None, x
    for p in range(passes):
        piece = rest.astype(MXU_DTYPE)
        part = jnp.dot(ones, piece, preferred_element_type=F32)
        acc = part if acc is None else acc + part
        if p + 1 < passes:
            rest = rest - piece.astype(F32)
    return acc


def _iota2(shape, axis):
    return lax.broadcasted_iota(jnp.int32, shape, axis)


_TILES = {"proj": (1024, 640, 1024), "ffn_out_dw": (1408, 1024, 2048),
          "ffn_in_dw": (1024, 1408, 2048), "out_proj_dx": (1024, 1024, 1024),
          "out_proj_dw": (1024, 1024, 2048), "proj_dx": (1024, 512, 3200), "proj_dw": (1024, 640, 2048)}


def _matmul(a, b, kind, tag, *, out_dtype=F32, add=None, add_scale=1.0, trans_a=False, trans_b=False):
    k, m = a.shape if trans_a else a.shape[::-1]
    n = b.shape[0] if trans_b else b.shape[1]
    tm, tn, tk = _TILES[kind]
    tm, tk, name = min(tm, m), min(tk, k), f"{kind}_{tag}"
    assert m % tm == 0 and n % tn == 0 and k % tk == 0, (a.shape, b.shape, tm, tn, tk)
    nk = k // tk

    def body(*refs):
        if add is None:
            a_ref, b_ref, o_ref = refs[:3]
            c_ref, scr = None, refs[3:]
        else:
            a_ref, b_ref, c_ref, o_ref = refs[:4]
            scr = refs[4:]
        dot = _dot_tn if trans_a else _dot_nt if trans_b else _dot
        part = dot(a_ref[...], b_ref[...])

        def finish(acc):
            if c_ref is not None:
                acc = acc + add_scale * c_ref[...]
            o_ref[...] = acc.astype(out_dtype)

        if nk == 1:
            finish(part)
        else:
            acc_ref = scr[0]
            kk = pl.program_id(2)

            @pl.when(kk == 0)
            def _():
                acc_ref[...] = part

            @pl.when(kk > 0)
            def _():
                acc_ref[...] += part

            @pl.when(kk == nk - 1)
            def _():
                finish(acc_ref[...])

    b_spec = pl.BlockSpec((tn, tk), lambda i, j, kk: (j, kk)) if trans_b else pl.BlockSpec((tk, tn), lambda i, j, kk: (kk, j))
    a_spec = pl.BlockSpec((tk, tm), lambda i, j, kk: (kk, i)) if trans_a else pl.BlockSpec((tm, tk), lambda i, j, kk: (i, kk))
    in_specs = [a_spec, b_spec]
    operands = [a, b]
    if add is not None:
        in_specs.append(pl.BlockSpec((tm, tn), lambda i, j, kk: (i, j)))
        operands.append(add)
    return pl.pallas_call(
        body, name=name, grid=(m // tm, n // tn, nk), in_specs=in_specs,
        out_specs=pl.BlockSpec((tm, tn), lambda i, j, kk: (i, j)),
        out_shape=jax.ShapeDtypeStruct((m, n), out_dtype),
        scratch_shapes=[pltpu.VMEM((tm, tn), F32)] if nk > 1 else [],
        compiler_params=_cparams("parallel", "parallel", "arbitrary"),
    )(*operands)


def _matmul_post_norm(a, b, xin, g, beta, name):
    t, k = a.shape
    d = b.shape[1]
    tm = 512

    def body(a_ref, b_ref, x_ref, g_ref, beta_ref, pre_ref, y_ref, yb_ref):
        pre = ALPHA * x_ref[...] + _dot(a_ref[...], b_ref[...])
        xhat, _ = _ln_stats(pre)
        y = xhat * g_ref[...] + beta_ref[...]
        pre_ref[...] = pre
        y_ref[...] = y
        yb_ref[...] = y.astype(yb_ref.dtype)

    row = pl.BlockSpec((tm, d), lambda i: (i, 0))
    vec = pl.BlockSpec((1, d), lambda i: (0, 0))
    return pl.pallas_call(
        body, name=name, grid=(t // tm,),
        in_specs=[pl.BlockSpec((tm, k), lambda i: (i, 0)), pl.BlockSpec((k, d), lambda i: (0, 0)), row, vec, vec],
        out_specs=[row, row, row],
        out_shape=[jax.ShapeDtypeStruct((t, d), F32)] * 2 + [jax.ShapeDtypeStruct((t, d), MXU_DTYPE)],
        compiler_params=_cparams("parallel"),
    )(a, b, xin, g.reshape(1, d), beta.reshape(1, d))


def _ffn_in(x1, w_gu, name, carry=None):
    t, d = x1.shape
    tm, tn = 512, D_FF // 2
    nj = D_FF // tn

    def body(x_ref, wg_ref, wu_ref, gate_ref, up_ref, h_ref):
        xb = x_ref[...].astype(MXU_DTYPE)
        gate = _dot(xb, wg_ref[...])
        up = _dot(xb, wu_ref[...])
        gate_ref[...] = gate
        up_ref[...] = up
        h_ref[...] = (gate * (1.0 / (1.0 + jnp.exp(-gate))) * up).astype(h_ref.dtype)

    out = pl.BlockSpec((tm, tn), lambda i, j: (i, j))
    return _host_call(
        body, carry, name=name, grid=(t // tm, nj),
        in_specs=[pl.BlockSpec((tm, d), lambda i, j: (i, 0)), pl.BlockSpec((d, tn), lambda i, j: (0, j)),
                  pl.BlockSpec((d, tn), lambda i, j: (0, nj + j))],
        out_specs=[out, out, out],
        out_shape=[jax.ShapeDtypeStruct((t, D_FF), F32)] * 2 + [jax.ShapeDtypeStruct((t, D_FF), MXU_DTYPE)],
        operands=(x1, w_gu, w_gu))


def _ffn_out_dx(dy, w_down, gate, up, name, carry=None):
    t, d = dy.shape
    tm, tn = 512, D_FF // 2

    def body(dy_ref, w_ref, gate_ref, up_ref, dg_ref, du_ref):
        dh = _dot_nt(dy_ref[...], w_ref[...])
        gate = gate_ref[...]
        sig = 1.0 / (1.0 + jnp.exp(-gate))
        dg_ref[...] = (dh * up_ref[...] * sig * (1.0 + gate * (1.0 - sig))).astype(dg_ref.dtype)
        du_ref[...] = (dh * gate * sig).astype(du_ref.dtype)

    tile = pl.BlockSpec((tm, tn), lambda i, j: (i, j))
    return _host_call(
        body, carry, name=name, grid=(t // tm, D_FF // tn),
        in_specs=[pl.BlockSpec((tm, d), lambda i, j: (i, 0)), pl.BlockSpec((tn, d), lambda i, j: (j, 0)), tile, tile],
        out_specs=[tile, tile], out_shape=[jax.ShapeDtypeStruct((t, D_FF), MXU_DTYPE)] * 2,
        operands=(dy, w_down, gate, up))


def _ffn_in_dx(dgate, dup, w_gu, add, name):
    t = dgate.shape[0]
    d = w_gu.shape[0]
    tm, tk = 1024, D_FF // 2
    nk = D_FF // tk

    def body(dg_ref, du_ref, wg_ref, wu_ref, add_ref, o_ref, acc_ref):
        kk = pl.program_id(1)
        part = _dot_nt(dg_ref[...], wg_ref[...]) + _dot_nt(du_ref[...], wu_ref[...])

        @pl.when(kk == 0)
        def _():
            acc_ref[...] = part

        @pl.when(kk > 0)
        def _():
            acc_ref[...] += part

        @pl.when(kk == nk - 1)
        def _():
            o_ref[...] = acc_ref[...] + ALPHA * add_ref[...]

    act = pl.BlockSpec((tm, tk), lambda i, kk: (i, kk))
    row = pl.BlockSpec((tm, d), lambda i, kk: (i, 0))
    return pl.pallas_call(
        body, name=name, grid=(t // tm, nk),
        in_specs=[act, act, pl.BlockSpec((d, tk), lambda i, kk: (0, kk)), pl.BlockSpec((d, tk), lambda i, kk: (0, nk + kk)), row],
        out_specs=row, out_shape=jax.ShapeDtypeStruct((t, d), F32), scratch_shapes=[pltpu.VMEM((tm, d), F32)],
        compiler_params=_cparams("parallel", "arbitrary"),
    )(dgate, dup, w_gu, w_gu, add)


def _ln_stats(pre):
    mu = jnp.mean(pre, axis=-1, keepdims=True)
    xc = pre - mu
    var = jnp.mean(xc * xc, axis=-1, keepdims=True)
    rstd = lax.rsqrt(var + LN_EPS)
    return xc * rstd, rstd


def _ln_bwd(dy, pre, g, name):
    t, d = dy.shape
    tile = 256

    def body(dy_ref, pre_ref, g_ref, dpre_ref, dpre_b_ref, dgb_ref):
        dyv = dy_ref[...]
        xhat, rstd = _ln_stats(pre_ref[...])
        dxh = dyv * g_ref[...]
        m1 = jnp.mean(dxh, axis=-1, keepdims=True)
        m2 = jnp.mean(dxh * xhat, axis=-1, keepdims=True)
        dpre = rstd * (dxh - m1 - xhat * m2)
        dpre_ref[...] = dpre
        dpre_b_ref[...] = dpre.astype(dpre_b_ref.dtype)

        @pl.when(pl.program_id(0) == 0)
        def _():
            dgb_ref[...] = jnp.zeros_like(dgb_ref)

        dgb_ref[0:1, :] += jnp.sum(dyv * xhat, axis=0, keepdims=True)
        dgb_ref[1:2, :] += jnp.sum(dyv, axis=0, keepdims=True)

    row = pl.BlockSpec((tile, d), lambda i: (i, 0))
    return pl.pallas_call(
        body, name=name, grid=(t // tile,), in_specs=[row, row, pl.BlockSpec((1, d), lambda i: (0, 0))],
        out_specs=[row, row, pl.BlockSpec((8, d), lambda i: (0, 0))],
        out_shape=[jax.ShapeDtypeStruct((t, d), F32), jax.ShapeDtypeStruct((t, d), MXU_DTYPE), jax.ShapeDtypeStruct((8, d), F32)],
        compiler_params=_cparams("arbitrary"),
    )(dy, pre, g.reshape(1, d))


def _loss_kernel(y, target, name):
    t, d = y.shape
    tile = 512

    def body(y_ref, t_ref, dy_ref, l_ref):
        err = y_ref[...] - t_ref[...]
        dy_ref[...] = err * (1.0 / d)

        @pl.when(pl.program_id(0) == 0)
        def _():
            l_ref[...] = jnp.zeros_like(l_ref)

        l_ref[...] += jnp.sum(err * err) * (0.5 / d)

    row = pl.BlockSpec((tile, d), lambda i: (i, 0))
    return pl.pallas_call(
        body, name=name, grid=(t // tile,), in_specs=[row, row],
        out_specs=[row, pl.BlockSpec((8, 128), lambda i: (0, 0))],
        out_shape=[jax.ShapeDtypeStruct((t, d), F32), jax.ShapeDtypeStruct((8, 128), F32)],
        compiler_params=_cparams("arbitrary"),
    )(y, target)


def _adamw(w, g, m, v, name):
    nl, r, c = w.shape
    tr = r
    for cand in (256, 352, 128, 64, 16, 8):
        if r % cand == 0:
            tr = cand
            break

    def body(w_ref, g_ref, m_ref, v_ref, d_ref, nm_ref, nv_ref):
        gv = g_ref[...]
        nm = ADAM_B1 * m_ref[...] + (1.0 - ADAM_B1) * gv
        nv = ADAM_B2 * v_ref[...] + (1.0 - ADAM_B2) * (gv * gv)
        m_hat = nm / (1.0 - ADAM_B1 ** ADAM_STEP)
        v_hat = nv / (1.0 - ADAM_B2 ** ADAM_STEP)
        d_ref[...] = -ADAM_LR * (m_hat / (jnp.sqrt(v_hat) + ADAM_EPS) + ADAM_WD * w_ref[...])
        nm_ref[...] = nm
        nv_ref[...] = nv

    blk = pl.BlockSpec((1, tr, c), lambda l, i: (l, i, 0))
    return pl.pallas_call(
        body, name=name, grid=(nl, r // tr), in_specs=[blk] * 4, out_specs=[blk] * 3,
        out_shape=[jax.ShapeDtypeStruct(w.shape, F32)] * 3, compiler_params=_cparams("parallel", "parallel"),
    )(w, g, m, v)


def _shift_down(u, k, rows):
    return jnp.where(rows >= k, pltpu.roll(u, k, 0), 0.0)


def _shift_up(u, k, rows, s):
    return jnp.where(rows < s - k, pltpu.roll(u, s - k, 0), 0.0)


def _conv_fwd(proj, conv_w, nb, s, name):
    def body(b_ref, c_ref, h_ref, w_ref, o_ref):
        rows = _iota2((s, CONV_W), 0)
        u = c_ref[...] * h_ref[...]
        y = w_ref[2:3, :] * u + w_ref[1:2, :] * _shift_down(u, 1, rows) + w_ref[0:1, :] * _shift_down(u, 2, rows)
        o_ref[...] = b_ref[...] * y

    col = lambda j: pl.BlockSpec((s, CONV_W), lambda b: (b, j))
    return pl.pallas_call(
        body, name=name, grid=(nb,),
        in_specs=[col(9), col(10), col(11), pl.BlockSpec((8, CONV_W), lambda b: (0, 0))],
        out_specs=pl.BlockSpec((s, CONV_W), lambda b: (b, 0)),
        out_shape=jax.ShapeDtypeStruct((nb * s, CONV_W), F32), compiler_params=_cparams("parallel"),
    )(proj, proj, proj, conv_w)


def _conv_bwd(dmixed, proj, conv_w, nb, s, name):
    def body(do_ref, b_ref, c_ref, h_ref, w_ref, dg_ref, dw_ref):
        rows = _iota2((s, CONV_W), 0)
        cg, hg, bg, dout = c_ref[...], h_ref[...], b_ref[...], do_ref[...]
        u = cg * hg
        u1 = _shift_down(u, 1, rows)
        u2 = _shift_down(u, 2, rows)
        y = w_ref[2:3, :] * u + w_ref[1:2, :] * u1 + w_ref[0:1, :] * u2
        dy = dout * bg
        du = w_ref[2:3, :] * dy + w_ref[1:2, :] * _shift_up(dy, 1, rows, s) + w_ref[0:1, :] * _shift_up(dy, 2, rows, s)
        dg_ref[:, 0:CONV_W] = dout * y
        dg_ref[:, CONV_W:2 * CONV_W] = du * hg
        dg_ref[:, 2 * CONV_W:3 * CONV_W] = du * cg

        @pl.when(pl.program_id(0) == 0)
        def _():
            dw_ref[...] = jnp.zeros_like(dw_ref)

        dw_ref[0:1, :] += jnp.sum(dy * u2, axis=0, keepdims=True)
        dw_ref[1:2, :] += jnp.sum(dy * u1, axis=0, keepdims=True)
        dw_ref[2:3, :] += jnp.sum(dy * u, axis=0, keepdims=True)

    col = lambda j: pl.BlockSpec((s, CONV_W), lambda b: (b, j))
    return pl.pallas_call(
        body, name=name, grid=(nb,),
        in_specs=[col(3), col(9), col(10), col(11), pl.BlockSpec((8, CONV_W), lambda b: (0, 0))],
        out_specs=[pl.BlockSpec((s, 3 * CONV_W), lambda b: (b, 0)), pl.BlockSpec((8, CONV_W), lambda b: (0, 0))],
        out_shape=[jax.ShapeDtypeStruct((nb * s, 3 * CONV_W), F32), jax.ShapeDtypeStruct((8, CONV_W), F32)],
        compiler_params=_cparams("arbitrary"),
    )(dmixed, proj, proj, proj, conv_w)


def _col_spec(s, base):
    return pl.BlockSpec((s, BLK), lambda b, p: (b, base + p))


def _qrows(i):
    return pl.ds(pl.multiple_of(i * QT, QT), QT)


def _rows(j):
    return pl.ds(pl.multiple_of(j * ATT, ATT), ATT)


def _keys_upto(i):
    return (i + 1) * (QT // ATT)


def _triangle(keep):
    return keep(_iota2((ATT, ATT), 0), _iota2((ATT, ATT), 1)).astype(MXU_DTYPE)


def _rows128(i):
    return pl.ds(pl.multiple_of(i * BLK, BLK), BLK)


def _log_sigmoid_parts(z):
    e = jnp.exp(-jnp.abs(z))
    l1p = jnp.log(1.0 + e)
    lb = jnp.minimum(z, 0.0) - l1p
    return lb, lb - z, e


def _head_masks():
    lane = _iota2((1, BLK), 1)
    return [(lane >= h * HEAD_DIM) & (lane < (h + 1) * HEAD_DIM) for h in range(2)]


def _split_heads(ref, scr, sels):
    for h, sel in enumerate(sels):
        scr[h] = jnp.where(sel, ref[...], 0.0).astype(MXU_DTYPE)


def _sb_fwd(proj, nb, s, name, carry=None):
    def body(q_ref, k_ref, v_ref, o_ref, tails_ref, km, vm):
        sels = _head_masks()
        _split_heads(k_ref, km, sels)
        _split_heads(v_ref, vm, sels)
        rows = _iota2((QT, ATT), 0)
        cols = _iota2((QT, ATT), 1)
        lane = _iota2((QT, BLK), 1)
        later = _triangle(lambda r, c: r > c)
        tails_ref[...] = jnp.zeros_like(tails_ref)

        def qblock(i, _):
            qi = (q_ref[_qrows(i), :] * 0.125).astype(MXU_DTYPE)

            def kblock(t, state):
                carries, acc = state
                j = _keys_upto(i) - 1 - t
                strict = (cols + (j * ATT - i * QT)) < rows
                out = []
                for h in range(2):
                    tails_ref[h, _qrows(i), :] = jnp.where(lane == j, carries[h], tails_ref[h, _qrows(i), :])
                    z = _dot_nt(qi, km[h, _rows(j), :])
                    lb, lr, _ = _log_sigmoid_parts(z)
                    lr = jnp.where(strict, lr, 0.0)
                    tail = _split_dot(lr, later, 2) + carries[h]
                    a = jnp.where(strict, jnp.exp(lb + tail), 0.0)
                    acc = acc + _dot(a, vm[h, _rows(j), :])
                    out.append(carries[h] + jnp.sum(lr, axis=-1, keepdims=True))
                return tuple(out), acc

            init = ((jnp.zeros((QT, 1), F32),) * 2, jnp.zeros((QT, BLK), F32))
            _, acc = lax.fori_loop(0, _keys_upto(i), kblock, init)
            o_ref[_qrows(i), :] = acc
            return 0

        lax.fori_loop(0, s // QT, qblock, 0)

    return _host_call(
        body, carry, name=name, grid=(nb, 2), in_specs=[_col_spec(s, 0), _col_spec(s, 2), _col_spec(s, 4)],
        out_specs=[_col_spec(s, 0), _pair_spec(s, BLK)],
        out_shape=[jax.ShapeDtypeStruct((nb * s, 2 * BLK), F32), jax.ShapeDtypeStruct((nb, N_HEADS, s, BLK), F32)],
        scratch_shapes=[pltpu.VMEM((2, s, BLK), MXU_DTYPE)] * 2, operands=(proj, proj, proj))


def _sb_bwd(proj, dmixed, tails, nb, s, name, carry=None):
    def body(q_ref, k_ref, v_ref, do_ref, tails_ref, dq_ref, dk_ref, dv_ref, km, vm):
        sels = _head_masks()
        _split_heads(k_ref, km, sels)
        _split_heads(v_ref, vm, sels)
        rows = _iota2((QT, ATT), 0)
        cols = _iota2((QT, ATT), 1)
        lane = _iota2((QT, BLK), 1)
        later = _triangle(lambda r, c: r > c)
        earlier = _triangle(lambda r, c: r < c)
        dk_ref[...] = jnp.zeros_like(dk_ref)
        dv_ref[...] = jnp.zeros_like(dv_ref)

        def qblock(i, _):
            qi = (q_ref[_qrows(i), :] * 0.125).astype(MXU_DTYPE)
            doi = do_ref[_qrows(i), :].astype(MXU_DTYPE)
            qm = [jnp.where(sel, qi, 0.0) for sel in sels]
            dom = [jnp.where(sel, doi, 0.0) for sel in sels]
            tails_i = [tails_ref[h, _qrows(i), :] for h in range(2)]

            def kblock(j, state):
                csums, dq = state
                strict = (cols + (j * ATT - i * QT)) < rows
                out = []
                for h in range(2):
                    z = _dot_nt(qi, km[h, _rows(j), :])
                    lb, lr, _ = _log_sigmoid_parts(z)
                    lr = jnp.where(strict, lr, 0.0)
                    after = jnp.sum(jnp.where(lane == j, tails_i[h], 0.0), axis=-1, keepdims=True)
                    a = jnp.where(strict, jnp.exp(lb + _split_dot(lr, later, 2) + after), 0.0)
                    dl = a * _dot_nt(doi, vm[h, _rows(j), :])
                    beta = jnp.exp(lb)
                    before = _split_dot(dl, earlier, 2) + csums[h]
                    dz = jnp.where(strict, dl * (1.0 - beta) - beta * before, 0.0).astype(MXU_DTYPE)
                    dq = dq + _dot(dz, km[h, _rows(j), :])
                    dk_ref[_rows(j), :] += _dot_tn(dz, qm[h])
                    dv_ref[_rows(j), :] += _dot_tn(a, dom[h])
                    out.append(csums[h] + jnp.sum(dl, axis=-1, keepdims=True))
                return tuple(out), dq

            init = ((jnp.zeros((QT, 1), F32),) * 2, jnp.zeros((QT, BLK), F32))
            _, dq = lax.fori_loop(0, _keys_upto(i), kblock, init)
            dq_ref[_qrows(i), :] = dq * 0.125
            return 0

        lax.fori_loop(0, s // QT, qblock, 0)

    out = _col_spec(s, 0)
    return _host_call(
        body, carry, name=name, grid=(nb, 2),
        in_specs=[_col_spec(s, 0), _col_spec(s, 2), _col_spec(s, 4), out, _pair_spec(s, BLK)], out_specs=[out] * 3,
        out_shape=[jax.ShapeDtypeStruct((nb * s, 2 * BLK), F32)] * 3,
        scratch_shapes=[pltpu.VMEM((2, s, BLK), MXU_DTYPE)] * 2, operands=(proj, proj, proj, dmixed, tails))


def _pair_spec(s, width):
    return pl.BlockSpec((None, 2, s, width), lambda b, p: (b, p, 0, 0))


def _fox_fwd(proj, ccol, crow, nb, s, name, carry=None):
    nblk = s // ATT

    def body(q_ref, k_ref, v_ref, cc_ref, cr_ref, o_ref, lse_ref, km, vm):
        sels = _head_masks()
        _split_heads(k_ref, km, sels)
        _split_heads(v_ref, vm, sels)
        rows = _iota2((QT, ATT), 0)
        cols = _iota2((QT, ATT), 1)

        def qblock(i, _):
            qi = (q_ref[_qrows(i), :] * 0.125).astype(MXU_DTYPE)
            ci = [cc_ref[h, _qrows(i), :] for h in range(2)]

            def kblock(j, state):
                ms, ls, acc = state
                causal = (cols + (j * ATT - i * QT)) <= rows
                new_m, new_l, scales, parts = [], [], [], []
                for h in range(2):
                    z = _dot_nt(qi, km[h, _rows(j), :]) + (ci[h] - cr_ref[h, j][0:1, :])
                    z = jnp.where(causal, z, NEG)
                    m_new = jnp.maximum(ms[h], jnp.max(z, axis=-1, keepdims=True))
                    p = jnp.exp(z - m_new)
                    scale = jnp.exp(ms[h] - m_new)
                    new_m.append(m_new)
                    new_l.append(scale * ls[h] + jnp.sum(p, axis=-1, keepdims=True))
                    scales.append(scale)
                    parts.append(_dot(p, vm[h, _rows(j), :]))
                acc = jnp.where(sels[0], scales[0], scales[1]) * acc + parts[0] + parts[1]
                return tuple(new_m), tuple(new_l), acc

            init = ((jnp.full((QT, 1), NEG, F32),) * 2, (jnp.zeros((QT, 1), F32),) * 2, jnp.zeros((QT, BLK), F32))
            ms, ls, acc = lax.fori_loop(0, _keys_upto(i), kblock, init)
            o_ref[_qrows(i), :] = acc / jnp.where(sels[0], ls[0], ls[1])
            for h in range(2):
                lse_ref[h, _qrows(i), :] = jnp.broadcast_to(ms[h] + jnp.log(ls[h]), (QT, ATT))
            return 0

        lax.fori_loop(0, s // QT, qblock, 0)

    crow_spec = pl.BlockSpec((None, 2, nblk, 8, ATT), lambda b, p: (b, p, 0, 0, 0))
    return _host_call(
        body, carry, name=name, grid=(nb, 2),
        in_specs=[_col_spec(s, 12), _col_spec(s, 14), _col_spec(s, 16), _pair_spec(s, ATT), crow_spec],
        out_specs=[_col_spec(s, 0), _pair_spec(s, ATT)],
        out_shape=[jax.ShapeDtypeStruct((nb * s, 2 * BLK), F32), jax.ShapeDtypeStruct((nb, N_HEADS, s, ATT), F32)],
        scratch_shapes=[pltpu.VMEM((2, s, BLK), MXU_DTYPE)] * 2, operands=(proj, proj, proj, ccol, crow))


def _fox_bwd(proj, dmixed, lse, ccol, crow, nb, s, name, carry=None):
    nblk = s // ATT

    def body(q_ref, k_ref, v_ref, do_ref, lse_ref, cc_ref, cr_ref, dq_ref, dk_ref, dv_ref, dc_ref, km, vm, p_scr, dp_scr):
        sels = _head_masks()
        _split_heads(k_ref, km, sels)
        _split_heads(v_ref, vm, sels)
        rows = _iota2((QT, ATT), 0)
        cols = _iota2((QT, ATT), 1)
        dk_ref[...] = jnp.zeros_like(dk_ref)
        dv_ref[...] = jnp.zeros_like(dv_ref)
        dc_ref[...] = jnp.zeros_like(dc_ref)

        def qblock(i, _):
            qi = (q_ref[_qrows(i), :] * 0.125).astype(MXU_DTYPE)
            doi = do_ref[_qrows(i), :].astype(MXU_DTYPE)
            qm = [jnp.where(sel, qi, 0.0) for sel in sels]
            dom = [jnp.where(sel, doi, 0.0) for sel in sels]
            ci = [cc_ref[h, _qrows(i), :] for h in range(2)]
            lsei = [lse_ref[h, _qrows(i), :] for h in range(2)]

            def probs(j, h):
                z = _dot_nt(qi, km[h, _rows(j), :]) + (ci[h] - cr_ref[h, j][0:1, :])
                p = jnp.where((cols + (j * ATT - i * QT)) <= rows, jnp.exp(z - lsei[h]), 0.0)
                return p, _dot_nt(doi, vm[h, _rows(j), :])

            def row_term(j, accs):
                out = []
                for h in range(2):
                    p, dp = probs(j, h)
                    p_scr[h, j] = p
                    dp_scr[h, j] = dp
                    out.append(accs[h] + jnp.sum(p * dp, axis=-1, keepdims=True))
                return tuple(out)

            di = lax.fori_loop(0, _keys_upto(i), row_term, (jnp.zeros((QT, 1), F32),) * 2)

            def kblock(j, dq):
                for h in range(2):
                    p = p_scr[h, j]
                    ds = p * (dp_scr[h, j] - di[h])
                    dc_ref[h, j] += jnp.broadcast_to(jnp.sum(ds, axis=0, keepdims=True), (8, ATT))
                    ds = ds.astype(MXU_DTYPE)
                    dk_ref[_rows(j), :] += _dot_tn(ds, qm[h])
                    dv_ref[_rows(j), :] += _dot_tn(p, dom[h])
                    dq = dq + _dot(ds, km[h, _rows(j), :])
                return dq

            dq = lax.fori_loop(0, _keys_upto(i), kblock, jnp.zeros((QT, BLK), F32))
            dq_ref[_qrows(i), :] = dq * 0.125
            return 0

        lax.fori_loop(0, s // QT, qblock, 0)

    crow_spec = pl.BlockSpec((None, 2, nblk, 8, ATT), lambda b, p: (b, p, 0, 0, 0))
    wide, cols_out = _pair_spec(s, ATT), _col_spec(s, 0)
    return _host_call(
        body, carry, name=name, grid=(nb, 2),
        in_specs=[_col_spec(s, 12), _col_spec(s, 14), _col_spec(s, 16), _col_spec(s, 4), wide, wide, crow_spec],
        out_specs=[cols_out, cols_out, cols_out, crow_spec],
        out_shape=[jax.ShapeDtypeStruct((nb * s, 2 * BLK), F32)] * 3 + [jax.ShapeDtypeStruct((nb, N_HEADS, nblk, 8, ATT), F32)],
        scratch_shapes=[pltpu.VMEM((2, s, BLK), MXU_DTYPE)] * 2 + [pltpu.VMEM((2, nblk, QT, ATT), F32)] * 2,
        operands=(proj, proj, proj, dmixed, lse, ccol, crow))


def _fox_gates_fwd(proj, f_bias, nb, s, name):
    chunk = 256

    def body(f_ref, b_ref, c_ref):
        lower = (_iota2((chunk, chunk), 0) >= _iota2((chunk, chunk), 1)).astype(MXU_DTYPE)
        carry = jnp.zeros((1, BLK), F32)
        for n in range(s // chunk):
            rows = pl.ds(n * chunk, chunk)
            lf, _, _ = _log_sigmoid_parts(f_ref[rows, :] + b_ref[0:1, :])
            c = _split_dot_lhs(lower, lf, 3) + carry
            c_ref[rows, :] = c
            carry = c[chunk - 1:chunk, :]

    return pl.pallas_call(
        body, name=name, grid=(nb,),
        in_specs=[pl.BlockSpec((s, BLK), lambda b: (b, (PROJ_PAD - BLK) // BLK)), pl.BlockSpec((8, BLK), lambda b: (0, 0))],
        out_specs=pl.BlockSpec((s, BLK), lambda b: (b, 0)),
        out_shape=jax.ShapeDtypeStruct((nb * s, BLK), F32), compiler_params=_cparams("parallel"),
    )(proj, f_bias)


def _fox_gates_bwd(dc, proj, f_bias, nb, s, name):
    chunk = 256

    def body(dc_ref, f_ref, b_ref, df_ref, db_ref):
        upper = (_iota2((chunk, chunk), 0) <= _iota2((chunk, chunk), 1)).astype(MXU_DTYPE)
        carry = jnp.zeros((1, BLK), F32)
        total = jnp.zeros((1, BLK), F32)
        for n in reversed(range(s // chunk)):
            rows = pl.ds(n * chunk, chunk)
            dlf = _split_dot_lhs(upper, dc_ref[rows, :], 3) + carry
            carry = dlf[0:1, :]
            pre = f_ref[rows, :] + b_ref[0:1, :]
            e = jnp.exp(-jnp.abs(pre))
            df = dlf * (jnp.where(pre >= 0.0, e, 1.0) / (1.0 + e))
            df_ref[rows, :] = df
            total = total + jnp.sum(df, axis=0, keepdims=True)

        @pl.when(pl.program_id(0) == 0)
        def _():
            db_ref[...] = jnp.zeros_like(db_ref)

        db_ref[0:1, :] += total

    return pl.pallas_call(
        body, name=name, grid=(nb,),
        in_specs=[pl.BlockSpec((s, BLK), lambda b: (b, 0)), pl.BlockSpec((s, BLK), lambda b: (b, (PROJ_PAD - BLK) // BLK)),
                  pl.BlockSpec((8, BLK), lambda b: (0, 0))],
        out_specs=[pl.BlockSpec((s, BLK), lambda b: (b, 0)), pl.BlockSpec((8, BLK), lambda b: (0, 0))],
        out_shape=[jax.ShapeDtypeStruct((nb * s, BLK), F32), jax.ShapeDtypeStruct((8, BLK), F32)],
        compiler_params=_cparams("arbitrary"),
    )(dc, proj, f_bias)


def _delta_kernel(dmixed, o, nb, s, name):
    def body(do_ref, o_ref, d_ref):
        prod = do_ref[...] * o_ref[...]
        for h, sel in enumerate(_head_masks()):
            d_ref[h] = jnp.broadcast_to(jnp.sum(jnp.where(sel, prod, 0.0), axis=-1, keepdims=True), (s, BLK))

    return pl.pallas_call(
        body, name=name, grid=(nb, 2), in_specs=[_col_spec(s, 2), _col_spec(s, 0)], out_specs=_pair_spec(s, BLK),
        out_shape=jax.ShapeDtypeStruct((nb, N_HEADS, s, BLK), F32), compiler_params=_cparams("parallel", "parallel"),
    )(dmixed, o)


def _t5_bucket_np(dist):
    max_exact = REL_BUCKETS // 2
    nf = np.maximum(dist, 1).astype(np.float32)
    large = max_exact + (np.log(nf / max_exact) / math.log(2048 / max_exact) * (REL_BUCKETS - max_exact)).astype(np.int32)
    large = np.minimum(large, REL_BUCKETS - 1)
    return np.where(dist < max_exact, dist, large)


def _bucket_table():
    qi = np.arange(BLK)[:, None]
    kj = np.arange(2 * BLK)[None, :]
    dist = qi + BLK - kj
    tables = []
    for window, dil in DIL_PATTERNS:
        in_band = (dist >= 0) & (dist <= window // dil)
        tables.append(np.where(in_band, _t5_bucket_np(np.maximum(dist, 0) * dil), -1).astype(np.int32))
    return np.stack(tables)


def _dil_scores(qb, kp, kc, b_ref, h, prev_valid):
    zp = _dot_nt(qb, kp) + b_ref[h, :, 0:BLK]
    zp = jnp.where(prev_valid, zp, NEG)
    zc = _dot_nt(qb, kc) + b_ref[h, :, BLK:2 * BLK]
    return zp, zc


def _residue_rows(b, seg, dil):
    if dil == 1:
        return _rows128(b), _rows128(jnp.maximum(b - 1, 0)), b > 0
    r, n = b // seg, b % seg
    cur = pl.ds(r + dil * n * BLK, BLK, stride=dil)
    prev = pl.ds(r + dil * jnp.maximum(n - 1, 0) * BLK, BLK, stride=dil)
    return cur, prev, n > 0


def _dil_attention_fwd(proj, bias, nb, s, name, carry=None):
    nblk = s // BLK

    def body(q_ref, k_ref, v_ref, b_ref, out_ref, lse_ref, o_scr, l_scr):
        sels = _head_masks()
        for p, (_, dil) in enumerate(DIL_PATTERNS):
            seg = s // dil // BLK

            def block(b, _, p=p, seg=seg, dil=dil):
                cur, prev, has_prev = _residue_rows(b, seg, dil)
                qb = (q_ref[cur, :] * 0.125).astype(MXU_DTYPE)
                kp, kc = k_ref[prev, :].astype(MXU_DTYPE), k_ref[cur, :].astype(MXU_DTYPE)
                vp, vc = v_ref[prev, :].astype(MXU_DTYPE), v_ref[cur, :].astype(MXU_DTYPE)
                acc = jnp.zeros((BLK, BLK), F32)
                for h, sel in enumerate(sels):
                    zp, zc = _dil_scores(qb, jnp.where(sel, kp, 0.0), jnp.where(sel, kc, 0.0), b_ref.at[p], h, has_prev)
                    m = jnp.maximum(jnp.max(zp, axis=-1, keepdims=True), jnp.max(zc, axis=-1, keepdims=True))
                    pp = jnp.exp(zp - m)
                    pc = jnp.exp(zc - m)
                    den = jnp.sum(pp, axis=-1, keepdims=True) + jnp.sum(pc, axis=-1, keepdims=True)
                    acc = acc + (_dot(pp, jnp.where(sel, vp, 0.0)) + _dot(pc, jnp.where(sel, vc, 0.0))) / den
                    l_scr[p, h, cur, :] = jnp.broadcast_to(m + jnp.log(den), (BLK, BLK))
                o_scr[p, cur, :] = acc
                return 0

            lax.fori_loop(0, nblk, block, 0, unroll=4)

        weights, dens = [], []
        for h in range(2):
            m = jnp.maximum(jnp.maximum(l_scr[0, h], l_scr[1, h]), l_scr[2, h])
            w = [jnp.exp(l_scr[p, h] - m) for p in range(3)]
            den = w[0] + w[1] + w[2]
            lse_ref[h] = m + jnp.log(den)
            weights.append(w)
            dens.append(den)
        num = sum(jnp.where(sels[0], weights[0][p], weights[1][p]) * o_scr[p] for p in range(3))
        out_ref[...] = num / jnp.where(sels[0], dens[0], dens[1])

    bias_spec = pl.BlockSpec((3, 2, BLK, 2 * BLK), lambda b, p: (0, p, 0, 0))
    return _host_call(
        body, carry, name=name, grid=(nb, 2), in_specs=[_col_spec(s, 6), _col_spec(s, 8), _col_spec(s, 10), bias_spec],
        out_specs=[_col_spec(s, 0), _pair_spec(s, BLK)],
        out_shape=[jax.ShapeDtypeStruct((nb * s, 2 * BLK), F32), jax.ShapeDtypeStruct((nb, N_HEADS, s, BLK), F32)],
        scratch_shapes=[pltpu.VMEM((3, s, BLK), F32), pltpu.VMEM((3, 2, s, BLK), F32)], operands=(proj, proj, proj, bias))


def _dil_attention_bwd(proj, dmixed, lse, delta, bias, nb, s, name, carry=None):
    nblk = s // BLK

    def body(q_ref, k_ref, v_ref, do_ref, lse_ref, dl_ref, b_ref, dq_ref, dk_ref, dv_ref, g_ref):
        sels = _head_masks()
        dq_ref[...] = jnp.zeros_like(dq_ref)
        dk_ref[...] = jnp.zeros_like(dk_ref)
        dv_ref[...] = jnp.zeros_like(dv_ref)
        g_ref[...] = jnp.zeros_like(g_ref)
        for p, (_, dil) in enumerate(DIL_PATTERNS):
            seg = s // dil // BLK

            def block(b, _, p=p, seg=seg, dil=dil):
                cur, prev, has_prev = _residue_rows(b, seg, dil)
                qb = (q_ref[cur, :] * 0.125).astype(MXU_DTYPE)
                dob = do_ref[cur, :].astype(MXU_DTYPE)
                kp, kc = k_ref[prev, :].astype(MXU_DTYPE), k_ref[cur, :].astype(MXU_DTYPE)
                vp, vc = v_ref[prev, :].astype(MXU_DTYPE), v_ref[cur, :].astype(MXU_DTYPE)
                dq = jnp.zeros((BLK, BLK), F32)
                dkp, dkc, dvp, dvc = dq, dq, dq, dq
                for h, sel in enumerate(sels):
                    kph, kch = jnp.where(sel, kp, 0.0), jnp.where(sel, kc, 0.0)
                    qh, doh = jnp.where(sel, qb, 0.0), jnp.where(sel, dob, 0.0)
                    lse_h = lse_ref[h, cur, :]
                    dlt = dl_ref[h, cur, :]
                    zp, zc = _dil_scores(qb, kph, kch, b_ref.at[p], h, has_prev)
                    pp = jnp.exp(zp - lse_h)
                    pc = jnp.exp(zc - lse_h)
                    dsp = pp * (_dot_nt(dob, jnp.where(sel, vp, 0.0)) - dlt)
                    dsc = pc * (_dot_nt(dob, jnp.where(sel, vc, 0.0)) - dlt)
                    g_ref[h, p, :, 0:BLK] += dsp
                    g_ref[h, p, :, BLK:2 * BLK] += dsc
                    dsp = dsp.astype(MXU_DTYPE)
                    dsc = dsc.astype(MXU_DTYPE)
                    dq = dq + _dot(dsp, kph) + _dot(dsc, kch)
                    dkp, dkc = dkp + _dot_tn(dsp, qh), dkc + _dot_tn(dsc, qh)
                    dvp, dvc = dvp + _dot_tn(pp, doh), dvc + _dot_tn(pc, doh)
                dq_ref[cur, :] += dq * 0.125
                dk_ref[prev, :] += dkp
                dk_ref[cur, :] += dkc
                dv_ref[prev, :] += dvp
                dv_ref[cur, :] += dvc
                return 0

            lax.fori_loop(0, nblk, block, 0, unroll=4)

    bias_spec = pl.BlockSpec((3, 2, BLK, 2 * BLK), lambda b, p: (0, p, 0, 0))
    cols, stats = _col_spec(s, 0), _pair_spec(s, BLK)
    return _host_call(
        body, carry, name=name, grid=(nb, 2),
        in_specs=[_col_spec(s, 6), _col_spec(s, 8), _col_spec(s, 10), _col_spec(s, 2), stats, stats, bias_spec],
        out_specs=[cols, cols, cols, pl.BlockSpec((None, 2, 3, BLK, 2 * BLK), lambda b, p: (b, p, 0, 0, 0))],
        out_shape=[jax.ShapeDtypeStruct((nb * s, 2 * BLK), F32)] * 3 + [jax.ShapeDtypeStruct((nb, N_HEADS, 3, BLK, 2 * BLK), F32)],
        operands=(proj, proj, proj, dmixed, lse, delta, bias))


def _bucket_reduce(gbias, table, name):
    nb = gbias.shape[0]

    def body(g_ref, t_ref, o_ref):
        row = _iota2((8, BLK), 0)
        lane = _iota2((8, BLK), 1)
        gsum = [[sum(g_ref[b, h, p] for b in range(nb)) for p in range(3)] for h in range(N_HEADS)]

        def bucket(k, acc):
            for h in range(N_HEADS):
                tot = sum(jnp.sum(jnp.where(t_ref[p] == k, gsum[h][p], 0.0)) for p in range(3))
                acc = acc + jnp.where((row == h) & (lane == k), tot, 0.0)
            return acc

        o_ref[...] = lax.fori_loop(0, REL_BUCKETS, bucket, jnp.zeros((8, BLK), F32))

    vm = pl.BlockSpec(memory_space=pltpu.VMEM)
    return pl.pallas_call(
        body, name=name, in_specs=[vm, vm], out_specs=vm, out_shape=jax.ShapeDtypeStruct((8, BLK), F32),
        compiler_params=pltpu.CompilerParams(vmem_limit_bytes=VMEM_LIMIT),
    )(gbias, table)


def _place():
    x, y, c = lax.axis_index("x"), lax.axis_index("y"), lax.axis_index("c")
    others = [(1 - x, y), (x, 1 - y), (1 - x, 1 - y)]
    return x, y, c, others


def _remote(src, dst, send_sem, recv_sem, to):
    return pltpu.make_async_remote_copy(src_ref=src, dst_ref=dst, send_sem=send_sem, recv_sem=recv_sem,
                                        device_id=to, device_id_type=MESH)


_HBM = pl.BlockSpec(memory_space=pl.ANY)


class _Exchange:
    def __init__(self, operands, out_shape, n_copies, copies, aliases=None):
        self.operands, self.out_shape, self.n_copies, self.copies = list(operands), list(out_shape), n_copies, copies
        self.aliases = dict(aliases or {})

    def sem_shapes(self):
        return [pltpu.SemaphoreType.DMA((self.n_copies,)), pltpu.SemaphoreType.DMA((self.n_copies,))]


def _start_all(sends):
    for cp in sends:
        cp.start()


def _wait_all(sends, arrivals):
    for cp in arrivals:
        cp.wait_recv()
    for cp in sends:
        cp.wait_send()


def _run_exchange(ex, name):
    ni = len(ex.operands)

    def body(*refs):
        sends, arrivals = ex.copies(refs[:ni], refs[ni:-2], refs[-2], refs[-1])
        _start_all(sends)
        _wait_all(sends, arrivals)

    return list(pl.pallas_call(
        body, name=name, in_specs=[_HBM] * ni, out_specs=[_HBM] * len(ex.out_shape), out_shape=ex.out_shape,
        scratch_shapes=ex.sem_shapes(), input_output_aliases=ex.aliases)(*ex.operands))


def _host_call(body, carry, *, name, grid, in_specs, out_specs, out_shape, operands, scratch_shapes=()):
    in_specs, out_specs, out_shape, scratch_shapes = list(in_specs), list(out_specs), list(out_shape), list(scratch_shapes)
    if carry is None:
        res = pl.pallas_call(body, name=name, grid=grid, in_specs=in_specs, out_specs=out_specs, out_shape=out_shape,
                             scratch_shapes=scratch_shapes, compiler_params=_cparams(*["parallel"] * len(grid)))(*operands)
        return list(res), []
    n_in, n_out, n_scr, c_in, c_out = len(in_specs), len(out_specs), len(scratch_shapes), len(carry.operands), len(carry.out_shape)
    steps = math.prod(grid)

    def wrapped(*refs):
        ins, refs = refs[:n_in], refs[n_in:]
        c_ins, refs = refs[:c_in], refs[c_in:]
        outs, refs = refs[:n_out], refs[n_out:]
        c_outs, refs = refs[:c_out], refs[c_out:]
        scr, (send_sems, recv_sems) = refs[:n_scr], refs[n_scr:]
        step = 0
        for d, size in enumerate(grid):
            step = step * size + pl.program_id(d)

        @pl.when(step == 0)
        def _():
            _start_all(carry.copies(c_ins, c_outs, send_sems, recv_sems)[0])

        body(*ins, *outs, *scr)

        @pl.when(step == steps - 1)
        def _():
            _wait_all(*carry.copies(c_ins, c_outs, send_sems, recv_sems))

    res = pl.pallas_call(
        wrapped, name=name, grid=grid, in_specs=in_specs + [_HBM] * c_in, out_specs=out_specs + [_HBM] * c_out,
        out_shape=out_shape + carry.out_shape, scratch_shapes=scratch_shapes + carry.sem_shapes(),
        input_output_aliases={n_in + i: n_out + j for i, j in carry.aliases.items()},
        compiler_params=_cparams(*["arbitrary"] * len(grid)))(*operands, *carry.operands)
    return list(res[:n_out]), list(res[n_out:])


def _half(which, rows):
    h = rows // 2
    return pl.ds(pl.multiple_of(which * h, 16), h)


def _like(arrays, shape_of=lambda t: t.shape):
    return [jax.ShapeDtypeStruct(shape_of(t), t.dtype) for t in arrays]


def _gather_ici(shards, layer):
    n = len(shards)

    def copies(ins, outs, send_sems, recv_sems, base=0):
        x, y, c, others = _place()
        me = 2 * x + y
        sends, arrivals = [], []
        for a in range(n):
            rows = _half(c, shards[a].shape[1])
            for k, (ox, oy) in enumerate(others):
                sems = (send_sems.at[base + 3 * a + k], recv_sems.at[base + 3 * a + k],(ox, oy, c))
                sends.append(_remote(ins[a].at[layer, rows], outs[a].at[me, rows], *sems))
                landed = outs[a].at[2 * ox + oy, rows]
                arrivals.append(_remote(landed, landed, *sems))
        return sends, arrivals

    return _Exchange(shards, _like(shards, lambda t: (N_CHIPS,) + t.shape[1:]), 3 * n, copies)


def _gather_d2d(gathered):
    n = len(gathered)

    def copies(ins, outs, send_sems, recv_sems, base=0):
        x, y, c, others = _place()
        sends, arrivals = [], []
        for a in range(n):
            r = gathered[a].shape[1]
            for k, (ox, oy) in enumerate(others):
                sems = (send_sems.at[base + 3 * a + k], recv_sems.at[base + 3 * a + k],(x, y, 1 - c))
                mine, theirs = outs[a].at[2 * ox + oy, _half(c, r)], outs[a].at[2 * ox + oy, _half(1 - c, r)]
                sends.append(_remote(mine, mine, *sems))
                arrivals.append(_remote(theirs, theirs, *sems))
        return sends, arrivals

    return _Exchange(gathered, _like(gathered), 3 * n, copies, aliases={a: a for a in range(n)})


def _swap_halves(g):
    n = len(g)

    def copies(ins, outs, send_sems, recv_sems, base=0):
        x, y, c, _ = _place()
        sends, arrivals = [], []
        for a in range(n):
            sems = (send_sems.at[base + a], recv_sems.at[base + a], (x, y, 1 - c))
            sends.append(_remote(ins[a].at[:, _half(1 - c, g[a].shape[1])], outs[a], *sems))
            arrivals.append(_remote(outs[a], outs[a], *sems))
        return sends, arrivals

    return _Exchange(g, _like(g, lambda t: (t.shape[0], t.shape[1] // 2, t.shape[2])), n, copies)


def _scatter_shards(ps):
    n = len(ps)

    def copies(ins, outs, send_sems, recv_sems, base=0):
        x, y, c, others = _place()
        me = 2 * x + y
        sends, arrivals = [], []
        for a in range(n):
            for k, (ox, oy) in enumerate(others):
                sems = (send_sems.at[base + 3 * a + k], recv_sems.at[base + 3 * a + k],(ox, oy, c))
                sends.append(_remote(ins[a].at[2 * ox + oy], outs[a].at[me], *sems))
                slot = outs[a].at[2 * ox + oy]
                arrivals.append(_remote(slot, slot, *sems))
        return sends, arrivals

    return _Exchange(ps, _like(ps), 3 * n, copies)


def _share_halves(mine):
    n = len(mine)

    def copies(ins, outs, send_sems, recv_sems, base=0):
        x, y, c, _ = _place()
        sends, arrivals = [], []
        for a in range(n):
            sems = (send_sems.at[base + a], recv_sems.at[base + a], (x, y, 1 - c))
            sends.append(_remote(ins[a], outs[a], *sems))
            arrivals.append(_remote(outs[a], outs[a], *sems))
        return sends, arrivals

    return _Exchange(mine, _like(mine), n, copies)


def _row_tile(r):
    for cand in (256, 352, 128):
        if r % cand == 0:
            return cand
    return r


def _pair_sum(g, other, core, name):
    ns, h, w = other.shape
    tr = _row_tile(h)
    per_half = h // tr

    def body(core_ref, g_ref, o_ref, out_ref):
        out_ref[...] = (g_ref[...] + o_ref[...]).astype(out_ref.dtype)

    blk = pl.BlockSpec((None, tr, w), lambda k, i, core_ref: (k, i, 0))
    grid_spec = pltpu.PrefetchScalarGridSpec(
        num_scalar_prefetch=1, grid=(ns, per_half),
        in_specs=[pl.BlockSpec((None, tr, w), lambda k, i, core_ref: (k, core_ref[0] * per_half + i, 0)), blk], out_specs=blk)
    return pl.pallas_call(
        body, name=name, grid_spec=grid_spec, out_shape=jax.ShapeDtypeStruct((ns, h, w), MXU_DTYPE),
        compiler_params=_cparams("parallel", "parallel"),
    )(core.reshape(1).astype(jnp.int32), g, other)


def _chip_sum(q, p, chip, name):
    ns, r, w = q.shape
    tr = _row_tile(r)

    def body(chip_ref, q_ref, own_ref, out_ref):
        me = chip_ref[0]
        own = own_ref[...].astype(F32)
        terms = [jnp.where(me == k, own, q_ref[k].astype(F32)) for k in range(ns)]
        out_ref[...] = ((terms[0] + terms[1]) + terms[2]) + terms[3]

    grid_spec = pltpu.PrefetchScalarGridSpec(
        num_scalar_prefetch=1, grid=(r // tr,),
        in_specs=[pl.BlockSpec((ns, tr, w), lambda i, chip_ref: (0, i, 0)),
                  pl.BlockSpec((None, tr, w), lambda i, chip_ref: (chip_ref[0], i, 0))],
        out_specs=pl.BlockSpec((tr, w), lambda i, chip_ref: (i, 0)))
    return pl.pallas_call(
        body, name=name, grid_spec=grid_spec, out_shape=jax.ShapeDtypeStruct((r, w), F32),
        compiler_params=_cparams("parallel"),
    )(chip.reshape(1).astype(jnp.int32), q, p)


def _merge(exchanges):
    if len(exchanges) <= 1:
        return exchanges[0] if exchanges else None
    operands, out_shape, aliases, spans, n = [], [], {}, [], 0
    for ex in exchanges:
        spans.append((len(operands), len(out_shape), n))
        aliases.update({len(operands) + i: len(out_shape) + j for i, j in ex.aliases.items()})
        operands += ex.operands
        out_shape += ex.out_shape
        n += ex.n_copies

    def copies(ins, outs, send_sems, recv_sems, base=0):
        sends, arrivals = [], []
        for ex, (i0, o0, s0) in zip(exchanges, spans):
            s, a = ex.copies(ins[i0:i0 + len(ex.operands)], outs[o0:o0 + len(ex.out_shape)], send_sems, recv_sems, base + s0)
            sends += s
            arrivals += a
        return sends, arrivals

    return _Exchange(operands, out_shape, n, copies, aliases)


def _take(hooks, host):
    stages = (hooks or {}).pop(host, [])
    exchanges = [make() for make, _ in stages]

    def finish(results):
        for (_, done), ex in zip(stages, exchanges):
            done(results[:len(ex.out_shape)])
            results = results[len(ex.out_shape):]

    return _merge(exchanges), finish


def _hook(hooks, host, make, done):
    hooks.setdefault(host, []).append((make, done))


class _WeightPrefetch:
    def __init__(self, names, shards, layer, chip):
        self.names, self.shards, self.layer, self.chip, self.result = names, [shards[n] for n in names], layer, chip, None

    def first(self):
        return _gather_ici(self.shards, self.layer)

    def got_first(self, arrived):
        self.arrived = arrived

    def second(self):
        return _gather_d2d(self.arrived)

    def got_second(self, gathered):
        self.result = {name: lax.dynamic_update_index_in_dim(got, own[self.layer], self.chip, 0)
                       for name, got, own in zip(self.names, gathered, self.shards)}

    def ride(self, hooks, first_host, second_host):
        _hook(hooks, first_host, self.first, self.got_first)
        _hook(hooks, second_host, self.second, self.got_second)

    def run(self, tag):
        self.got_first(_run_exchange(self.first(), f"gather_ici_{tag}"))
        self.got_second(_run_exchange(self.second(), f"gather_d2d_{tag}"))


class _GradReduce:
    def __init__(self, g, chip, core, tag):
        self.names, self.g, self.chip, self.core, self.tag, self.result = list(g), list(g.values()), chip, core, tag, None

    def swap(self):
        return _swap_halves(self.g)

    def got_swap(self, theirs):
        self.pair = [_pair_sum(g, t, self.core, f"pair_sum_{n}_{self.tag}") for n, g, t in zip(self.names, self.g, theirs)]

    def scatter(self):
        return _scatter_shards(self.pair)

    def got_scatter(self, q):
        self.mine = [_chip_sum(qa, pa, self.chip, f"chip_sum_{n}_{self.tag}") for n, qa, pa in zip(self.names, q, self.pair)]

    def share(self):
        return _share_halves(self.mine)

    def got_share(self, theirs):
        self.result = {n: jnp.where(self.core == 0, jnp.concatenate([a, b]), jnp.concatenate([b, a]))
                       for n, a, b in zip(self.names, self.mine, theirs)}

    def ride(self, hooks, swap_host, scatter_host, share_host):
        _hook(hooks, swap_host, self.swap, self.got_swap)
        _hook(hooks, scatter_host, self.scatter, self.got_scatter)
        _hook(hooks, share_host, self.share, self.got_share)

    def run(self):
        self.got_swap(_run_exchange(self.swap(), f"swap_halves_{self.tag}"))
        self.got_scatter(_run_exchange(self.scatter(), f"scatter_shards_{self.tag}"))
        self.got_share(_run_exchange(self.share(), f"share_halves_{self.tag}"))


class _LayerWeights:
    def __init__(self, gathered):
        self.gathered, self.made = gathered, {}

    def __getitem__(self, key):
        if key not in self.made:
            cols = lambda t: jnp.swapaxes(t, 0, 1).reshape(t.shape[1], -1)
            rows = lambda t: t.reshape(-1, t.shape[2])
            if key == "w_in":
                made = jnp.pad(cols(self.gathered("w_in")), ((0, 0), (0, PROJ_PAD - PROJ)))
            elif key == "w_gu":
                made = jnp.concatenate([cols(self.gathered("w_gate")), cols(self.gathered("w_up"))], axis=-1)
            else:
                made = rows(self.gathered(key))
            self.made[key] = made
        return self.made[key]


def _gather_small(pk, name):
    rows, w = pk.shape

    def body(pk_ref, all_ref, sum_ref, send_sems, recv_sems):
        x, y, c, _ = _place()
        me = 4 * x + 2 * y + c
        all_ref[me] = pk_ref[...]
        flips = [(fx, fy, fc) for fx in (0, 1) for fy in (0, 1) for fc in (0, 1)][1:]
        peers = [(x ^ fx, y ^ fy, c ^ fc) for fx, fy, fc in flips]
        sends = [_remote(pk_ref, all_ref.at[me], send_sems.at[k], recv_sems.at[k], peer) for k, peer in enumerate(peers)]
        for cp in sends:
            cp.start()
        for k, (px, py, pc) in enumerate(peers):
            slot = all_ref.at[4 * px + 2 * py + pc]
            _remote(slot, slot, send_sems.at[k], recv_sems.at[k], (px, py, pc)).wait_recv()
        for cp in sends:
            cp.wait_send()
        total = all_ref[0]
        for d in range(1, N_DEV):
            total = total + all_ref[d]
        sum_ref[...] = total

    vm = pl.BlockSpec(memory_space=pltpu.VMEM)
    return pl.pallas_call(
        body, name=name, in_specs=[vm], out_specs=[vm, vm],
        out_shape=[jax.ShapeDtypeStruct((N_DEV, rows, w), F32), jax.ShapeDtypeStruct((rows, w), F32)],
        scratch_shapes=[pltpu.SemaphoreType.DMA((7,)), pltpu.SemaphoreType.DMA((7,))],
    )(pk)


def _row_layout(c, nb, s):
    ch = jnp.swapaxes(c[:, :N_HEADS].reshape(nb, s, N_HEADS), 1, 2)
    ccol = jnp.broadcast_to(ch[..., None], (nb, N_HEADS, s, ATT))
    crow = jnp.broadcast_to(ch.reshape(nb, N_HEADS, s // ATT, 1, ATT), (nb, N_HEADS, s // ATT, 8, ATT))
    return ccol, crow


def _dil_bias(rel_bias, name):
    def body(rel_ref, t_ref, o_ref):
        for p in range(len(DIL_PATTERNS)):
            table = t_ref[p]

            def bucket(k, accs, table=table):
                return tuple(jnp.where(table == k, rel_ref[k, h], acc) for h, acc in enumerate(accs))

            accs = lax.fori_loop(0, REL_BUCKETS, bucket, tuple(jnp.full((BLK, 2 * BLK), NEG, F32) for _ in range(N_HEADS)))
            for h in range(N_HEADS):
                o_ref[p, h] = accs[h]

    vm = pl.BlockSpec(memory_space=pltpu.VMEM)
    return pl.pallas_call(
        body, name=name, in_specs=[pl.BlockSpec(memory_space=pltpu.SMEM), vm], out_specs=vm,
        out_shape=jax.ShapeDtypeStruct((len(DIL_PATTERNS), N_HEADS, BLK, 2 * BLK), F32),
        compiler_params=pltpu.CompilerParams(vmem_limit_bytes=VMEM_LIMIT),
    )(rel_bias, jnp.asarray(_bucket_table()))


def _layer_forward(x, x_b, wts, small, nb, s, tag, hooks=None):
    proj = _matmul(x_b, wts["w_in"], "proj", tag)

    carry, finish = _take(hooks, "sb_fwd")
    (o_sb, tails_sb), carried = _sb_fwd(proj, nb, s, f"sb_fwd_{tag}", carry)
    finish(carried)

    bias = _dil_bias(small["rel_bias"], f"dil_bias_{tag}")
    carry, finish = _take(hooks, "dil_fwd")
    (o_dl, lse_dl), carried = _dil_attention_fwd(proj, bias, nb, s, f"dil_fwd_{tag}", carry)
    finish(carried)

    fb = jnp.zeros((8, BLK), F32).at[0, :N_HEADS].set(small["f_bias"])
    csum = _fox_gates_fwd(proj, fb, nb, s, f"fox_gates_{tag}")
    ccol, crow = _row_layout(csum, nb, s)
    carry, finish = _take(hooks, "fox_fwd")
    (o_fx, lse_fx), carried = _fox_fwd(proj, ccol, crow, nb, s, f"fox_fwd_{tag}", carry)
    finish(carried)

    cw = jnp.zeros((8, CONV_W), F32).at[:3].set(small["conv_w"])
    o_cv = _conv_fwd(proj, cw, nb, s, f"conv_fwd_{tag}")

    mixed = jnp.concatenate([o_sb, o_dl, o_fx, o_cv], axis=-1).astype(MXU_DTYPE)
    pre1, x1, x1_b = _matmul_post_norm(mixed, wts["w_out"], x, small["ln1_g"], small["ln1_b"], f"out_proj_ln1_{tag}")
    carry, finish = _take(hooks, "ffn_in")
    (gate, up, hid), carried = _ffn_in(x1_b, wts["w_gu"], f"ffn_in_{tag}", carry)
    finish(carried)
    pre2, x2, x2_b = _matmul_post_norm(hid, wts["w_down"], x1, small["ln2_g"], small["ln2_b"], f"ffn_out_ln2_{tag}")
    saved = dict(x_b=x_b, proj=proj, tails_sb=tails_sb, bias=bias, o_dl=o_dl, lse_dl=lse_dl, fb=fb, ccol=ccol, crow=crow, o_fx=o_fx,
                 lse_fx=lse_fx, cw=cw, mixed=mixed, pre1=pre1, x1_b=x1_b, gate=gate, up=up, hid=hid, pre2=pre2)
    return (x2, x2_b), saved


def _layer_backward(dx2, sv, wts, small, nb, s, tag, hooks=None, ffn_grads_ready=None):
    t = nb * s
    dpre2, dpre2_b, dgb2 = _ln_bwd(dx2, sv["pre2"], small["ln2_g"], f"ln2_bwd_{tag}")
    carry, finish = _take(hooks, "ffn_out_dx")
    (dgate, dup), carried = _ffn_out_dx(dpre2_b, wts["w_down"], sv["gate"], sv["up"], f"ffn_out_dx_{tag}", carry)
    finish(carried)
    dw_down = _matmul(sv["hid"], dpre2_b, "ffn_out_dw", tag, trans_a=True)
    dx1 = _ffn_in_dx(dgate, dup, wts["w_gu"], dpre2, f"ffn_in_dx_{tag}")
    x1_b = sv["x1_b"]
    dw_gate = _matmul(x1_b, dgate, "ffn_in_dw", f"{tag}_gate", trans_a=True)
    dw_up = _matmul(x1_b, dup, "ffn_in_dw", f"{tag}_up", trans_a=True)

    dpre1, dpre1_b, dgb1 = _ln_bwd(dx1, sv["pre1"], small["ln1_g"], f"ln1_bwd_{tag}")
    dmixed = _matmul(dpre1_b, wts["w_out"], "out_proj_dx", tag, trans_b=True)
    dw_out = _matmul(sv["mixed"], dpre1_b, "out_proj_dw", tag, trans_a=True)
    if ffn_grads_ready:
        ffn_grads_ready(dict(w_down=dw_down, w_gate=dw_gate, w_up=dw_up, w_out=dw_out))
    proj = sv["proj"]

    carry, finish = _take(hooks, "sb_bwd")
    (dq_sb, dk_sb, dv_sb), carried = _sb_bwd(proj, dmixed, sv["tails_sb"], nb, s, f"sb_bwd_{tag}", carry)
    finish(carried)

    delta_dl = _delta_kernel(dmixed, sv["o_dl"], nb, s, f"dil_delta_{tag}")
    carry, finish = _take(hooks, "dil_bwd")
    (dq_dl, dk_dl, dv_dl, gbias), carried = _dil_attention_bwd(proj, dmixed, sv["lse_dl"], delta_dl, sv["bias"], nb, s,
                                                               f"dil_bwd_{tag}", carry)
    finish(carried)
    drel = _bucket_reduce(gbias, jnp.asarray(_bucket_table()), f"rel_bias_grad_{tag}")

    carry, finish = _take(hooks, "fox_bwd")
    (dq_fx, dk_fx, dv_fx, dcol), carried = _fox_bwd(proj, dmixed, sv["lse_fx"], sv["ccol"], sv["crow"], nb, s,
                                                    f"fox_bwd_{tag}", carry)
    finish(carried)
    dcs = -jnp.swapaxes(dcol[:, :, :, 0, :].reshape(nb, N_HEADS, s), 1, 2).reshape(t, N_HEADS)
    dcs = jnp.pad(dcs, ((0, 0), (0, BLK - N_HEADS)))
    dfx, dfb = _fox_gates_bwd(dcs, proj, sv["fb"], nb, s, f"fox_gates_bwd_{tag}")

    dgates, dcw = _conv_bwd(dmixed, proj, sv["cw"], nb, s, f"conv_bwd_{tag}")

    dproj = jnp.concatenate([dq_sb, dk_sb, dv_sb, dq_dl, dk_dl, dv_dl, dq_fx, dk_fx, dv_fx, dgates, dfx],
                            axis=-1).astype(MXU_DTYPE)
    dx = _matmul(dproj, wts["w_in"], "proj_dx", tag, add=dpre1, add_scale=ALPHA, trans_b=True)
    dw_in = _matmul(sv["x_b"], dproj, "proj_dw", tag, trans_a=True)

    grads = dict(w_in=dw_in[:, :PROJ], w_out=dw_out, w_gate=dw_gate, w_up=dw_up, w_down=dw_down,
                 ln1_g=dgb1[0], ln1_b=dgb1[1], ln2_g=dgb2[0], ln2_b=dgb2[1], conv_w=dcw[:3], f_bias=dfb[0, :N_HEADS],
                 rel_bias=drel[:N_HEADS, :REL_BUCKETS].T)
    return dx, grads


class _NoExchanges:
    def forward_hooks(self, layer):
        return None

    def backward_hooks(self, layer):
        return None

    def ffn_grads_ready(self, layer):
        return None

    def layer_done(self, layer, grads):
        pass


def _local_step(x, target, weights_of, small_all, schedule=None):
    schedule = schedule or _NoExchanges()
    nb, s, d = x.shape
    h = x.reshape(nb * s, d)
    h_b = h.astype(MXU_DTYPE)
    saved = []
    for layer in range(DEPTH):
        wts = weights_of(layer)
        (h, h_b), sv = _layer_forward(h, h_b, wts, small_all[layer], nb, s, f"l{layer}", schedule.forward_hooks(layer))
        saved.append((sv, wts))
    dy, lossp = _loss_kernel(h, target.reshape(nb * s, d), "loss")
    grads = [None] * DEPTH
    for layer in reversed(range(DEPTH)):
        sv, wts = saved[layer]
        dy, grads[layer] = _layer_backward(dy, sv, wts, small_all[layer], nb, s, f"l{layer}",
                                           schedule.backward_hooks(layer), schedule.ffn_grads_ready(layer))
        schedule.layer_done(layer, grads[layer])
    return lossp, dy.reshape(nb, s, d), grads


_BIG = ("w_in", "w_out", "w_gate", "w_up", "w_down")
_COL_SHARDED = ("w_in", "w_gate", "w_up")


class _Schedule:
    def __init__(self, shards, chip, core):
        self.chip, self.core, self.reduces = chip, core, [[] for _ in range(DEPTH)]
        first = _WeightPrefetch(["w_in"], shards, 0, chip)
        first.run("l0_w_in")
        rest = _WeightPrefetch(["w_out", "w_gate", "w_up", "w_down"], shards, 0, chip)
        ahead_a = _WeightPrefetch(["w_in", "w_out", "w_down"], shards, 1, chip)
        ahead_b = _WeightPrefetch(["w_gate", "w_up"], shards, 1, chip)
        self.fetches = [[first, rest], [ahead_a, ahead_b]]
        self.forward, self.backward = [{} for _ in range(DEPTH)], [{} for _ in range(DEPTH)]
        rest.ride(self.forward[0], "sb_fwd", "fox_fwd")
        ahead_a.ride(self.forward[0], "dil_fwd", "ffn_in")
        ahead_b.ride(self.forward[0], "fox_fwd", "ffn_in")

    def weights(self, layer):
        def gathered(name):
            return next(f.result[name] for f in self.fetches[layer] if name in f.names)
        return _LayerWeights(gathered)

    def forward_hooks(self, layer):
        return self.forward[layer]

    def backward_hooks(self, layer):
        return self.backward[layer]

    def _reduce(self, layer, grads, tag):
        red = _GradReduce({name: _by_chip(name, g) for name, g in grads.items()}, self.chip, self.core, tag)
        self.reduces[layer].append(red)
        return red

    def ffn_grads_ready(self, layer):
        if layer != 0:
            return None

        def ready(early):
            self._reduce(0, early, "l0_early").ride(self.backward[0], "sb_bwd", "dil_bwd", "fox_bwd")

        return ready

    def layer_done(self, layer, grads):
        if layer == 1:
            self._reduce(1, {name: grads[name] for name in _BIG}, "l1").ride(self.backward[0], "ffn_out_dx", "sb_bwd", "fox_bwd")
        else:
            self._reduce(0, dict(w_in=grads["w_in"]), "l0_w_in").run()

    def reduced(self, layer, name):
        return next(r.result[name] for r in self.reduces[layer] if name in r.names)


def _by_chip(name, g):
    if name in _COL_SHARDED:
        return jnp.swapaxes(g.reshape(g.shape[0], N_CHIPS, -1), 0, 1)
    return g.reshape(N_CHIPS, -1, g.shape[1])


_SMALL_LAYOUT = (("ln1_g", 0), ("ln1_b", 2), ("ln2_g", 4), ("ln2_b", 6), ("conv_w", 8))
_ROW_MISC = 10
_ROW_LOSS = 11


def _pack_small(per_layer, rel_bias, loss=None):
    pk = jnp.zeros((SMALL_ROWS, D_MODEL), F32)
    for name, row in _SMALL_LAYOUT:
        for l in range(DEPTH):
            v = per_layer[l][name].reshape(-1)
            pk = pk.at[row + l, :v.shape[0]].set(v)
    fb = jnp.concatenate([per_layer[l]["f_bias"] for l in range(DEPTH)])
    pk = pk.at[_ROW_MISC, :2 * N_HEADS].set(fb)
    pk = pk.at[_ROW_MISC, BLK:BLK + REL_BUCKETS * N_HEADS].set(rel_bias.reshape(-1))
    if loss is not None:
        pk = pk.at[_ROW_LOSS, 0].set(loss)
    return pk


def _unpack_small(pk, conv_cols):
    out = {}
    for name, row in _SMALL_LAYOUT:
        n = 3 * conv_cols if name == "conv_w" else D_MODEL
        v = pk[row:row + DEPTH, :n]
        out[name] = v.reshape(DEPTH, 3, conv_cols) if name == "conv_w" else v
    out["f_bias"] = pk[_ROW_MISC, :2 * N_HEADS].reshape(DEPTH, N_HEADS)
    out["rel_bias"] = pk[_ROW_MISC, BLK:BLK + REL_BUCKETS * N_HEADS].reshape(REL_BUCKETS, N_HEADS)
    return out


_WEIGHTS = ("w_in", "f_bias", "conv_w", "w_out", "rel_bias", "ln1_g", "ln1_b", "w_gate", "w_up", "w_down", "ln2_g", "ln2_b")


def kernel(x, w_in, f_bias, conv_w, w_out, rel_bias, ln1_g, ln1_b, w_gate, w_up, w_down, ln2_g, ln2_b, loss_target, m_w_in, m_f_bias, m_conv_w, m_w_out, m_rel_bias, m_ln1_g, m_ln1_b, m_w_gate, m_w_up, m_w_down, m_ln2_g, m_ln2_b, v_w_in, v_f_bias, v_conv_w, v_w_out, v_rel_bias, v_ln1_g, v_ln1_b, v_w_gate, v_w_up, v_w_down, v_ln2_g, v_ln2_b):
    w = dict(w_in=w_in, f_bias=f_bias, conv_w=conv_w, w_out=w_out, rel_bias=rel_bias, ln1_g=ln1_g, ln1_b=ln1_b,
             w_gate=w_gate, w_up=w_up, w_down=w_down, ln2_g=ln2_g, ln2_b=ln2_b)
    m = dict(w_in=m_w_in, f_bias=m_f_bias, conv_w=m_conv_w, w_out=m_w_out, rel_bias=m_rel_bias, ln1_g=m_ln1_g,
             ln1_b=m_ln1_b, w_gate=m_w_gate, w_up=m_w_up, w_down=m_w_down, ln2_g=m_ln2_g, ln2_b=m_ln2_b)
    v = dict(w_in=v_w_in, f_bias=v_f_bias, conv_w=v_conv_w, w_out=v_w_out, rel_bias=v_rel_bias, ln1_g=v_ln1_g,
             ln1_b=v_ln1_b, w_gate=v_w_gate, w_up=v_w_up, w_down=v_w_down, ln2_g=v_ln2_g, ln2_b=v_ln2_b)
    chip = 2 * lax.axis_index("x") + lax.axis_index("y")
    core = lax.axis_index("c")
    conv_shard = CONV_W // N_CHIPS

    schedule = _Schedule({name: w[name].astype(MXU_DTYPE) for name in _BIG}, chip, core)
    cw_pk = jnp.zeros((8, D_MODEL), F32).at[0, :DEPTH * 3 * conv_shard].set(conv_w.reshape(-1))
    cw_all, _ = _gather_small(cw_pk, "gather_conv_w")
    cw_chips = cw_all[0::2, 0, :DEPTH * 3 * conv_shard].reshape(N_CHIPS, DEPTH, 3, conv_shard)
    conv_full = jnp.moveaxis(cw_chips, 0, 2).reshape(DEPTH, 3, CONV_W)
    small_all = [dict(f_bias=f_bias[l], conv_w=conv_full[l], rel_bias=rel_bias, ln1_g=ln1_g[l], ln1_b=ln1_b[l],
                      ln2_g=ln2_g[l], ln2_b=ln2_b[l]) for l in range(DEPTH)]

    lossp, grad_x, grads = _local_step(x, loss_target, schedule.weights, small_all, schedule)
    big_g = {name: jnp.stack([schedule.reduced(l, name) for l in range(DEPTH)]) for name in _BIG}

    drel = grads[0]["rel_bias"] + grads[1]["rel_bias"]
    small_pk = _pack_small(grads, drel, lossp[0, 0])
    _, small_sum = _gather_small(small_pk, "gather_small_grads")
    loss = small_sum[_ROW_LOSS, 0]
    small_g = _unpack_small(small_sum, CONV_W)
    small_g["conv_w"] = lax.dynamic_slice_in_dim(small_g["conv_w"], chip * conv_shard, conv_shard, axis=2)

    out_g, out_d, out_m, out_v = dict(small_g), {}, {}, {}
    for name in _BIG:
        out_g[name] = big_g[name]
        out_d[name], out_m[name], out_v[name] = _adamw(w[name], big_g[name], m[name], v[name], f"adamw_{name}")
    as_3d = lambda t: t if t.ndim == 3 else t[None]
    for name in _WEIGHTS:
        if name not in _BIG:
            stepped = _adamw(as_3d(w[name]), as_3d(small_g[name]), as_3d(m[name]), as_3d(v[name]), f"adamw_{name}")
            out_d[name], out_m[name], out_v[name] = (t.reshape(w[name].shape) for t in stepped)

    return (loss, grad_x, *[out_g[n] for n in _WEIGHTS], *[out_d[n] for n in _WEIGHTS],
            *[out_m[n] for n in _WEIGHTS], *[out_v[n] for n in _WEIGHTS])
```

```python
import functools
import math

import numpy as np
import jax
import jax.numpy as jnp
from jax import lax
from jax.experimental import pallas as pl
from jax.experimental.pallas import tpu as pltpu

F32 = jnp.float32
BF16 = jnp.bfloat16
MXU_DTYPE = BF16

D_MODEL = 1024
HEAD_DIM = 64
N_HEADS = 4
BLK = 128
ATT = 256
QT = 512
CONV_W = 256
PROJ = 3076
PROJ_PAD = 3200
D_FF = 2816
DEPTH = 2
ALPHA = (2 * DEPTH) ** 0.25
LN_EPS = 1e-5
NEG = -1e30
DIL_PATTERNS = ((128, 1), (512, 4), (2048, 16))
REL_BUCKETS = 32
N_CHIPS = 4
N_DEV = 8
SMALL_ROWS = 16

ADAM_LR = 0.001
ADAM_B1 = 0.9
ADAM_B2 = 0.999
ADAM_EPS = 1e-08
ADAM_WD = 0.01
ADAM_STEP = 10

VMEM_LIMIT = 56 * 2 ** 20
MESH = pl.DeviceIdType.MESH


def _cparams(*sem):
    return pltpu.CompilerParams(dimension_semantics=tuple(sem), vmem_limit_bytes=VMEM_LIMIT)


def _dot(a, b):
    return jnp.dot(a.astype(MXU_DTYPE), b.astype(MXU_DTYPE), preferred_element_type=F32)


def _dot_nt(a, b):
    return lax.dot_general(a.astype(MXU_DTYPE), b.astype(MXU_DTYPE), (((1,), (1,)), ((), ())),
                           preferred_element_type=F32)


def _dot_tn(a, b):
    return lax.dot_general(a.astype(MXU_DTYPE), b.astype(MXU_DTYPE), (((0,), (0,)), ((), ())),
                           preferred_element_type=F32)


def _split_dot(x, ones, passes):
    acc, rest = None, x
    for p in range(passes):
        piece = rest.astype(MXU_DTYPE)
        part = jnp.dot(piece, ones, preferred_element_type=F32)
        acc = part if acc is None else acc + part
        if p + 1 < passes:
            rest = rest - piece.astype(F32)
    return acc


def _split_dot_lhs(ones, x, passes):
    acc, rest = None, x
    for p in range(passes):
        piece = rest.astype(MXU_DTYPE)
        part = jnp.dot(ones, piece, preferred_element_type=F32)
        acc = part if acc is None else acc + part
        if p + 1 < passes:
            rest = rest - piece.astype(F32)
    return acc


def _iota2(shape, axis):
    return lax.broadcasted_iota(jnp.int32, shape, axis)


_TILES = {"proj": (1024, 640, 1024), "ffn_out_dw": (1408, 1024, 2048),
          "ffn_in_dw": (1024, 1408, 2048), "out_proj_dx": (1024, 1024, 1024),
          "out_proj_dw": (1024, 1024, 2048), "proj_dx": (1024, 512, 3200), "proj_dw": (1024, 640, 2048)}


def _matmul(a, b, kind, tag, *, out_dtype=F32, add=None, add_scale=1.0, trans_a=False, trans_b=False):
    k, m = a.shape if trans_a else a.shape[::-1]
    n = b.shape[0] if trans_b else b.shape[1]
    tm, tn, tk = _TILES[kind]
    tm, tk, name = min(tm, m), min(tk, k), f"{kind}_{tag}"
    assert m % tm == 0 and n % tn == 0 and k % tk == 0, (a.shape, b.shape, tm, tn, tk)
    nk = k // tk

    def body(*refs):
        if add is None:
            a_ref, b_ref, o_ref = refs[:3]
            c_ref, scr = None, refs[3:]
        else:
            a_ref, b_ref, c_ref, o_ref = refs[:4]
            scr = refs[4:]
        dot = _dot_tn if trans_a else _dot_nt if trans_b else _dot
        part = dot(a_ref[...], b_ref[...])

        def finish(acc):
            if c_ref is not None:
                acc = acc + add_scale * c_ref[...]
            o_ref[...] = acc.astype(out_dtype)

        if nk == 1:
            finish(part)
        else:
            acc_ref = scr[0]
            kk = pl.program_id(2)

            @pl.when(kk == 0)
            def _():
                acc_ref[...] = part

            @pl.when(kk > 0)
            def _():
                acc_ref[...] += part

            @pl.when(kk == nk - 1)
            def _():
                finish(acc_ref[...])

    b_spec = pl.BlockSpec((tn, tk), lambda i, j, kk: (j, kk)) if trans_b else pl.BlockSpec((tk, tn), lambda i, j, kk: (kk, j))
    a_spec = pl.BlockSpec((tk, tm), lambda i, j, kk: (kk, i)) if trans_a else pl.BlockSpec((tm, tk), lambda i, j, kk: (i, kk))
    in_specs = [a_spec, b_spec]
    operands = [a, b]
    if add is not None:
        in_specs.append(pl.BlockSpec((tm, tn), lambda i, j, kk: (i, j)))
        operands.append(add)
    return pl.pallas_call(
        body, name=name, grid=(m // tm, n // tn, nk), in_specs=in_specs,
        out_specs=pl.BlockSpec((tm, tn), lambda i, j, kk: (i, j)),
        out_shape=jax.ShapeDtypeStruct((m, n), out_dtype),
        scratch_shapes=[pltpu.VMEM((tm, tn), F32)] if nk > 1 else [],
        compiler_params=_cparams("parallel", "parallel", "arbitrary"),
    )(*operands)


def _matmul_post_norm(a, b, xin, g, beta, name):
    t, k = a.shape
    d = b.shape[1]
    tm = 512

    def body(a_ref, b_ref, x_ref, g_ref, beta_ref, pre_ref, y_ref, yb_ref):
        pre = ALPHA * x_ref[...] + _dot(a_ref[...], b_ref[...])
        xhat, _ = _ln_stats(pre)
        y = xhat * g_ref[...] + beta_ref[...]
        pre_ref[...] = pre
        y_ref[...] = y
        yb_ref[...] = y.astype(yb_ref.dtype)

    row = pl.BlockSpec((tm, d), lambda i: (i, 0))
    vec = pl.BlockSpec((1, d), lambda i: (0, 0))
    return pl.pallas_call(
        body, name=name, grid=(t // tm,),
        in_specs=[pl.BlockSpec((tm, k), lambda i: (i, 0)), pl.BlockSpec((k, d), lambda i: (0, 0)), row, vec, vec],
        out_specs=[row, row, row],
        out_shape=[jax.ShapeDtypeStruct((t, d), F32)] * 2 + [jax.ShapeDtypeStruct((t, d), MXU_DTYPE)],
        compiler_params=_cparams("parallel"),
    )(a, b, xin, g.reshape(1, d), beta.reshape(1, d))


def _ffn_in(x1, w_gu, name, carry=None):
    t, d = x1.shape
    tm, tn = 512, D_FF // 2
    nj = D_FF // tn

    def body(x_ref, wg_ref, wu_ref, gate_ref, up_ref, h_ref):
        xb = x_ref[...].astype(MXU_DTYPE)
        gate = _dot(xb, wg_ref[...])
        up = _dot(xb, wu_ref[...])
        gate_ref[...] = gate
        up_ref[...] = up
        h_ref[...] = (gate * (1.0 / (1.0 + jnp.exp(-gate))) * up).astype(h_ref.dtype)

    out = pl.BlockSpec((tm, tn), lambda i, j: (i, j))
    return _host_call(
        body, carry, name=name, grid=(t // tm, nj),
        in_specs=[pl.BlockSpec((tm, d), lambda i, j: (i, 0)), pl.BlockSpec((d, tn), lambda i, j: (0, j)),
                  pl.BlockSpec((d, tn), lambda i, j: (0, nj + j))],
        out_specs=[out, out, out],
        out_shape=[jax.ShapeDtypeStruct((t, D_FF), F32)] * 2 + [jax.ShapeDtypeStruct((t, D_FF), MXU_DTYPE)],
        operands=(x1, w_gu, w_gu))


def _ffn_out_dx(dy, w_down, gate, up, name, carry=None):
    t, d = dy.shape
    tm, tn = 512, D_FF // 2

    def body(dy_ref, w_ref, gate_ref, up_ref, dg_ref, du_ref):
        dh = _dot_nt(dy_ref[...], w_ref[...])
        gate = gate_ref[...]
        sig = 1.0 / (1.0 + jnp.exp(-gate))
        dg_ref[...] = (dh * up_ref[...] * sig * (1.0 + gate * (1.0 - sig))).astype(dg_ref.dtype)
        du_ref[...] = (dh * gate * sig).astype(du_ref.dtype)

    tile = pl.BlockSpec((tm, tn), lambda i, j: (i, j))
    return _host_call(
        body, carry, name=name, grid=(t // tm, D_FF // tn),
        in_specs=[pl.BlockSpec((tm, d), lambda i, j: (i, 0)), pl.BlockSpec((tn, d), lambda i, j: (j, 0)), tile, tile],
        out_specs=[tile, tile], out_shape=[jax.ShapeDtypeStruct((t, D_FF), MXU_DTYPE)] * 2,
        operands=(dy, w_down, gate, up))


def _ffn_in_dx(dgate, dup, w_gu, add, name):
    t = dgate.shape[0]
    d = w_gu.shape[0]
    tm, tk = 1024, D_FF // 2
    nk = D_FF // tk

    def body(dg_ref, du_ref, wg_ref, wu_ref, add_ref, o_ref, acc_ref):
        kk = pl.program_id(1)
        part = _dot_nt(dg_ref[...], wg_ref[...]) + _dot_nt(du_ref[...], wu_ref[...])

        @pl.when(kk == 0)
        def _():
            acc_ref[...] = part

        @pl.when(kk > 0)
        def _():
            acc_ref[...] += part

        @pl.when(kk == nk - 1)
        def _():
            o_ref[...] = acc_ref[...] + ALPHA * add_ref[...]

    act = pl.BlockSpec((tm, tk), lambda i, kk: (i, kk))
    row = pl.BlockSpec((tm, d), lambda i, kk: (i, 0))
    return pl.pallas_call(
        body, name=name, grid=(t // tm, nk),
        in_specs=[act, act, pl.BlockSpec((d, tk), lambda i, kk: (0, kk)), pl.BlockSpec((d, tk), lambda i, kk: (0, nk + kk)), row],
        out_specs=row, out_shape=jax.ShapeDtypeStruct((t, d), F32), scratch_shapes=[pltpu.VMEM((tm, d), F32)],
        compiler_params=_cparams("parallel", "arbitrary"),
    )(dgate, dup, w_gu, w_gu, add)


def _ln_stats(pre):
    mu = jnp.mean(pre, axis=-1, keepdims=True)
    xc = pre - mu
    var = jnp.mean(xc * xc, axis=-1, keepdims=True)
    rstd = lax.rsqrt(var + LN_EPS)
    return xc * rstd, rstd


def _ln_bwd(dy, pre, g, name):
    t, d = dy.shape
    tile = 256

    def body(dy_ref, pre_ref, g_ref, dpre_ref, dpre_b_ref, dgb_ref):
        dyv = dy_ref[...]
        xhat, rstd = _ln_stats(pre_ref[...])
        dxh = dyv * g_ref[...]
        m1 = jnp.mean(dxh, axis=-1, keepdims=True)
        m2 = jnp.mean(dxh * xhat, axis=-1, keepdims=True)
        dpre = rstd * (dxh - m1 - xhat * m2)
        dpre_ref[...] = dpre
        dpre_b_ref[...] = dpre.astype(dpre_b_ref.dtype)

        @pl.when(pl.program_id(0) == 0)
        def _():
            dgb_ref[...] = jnp.zeros_like(dgb_ref)

        dgb_ref[0:1, :] += jnp.sum(dyv * xhat, axis=0, keepdims=True)
        dgb_ref[1:2, :] += jnp.sum(dyv, axis=0, keepdims=True)

    row = pl.BlockSpec((tile, d), lambda i: (i, 0))
    return pl.pallas_call(
        body, name=name, grid=(t // tile,), in_specs=[row, row, pl.BlockSpec((1, d), lambda i: (0, 0))],
        out_specs=[row, row, pl.BlockSpec((8, d), lambda i: (0, 0))],
        out_shape=[jax.ShapeDtypeStruct((t, d), F32), jax.ShapeDtypeStruct((t, d), MXU_DTYPE), jax.ShapeDtypeStruct((8, d), F32)],
        compiler_params=_cparams("arbitrary"),
    )(dy, pre, g.reshape(1, d))


def _loss_kernel(y, target, name):
    t, d = y.shape
    tile = 512

    def body(y_ref, t_ref, dy_ref, l_ref):
        err = y_ref[...] - t_ref[...]
        dy_ref[...] = err * (1.0 / d)

        @pl.when(pl.program_id(0) == 0)
        def _():
            l_ref[...] = jnp.zeros_like(l_ref)

        l_ref[...] += jnp.sum(err * err) * (0.5 / d)

    row = pl.BlockSpec((tile, d), lambda i: (i, 0))
    return pl.pallas_call(
        body, name=name, grid=(t // tile,), in_specs=[row, row],
        out_specs=[row, pl.BlockSpec((8, 128), lambda i: (0, 0))],
        out_shape=[jax.ShapeDtypeStruct((t, d), F32), jax.ShapeDtypeStruct((8, 128), F32)],
        compiler_params=_cparams("arbitrary"),
    )(y, target)


def _adamw(w, g, m, v, name, carry=None):
    nl, r, c = w.shape
    tr = r
    for cand in (256, 352, 128, 64, 16, 8):
        if r % cand == 0:
            tr = cand
            break

    def body(w_ref, g_ref, m_ref, v_ref, d_ref, nm_ref, nv_ref):
        gv = g_ref[...]
        nm = ADAM_B1 * m_ref[...] + (1.0 - ADAM_B1) * gv
        nv = ADAM_B2 * v_ref[...] + (1.0 - ADAM_B2) * (gv * gv)
        m_hat = nm / (1.0 - ADAM_B1 ** ADAM_STEP)
        v_hat = nv / (1.0 - ADAM_B2 ** ADAM_STEP)
        d_ref[...] = -ADAM_LR * (m_hat / (jnp.sqrt(v_hat) + ADAM_EPS) + ADAM_WD * w_ref[...])
        nm_ref[...] = nm
        nv_ref[...] = nv

    blk = pl.BlockSpec((1, tr, c), lambda l, i: (l, i, 0))
    return _host_call(body, carry, name=name, grid=(nl, r // tr), in_specs=[blk] * 4, out_specs=[blk] * 3,
                      out_shape=[jax.ShapeDtypeStruct(w.shape, F32)] * 3, operands=(w, g, m, v))


def _shift_down(u, k, rows):
    return jnp.where(rows >= k, pltpu.roll(u, k, 0), 0.0)


def _shift_up(u, k, rows, s):
    return jnp.where(rows < s - k, pltpu.roll(u, s - k, 0), 0.0)


def _conv_fwd(proj, conv_w, nb, s, name):
    def body(b_ref, c_ref, h_ref, w_ref, o_ref):
        rows = _iota2((s, CONV_W), 0)
        u = c_ref[...] * h_ref[...]
        y = w_ref[2:3, :] * u + w_ref[1:2, :] * _shift_down(u, 1, rows) + w_ref[0:1, :] * _shift_down(u, 2, rows)
        o_ref[...] = b_ref[...] * y

    col = lambda j: pl.BlockSpec((s, CONV_W), lambda b: (b, j))
    return pl.pallas_call(
        body, name=name, grid=(nb,),
        in_specs=[col(9), col(10), col(11), pl.BlockSpec((8, CONV_W), lambda b: (0, 0))],
        out_specs=pl.BlockSpec((s, CONV_W), lambda b: (b, 0)),
        out_shape=jax.ShapeDtypeStruct((nb * s, CONV_W), F32), compiler_params=_cparams("parallel"),
    )(proj, proj, proj, conv_w)


def _conv_bwd(dmixed, proj, conv_w, nb, s, name):
    def body(do_ref, b_ref, c_ref, h_ref, w_ref, dg_ref, dw_ref):
        rows = _iota2((s, CONV_W), 0)
        cg, hg, bg, dout = c_ref[...], h_ref[...], b_ref[...], do_ref[...]
        u = cg * hg
        u1 = _shift_down(u, 1, rows)
        u2 = _shift_down(u, 2, rows)
        y = w_ref[2:3, :] * u + w_ref[1:2, :] * u1 + w_ref[0:1, :] * u2
        dy = dout * bg
        du = w_ref[2:3, :] * dy + w_ref[1:2, :] * _shift_up(dy, 1, rows, s) + w_ref[0:1, :] * _shift_up(dy, 2, rows, s)
        dg_ref[:, 0:CONV_W] = dout * y
        dg_ref[:, CONV_W:2 * CONV_W] = du * hg
        dg_ref[:, 2 * CONV_W:3 * CONV_W] = du * cg

        @pl.when(pl.program_id(0) == 0)
        def _():
            dw_ref[...] = jnp.zeros_like(dw_ref)

        dw_ref[0:1, :] += jnp.sum(dy * u2, axis=0, keepdims=True)
        dw_ref[1:2, :] += jnp.sum(dy * u1, axis=0, keepdims=True)
        dw_ref[2:3, :] += jnp.sum(dy * u, axis=0, keepdims=True)

    col = lambda j: pl.BlockSpec((s, CONV_W), lambda b: (b, j))
    return pl.pallas_call(
        body, name=name, grid=(nb,),
        in_specs=[col(3), col(9), col(10), col(11), pl.BlockSpec((8, CONV_W), lambda b: (0, 0))],
        out_specs=[pl.BlockSpec((s, 3 * CONV_W), lambda b: (b, 0)), pl.BlockSpec((8, CONV_W), lambda b: (0, 0))],
        out_shape=[jax.ShapeDtypeStruct((nb * s, 3 * CONV_W), F32), jax.ShapeDtypeStruct((8, CONV_W), F32)],
        compiler_params=_cparams("arbitrary"),
    )(dmixed, proj, proj, proj, conv_w)


def _col_spec(s, base):
    return pl.BlockSpec((s, BLK), lambda b, p: (b, base + p))


def _qrows(i):
    return pl.ds(pl.multiple_of(i * QT, QT), QT)


def _rows(j):
    return pl.ds(pl.multiple_of(j * ATT, ATT), ATT)


def _keys_upto(i):
    return (i + 1) * (QT // ATT)


def _triangle(keep):
    return keep(_iota2((ATT, ATT), 0), _iota2((ATT, ATT), 1)).astype(MXU_DTYPE)


def _rows128(i):
    return pl.ds(pl.multiple_of(i * BLK, BLK), BLK)


def _log_sigmoid_parts(z):
    e = jnp.exp(-jnp.abs(z))
    l1p = jnp.log(1.0 + e)
    lb = jnp.minimum(z, 0.0) - l1p
    return lb, lb - z, e


def _head_masks():
    lane = _iota2((1, BLK), 1)
    return [(lane >= h * HEAD_DIM) & (lane < (h + 1) * HEAD_DIM) for h in range(2)]


def _split_heads(ref, scr, sels):
    for h, sel in enumerate(sels):
        scr[h] = jnp.where(sel, ref[...], 0.0).astype(MXU_DTYPE)


def _sb_fwd(proj, nb, s, name, carry=None):
    def body(q_ref, k_ref, v_ref, o_ref, tails_ref, km, vm):
        sels = _head_masks()
        _split_heads(k_ref, km, sels)
        _split_heads(v_ref, vm, sels)
        rows = _iota2((QT, ATT), 0)
        cols = _iota2((QT, ATT), 1)
        lane = _iota2((QT, BLK), 1)
        later = _triangle(lambda r, c: r > c)
        tails_ref[...] = jnp.zeros_like(tails_ref)

        def qblock(i, _):
            qi = (q_ref[_qrows(i), :] * 0.125).astype(MXU_DTYPE)

            def kblock(t, state):
                carries, acc = state
                j = _keys_upto(i) - 1 - t
                strict = (cols + (j * ATT - i * QT)) < rows
                out = []
                for h in range(2):
                    tails_ref[h, _qrows(i), :] = jnp.where(lane == j, carries[h], tails_ref[h, _qrows(i), :])
                    z = _dot_nt(qi, km[h, _rows(j), :])
                    lb, lr, _ = _log_sigmoid_parts(z)
                    lr = jnp.where(strict, lr, 0.0)
                    tail = _split_dot(lr, later, 2) + carries[h]
                    a = jnp.where(strict, jnp.exp(lb + tail), 0.0)
                    acc = acc + _dot(a, vm[h, _rows(j), :])
                    out.append(carries[h] + jnp.sum(lr, axis=-1, keepdims=True))
                return tuple(out), acc

            init = ((jnp.zeros((QT, 1), F32),) * 2, jnp.zeros((QT, BLK), F32))
            _, acc = lax.fori_loop(0, _keys_upto(i), kblock, init)
            o_ref[_qrows(i), :] = acc
            return 0

        lax.fori_loop(0, s // QT, qblock, 0)

    return _host_call(
        body, carry, name=name, grid=(nb, 2), in_specs=[_col_spec(s, 0), _col_spec(s, 2), _col_spec(s, 4)],
        out_specs=[_col_spec(s, 0), _pair_spec(s, BLK)],
        out_shape=[jax.ShapeDtypeStruct((nb * s, 2 * BLK), F32), jax.ShapeDtypeStruct((nb, N_HEADS, s, BLK), F32)],
        scratch_shapes=[pltpu.VMEM((2, s, BLK), MXU_DTYPE)] * 2, operands=(proj, proj, proj))


def _sb_bwd(proj, dmixed, tails, nb, s, name, carry=None):
    def body(q_ref, k_ref, v_ref, do_ref, tails_ref, dq_ref, dk_ref, dv_ref, km, vm):
        sels = _head_masks()
        _split_heads(k_ref, km, sels)
        _split_heads(v_ref, vm, sels)
        rows = _iota2((QT, ATT), 0)
        cols = _iota2((QT, ATT), 1)
        lane = _iota2((QT, BLK), 1)
        later = _triangle(lambda r, c: r > c)
        earlier = _triangle(lambda r, c: r < c)
        dk_ref[...] = jnp.zeros_like(dk_ref)
        dv_ref[...] = jnp.zeros_like(dv_ref)

        def qblock(i, _):
            qi = (q_ref[_qrows(i), :] * 0.125).astype(MXU_DTYPE)
            doi = do_ref[_qrows(i), :].astype(MXU_DTYPE)
            qm = [jnp.where(sel, qi, 0.0) for sel in sels]
            dom = [jnp.where(sel, doi, 0.0) for sel in sels]
            tails_i = [tails_ref[h, _qrows(i), :] for h in range(2)]

            def kblock(j, state):
                csums, dq = state
                strict = (cols + (j * ATT - i * QT)) < rows
                out = []
                for h in range(2):
                    z = _dot_nt(qi, km[h, _rows(j), :])
                    lb, lr, _ = _log_sigmoid_parts(z)
                    lr = jnp.where(strict, lr, 0.0)
                    after = jnp.sum(jnp.where(lane == j, tails_i[h], 0.0), axis=-1, keepdims=True)
                    a = jnp.where(strict, jnp.exp(lb + _split_dot(lr, later, 2) + after), 0.0)
                    dl = a * _dot_nt(doi, vm[h, _rows(j), :])
                    beta = jnp.exp(lb)
                    before = _split_dot(dl, earlier, 2) + csums[h]
                    dz = jnp.where(strict, dl * (1.0 - beta) - beta * before, 0.0).astype(MXU_DTYPE)
                    dq = dq + _dot(dz, km[h, _rows(j), :])
                    dk_ref[_rows(j), :] += _dot_tn(dz, qm[h])
                    dv_ref[_rows(j), :] += _dot_tn(a, dom[h])
                    out.append(csums[h] + jnp.sum(dl, axis=-1, keepdims=True))
                return tuple(out), dq

            init = ((jnp.zeros((QT, 1), F32),) * 2, jnp.zeros((QT, BLK), F32))
            _, dq = lax.fori_loop(0, _keys_upto(i), kblock, init)
            dq_ref[_qrows(i), :] = dq * 0.125
            return 0

        lax.fori_loop(0, s // QT, qblock, 0)

    out = _col_spec(s, 0)
    return _host_call(
        body, carry, name=name, grid=(nb, 2),
        in_specs=[_col_spec(s, 0), _col_spec(s, 2), _col_spec(s, 4), out, _pair_spec(s, BLK)], out_specs=[out] * 3,
        out_shape=[jax.ShapeDtypeStruct((nb * s, 2 * BLK), F32)] * 3,
        scratch_shapes=[pltpu.VMEM((2, s, BLK), MXU_DTYPE)] * 2, operands=(proj, proj, proj, dmixed, tails))


def _pair_spec(s, width):
    return pl.BlockSpec((None, 2, s, width), lambda b, p: (b, p, 0, 0))


def _fox_fwd(proj, ccol, crow, nb, s, name, carry=None):
    nblk = s // ATT

    def body(q_ref, k_ref, v_ref, cc_ref, cr_ref, o_ref, lse_ref, km, vm):
        sels = _head_masks()
        _split_heads(k_ref, km, sels)
        _split_heads(v_ref, vm, sels)
        rows = _iota2((QT, ATT), 0)
        cols = _iota2((QT, ATT), 1)

        def qblock(i, _):
            qi = (q_ref[_qrows(i), :] * 0.125).astype(MXU_DTYPE)
            ci = [cc_ref[h, _qrows(i), :] for h in range(2)]

            def kblock(j, state):
                ms, ls, acc = state
                causal = (cols + (j * ATT - i * QT)) <= rows
                new_m, new_l, scales, parts = [], [], [], []
                for h in range(2):
                    z = _dot_nt(qi, km[h, _rows(j), :]) + (ci[h] - cr_ref[h, j][0:1, :])
                    z = jnp.where(causal, z, NEG)
                    m_new = jnp.maximum(ms[h], jnp.max(z, axis=-1, keepdims=True))
                    p = jnp.exp(z - m_new)
                    scale = jnp.exp(ms[h] - m_new)
                    new_m.append(m_new)
                    new_l.append(scale * ls[h] + jnp.sum(p, axis=-1, keepdims=True))
                    scales.append(scale)
                    parts.append(_dot(p, vm[h, _rows(j), :]))
                acc = jnp.where(sels[0], scales[0], scales[1]) * acc + parts[0] + parts[1]
                return tuple(new_m), tuple(new_l), acc

            init = ((jnp.full((QT, 1), NEG, F32),) * 2, (jnp.zeros((QT, 1), F32),) * 2, jnp.zeros((QT, BLK), F32))
            ms, ls, acc = lax.fori_loop(0, _keys_upto(i), kblock, init)
            o_ref[_qrows(i), :] = acc / jnp.where(sels[0], ls[0], ls[1])
            for h in range(2):
                lse_ref[h, _qrows(i), :] = jnp.broadcast_to(ms[h] + jnp.log(ls[h]), (QT, ATT))
            return 0

        lax.fori_loop(0, s // QT, qblock, 0)

    crow_spec = pl.BlockSpec((None, 2, nblk, 8, ATT), lambda b, p: (b, p, 0, 0, 0))
    return _host_call(
        body, carry, name=name, grid=(nb, 2),
        in_specs=[_col_spec(s, 12), _col_spec(s, 14), _col_spec(s, 16), _pair_spec(s, ATT), crow_spec],
        out_specs=[_col_spec(s, 0), _pair_spec(s, ATT)],
        out_shape=[jax.ShapeDtypeStruct((nb * s, 2 * BLK), F32), jax.ShapeDtypeStruct((nb, N_HEADS, s, ATT), F32)],
        scratch_shapes=[pltpu.VMEM((2, s, BLK), MXU_DTYPE)] * 2, operands=(proj, proj, proj, ccol, crow))


def _fox_bwd(proj, dmixed, lse, ccol, crow, nb, s, name, carry=None):
    nblk = s // ATT

    def body(q_ref, k_ref, v_ref, do_ref, lse_ref, cc_ref, cr_ref, dq_ref, dk_ref, dv_ref, dc_ref, km, vm, p_scr, dp_scr):
        sels = _head_masks()
        _split_heads(k_ref, km, sels)
        _split_heads(v_ref, vm, sels)
        rows = _iota2((QT, ATT), 0)
        cols = _iota2((QT, ATT), 1)
        dk_ref[...] = jnp.zeros_like(dk_ref)
        dv_ref[...] = jnp.zeros_like(dv_ref)
        dc_ref[...] = jnp.zeros_like(dc_ref)

        def qblock(i, _):
            qi = (q_ref[_qrows(i), :] * 0.125).astype(MXU_DTYPE)
            doi = do_ref[_qrows(i), :].astype(MXU_DTYPE)
            qm = [jnp.where(sel, qi, 0.0) for sel in sels]
            dom = [jnp.where(sel, doi, 0.0) for sel in sels]
            ci = [cc_ref[h, _qrows(i), :] for h in range(2)]
            lsei = [lse_ref[h, _qrows(i), :] for h in range(2)]

            def probs(j, h):
                z = _dot_nt(qi, km[h, _rows(j), :]) + (ci[h] - cr_ref[h, j][0:1, :])
                p = jnp.where((cols + (j * ATT - i * QT)) <= rows, jnp.exp(z - lsei[h]), 0.0)
                return p, _dot_nt(doi, vm[h, _rows(j), :])

            def row_term(j, accs):
                out = []
                for h in range(2):
                    p, dp = probs(j, h)
                    p_scr[h, j] = p
                    dp_scr[h, j] = dp
                    out.append(accs[h] + jnp.sum(p * dp, axis=-1, keepdims=True))
                return tuple(out)

            di = lax.fori_loop(0, _keys_upto(i), row_term, (jnp.zeros((QT, 1), F32),) * 2)

            def kblock(j, dq):
                for h in range(2):
                    p = p_scr[h, j]
                    ds = p * (dp_scr[h, j] - di[h])
                    dc_ref[h, j] += jnp.broadcast_to(jnp.sum(ds, axis=0, keepdims=True), (8, ATT))
                    ds = ds.astype(MXU_DTYPE)
                    dk_ref[_rows(j), :] += _dot_tn(ds, qm[h])
                    dv_ref[_rows(j), :] += _dot_tn(p, dom[h])
                    dq = dq + _dot(ds, km[h, _rows(j), :])
                return dq

            dq = lax.fori_loop(0, _keys_upto(i), kblock, jnp.zeros((QT, BLK), F32))
            dq_ref[_qrows(i), :] = dq * 0.125
            return 0

        lax.fori_loop(0, s // QT, qblock, 0)

    crow_spec = pl.BlockSpec((None, 2, nblk, 8, ATT), lambda b, p: (b, p, 0, 0, 0))
    wide, cols_out = _pair_spec(s, ATT), _col_spec(s, 0)
    return _host_call(
        body, carry, name=name, grid=(nb, 2),
        in_specs=[_col_spec(s, 12), _col_spec(s, 14), _col_spec(s, 16), _col_spec(s, 4), wide, wide, crow_spec],
        out_specs=[cols_out, cols_out, cols_out, crow_spec],
        out_shape=[jax.ShapeDtypeStruct((nb * s, 2 * BLK), F32)] * 3 + [jax.ShapeDtypeStruct((nb, N_HEADS, nblk, 8, ATT), F32)],
        scratch_shapes=[pltpu.VMEM((2, s, BLK), MXU_DTYPE)] * 2 + [pltpu.VMEM((2, nblk, QT, ATT), F32)] * 2,
        operands=(proj, proj, proj, dmixed, lse, ccol, crow))


def _fox_gates_fwd(proj, f_bias, nb, s, name):
    chunk = 256

    def body(f_ref, b_ref, c_ref):
        lower = (_iota2((chunk, chunk), 0) >= _iota2((chunk, chunk), 1)).astype(MXU_DTYPE)
        carry = jnp.zeros((1, BLK), F32)
        for n in range(s // chunk):
            rows = pl.ds(n * chunk, chunk)
            lf, _, _ = _log_sigmoid_parts(f_ref[rows, :] + b_ref[0:1, :])
            c = _split_dot_lhs(lower, lf, 3) + carry
            c_ref[rows, :] = c
            carry = c[chunk - 1:chunk, :]

    return pl.pallas_call(
        body, name=name, grid=(nb,),
        in_specs=[pl.BlockSpec((s, BLK), lambda b: (b, (PROJ_PAD - BLK) // BLK)), pl.BlockSpec((8, BLK), lambda b: (0, 0))],
        out_specs=pl.BlockSpec((s, BLK), lambda b: (b, 0)),
        out_shape=jax.ShapeDtypeStruct((nb * s, BLK), F32), compiler_params=_cparams("parallel"),
    )(proj, f_bias)


def _fox_gates_bwd(dc, proj, f_bias, nb, s, name):
    chunk = 256

    def body(dc_ref, f_ref, b_ref, df_ref, db_ref):
        upper = (_iota2((chunk, chunk), 0) <= _iota2((chunk, chunk), 1)).astype(MXU_DTYPE)
        carry = jnp.zeros((1, BLK), F32)
        total = jnp.zeros((1, BLK), F32)
        for n in reversed(range(s // chunk)):
            rows = pl.ds(n * chunk, chunk)
            dlf = _split_dot_lhs(upper, dc_ref[rows, :], 3) + carry
            carry = dlf[0:1, :]
            pre = f_ref[rows, :] + b_ref[0:1, :]
            e = jnp.exp(-jnp.abs(pre))
            df = dlf * (jnp.where(pre >= 0.0, e, 1.0) / (1.0 + e))
            df_ref[rows, :] = df
            total = total + jnp.sum(df, axis=0, keepdims=True)

        @pl.when(pl.program_id(0) == 0)
        def _():
            db_ref[...] = jnp.zeros_like(db_ref)

        db_ref[0:1, :] += total

    return pl.pallas_call(
        body, name=name, grid=(nb,),
        in_specs=[pl.BlockSpec((s, BLK), lambda b: (b, 0)), pl.BlockSpec((s, BLK), lambda b: (b, (PROJ_PAD - BLK) // BLK)),
                  pl.BlockSpec((8, BLK), lambda b: (0, 0))],
        out_specs=[pl.BlockSpec((s, BLK), lambda b: (b, 0)), pl.BlockSpec((8, BLK), lambda b: (0, 0))],
        out_shape=[jax.ShapeDtypeStruct((nb * s, BLK), F32), jax.ShapeDtypeStruct((8, BLK), F32)],
        compiler_params=_cparams("arbitrary"),
    )(dc, proj, f_bias)


def _delta_kernel(dmixed, o, nb, s, name):
    def body(do_ref, o_ref, d_ref):
        prod = do_ref[...] * o_ref[...]
        for h, sel in enumerate(_head_masks()):
            d_ref[h] = jnp.broadcast_to(jnp.sum(jnp.where(sel, prod, 0.0), axis=-1, keepdims=True), (s, BLK))

    return pl.pallas_call(
        body, name=name, grid=(nb, 2), in_specs=[_col_spec(s, 2), _col_spec(s, 0)], out_specs=_pair_spec(s, BLK),
        out_shape=jax.ShapeDtypeStruct((nb, N_HEADS, s, BLK), F32), compiler_params=_cparams("parallel", "parallel"),
    )(dmixed, o)


def _t5_bucket_np(dist):
    max_exact = REL_BUCKETS // 2
    nf = np.maximum(dist, 1).astype(np.float32)
    large = max_exact + (np.log(nf / max_exact) / math.log(2048 / max_exact) * (REL_BUCKETS - max_exact)).astype(np.int32)
    large = np.minimum(large, REL_BUCKETS - 1)
    return np.where(dist < max_exact, dist, large)


def _bucket_table():
    qi = np.arange(BLK)[:, None]
    kj = np.arange(2 * BLK)[None, :]
    dist = qi + BLK - kj
    tables = []
    for window, dil in DIL_PATTERNS:
        in_band = (dist >= 0) & (dist <= window // dil)
        tables.append(np.where(in_band, _t5_bucket_np(np.maximum(dist, 0) * dil), -1).astype(np.int32))
    return np.stack(tables)


def _dil_scores(qb, kp, kc, b_ref, h, prev_valid):
    zp = _dot_nt(qb, kp) + b_ref[h, :, 0:BLK]
    zp = jnp.where(prev_valid, zp, NEG)
    zc = _dot_nt(qb, kc) + b_ref[h, :, BLK:2 * BLK]
    return zp, zc


def _residue_rows(b, seg, dil):
    if dil == 1:
        return _rows128(b), _rows128(jnp.maximum(b - 1, 0)), b > 0
    r, n = b // seg, b % seg
    cur = pl.ds(r + dil * n * BLK, BLK, stride=dil)
    prev = pl.ds(r + dil * jnp.maximum(n - 1, 0) * BLK, BLK, stride=dil)
    return cur, prev, n > 0


def _dil_attention_fwd(proj, bias, nb, s, name, carry=None):
    nblk = s // BLK

    def body(q_ref, k_ref, v_ref, b_ref, out_ref, lse_ref, o_scr, l_scr):
        sels = _head_masks()
        for p, (_, dil) in enumerate(DIL_PATTERNS):
            seg = s // dil // BLK

            def block(b, _, p=p, seg=seg, dil=dil):
                cur, prev, has_prev = _residue_rows(b, seg, dil)
                qb = (q_ref[cur, :] * 0.125).astype(MXU_DTYPE)
                kp, kc = k_ref[prev, :].astype(MXU_DTYPE), k_ref[cur, :].astype(MXU_DTYPE)
                vp, vc = v_ref[prev, :].astype(MXU_DTYPE), v_ref[cur, :].astype(MXU_DTYPE)
                acc = jnp.zeros((BLK, BLK), F32)
                for h, sel in enumerate(sels):
                    zp, zc = _dil_scores(qb, jnp.where(sel, kp, 0.0), jnp.where(sel, kc, 0.0), b_ref.at[p], h, has_prev)
                    m = jnp.maximum(jnp.max(zp, axis=-1, keepdims=True), jnp.max(zc, axis=-1, keepdims=True))
                    pp = jnp.exp(zp - m)
                    pc = jnp.exp(zc - m)
                    den = jnp.sum(pp, axis=-1, keepdims=True) + jnp.sum(pc, axis=-1, keepdims=True)
                    acc = acc + (_dot(pp, jnp.where(sel, vp, 0.0)) + _dot(pc, jnp.where(sel, vc, 0.0))) / den
                    l_scr[p, h, cur, :] = jnp.broadcast_to(m + jnp.log(den), (BLK, BLK))
                o_scr[p, cur, :] = acc
                return 0

            lax.fori_loop(0, nblk, block, 0, unroll=4)

        weights, dens = [], []
        for h in range(2):
            m = jnp.maximum(jnp.maximum(l_scr[0, h], l_scr[1, h]), l_scr[2, h])
            w = [jnp.exp(l_scr[p, h] - m) for p in range(3)]
            den = w[0] + w[1] + w[2]
            lse_ref[h] = m + jnp.log(den)
            weights.append(w)
            dens.append(den)
        num = sum(jnp.where(sels[0], weights[0][p], weights[1][p]) * o_scr[p] for p in range(3))
        out_ref[...] = num / jnp.where(sels[0], dens[0], dens[1])

    bias_spec = pl.BlockSpec((3, 2, BLK, 2 * BLK), lambda b, p: (0, p, 0, 0))
    return _host_call(
        body, carry, name=name, grid=(nb, 2), in_specs=[_col_spec(s, 6), _col_spec(s, 8), _col_spec(s, 10), bias_spec],
        out_specs=[_col_spec(s, 0), _pair_spec(s, BLK)],
        out_shape=[jax.ShapeDtypeStruct((nb * s, 2 * BLK), F32), jax.ShapeDtypeStruct((nb, N_HEADS, s, BLK), F32)],
        scratch_shapes=[pltpu.VMEM((3, s, BLK), F32), pltpu.VMEM((3, 2, s, BLK), F32)], operands=(proj, proj, proj, bias))


def _dil_attention_bwd(proj, dmixed, lse, delta, bias, nb, s, name, carry=None):
    nblk = s // BLK

    def body(q_ref, k_ref, v_ref, do_ref, lse_ref, dl_ref, b_ref, dq_ref, dk_ref, dv_ref, g_ref):
        sels = _head_masks()
        dq_ref[...] = jnp.zeros_like(dq_ref)
        dk_ref[...] = jnp.zeros_like(dk_ref)
        dv_ref[...] = jnp.zeros_like(dv_ref)
        g_ref[...] = jnp.zeros_like(g_ref)
        for p, (_, dil) in enumerate(DIL_PATTERNS):
            seg = s // dil // BLK

            def block(b, _, p=p, seg=seg, dil=dil):
                cur, prev, has_prev = _residue_rows(b, seg, dil)
                qb = (q_ref[cur, :] * 0.125).astype(MXU_DTYPE)
                dob = do_ref[cur, :].astype(MXU_DTYPE)
                kp, kc = k_ref[prev, :].astype(MXU_DTYPE), k_ref[cur, :].astype(MXU_DTYPE)
                vp, vc = v_ref[prev, :].astype(MXU_DTYPE), v_ref[cur, :].astype(MXU_DTYPE)
                dq = jnp.zeros((BLK, BLK), F32)
                dkp, dkc, dvp, dvc = dq, dq, dq, dq
                for h, sel in enumerate(sels):
                    kph, kch = jnp.where(sel, kp, 0.0), jnp.where(sel, kc, 0.0)
                    qh, doh = jnp.where(sel, qb, 0.0), jnp.where(sel, dob, 0.0)
                    lse_h = lse_ref[h, cur, :]
                    dlt = dl_ref[h, cur, :]
                    zp, zc = _dil_scores(qb, kph, kch, b_ref.at[p], h, has_prev)
                    pp = jnp.exp(zp - lse_h)
                    pc = jnp.exp(zc - lse_h)
                    dsp = pp * (_dot_nt(dob, jnp.where(sel, vp, 0.0)) - dlt)
                    dsc = pc * (_dot_nt(dob, jnp.where(sel, vc, 0.0)) - dlt)
                    g_ref[h, p, :, 0:BLK] += dsp
                    g_ref[h, p, :, BLK:2 * BLK] += dsc
                    dsp = dsp.astype(MXU_DTYPE)
                    dsc = dsc.astype(MXU_DTYPE)
                    dq = dq + _dot(dsp, kph) + _dot(dsc, kch)
                    dkp, dkc = dkp + _dot_tn(dsp, qh), dkc + _dot_tn(dsc, qh)
                    dvp, dvc = dvp + _dot_tn(pp, doh), dvc + _dot_tn(pc, doh)
                dq_ref[cur, :] += dq * 0.125
                dk_ref[prev, :] += dkp
                dk_ref[cur, :] += dkc
                dv_ref[prev, :] += dvp
                dv_ref[cur, :] += dvc
                return 0

            lax.fori_loop(0, nblk, block, 0, unroll=4)

    bias_spec = pl.BlockSpec((3, 2, BLK, 2 * BLK), lambda b, p: (0, p, 0, 0))
    cols, stats = _col_spec(s, 0), _pair_spec(s, BLK)
    return _host_call(
        body, carry, name=name, grid=(nb, 2),
        in_specs=[_col_spec(s, 6), _col_spec(s, 8), _col_spec(s, 10), _col_spec(s, 2), stats, stats, bias_spec],
        out_specs=[cols, cols, cols, pl.BlockSpec((None, 2, 3, BLK, 2 * BLK), lambda b, p: (b, p, 0, 0, 0))],
        out_shape=[jax.ShapeDtypeStruct((nb * s, 2 * BLK), F32)] * 3 + [jax.ShapeDtypeStruct((nb, N_HEADS, 3, BLK, 2 * BLK), F32)],
        operands=(proj, proj, proj, dmixed, lse, delta, bias))


def _bucket_reduce(gbias, table, name):
    nb = gbias.shape[0]

    def body(g_ref, t_ref, o_ref):
        row = _iota2((8, BLK), 0)
        lane = _iota2((8, BLK), 1)
        gsum = [[sum(g_ref[b, h, p] for b in range(nb)) for p in range(3)] for h in range(N_HEADS)]

        def bucket(k, acc):
            for h in range(N_HEADS):
                tot = sum(jnp.sum(jnp.where(t_ref[p] == k, gsum[h][p], 0.0)) for p in range(3))
                acc = acc + jnp.where((row == h) & (lane == k), tot, 0.0)
            return acc

        o_ref[...] = lax.fori_loop(0, REL_BUCKETS, bucket, jnp.zeros((8, BLK), F32))

    vm = pl.BlockSpec(memory_space=pltpu.VMEM)
    return pl.pallas_call(
        body, name=name, in_specs=[vm, vm], out_specs=vm, out_shape=jax.ShapeDtypeStruct((8, BLK), F32),
        compiler_params=pltpu.CompilerParams(vmem_limit_bytes=VMEM_LIMIT),
    )(gbias, table)


def _place():
    x, y, c = lax.axis_index("x"), lax.axis_index("y"), lax.axis_index("c")
    others = [(1 - x, y), (x, 1 - y), (1 - x, 1 - y)]
    return x, y, c, others


def _remote(src, dst, send_sem, recv_sem, to):
    return pltpu.make_async_remote_copy(src_ref=src, dst_ref=dst, send_sem=send_sem, recv_sem=recv_sem,
                                        device_id=to, device_id_type=MESH)


_HBM = pl.BlockSpec(memory_space=pl.ANY)


class _Exchange:
    def __init__(self, operands, out_shape, n_copies, copies, aliases=None):
        self.operands, self.out_shape, self.n_copies, self.copies = list(operands), list(out_shape), n_copies, copies
        self.aliases = dict(aliases or {})

    def sem_shapes(self):
        return [pltpu.SemaphoreType.DMA((self.n_copies,)), pltpu.SemaphoreType.DMA((self.n_copies,))]


def _start_all(sends):
    for cp in sends:
        cp.start()


def _wait_all(sends, arrivals):
    for cp in arrivals:
        cp.wait_recv()
    for cp in sends:
        cp.wait_send()


def _run_exchange(ex, name):
    ni = len(ex.operands)

    def body(*refs):
        sends, arrivals = ex.copies(refs[:ni], refs[ni:-2], refs[-2], refs[-1])
        _start_all(sends)
        _wait_all(sends, arrivals)

    return list(pl.pallas_call(
        body, name=name, in_specs=[_HBM] * ni, out_specs=[_HBM] * len(ex.out_shape), out_shape=ex.out_shape,
        scratch_shapes=ex.sem_shapes(), input_output_aliases=ex.aliases)(*ex.operands))


def _host_call(body, carry, *, name, grid, in_specs, out_specs, out_shape, operands, scratch_shapes=()):
    in_specs, out_specs, out_shape, scratch_shapes = list(in_specs), list(out_specs), list(out_shape), list(scratch_shapes)
    if carry is None:
        res = pl.pallas_call(body, name=name, grid=grid, in_specs=in_specs, out_specs=out_specs, out_shape=out_shape,
                             scratch_shapes=scratch_shapes, compiler_params=_cparams(*["parallel"] * len(grid)))(*operands)
        return list(res), []
    n_in, n_out, n_scr, c_in, c_out = len(in_specs), len(out_specs), len(scratch_shapes), len(carry.operands), len(carry.out_shape)
    steps = math.prod(grid)

    def wrapped(*refs):
        ins, refs = refs[:n_in], refs[n_in:]
        c_ins, refs = refs[:c_in], refs[c_in:]
        outs, refs = refs[:n_out], refs[n_out:]
        c_outs, refs = refs[:c_out], refs[c_out:]
        scr, (send_sems, recv_sems) = refs[:n_scr], refs[n_scr:]
        step = 0
        for d, size in enumerate(grid):
            step = step * size + pl.program_id(d)

        @pl.when(step == 0)
        def _():
            _start_all(carry.copies(c_ins, c_outs, send_sems, recv_sems)[0])

        body(*ins, *outs, *scr)

        @pl.when(step == steps - 1)
        def _():
            _wait_all(*carry.copies(c_ins, c_outs, send_sems, recv_sems))

    res = pl.pallas_call(
        wrapped, name=name, grid=grid, in_specs=in_specs + [_HBM] * c_in, out_specs=out_specs + [_HBM] * c_out,
        out_shape=out_shape + carry.out_shape, scratch_shapes=scratch_shapes + carry.sem_shapes(),
        input_output_aliases={n_in + i: n_out + j for i, j in carry.aliases.items()},
        compiler_params=_cparams(*["arbitrary"] * len(grid)))(*operands, *carry.operands)
    return list(res[:n_out]), list(res[n_out:])


def _half(which, rows):
    h = rows // 2
    return pl.ds(pl.multiple_of(which * h, 16), h)


def _like(arrays, shape_of=lambda t: t.shape):
    return [jax.ShapeDtypeStruct(shape_of(t), t.dtype) for t in arrays]


def _gather_ici(shards, layer):
    n = len(shards)

    def copies(ins, outs, send_sems, recv_sems, base=0):
        x, y, c, others = _place()
        me = 2 * x + y
        sends, arrivals = [], []
        for a in range(n):
            rows = _half(c, shards[a].shape[1])
            for k, (ox, oy) in enumerate(others):
                sems = (send_sems.at[base + 3 * a + k], recv_sems.at[base + 3 * a + k],(ox, oy, c))
                sends.append(_remote(ins[a].at[layer, rows], outs[a].at[me, rows], *sems))
                landed = outs[a].at[2 * ox + oy, rows]
                arrivals.append(_remote(landed, landed, *sems))
        return sends, arrivals

    return _Exchange(shards, _like(shards, lambda t: (N_CHIPS,) + t.shape[1:]), 3 * n, copies)


def _gather_d2d(gathered):
    n = len(gathered)

    def copies(ins, outs, send_sems, recv_sems, base=0):
        x, y, c, others = _place()
        sends, arrivals = [], []
        for a in range(n):
            r = gathered[a].shape[1]
            for k, (ox, oy) in enumerate(others):
                sems = (send_sems.at[base + 3 * a + k], recv_sems.at[base + 3 * a + k],(x, y, 1 - c))
                mine, theirs = outs[a].at[2 * ox + oy, _half(c, r)], outs[a].at[2 * ox + oy, _half(1 - c, r)]
                sends.append(_remote(mine, mine, *sems))
                arrivals.append(_remote(theirs, theirs, *sems))
        return sends, arrivals

    return _Exchange(gathered, _like(gathered), 3 * n, copies, aliases={a: a for a in range(n)})


def _swap_halves(g):
    n = len(g)

    def copies(ins, outs, send_sems, recv_sems, base=0):
        x, y, c, _ = _place()
        sends, arrivals = [], []
        for a in range(n):
            sems = (send_sems.at[base + a], recv_sems.at[base + a], (x, y, 1 - c))
            sends.append(_remote(ins[a].at[:, _half(1 - c, g[a].shape[1])], outs[a], *sems))
            arrivals.append(_remote(outs[a], outs[a], *sems))
        return sends, arrivals

    return _Exchange(g, _like(g, lambda t: (t.shape[0], t.shape[1] // 2, t.shape[2])), n, copies)


def _scatter_shards(ps):
    n = len(ps)

    def copies(ins, outs, send_sems, recv_sems, base=0):
        x, y, c, others = _place()
        me = 2 * x + y
        sends, arrivals = [], []
        for a in range(n):
            for k, (ox, oy) in enumerate(others):
                sems = (send_sems.at[base + 3 * a + k], recv_sems.at[base + 3 * a + k],(ox, oy, c))
                sends.append(_remote(ins[a].at[2 * ox + oy], outs[a].at[me], *sems))
                slot = outs[a].at[2 * ox + oy]
                arrivals.append(_remote(slot, slot, *sems))
        return sends, arrivals

    return _Exchange(ps, _like(ps), 3 * n, copies)


def _share_halves(mine):
    n = len(mine)

    def copies(ins, outs, send_sems, recv_sems, base=0):
        x, y, c, _ = _place()
        sends, arrivals = [], []
        for a in range(n):
            sems = (send_sems.at[base + a], recv_sems.at[base + a], (x, y, 1 - c))
            sends.append(_remote(ins[a], outs[a], *sems))
            arrivals.append(_remote(outs[a], outs[a], *sems))
        return sends, arrivals

    return _Exchange(mine, _like(mine), n, copies)


def _row_tile(r):
    for cand in (256, 352, 128):
        if r % cand == 0:
            return cand
    return r


def _pair_sum(g, other, core, name):
    ns, h, w = other.shape
    tr = _row_tile(h)
    per_half = h // tr

    def body(core_ref, g_ref, o_ref, out_ref):
        out_ref[...] = (g_ref[...] + o_ref[...]).astype(out_ref.dtype)

    blk = pl.BlockSpec((None, tr, w), lambda k, i, core_ref: (k, i, 0))
    grid_spec = pltpu.PrefetchScalarGridSpec(
        num_scalar_prefetch=1, grid=(ns, per_half),
        in_specs=[pl.BlockSpec((None, tr, w), lambda k, i, core_ref: (k, core_ref[0] * per_half + i, 0)), blk], out_specs=blk)
    return pl.pallas_call(
        body, name=name, grid_spec=grid_spec, out_shape=jax.ShapeDtypeStruct((ns, h, w), MXU_DTYPE),
        compiler_params=_cparams("parallel", "parallel"),
    )(core.reshape(1).astype(jnp.int32), g, other)


def _chip_sum(q, p, chip, name):
    ns, r, w = q.shape
    tr = _row_tile(r)

    def body(chip_ref, q_ref, own_ref, out_ref):
        me = chip_ref[0]
        own = own_ref[...].astype(F32)
        terms = [jnp.where(me == k, own, q_ref[k].astype(F32)) for k in range(ns)]
        out_ref[...] = ((terms[0] + terms[1]) + terms[2]) + terms[3]

    grid_spec = pltpu.PrefetchScalarGridSpec(
        num_scalar_prefetch=1, grid=(r // tr,),
        in_specs=[pl.BlockSpec((ns, tr, w), lambda i, chip_ref: (0, i, 0)),
                  pl.BlockSpec((None, tr, w), lambda i, chip_ref: (chip_ref[0], i, 0))],
        out_specs=pl.BlockSpec((tr, w), lambda i, chip_ref: (i, 0)))
    return pl.pallas_call(
        body, name=name, grid_spec=grid_spec, out_shape=jax.ShapeDtypeStruct((r, w), F32),
        compiler_params=_cparams("parallel"),
    )(chip.reshape(1).astype(jnp.int32), q, p)


def _merge(exchanges):
    if len(exchanges) <= 1:
        return exchanges[0] if exchanges else None
    operands, out_shape, aliases, spans, n = [], [], {}, [], 0
    for ex in exchanges:
        spans.append((len(operands), len(out_shape), n))
        aliases.update({len(operands) + i: len(out_shape) + j for i, j in ex.aliases.items()})
        operands += ex.operands
        out_shape += ex.out_shape
        n += ex.n_copies

    def copies(ins, outs, send_sems, recv_sems, base=0):
        sends, arrivals = [], []
        for ex, (i0, o0, s0) in zip(exchanges, spans):
            s, a = ex.copies(ins[i0:i0 + len(ex.operands)], outs[o0:o0 + len(ex.out_shape)], send_sems, recv_sems, base + s0)
            sends += s
            arrivals += a
        return sends, arrivals

    return _Exchange(operands, out_shape, n, copies, aliases)


def _take(hooks, host):
    stages = (hooks or {}).pop(host, [])
    exchanges = [make() for make, _ in stages]

    def finish(results):
        for (_, done), ex in zip(stages, exchanges):
            done(results[:len(ex.out_shape)])
            results = results[len(ex.out_shape):]

    return _merge(exchanges), finish


def _hook(hooks, host, make, done):
    hooks.setdefault(host, []).append((make, done))


class _WeightPrefetch:
    def __init__(self, names, shards, layer, chip):
        self.names, self.shards, self.layer, self.chip, self.result = names, [shards[n] for n in names], layer, chip, None

    def first(self):
        return _gather_ici(self.shards, self.layer)

    def got_first(self, arrived):
        self.arrived = arrived

    def second(self):
        return _gather_d2d(self.arrived)

    def got_second(self, gathered):
        self.result = {name: lax.dynamic_update_index_in_dim(got, own[self.layer], self.chip, 0)
                       for name, got, own in zip(self.names, gathered, self.shards)}

    def ride(self, hooks, first_host, second_host):
        _hook(hooks, first_host, self.first, self.got_first)
        _hook(hooks, second_host, self.second, self.got_second)

    def run(self, tag):
        self.got_first(_run_exchange(self.first(), f"gather_ici_{tag}"))
        self.got_second(_run_exchange(self.second(), f"gather_d2d_{tag}"))


class _GradReduce:
    def __init__(self, g, chip, core, tag):
        self.names, self.g, self.chip, self.core, self.tag, self.result = list(g), list(g.values()), chip, core, tag, None

    def swap(self):
        return _swap_halves(self.g)

    def got_swap(self, theirs):
        self.pair = [_pair_sum(g, t, self.core, f"pair_sum_{n}_{self.tag}") for n, g, t in zip(self.names, self.g, theirs)]

    def scatter(self):
        return _scatter_shards(self.pair)

    def got_scatter(self, q):
        self.mine = [_chip_sum(qa, pa, self.chip, f"chip_sum_{n}_{self.tag}") for n, qa, pa in zip(self.names, q, self.pair)]

    def share(self):
        return _share_halves(self.mine)

    def got_share(self, theirs):
        self.result = {n: jnp.where(self.core == 0, jnp.concatenate([a, b]), jnp.concatenate([b, a]))
                       for n, a, b in zip(self.names, self.mine, theirs)}

    def ride(self, hooks, swap_host, scatter_host, share_host):
        _hook(hooks, swap_host, self.swap, self.got_swap)
        _hook(hooks, scatter_host, self.scatter, self.got_scatter)
        _hook(hooks, share_host, self.share, self.got_share)

    def run(self):
        self.got_swap(_run_exchange(self.swap(), f"swap_halves_{self.tag}"))
        self.got_scatter(_run_exchange(self.scatter(), f"scatter_shards_{self.tag}"))
        self.got_share(_run_exchange(self.share(), f"share_halves_{self.tag}"))


class _LayerWeights:
    def __init__(self, gathered):
        self.gathered, self.made = gathered, {}

    def __getitem__(self, key):
        if key not in self.made:
            cols = lambda t: jnp.swapaxes(t, 0, 1).reshape(t.shape[1], -1)
            rows = lambda t: t.reshape(-1, t.shape[2])
            if key == "w_in":
                made = jnp.pad(cols(self.gathered("w_in")), ((0, 0), (0, PROJ_PAD - PROJ)))
            elif key == "w_gu":
                made = jnp.concatenate([cols(self.gathered("w_gate")), cols(self.gathered("w_up"))], axis=-1)
            else:
                made = rows(self.gathered(key))
            self.made[key] = made
        return self.made[key]


def _gather_small(pk, name):
    rows, w = pk.shape

    def body(pk_ref, all_ref, sum_ref, send_sems, recv_sems):
        x, y, c, _ = _place()
        me = 4 * x + 2 * y + c
        all_ref[me] = pk_ref[...]
        flips = [(fx, fy, fc) for fx in (0, 1) for fy in (0, 1) for fc in (0, 1)][1:]
        peers = [(x ^ fx, y ^ fy, c ^ fc) for fx, fy, fc in flips]
        sends = [_remote(pk_ref, all_ref.at[me], send_sems.at[k], recv_sems.at[k], peer) for k, peer in enumerate(peers)]
        for cp in sends:
            cp.start()
        for k, (px, py, pc) in enumerate(peers):
            slot = all_ref.at[4 * px + 2 * py + pc]
            _remote(slot, slot, send_sems.at[k], recv_sems.at[k], (px, py, pc)).wait_recv()
        for cp in sends:
            cp.wait_send()
        total = all_ref[0]
        for d in range(1, N_DEV):
            total = total + all_ref[d]
        sum_ref[...] = total

    vm = pl.BlockSpec(memory_space=pltpu.VMEM)
    return pl.pallas_call(
        body, name=name, in_specs=[vm], out_specs=[vm, vm],
        out_shape=[jax.ShapeDtypeStruct((N_DEV, rows, w), F32), jax.ShapeDtypeStruct((rows, w), F32)],
        scratch_shapes=[pltpu.SemaphoreType.DMA((7,)), pltpu.SemaphoreType.DMA((7,))],
    )(pk)


def _row_layout(c, nb, s):
    ch = jnp.swapaxes(c[:, :N_HEADS].reshape(nb, s, N_HEADS), 1, 2)
    ccol = jnp.broadcast_to(ch[..., None], (nb, N_HEADS, s, ATT))
    crow = jnp.broadcast_to(ch.reshape(nb, N_HEADS, s // ATT, 1, ATT), (nb, N_HEADS, s // ATT, 8, ATT))
    return ccol, crow


def _dil_bias(rel_bias, name):
    def body(rel_ref, t_ref, o_ref):
        for p in range(len(DIL_PATTERNS)):
            table = t_ref[p]

            def bucket(k, accs, table=table):
                return tuple(jnp.where(table == k, rel_ref[k, h], acc) for h, acc in enumerate(accs))

            accs = lax.fori_loop(0, REL_BUCKETS, bucket, tuple(jnp.full((BLK, 2 * BLK), NEG, F32) for _ in range(N_HEADS)))
            for h in range(N_HEADS):
                o_ref[p, h] = accs[h]

    vm = pl.BlockSpec(memory_space=pltpu.VMEM)
    return pl.pallas_call(
        body, name=name, in_specs=[pl.BlockSpec(memory_space=pltpu.SMEM), vm], out_specs=vm,
        out_shape=jax.ShapeDtypeStruct((len(DIL_PATTERNS), N_HEADS, BLK, 2 * BLK), F32),
        compiler_params=pltpu.CompilerParams(vmem_limit_bytes=VMEM_LIMIT),
    )(rel_bias, jnp.asarray(_bucket_table()))


def _layer_forward(x, x_b, wts, small, nb, s, tag, hooks=None):
    proj = _matmul(x_b, wts["w_in"], "proj", tag)

    carry, finish = _take(hooks, "sb_fwd")
    (o_sb, tails_sb), carried = _sb_fwd(proj, nb, s, f"sb_fwd_{tag}", carry)
    finish(carried)

    bias = _dil_bias(small["rel_bias"], f"dil_bias_{tag}")
    carry, finish = _take(hooks, "dil_fwd")
    (o_dl, lse_dl), carried = _dil_attention_fwd(proj, bias, nb, s, f"dil_fwd_{tag}", carry)
    finish(carried)

    fb = jnp.zeros((8, BLK), F32).at[0, :N_HEADS].set(small["f_bias"])
    csum = _fox_gates_fwd(proj, fb, nb, s, f"fox_gates_{tag}")
    ccol, crow = _row_layout(csum, nb, s)
    carry, finish = _take(hooks, "fox_fwd")
    (o_fx, lse_fx), carried = _fox_fwd(proj, ccol, crow, nb, s, f"fox_fwd_{tag}", carry)
    finish(carried)

    cw = jnp.zeros((8, CONV_W), F32).at[:3].set(small["conv_w"])
    o_cv = _conv_fwd(proj, cw, nb, s, f"conv_fwd_{tag}")

    mixed = jnp.concatenate([o_sb, o_dl, o_fx, o_cv], axis=-1).astype(MXU_DTYPE)
    pre1, x1, x1_b = _matmul_post_norm(mixed, wts["w_out"], x, small["ln1_g"], small["ln1_b"], f"out_proj_ln1_{tag}")
    carry, finish = _take(hooks, "ffn_in")
    (gate, up, hid), carried = _ffn_in(x1_b, wts["w_gu"], f"ffn_in_{tag}", carry)
    finish(carried)
    pre2, x2, x2_b = _matmul_post_norm(hid, wts["w_down"], x1, small["ln2_g"], small["ln2_b"], f"ffn_out_ln2_{tag}")
    saved = dict(x_b=x_b, proj=proj, tails_sb=tails_sb, bias=bias, o_dl=o_dl, lse_dl=lse_dl, fb=fb, ccol=ccol, crow=crow, o_fx=o_fx,
                 lse_fx=lse_fx, cw=cw, mixed=mixed, pre1=pre1, x1_b=x1_b, gate=gate, up=up, hid=hid, pre2=pre2)
    return (x2, x2_b), saved


def _layer_backward(dx2, sv, wts, small, nb, s, tag, hooks=None, ffn_grads_ready=None):
    t = nb * s
    dpre2, dpre2_b, dgb2 = _ln_bwd(dx2, sv["pre2"], small["ln2_g"], f"ln2_bwd_{tag}")
    carry, finish = _take(hooks, "ffn_out_dx")
    (dgate, dup), carried = _ffn_out_dx(dpre2_b, wts["w_down"], sv["gate"], sv["up"], f"ffn_out_dx_{tag}", carry)
    finish(carried)
    dw_down = _matmul(sv["hid"], dpre2_b, "ffn_out_dw", tag, trans_a=True)
    dx1 = _ffn_in_dx(dgate, dup, wts["w_gu"], dpre2, f"ffn_in_dx_{tag}")
    x1_b = sv["x1_b"]
    dw_gate = _matmul(x1_b, dgate, "ffn_in_dw", f"{tag}_gate", trans_a=True)
    dw_up = _matmul(x1_b, dup, "ffn_in_dw", f"{tag}_up", trans_a=True)

    dpre1, dpre1_b, dgb1 = _ln_bwd(dx1, sv["pre1"], small["ln1_g"], f"ln1_bwd_{tag}")
    dmixed = _matmul(dpre1_b, wts["w_out"], "out_proj_dx", tag, trans_b=True)
    dw_out = _matmul(sv["mixed"], dpre1_b, "out_proj_dw", tag, trans_a=True)
    if ffn_grads_ready:
        ffn_grads_ready(dict(w_down=dw_down, w_gate=dw_gate, w_up=dw_up, w_out=dw_out))
    proj = sv["proj"]

    carry, finish = _take(hooks, "sb_bwd")
    (dq_sb, dk_sb, dv_sb), carried = _sb_bwd(proj, dmixed, sv["tails_sb"], nb, s, f"sb_bwd_{tag}", carry)
    finish(carried)

    delta_dl = _delta_kernel(dmixed, sv["o_dl"], nb, s, f"dil_delta_{tag}")
    carry, finish = _take(hooks, "dil_bwd")
    (dq_dl, dk_dl, dv_dl, gbias), carried = _dil_attention_bwd(proj, dmixed, sv["lse_dl"], delta_dl, sv["bias"], nb, s,
                                                               f"dil_bwd_{tag}", carry)
    finish(carried)
    drel = _bucket_reduce(gbias, jnp.asarray(_bucket_table()), f"rel_bias_grad_{tag}")

    carry, finish = _take(hooks, "fox_bwd")
    (dq_fx, dk_fx, dv_fx, dcol), carried = _fox_bwd(proj, dmixed, sv["lse_fx"], sv["ccol"], sv["crow"], nb, s,
                                                    f"fox_bwd_{tag}", carry)
    finish(carried)
    dcs = -jnp.swapaxes(dcol[:, :, :, 0, :].reshape(nb, N_HEADS, s), 1, 2).reshape(t, N_HEADS)
    dcs = jnp.pad(dcs, ((0, 0), (0, BLK - N_HEADS)))
    dfx, dfb = _fox_gates_bwd(dcs, proj, sv["fb"], nb, s, f"fox_gates_bwd_{tag}")

    dgates, dcw = _conv_bwd(dmixed, proj, sv["cw"], nb, s, f"conv_bwd_{tag}")

    dproj = jnp.concatenate([dq_sb, dk_sb, dv_sb, dq_dl, dk_dl, dv_dl, dq_fx, dk_fx, dv_fx, dgates, dfx],
                            axis=-1).astype(MXU_DTYPE)
    dx = _matmul(dproj, wts["w_in"], "proj_dx", tag, add=dpre1, add_scale=ALPHA, trans_b=True)
    dw_in = _matmul(sv["x_b"], dproj, "proj_dw", tag, trans_a=True)

    grads = dict(w_in=dw_in[:, :PROJ], w_out=dw_out, w_gate=dw_gate, w_up=dw_up, w_down=dw_down,
                 ln1_g=dgb1[0], ln1_b=dgb1[1], ln2_g=dgb2[0], ln2_b=dgb2[1], conv_w=dcw[:3], f_bias=dfb[0, :N_HEADS],
                 rel_bias=drel[:N_HEADS, :REL_BUCKETS].T)
    return dx, grads


class _NoExchanges:
    def forward_hooks(self, layer):
        return None

    def backward_hooks(self, layer):
        return None

    def ffn_grads_ready(self, layer):
        return None

    def layer_done(self, layer, grads):
        pass


def _local_step(x, target, weights_of, small_all, schedule=None):
    schedule = schedule or _NoExchanges()
    nb, s, d = x.shape
    h = x.reshape(nb * s, d)
    h_b = h.astype(MXU_DTYPE)
    saved = []
    for layer in range(DEPTH):
        wts = weights_of(layer)
        (h, h_b), sv = _layer_forward(h, h_b, wts, small_all[layer], nb, s, f"l{layer}", schedule.forward_hooks(layer))
        saved.append((sv, wts))
    dy, lossp = _loss_kernel(h, target.reshape(nb * s, d), "loss")
    grads = [None] * DEPTH
    for layer in reversed(range(DEPTH)):
        sv, wts = saved[layer]
        dy, grads[layer] = _layer_backward(dy, sv, wts, small_all[layer], nb, s, f"l{layer}",
                                           schedule.backward_hooks(layer), schedule.ffn_grads_ready(layer))
        schedule.layer_done(layer, grads[layer])
    return lossp, dy.reshape(nb, s, d), grads


_BIG = ("w_in", "w_out", "w_gate", "w_up", "w_down")
_COL_SHARDED = ("w_in", "w_gate", "w_up")


class _Schedule:
    def __init__(self, shards, chip, core):
        self.chip, self.core, self.reduces = chip, core, [[] for _ in range(DEPTH)]
        first = _WeightPrefetch(["w_in"], shards, 0, chip)
        first.run("l0_w_in")
        rest = _WeightPrefetch(["w_out", "w_gate", "w_up", "w_down"], shards, 0, chip)
        ahead_a = _WeightPrefetch(["w_in", "w_out", "w_down"], shards, 1, chip)
        ahead_b = _WeightPrefetch(["w_gate", "w_up"], shards, 1, chip)
        self.fetches = [[first, rest], [ahead_a, ahead_b]]
        self.forward, self.backward, self.tail = [{} for _ in range(DEPTH)], [{} for _ in range(DEPTH)], {}
        rest.ride(self.forward[0], "sb_fwd", "fox_fwd")
        ahead_a.ride(self.forward[0], "dil_fwd", "ffn_in")
        ahead_b.ride(self.forward[0], "fox_fwd", "ffn_in")

    def weights(self, layer):
        def gathered(name):
            return next(f.result[name] for f in self.fetches[layer] if name in f.names)
        return _LayerWeights(gathered)

    def forward_hooks(self, layer):
        return self.forward[layer]

    def backward_hooks(self, layer):
        return self.backward[layer]

    def _reduce(self, layer, grads, tag):
        red = _GradReduce({name: _by_chip(name, g) for name, g in grads.items()}, self.chip, self.core, tag)
        self.reduces[layer].append(red)
        return red

    def ffn_grads_ready(self, layer):
        if layer != 0:
            return None

        def ready(early):
            self._reduce(0, early, "l0_early").ride(self.backward[0], "sb_bwd", "dil_bwd", "fox_bwd")

        return ready

    def layer_done(self, layer, grads):
        if layer == 1:
            self._reduce(1, {name: grads[name] for name in _BIG}, "l1").ride(self.backward[0], "ffn_out_dx", "sb_bwd", "fox_bwd")
        else:
            self._reduce(0, dict(w_in=grads["w_in"]), "l0_w_in").ride(self.tail, "adamw_w_out", "adamw_w_gate", "adamw_w_up")

    def reduced(self, layer, name):
        return next(r.result[name] for r in self.reduces[layer] if name in r.names)


def _by_chip(name, g):
    if name in _COL_SHARDED:
        return jnp.swapaxes(g.reshape(g.shape[0], N_CHIPS, -1), 0, 1)
    return g.reshape(N_CHIPS, -1, g.shape[1])


_SMALL_LAYOUT = (("ln1_g", 0), ("ln1_b", 2), ("ln2_g", 4), ("ln2_b", 6), ("conv_w", 8))
_ROW_MISC = 10
_ROW_LOSS = 11


def _pack_small(per_layer, rel_bias, loss=None):
    pk = jnp.zeros((SMALL_ROWS, D_MODEL), F32)
    for name, row in _SMALL_LAYOUT:
        for l in range(DEPTH):
            v = per_layer[l][name].reshape(-1)
            pk = pk.at[row + l, :v.shape[0]].set(v)
    fb = jnp.concatenate([per_layer[l]["f_bias"] for l in range(DEPTH)])
    pk = pk.at[_ROW_MISC, :2 * N_HEADS].set(fb)
    pk = pk.at[_ROW_MISC, BLK:BLK + REL_BUCKETS * N_HEADS].set(rel_bias.reshape(-1))
    if loss is not None:
        pk = pk.at[_ROW_LOSS, 0].set(loss)
    return pk


def _unpack_small(pk, conv_cols):
    out = {}
    for name, row in _SMALL_LAYOUT:
        n = 3 * conv_cols if name == "conv_w" else D_MODEL
        v = pk[row:row + DEPTH, :n]
        out[name] = v.reshape(DEPTH, 3, conv_cols) if name == "conv_w" else v
    out["f_bias"] = pk[_ROW_MISC, :2 * N_HEADS].reshape(DEPTH, N_HEADS)
    out["rel_bias"] = pk[_ROW_MISC, BLK:BLK + REL_BUCKETS * N_HEADS].reshape(REL_BUCKETS, N_HEADS)
    return out


_WEIGHTS = ("w_in", "f_bias", "conv_w", "w_out", "rel_bias", "ln1_g", "ln1_b", "w_gate", "w_up", "w_down", "ln2_g", "ln2_b")


def kernel(x, w_in, f_bias, conv_w, w_out, rel_bias, ln1_g, ln1_b, w_gate, w_up, w_down, ln2_g, ln2_b, loss_target, m_w_in, m_f_bias, m_conv_w, m_w_out, m_rel_bias, m_ln1_g, m_ln1_b, m_w_gate, m_w_up, m_w_down, m_ln2_g, m_ln2_b, v_w_in, v_f_bias, v_conv_w, v_w_out, v_rel_bias, v_ln1_g, v_ln1_b, v_w_gate, v_w_up, v_w_down, v_ln2_g, v_ln2_b):
    w = dict(w_in=w_in, f_bias=f_bias, conv_w=conv_w, w_out=w_out, rel_bias=rel_bias, ln1_g=ln1_g, ln1_b=ln1_b,
             w_gate=w_gate, w_up=w_up, w_down=w_down, ln2_g=ln2_g, ln2_b=ln2_b)
    m = dict(w_in=m_w_in, f_bias=m_f_bias, conv_w=m_conv_w, w_out=m_w_out, rel_bias=m_rel_bias, ln1_g=m_ln1_g,
             ln1_b=m_ln1_b, w_gate=m_w_gate, w_up=m_w_up, w_down=m_w_down, ln2_g=m_ln2_g, ln2_b=m_ln2_b)
    v = dict(w_in=v_w_in, f_bias=v_f_bias, conv_w=v_conv_w, w_out=v_w_out, rel_bias=v_rel_bias, ln1_g=v_ln1_g,
             ln1_b=v_ln1_b, w_gate=v_w_gate, w_up=v_w_up, w_down=v_w_down, ln2_g=v_ln2_g, ln2_b=v_ln2_b)
    chip = 2 * lax.axis_index("x") + lax.axis_index("y")
    core = lax.axis_index("c")
    conv_shard = CONV_W // N_CHIPS

    schedule = _Schedule({name: w[name].astype(MXU_DTYPE) for name in _BIG}, chip, core)
    cw_pk = jnp.zeros((8, D_MODEL), F32).at[0, :DEPTH * 3 * conv_shard].set(conv_w.reshape(-1))
    cw_all, _ = _gather_small(cw_pk, "gather_conv_w")
    cw_chips = cw_all[0::2, 0, :DEPTH * 3 * conv_shard].reshape(N_CHIPS, DEPTH, 3, conv_shard)
    conv_full = jnp.moveaxis(cw_chips, 0, 2).reshape(DEPTH, 3, CONV_W)
    small_all = [dict(f_bias=f_bias[l], conv_w=conv_full[l], rel_bias=rel_bias, ln1_g=ln1_g[l], ln1_b=ln1_b[l],
                      ln2_g=ln2_g[l], ln2_b=ln2_b[l]) for l in range(DEPTH)]

    lossp, grad_x, grads = _local_step(x, loss_target, schedule.weights, small_all, schedule)

    drel = grads[0]["rel_bias"] + grads[1]["rel_bias"]
    small_pk = _pack_small(grads, drel, lossp[0, 0])
    _, small_sum = _gather_small(small_pk, "gather_small_grads")
    loss = small_sum[_ROW_LOSS, 0]
    small_g = _unpack_small(small_sum, CONV_W)
    small_g["conv_w"] = lax.dynamic_slice_in_dim(small_g["conv_w"], chip * conv_shard, conv_shard, axis=2)

    out_g, out_d, out_m, out_v = dict(small_g), {}, {}, {}
    for name in ("w_out", "w_gate", "w_up", "w_down", "w_in"):
        out_g[name] = jnp.stack([schedule.reduced(l, name) for l in range(DEPTH)])
        carry, finish = _take(schedule.tail, f"adamw_{name}")
        (out_d[name], out_m[name], out_v[name]), carried = _adamw(w[name], out_g[name], m[name], v[name], f"adamw_{name}", carry)
        finish(carried)
    as_3d = lambda t: t if t.ndim == 3 else t[None]
    for name in _WEIGHTS:
        if name not in _BIG:
            stepped, _ = _adamw(as_3d(w[name]), as_3d(small_g[name]), as_3d(m[name]), as_3d(v[name]), f"adamw_{name}")
            out_d[name], out_m[name], out_v[name] = (t.reshape(w[name].shape) for t in stepped)

    return (loss, grad_x, *[out_g[n] for n in _WEIGHTS], *[out_d[n] for n in _WEIGHTS],
            *[out_m[n] for n in _WEIGHTS], *[out_v[n] for n in _WEIGHTS])
```

```python
import functools
import math

import numpy as np
import jax
import jax.numpy as jnp
from jax import lax
from jax.experimental import pallas as pl
from jax.experimental.pallas import tpu as pltpu

F32 = jnp.float32
BF16 = jnp.bfloat16
MXU_DTYPE = BF16

D_MODEL = 1024
HEAD_DIM = 64
N_HEADS = 4
BLK = 128
ATT = 256
QT = 512
CONV_W = 256
PROJ = 3076
PROJ_PAD = 3200
D_FF = 2816
DEPTH = 2
ALPHA = (2 * DEPTH) ** 0.25
LN_EPS = 1e-5
NEG = -1e30
DIL_PATTERNS = ((128, 1), (512, 4), (2048, 16))
REL_BUCKETS = 32
N_CHIPS = 4
N_DEV = 8
SMALL_ROWS = 16

ADAM_LR = 0.001
ADAM_B1 = 0.9
ADAM_B2 = 0.999
ADAM_EPS = 1e-08
ADAM_WD = 0.01
ADAM_STEP = 10

VMEM_LIMIT = 56 * 2 ** 20
MESH = pl.DeviceIdType.MESH


def _cparams(*sem):
    return pltpu.CompilerParams(dimension_semantics=tuple(sem), vmem_limit_bytes=VMEM_LIMIT)


def _dot(a, b):
    return jnp.dot(a.astype(MXU_DTYPE), b.astype(MXU_DTYPE), preferred_element_type=F32)


def _dot_nt(a, b):
    return lax.dot_general(a.astype(MXU_DTYPE), b.astype(MXU_DTYPE), (((1,), (1,)), ((), ())),
                           preferred_element_type=F32)


def _dot_tn(a, b):
    return lax.dot_general(a.astype(MXU_DTYPE), b.astype(MXU_DTYPE), (((0,), (0,)), ((), ())),
                           preferred_element_type=F32)


def _split_dot(x, ones, passes):
    acc, rest = None, x
    for p in range(passes):
        piece = rest.astype(MXU_DTYPE)
        part = jnp.dot(piece, ones, preferred_element_type=F32)
        acc = part if acc is None else acc + part
        if p + 1 < passes:
            rest = rest - piece.astype(F32)
    return acc


def _split_dot_lhs(ones, x, passes):
    acc, rest = None, x
    for p in range(passes):
        piece = rest.astype(MXU_DTYPE)
        part = jnp.dot(ones, piece, preferred_element_type=F32)
        acc = part if acc is None else acc + part
        if p + 1 < passes:
            rest = rest - piece.astype(F32)
    return acc


def _iota2(shape, axis):
    return lax.broadcasted_iota(jnp.int32, shape, axis)


_TILES = {"proj": (2048, 640, 1024), "ffn_out_dw": (1408, 1024, 2048),
          "ffn_in_dw": (1024, 1408, 2048), "out_proj_dx": (1024, 1024, 1024),
          "out_proj_dw": (1024, 1024, 2048), "proj_dx": (1024, 512, 3200), "proj_dw": (1024, 640, 2048)}


def _matmul(a, b, kind, tag, *, out_dtype=F32, add=None, add_scale=1.0, trans_a=False, trans_b=False):
    k, m = a.shape if trans_a else a.shape[::-1]
    n = b.shape[0] if trans_b else b.shape[1]
    tm, tn, tk = _TILES[kind]
    tm, tk, name = min(tm, m), min(tk, k), f"{kind}_{tag}"
    assert m % tm == 0 and n % tn == 0 and k % tk == 0, (a.shape, b.shape, tm, tn, tk)
    nk = k // tk

    def body(*refs):
        if add is None:
            a_ref, b_ref, o_ref = refs[:3]
            c_ref, scr = None, refs[3:]
        else:
            a_ref, b_ref, c_ref, o_ref = refs[:4]
            scr = refs[4:]
        dot = _dot_tn if trans_a else _dot_nt if trans_b else _dot
        part = dot(a_ref[...], b_ref[...])

        def finish(acc):
            if c_ref is not None:
                acc = acc + add_scale * c_ref[...]
            o_ref[...] = acc.astype(out_dtype)

        if nk == 1:
            finish(part)
        else:
            acc_ref = scr[0]
            kk = pl.program_id(2)

            @pl.when(kk == 0)
            def _():
                acc_ref[...] = part

            @pl.when(kk > 0)
            def _():
                acc_ref[...] += part

            @pl.when(kk == nk - 1)
            def _():
                finish(acc_ref[...])

    b_spec = pl.BlockSpec((tn, tk), lambda i, j, kk: (j, kk)) if trans_b else pl.BlockSpec((tk, tn), lambda i, j, kk: (kk, j))
    a_spec = pl.BlockSpec((tk, tm), lambda i, j, kk: (kk, i)) if trans_a else pl.BlockSpec((tm, tk), lambda i, j, kk: (i, kk))
    in_specs = [a_spec, b_spec]
    operands = [a, b]
    if add is not None:
        in_specs.append(pl.BlockSpec((tm, tn), lambda i, j, kk: (i, j)))
        operands.append(add)
    return pl.pallas_call(
        body, name=name, grid=(m // tm, n // tn, nk), in_specs=in_specs,
        out_specs=pl.BlockSpec((tm, tn), lambda i, j, kk: (i, j)),
        out_shape=jax.ShapeDtypeStruct((m, n), out_dtype),
        scratch_shapes=[pltpu.VMEM((tm, tn), F32)] if nk > 1 else [],
        compiler_params=_cparams("parallel", "parallel", "arbitrary"),
    )(*operands)


def _matmul_post_norm(a, b, xin, g, beta, name):
    t, k = a.shape
    d = b.shape[1]
    tm = 512

    def body(a_ref, b_ref, x_ref, g_ref, beta_ref, pre_ref, y_ref, yb_ref):
        pre = ALPHA * x_ref[...] + _dot(a_ref[...], b_ref[...])
        xhat, _ = _ln_stats(pre)
        y = xhat * g_ref[...] + beta_ref[...]
        pre_ref[...] = pre
        y_ref[...] = y
        yb_ref[...] = y.astype(yb_ref.dtype)

    row = pl.BlockSpec((tm, d), lambda i: (i, 0))
    vec = pl.BlockSpec((1, d), lambda i: (0, 0))
    return pl.pallas_call(
        body, name=name, grid=(t // tm,),
        in_specs=[pl.BlockSpec((tm, k), lambda i: (i, 0)), pl.BlockSpec((k, d), lambda i: (0, 0)), row, vec, vec],
        out_specs=[row, row, row],
        out_shape=[jax.ShapeDtypeStruct((t, d), F32)] * 2 + [jax.ShapeDtypeStruct((t, d), MXU_DTYPE)],
        compiler_params=_cparams("parallel"),
    )(a, b, xin, g.reshape(1, d), beta.reshape(1, d))


def _ffn_in(x1, w_gu, name, carry=None):
    t, d = x1.shape
    tm, tn = 512, D_FF // 2
    nj = D_FF // tn

    def body(x_ref, wg_ref, wu_ref, gate_ref, up_ref, h_ref):
        xb = x_ref[...].astype(MXU_DTYPE)
        gate = _dot(xb, wg_ref[...])
        up = _dot(xb, wu_ref[...])
        gate_ref[...] = gate
        up_ref[...] = up
        h_ref[...] = (gate * (1.0 / (1.0 + jnp.exp(-gate))) * up).astype(h_ref.dtype)

    out = pl.BlockSpec((tm, tn), lambda i, j: (i, j))
    return _host_call(
        body, carry, name=name, grid=(t // tm, nj),
        in_specs=[pl.BlockSpec((tm, d), lambda i, j: (i, 0)), pl.BlockSpec((d, tn), lambda i, j: (0, j)),
                  pl.BlockSpec((d, tn), lambda i, j: (0, nj + j))],
        out_specs=[out, out, out],
        out_shape=[jax.ShapeDtypeStruct((t, D_FF), F32)] * 2 + [jax.ShapeDtypeStruct((t, D_FF), MXU_DTYPE)],
        operands=(x1, w_gu, w_gu))


def _ffn_out_dx(dy, w_down, gate, up, name, carry=None):
    t, d = dy.shape
    tm, tn = 512, D_FF // 2

    def body(dy_ref, w_ref, gate_ref, up_ref, dg_ref, du_ref):
        dh = _dot_nt(dy_ref[...], w_ref[...])
        gate = gate_ref[...]
        sig = 1.0 / (1.0 + jnp.exp(-gate))
        dg_ref[...] = (dh * up_ref[...] * sig * (1.0 + gate * (1.0 - sig))).astype(dg_ref.dtype)
        du_ref[...] = (dh * gate * sig).astype(du_ref.dtype)

    tile = pl.BlockSpec((tm, tn), lambda i, j: (i, j))
    return _host_call(
        body, carry, name=name, grid=(t // tm, D_FF // tn),
        in_specs=[pl.BlockSpec((tm, d), lambda i, j: (i, 0)), pl.BlockSpec((tn, d), lambda i, j: (j, 0)), tile, tile],
        out_specs=[tile, tile], out_shape=[jax.ShapeDtypeStruct((t, D_FF), MXU_DTYPE)] * 2,
        operands=(dy, w_down, gate, up))


def _ffn_in_dx(dgate, dup, w_gu, add, name):
    t = dgate.shape[0]
    d = w_gu.shape[0]
    tm, tk = 1024, D_FF // 2
    nk = D_FF // tk

    def body(dg_ref, du_ref, wg_ref, wu_ref, add_ref, o_ref, acc_ref):
        kk = pl.program_id(1)
        part = _dot_nt(dg_ref[...], wg_ref[...]) + _dot_nt(du_ref[...], wu_ref[...])

        @pl.when(kk == 0)
        def _():
            acc_ref[...] = part

        @pl.when(kk > 0)
        def _():
            acc_ref[...] += part

        @pl.when(kk == nk - 1)
        def _():
            o_ref[...] = acc_ref[...] + ALPHA * add_ref[...]

    act = pl.BlockSpec((tm, tk), lambda i, kk: (i, kk))
    row = pl.BlockSpec((tm, d), lambda i, kk: (i, 0))
    return pl.pallas_call(
        body, name=name, grid=(t // tm, nk),
        in_specs=[act, act, pl.BlockSpec((d, tk), lambda i, kk: (0, kk)), pl.BlockSpec((d, tk), lambda i, kk: (0, nk + kk)), row],
        out_specs=row, out_shape=jax.ShapeDtypeStruct((t, d), F32), scratch_shapes=[pltpu.VMEM((tm, d), F32)],
        compiler_params=_cparams("parallel", "arbitrary"),
    )(dgate, dup, w_gu, w_gu, add)


def _ln_stats(pre):
    mu = jnp.mean(pre, axis=-1, keepdims=True)
    xc = pre - mu
    var = jnp.mean(xc * xc, axis=-1, keepdims=True)
    rstd = lax.rsqrt(var + LN_EPS)
    return xc * rstd, rstd


def _ln_bwd(dy, pre, g, name):
    t, d = dy.shape
    tile = 256

    def body(dy_ref, pre_ref, g_ref, dpre_ref, dpre_b_ref, dgb_ref):
        dyv = dy_ref[...]
        xhat, rstd = _ln_stats(pre_ref[...])
        dxh = dyv * g_ref[...]
        m1 = jnp.mean(dxh, axis=-1, keepdims=True)
        m2 = jnp.mean(dxh * xhat, axis=-1, keepdims=True)
        dpre = rstd * (dxh - m1 - xhat * m2)
        dpre_ref[...] = dpre
        dpre_b_ref[...] = dpre.astype(dpre_b_ref.dtype)

        @pl.when(pl.program_id(0) == 0)
        def _():
            dgb_ref[...] = jnp.zeros_like(dgb_ref)

        dgb_ref[0:1, :] += jnp.sum(dyv * xhat, axis=0, keepdims=True)
        dgb_ref[1:2, :] += jnp.sum(dyv, axis=0, keepdims=True)

    row = pl.BlockSpec((tile, d), lambda i: (i, 0))
    return pl.pallas_call(
        body, name=name, grid=(t // tile,), in_specs=[row, row, pl.BlockSpec((1, d), lambda i: (0, 0))],
        out_specs=[row, row, pl.BlockSpec((8, d), lambda i: (0, 0))],
        out_shape=[jax.ShapeDtypeStruct((t, d), F32), jax.ShapeDtypeStruct((t, d), MXU_DTYPE), jax.ShapeDtypeStruct((8, d), F32)],
        compiler_params=_cparams("arbitrary"),
    )(dy, pre, g.reshape(1, d))


def _loss_kernel(y, target, name):
    t, d = y.shape
    tile = 512

    def body(y_ref, t_ref, dy_ref, l_ref):
        err = y_ref[...] - t_ref[...]
        dy_ref[...] = err * (1.0 / d)

        @pl.when(pl.program_id(0) == 0)
        def _():
            l_ref[...] = jnp.zeros_like(l_ref)

        l_ref[...] += jnp.sum(err * err) * (0.5 / d)

    row = pl.BlockSpec((tile, d), lambda i: (i, 0))
    return pl.pallas_call(
        body, name=name, grid=(t // tile,), in_specs=[row, row],
        out_specs=[row, pl.BlockSpec((8, 128), lambda i: (0, 0))],
        out_shape=[jax.ShapeDtypeStruct((t, d), F32), jax.ShapeDtypeStruct((8, 128), F32)],
        compiler_params=_cparams("arbitrary"),
    )(y, target)


def _adamw(w, g, m, v, name):
    nl, r, c = w.shape
    tr = r
    for cand in (256, 352, 128, 64, 16, 8):
        if r % cand == 0:
            tr = cand
            break

    def body(w_ref, g_ref, m_ref, v_ref, d_ref, nm_ref, nv_ref):
        gv = g_ref[...]
        nm = ADAM_B1 * m_ref[...] + (1.0 - ADAM_B1) * gv
        nv = ADAM_B2 * v_ref[...] + (1.0 - ADAM_B2) * (gv * gv)
        m_hat = nm / (1.0 - ADAM_B1 ** ADAM_STEP)
        v_hat = nv / (1.0 - ADAM_B2 ** ADAM_STEP)
        d_ref[...] = -ADAM_LR * (m_hat / (jnp.sqrt(v_hat) + ADAM_EPS) + ADAM_WD * w_ref[...])
        nm_ref[...] = nm
        nv_ref[...] = nv

    blk = pl.BlockSpec((1, tr, c), lambda l, i: (l, i, 0))
    return pl.pallas_call(
        body, name=name, grid=(nl, r // tr), in_specs=[blk] * 4, out_specs=[blk] * 3,
        out_shape=[jax.ShapeDtypeStruct(w.shape, F32)] * 3, compiler_params=_cparams("parallel", "parallel"),
    )(w, g, m, v)


def _shift_down(u, k, rows):
    return jnp.where(rows >= k, pltpu.roll(u, k, 0), 0.0)


def _shift_up(u, k, rows, s):
    return jnp.where(rows < s - k, pltpu.roll(u, s - k, 0), 0.0)


def _conv_fwd(proj, conv_w, nb, s, name):
    def body(b_ref, c_ref, h_ref, w_ref, o_ref):
        rows = _iota2((s, CONV_W), 0)
        u = c_ref[...] * h_ref[...]
        y = w_ref[2:3, :] * u + w_ref[1:2, :] * _shift_down(u, 1, rows) + w_ref[0:1, :] * _shift_down(u, 2, rows)
        o_ref[...] = b_ref[...] * y

    col = lambda j: pl.BlockSpec((s, CONV_W), lambda b: (b, j))
    return pl.pallas_call(
        body, name=name, grid=(nb,),
        in_specs=[col(9), col(10), col(11), pl.BlockSpec((8, CONV_W), lambda b: (0, 0))],
        out_specs=pl.BlockSpec((s, CONV_W), lambda b: (b, 0)),
        out_shape=jax.ShapeDtypeStruct((nb * s, CONV_W), F32), compiler_params=_cparams("parallel"),
    )(proj, proj, proj, conv_w)


def _conv_bwd(dmixed, proj, conv_w, nb, s, name):
    def body(do_ref, b_ref, c_ref, h_ref, w_ref, dg_ref, dw_ref):
        rows = _iota2((s, CONV_W), 0)
        cg, hg, bg, dout = c_ref[...], h_ref[...], b_ref[...], do_ref[...]
        u = cg * hg
        u1 = _shift_down(u, 1, rows)
        u2 = _shift_down(u, 2, rows)
        y = w_ref[2:3, :] * u + w_ref[1:2, :] * u1 + w_ref[0:1, :] * u2
        dy = dout * bg
        du = w_ref[2:3, :] * dy + w_ref[1:2, :] * _shift_up(dy, 1, rows, s) + w_ref[0:1, :] * _shift_up(dy, 2, rows, s)
        dg_ref[:, 0:CONV_W] = dout * y
        dg_ref[:, CONV_W:2 * CONV_W] = du * hg
        dg_ref[:, 2 * CONV_W:3 * CONV_W] = du * cg

        @pl.when(pl.program_id(0) == 0)
        def _():
            dw_ref[...] = jnp.zeros_like(dw_ref)

        dw_ref[0:1, :] += jnp.sum(dy * u2, axis=0, keepdims=True)
        dw_ref[1:2, :] += jnp.sum(dy * u1, axis=0, keepdims=True)
        dw_ref[2:3, :] += jnp.sum(dy * u, axis=0, keepdims=True)

    col = lambda j: pl.BlockSpec((s, CONV_W), lambda b: (b, j))
    return pl.pallas_call(
        body, name=name, grid=(nb,),
        in_specs=[col(3), col(9), col(10), col(11), pl.BlockSpec((8, CONV_W), lambda b: (0, 0))],
        out_specs=[pl.BlockSpec((s, 3 * CONV_W), lambda b: (b, 0)), pl.BlockSpec((8, CONV_W), lambda b: (0, 0))],
        out_shape=[jax.ShapeDtypeStruct((nb * s, 3 * CONV_W), F32), jax.ShapeDtypeStruct((8, CONV_W), F32)],
        compiler_params=_cparams("arbitrary"),
    )(dmixed, proj, proj, proj, conv_w)


def _col_spec(s, base):
    return pl.BlockSpec((s, BLK), lambda b, p: (b, base + p))


def _qrows(i):
    return pl.ds(pl.multiple_of(i * QT, QT), QT)


def _rows(j):
    return pl.ds(pl.multiple_of(j * ATT, ATT), ATT)


def _keys_upto(i):
    return (i + 1) * (QT // ATT)


def _triangle(keep):
    return keep(_iota2((ATT, ATT), 0), _iota2((ATT, ATT), 1)).astype(MXU_DTYPE)


def _rows128(i):
    return pl.ds(pl.multiple_of(i * BLK, BLK), BLK)


def _log_sigmoid_parts(z):
    e = jnp.exp(-jnp.abs(z))
    l1p = jnp.log(1.0 + e)
    lb = jnp.minimum(z, 0.0) - l1p
    return lb, lb - z, e


def _head_masks():
    lane = _iota2((1, BLK), 1)
    return [(lane >= h * HEAD_DIM) & (lane < (h + 1) * HEAD_DIM) for h in range(2)]


def _split_heads(ref, scr, sels):
    for h, sel in enumerate(sels):
        scr[h] = jnp.where(sel, ref[...], 0.0).astype(MXU_DTYPE)


def _sb_fwd(proj, nb, s, name, carry=None):
    def body(q_ref, k_ref, v_ref, o_ref, tails_ref, km, vm):
        sels = _head_masks()
        _split_heads(k_ref, km, sels)
        _split_heads(v_ref, vm, sels)
        rows = _iota2((QT, ATT), 0)
        cols = _iota2((QT, ATT), 1)
        lane = _iota2((QT, BLK), 1)
        later = _triangle(lambda r, c: r > c)
        tails_ref[...] = jnp.zeros_like(tails_ref)

        def qblock(i, _):
            qi = (q_ref[_qrows(i), :] * 0.125).astype(MXU_DTYPE)

            def kblock(t, state):
                carries, acc = state
                j = _keys_upto(i) - 1 - t
                strict = (cols + (j * ATT - i * QT)) < rows
                out = []
                for h in range(2):
                    tails_ref[h, _qrows(i), :] = jnp.where(lane == j, carries[h], tails_ref[h, _qrows(i), :])
                    z = _dot_nt(qi, km[h, _rows(j), :])
                    lb, lr, _ = _log_sigmoid_parts(z)
                    lr = jnp.where(strict, lr, 0.0)
                    tail = _split_dot(lr, later, 2) + carries[h]
                    a = jnp.where(strict, jnp.exp(lb + tail), 0.0)
                    acc = acc + _dot(a, vm[h, _rows(j), :])
                    out.append(carries[h] + jnp.sum(lr, axis=-1, keepdims=True))
                return tuple(out), acc

            init = ((jnp.zeros((QT, 1), F32),) * 2, jnp.zeros((QT, BLK), F32))
            _, acc = lax.fori_loop(0, _keys_upto(i), kblock, init)
            o_ref[_qrows(i), :] = acc
            return 0

        lax.fori_loop(0, s // QT, qblock, 0)

    return _host_call(
        body, carry, name=name, grid=(nb, 2), in_specs=[_col_spec(s, 0), _col_spec(s, 2), _col_spec(s, 4)],
        out_specs=[_col_spec(s, 0), _pair_spec(s, BLK)],
        out_shape=[jax.ShapeDtypeStruct((nb * s, 2 * BLK), F32), jax.ShapeDtypeStruct((nb, N_HEADS, s, BLK), F32)],
        scratch_shapes=[pltpu.VMEM((2, s, BLK), MXU_DTYPE)] * 2, operands=(proj, proj, proj))


def _sb_bwd(proj, dmixed, tails, nb, s, name, carry=None):
    def body(q_ref, k_ref, v_ref, do_ref, tails_ref, dq_ref, dk_ref, dv_ref, km, vm):
        sels = _head_masks()
        _split_heads(k_ref, km, sels)
        _split_heads(v_ref, vm, sels)
        rows = _iota2((QT, ATT), 0)
        cols = _iota2((QT, ATT), 1)
        lane = _iota2((QT, BLK), 1)
        later = _triangle(lambda r, c: r > c)
        earlier = _triangle(lambda r, c: r < c)
        dk_ref[...] = jnp.zeros_like(dk_ref)
        dv_ref[...] = jnp.zeros_like(dv_ref)

        def qblock(i, _):
            qi = (q_ref[_qrows(i), :] * 0.125).astype(MXU_DTYPE)
            doi = do_ref[_qrows(i), :].astype(MXU_DTYPE)
            qm = [jnp.where(sel, qi, 0.0) for sel in sels]
            dom = [jnp.where(sel, doi, 0.0) for sel in sels]
            tails_i = [tails_ref[h, _qrows(i), :] for h in range(2)]

            def kblock(j, state):
                csums, dq = state
                strict = (cols + (j * ATT - i * QT)) < rows
                out = []
                for h in range(2):
                    z = _dot_nt(qi, km[h, _rows(j), :])
                    lb, lr, _ = _log_sigmoid_parts(z)
                    lr = jnp.where(strict, lr, 0.0)
                    after = jnp.sum(jnp.where(lane == j, tails_i[h], 0.0), axis=-1, keepdims=True)
                    a = jnp.where(strict, jnp.exp(lb + _split_dot(lr, later, 2) + after), 0.0)
                    dl = a * _dot_nt(doi, vm[h, _rows(j), :])
                    beta = jnp.exp(lb)
                    before = _split_dot(dl, earlier, 2) + csums[h]
                    dz = jnp.where(strict, dl * (1.0 - beta) - beta * before, 0.0).astype(MXU_DTYPE)
                    dq = dq + _dot(dz, km[h, _rows(j), :])
                    dk_ref[_rows(j), :] += _dot_tn(dz, qm[h])
                    dv_ref[_rows(j), :] += _dot_tn(a, dom[h])
                    out.append(csums[h] + jnp.sum(dl, axis=-1, keepdims=True))
                return tuple(out), dq

            init = ((jnp.zeros((QT, 1), F32),) * 2, jnp.zeros((QT, BLK), F32))
            _, dq = lax.fori_loop(0, _keys_upto(i), kblock, init)
            dq_ref[_qrows(i), :] = dq * 0.125
            return 0

        lax.fori_loop(0, s // QT, qblock, 0)

    out = _col_spec(s, 0)
    return _host_call(
        body, carry, name=name, grid=(nb, 2),
        in_specs=[_col_spec(s, 0), _col_spec(s, 2), _col_spec(s, 4), out, _pair_spec(s, BLK)], out_specs=[out] * 3,
        out_shape=[jax.ShapeDtypeStruct((nb * s, 2 * BLK), F32)] * 3,
        scratch_shapes=[pltpu.VMEM((2, s, BLK), MXU_DTYPE)] * 2, operands=(proj, proj, proj, dmixed, tails))


def _pair_spec(s, width):
    return pl.BlockSpec((None, 2, s, width), lambda b, p: (b, p, 0, 0))


def _fox_fwd(proj, ccol, crow, nb, s, name, carry=None):
    nblk = s // ATT

    def body(q_ref, k_ref, v_ref, cc_ref, cr_ref, o_ref, lse_ref, km, vm):
        sels = _head_masks()
        _split_heads(k_ref, km, sels)
        _split_heads(v_ref, vm, sels)
        rows = _iota2((QT, ATT), 0)
        cols = _iota2((QT, ATT), 1)

        def qblock(i, _):
            qi = (q_ref[_qrows(i), :] * 0.125).astype(MXU_DTYPE)
            ci = [cc_ref[h, _qrows(i), :] for h in range(2)]

            def kblock(j, state):
                ms, ls, acc = state
                causal = (cols + (j * ATT - i * QT)) <= rows
                new_m, new_l, scales, parts = [], [], [], []
                for h in range(2):
                    z = _dot_nt(qi, km[h, _rows(j), :]) + (ci[h] - cr_ref[h, j][0:1, :])
                    z = jnp.where(causal, z, NEG)
                    m_new = jnp.maximum(ms[h], jnp.max(z, axis=-1, keepdims=True))
                    p = jnp.exp(z - m_new)
                    scale = jnp.exp(ms[h] - m_new)
                    new_m.append(m_new)
                    new_l.append(scale * ls[h] + jnp.sum(p, axis=-1, keepdims=True))
                    scales.append(scale)
                    parts.append(_dot(p, vm[h, _rows(j), :]))
                acc = jnp.where(sels[0], scales[0], scales[1]) * acc + parts[0] + parts[1]
                return tuple(new_m), tuple(new_l), acc

            init = ((jnp.full((QT, 1), NEG, F32),) * 2, (jnp.zeros((QT, 1), F32),) * 2, jnp.zeros((QT, BLK), F32))
            ms, ls, acc = lax.fori_loop(0, _keys_upto(i), kblock, init)
            o_ref[_qrows(i), :] = acc / jnp.where(sels[0], ls[0], ls[1])
            for h in range(2):
                lse_ref[h, _qrows(i), :] = jnp.broadcast_to(ms[h] + jnp.log(ls[h]), (QT, ATT))
            return 0

        lax.fori_loop(0, s // QT, qblock, 0)

    crow_spec = pl.BlockSpec((None, 2, nblk, 8, ATT), lambda b, p: (b, p, 0, 0, 0))
    return _host_call(
        body, carry, name=name, grid=(nb, 2),
        in_specs=[_col_spec(s, 12), _col_spec(s, 14), _col_spec(s, 16), _pair_spec(s, ATT), crow_spec],
        out_specs=[_col_spec(s, 0), _pair_spec(s, ATT)],
        out_shape=[jax.ShapeDtypeStruct((nb * s, 2 * BLK), F32), jax.ShapeDtypeStruct((nb, N_HEADS, s, ATT), F32)],
        scratch_shapes=[pltpu.VMEM((2, s, BLK), MXU_DTYPE)] * 2, operands=(proj, proj, proj, ccol, crow))


def _fox_bwd(proj, dmixed, lse, ccol, crow, nb, s, name, carry=None):
    nblk = s // ATT

    def body(q_ref, k_ref, v_ref, do_ref, lse_ref, cc_ref, cr_ref, dq_ref, dk_ref, dv_ref, dc_ref, km, vm, p_scr, dp_scr):
        sels = _head_masks()
        _split_heads(k_ref, km, sels)
        _split_heads(v_ref, vm, sels)
        rows = _iota2((QT, ATT), 0)
        cols = _iota2((QT, ATT), 1)
        dk_ref[...] = jnp.zeros_like(dk_ref)
        dv_ref[...] = jnp.zeros_like(dv_ref)
        dc_ref[...] = jnp.zeros_like(dc_ref)

        def qblock(i, _):
            qi = (q_ref[_qrows(i), :] * 0.125).astype(MXU_DTYPE)
            doi = do_ref[_qrows(i), :].astype(MXU_DTYPE)
            qm = [jnp.where(sel, qi, 0.0) for sel in sels]
            dom = [jnp.where(sel, doi, 0.0) for sel in sels]
            ci = [cc_ref[h, _qrows(i), :] for h in range(2)]
            lsei = [lse_ref[h, _qrows(i), :] for h in range(2)]

            def probs(j, h):
                z = _dot_nt(qi, km[h, _rows(j), :]) + (ci[h] - cr_ref[h, j][0:1, :])
                p = jnp.where((cols + (j * ATT - i * QT)) <= rows, jnp.exp(z - lsei[h]), 0.0)
                return p, _dot_nt(doi, vm[h, _rows(j), :])

            def row_term(j, accs):
                out = []
                for h in range(2):
                    p, dp = probs(j, h)
                    p_scr[h, j] = p
                    dp_scr[h, j] = dp
                    out.append(accs[h] + jnp.sum(p * dp, axis=-1, keepdims=True))
                return tuple(out)

            di = lax.fori_loop(0, _keys_upto(i), row_term, (jnp.zeros((QT, 1), F32),) * 2)

            def kblock(j, dq):
                for h in range(2):
                    p = p_scr[h, j]
                    ds = p * (dp_scr[h, j] - di[h])
                    dc_ref[h, j] += jnp.broadcast_to(jnp.sum(ds, axis=0, keepdims=True), (8, ATT))
                    ds = ds.astype(MXU_DTYPE)
                    dk_ref[_rows(j), :] += _dot_tn(ds, qm[h])
                    dv_ref[_rows(j), :] += _dot_tn(p, dom[h])
                    dq = dq + _dot(ds, km[h, _rows(j), :])
                return dq

            dq = lax.fori_loop(0, _keys_upto(i), kblock, jnp.zeros((QT, BLK), F32))
            dq_ref[_qrows(i), :] = dq * 0.125
            return 0

        lax.fori_loop(0, s // QT, qblock, 0)

    crow_spec = pl.BlockSpec((None, 2, nblk, 8, ATT), lambda b, p: (b, p, 0, 0, 0))
    wide, cols_out = _pair_spec(s, ATT), _col_spec(s, 0)
    return _host_call(
        body, carry, name=name, grid=(nb, 2),
        in_specs=[_col_spec(s, 12), _col_spec(s, 14), _col_spec(s, 16), _col_spec(s, 4), wide, wide, crow_spec],
        out_specs=[cols_out, cols_out, cols_out, crow_spec],
        out_shape=[jax.ShapeDtypeStruct((nb * s, 2 * BLK), F32)] * 3 + [jax.ShapeDtypeStruct((nb, N_HEADS, nblk, 8, ATT), F32)],
        scratch_shapes=[pltpu.VMEM((2, s, BLK), MXU_DTYPE)] * 2 + [pltpu.VMEM((2, nblk, QT, ATT), F32)] * 2,
        operands=(proj, proj, proj, dmixed, lse, ccol, crow))


def _fox_gates_fwd(proj, f_bias, nb, s, name):
    chunk = 256

    def body(f_ref, b_ref, c_ref):
        lower = (_iota2((chunk, chunk), 0) >= _iota2((chunk, chunk), 1)).astype(MXU_DTYPE)
        carry = jnp.zeros((1, BLK), F32)
        for n in range(s // chunk):
            rows = pl.ds(n * chunk, chunk)
            lf, _, _ = _log_sigmoid_parts(f_ref[rows, :] + b_ref[0:1, :])
            c = _split_dot_lhs(lower, lf, 3) + carry
            c_ref[rows, :] = c
            carry = c[chunk - 1:chunk, :]

    return pl.pallas_call(
        body, name=name, grid=(nb,),
        in_specs=[pl.BlockSpec((s, BLK), lambda b: (b, (PROJ_PAD - BLK) // BLK)), pl.BlockSpec((8, BLK), lambda b: (0, 0))],
        out_specs=pl.BlockSpec((s, BLK), lambda b: (b, 0)),
        out_shape=jax.ShapeDtypeStruct((nb * s, BLK), F32), compiler_params=_cparams("parallel"),
    )(proj, f_bias)


def _fox_gates_bwd(dc, proj, f_bias, nb, s, name):
    chunk = 256

    def body(dc_ref, f_ref, b_ref, df_ref, db_ref):
        upper = (_iota2((chunk, chunk), 0) <= _iota2((chunk, chunk), 1)).astype(MXU_DTYPE)
        carry = jnp.zeros((1, BLK), F32)
        total = jnp.zeros((1, BLK), F32)
        for n in reversed(range(s // chunk)):
            rows = pl.ds(n * chunk, chunk)
            dlf = _split_dot_lhs(upper, dc_ref[rows, :], 3) + carry
            carry = dlf[0:1, :]
            pre = f_ref[rows, :] + b_ref[0:1, :]
            e = jnp.exp(-jnp.abs(pre))
            df = dlf * (jnp.where(pre >= 0.0, e, 1.0) / (1.0 + e))
            df_ref[rows, :] = df
            total = total + jnp.sum(df, axis=0, keepdims=True)

        @pl.when(pl.program_id(0) == 0)
        def _():
            db_ref[...] = jnp.zeros_like(db_ref)

        db_ref[0:1, :] += total

    return pl.pallas_call(
        body, name=name, grid=(nb,),
        in_specs=[pl.BlockSpec((s, BLK), lambda b: (b, 0)), pl.BlockSpec((s, BLK), lambda b: (b, (PROJ_PAD - BLK) // BLK)),
                  pl.BlockSpec((8, BLK), lambda b: (0, 0))],
        out_specs=[pl.BlockSpec((s, BLK), lambda b: (b, 0)), pl.BlockSpec((8, BLK), lambda b: (0, 0))],
        out_shape=[jax.ShapeDtypeStruct((nb * s, BLK), F32), jax.ShapeDtypeStruct((8, BLK), F32)],
        compiler_params=_cparams("arbitrary"),
    )(dc, proj, f_bias)


def _delta_kernel(dmixed, o, nb, s, name):
    def body(do_ref, o_ref, d_ref):
        prod = do_ref[...] * o_ref[...]
        for h, sel in enumerate(_head_masks()):
            d_ref[h] = jnp.broadcast_to(jnp.sum(jnp.where(sel, prod, 0.0), axis=-1, keepdims=True), (s, BLK))

    return pl.pallas_call(
        body, name=name, grid=(nb, 2), in_specs=[_col_spec(s, 2), _col_spec(s, 0)], out_specs=_pair_spec(s, BLK),
        out_shape=jax.ShapeDtypeStruct((nb, N_HEADS, s, BLK), F32), compiler_params=_cparams("parallel", "parallel"),
    )(dmixed, o)


def _t5_bucket_np(dist):
    max_exact = REL_BUCKETS // 2
    nf = np.maximum(dist, 1).astype(np.float32)
    large = max_exact + (np.log(nf / max_exact) / math.log(2048 / max_exact) * (REL_BUCKETS - max_exact)).astype(np.int32)
    large = np.minimum(large, REL_BUCKETS - 1)
    return np.where(dist < max_exact, dist, large)


def _bucket_table():
    qi = np.arange(BLK)[:, None]
    kj = np.arange(2 * BLK)[None, :]
    dist = qi + BLK - kj
    tables = []
    for window, dil in DIL_PATTERNS:
        in_band = (dist >= 0) & (dist <= window // dil)
        tables.append(np.where(in_band, _t5_bucket_np(np.maximum(dist, 0) * dil), -1).astype(np.int32))
    return np.stack(tables)


def _dil_scores(qb, kp, kc, b_ref, h, prev_valid):
    zp = _dot_nt(qb, kp) + b_ref[h, :, 0:BLK]
    zp = jnp.where(prev_valid, zp, NEG)
    zc = _dot_nt(qb, kc) + b_ref[h, :, BLK:2 * BLK]
    return zp, zc


def _residue_rows(b, seg, dil):
    if dil == 1:
        return _rows128(b), _rows128(jnp.maximum(b - 1, 0)), b > 0
    r, n = b // seg, b % seg
    cur = pl.ds(r + dil * n * BLK, BLK, stride=dil)
    prev = pl.ds(r + dil * jnp.maximum(n - 1, 0) * BLK, BLK, stride=dil)
    return cur, prev, n > 0


def _dil_attention_fwd(proj, bias, nb, s, name, carry=None):
    nblk = s // BLK

    def body(q_ref, k_ref, v_ref, b_ref, out_ref, lse_ref, o_scr, l_scr):
        sels = _head_masks()
        for p, (_, dil) in enumerate(DIL_PATTERNS):
            seg = s // dil // BLK

            def block(b, _, p=p, seg=seg, dil=dil):
                cur, prev, has_prev = _residue_rows(b, seg, dil)
                qb = (q_ref[cur, :] * 0.125).astype(MXU_DTYPE)
                kp, kc = k_ref[prev, :].astype(MXU_DTYPE), k_ref[cur, :].astype(MXU_DTYPE)
                vp, vc = v_ref[prev, :].astype(MXU_DTYPE), v_ref[cur, :].astype(MXU_DTYPE)
                acc = jnp.zeros((BLK, BLK), F32)
                for h, sel in enumerate(sels):
                    zp, zc = _dil_scores(qb, jnp.where(sel, kp, 0.0), jnp.where(sel, kc, 0.0), b_ref.at[p], h, has_prev)
                    m = jnp.maximum(jnp.max(zp, axis=-1, keepdims=True), jnp.max(zc, axis=-1, keepdims=True))
                    pp = jnp.exp(zp - m)
                    pc = jnp.exp(zc - m)
                    den = jnp.sum(pp, axis=-1, keepdims=True) + jnp.sum(pc, axis=-1, keepdims=True)
                    acc = acc + (_dot(pp, jnp.where(sel, vp, 0.0)) + _dot(pc, jnp.where(sel, vc, 0.0))) / den
                    l_scr[p, h, cur, :] = jnp.broadcast_to(m + jnp.log(den), (BLK, BLK))
                o_scr[p, cur, :] = acc
                return 0

            lax.fori_loop(0, nblk, block, 0, unroll=4)

        weights, dens = [], []
        for h in range(2):
            m = jnp.maximum(jnp.maximum(l_scr[0, h], l_scr[1, h]), l_scr[2, h])
            w = [jnp.exp(l_scr[p, h] - m) for p in range(3)]
            den = w[0] + w[1] + w[2]
            lse_ref[h] = m + jnp.log(den)
            weights.append(w)
            dens.append(den)
        num = sum(jnp.where(sels[0], weights[0][p], weights[1][p]) * o_scr[p] for p in range(3))
        out_ref[...] = num / jnp.where(sels[0], dens[0], dens[1])

    bias_spec = pl.BlockSpec((3, 2, BLK, 2 * BLK), lambda b, p: (0, p, 0, 0))
    return _host_call(
        body, carry, name=name, grid=(nb, 2), in_specs=[_col_spec(s, 6), _col_spec(s, 8), _col_spec(s, 10), bias_spec],
        out_specs=[_col_spec(s, 0), _pair_spec(s, BLK)],
        out_shape=[jax.ShapeDtypeStruct((nb * s, 2 * BLK), F32), jax.ShapeDtypeStruct((nb, N_HEADS, s, BLK), F32)],
        scratch_shapes=[pltpu.VMEM((3, s, BLK), F32), pltpu.VMEM((3, 2, s, BLK), F32)], operands=(proj, proj, proj, bias))


def _dil_attention_bwd(proj, dmixed, lse, delta, bias, nb, s, name, carry=None):
    nblk = s // BLK

    def body(q_ref, k_ref, v_ref, do_ref, lse_ref, dl_ref, b_ref, dq_ref, dk_ref, dv_ref, g_ref):
        sels = _head_masks()
        dq_ref[...] = jnp.zeros_like(dq_ref)
        dk_ref[...] = jnp.zeros_like(dk_ref)
        dv_ref[...] = jnp.zeros_like(dv_ref)
        g_ref[...] = jnp.zeros_like(g_ref)
        for p, (_, dil) in enumerate(DIL_PATTERNS):
            seg = s // dil // BLK

            def block(b, _, p=p, seg=seg, dil=dil):
                cur, prev, has_prev = _residue_rows(b, seg, dil)
                qb = (q_ref[cur, :] * 0.125).astype(MXU_DTYPE)
                dob = do_ref[cur, :].astype(MXU_DTYPE)
                kp, kc = k_ref[prev, :].astype(MXU_DTYPE), k_ref[cur, :].astype(MXU_DTYPE)
                vp, vc = v_ref[prev, :].astype(MXU_DTYPE), v_ref[cur, :].astype(MXU_DTYPE)
                dq = jnp.zeros((BLK, BLK), F32)
                dkp, dkc, dvp, dvc = dq, dq, dq, dq
                for h, sel in enumerate(sels):
                    kph, kch = jnp.where(sel, kp, 0.0), jnp.where(sel, kc, 0.0)
                    qh, doh = jnp.where(sel, qb, 0.0), jnp.where(sel, dob, 0.0)
                    lse_h = lse_ref[h, cur, :]
                    dlt = dl_ref[h, cur, :]
                    zp, zc = _dil_scores(qb, kph, kch, b_ref.at[p], h, has_prev)
                    pp = jnp.exp(zp - lse_h)
                    pc = jnp.exp(zc - lse_h)
                    dsp = pp * (_dot_nt(dob, jnp.where(sel, vp, 0.0)) - dlt)
                    dsc = pc * (_dot_nt(dob, jnp.where(sel, vc, 0.0)) - dlt)
                    g_ref[h, p, :, 0:BLK] += dsp
                    g_ref[h, p, :, BLK:2 * BLK] += dsc
                    dsp = dsp.astype(MXU_DTYPE)
                    dsc = dsc.astype(MXU_DTYPE)
                    dq = dq + _dot(dsp, kph) + _dot(dsc, kch)
                    dkp, dkc = dkp + _dot_tn(dsp, qh), dkc + _dot_tn(dsc, qh)
                    dvp, dvc = dvp + _dot_tn(pp, doh), dvc + _dot_tn(pc, doh)
                dq_ref[cur, :] += dq * 0.125
                dk_ref[prev, :] += dkp
                dk_ref[cur, :] += dkc
                dv_ref[prev, :] += dvp
                dv_ref[cur, :] += dvc
                return 0

            lax.fori_loop(0, nblk, block, 0, unroll=4)

    bias_spec = pl.BlockSpec((3, 2, BLK, 2 * BLK), lambda b, p: (0, p, 0, 0))
    cols, stats = _col_spec(s, 0), _pair_spec(s, BLK)
    return _host_call(
        body, carry, name=name, grid=(nb, 2),
        in_specs=[_col_spec(s, 6), _col_spec(s, 8), _col_spec(s, 10), _col_spec(s, 2), stats, stats, bias_spec],
        out_specs=[cols, cols, cols, pl.BlockSpec((None, 2, 3, BLK, 2 * BLK), lambda b, p: (b, p, 0, 0, 0))],
        out_shape=[jax.ShapeDtypeStruct((nb * s, 2 * BLK), F32)] * 3 + [jax.ShapeDtypeStruct((nb, N_HEADS, 3, BLK, 2 * BLK), F32)],
        operands=(proj, proj, proj, dmixed, lse, delta, bias))


def _bucket_reduce(gbias, table, name):
    nb = gbias.shape[0]

    def body(g_ref, t_ref, o_ref):
        row = _iota2((8, BLK), 0)
        lane = _iota2((8, BLK), 1)
        gsum = [[sum(g_ref[b, h, p] for b in range(nb)) for p in range(3)] for h in range(N_HEADS)]

        def bucket(k, acc):
            for h in range(N_HEADS):
                tot = sum(jnp.sum(jnp.where(t_ref[p] == k, gsum[h][p], 0.0)) for p in range(3))
                acc = acc + jnp.where((row == h) & (lane == k), tot, 0.0)
            return acc

        o_ref[...] = lax.fori_loop(0, REL_BUCKETS, bucket, jnp.zeros((8, BLK), F32))

    vm = pl.BlockSpec(memory_space=pltpu.VMEM)
    return pl.pallas_call(
        body, name=name, in_specs=[vm, vm], out_specs=vm, out_shape=jax.ShapeDtypeStruct((8, BLK), F32),
        compiler_params=pltpu.CompilerParams(vmem_limit_bytes=VMEM_LIMIT),
    )(gbias, table)


def _place():
    x, y, c = lax.axis_index("x"), lax.axis_index("y"), lax.axis_index("c")
    others = [(1 - x, y), (x, 1 - y), (1 - x, 1 - y)]
    return x, y, c, others


def _remote(src, dst, send_sem, recv_sem, to):
    return pltpu.make_async_remote_copy(src_ref=src, dst_ref=dst, send_sem=send_sem, recv_sem=recv_sem,
                                        device_id=to, device_id_type=MESH)


_HBM = pl.BlockSpec(memory_space=pl.ANY)


class _Exchange:
    def __init__(self, operands, out_shape, n_copies, copies, aliases=None):
        self.operands, self.out_shape, self.n_copies, self.copies = list(operands), list(out_shape), n_copies, copies
        self.aliases = dict(aliases or {})

    def sem_shapes(self):
        return [pltpu.SemaphoreType.DMA((self.n_copies,)), pltpu.SemaphoreType.DMA((self.n_copies,))]


def _start_all(sends):
    for cp in sends:
        cp.start()


def _wait_all(sends, arrivals):
    for cp in arrivals:
        cp.wait_recv()
    for cp in sends:
        cp.wait_send()


def _run_exchange(ex, name):
    ni = len(ex.operands)

    def body(*refs):
        sends, arrivals = ex.copies(refs[:ni], refs[ni:-2], refs[-2], refs[-1])
        _start_all(sends)
        _wait_all(sends, arrivals)

    return list(pl.pallas_call(
        body, name=name, in_specs=[_HBM] * ni, out_specs=[_HBM] * len(ex.out_shape), out_shape=ex.out_shape,
        scratch_shapes=ex.sem_shapes(), input_output_aliases=ex.aliases)(*ex.operands))


def _host_call(body, carry, *, name, grid, in_specs, out_specs, out_shape, operands, scratch_shapes=()):
    in_specs, out_specs, out_shape, scratch_shapes = list(in_specs), list(out_specs), list(out_shape), list(scratch_shapes)
    if carry is None:
        res = pl.pallas_call(body, name=name, grid=grid, in_specs=in_specs, out_specs=out_specs, out_shape=out_shape,
                             scratch_shapes=scratch_shapes, compiler_params=_cparams(*["parallel"] * len(grid)))(*operands)
        return list(res), []
    n_in, n_out, n_scr, c_in, c_out = len(in_specs), len(out_specs), len(scratch_shapes), len(carry.operands), len(carry.out_shape)
    steps = math.prod(grid)

    def wrapped(*refs):
        ins, refs = refs[:n_in], refs[n_in:]
        c_ins, refs = refs[:c_in], refs[c_in:]
        outs, refs = refs[:n_out], refs[n_out:]
        c_outs, refs = refs[:c_out], refs[c_out:]
        scr, (send_sems, recv_sems) = refs[:n_scr], refs[n_scr:]
        step = 0
        for d, size in enumerate(grid):
            step = step * size + pl.program_id(d)

        @pl.when(step == 0)
        def _():
            _start_all(carry.copies(c_ins, c_outs, send_sems, recv_sems)[0])

        body(*ins, *outs, *scr)

        @pl.when(step == steps - 1)
        def _():
            _wait_all(*carry.copies(c_ins, c_outs, send_sems, recv_sems))

    res = pl.pallas_call(
        wrapped, name=name, grid=grid, in_specs=in_specs + [_HBM] * c_in, out_specs=out_specs + [_HBM] * c_out,
        out_shape=out_shape + carry.out_shape, scratch_shapes=scratch_shapes + carry.sem_shapes(),
        input_output_aliases={n_in + i: n_out + j for i, j in carry.aliases.items()},
        compiler_params=_cparams(*["arbitrary"] * len(grid)))(*operands, *carry.operands)
    return list(res[:n_out]), list(res[n_out:])


def _half(which, rows):
    h = rows // 2
    return pl.ds(pl.multiple_of(which * h, 16), h)


def _like(arrays, shape_of=lambda t: t.shape):
    return [jax.ShapeDtypeStruct(shape_of(t), t.dtype) for t in arrays]


def _gather_ici(shards, layer):
    n = len(shards)

    def copies(ins, outs, send_sems, recv_sems, base=0):
        x, y, c, others = _place()
        me = 2 * x + y
        sends, arrivals = [], []
        for a in range(n):
            rows = _half(c, shards[a].shape[1])
            for k, (ox, oy) in enumerate(others):
                sems = (send_sems.at[base + 3 * a + k], recv_sems.at[base + 3 * a + k],(ox, oy, c))
                sends.append(_remote(ins[a].at[layer, rows], outs[a].at[me, rows], *sems))
                landed = outs[a].at[2 * ox + oy, rows]
                arrivals.append(_remote(landed, landed, *sems))
        return sends, arrivals

    return _Exchange(shards, _like(shards, lambda t: (N_CHIPS,) + t.shape[1:]), 3 * n, copies)


def _gather_d2d(gathered):
    n = len(gathered)

    def copies(ins, outs, send_sems, recv_sems, base=0):
        x, y, c, others = _place()
        sends, arrivals = [], []
        for a in range(n):
            r = gathered[a].shape[1]
            for k, (ox, oy) in enumerate(others):
                sems = (send_sems.at[base + 3 * a + k], recv_sems.at[base + 3 * a + k],(x, y, 1 - c))
                mine, theirs = outs[a].at[2 * ox + oy, _half(c, r)], outs[a].at[2 * ox + oy, _half(1 - c, r)]
                sends.append(_remote(mine, mine, *sems))
                arrivals.append(_remote(theirs, theirs, *sems))
        return sends, arrivals

    return _Exchange(gathered, _like(gathered), 3 * n, copies, aliases={a: a for a in range(n)})


def _swap_halves(g):
    n = len(g)

    def copies(ins, outs, send_sems, recv_sems, base=0):
        x, y, c, _ = _place()
        sends, arrivals = [], []
        for a in range(n):
            sems = (send_sems.at[base + a], recv_sems.at[base + a], (x, y, 1 - c))
            sends.append(_remote(ins[a].at[:, _half(1 - c, g[a].shape[1])], outs[a], *sems))
            arrivals.append(_remote(outs[a], outs[a], *sems))
        return sends, arrivals

    return _Exchange(g, _like(g, lambda t: (t.shape[0], t.shape[1] // 2, t.shape[2])), n, copies)


def _scatter_shards(ps):
    n = len(ps)

    def copies(ins, outs, send_sems, recv_sems, base=0):
        x, y, c, others = _place()
        me = 2 * x + y
        sends, arrivals = [], []
        for a in range(n):
            for k, (ox, oy) in enumerate(others):
                sems = (send_sems.at[base + 3 * a + k], recv_sems.at[base + 3 * a + k],(ox, oy, c))
                sends.append(_remote(ins[a].at[2 * ox + oy], outs[a].at[me], *sems))
                slot = outs[a].at[2 * ox + oy]
                arrivals.append(_remote(slot, slot, *sems))
        return sends, arrivals

    return _Exchange(ps, _like(ps), 3 * n, copies)


def _share_halves(mine):
    n = len(mine)

    def copies(ins, outs, send_sems, recv_sems, base=0):
        x, y, c, _ = _place()
        sends, arrivals = [], []
        for a in range(n):
            sems = (send_sems.at[base + a], recv_sems.at[base + a], (x, y, 1 - c))
            sends.append(_remote(ins[a], outs[a], *sems))
            arrivals.append(_remote(outs[a], outs[a], *sems))
        return sends, arrivals

    return _Exchange(mine, _like(mine), n, copies)


def _row_tile(r):
    for cand in (256, 352, 128):
        if r % cand == 0:
            return cand
    return r


def _pair_sum(g, other, core, name):
    ns, h, w = other.shape
    tr = _row_tile(h)
    per_half = h // tr

    def body(core_ref, g_ref, o_ref, out_ref):
        out_ref[...] = (g_ref[...] + o_ref[...]).astype(out_ref.dtype)

    blk = pl.BlockSpec((None, tr, w), lambda k, i, core_ref: (k, i, 0))
    grid_spec = pltpu.PrefetchScalarGridSpec(
        num_scalar_prefetch=1, grid=(ns, per_half),
        in_specs=[pl.BlockSpec((None, tr, w), lambda k, i, core_ref: (k, core_ref[0] * per_half + i, 0)), blk], out_specs=blk)
    return pl.pallas_call(
        body, name=name, grid_spec=grid_spec, out_shape=jax.ShapeDtypeStruct((ns, h, w), MXU_DTYPE),
        compiler_params=_cparams("parallel", "parallel"),
    )(core.reshape(1).astype(jnp.int32), g, other)


def _chip_sum(q, p, chip, name):
    ns, r, w = q.shape
    tr = _row_tile(r)

    def body(chip_ref, q_ref, own_ref, out_ref):
        me = chip_ref[0]
        own = own_ref[...].astype(F32)
        terms = [jnp.where(me == k, own, q_ref[k].astype(F32)) for k in range(ns)]
        out_ref[...] = ((terms[0] + terms[1]) + terms[2]) + terms[3]

    grid_spec = pltpu.PrefetchScalarGridSpec(
        num_scalar_prefetch=1, grid=(r // tr,),
        in_specs=[pl.BlockSpec((ns, tr, w), lambda i, chip_ref: (0, i, 0)),
                  pl.BlockSpec((None, tr, w), lambda i, chip_ref: (chip_ref[0], i, 0))],
        out_specs=pl.BlockSpec((tr, w), lambda i, chip_ref: (i, 0)))
    return pl.pallas_call(
        body, name=name, grid_spec=grid_spec, out_shape=jax.ShapeDtypeStruct((r, w), F32),
        compiler_params=_cparams("parallel"),
    )(chip.reshape(1).astype(jnp.int32), q, p)


def _merge(exchanges):
    if len(exchanges) <= 1:
        return exchanges[0] if exchanges else None
    operands, out_shape, aliases, spans, n = [], [], {}, [], 0
    for ex in exchanges:
        spans.append((len(operands), len(out_shape), n))
        aliases.update({len(operands) + i: len(out_shape) + j for i, j in ex.aliases.items()})
        operands += ex.operands
        out_shape += ex.out_shape
        n += ex.n_copies

    def copies(ins, outs, send_sems, recv_sems, base=0):
        sends, arrivals = [], []
        for ex, (i0, o0, s0) in zip(exchanges, spans):
            s, a = ex.copies(ins[i0:i0 + len(ex.operands)], outs[o0:o0 + len(ex.out_shape)], send_sems, recv_sems, base + s0)
            sends += s
            arrivals += a
        return sends, arrivals

    return _Exchange(operands, out_shape, n, copies, aliases)


def _take(hooks, host):
    stages = (hooks or {}).pop(host, [])
    exchanges = [make() for make, _ in stages]

    def finish(results):
        for (_, done), ex in zip(stages, exchanges):
            done(results[:len(ex.out_shape)])
            results = results[len(ex.out_shape):]

    return _merge(exchanges), finish


def _hook(hooks, host, make, done):
    hooks.setdefault(host, []).append((make, done))


class _WeightPrefetch:
    def __init__(self, names, shards, layer, chip):
        self.names, self.shards, self.layer, self.chip, self.result = names, [shards[n] for n in names], layer, chip, None

    def first(self):
        return _gather_ici(self.shards, self.layer)

    def got_first(self, arrived):
        self.arrived = arrived

    def second(self):
        return _gather_d2d(self.arrived)

    def got_second(self, gathered):
        self.result = {name: lax.dynamic_update_index_in_dim(got, own[self.layer], self.chip, 0)
                       for name, got, own in zip(self.names, gathered, self.shards)}

    def ride(self, hooks, first_host, second_host):
        _hook(hooks, first_host, self.first, self.got_first)
        _hook(hooks, second_host, self.second, self.got_second)

    def run(self, tag):
        self.got_first(_run_exchange(self.first(), f"gather_ici_{tag}"))
        self.got_second(_run_exchange(self.second(), f"gather_d2d_{tag}"))


class _GradReduce:
    def __init__(self, g, chip, core, tag):
        self.names, self.g, self.chip, self.core, self.tag, self.result = list(g), list(g.values()), chip, core, tag, None

    def swap(self):
        return _swap_halves(self.g)

    def got_swap(self, theirs):
        self.pair = [_pair_sum(g, t, self.core, f"pair_sum_{n}_{self.tag}") for n, g, t in zip(self.names, self.g, theirs)]

    def scatter(self):
        return _scatter_shards(self.pair)

    def got_scatter(self, q):
        self.mine = [_chip_sum(qa, pa, self.chip, f"chip_sum_{n}_{self.tag}") for n, qa, pa in zip(self.names, q, self.pair)]

    def share(self):
        return _share_halves(self.mine)

    def got_share(self, theirs):
        self.result = {n: jnp.where(self.core == 0, jnp.concatenate([a, b]), jnp.concatenate([b, a]))
                       for n, a, b in zip(self.names, self.mine, theirs)}

    def ride(self, hooks, swap_host, scatter_host, share_host):
        _hook(hooks, swap_host, self.swap, self.got_swap)
        _hook(hooks, scatter_host, self.scatter, self.got_scatter)
        _hook(hooks, share_host, self.share, self.got_share)

    def run(self):
        self.got_swap(_run_exchange(self.swap(), f"swap_halves_{self.tag}"))
        self.got_scatter(_run_exchange(self.scatter(), f"scatter_shards_{self.tag}"))
        self.got_share(_run_exchange(self.share(), f"share_halves_{self.tag}"))


class _LayerWeights:
    def __init__(self, gathered):
        self.gathered, self.made = gathered, {}

    def __getitem__(self, key):
        if key not in self.made:
            cols = lambda t: jnp.swapaxes(t, 0, 1).reshape(t.shape[1], -1)
            rows = lambda t: t.reshape(-1, t.shape[2])
            if key == "w_in":
                made = jnp.pad(cols(self.gathered("w_in")), ((0, 0), (0, PROJ_PAD - PROJ)))
            elif key == "w_gu":
                made = jnp.concatenate([cols(self.gathered("w_gate")), cols(self.gathered("w_up"))], axis=-1)
            else:
                made = rows(self.gathered(key))
            self.made[key] = made
        return self.made[key]


def _gather_small(pk, name):
    rows, w = pk.shape

    def body(pk_ref, all_ref, sum_ref, send_sems, recv_sems):
        x, y, c, _ = _place()
        me = 4 * x + 2 * y + c
        all_ref[me] = pk_ref[...]
        flips = [(fx, fy, fc) for fx in (0, 1) for fy in (0, 1) for fc in (0, 1)][1:]
        peers = [(x ^ fx, y ^ fy, c ^ fc) for fx, fy, fc in flips]
        sends = [_remote(pk_ref, all_ref.at[me], send_sems.at[k], recv_sems.at[k], peer) for k, peer in enumerate(peers)]
        for cp in sends:
            cp.start()
        for k, (px, py, pc) in enumerate(peers):
            slot = all_ref.at[4 * px + 2 * py + pc]
            _remote(slot, slot, send_sems.at[k], recv_sems.at[k], (px, py, pc)).wait_recv()
        for cp in sends:
            cp.wait_send()
        total = all_ref[0]
        for d in range(1, N_DEV):
            total = total + all_ref[d]
        sum_ref[...] = total

    vm = pl.BlockSpec(memory_space=pltpu.VMEM)
    return pl.pallas_call(
        body, name=name, in_specs=[vm], out_specs=[vm, vm],
        out_shape=[jax.ShapeDtypeStruct((N_DEV, rows, w), F32), jax.ShapeDtypeStruct((rows, w), F32)],
        scratch_shapes=[pltpu.SemaphoreType.DMA((7,)), pltpu.SemaphoreType.DMA((7,))],
    )(pk)


def _row_layout(c, nb, s):
    ch = jnp.swapaxes(c[:, :N_HEADS].reshape(nb, s, N_HEADS), 1, 2)
    ccol = jnp.broadcast_to(ch[..., None], (nb, N_HEADS, s, ATT))
    crow = jnp.broadcast_to(ch.reshape(nb, N_HEADS, s // ATT, 1, ATT), (nb, N_HEADS, s // ATT, 8, ATT))
    return ccol, crow


def _dil_bias(rel_bias, name):
    def body(rel_ref, t_ref, o_ref):
        for p in range(len(DIL_PATTERNS)):
            table = t_ref[p]

            def bucket(k, accs, table=table):
                return tuple(jnp.where(table == k, rel_ref[k, h], acc) for h, acc in enumerate(accs))

            accs = lax.fori_loop(0, REL_BUCKETS, bucket, tuple(jnp.full((BLK, 2 * BLK), NEG, F32) for _ in range(N_HEADS)))
            for h in range(N_HEADS):
                o_ref[p, h] = accs[h]

    vm = pl.BlockSpec(memory_space=pltpu.VMEM)
    return pl.pallas_call(
        body, name=name, in_specs=[pl.BlockSpec(memory_space=pltpu.SMEM), vm], out_specs=vm,
        out_shape=jax.ShapeDtypeStruct((len(DIL_PATTERNS), N_HEADS, BLK, 2 * BLK), F32),
        compiler_params=pltpu.CompilerParams(vmem_limit_bytes=VMEM_LIMIT),
    )(rel_bias, jnp.asarray(_bucket_table()))


def _layer_forward(x, x_b, wts, small, bias, nb, s, tag, hooks=None):
    proj = _matmul(x_b, wts["w_in"], "proj", tag)

    carry, finish = _take(hooks, "sb_fwd")
    (o_sb, tails_sb), carried = _sb_fwd(proj, nb, s, f"sb_fwd_{tag}", carry)
    finish(carried)

    carry, finish = _take(hooks, "dil_fwd")
    (o_dl, lse_dl), carried = _dil_attention_fwd(proj, bias, nb, s, f"dil_fwd_{tag}", carry)
    finish(carried)

    fb = jnp.zeros((8, BLK), F32).at[0, :N_HEADS].set(small["f_bias"])
    csum = _fox_gates_fwd(proj, fb, nb, s, f"fox_gates_{tag}")
    ccol, crow = _row_layout(csum, nb, s)
    carry, finish = _take(hooks, "fox_fwd")
    (o_fx, lse_fx), carried = _fox_fwd(proj, ccol, crow, nb, s, f"fox_fwd_{tag}", carry)
    finish(carried)

    cw = jnp.zeros((8, CONV_W), F32).at[:3].set(small["conv_w"])
    o_cv = _conv_fwd(proj, cw, nb, s, f"conv_fwd_{tag}")

    mixed = jnp.concatenate([o_sb, o_dl, o_fx, o_cv], axis=-1).astype(MXU_DTYPE)
    pre1, x1, x1_b = _matmul_post_norm(mixed, wts["w_out"], x, small["ln1_g"], small["ln1_b"], f"out_proj_ln1_{tag}")
    carry, finish = _take(hooks, "ffn_in")
    (gate, up, hid), carried = _ffn_in(x1_b, wts["w_gu"], f"ffn_in_{tag}", carry)
    finish(carried)
    pre2, x2, x2_b = _matmul_post_norm(hid, wts["w_down"], x1, small["ln2_g"], small["ln2_b"], f"ffn_out_ln2_{tag}")
    saved = dict(x_b=x_b, proj=proj, tails_sb=tails_sb, bias=bias, o_dl=o_dl, lse_dl=lse_dl, fb=fb, ccol=ccol, crow=crow, o_fx=o_fx,
                 lse_fx=lse_fx, cw=cw, mixed=mixed, pre1=pre1, x1_b=x1_b, gate=gate, up=up, hid=hid, pre2=pre2)
    return (x2, x2_b), saved


def _layer_backward(dx2, sv, wts, small, nb, s, tag, hooks=None, ffn_grads_ready=None):
    t = nb * s
    dpre2, dpre2_b, dgb2 = _ln_bwd(dx2, sv["pre2"], small["ln2_g"], f"ln2_bwd_{tag}")
    carry, finish = _take(hooks, "ffn_out_dx")
    (dgate, dup), carried = _ffn_out_dx(dpre2_b, wts["w_down"], sv["gate"], sv["up"], f"ffn_out_dx_{tag}", carry)
    finish(carried)
    dw_down = _matmul(sv["hid"], dpre2_b, "ffn_out_dw", tag, trans_a=True)
    dx1 = _ffn_in_dx(dgate, dup, wts["w_gu"], dpre2, f"ffn_in_dx_{tag}")
    x1_b = sv["x1_b"]
    dw_gate = _matmul(x1_b, dgate, "ffn_in_dw", f"{tag}_gate", trans_a=True)
    dw_up = _matmul(x1_b, dup, "ffn_in_dw", f"{tag}_up", trans_a=True)

    dpre1, dpre1_b, dgb1 = _ln_bwd(dx1, sv["pre1"], small["ln1_g"], f"ln1_bwd_{tag}")
    dmixed = _matmul(dpre1_b, wts["w_out"], "out_proj_dx", tag, trans_b=True)
    dw_out = _matmul(sv["mixed"], dpre1_b, "out_proj_dw", tag, trans_a=True)
    if ffn_grads_ready:
        ffn_grads_ready(dict(w_down=dw_down, w_gate=dw_gate, w_up=dw_up, w_out=dw_out))
    proj = sv["proj"]

    carry, finish = _take(hooks, "sb_bwd")
    (dq_sb, dk_sb, dv_sb), carried = _sb_bwd(proj, dmixed, sv["tails_sb"], nb, s, f"sb_bwd_{tag}", carry)
    finish(carried)

    delta_dl = _delta_kernel(dmixed, sv["o_dl"], nb, s, f"dil_delta_{tag}")
    carry, finish = _take(hooks, "dil_bwd")
    (dq_dl, dk_dl, dv_dl, gbias), carried = _dil_attention_bwd(proj, dmixed, sv["lse_dl"], delta_dl, sv["bias"], nb, s,
                                                               f"dil_bwd_{tag}", carry)
    finish(carried)
    drel = _bucket_reduce(gbias, jnp.asarray(_bucket_table()), f"rel_bias_grad_{tag}")

    carry, finish = _take(hooks, "fox_bwd")
    (dq_fx, dk_fx, dv_fx, dcol), carried = _fox_bwd(proj, dmixed, sv["lse_fx"], sv["ccol"], sv["crow"], nb, s,
                                                    f"fox_bwd_{tag}", carry)
    finish(carried)
    dcs = -jnp.swapaxes(dcol[:, :, :, 0, :].reshape(nb, N_HEADS, s), 1, 2).reshape(t, N_HEADS)
    dcs = jnp.pad(dcs, ((0, 0), (0, BLK - N_HEADS)))
    dfx, dfb = _fox_gates_bwd(dcs, proj, sv["fb"], nb, s, f"fox_gates_bwd_{tag}")

    dgates, dcw = _conv_bwd(dmixed, proj, sv["cw"], nb, s, f"conv_bwd_{tag}")

    dproj = jnp.concatenate([dq_sb, dk_sb, dv_sb, dq_dl, dk_dl, dv_dl, dq_fx, dk_fx, dv_fx, dgates, dfx],
                            axis=-1).astype(MXU_DTYPE)
    dx = _matmul(dproj, wts["w_in"], "proj_dx", tag, add=dpre1, add_scale=ALPHA, trans_b=True)
    dw_in = _matmul(sv["x_b"], dproj, "proj_dw", tag, trans_a=True)

    grads = dict(w_in=dw_in[:, :PROJ], w_out=dw_out, w_gate=dw_gate, w_up=dw_up, w_down=dw_down,
                 ln1_g=dgb1[0], ln1_b=dgb1[1], ln2_g=dgb2[0], ln2_b=dgb2[1], conv_w=dcw[:3], f_bias=dfb[0, :N_HEADS],
                 rel_bias=drel[:N_HEADS, :REL_BUCKETS].T)
    return dx, grads


class _NoExchanges:
    def forward_hooks(self, layer):
        return None

    def backward_hooks(self, layer):
        return None

    def ffn_grads_ready(self, layer):
        return None

    def layer_done(self, layer, grads):
        pass


def _local_step(x, target, weights_of, small_all, schedule=None):
    schedule = schedule or _NoExchanges()
    nb, s, d = x.shape
    h = x.reshape(nb * s, d)
    h_b = h.astype(MXU_DTYPE)
    bias = _dil_bias(small_all[0]["rel_bias"], "dil_bias")
    saved = []
    for layer in range(DEPTH):
        wts = weights_of(layer)
        (h, h_b), sv = _layer_forward(h, h_b, wts, small_all[layer], bias, nb, s, f"l{layer}", schedule.forward_hooks(layer))
        saved.append((sv, wts))
    dy, lossp = _loss_kernel(h, target.reshape(nb * s, d), "loss")
    grads = [None] * DEPTH
    for layer in reversed(range(DEPTH)):
        sv, wts = saved[layer]
        dy, grads[layer] = _layer_backward(dy, sv, wts, small_all[layer], nb, s, f"l{layer}",
                                           schedule.backward_hooks(layer), schedule.ffn_grads_ready(layer))
        schedule.layer_done(layer, grads[layer])
    return lossp, dy.reshape(nb, s, d), grads


_BIG = ("w_in", "w_out", "w_gate", "w_up", "w_down")
_COL_SHARDED = ("w_in", "w_gate", "w_up")


class _Schedule:
    def __init__(self, shards, chip, core):
        self.chip, self.core, self.reduces = chip, core, [[] for _ in range(DEPTH)]
        first = _WeightPrefetch(["w_in"], shards, 0, chip)
        first.run("l0_w_in")
        rest = _WeightPrefetch(["w_out", "w_gate", "w_up", "w_down"], shards, 0, chip)
        ahead_a = _WeightPrefetch(["w_in", "w_out", "w_down"], shards, 1, chip)
        ahead_b = _WeightPrefetch(["w_gate", "w_up"], shards, 1, chip)
        self.fetches = [[first, rest], [ahead_a, ahead_b]]
        self.forward, self.backward = [{} for _ in range(DEPTH)], [{} for _ in range(DEPTH)]
        rest.ride(self.forward[0], "sb_fwd", "fox_fwd")
        ahead_a.ride(self.forward[0], "dil_fwd", "ffn_in")
        ahead_b.ride(self.forward[0], "fox_fwd", "ffn_in")

    def weights(self, layer):
        def gathered(name):
            return next(f.result[name] for f in self.fetches[layer] if name in f.names)
        return _LayerWeights(gathered)

    def forward_hooks(self, layer):
        return self.forward[layer]

    def backward_hooks(self, layer):
        return self.backward[layer]

    def _reduce(self, layer, grads, tag):
        red = _GradReduce({name: _by_chip(name, g) for name, g in grads.items()}, self.chip, self.core, tag)
        self.reduces[layer].append(red)
        return red

    def ffn_grads_ready(self, layer):
        if layer != 0:
            return None

        def ready(early):
            self._reduce(0, early, "l0_early").ride(self.backward[0], "sb_bwd", "dil_bwd", "fox_bwd")

        return ready

    def layer_done(self, layer, grads):
        if layer == 1:
            self._reduce(1, {name: grads[name] for name in _BIG}, "l1").ride(self.backward[0], "ffn_out_dx", "sb_bwd", "fox_bwd")
        else:
            self._reduce(0, dict(w_in=grads["w_in"]), "l0_w_in").run()

    def reduced(self, layer, name):
        return next(r.result[name] for r in self.reduces[layer] if name in r.names)


def _by_chip(name, g):
    if name in _COL_SHARDED:
        return jnp.swapaxes(g.reshape(g.shape[0], N_CHIPS, -1), 0, 1)
    return g.reshape(N_CHIPS, -1, g.shape[1])


_SMALL_LAYOUT = (("ln1_g", 0), ("ln1_b", 2), ("ln2_g", 4), ("ln2_b", 6), ("conv_w", 8))
_ROW_MISC = 10
_ROW_LOSS = 11


def _pack_small(per_layer, rel_bias, loss=None):
    pk = jnp.zeros((SMALL_ROWS, D_MODEL), F32)
    for name, row in _SMALL_LAYOUT:
        for l in range(DEPTH):
            v = per_layer[l][name].reshape(-1)
            pk = pk.at[row + l, :v.shape[0]].set(v)
    fb = jnp.concatenate([per_layer[l]["f_bias"] for l in range(DEPTH)])
    pk = pk.at[_ROW_MISC, :2 * N_HEADS].set(fb)
    pk = pk.at[_ROW_MISC, BLK:BLK + REL_BUCKETS * N_HEADS].set(rel_bias.reshape(-1))
    if loss is not None:
        pk = pk.at[_ROW_LOSS, 0].set(loss)
    return pk


def _unpack_small(pk, conv_cols):
    out = {}
    for name, row in _SMALL_LAYOUT:
        n = 3 * conv_cols if name == "conv_w" else D_MODEL
        v = pk[row:row + DEPTH, :n]
        out[name] = v.reshape(DEPTH, 3, conv_cols) if name == "conv_w" else v
    out["f_bias"] = pk[_ROW_MISC, :2 * N_HEADS].reshape(DEPTH, N_HEADS)
    out["rel_bias"] = pk[_ROW_MISC, BLK:BLK + REL_BUCKETS * N_HEADS].reshape(REL_BUCKETS, N_HEADS)
    return out


_WEIGHTS = ("w_in", "f_bias", "conv_w", "w_out", "rel_bias", "ln1_g", "ln1_b", "w_gate", "w_up", "w_down", "ln2_g", "ln2_b")


def kernel(x, w_in, f_bias, conv_w, w_out, rel_bias, ln1_g, ln1_b, w_gate, w_up, w_down, ln2_g, ln2_b, loss_target, m_w_in, m_f_bias, m_conv_w, m_w_out, m_rel_bias, m_ln1_g, m_ln1_b, m_w_gate, m_w_up, m_w_down, m_ln2_g, m_ln2_b, v_w_in, v_f_bias, v_conv_w, v_w_out, v_rel_bias, v_ln1_g, v_ln1_b, v_w_gate, v_w_up, v_w_down, v_ln2_g, v_ln2_b):
    w = dict(w_in=w_in, f_bias=f_bias, conv_w=conv_w, w_out=w_out, rel_bias=rel_bias, ln1_g=ln1_g, ln1_b=ln1_b,
             w_gate=w_gate, w_up=w_up, w_down=w_down, ln2_g=ln2_g, ln2_b=ln2_b)
    m = dict(w_in=m_w_in, f_bias=m_f_bias, conv_w=m_conv_w, w_out=m_w_out, rel_bias=m_rel_bias, ln1_g=m_ln1_g,
             ln1_b=m_ln1_b, w_gate=m_w_gate, w_up=m_w_up, w_down=m_w_down, ln2_g=m_ln2_g, ln2_b=m_ln2_b)
    v = dict(w_in=v_w_in, f_bias=v_f_bias, conv_w=v_conv_w, w_out=v_w_out, rel_bias=v_rel_bias, ln1_g=v_ln1_g,
             ln1_b=v_ln1_b, w_gate=v_w_gate, w_up=v_w_up, w_down=v_w_down, ln2_g=v_ln2_g, ln2_b=v_ln2_b)
    chip = 2 * lax.axis_index("x") + lax.axis_index("y")
    core = lax.axis_index("c")
    conv_shard = CONV_W // N_CHIPS

    schedule = _Schedule({name: w[name].astype(MXU_DTYPE) for name in _BIG}, chip, core)
    cw_pk = jnp.zeros((8, D_MODEL), F32).at[0, :DEPTH * 3 * conv_shard].set(conv_w.reshape(-1))
    cw_all, _ = _gather_small(cw_pk, "gather_conv_w")
    cw_chips = cw_all[0::2, 0, :DEPTH * 3 * conv_shard].reshape(N_CHIPS, DEPTH, 3, conv_shard)
    conv_full = jnp.moveaxis(cw_chips, 0, 2).reshape(DEPTH, 3, CONV_W)
    small_all = [dict(f_bias=f_bias[l], conv_w=conv_full[l], rel_bias=rel_bias, ln1_g=ln1_g[l], ln1_b=ln1_b[l],
                      ln2_g=ln2_g[l], ln2_b=ln2_b[l]) for l in range(DEPTH)]

    lossp, grad_x, grads = _local_step(x, loss_target, schedule.weights, small_all, schedule)
    big_g = {name: jnp.stack([schedule.reduced(l, name) for l in range(DEPTH)]) for name in _BIG}

    drel = grads[0]["rel_bias"] + grads[1]["rel_bias"]
    small_pk = _pack_small(grads, drel, lossp[0, 0])
    _, small_sum = _gather_small(small_pk, "gather_small_grads")
    loss = small_sum[_ROW_LOSS, 0]
    small_g = _unpack_small(small_sum, CONV_W)
    small_g["conv_w"] = lax.dynamic_slice_in_dim(small_g["conv_w"], chip * conv_shard, conv_shard, axis=2)

    out_g, out_d, out_m, out_v = dict(small_g), {}, {}, {}
    for name in _BIG:
        out_g[name] = big_g[name]
        out_d[name], out_m[name], out_v[name] = _adamw(w[name], big_g[name], m[name], v[name], f"adamw_{name}")
    as_3d = lambda t: t if t.ndim == 3 else t[None]
    for name in _WEIGHTS:
        if name not in _BIG:
            stepped = _adamw(as_3d(w[name]), as_3d(small_g[name]), as_3d(m[name]), as_3d(v[name]), f"adamw_{name}")
            out_d[name], out_m[name], out_v[name] = (t.reshape(w[name].shape) for t in stepped)

    return (loss, grad_x, *[out_g[n] for n in _WEIGHTS], *[out_d[n] for n in _WEIGHTS],
            *[out_m[n] for n in _WEIGHTS], *[out_v[n] for n in _WEIGHTS])
```

```python
import functools
import math

import numpy as np
import jax
import jax.numpy as jnp
from jax import lax
from jax.experimental import pallas as pl
from jax.experimental.pallas import tpu as pltpu

F32 = jnp.float32
BF16 = jnp.bfloat16
MXU_DTYPE = BF16

D_MODEL = 1024
HEAD_DIM = 64
N_HEADS = 4
BLK = 128
ATT = 256
QT = 512
CONV_W = 256
PROJ = 3076
PROJ_PAD = 3200
D_FF = 2816
DEPTH = 2
ALPHA = (2 * DEPTH) ** 0.25
LN_EPS = 1e-5
NEG = -1e30
DIL_PATTERNS = ((128, 1), (512, 4), (2048, 16))
REL_BUCKETS = 32
N_CHIPS = 4
N_DEV = 8
SMALL_ROWS = 16

ADAM_LR = 0.001
ADAM_B1 = 0.9
ADAM_B2 = 0.999
ADAM_EPS = 1e-08
ADAM_WD = 0.01
ADAM_STEP = 10

VMEM_LIMIT = 56 * 2 ** 20
MESH = pl.DeviceIdType.MESH


def _cparams(*sem):
    return pltpu.CompilerParams(dimension_semantics=tuple(sem), vmem_limit_bytes=VMEM_LIMIT)


def _dot(a, b):
    return jnp.dot(a.astype(MXU_DTYPE), b.astype(MXU_DTYPE), preferred_element_type=F32)


def _dot_nt(a, b):
    return lax.dot_general(a.astype(MXU_DTYPE), b.astype(MXU_DTYPE), (((1,), (1,)), ((), ())),
                           preferred_element_type=F32)


def _dot_tn(a, b):
    return lax.dot_general(a.astype(MXU_DTYPE), b.astype(MXU_DTYPE), (((0,), (0,)), ((), ())),
                           preferred_element_type=F32)


def _split_dot(x, ones, passes):
    acc, rest = None, x
    for p in range(passes):
        piece = rest.astype(MXU_DTYPE)
        part = jnp.dot(piece, ones, preferred_element_type=F32)
        acc = part if acc is None else acc + part
        if p + 1 < passes:
            rest = rest - piece.astype(F32)
    return acc


def _split_dot_lhs(ones, x, passes):
    acc, rest = None, x
    for p in range(passes):
        piece = rest.astype(MXU_DTYPE)
        part = jnp.dot(ones, piece, preferred_element_type=F32)
        acc = part if acc is None else acc + part
        if p + 1 < passes:
            rest = rest - piece.astype(F32)
    return acc


def _iota2(shape, axis):
    return lax.broadcasted_iota(jnp.int32, shape, axis)


_TILES = {"proj": (2048, 640, 1024), "ffn_out_dw": (1408, 1024, 2048),
          "ffn_in_dw": (1024, 1408, 2048), "out_proj_dx": (1024, 1024, 1024),
          "out_proj_dw": (1024, 1024, 2048), "proj_dx": (1024, 512, 3200), "proj_dw": (1024, 640, 4096)}


def _matmul(a, b, kind, tag, *, out_dtype=F32, add=None, add_scale=1.0, trans_a=False, trans_b=False):
    k, m = a.shape if trans_a else a.shape[::-1]
    n = b.shape[0] if trans_b else b.shape[1]
    tm, tn, tk = _TILES[kind]
    tm, tk, name = min(tm, m), min(tk, k), f"{kind}_{tag}"
    assert m % tm == 0 and n % tn == 0 and k % tk == 0, (a.shape, b.shape, tm, tn, tk)
    nk = k // tk

    def body(*refs):
        if add is None:
            a_ref, b_ref, o_ref = refs[:3]
            c_ref, scr = None, refs[3:]
        else:
            a_ref, b_ref, c_ref, o_ref = refs[:4]
            scr = refs[4:]
        dot = _dot_tn if trans_a else _dot_nt if trans_b else _dot
        part = dot(a_ref[...], b_ref[...])

        def finish(acc):
            if c_ref is not None:
                acc = acc + add_scale * c_ref[...]
            o_ref[...] = acc.astype(out_dtype)

        if nk == 1:
            finish(part)
        else:
            acc_ref = scr[0]
            kk = pl.program_id(2)

            @pl.when(kk == 0)
            def _():
                acc_ref[...] = part

            @pl.when(kk > 0)
            def _():
                acc_ref[...] += part

            @pl.when(kk == nk - 1)
            def _():
                finish(acc_ref[...])

    b_spec = pl.BlockSpec((tn, tk), lambda i, j, kk: (j, kk)) if trans_b else pl.BlockSpec((tk, tn), lambda i, j, kk: (kk, j))
    a_spec = pl.BlockSpec((tk, tm), lambda i, j, kk: (kk, i)) if trans_a else pl.BlockSpec((tm, tk), lambda i, j, kk: (i, kk))
    in_specs = [a_spec, b_spec]
    operands = [a, b]
    if add is not None:
        in_specs.append(pl.BlockSpec((tm, tn), lambda i, j, kk: (i, j)))
        operands.append(add)
    return pl.pallas_call(
        body, name=name, grid=(m // tm, n // tn, nk), in_specs=in_specs,
        out_specs=pl.BlockSpec((tm, tn), lambda i, j, kk: (i, j)),
        out_shape=jax.ShapeDtypeStruct((m, n), out_dtype),
        scratch_shapes=[pltpu.VMEM((tm, tn), F32)] if nk > 1 else [],
        compiler_params=_cparams("parallel", "parallel", "arbitrary"),
    )(*operands)


def _matmul_post_norm(a, b, xin, g, beta, name):
    t, k = a.shape
    d = b.shape[1]
    tm = 512

    def body(a_ref, b_ref, x_ref, g_ref, beta_ref, pre_ref, y_ref, yb_ref):
        pre = ALPHA * x_ref[...] + _dot(a_ref[...], b_ref[...])
        xhat, _ = _ln_stats(pre)
        y = xhat * g_ref[...] + beta_ref[...]
        pre_ref[...] = pre
        y_ref[...] = y
        yb_ref[...] = y.astype(yb_ref.dtype)

    row = pl.BlockSpec((tm, d), lambda i: (i, 0))
    vec = pl.BlockSpec((1, d), lambda i: (0, 0))
    return pl.pallas_call(
        body, name=name, grid=(t // tm,),
        in_specs=[pl.BlockSpec((tm, k), lambda i: (i, 0)), pl.BlockSpec((k, d), lambda i: (0, 0)), row, vec, vec],
        out_specs=[row, row, row],
        out_shape=[jax.ShapeDtypeStruct((t, d), F32)] * 2 + [jax.ShapeDtypeStruct((t, d), MXU_DTYPE)],
        compiler_params=_cparams("parallel"),
    )(a, b, xin, g.reshape(1, d), beta.reshape(1, d))


def _ffn_in(x1, w_gu, name, carry=None):
    t, d = x1.shape
    tm, tn = 512, D_FF // 2
    nj = D_FF // tn

    def body(x_ref, wg_ref, wu_ref, gate_ref, up_ref, h_ref):
        xb = x_ref[...].astype(MXU_DTYPE)
        gate = _dot(xb, wg_ref[...])
        up = _dot(xb, wu_ref[...])
        gate_ref[...] = gate
        up_ref[...] = up
        h_ref[...] = (gate * (1.0 / (1.0 + jnp.exp(-gate))) * up).astype(h_ref.dtype)

    out = pl.BlockSpec((tm, tn), lambda i, j: (i, j))
    return _host_call(
        body, carry, name=name, grid=(t // tm, nj),
        in_specs=[pl.BlockSpec((tm, d), lambda i, j: (i, 0)), pl.BlockSpec((d, tn), lambda i, j: (0, j)),
                  pl.BlockSpec((d, tn), lambda i, j: (0, nj + j))],
        out_specs=[out, out, out],
        out_shape=[jax.ShapeDtypeStruct((t, D_FF), F32)] * 2 + [jax.ShapeDtypeStruct((t, D_FF), MXU_DTYPE)],
        operands=(x1, w_gu, w_gu))


def _ffn_out_dx(dy, w_down, gate, up, name, carry=None):
    t, d = dy.shape
    tm, tn = 512, D_FF // 2

    def body(dy_ref, w_ref, gate_ref, up_ref, dg_ref, du_ref):
        dh = _dot_nt(dy_ref[...], w_ref[...])
        gate = gate_ref[...]
        sig = 1.0 / (1.0 + jnp.exp(-gate))
        dg_ref[...] = (dh * up_ref[...] * sig * (1.0 + gate * (1.0 - sig))).astype(dg_ref.dtype)
        du_ref[...] = (dh * gate * sig).astype(du_ref.dtype)

    tile = pl.BlockSpec((tm, tn), lambda i, j: (i, j))
    return _host_call(
        body, carry, name=name, grid=(t // tm, D_FF // tn),
        in_specs=[pl.BlockSpec((tm, d), lambda i, j: (i, 0)), pl.BlockSpec((tn, d), lambda i, j: (j, 0)), tile, tile],
        out_specs=[tile, tile], out_shape=[jax.ShapeDtypeStruct((t, D_FF), MXU_DTYPE)] * 2,
        operands=(dy, w_down, gate, up))


def _ffn_in_dx(dgate, dup, w_gu, add, name):
    t = dgate.shape[0]
    d = w_gu.shape[0]
    tm, tk = 1024, D_FF // 2
    nk = D_FF // tk

    def body(dg_ref, du_ref, wg_ref, wu_ref, add_ref, o_ref, acc_ref):
        kk = pl.program_id(1)
        part = _dot_nt(dg_ref[...], wg_ref[...]) + _dot_nt(du_ref[...], wu_ref[...])

        @pl.when(kk == 0)
        def _():
            acc_ref[...] = part

        @pl.when(kk > 0)
        def _():
            acc_ref[...] += part

        @pl.when(kk == nk - 1)
        def _():
            o_ref[...] = acc_ref[...] + ALPHA * add_ref[...]

    act = pl.BlockSpec((tm, tk), lambda i, kk: (i, kk))
    row = pl.BlockSpec((tm, d), lambda i, kk: (i, 0))
    return pl.pallas_call(
        body, name=name, grid=(t // tm, nk),
        in_specs=[act, act, pl.BlockSpec((d, tk), lambda i, kk: (0, kk)), pl.BlockSpec((d, tk), lambda i, kk: (0, nk + kk)), row],
        out_specs=row, out_shape=jax.ShapeDtypeStruct((t, d), F32), scratch_shapes=[pltpu.VMEM((tm, d), F32)],
        compiler_params=_cparams("parallel", "arbitrary"),
    )(dgate, dup, w_gu, w_gu, add)


def _ln_stats(pre):
    mu = jnp.mean(pre, axis=-1, keepdims=True)
    xc = pre - mu
    var = jnp.mean(xc * xc, axis=-1, keepdims=True)
    rstd = lax.rsqrt(var + LN_EPS)
    return xc * rstd, rstd


def _ln_bwd(dy, pre, g, name):
    t, d = dy.shape
    tile = 256

    def body(dy_ref, pre_ref, g_ref, dpre_ref, dpre_b_ref, dgb_ref):
        dyv = dy_ref[...]
        xhat, rstd = _ln_stats(pre_ref[...])
        dxh = dyv * g_ref[...]
        m1 = jnp.mean(dxh, axis=-1, keepdims=True)
        m2 = jnp.mean(dxh * xhat, axis=-1, keepdims=True)
        dpre = rstd * (dxh - m1 - xhat * m2)
        dpre_ref[...] = dpre
        dpre_b_ref[...] = dpre.astype(dpre_b_ref.dtype)

        @pl.when(pl.program_id(0) == 0)
        def _():
            dgb_ref[...] = jnp.zeros_like(dgb_ref)

        dgb_ref[0:1, :] += jnp.sum(dyv * xhat, axis=0, keepdims=True)
        dgb_ref[1:2, :] += jnp.sum(dyv, axis=0, keepdims=True)

    row = pl.BlockSpec((tile, d), lambda i: (i, 0))
    return pl.pallas_call(
        body, name=name, grid=(t // tile,), in_specs=[row, row, pl.BlockSpec((1, d), lambda i: (0, 0))],
        out_specs=[row, row, pl.BlockSpec((8, d), lambda i: (0, 0))],
        out_shape=[jax.ShapeDtypeStruct((t, d), F32), jax.ShapeDtypeStruct((t, d), MXU_DTYPE), jax.ShapeDtypeStruct((8, d), F32)],
        compiler_params=_cparams("arbitrary"),
    )(dy, pre, g.reshape(1, d))


def _loss_kernel(y, target, name):
    t, d = y.shape
    tile = 512

    def body(y_ref, t_ref, dy_ref, l_ref):
        err = y_ref[...] - t_ref[...]
        dy_ref[...] = err * (1.0 / d)

        @pl.when(pl.program_id(0) == 0)
        def _():
            l_ref[...] = jnp.zeros_like(l_ref)

        l_ref[...] += jnp.sum(err * err) * (0.5 / d)

    row = pl.BlockSpec((tile, d), lambda i: (i, 0))
    return pl.pallas_call(
        body, name=name, grid=(t // tile,), in_specs=[row, row],
        out_specs=[row, pl.BlockSpec((8, 128), lambda i: (0, 0))],
        out_shape=[jax.ShapeDtypeStruct((t, d), F32), jax.ShapeDtypeStruct((8, 128), F32)],
        compiler_params=_cparams("arbitrary"),
    )(y, target)


def _adamw(w, g, m, v, name):
    nl, r, c = w.shape
    tr = r
    for cand in (256, 352, 128, 64, 16, 8):
        if r % cand == 0:
            tr = cand
            break

    def body(w_ref, g_ref, m_ref, v_ref, d_ref, nm_ref, nv_ref):
        gv = g_ref[...]
        nm = ADAM_B1 * m_ref[...] + (1.0 - ADAM_B1) * gv
        nv = ADAM_B2 * v_ref[...] + (1.0 - ADAM_B2) * (gv * gv)
        m_hat = nm / (1.0 - ADAM_B1 ** ADAM_STEP)
        v_hat = nv / (1.0 - ADAM_B2 ** ADAM_STEP)
        d_ref[...] = -ADAM_LR * (m_hat / (jnp.sqrt(v_hat) + ADAM_EPS) + ADAM_WD * w_ref[...])
        nm_ref[...] = nm
        nv_ref[...] = nv

    blk = pl.BlockSpec((1, tr, c), lambda l, i: (l, i, 0))
    return pl.pallas_call(
        body, name=name, grid=(nl, r // tr), in_specs=[blk] * 4, out_specs=[blk] * 3,
        out_shape=[jax.ShapeDtypeStruct(w.shape, F32)] * 3, compiler_params=_cparams("parallel", "parallel"),
    )(w, g, m, v)


def _shift_down(u, k, rows):
    return jnp.where(rows >= k, pltpu.roll(u, k, 0), 0.0)


def _shift_up(u, k, rows, s):
    return jnp.where(rows < s - k, pltpu.roll(u, s - k, 0), 0.0)


def _conv_fwd(proj, conv_w, nb, s, name):
    def body(b_ref, c_ref, h_ref, w_ref, o_ref):
        rows = _iota2((s, CONV_W), 0)
        u = c_ref[...] * h_ref[...]
        y = w_ref[2:3, :] * u + w_ref[1:2, :] * _shift_down(u, 1, rows) + w_ref[0:1, :] * _shift_down(u, 2, rows)
        o_ref[...] = b_ref[...] * y

    col = lambda j: pl.BlockSpec((s, CONV_W), lambda b: (b, j))
    return pl.pallas_call(
        body, name=name, grid=(nb,),
        in_specs=[col(9), col(10), col(11), pl.BlockSpec((8, CONV_W), lambda b: (0, 0))],
        out_specs=pl.BlockSpec((s, CONV_W), lambda b: (b, 0)),
        out_shape=jax.ShapeDtypeStruct((nb * s, CONV_W), F32), compiler_params=_cparams("parallel"),
    )(proj, proj, proj, conv_w)


def _conv_bwd(dmixed, proj, conv_w, nb, s, name):
    def body(do_ref, b_ref, c_ref, h_ref, w_ref, dg_ref, dw_ref):
        rows = _iota2((s, CONV_W), 0)
        cg, hg, bg, dout = c_ref[...], h_ref[...], b_ref[...], do_ref[...]
        u = cg * hg
        u1 = _shift_down(u, 1, rows)
        u2 = _shift_down(u, 2, rows)
        y = w_ref[2:3, :] * u + w_ref[1:2, :] * u1 + w_ref[0:1, :] * u2
        dy = dout * bg
        du = w_ref[2:3, :] * dy + w_ref[1:2, :] * _shift_up(dy, 1, rows, s) + w_ref[0:1, :] * _shift_up(dy, 2, rows, s)
        dg_ref[:, 0:CONV_W] = dout * y
        dg_ref[:, CONV_W:2 * CONV_W] = du * hg
        dg_ref[:, 2 * CONV_W:3 * CONV_W] = du * cg

        @pl.when(pl.program_id(0) == 0)
        def _():
            dw_ref[...] = jnp.zeros_like(dw_ref)

        dw_ref[0:1, :] += jnp.sum(dy * u2, axis=0, keepdims=True)
        dw_ref[1:2, :] += jnp.sum(dy * u1, axis=0, keepdims=True)
        dw_ref[2:3, :] += jnp.sum(dy * u, axis=0, keepdims=True)

    col = lambda j: pl.BlockSpec((s, CONV_W), lambda b: (b, j))
    return pl.pallas_call(
        body, name=name, grid=(nb,),
        in_specs=[col(3), col(9), col(10), col(11), pl.BlockSpec((8, CONV_W), lambda b: (0, 0))],
        out_specs=[pl.BlockSpec((s, 3 * CONV_W), lambda b: (b, 0)), pl.BlockSpec((8, CONV_W), lambda b: (0, 0))],
        out_shape=[jax.ShapeDtypeStruct((nb * s, 3 * CONV_W), F32), jax.ShapeDtypeStruct((8, CONV_W), F32)],
        compiler_params=_cparams("arbitrary"),
    )(dmixed, proj, proj, proj, conv_w)


def _col_spec(s, base):
    return pl.BlockSpec((s, BLK), lambda b, p: (b, base + p))


def _qrows(i):
    return pl.ds(pl.multiple_of(i * QT, QT), QT)


def _rows(j):
    return pl.ds(pl.multiple_of(j * ATT, ATT), ATT)


def _keys_upto(i):
    return (i + 1) * (QT // ATT)


def _triangle(keep):
    return keep(_iota2((ATT, ATT), 0), _iota2((ATT, ATT), 1)).astype(MXU_DTYPE)


def _rows128(i):
    return pl.ds(pl.multiple_of(i * BLK, BLK), BLK)


def _log_sigmoid_parts(z):
    e = jnp.exp(-jnp.abs(z))
    l1p = jnp.log(1.0 + e)
    lb = jnp.minimum(z, 0.0) - l1p
    return lb, lb - z, e


def _head_masks():
    lane = _iota2((1, BLK), 1)
    return [(lane >= h * HEAD_DIM) & (lane < (h + 1) * HEAD_DIM) for h in range(2)]


def _split_heads(ref, scr, sels):
    for h, sel in enumerate(sels):
        scr[h] = jnp.where(sel, ref[...], 0.0).astype(MXU_DTYPE)


def _sb_fwd(proj, nb, s, name, carry=None):
    def body(q_ref, k_ref, v_ref, o_ref, tails_ref, km, vm):
        sels = _head_masks()
        _split_heads(k_ref, km, sels)
        _split_heads(v_ref, vm, sels)
        rows = _iota2((QT, ATT), 0)
        cols = _iota2((QT, ATT), 1)
        lane = _iota2((QT, BLK), 1)
        later = _triangle(lambda r, c: r > c)
        tails_ref[...] = jnp.zeros_like(tails_ref)

        def qblock(i, _):
            qi = (q_ref[_qrows(i), :] * 0.125).astype(MXU_DTYPE)

            def kblock(t, state):
                carries, acc = state
                j = _keys_upto(i) - 1 - t
                strict = (cols + (j * ATT - i * QT)) < rows
                out = []
                for h in range(2):
                    tails_ref[h, _qrows(i), :] = jnp.where(lane == j, carries[h], tails_ref[h, _qrows(i), :])
                    z = _dot_nt(qi, km[h, _rows(j), :])
                    lb, lr, _ = _log_sigmoid_parts(z)
                    lr = jnp.where(strict, lr, 0.0)
                    tail = _split_dot(lr, later, 2) + carries[h]
                    a = jnp.where(strict, jnp.exp(lb + tail), 0.0)
                    acc = acc + _dot(a, vm[h, _rows(j), :])
                    out.append(carries[h] + jnp.sum(lr, axis=-1, keepdims=True))
                return tuple(out), acc

            init = ((jnp.zeros((QT, 1), F32),) * 2, jnp.zeros((QT, BLK), F32))
            _, acc = lax.fori_loop(0, _keys_upto(i), kblock, init)
            o_ref[_qrows(i), :] = acc
            return 0

        lax.fori_loop(0, s // QT, qblock, 0)

    return _host_call(
        body, carry, name=name, grid=(nb, 2), in_specs=[_col_spec(s, 0), _col_spec(s, 2), _col_spec(s, 4)],
        out_specs=[_col_spec(s, 0), _pair_spec(s, BLK)],
        out_shape=[jax.ShapeDtypeStruct((nb * s, 2 * BLK), F32), jax.ShapeDtypeStruct((nb, N_HEADS, s, BLK), F32)],
        scratch_shapes=[pltpu.VMEM((2, s, BLK), MXU_DTYPE)] * 2, operands=(proj, proj, proj))


def _sb_bwd(proj, dmixed, tails, nb, s, name, carry=None):
    def body(q_ref, k_ref, v_ref, do_ref, tails_ref, dq_ref, dk_ref, dv_ref, km, vm):
        sels = _head_masks()
        _split_heads(k_ref, km, sels)
        _split_heads(v_ref, vm, sels)
        rows = _iota2((QT, ATT), 0)
        cols = _iota2((QT, ATT), 1)
        lane = _iota2((QT, BLK), 1)
        later = _triangle(lambda r, c: r > c)
        earlier = _triangle(lambda r, c: r < c)
        dk_ref[...] = jnp.zeros_like(dk_ref)
        dv_ref[...] = jnp.zeros_like(dv_ref)

        def qblock(i, _):
            qi = (q_ref[_qrows(i), :] * 0.125).astype(MXU_DTYPE)
            doi = do_ref[_qrows(i), :].astype(MXU_DTYPE)
            qm = [jnp.where(sel, qi, 0.0) for sel in sels]
            dom = [jnp.where(sel, doi, 0.0) for sel in sels]
            tails_i = [tails_ref[h, _qrows(i), :] for h in range(2)]

            def kblock(j, state):
                csums, dq = state
                strict = (cols + (j * ATT - i * QT)) < rows
                out = []
                for h in range(2):
                    z = _dot_nt(qi, km[h, _rows(j), :])
                    lb, lr, _ = _log_sigmoid_parts(z)
                    lr = jnp.where(strict, lr, 0.0)
                    after = jnp.sum(jnp.where(lane == j, tails_i[h], 0.0), axis=-1, keepdims=True)
                    a = jnp.where(strict, jnp.exp(lb + _split_dot(lr, later, 2) + after), 0.0)
                    dl = a * _dot_nt(doi, vm[h, _rows(j), :])
                    beta = jnp.exp(lb)
                    before = _split_dot(dl, earlier, 2) + csums[h]
                    dz = jnp.where(strict, dl * (1.0 - beta) - beta * before, 0.0).astype(MXU_DTYPE)
                    dq = dq + _dot(dz, km[h, _rows(j), :])
                    dk_ref[_rows(j), :] += _dot_tn(dz, qm[h])
                    dv_ref[_rows(j), :] += _dot_tn(a, dom[h])
                    out.append(csums[h] + jnp.sum(dl, axis=-1, keepdims=True))
                return tuple(out), dq

            init = ((jnp.zeros((QT, 1), F32),) * 2, jnp.zeros((QT, BLK), F32))
            _, dq = lax.fori_loop(0, _keys_upto(i), kblock, init)
            dq_ref[_qrows(i), :] = dq * 0.125
            return 0

        lax.fori_loop(0, s // QT, qblock, 0)

    out = _col_spec(s, 0)
    return _host_call(
        body, carry, name=name, grid=(nb, 2),
        in_specs=[_col_spec(s, 0), _col_spec(s, 2), _col_spec(s, 4), out, _pair_spec(s, BLK)], out_specs=[out] * 3,
        out_shape=[jax.ShapeDtypeStruct((nb * s, 2 * BLK), F32)] * 3,
        scratch_shapes=[pltpu.VMEM((2, s, BLK), MXU_DTYPE)] * 2, operands=(proj, proj, proj, dmixed, tails))


def _pair_spec(s, width):
    return pl.BlockSpec((None, 2, s, width), lambda b, p: (b, p, 0, 0))


def _fox_fwd(proj, ccol, crow, nb, s, name, carry=None):
    nblk = s // ATT

    def body(q_ref, k_ref, v_ref, cc_ref, cr_ref, o_ref, lse_ref, km, vm):
        sels = _head_masks()
        _split_heads(k_ref, km, sels)
        _split_heads(v_ref, vm, sels)
        rows = _iota2((QT, ATT), 0)
        cols = _iota2((QT, ATT), 1)

        def qblock(i, _):
            qi = (q_ref[_qrows(i), :] * 0.125).astype(MXU_DTYPE)
            ci = [cc_ref[h, _qrows(i), :] for h in range(2)]

            def kblock(j, state):
                ms, ls, acc = state
                causal = (cols + (j * ATT - i * QT)) <= rows
                new_m, new_l, scales, parts = [], [], [], []
                for h in range(2):
                    z = _dot_nt(qi, km[h, _rows(j), :]) + (ci[h] - cr_ref[h, j][0:1, :])
                    z = jnp.where(causal, z, NEG)
                    m_new = jnp.maximum(ms[h], jnp.max(z, axis=-1, keepdims=True))
                    p = jnp.exp(z - m_new)
                    scale = jnp.exp(ms[h] - m_new)
                    new_m.append(m_new)
                    new_l.append(scale * ls[h] + jnp.sum(p, axis=-1, keepdims=True))
                    scales.append(scale)
                    parts.append(_dot(p, vm[h, _rows(j), :]))
                acc = jnp.where(sels[0], scales[0], scales[1]) * acc + parts[0] + parts[1]
                return tuple(new_m), tuple(new_l), acc

            init = ((jnp.full((QT, 1), NEG, F32),) * 2, (jnp.zeros((QT, 1), F32),) * 2, jnp.zeros((QT, BLK), F32))
            ms, ls, acc = lax.fori_loop(0, _keys_upto(i), kblock, init)
            o_ref[_qrows(i), :] = acc / jnp.where(sels[0], ls[0], ls[1])
            for h in range(2):
                lse_ref[h, _qrows(i), :] = jnp.broadcast_to(ms[h] + jnp.log(ls[h]), (QT, ATT))
            return 0

        lax.fori_loop(0, s // QT, qblock, 0)

    crow_spec = pl.BlockSpec((None, 2, nblk, 8, ATT), lambda b, p: (b, p, 0, 0, 0))
    return _host_call(
        body, carry, name=name, grid=(nb, 2),
        in_specs=[_col_spec(s, 12), _col_spec(s, 14), _col_spec(s, 16), _pair_spec(s, ATT), crow_spec],
        out_specs=[_col_spec(s, 0), _pair_spec(s, ATT)],
        out_shape=[jax.ShapeDtypeStruct((nb * s, 2 * BLK), F32), jax.ShapeDtypeStruct((nb, N_HEADS, s, ATT), F32)],
        scratch_shapes=[pltpu.VMEM((2, s, BLK), MXU_DTYPE)] * 2, operands=(proj, proj, proj, ccol, crow))


def _fox_bwd(proj, dmixed, lse, ccol, crow, nb, s, name, carry=None):
    nblk = s // ATT

    def body(q_ref, k_ref, v_ref, do_ref, lse_ref, cc_ref, cr_ref, dq_ref, dk_ref, dv_ref, dc_ref, km, vm, p_scr, dp_scr):
        sels = _head_masks()
        _split_heads(k_ref, km, sels)
        _split_heads(v_ref, vm, sels)
        rows = _iota2((QT, ATT), 0)
        cols = _iota2((QT, ATT), 1)
        dk_ref[...] = jnp.zeros_like(dk_ref)
        dv_ref[...] = jnp.zeros_like(dv_ref)
        dc_ref[...] = jnp.zeros_like(dc_ref)

        def qblock(i, _):
            qi = (q_ref[_qrows(i), :] * 0.125).astype(MXU_DTYPE)
            doi = do_ref[_qrows(i), :].astype(MXU_DTYPE)
            qm = [jnp.where(sel, qi, 0.0) for sel in sels]
            dom = [jnp.where(sel, doi, 0.0) for sel in sels]
            ci = [cc_ref[h, _qrows(i), :] for h in range(2)]
            lsei = [lse_ref[h, _qrows(i), :] for h in range(2)]

            def probs(j, h):
                z = _dot_nt(qi, km[h, _rows(j), :]) + (ci[h] - cr_ref[h, j][0:1, :])
                p = jnp.where((cols + (j * ATT - i * QT)) <= rows, jnp.exp(z - lsei[h]), 0.0)
                return p, _dot_nt(doi, vm[h, _rows(j), :])

            def row_term(j, accs):
                out = []
                for h in range(2):
                    p, dp = probs(j, h)
                    p_scr[h, j] = p
                    dp_scr[h, j] = dp
                    out.append(accs[h] + jnp.sum(p * dp, axis=-1, keepdims=True))
                return tuple(out)

            di = lax.fori_loop(0, _keys_upto(i), row_term, (jnp.zeros((QT, 1), F32),) * 2)

            def kblock(j, dq):
                for h in range(2):
                    p = p_scr[h, j]
                    ds = p * (dp_scr[h, j] - di[h])
                    dc_ref[h, j] += jnp.broadcast_to(jnp.sum(ds, axis=0, keepdims=True), (8, ATT))
                    ds = ds.astype(MXU_DTYPE)
                    dk_ref[_rows(j), :] += _dot_tn(ds, qm[h])
                    dv_ref[_rows(j), :] += _dot_tn(p, dom[h])
                    dq = dq + _dot(ds, km[h, _rows(j), :])
                return dq

            dq = lax.fori_loop(0, _keys_upto(i), kblock, jnp.zeros((QT, BLK), F32))
            dq_ref[_qrows(i), :] = dq * 0.125
            return 0

        lax.fori_loop(0, s // QT, qblock, 0)

    crow_spec = pl.BlockSpec((None, 2, nblk, 8, ATT), lambda b, p: (b, p, 0, 0, 0))
    wide, cols_out = _pair_spec(s, ATT), _col_spec(s, 0)
    return _host_call(
        body, carry, name=name, grid=(nb, 2),
        in_specs=[_col_spec(s, 12), _col_spec(s, 14), _col_spec(s, 16), _col_spec(s, 4), wide, wide, crow_spec],
        out_specs=[cols_out, cols_out, cols_out, crow_spec],
        out_shape=[jax.ShapeDtypeStruct((nb * s, 2 * BLK), F32)] * 3 + [jax.ShapeDtypeStruct((nb, N_HEADS, nblk, 8, ATT), F32)],
        scratch_shapes=[pltpu.VMEM((2, s, BLK), MXU_DTYPE)] * 2 + [pltpu.VMEM((2, nblk, QT, ATT), F32)] * 2,
        operands=(proj, proj, proj, dmixed, lse, ccol, crow))


def _fox_gates_fwd(proj, f_bias, nb, s, name):
    chunk = 256

    def body(f_ref, b_ref, c_ref):
        lower = (_iota2((chunk, chunk), 0) >= _iota2((chunk, chunk), 1)).astype(MXU_DTYPE)
        carry = jnp.zeros((1, BLK), F32)
        for n in range(s // chunk):
            rows = pl.ds(n * chunk, chunk)
            lf, _, _ = _log_sigmoid_parts(f_ref[rows, :] + b_ref[0:1, :])
            c = _split_dot_lhs(lower, lf, 3) + carry
            c_ref[rows, :] = c
            carry = c[chunk - 1:chunk, :]

    return pl.pallas_call(
        body, name=name, grid=(nb,),
        in_specs=[pl.BlockSpec((s, BLK), lambda b: (b, (PROJ_PAD - BLK) // BLK)), pl.BlockSpec((8, BLK), lambda b: (0, 0))],
        out_specs=pl.BlockSpec((s, BLK), lambda b: (b, 0)),
        out_shape=jax.ShapeDtypeStruct((nb * s, BLK), F32), compiler_params=_cparams("parallel"),
    )(proj, f_bias)


def _fox_gates_bwd(dc, proj, f_bias, nb, s, name):
    chunk = 256

    def body(dc_ref, f_ref, b_ref, df_ref, db_ref):
        upper = (_iota2((chunk, chunk), 0) <= _iota2((chunk, chunk), 1)).astype(MXU_DTYPE)
        carry = jnp.zeros((1, BLK), F32)
        total = jnp.zeros((1, BLK), F32)
        for n in reversed(range(s // chunk)):
            rows = pl.ds(n * chunk, chunk)
            dlf = _split_dot_lhs(upper, dc_ref[rows, :], 3) + carry
            carry = dlf[0:1, :]
            pre = f_ref[rows, :] + b_ref[0:1, :]
            e = jnp.exp(-jnp.abs(pre))
            df = dlf * (jnp.where(pre >= 0.0, e, 1.0) / (1.0 + e))
            df_ref[rows, :] = df
            total = total + jnp.sum(df, axis=0, keepdims=True)

        @pl.when(pl.program_id(0) == 0)
        def _():
            db_ref[...] = jnp.zeros_like(db_ref)

        db_ref[0:1, :] += total

    return pl.pallas_call(
        body, name=name, grid=(nb,),
        in_specs=[pl.BlockSpec((s, BLK), lambda b: (b, 0)), pl.BlockSpec((s, BLK), lambda b: (b, (PROJ_PAD - BLK) // BLK)),
                  pl.BlockSpec((8, BLK), lambda b: (0, 0))],
        out_specs=[pl.BlockSpec((s, BLK), lambda b: (b, 0)), pl.BlockSpec((8, BLK), lambda b: (0, 0))],
        out_shape=[jax.ShapeDtypeStruct((nb * s, BLK), F32), jax.ShapeDtypeStruct((8, BLK), F32)],
        compiler_params=_cparams("arbitrary"),
    )(dc, proj, f_bias)


def _delta_kernel(dmixed, o, nb, s, name):
    def body(do_ref, o_ref, d_ref):
        prod = do_ref[...] * o_ref[...]
        for h, sel in enumerate(_head_masks()):
            d_ref[h] = jnp.broadcast_to(jnp.sum(jnp.where(sel, prod, 0.0), axis=-1, keepdims=True), (s, BLK))

    return pl.pallas_call(
        body, name=name, grid=(nb, 2), in_specs=[_col_spec(s, 2), _col_spec(s, 0)], out_specs=_pair_spec(s, BLK),
        out_shape=jax.ShapeDtypeStruct((nb, N_HEADS, s, BLK), F32), compiler_params=_cparams("parallel", "parallel"),
    )(dmixed, o)


def _t5_bucket_np(dist):
    max_exact = REL_BUCKETS // 2
    nf = np.maximum(dist, 1).astype(np.float32)
    large = max_exact + (np.log(nf / max_exact) / math.log(2048 / max_exact) * (REL_BUCKETS - max_exact)).astype(np.int32)
    large = np.minimum(large, REL_BUCKETS - 1)
    return np.where(dist < max_exact, dist, large)


def _bucket_table():
    qi = np.arange(BLK)[:, None]
    kj = np.arange(2 * BLK)[None, :]
    dist = qi + BLK - kj
    tables = []
    for window, dil in DIL_PATTERNS:
        in_band = (dist >= 0) & (dist <= window // dil)
        tables.append(np.where(in_band, _t5_bucket_np(np.maximum(dist, 0) * dil), -1).astype(np.int32))
    return np.stack(tables)


def _dil_scores(qb, kp, kc, b_ref, h, prev_valid):
    zp = _dot_nt(qb, kp) + b_ref[h, :, 0:BLK]
    zp = jnp.where(prev_valid, zp, NEG)
    zc = _dot_nt(qb, kc) + b_ref[h, :, BLK:2 * BLK]
    return zp, zc


def _residue_rows(b, seg, dil):
    if dil == 1:
        return _rows128(b), _rows128(jnp.maximum(b - 1, 0)), b > 0
    r, n = b // seg, b % seg
    cur = pl.ds(r + dil * n * BLK, BLK, stride=dil)
    prev = pl.ds(r + dil * jnp.maximum(n - 1, 0) * BLK, BLK, stride=dil)
    return cur, prev, n > 0


def _dil_attention_fwd(proj, bias, nb, s, name, carry=None):
    nblk = s // BLK

    def body(q_ref, k_ref, v_ref, b_ref, out_ref, lse_ref, o_scr, l_scr):
        sels = _head_masks()
        for p, (_, dil) in enumerate(DIL_PATTERNS):
            seg = s // dil // BLK

            def block(b, _, p=p, seg=seg, dil=dil):
                cur, prev, has_prev = _residue_rows(b, seg, dil)
                qb = (q_ref[cur, :] * 0.125).astype(MXU_DTYPE)
                kp, kc = k_ref[prev, :].astype(MXU_DTYPE), k_ref[cur, :].astype(MXU_DTYPE)
                vp, vc = v_ref[prev, :].astype(MXU_DTYPE), v_ref[cur, :].astype(MXU_DTYPE)
                acc = jnp.zeros((BLK, BLK), F32)
                for h, sel in enumerate(sels):
                    zp, zc = _dil_scores(qb, jnp.where(sel, kp, 0.0), jnp.where(sel, kc, 0.0), b_ref.at[p], h, has_prev)
                    m = jnp.maximum(jnp.max(zp, axis=-1, keepdims=True), jnp.max(zc, axis=-1, keepdims=True))
                    pp = jnp.exp(zp - m)
                    pc = jnp.exp(zc - m)
                    den = jnp.sum(pp, axis=-1, keepdims=True) + jnp.sum(pc, axis=-1, keepdims=True)
                    acc = acc + (_dot(pp, jnp.where(sel, vp, 0.0)) + _dot(pc, jnp.where(sel, vc, 0.0))) / den
                    l_scr[p, h, cur, :] = jnp.broadcast_to(m + jnp.log(den), (BLK, BLK))
                o_scr[p, cur, :] = acc
                return 0

            lax.fori_loop(0, nblk, block, 0, unroll=4)

        weights, dens = [], []
        for h in range(2):
            m = jnp.maximum(jnp.maximum(l_scr[0, h], l_scr[1, h]), l_scr[2, h])
            w = [jnp.exp(l_scr[p, h] - m) for p in range(3)]
            den = w[0] + w[1] + w[2]
            lse_ref[h] = m + jnp.log(den)
            weights.append(w)
            dens.append(den)
        num = sum(jnp.where(sels[0], weights[0][p], weights[1][p]) * o_scr[p] for p in range(3))
        out_ref[...] = num / jnp.where(sels[0], dens[0], dens[1])

    bias_spec = pl.BlockSpec((3, 2, BLK, 2 * BLK), lambda b, p: (0, p, 0, 0))
    return _host_call(
        body, carry, name=name, grid=(nb, 2), in_specs=[_col_spec(s, 6), _col_spec(s, 8), _col_spec(s, 10), bias_spec],
        out_specs=[_col_spec(s, 0), _pair_spec(s, BLK)],
        out_shape=[jax.ShapeDtypeStruct((nb * s, 2 * BLK), F32), jax.ShapeDtypeStruct((nb, N_HEADS, s, BLK), F32)],
        scratch_shapes=[pltpu.VMEM((3, s, BLK), F32), pltpu.VMEM((3, 2, s, BLK), F32)], operands=(proj, proj, proj, bias))


def _dil_attention_bwd(proj, dmixed, lse, delta, bias, nb, s, name, carry=None):
    nblk = s // BLK

    def body(q_ref, k_ref, v_ref, do_ref, lse_ref, dl_ref, b_ref, dq_ref, dk_ref, dv_ref, g_ref):
        sels = _head_masks()
        dq_ref[...] = jnp.zeros_like(dq_ref)
        dk_ref[...] = jnp.zeros_like(dk_ref)
        dv_ref[...] = jnp.zeros_like(dv_ref)
        g_ref[...] = jnp.zeros_like(g_ref)
        for p, (_, dil) in enumerate(DIL_PATTERNS):
            seg = s // dil // BLK

            def block(b, _, p=p, seg=seg, dil=dil):
                cur, prev, has_prev = _residue_rows(b, seg, dil)
                qb = (q_ref[cur, :] * 0.125).astype(MXU_DTYPE)
                dob = do_ref[cur, :].astype(MXU_DTYPE)
                kp, kc = k_ref[prev, :].astype(MXU_DTYPE), k_ref[cur, :].astype(MXU_DTYPE)
                vp, vc = v_ref[prev, :].astype(MXU_DTYPE), v_ref[cur, :].astype(MXU_DTYPE)
                dq = jnp.zeros((BLK, BLK), F32)
                dkp, dkc, dvp, dvc = dq, dq, dq, dq
                for h, sel in enumerate(sels):
                    kph, kch = jnp.where(sel, kp, 0.0), jnp.where(sel, kc, 0.0)
                    qh, doh = jnp.where(sel, qb, 0.0), jnp.where(sel, dob, 0.0)
                    lse_h = lse_ref[h, cur, :]
                    dlt = dl_ref[h, cur, :]
                    zp, zc = _dil_scores(qb, kph, kch, b_ref.at[p], h, has_prev)
                    pp = jnp.exp(zp - lse_h)
                    pc = jnp.exp(zc - lse_h)
                    dsp = pp * (_dot_nt(dob, jnp.where(sel, vp, 0.0)) - dlt)
                    dsc = pc * (_dot_nt(dob, jnp.where(sel, vc, 0.0)) - dlt)
                    g_ref[h, p, :, 0:BLK] += dsp
                    g_ref[h, p, :, BLK:2 * BLK] += dsc
                    dsp = dsp.astype(MXU_DTYPE)
                    dsc = dsc.astype(MXU_DTYPE)
                    dq = dq + _dot(dsp, kph) + _dot(dsc, kch)
                    dkp, dkc = dkp + _dot_tn(dsp, qh), dkc + _dot_tn(dsc, qh)
                    dvp, dvc = dvp + _dot_tn(pp, doh), dvc + _dot_tn(pc, doh)
                dq_ref[cur, :] += dq * 0.125
                dk_ref[prev, :] += dkp
                dk_ref[cur, :] += dkc
                dv_ref[prev, :] += dvp
                dv_ref[cur, :] += dvc
                return 0

            lax.fori_loop(0, nblk, block, 0, unroll=4)

    bias_spec = pl.BlockSpec((3, 2, BLK, 2 * BLK), lambda b, p: (0, p, 0, 0))
    cols, stats = _col_spec(s, 0), _pair_spec(s, BLK)
    return _host_call(
        body, carry, name=name, grid=(nb, 2),
        in_specs=[_col_spec(s, 6), _col_spec(s, 8), _col_spec(s, 10), _col_spec(s, 2), stats, stats, bias_spec],
        out_specs=[cols, cols, cols, pl.BlockSpec((None, 2, 3, BLK, 2 * BLK), lambda b, p: (b, p, 0, 0, 0))],
        out_shape=[jax.ShapeDtypeStruct((nb * s, 2 * BLK), F32)] * 3 + [jax.ShapeDtypeStruct((nb, N_HEADS, 3, BLK, 2 * BLK), F32)],
        operands=(proj, proj, proj, dmixed, lse, delta, bias))


def _bucket_reduce(gbias, table, name):
    nb = gbias.shape[0]

    def body(g_ref, t_ref, o_ref):
        row = _iota2((8, BLK), 0)
        lane = _iota2((8, BLK), 1)
        gsum = [[sum(g_ref[b, h, p] for b in range(nb)) for p in range(3)] for h in range(N_HEADS)]

        def bucket(k, acc):
            for h in range(N_HEADS):
                tot = sum(jnp.sum(jnp.where(t_ref[p] == k, gsum[h][p], 0.0)) for p in range(3))
                acc = acc + jnp.where((row == h) & (lane == k), tot, 0.0)
            return acc

        o_ref[...] = lax.fori_loop(0, REL_BUCKETS, bucket, jnp.zeros((8, BLK), F32))

    vm = pl.BlockSpec(memory_space=pltpu.VMEM)
    return pl.pallas_call(
        body, name=name, in_specs=[vm, vm], out_specs=vm, out_shape=jax.ShapeDtypeStruct((8, BLK), F32),
        compiler_params=pltpu.CompilerParams(vmem_limit_bytes=VMEM_LIMIT),
    )(gbias, table)


def _place():
    x, y, c = lax.axis_index("x"), lax.axis_index("y"), lax.axis_index("c")
    others = [(1 - x, y), (x, 1 - y), (1 - x, 1 - y)]
    return x, y, c, others


def _remote(src, dst, send_sem, recv_sem, to):
    return pltpu.make_async_remote_copy(src_ref=src, dst_ref=dst, send_sem=send_sem, recv_sem=recv_sem,
                                        device_id=to, device_id_type=MESH)


_HBM = pl.BlockSpec(memory_space=pl.ANY)


class _Exchange:
    def __init__(self, operands, out_shape, n_copies, copies, aliases=None):
        self.operands, self.out_shape, self.n_copies, self.copies = list(operands), list(out_shape), n_copies, copies
        self.aliases = dict(aliases or {})

    def sem_shapes(self):
        return [pltpu.SemaphoreType.DMA((self.n_copies,)), pltpu.SemaphoreType.DMA((self.n_copies,))]


def _start_all(sends):
    for cp in sends:
        cp.start()


def _wait_all(sends, arrivals):
    for cp in arrivals:
        cp.wait_recv()
    for cp in sends:
        cp.wait_send()


def _run_exchange(ex, name):
    ni = len(ex.operands)

    def body(*refs):
        sends, arrivals = ex.copies(refs[:ni], refs[ni:-2], refs[-2], refs[-1])
        _start_all(sends)
        _wait_all(sends, arrivals)

    return list(pl.pallas_call(
        body, name=name, in_specs=[_HBM] * ni, out_specs=[_HBM] * len(ex.out_shape), out_shape=ex.out_shape,
        scratch_shapes=ex.sem_shapes(), input_output_aliases=ex.aliases)(*ex.operands))


def _host_call(body, carry, *, name, grid, in_specs, out_specs, out_shape, operands, scratch_shapes=()):
    in_specs, out_specs, out_shape, scratch_shapes = list(in_specs), list(out_specs), list(out_shape), list(scratch_shapes)
    if carry is None:
        res = pl.pallas_call(body, name=name, grid=grid, in_specs=in_specs, out_specs=out_specs, out_shape=out_shape,
                             scratch_shapes=scratch_shapes, compiler_params=_cparams(*["parallel"] * len(grid)))(*operands)
        return list(res), []
    n_in, n_out, n_scr, c_in, c_out = len(in_specs), len(out_specs), len(scratch_shapes), len(carry.operands), len(carry.out_shape)
    steps = math.prod(grid)

    def wrapped(*refs):
        ins, refs = refs[:n_in], refs[n_in:]
        c_ins, refs = refs[:c_in], refs[c_in:]
        outs, refs = refs[:n_out], refs[n_out:]
        c_outs, refs = refs[:c_out], refs[c_out:]
        scr, (send_sems, recv_sems) = refs[:n_scr], refs[n_scr:]
        step = 0
        for d, size in enumerate(grid):
            step = step * size + pl.program_id(d)

        @pl.when(step == 0)
        def _():
            _start_all(carry.copies(c_ins, c_outs, send_sems, recv_sems)[0])

        body(*ins, *outs, *scr)

        @pl.when(step == steps - 1)
        def _():
            _wait_all(*carry.copies(c_ins, c_outs, send_sems, recv_sems))

    res = pl.pallas_call(
        wrapped, name=name, grid=grid, in_specs=in_specs + [_HBM] * c_in, out_specs=out_specs + [_HBM] * c_out,
        out_shape=out_shape + carry.out_shape, scratch_shapes=scratch_shapes + carry.sem_shapes(),
        input_output_aliases={n_in + i: n_out + j for i, j in carry.aliases.items()},
        compiler_params=_cparams(*["arbitrary"] * len(grid)))(*operands, *carry.operands)
    return list(res[:n_out]), list(res[n_out:])


def _half(which, rows):
    h = rows // 2
    return pl.ds(pl.multiple_of(which * h, 16), h)


def _like(arrays, shape_of=lambda t: t.shape):
    return [jax.ShapeDtypeStruct(shape_of(t), t.dtype) for t in arrays]


def _gather_ici(shards, layer):
    n = len(shards)

    def copies(ins, outs, send_sems, recv_sems, base=0):
        x, y, c, others = _place()
        me = 2 * x + y
        sends, arrivals = [], []
        for a in range(n):
            rows = _half(c, shards[a].shape[1])
            for k, (ox, oy) in enumerate(others):
                sems = (send_sems.at[base + 3 * a + k], recv_sems.at[base + 3 * a + k],(ox, oy, c))
                sends.append(_remote(ins[a].at[layer, rows], outs[a].at[me, rows], *sems))
                landed = outs[a].at[2 * ox + oy, rows]
                arrivals.append(_remote(landed, landed, *sems))
        return sends, arrivals

    return _Exchange(shards, _like(shards, lambda t: (N_CHIPS,) + t.shape[1:]), 3 * n, copies)


def _gather_d2d(gathered):
    n = len(gathered)

    def copies(ins, outs, send_sems, recv_sems, base=0):
        x, y, c, others = _place()
        sends, arrivals = [], []
        for a in range(n):
            r = gathered[a].shape[1]
            for k, (ox, oy) in enumerate(others):
                sems = (send_sems.at[base + 3 * a + k], recv_sems.at[base + 3 * a + k],(x, y, 1 - c))
                mine, theirs = outs[a].at[2 * ox + oy, _half(c, r)], outs[a].at[2 * ox + oy, _half(1 - c, r)]
                sends.append(_remote(mine, mine, *sems))
                arrivals.append(_remote(theirs, theirs, *sems))
        return sends, arrivals

    return _Exchange(gathered, _like(gathered), 3 * n, copies, aliases={a: a for a in range(n)})


def _swap_halves(g):
    n = len(g)

    def copies(ins, outs, send_sems, recv_sems, base=0):
        x, y, c, _ = _place()
        sends, arrivals = [], []
        for a in range(n):
            sems = (send_sems.at[base + a], recv_sems.at[base + a], (x, y, 1 - c))
            sends.append(_remote(ins[a].at[:, _half(1 - c, g[a].shape[1])], outs[a], *sems))
            arrivals.append(_remote(outs[a], outs[a], *sems))
        return sends, arrivals

    return _Exchange(g, _like(g, lambda t: (t.shape[0], t.shape[1] // 2, t.shape[2])), n, copies)


def _scatter_shards(ps):
    n = len(ps)

    def copies(ins, outs, send_sems, recv_sems, base=0):
        x, y, c, others = _place()
        me = 2 * x + y
        sends, arrivals = [], []
        for a in range(n):
            for k, (ox, oy) in enumerate(others):
                sems = (send_sems.at[base + 3 * a + k], recv_sems.at[base + 3 * a + k],(ox, oy, c))
                sends.append(_remote(ins[a].at[2 * ox + oy], outs[a].at[me], *sems))
                slot = outs[a].at[2 * ox + oy]
                arrivals.append(_remote(slot, slot, *sems))
        return sends, arrivals

    return _Exchange(ps, _like(ps), 3 * n, copies)


def _share_halves(mine):
    n = len(mine)

    def copies(ins, outs, send_sems, recv_sems, base=0):
        x, y, c, _ = _place()
        sends, arrivals = [], []
        for a in range(n):
            sems = (send_sems.at[base + a], recv_sems.at[base + a], (x, y, 1 - c))
            sends.append(_remote(ins[a], outs[a], *sems))
            arrivals.append(_remote(outs[a], outs[a], *sems))
        return sends, arrivals

    return _Exchange(mine, _like(mine), n, copies)


def _row_tile(r):
    for cand in (256, 352, 128):
        if r % cand == 0:
            return cand
    return r


def _pair_sum(g, other, core, name):
    ns, h, w = other.shape
    tr = _row_tile(h)
    per_half = h // tr

    def body(core_ref, g_ref, o_ref, out_ref):
        out_ref[...] = (g_ref[...] + o_ref[...]).astype(out_ref.dtype)

    blk = pl.BlockSpec((None, tr, w), lambda k, i, core_ref: (k, i, 0))
    grid_spec = pltpu.PrefetchScalarGridSpec(
        num_scalar_prefetch=1, grid=(ns, per_half),
        in_specs=[pl.BlockSpec((None, tr, w), lambda k, i, core_ref: (k, core_ref[0] * per_half + i, 0)), blk], out_specs=blk)
    return pl.pallas_call(
        body, name=name, grid_spec=grid_spec, out_shape=jax.ShapeDtypeStruct((ns, h, w), MXU_DTYPE),
        compiler_params=_cparams("parallel", "parallel"),
    )(core.reshape(1).astype(jnp.int32), g, other)


def _chip_sum(q, p, chip, name):
    ns, r, w = q.shape
    tr = _row_tile(r)

    def body(chip_ref, q_ref, own_ref, out_ref):
        me = chip_ref[0]
        own = own_ref[...].astype(F32)
        terms = [jnp.where(me == k, own, q_ref[k].astype(F32)) for k in range(ns)]
        out_ref[...] = ((terms[0] + terms[1]) + terms[2]) + terms[3]

    grid_spec = pltpu.PrefetchScalarGridSpec(
        num_scalar_prefetch=1, grid=(r // tr,),
        in_specs=[pl.BlockSpec((ns, tr, w), lambda i, chip_ref: (0, i, 0)),
                  pl.BlockSpec((None, tr, w), lambda i, chip_ref: (chip_ref[0], i, 0))],
        out_specs=pl.BlockSpec((tr, w), lambda i, chip_ref: (i, 0)))
    return pl.pallas_call(
        body, name=name, grid_spec=grid_spec, out_shape=jax.ShapeDtypeStruct((r, w), F32),
        compiler_params=_cparams("parallel"),
    )(chip.reshape(1).astype(jnp.int32), q, p)


def _merge(exchanges):
    if len(exchanges) <= 1:
        return exchanges[0] if exchanges else None
    operands, out_shape, aliases, spans, n = [], [], {}, [], 0
    for ex in exchanges:
        spans.append((len(operands), len(out_shape), n))
        aliases.update({len(operands) + i: len(out_shape) + j for i, j in ex.aliases.items()})
        operands += ex.operands
        out_shape += ex.out_shape
        n += ex.n_copies

    def copies(ins, outs, send_sems, recv_sems, base=0):
        sends, arrivals = [], []
        for ex, (i0, o0, s0) in zip(exchanges, spans):
            s, a = ex.copies(ins[i0:i0 + len(ex.operands)], outs[o0:o0 + len(ex.out_shape)], send_sems, recv_sems, base + s0)
            sends += s
            arrivals += a
        return sends, arrivals

    return _Exchange(operands, out_shape, n, copies, aliases)


def _take(hooks, host):
    stages = (hooks or {}).pop(host, [])
    exchanges = [make() for make, _ in stages]

    def finish(results):
        for (_, done), ex in zip(stages, exchanges):
            done(results[:len(ex.out_shape)])
            results = results[len(ex.out_shape):]

    return _merge(exchanges), finish


def _hook(hooks, host, make, done):
    hooks.setdefault(host, []).append((make, done))


class _WeightPrefetch:
    def __init__(self, names, shards, layer, chip):
        self.names, self.shards, self.layer, self.chip, self.result = names, [shards[n] for n in names], layer, chip, None

    def first(self):
        return _gather_ici(self.shards, self.layer)

    def got_first(self, arrived):
        self.arrived = arrived

    def second(self):
        return _gather_d2d(self.arrived)

    def got_second(self, gathered):
        self.result = {name: lax.dynamic_update_index_in_dim(got, own[self.layer], self.chip, 0)
                       for name, got, own in zip(self.names, gathered, self.shards)}

    def ride(self, hooks, first_host, second_host):
        _hook(hooks, first_host, self.first, self.got_first)
        _hook(hooks, second_host, self.second, self.got_second)

    def run(self, tag):
        self.got_first(_run_exchange(self.first(), f"gather_ici_{tag}"))
        self.got_second(_run_exchange(self.second(), f"gather_d2d_{tag}"))


class _GradReduce:
    def __init__(self, g, chip, core, tag):
        self.names, self.g, self.chip, self.core, self.tag, self.result = list(g), list(g.values()), chip, core, tag, None

    def swap(self):
        return _swap_halves(self.g)

    def got_swap(self, theirs):
        self.pair = [_pair_sum(g, t, self.core, f"pair_sum_{n}_{self.tag}") for n, g, t in zip(self.names, self.g, theirs)]

    def scatter(self):
        return _scatter_shards(self.pair)

    def got_scatter(self, q):
        self.mine = [_chip_sum(qa, pa, self.chip, f"chip_sum_{n}_{self.tag}") for n, qa, pa in zip(self.names, q, self.pair)]

    def share(self):
        return _share_halves(self.mine)

    def got_share(self, theirs):
        self.result = {n: jnp.where(self.core == 0, jnp.concatenate([a, b]), jnp.concatenate([b, a]))
                       for n, a, b in zip(self.names, self.mine, theirs)}

    def ride(self, hooks, swap_host, scatter_host, share_host):
        _hook(hooks, swap_host, self.swap, self.got_swap)
        _hook(hooks, scatter_host, self.scatter, self.got_scatter)
        _hook(hooks, share_host, self.share, self.got_share)

    def run(self):
        self.got_swap(_run_exchange(self.swap(), f"swap_halves_{self.tag}"))
        self.got_scatter(_run_exchange(self.scatter(), f"scatter_shards_{self.tag}"))
        self.got_share(_run_exchange(self.share(), f"share_halves_{self.tag}"))


class _LayerWeights:
    def __init__(self, gathered):
        self.gathered, self.made = gathered, {}

    def __getitem__(self, key):
        if key not in self.made:
            cols = lambda t: jnp.swapaxes(t, 0, 1).reshape(t.shape[1], -1)
            rows = lambda t: t.reshape(-1, t.shape[2])
            if key == "w_in":
                made = jnp.pad(cols(self.gathered("w_in")), ((0, 0), (0, PROJ_PAD - PROJ)))
            elif key == "w_gu":
                made = jnp.concatenate([cols(self.gathered("w_gate")), cols(self.gathered("w_up"))], axis=-1)
            else:
                made = rows(self.gathered(key))
            self.made[key] = made
        return self.made[key]


def _gather_small(pk, name):
    rows, w = pk.shape

    def body(pk_ref, all_ref, sum_ref, send_sems, recv_sems):
        x, y, c, _ = _place()
        me = 4 * x + 2 * y + c
        all_ref[me] = pk_ref[...]
        flips = [(fx, fy, fc) for fx in (0, 1) for fy in (0, 1) for fc in (0, 1)][1:]
        peers = [(x ^ fx, y ^ fy, c ^ fc) for fx, fy, fc in flips]
        sends = [_remote(pk_ref, all_ref.at[me], send_sems.at[k], recv_sems.at[k], peer) for k, peer in enumerate(peers)]
        for cp in sends:
            cp.start()
        for k, (px, py, pc) in enumerate(peers):
            slot = all_ref.at[4 * px + 2 * py + pc]
            _remote(slot, slot, send_sems.at[k], recv_sems.at[k], (px, py, pc)).wait_recv()
        for cp in sends:
            cp.wait_send()
        total = all_ref[0]
        for d in range(1, N_DEV):
            total = total + all_ref[d]
        sum_ref[...] = total

    vm = pl.BlockSpec(memory_space=pltpu.VMEM)
    return pl.pallas_call(
        body, name=name, in_specs=[vm], out_specs=[vm, vm],
        out_shape=[jax.ShapeDtypeStruct((N_DEV, rows, w), F32), jax.ShapeDtypeStruct((rows, w), F32)],
        scratch_shapes=[pltpu.SemaphoreType.DMA((7,)), pltpu.SemaphoreType.DMA((7,))],
    )(pk)


def _row_layout(c, nb, s):
    ch = jnp.swapaxes(c[:, :N_HEADS].reshape(nb, s, N_HEADS), 1, 2)
    ccol = jnp.broadcast_to(ch[..., None], (nb, N_HEADS, s, ATT))
    crow = jnp.broadcast_to(ch.reshape(nb, N_HEADS, s // ATT, 1, ATT), (nb, N_HEADS, s // ATT, 8, ATT))
    return ccol, crow


def _dil_bias(rel_bias, name):
    def body(rel_ref, t_ref, o_ref):
        for p in range(len(DIL_PATTERNS)):
            table = t_ref[p]

            def bucket(k, accs, table=table):
                return tuple(jnp.where(table == k, rel_ref[k, h], acc) for h, acc in enumerate(accs))

            accs = lax.fori_loop(0, REL_BUCKETS, bucket, tuple(jnp.full((BLK, 2 * BLK), NEG, F32) for _ in range(N_HEADS)))
            for h in range(N_HEADS):
                o_ref[p, h] = accs[h]

    vm = pl.BlockSpec(memory_space=pltpu.VMEM)
    return pl.pallas_call(
        body, name=name, in_specs=[pl.BlockSpec(memory_space=pltpu.SMEM), vm], out_specs=vm,
        out_shape=jax.ShapeDtypeStruct((len(DIL_PATTERNS), N_HEADS, BLK, 2 * BLK), F32),
        compiler_params=pltpu.CompilerParams(vmem_limit_bytes=VMEM_LIMIT),
    )(rel_bias, jnp.asarray(_bucket_table()))


def _layer_forward(x, x_b, wts, small, bias, nb, s, tag, hooks=None):
    proj = _matmul(x_b, wts["w_in"], "proj", tag)

    carry, finish = _take(hooks, "sb_fwd")
    (o_sb, tails_sb), carried = _sb_fwd(proj, nb, s, f"sb_fwd_{tag}", carry)
    finish(carried)

    carry, finish = _take(hooks, "dil_fwd")
    (o_dl, lse_dl), carried = _dil_attention_fwd(proj, bias, nb, s, f"dil_fwd_{tag}", carry)
    finish(carried)

    fb = jnp.zeros((8, BLK), F32).at[0, :N_HEADS].set(small["f_bias"])
    csum = _fox_gates_fwd(proj, fb, nb, s, f"fox_gates_{tag}")
    ccol, crow = _row_layout(csum, nb, s)
    carry, finish = _take(hooks, "fox_fwd")
    (o_fx, lse_fx), carried = _fox_fwd(proj, ccol, crow, nb, s, f"fox_fwd_{tag}", carry)
    finish(carried)

    cw = jnp.zeros((8, CONV_W), F32).at[:3].set(small["conv_w"])
    o_cv = _conv_fwd(proj, cw, nb, s, f"conv_fwd_{tag}")

    mixed = jnp.concatenate([o_sb, o_dl, o_fx, o_cv], axis=-1).astype(MXU_DTYPE)
    pre1, x1, x1_b = _matmul_post_norm(mixed, wts["w_out"], x, small["ln1_g"], small["ln1_b"], f"out_proj_ln1_{tag}")
    carry, finish = _take(hooks, "ffn_in")
    (gate, up, hid), carried = _ffn_in(x1_b, wts["w_gu"], f"ffn_in_{tag}", carry)
    finish(carried)
    pre2, x2, x2_b = _matmul_post_norm(hid, wts["w_down"], x1, small["ln2_g"], small["ln2_b"], f"ffn_out_ln2_{tag}")
    saved = dict(x_b=x_b, proj=proj, tails_sb=tails_sb, bias=bias, o_dl=o_dl, lse_dl=lse_dl, fb=fb, ccol=ccol, crow=crow, o_fx=o_fx,
                 lse_fx=lse_fx, cw=cw, mixed=mixed, pre1=pre1, x1_b=x1_b, gate=gate, up=up, hid=hid, pre2=pre2)
    return (x2, x2_b), saved


def _layer_backward(dx2, sv, wts, small, nb, s, tag, hooks=None, ffn_grads_ready=None):
    t = nb * s
    dpre2, dpre2_b, dgb2 = _ln_bwd(dx2, sv["pre2"], small["ln2_g"], f"ln2_bwd_{tag}")
    carry, finish = _take(hooks, "ffn_out_dx")
    (dgate, dup), carried = _ffn_out_dx(dpre2_b, wts["w_down"], sv["gate"], sv["up"], f"ffn_out_dx_{tag}", carry)
    finish(carried)
    dw_down = _matmul(sv["hid"], dpre2_b, "ffn_out_dw", tag, trans_a=True)
    dx1 = _ffn_in_dx(dgate, dup, wts["w_gu"], dpre2, f"ffn_in_dx_{tag}")
    x1_b = sv["x1_b"]
    dw_gate = _matmul(x1_b, dgate, "ffn_in_dw", f"{tag}_gate", trans_a=True)
    dw_up = _matmul(x1_b, dup, "ffn_in_dw", f"{tag}_up", trans_a=True)

    dpre1, dpre1_b, dgb1 = _ln_bwd(dx1, sv["pre1"], small["ln1_g"], f"ln1_bwd_{tag}")
    dmixed = _matmul(dpre1_b, wts["w_out"], "out_proj_dx", tag, trans_b=True)
    dw_out = _matmul(sv["mixed"], dpre1_b, "out_proj_dw", tag, trans_a=True)
    if ffn_grads_ready:
        ffn_grads_ready(dict(w_down=dw_down, w_gate=dw_gate, w_up=dw_up, w_out=dw_out))
    proj = sv["proj"]

    carry, finish = _take(hooks, "sb_bwd")
    (dq_sb, dk_sb, dv_sb), carried = _sb_bwd(proj, dmixed, sv["tails_sb"], nb, s, f"sb_bwd_{tag}", carry)
    finish(carried)

    delta_dl = _delta_kernel(dmixed, sv["o_dl"], nb, s, f"dil_delta_{tag}")
    carry, finish = _take(hooks, "dil_bwd")
    (dq_dl, dk_dl, dv_dl, gbias), carried = _dil_attention_bwd(proj, dmixed, sv["lse_dl"], delta_dl, sv["bias"], nb, s,
                                                               f"dil_bwd_{tag}", carry)
    finish(carried)

    carry, finish = _take(hooks, "fox_bwd")
    (dq_fx, dk_fx, dv_fx, dcol), carried = _fox_bwd(proj, dmixed, sv["lse_fx"], sv["ccol"], sv["crow"], nb, s,
                                                    f"fox_bwd_{tag}", carry)
    finish(carried)
    dcs = -jnp.swapaxes(dcol[:, :, :, 0, :].reshape(nb, N_HEADS, s), 1, 2).reshape(t, N_HEADS)
    dcs = jnp.pad(dcs, ((0, 0), (0, BLK - N_HEADS)))
    dfx, dfb = _fox_gates_bwd(dcs, proj, sv["fb"], nb, s, f"fox_gates_bwd_{tag}")

    dgates, dcw = _conv_bwd(dmixed, proj, sv["cw"], nb, s, f"conv_bwd_{tag}")

    dproj = jnp.concatenate([dq_sb, dk_sb, dv_sb, dq_dl, dk_dl, dv_dl, dq_fx, dk_fx, dv_fx, dgates, dfx],
                            axis=-1).astype(MXU_DTYPE)
    dx = _matmul(dproj, wts["w_in"], "proj_dx", tag, add=dpre1, add_scale=ALPHA, trans_b=True)
    dw_in = _matmul(sv["x_b"], dproj, "proj_dw", tag, trans_a=True)

    grads = dict(w_in=dw_in[:, :PROJ], w_out=dw_out, w_gate=dw_gate, w_up=dw_up, w_down=dw_down,
                 ln1_g=dgb1[0], ln1_b=dgb1[1], ln2_g=dgb2[0], ln2_b=dgb2[1], conv_w=dcw[:3], f_bias=dfb[0, :N_HEADS],
                 bias_tile=gbias)
    return dx, grads


class _NoExchanges:
    def forward_hooks(self, layer):
        return None

    def backward_hooks(self, layer):
        return None

    def ffn_grads_ready(self, layer):
        return None

    def layer_done(self, layer, grads):
        pass


def _local_step(x, target, weights_of, small_all, schedule=None):
    schedule = schedule or _NoExchanges()
    nb, s, d = x.shape
    h = x.reshape(nb * s, d)
    h_b = h.astype(MXU_DTYPE)
    bias = _dil_bias(small_all[0]["rel_bias"], "dil_bias")
    saved = []
    for layer in range(DEPTH):
        wts = weights_of(layer)
        (h, h_b), sv = _layer_forward(h, h_b, wts, small_all[layer], bias, nb, s, f"l{layer}", schedule.forward_hooks(layer))
        saved.append((sv, wts))
    dy, lossp = _loss_kernel(h, target.reshape(nb * s, d), "loss")
    grads = [None] * DEPTH
    for layer in reversed(range(DEPTH)):
        sv, wts = saved[layer]
        dy, grads[layer] = _layer_backward(dy, sv, wts, small_all[layer], nb, s, f"l{layer}",
                                           schedule.backward_hooks(layer), schedule.ffn_grads_ready(layer))
        schedule.layer_done(layer, grads[layer])
    drel = _bucket_reduce(sum(g["bias_tile"] for g in grads), jnp.asarray(_bucket_table()), "rel_bias_grad")
    for layer in range(DEPTH):
        grads[layer]["rel_bias"] = drel[:N_HEADS, :REL_BUCKETS].T if layer == 0 else jnp.zeros((REL_BUCKETS, N_HEADS), F32)
    return lossp, dy.reshape(nb, s, d), grads


_BIG = ("w_in", "w_out", "w_gate", "w_up", "w_down")
_COL_SHARDED = ("w_in", "w_gate", "w_up")


class _Schedule:
    def __init__(self, shards, chip, core):
        self.chip, self.core, self.reduces = chip, core, [[] for _ in range(DEPTH)]
        first = _WeightPrefetch(["w_in"], shards, 0, chip)
        first.run("l0_w_in")
        rest = _WeightPrefetch(["w_out", "w_gate", "w_up", "w_down"], shards, 0, chip)
        ahead_a = _WeightPrefetch(["w_in", "w_out", "w_down"], shards, 1, chip)
        ahead_b = _WeightPrefetch(["w_gate", "w_up"], shards, 1, chip)
        self.fetches = [[first, rest], [ahead_a, ahead_b]]
        self.forward, self.backward = [{} for _ in range(DEPTH)], [{} for _ in range(DEPTH)]
        rest.ride(self.forward[0], "sb_fwd", "fox_fwd")
        ahead_a.ride(self.forward[0], "dil_fwd", "ffn_in")
        ahead_b.ride(self.forward[0], "fox_fwd", "ffn_in")

    def weights(self, layer):
        def gathered(name):
            return next(f.result[name] for f in self.fetches[layer] if name in f.names)
        return _LayerWeights(gathered)

    def forward_hooks(self, layer):
        return self.forward[layer]

    def backward_hooks(self, layer):
        return self.backward[layer]

    def _reduce(self, layer, grads, tag):
        red = _GradReduce({name: _by_chip(name, g) for name, g in grads.items()}, self.chip, self.core, tag)
        self.reduces[layer].append(red)
        return red

    def ffn_grads_ready(self, layer):
        if layer != 0:
            return None

        def ready(early):
            self._reduce(0, early, "l0_early").ride(self.backward[0], "sb_bwd", "dil_bwd", "fox_bwd")

        return ready

    def layer_done(self, layer, grads):
        if layer == 1:
            self._reduce(1, {name: grads[name] for name in _BIG}, "l1").ride(self.backward[0], "ffn_out_dx", "sb_bwd", "fox_bwd")
        else:
            self._reduce(0, dict(w_in=grads["w_in"]), "l0_w_in").run()

    def reduced(self, layer, name):
        return next(r.result[name] for r in self.reduces[layer] if name in r.names)


def _by_chip(name, g):
    if name in _COL_SHARDED:
        return jnp.swapaxes(g.reshape(g.shape[0], N_CHIPS, -1), 0, 1)
    return g.reshape(N_CHIPS, -1, g.shape[1])


_SMALL_LAYOUT = (("ln1_g", 0), ("ln1_b", 2), ("ln2_g", 4), ("ln2_b", 6), ("conv_w", 8))
_ROW_MISC = 10
_ROW_LOSS = 11


def _pack_small(per_layer, rel_bias, loss=None):
    pk = jnp.zeros((SMALL_ROWS, D_MODEL), F32)
    for name, row in _SMALL_LAYOUT:
        for l in range(DEPTH):
            v = per_layer[l][name].reshape(-1)
            pk = pk.at[row + l, :v.shape[0]].set(v)
    fb = jnp.concatenate([per_layer[l]["f_bias"] for l in range(DEPTH)])
    pk = pk.at[_ROW_MISC, :2 * N_HEADS].set(fb)
    pk = pk.at[_ROW_MISC, BLK:BLK + REL_BUCKETS * N_HEADS].set(rel_bias.reshape(-1))
    if loss is not None:
        pk = pk.at[_ROW_LOSS, 0].set(loss)
    return pk


def _unpack_small(pk, conv_cols):
    out = {}
    for name, row in _SMALL_LAYOUT:
        n = 3 * conv_cols if name == "conv_w" else D_MODEL
        v = pk[row:row + DEPTH, :n]
        out[name] = v.reshape(DEPTH, 3, conv_cols) if name == "conv_w" else v
    out["f_bias"] = pk[_ROW_MISC, :2 * N_HEADS].reshape(DEPTH, N_HEADS)
    out["rel_bias"] = pk[_ROW_MISC, BLK:BLK + REL_BUCKETS * N_HEADS].reshape(REL_BUCKETS, N_HEADS)
    return out


_WEIGHTS = ("w_in", "f_bias", "conv_w", "w_out", "rel_bias", "ln1_g", "ln1_b", "w_gate", "w_up", "w_down", "ln2_g", "ln2_b")


def kernel(x, w_in, f_bias, conv_w, w_out, rel_bias, ln1_g, ln1_b, w_gate, w_up, w_down, ln2_g, ln2_b, loss_target, m_w_in, m_f_bias, m_conv_w, m_w_out, m_rel_bias, m_ln1_g, m_ln1_b, m_w_gate, m_w_up, m_w_down, m_ln2_g, m_ln2_b, v_w_in, v_f_bias, v_conv_w, v_w_out, v_rel_bias, v_ln1_g, v_ln1_b, v_w_gate, v_w_up, v_w_down, v_ln2_g, v_ln2_b):
    w = dict(w_in=w_in, f_bias=f_bias, conv_w=conv_w, w_out=w_out, rel_bias=rel_bias, ln1_g=ln1_g, ln1_b=ln1_b,
             w_gate=w_gate, w_up=w_up, w_down=w_down, ln2_g=ln2_g, ln2_b=ln2_b)
    m = dict(w_in=m_w_in, f_bias=m_f_bias, conv_w=m_conv_w, w_out=m_w_out, rel_bias=m_rel_bias, ln1_g=m_ln1_g,
             ln1_b=m_ln1_b, w_gate=m_w_gate, w_up=m_w_up, w_down=m_w_down, ln2_g=m_ln2_g, ln2_b=m_ln2_b)
    v = dict(w_in=v_w_in, f_bias=v_f_bias, conv_w=v_conv_w, w_out=v_w_out, rel_bias=v_rel_bias, ln1_g=v_ln1_g,
             ln1_b=v_ln1_b, w_gate=v_w_gate, w_up=v_w_up, w_down=v_w_down, ln2_g=v_ln2_g, ln2_b=v_ln2_b)
    chip = 2 * lax.axis_index("x") + lax.axis_index("y")
    core = lax.axis_index("c")
    conv_shard = CONV_W // N_CHIPS

    schedule = _Schedule({name: w[name].astype(MXU_DTYPE) for name in _BIG}, chip, core)
    cw_pk = jnp.zeros((8, D_MODEL), F32).at[0, :DEPTH * 3 * conv_shard].set(conv_w.reshape(-1))
    cw_all, _ = _gather_small(cw_pk, "gather_conv_w")
    cw_chips = cw_all[0::2, 0, :DEPTH * 3 * conv_shard].reshape(N_CHIPS, DEPTH, 3, conv_shard)
    conv_full = jnp.moveaxis(cw_chips, 0, 2).reshape(DEPTH, 3, CONV_W)
    small_all = [dict(f_bias=f_bias[l], conv_w=conv_full[l], rel_bias=rel_bias, ln1_g=ln1_g[l], ln1_b=ln1_b[l],
                      ln2_g=ln2_g[l], ln2_b=ln2_b[l]) for l in range(DEPTH)]

    lossp, grad_x, grads = _local_step(x, loss_target, schedule.weights, small_all, schedule)
    big_g = {name: jnp.stack([schedule.reduced(l, name) for l in range(DEPTH)]) for name in _BIG}

    drel = grads[0]["rel_bias"] + grads[1]["rel_bias"]
    small_pk = _pack_small(grads, drel, lossp[0, 0])
    _, small_sum = _gather_small(small_pk, "gather_small_grads")
    loss = small_sum[_ROW_LOSS, 0]
    small_g = _unpack_small(small_sum, CONV_W)
    small_g["conv_w"] = lax.dynamic_slice_in_dim(small_g["conv_w"], chip * conv_shard, conv_shard, axis=2)

    out_g, out_d, out_m, out_v = dict(small_g), {}, {}, {}
    for name in _BIG:
        out_g[name] = big_g[name]
        out_d[name], out_m[name], out_v[name] = _adamw(w[name], big_g[name], m[name], v[name], f"adamw_{name}")
    as_3d = lambda t: t if t.ndim == 3 else t[None]
    for name in _WEIGHTS:
        if name not in _BIG:
            stepped = _adamw(as_3d(w[name]), as_3d(small_g[name]), as_3d(m[name]), as_3d(v[name]), f"adamw_{name}")
            out_d[name], out_m[name], out_v[name] = (t.reshape(w[name].shape) for t in stepped)

    return (loss, grad_x, *[out_g[n] for n in _WEIGHTS], *[out_d[n] for n in _WEIGHTS],
            *[out_m[n] for n in _WEIGHTS], *[out_v[n] for n in _WEIGHTS])
```

```python
import functools
import math

import numpy as np
import jax
import jax.numpy as jnp
from jax import lax
from jax.experimental import pallas as pl
from jax.experimental.pallas import tpu as pltpu

F32 = jnp.float32
BF16 = jnp.bfloat16
MXU_DTYPE = BF16

D_MODEL = 1024
HEAD_DIM = 64
N_HEADS = 4
BLK = 128
ATT = 256
QT = 512
CONV_W = 256
PROJ = 3076
PROJ_PAD = 3200
D_FF = 2816
DEPTH = 2
ALPHA = (2 * DEPTH) ** 0.25
LN_EPS = 1e-5
NEG = -1e30
DIL_PATTERNS = ((128, 1), (512, 4), (2048, 16))
REL_BUCKETS = 32
N_CHIPS = 4
N_DEV = 8
SMALL_ROWS = 16

ADAM_LR = 0.001
ADAM_B1 = 0.9
ADAM_B2 = 0.999
ADAM_EPS = 1e-08
ADAM_WD = 0.01
ADAM_STEP = 10

VMEM_LIMIT = 56 * 2 ** 20
MESH = pl.DeviceIdType.MESH


def _cparams(*sem):
    return pltpu.CompilerParams(dimension_semantics=tuple(sem), vmem_limit_bytes=VMEM_LIMIT)


def _dot(a, b):
    return jnp.dot(a.astype(MXU_DTYPE), b.astype(MXU_DTYPE), preferred_element_type=F32)


def _dot_nt(a, b):
    return lax.dot_general(a.astype(MXU_DTYPE), b.astype(MXU_DTYPE), (((1,), (1,)), ((), ())),
                           preferred_element_type=F32)


def _dot_tn(a, b):
    return lax.dot_general(a.astype(MXU_DTYPE), b.astype(MXU_DTYPE), (((0,), (0,)), ((), ())),
                           preferred_element_type=F32)


def _split_dot(x, ones, passes):
    acc, rest = None, x
    for p in range(passes):
        piece = rest.astype(MXU_DTYPE)
        part = jnp.dot(piece, ones, preferred_element_type=F32)
        acc = part if acc is None else acc + part
        if p + 1 < passes:
            rest = rest - piece.astype(F32)
    return acc


def _split_dot_lhs(ones, x, passes):
    acc, rest = None, x
    for p in range(passes):
        piece = rest.astype(MXU_DTYPE)
        part = jnp.dot(ones, piece, preferred_element_type=F32)
        acc = part if acc is None else acc + part
        if p + 1 < passes:
            rest = rest - piece.astype(F32)
    return acc


def _iota2(shape, axis):
    return lax.broadcasted_iota(jnp.int32, shape, axis)


_TILES = {"proj": (2048, 640, 1024), "ffn_out_dw": (1408, 1024, 2048),
          "ffn_in_dw": (1024, 1408, 2048), "out_proj_dx": (1024, 1024, 1024),
          "out_proj_dw": (1024, 1024, 2048), "proj_dx": (1024, 512, 3200), "proj_dw": (1024, 640, 2048)}


def _matmul(a, b, kind, tag, *, out_dtype=F32, add=None, add_scale=1.0, trans_a=False, trans_b=False):
    k, m = a.shape if trans_a else a.shape[::-1]
    n = b.shape[0] if trans_b else b.shape[1]
    tm, tn, tk = _TILES[kind]
    tm, tk, name = min(tm, m), min(tk, k), f"{kind}_{tag}"
    assert m % tm == 0 and n % tn == 0 and k % tk == 0, (a.shape, b.shape, tm, tn, tk)
    nk = k // tk

    def body(*refs):
        if add is None:
            a_ref, b_ref, o_ref = refs[:3]
            c_ref, scr = None, refs[3:]
        else:
            a_ref, b_ref, c_ref, o_ref = refs[:4]
            scr = refs[4:]
        dot = _dot_tn if trans_a else _dot_nt if trans_b else _dot
        part = dot(a_ref[...], b_ref[...])

        def finish(acc):
            if c_ref is not None:
                acc = acc + add_scale * c_ref[...]
            o_ref[...] = acc.astype(out_dtype)

        if nk == 1:
            finish(part)
        else:
            acc_ref = scr[0]
            kk = pl.program_id(2)

            @pl.when(kk == 0)
            def _():
                acc_ref[...] = part

            @pl.when(kk > 0)
            def _():
                acc_ref[...] += part

            @pl.when(kk == nk - 1)
            def _():
                finish(acc_ref[...])

    b_spec = pl.BlockSpec((tn, tk), lambda i, j, kk: (j, kk)) if trans_b else pl.BlockSpec((tk, tn), lambda i, j, kk: (kk, j))
    a_spec = pl.BlockSpec((tk, tm), lambda i, j, kk: (kk, i)) if trans_a else pl.BlockSpec((tm, tk), lambda i, j, kk: (i, kk))
    in_specs = [a_spec, b_spec]
    operands = [a, b]
    if add is not None:
        in_specs.append(pl.BlockSpec((tm, tn), lambda i, j, kk: (i, j)))
        operands.append(add)
    return pl.pallas_call(
        body, name=name, grid=(m // tm, n // tn, nk), in_specs=in_specs,
        out_specs=pl.BlockSpec((tm, tn), lambda i, j, kk: (i, j)),
        out_shape=jax.ShapeDtypeStruct((m, n), out_dtype),
        scratch_shapes=[pltpu.VMEM((tm, tn), F32)] if nk > 1 else [],
        compiler_params=_cparams("parallel", "parallel", "arbitrary"),
    )(*operands)


def _matmul_post_norm(a, b, xin, g, beta, name):
    t, k = a.shape
    d = b.shape[1]
    tm = 512

    def body(a_ref, b_ref, x_ref, g_ref, beta_ref, pre_ref, y_ref, yb_ref):
        pre = ALPHA * x_ref[...] + _dot(a_ref[...], b_ref[...])
        xhat, _ = _ln_stats(pre)
        y = xhat * g_ref[...] + beta_ref[...]
        pre_ref[...] = pre
        y_ref[...] = y
        yb_ref[...] = y.astype(yb_ref.dtype)

    row = pl.BlockSpec((tm, d), lambda i: (i, 0))
    vec = pl.BlockSpec((1, d), lambda i: (0, 0))
    return pl.pallas_call(
        body, name=name, grid=(t // tm,),
        in_specs=[pl.BlockSpec((tm, k), lambda i: (i, 0)), pl.BlockSpec((k, d), lambda i: (0, 0)), row, vec, vec],
        out_specs=[row, row, row],
        out_shape=[jax.ShapeDtypeStruct((t, d), F32)] * 2 + [jax.ShapeDtypeStruct((t, d), MXU_DTYPE)],
        compiler_params=_cparams("parallel"),
    )(a, b, xin, g.reshape(1, d), beta.reshape(1, d))


def _ffn_in(x1, w_gu, name, carry=None):
    t, d = x1.shape
    tm, tn = 512, D_FF // 2
    nj = D_FF // tn

    def body(x_ref, wg_ref, wu_ref, gate_ref, up_ref, h_ref):
        xb = x_ref[...].astype(MXU_DTYPE)
        gate = _dot(xb, wg_ref[...])
        up = _dot(xb, wu_ref[...])
        gate_ref[...] = gate
        up_ref[...] = up
        h_ref[...] = (gate * (1.0 / (1.0 + jnp.exp(-gate))) * up).astype(h_ref.dtype)

    out = pl.BlockSpec((tm, tn), lambda i, j: (i, j))
    return _host_call(
        body, carry, name=name, grid=(t // tm, nj),
        in_specs=[pl.BlockSpec((tm, d), lambda i, j: (i, 0)), pl.BlockSpec((d, tn), lambda i, j: (0, j)),
                  pl.BlockSpec((d, tn), lambda i, j: (0, nj + j))],
        out_specs=[out, out, out],
        out_shape=[jax.ShapeDtypeStruct((t, D_FF), F32)] * 2 + [jax.ShapeDtypeStruct((t, D_FF), MXU_DTYPE)],
        operands=(x1, w_gu, w_gu))


def _ffn_out_dx(dy, w_down, gate, up, name, carry=None):
    t, d = dy.shape
    tm, tn = 512, D_FF // 2

    def body(dy_ref, w_ref, gate_ref, up_ref, dg_ref, du_ref):
        dh = _dot_nt(dy_ref[...], w_ref[...])
        gate = gate_ref[...]
        sig = 1.0 / (1.0 + jnp.exp(-gate))
        dg_ref[...] = (dh * up_ref[...] * sig * (1.0 + gate * (1.0 - sig))).astype(dg_ref.dtype)
        du_ref[...] = (dh * gate * sig).astype(du_ref.dtype)

    tile = pl.BlockSpec((tm, tn), lambda i, j: (i, j))
    return _host_call(
        body, carry, name=name, grid=(t // tm, D_FF // tn),
        in_specs=[pl.BlockSpec((tm, d), lambda i, j: (i, 0)), pl.BlockSpec((tn, d), lambda i, j: (j, 0)), tile, tile],
        out_specs=[tile, tile], out_shape=[jax.ShapeDtypeStruct((t, D_FF), MXU_DTYPE)] * 2,
        operands=(dy, w_down, gate, up))


def _ffn_in_dx(dgate, dup, w_gu, add, name):
    t = dgate.shape[0]
    d = w_gu.shape[0]
    tm, tk = 1024, D_FF // 2
    nk = D_FF // tk

    def body(dg_ref, du_ref, wg_ref, wu_ref, add_ref, o_ref, acc_ref):
        kk = pl.program_id(1)
        part = _dot_nt(dg_ref[...], wg_ref[...]) + _dot_nt(du_ref[...], wu_ref[...])

        @pl.when(kk == 0)
        def _():
            acc_ref[...] = part

        @pl.when(kk > 0)
        def _():
            acc_ref[...] += part

        @pl.when(kk == nk - 1)
        def _():
            o_ref[...] = acc_ref[...] + ALPHA * add_ref[...]

    act = pl.BlockSpec((tm, tk), lambda i, kk: (i, kk))
    row = pl.BlockSpec((tm, d), lambda i, kk: (i, 0))
    return pl.pallas_call(
        body, name=name, grid=(t // tm, nk),
        in_specs=[act, act, pl.BlockSpec((d, tk), lambda i, kk: (0, kk)), pl.BlockSpec((d, tk), lambda i, kk: (0, nk + kk)), row],
        out_specs=row, out_shape=jax.ShapeDtypeStruct((t, d), F32), scratch_shapes=[pltpu.VMEM((tm, d), F32)],
        compiler_params=_cparams("parallel", "arbitrary"),
    )(dgate, dup, w_gu, w_gu, add)


def _ln_stats(pre):
    mu = jnp.mean(pre, axis=-1, keepdims=True)
    xc = pre - mu
    var = jnp.mean(xc * xc, axis=-1, keepdims=True)
    rstd = lax.rsqrt(var + LN_EPS)
    return xc * rstd, rstd


def _ln_bwd(dy, pre, g, name):
    t, d = dy.shape
    tile = 256

    def body(dy_ref, pre_ref, g_ref, dpre_ref, dpre_b_ref, dgb_ref):
        dyv = dy_ref[...]
        xhat, rstd = _ln_stats(pre_ref[...])
        dxh = dyv * g_ref[...]
        m1 = jnp.mean(dxh, axis=-1, keepdims=True)
        m2 = jnp.mean(dxh * xhat, axis=-1, keepdims=True)
        dpre = rstd * (dxh - m1 - xhat * m2)
        dpre_ref[...] = dpre
        dpre_b_ref[...] = dpre.astype(dpre_b_ref.dtype)

        @pl.when(pl.program_id(0) == 0)
        def _():
            dgb_ref[...] = jnp.zeros_like(dgb_ref)

        dgb_ref[0:1, :] += jnp.sum(dyv * xhat, axis=0, keepdims=True)
        dgb_ref[1:2, :] += jnp.sum(dyv, axis=0, keepdims=True)

    row = pl.BlockSpec((tile, d), lambda i: (i, 0))
    return pl.pallas_call(
        body, name=name, grid=(t // tile,), in_specs=[row, row, pl.BlockSpec((1, d), lambda i: (0, 0))],
        out_specs=[row, row, pl.BlockSpec((8, d), lambda i: (0, 0))],
        out_shape=[jax.ShapeDtypeStruct((t, d), F32), jax.ShapeDtypeStruct((t, d), MXU_DTYPE), jax.ShapeDtypeStruct((8, d), F32)],
        compiler_params=_cparams("arbitrary"),
    )(dy, pre, g.reshape(1, d))


def _loss_kernel(y, target, name):
    t, d = y.shape
    tile = 512

    def body(y_ref, t_ref, dy_ref, l_ref):
        err = y_ref[...] - t_ref[...]
        dy_ref[...] = err * (1.0 / d)

        @pl.when(pl.program_id(0) == 0)
        def _():
            l_ref[...] = jnp.zeros_like(l_ref)

        l_ref[...] += jnp.sum(err * err) * (0.5 / d)

    row = pl.BlockSpec((tile, d), lambda i: (i, 0))
    return pl.pallas_call(
        body, name=name, grid=(t // tile,), in_specs=[row, row],
        out_specs=[row, pl.BlockSpec((8, 128), lambda i: (0, 0))],
        out_shape=[jax.ShapeDtypeStruct((t, d), F32), jax.ShapeDtypeStruct((8, 128), F32)],
        compiler_params=_cparams("arbitrary"),
    )(y, target)


def _adamw(w, g, m, v, name):
    nl, r, c = w.shape
    tr = r
    for cand in (256, 352, 128, 64, 16, 8):
        if r % cand == 0:
            tr = cand
            break

    def body(w_ref, g_ref, m_ref, v_ref, d_ref, nm_ref, nv_ref):
        gv = g_ref[...]
        nm = ADAM_B1 * m_ref[...] + (1.0 - ADAM_B1) * gv
        nv = ADAM_B2 * v_ref[...] + (1.0 - ADAM_B2) * (gv * gv)
        m_hat = nm / (1.0 - ADAM_B1 ** ADAM_STEP)
        v_hat = nv / (1.0 - ADAM_B2 ** ADAM_STEP)
        d_ref[...] = -ADAM_LR * (m_hat / (jnp.sqrt(v_hat) + ADAM_EPS) + ADAM_WD * w_ref[...])
        nm_ref[...] = nm
        nv_ref[...] = nv

    blk = pl.BlockSpec((1, tr, c), lambda l, i: (l, i, 0))
    return pl.pallas_call(
        body, name=name, grid=(nl, r // tr), in_specs=[blk] * 4, out_specs=[blk] * 3,
        out_shape=[jax.ShapeDtypeStruct(w.shape, F32)] * 3, compiler_params=_cparams("parallel", "parallel"),
    )(w, g, m, v)


def _shift_down(u, k, rows):
    return jnp.where(rows >= k, pltpu.roll(u, k, 0), 0.0)


def _shift_up(u, k, rows, s):
    return jnp.where(rows < s - k, pltpu.roll(u, s - k, 0), 0.0)


def _conv_fwd(proj, conv_w, nb, s, name):
    def body(b_ref, c_ref, h_ref, w_ref, o_ref):
        rows = _iota2((s, CONV_W), 0)
        u = c_ref[...] * h_ref[...]
        y = w_ref[2:3, :] * u + w_ref[1:2, :] * _shift_down(u, 1, rows) + w_ref[0:1, :] * _shift_down(u, 2, rows)
        o_ref[...] = b_ref[...] * y

    col = lambda j: pl.BlockSpec((s, CONV_W), lambda b: (b, j))
    return pl.pallas_call(
        body, name=name, grid=(nb,),
        in_specs=[col(9), col(10), col(11), pl.BlockSpec((8, CONV_W), lambda b: (0, 0))],
        out_specs=pl.BlockSpec((s, CONV_W), lambda b: (b, 0)),
        out_shape=jax.ShapeDtypeStruct((nb * s, CONV_W), F32), compiler_params=_cparams("parallel"),
    )(proj, proj, proj, conv_w)


def _conv_bwd(dmixed, proj, conv_w, nb, s, name):
    def body(do_ref, b_ref, c_ref, h_ref, w_ref, dg_ref, dw_ref):
        rows = _iota2((s, CONV_W), 0)
        cg, hg, bg, dout = c_ref[...], h_ref[...], b_ref[...], do_ref[...]
        u = cg * hg
        u1 = _shift_down(u, 1, rows)
        u2 = _shift_down(u, 2, rows)
        y = w_ref[2:3, :] * u + w_ref[1:2, :] * u1 + w_ref[0:1, :] * u2
        dy = dout * bg
        du = w_ref[2:3, :] * dy + w_ref[1:2, :] * _shift_up(dy, 1, rows, s) + w_ref[0:1, :] * _shift_up(dy, 2, rows, s)
        dg_ref[:, 0:CONV_W] = dout * y
        dg_ref[:, CONV_W:2 * CONV_W] = du * hg
        dg_ref[:, 2 * CONV_W:3 * CONV_W] = du * cg

        @pl.when(pl.program_id(0) == 0)
        def _():
            dw_ref[...] = jnp.zeros_like(dw_ref)

        dw_ref[0:1, :] += jnp.sum(dy * u2, axis=0, keepdims=True)
        dw_ref[1:2, :] += jnp.sum(dy * u1, axis=0, keepdims=True)
        dw_ref[2:3, :] += jnp.sum(dy * u, axis=0, keepdims=True)

    col = lambda j: pl.BlockSpec((s, CONV_W), lambda b: (b, j))
    return pl.pallas_call(
        body, name=name, grid=(nb,),
        in_specs=[col(3), col(9), col(10), col(11), pl.BlockSpec((8, CONV_W), lambda b: (0, 0))],
        out_specs=[pl.BlockSpec((s, 3 * CONV_W), lambda b: (b, 0)), pl.BlockSpec((8, CONV_W), lambda b: (0, 0))],
        out_shape=[jax.ShapeDtypeStruct((nb * s, 3 * CONV_W), F32), jax.ShapeDtypeStruct((8, CONV_W), F32)],
        compiler_params=_cparams("arbitrary"),
    )(dmixed, proj, proj, proj, conv_w)


def _col_spec(s, base):
    return pl.BlockSpec((s, BLK), lambda b, p: (b, base + p))


def _qrows(i):
    return pl.ds(pl.multiple_of(i * QT, QT), QT)


def _rows(j):
    return pl.ds(pl.multiple_of(j * ATT, ATT), ATT)


def _keys_upto(i):
    return (i + 1) * (QT // ATT)


def _triangle(keep):
    return keep(_iota2((ATT, ATT), 0), _iota2((ATT, ATT), 1)).astype(MXU_DTYPE)


def _rows128(i):
    return pl.ds(pl.multiple_of(i * BLK, BLK), BLK)


def _log_sigmoid_parts(z):
    e = jnp.exp(-jnp.abs(z))
    l1p = jnp.log(1.0 + e)
    lb = jnp.minimum(z, 0.0) - l1p
    return lb, lb - z, e


MIXER_W = N_HEADS * HEAD_DIM


def _mixer_spec(s, block):
    return pl.BlockSpec((s, MIXER_W), lambda b: (b, block))


def _heads_spec(s, width):
    return pl.BlockSpec((None, N_HEADS, s, width), lambda b: (b, 0, 0, 0))


def _head_masks(heads=2):
    lane = _iota2((1, heads * HEAD_DIM), 1)
    return [(lane >= h * HEAD_DIM) & (lane < (h + 1) * HEAD_DIM) for h in range(heads)]


def _split_heads(ref, scr, sels):
    for h, sel in enumerate(sels):
        scr[h] = jnp.where(sel, ref[...], 0.0).astype(MXU_DTYPE)


def _sb_fwd(proj, nb, s, name, carry=None):
    def body(q_ref, k_ref, v_ref, o_ref, tails_ref, km, vm):
        sels = _head_masks(N_HEADS)
        _split_heads(k_ref, km, sels)
        _split_heads(v_ref, vm, sels)
        rows = _iota2((QT, ATT), 0)
        cols = _iota2((QT, ATT), 1)
        lane = _iota2((QT, BLK), 1)
        later = _triangle(lambda r, c: r > c)
        tails_ref[...] = jnp.zeros_like(tails_ref)

        def qblock(i, _):
            qi = (q_ref[_qrows(i), :] * 0.125).astype(MXU_DTYPE)

            def kblock(t, state):
                carries, acc = state
                j = _keys_upto(i) - 1 - t
                strict = (cols + (j * ATT - i * QT)) < rows
                out = []
                for h in range(N_HEADS):
                    tails_ref[h, _qrows(i), :] = jnp.where(lane == j, carries[h], tails_ref[h, _qrows(i), :])
                    z = _dot_nt(qi, km[h, _rows(j), :])
                    lb, lr, _ = _log_sigmoid_parts(z)
                    lr = jnp.where(strict, lr, 0.0)
                    tail = _split_dot(lr, later, 2) + carries[h]
                    a = jnp.where(strict, jnp.exp(lb + tail), 0.0)
                    acc = acc + _dot(a, vm[h, _rows(j), :])
                    out.append(carries[h] + jnp.sum(lr, axis=-1, keepdims=True))
                return tuple(out), acc

            init = ((jnp.zeros((QT, 1), F32),) * N_HEADS, jnp.zeros((QT, MIXER_W), F32))
            _, acc = lax.fori_loop(0, _keys_upto(i), kblock, init)
            o_ref[_qrows(i), :] = acc
            return 0

        lax.fori_loop(0, s // QT, qblock, 0)

    return _host_call(
        body, carry, name=name, grid=(nb,), in_specs=[_mixer_spec(s, 0), _mixer_spec(s, 1), _mixer_spec(s, 2)],
        out_specs=[_mixer_spec(s, 0), _heads_spec(s, BLK)],
        out_shape=[jax.ShapeDtypeStruct((nb * s, MIXER_W), F32), jax.ShapeDtypeStruct((nb, N_HEADS, s, BLK), F32)],
        scratch_shapes=[pltpu.VMEM((N_HEADS, s, MIXER_W), MXU_DTYPE)] * 2, operands=(proj, proj, proj))


def _sb_bwd(proj, dmixed, tails, nb, s, name, carry=None):
    def body(q_ref, k_ref, v_ref, do_ref, tails_ref, dq_ref, dk_ref, dv_ref, km, vm):
        sels = _head_masks(N_HEADS)
        _split_heads(k_ref, km, sels)
        _split_heads(v_ref, vm, sels)
        rows = _iota2((QT, ATT), 0)
        cols = _iota2((QT, ATT), 1)
        lane = _iota2((QT, BLK), 1)
        later = _triangle(lambda r, c: r > c)
        earlier = _triangle(lambda r, c: r < c)
        dk_ref[...] = jnp.zeros_like(dk_ref)
        dv_ref[...] = jnp.zeros_like(dv_ref)

        def qblock(i, _):
            qi = (q_ref[_qrows(i), :] * 0.125).astype(MXU_DTYPE)
            doi = do_ref[_qrows(i), :].astype(MXU_DTYPE)
            qm = [jnp.where(sel, qi, 0.0) for sel in sels]
            dom = [jnp.where(sel, doi, 0.0) for sel in sels]
            tails_i = [tails_ref[h, _qrows(i), :] for h in range(N_HEADS)]

            def kblock(j, state):
                csums, dq = state
                strict = (cols + (j * ATT - i * QT)) < rows
                out = []
                for h in range(N_HEADS):
                    z = _dot_nt(qi, km[h, _rows(j), :])
                    lb, lr, _ = _log_sigmoid_parts(z)
                    lr = jnp.where(strict, lr, 0.0)
                    after = jnp.sum(jnp.where(lane == j, tails_i[h], 0.0), axis=-1, keepdims=True)
                    a = jnp.where(strict, jnp.exp(lb + _split_dot(lr, later, 2) + after), 0.0)
                    dl = a * _dot_nt(doi, vm[h, _rows(j), :])
                    beta = jnp.exp(lb)
                    before = _split_dot(dl, earlier, 2) + csums[h]
                    dz = jnp.where(strict, dl * (1.0 - beta) - beta * before, 0.0).astype(MXU_DTYPE)
                    dq = dq + _dot(dz, km[h, _rows(j), :])
                    dk_ref[_rows(j), :] += _dot_tn(dz, qm[h])
                    dv_ref[_rows(j), :] += _dot_tn(a, dom[h])
                    out.append(csums[h] + jnp.sum(dl, axis=-1, keepdims=True))
                return tuple(out), dq

            init = ((jnp.zeros((QT, 1), F32),) * N_HEADS, jnp.zeros((QT, MIXER_W), F32))
            _, dq = lax.fori_loop(0, _keys_upto(i), kblock, init)
            dq_ref[_qrows(i), :] = dq * 0.125
            return 0

        lax.fori_loop(0, s // QT, qblock, 0)

    out = _mixer_spec(s, 0)
    return _host_call(
        body, carry, name=name, grid=(nb,),
        in_specs=[_mixer_spec(s, 0), _mixer_spec(s, 1), _mixer_spec(s, 2), out, _heads_spec(s, BLK)], out_specs=[out] * 3,
        out_shape=[jax.ShapeDtypeStruct((nb * s, MIXER_W), F32)] * 3,
        scratch_shapes=[pltpu.VMEM((N_HEADS, s, MIXER_W), MXU_DTYPE)] * 2, operands=(proj, proj, proj, dmixed, tails))


def _pair_spec(s, width):
    return pl.BlockSpec((None, 2, s, width), lambda b, p: (b, p, 0, 0))


def _fox_fwd(proj, ccol, crow, nb, s, name, carry=None):
    nblk = s // ATT

    def body(q_ref, k_ref, v_ref, cc_ref, cr_ref, o_ref, lse_ref, km, vm):
        sels = _head_masks()
        _split_heads(k_ref, km, sels)
        _split_heads(v_ref, vm, sels)
        rows = _iota2((QT, ATT), 0)
        cols = _iota2((QT, ATT), 1)

        def qblock(i, _):
            qi = (q_ref[_qrows(i), :] * 0.125).astype(MXU_DTYPE)
            ci = [cc_ref[h, _qrows(i), :] for h in range(2)]

            def kblock(j, state):
                ms, ls, acc = state
                causal = (cols + (j * ATT - i * QT)) <= rows
                new_m, new_l, scales, parts = [], [], [], []
                for h in range(2):
                    z = _dot_nt(qi, km[h, _rows(j), :]) + (ci[h] - cr_ref[h, j][0:1, :])
                    z = jnp.where(causal, z, NEG)
                    m_new = jnp.maximum(ms[h], jnp.max(z, axis=-1, keepdims=True))
                    p = jnp.exp(z - m_new)
                    scale = jnp.exp(ms[h] - m_new)
                    new_m.append(m_new)
                    new_l.append(scale * ls[h] + jnp.sum(p, axis=-1, keepdims=True))
                    scales.append(scale)
                    parts.append(_dot(p, vm[h, _rows(j), :]))
                acc = jnp.where(sels[0], scales[0], scales[1]) * acc + parts[0] + parts[1]
                return tuple(new_m), tuple(new_l), acc

            init = ((jnp.full((QT, 1), NEG, F32),) * 2, (jnp.zeros((QT, 1), F32),) * 2, jnp.zeros((QT, BLK), F32))
            ms, ls, acc = lax.fori_loop(0, _keys_upto(i), kblock, init)
            o_ref[_qrows(i), :] = acc / jnp.where(sels[0], ls[0], ls[1])
            for h in range(2):
                lse_ref[h, _qrows(i), :] = jnp.broadcast_to(ms[h] + jnp.log(ls[h]), (QT, ATT))
            return 0

        lax.fori_loop(0, s // QT, qblock, 0)

    crow_spec = pl.BlockSpec((None, 2, nblk, 8, ATT), lambda b, p: (b, p, 0, 0, 0))
    return _host_call(
        body, carry, name=name, grid=(nb, 2),
        in_specs=[_col_spec(s, 12), _col_spec(s, 14), _col_spec(s, 16), _pair_spec(s, ATT), crow_spec],
        out_specs=[_col_spec(s, 0), _pair_spec(s, ATT)],
        out_shape=[jax.ShapeDtypeStruct((nb * s, 2 * BLK), F32), jax.ShapeDtypeStruct((nb, N_HEADS, s, ATT), F32)],
        scratch_shapes=[pltpu.VMEM((2, s, BLK), MXU_DTYPE)] * 2, operands=(proj, proj, proj, ccol, crow))


def _fox_bwd(proj, dmixed, lse, ccol, crow, nb, s, name, carry=None):
    nblk = s // ATT

    def body(q_ref, k_ref, v_ref, do_ref, lse_ref, cc_ref, cr_ref, dq_ref, dk_ref, dv_ref, dc_ref, km, vm, p_scr, dp_scr):
        sels = _head_masks()
        _split_heads(k_ref, km, sels)
        _split_heads(v_ref, vm, sels)
        rows = _iota2((QT, ATT), 0)
        cols = _iota2((QT, ATT), 1)
        dk_ref[...] = jnp.zeros_like(dk_ref)
        dv_ref[...] = jnp.zeros_like(dv_ref)
        dc_ref[...] = jnp.zeros_like(dc_ref)

        def qblock(i, _):
            qi = (q_ref[_qrows(i), :] * 0.125).astype(MXU_DTYPE)
            doi = do_ref[_qrows(i), :].astype(MXU_DTYPE)
            qm = [jnp.where(sel, qi, 0.0) for sel in sels]
            dom = [jnp.where(sel, doi, 0.0) for sel in sels]
            ci = [cc_ref[h, _qrows(i), :] for h in range(2)]
            lsei = [lse_ref[h, _qrows(i), :] for h in range(2)]

            def probs(j, h):
                z = _dot_nt(qi, km[h, _rows(j), :]) + (ci[h] - cr_ref[h, j][0:1, :])
                p = jnp.where((cols + (j * ATT - i * QT)) <= rows, jnp.exp(z - lsei[h]), 0.0)
                return p, _dot_nt(doi, vm[h, _rows(j), :])

            def row_term(j, accs):
                out = []
                for h in range(2):
                    p, dp = probs(j, h)
                    p_scr[h, j] = p
                    dp_scr[h, j] = dp
                    out.append(accs[h] + jnp.sum(p * dp, axis=-1, keepdims=True))
                return tuple(out)

            di = lax.fori_loop(0, _keys_upto(i), row_term, (jnp.zeros((QT, 1), F32),) * 2)

            def kblock(j, dq):
                for h in range(2):
                    p = p_scr[h, j]
                    ds = p * (dp_scr[h, j] - di[h])
                    dc_ref[h, j] += jnp.broadcast_to(jnp.sum(ds, axis=0, keepdims=True), (8, ATT))
                    ds = ds.astype(MXU_DTYPE)
                    dk_ref[_rows(j), :] += _dot_tn(ds, qm[h])
                    dv_ref[_rows(j), :] += _dot_tn(p, dom[h])
                    dq = dq + _dot(ds, km[h, _rows(j), :])
                return dq

            dq = lax.fori_loop(0, _keys_upto(i), kblock, jnp.zeros((QT, BLK), F32))
            dq_ref[_qrows(i), :] = dq * 0.125
            return 0

        lax.fori_loop(0, s // QT, qblock, 0)

    crow_spec = pl.BlockSpec((None, 2, nblk, 8, ATT), lambda b, p: (b, p, 0, 0, 0))
    wide, cols_out = _pair_spec(s, ATT), _col_spec(s, 0)
    return _host_call(
        body, carry, name=name, grid=(nb, 2),
        in_specs=[_col_spec(s, 12), _col_spec(s, 14), _col_spec(s, 16), _col_spec(s, 4), wide, wide, crow_spec],
        out_specs=[cols_out, cols_out, cols_out, crow_spec],
        out_shape=[jax.ShapeDtypeStruct((nb * s, 2 * BLK), F32)] * 3 + [jax.ShapeDtypeStruct((nb, N_HEADS, nblk, 8, ATT), F32)],
        scratch_shapes=[pltpu.VMEM((2, s, BLK), MXU_DTYPE)] * 2 + [pltpu.VMEM((2, nblk, QT, ATT), F32)] * 2,
        operands=(proj, proj, proj, dmixed, lse, ccol, crow))


def _fox_gates_fwd(proj, f_bias, nb, s, name):
    chunk = 256

    def body(f_ref, b_ref, c_ref):
        lower = (_iota2((chunk, chunk), 0) >= _iota2((chunk, chunk), 1)).astype(MXU_DTYPE)
        carry = jnp.zeros((1, BLK), F32)
        for n in range(s // chunk):
            rows = pl.ds(n * chunk, chunk)
            lf, _, _ = _log_sigmoid_parts(f_ref[rows, :] + b_ref[0:1, :])
            c = _split_dot_lhs(lower, lf, 3) + carry
            c_ref[rows, :] = c
            carry = c[chunk - 1:chunk, :]

    return pl.pallas_call(
        body, name=name, grid=(nb,),
        in_specs=[pl.BlockSpec((s, BLK), lambda b: (b, (PROJ_PAD - BLK) // BLK)), pl.BlockSpec((8, BLK), lambda b: (0, 0))],
        out_specs=pl.BlockSpec((s, BLK), lambda b: (b, 0)),
        out_shape=jax.ShapeDtypeStruct((nb * s, BLK), F32), compiler_params=_cparams("parallel"),
    )(proj, f_bias)


def _fox_gates_bwd(dc, proj, f_bias, nb, s, name):
    chunk = 256

    def body(dc_ref, f_ref, b_ref, df_ref, db_ref):
        upper = (_iota2((chunk, chunk), 0) <= _iota2((chunk, chunk), 1)).astype(MXU_DTYPE)
        carry = jnp.zeros((1, BLK), F32)
        total = jnp.zeros((1, BLK), F32)
        for n in reversed(range(s // chunk)):
            rows = pl.ds(n * chunk, chunk)
            dlf = _split_dot_lhs(upper, dc_ref[rows, :], 3) + carry
            carry = dlf[0:1, :]
            pre = f_ref[rows, :] + b_ref[0:1, :]
            e = jnp.exp(-jnp.abs(pre))
            df = dlf * (jnp.where(pre >= 0.0, e, 1.0) / (1.0 + e))
            df_ref[rows, :] = df
            total = total + jnp.sum(df, axis=0, keepdims=True)

        @pl.when(pl.program_id(0) == 0)
        def _():
            db_ref[...] = jnp.zeros_like(db_ref)

        db_ref[0:1, :] += total

    return pl.pallas_call(
        body, name=name, grid=(nb,),
        in_specs=[pl.BlockSpec((s, BLK), lambda b: (b, 0)), pl.BlockSpec((s, BLK), lambda b: (b, (PROJ_PAD - BLK) // BLK)),
                  pl.BlockSpec((8, BLK), lambda b: (0, 0))],
        out_specs=[pl.BlockSpec((s, BLK), lambda b: (b, 0)), pl.BlockSpec((8, BLK), lambda b: (0, 0))],
        out_shape=[jax.ShapeDtypeStruct((nb * s, BLK), F32), jax.ShapeDtypeStruct((8, BLK), F32)],
        compiler_params=_cparams("arbitrary"),
    )(dc, proj, f_bias)


def _delta_kernel(dmixed, o, nb, s, name):
    def body(do_ref, o_ref, d_ref):
        prod = do_ref[...] * o_ref[...]
        for h, sel in enumerate(_head_masks()):
            d_ref[h] = jnp.broadcast_to(jnp.sum(jnp.where(sel, prod, 0.0), axis=-1, keepdims=True), (s, BLK))

    return pl.pallas_call(
        body, name=name, grid=(nb, 2), in_specs=[_col_spec(s, 2), _col_spec(s, 0)], out_specs=_pair_spec(s, BLK),
        out_shape=jax.ShapeDtypeStruct((nb, N_HEADS, s, BLK), F32), compiler_params=_cparams("parallel", "parallel"),
    )(dmixed, o)


def _t5_bucket_np(dist):
    max_exact = REL_BUCKETS // 2
    nf = np.maximum(dist, 1).astype(np.float32)
    large = max_exact + (np.log(nf / max_exact) / math.log(2048 / max_exact) * (REL_BUCKETS - max_exact)).astype(np.int32)
    large = np.minimum(large, REL_BUCKETS - 1)
    return np.where(dist < max_exact, dist, large)


def _bucket_table():
    qi = np.arange(BLK)[:, None]
    kj = np.arange(2 * BLK)[None, :]
    dist = qi + BLK - kj
    tables = []
    for window, dil in DIL_PATTERNS:
        in_band = (dist >= 0) & (dist <= window // dil)
        tables.append(np.where(in_band, _t5_bucket_np(np.maximum(dist, 0) * dil), -1).astype(np.int32))
    return np.stack(tables)


def _dil_scores(qb, kp, kc, b_ref, h, prev_valid):
    zp = _dot_nt(qb, kp) + b_ref[h, :, 0:BLK]
    zp = jnp.where(prev_valid, zp, NEG)
    zc = _dot_nt(qb, kc) + b_ref[h, :, BLK:2 * BLK]
    return zp, zc


def _residue_rows(b, seg, dil):
    if dil == 1:
        return _rows128(b), _rows128(jnp.maximum(b - 1, 0)), b > 0
    r, n = b // seg, b % seg
    cur = pl.ds(r + dil * n * BLK, BLK, stride=dil)
    prev = pl.ds(r + dil * jnp.maximum(n - 1, 0) * BLK, BLK, stride=dil)
    return cur, prev, n > 0


def _dil_attention_fwd(proj, bias, nb, s, name, carry=None):
    nblk = s // BLK

    def body(q_ref, k_ref, v_ref, b_ref, out_ref, lse_ref, o_scr, l_scr):
        sels = _head_masks()
        for p, (_, dil) in enumerate(DIL_PATTERNS):
            seg = s // dil // BLK

            def block(b, _, p=p, seg=seg, dil=dil):
                cur, prev, has_prev = _residue_rows(b, seg, dil)
                qb = (q_ref[cur, :] * 0.125).astype(MXU_DTYPE)
                kp, kc = k_ref[prev, :].astype(MXU_DTYPE), k_ref[cur, :].astype(MXU_DTYPE)
                vp, vc = v_ref[prev, :].astype(MXU_DTYPE), v_ref[cur, :].astype(MXU_DTYPE)
                acc = jnp.zeros((BLK, BLK), F32)
                for h, sel in enumerate(sels):
                    zp, zc = _dil_scores(qb, jnp.where(sel, kp, 0.0), jnp.where(sel, kc, 0.0), b_ref.at[p], h, has_prev)
                    m = jnp.maximum(jnp.max(zp, axis=-1, keepdims=True), jnp.max(zc, axis=-1, keepdims=True))
                    pp = jnp.exp(zp - m)
                    pc = jnp.exp(zc - m)
                    den = jnp.sum(pp, axis=-1, keepdims=True) + jnp.sum(pc, axis=-1, keepdims=True)
                    acc = acc + (_dot(pp, jnp.where(sel, vp, 0.0)) + _dot(pc, jnp.where(sel, vc, 0.0))) / den
                    l_scr[p, h, cur, :] = jnp.broadcast_to(m + jnp.log(den), (BLK, BLK))
                o_scr[p, cur, :] = acc
                return 0

            lax.fori_loop(0, nblk, block, 0, unroll=4)

        weights, dens = [], []
        for h in range(2):
            m = jnp.maximum(jnp.maximum(l_scr[0, h], l_scr[1, h]), l_scr[2, h])
            w = [jnp.exp(l_scr[p, h] - m) for p in range(3)]
            den = w[0] + w[1] + w[2]
            lse_ref[h] = m + jnp.log(den)
            weights.append(w)
            dens.append(den)
        num = sum(jnp.where(sels[0], weights[0][p], weights[1][p]) * o_scr[p] for p in range(3))
        out_ref[...] = num / jnp.where(sels[0], dens[0], dens[1])

    bias_spec = pl.BlockSpec((3, 2, BLK, 2 * BLK), lambda b, p: (0, p, 0, 0))
    return _host_call(
        body, carry, name=name, grid=(nb, 2), in_specs=[_col_spec(s, 6), _col_spec(s, 8), _col_spec(s, 10), bias_spec],
        out_specs=[_col_spec(s, 0), _pair_spec(s, BLK)],
        out_shape=[jax.ShapeDtypeStruct((nb * s, 2 * BLK), F32), jax.ShapeDtypeStruct((nb, N_HEADS, s, BLK), F32)],
        scratch_shapes=[pltpu.VMEM((3, s, BLK), F32), pltpu.VMEM((3, 2, s, BLK), F32)], operands=(proj, proj, proj, bias))


def _dil_attention_bwd(proj, dmixed, lse, delta, bias, nb, s, name, carry=None):
    nblk = s // BLK

    def body(q_ref, k_ref, v_ref, do_ref, lse_ref, dl_ref, b_ref, dq_ref, dk_ref, dv_ref, g_ref):
        sels = _head_masks()
        dq_ref[...] = jnp.zeros_like(dq_ref)
        dk_ref[...] = jnp.zeros_like(dk_ref)
        dv_ref[...] = jnp.zeros_like(dv_ref)
        g_ref[...] = jnp.zeros_like(g_ref)
        for p, (_, dil) in enumerate(DIL_PATTERNS):
            seg = s // dil // BLK

            def block(b, _, p=p, seg=seg, dil=dil):
                cur, prev, has_prev = _residue_rows(b, seg, dil)
                qb = (q_ref[cur, :] * 0.125).astype(MXU_DTYPE)
                dob = do_ref[cur, :].astype(MXU_DTYPE)
                kp, kc = k_ref[prev, :].astype(MXU_DTYPE), k_ref[cur, :].astype(MXU_DTYPE)
                vp, vc = v_ref[prev, :].astype(MXU_DTYPE), v_ref[cur, :].astype(MXU_DTYPE)
                dq = jnp.zeros((BLK, BLK), F32)
                dkp, dkc, dvp, dvc = dq, dq, dq, dq
                for h, sel in enumerate(sels):
                    kph, kch = jnp.where(sel, kp, 0.0), jnp.where(sel, kc, 0.0)
                    qh, doh = jnp.where(sel, qb, 0.0), jnp.where(sel, dob, 0.0)
                    lse_h = lse_ref[h, cur, :]
                    dlt = dl_ref[h, cur, :]
                    zp, zc = _dil_scores(qb, kph, kch, b_ref.at[p], h, has_prev)
                    pp = jnp.exp(zp - lse_h)
                    pc = jnp.exp(zc - lse_h)
                    dsp = pp * (_dot_nt(dob, jnp.where(sel, vp, 0.0)) - dlt)
                    dsc = pc * (_dot_nt(dob, jnp.where(sel, vc, 0.0)) - dlt)
                    g_ref[h, p, :, 0:BLK] += dsp
                    g_ref[h, p, :, BLK:2 * BLK] += dsc
                    dsp = dsp.astype(MXU_DTYPE)
                    dsc = dsc.astype(MXU_DTYPE)
                    dq = dq + _dot(dsp, kph) + _dot(dsc, kch)
                    dkp, dkc = dkp + _dot_tn(dsp, qh), dkc + _dot_tn(dsc, qh)
                    dvp, dvc = dvp + _dot_tn(pp, doh), dvc + _dot_tn(pc, doh)
                dq_ref[cur, :] += dq * 0.125
                dk_ref[prev, :] += dkp
                dk_ref[cur, :] += dkc
                dv_ref[prev, :] += dvp
                dv_ref[cur, :] += dvc
                return 0

            lax.fori_loop(0, nblk, block, 0, unroll=4)

    bias_spec = pl.BlockSpec((3, 2, BLK, 2 * BLK), lambda b, p: (0, p, 0, 0))
    cols, stats = _col_spec(s, 0), _pair_spec(s, BLK)
    return _host_call(
        body, carry, name=name, grid=(nb, 2),
        in_specs=[_col_spec(s, 6), _col_spec(s, 8), _col_spec(s, 10), _col_spec(s, 2), stats, stats, bias_spec],
        out_specs=[cols, cols, cols, pl.BlockSpec((None, 2, 3, BLK, 2 * BLK), lambda b, p: (b, p, 0, 0, 0))],
        out_shape=[jax.ShapeDtypeStruct((nb * s, 2 * BLK), F32)] * 3 + [jax.ShapeDtypeStruct((nb, N_HEADS, 3, BLK, 2 * BLK), F32)],
        operands=(proj, proj, proj, dmixed, lse, delta, bias))


def _bucket_reduce(gbias, table, name):
    nb = gbias.shape[0]

    def body(g_ref, t_ref, o_ref):
        row = _iota2((8, BLK), 0)
        lane = _iota2((8, BLK), 1)
        gsum = [[sum(g_ref[b, h, p] for b in range(nb)) for p in range(3)] for h in range(N_HEADS)]

        def bucket(k, acc):
            for h in range(N_HEADS):
                tot = sum(jnp.sum(jnp.where(t_ref[p] == k, gsum[h][p], 0.0)) for p in range(3))
                acc = acc + jnp.where((row == h) & (lane == k), tot, 0.0)
            return acc

        o_ref[...] = lax.fori_loop(0, REL_BUCKETS, bucket, jnp.zeros((8, BLK), F32))

    vm = pl.BlockSpec(memory_space=pltpu.VMEM)
    return pl.pallas_call(
        body, name=name, in_specs=[vm, vm], out_specs=vm, out_shape=jax.ShapeDtypeStruct((8, BLK), F32),
        compiler_params=pltpu.CompilerParams(vmem_limit_bytes=VMEM_LIMIT),
    )(gbias, table)


def _place():
    x, y, c = lax.axis_index("x"), lax.axis_index("y"), lax.axis_index("c")
    others = [(1 - x, y), (x, 1 - y), (1 - x, 1 - y)]
    return x, y, c, others


def _remote(src, dst, send_sem, recv_sem, to):
    return pltpu.make_async_remote_copy(src_ref=src, dst_ref=dst, send_sem=send_sem, recv_sem=recv_sem,
                                        device_id=to, device_id_type=MESH)


_HBM = pl.BlockSpec(memory_space=pl.ANY)


class _Exchange:
    def __init__(self, operands, out_shape, n_copies, copies, aliases=None):
        self.operands, self.out_shape, self.n_copies, self.copies = list(operands), list(out_shape), n_copies, copies
        self.aliases = dict(aliases or {})

    def sem_shapes(self):
        return [pltpu.SemaphoreType.DMA((self.n_copies,)), pltpu.SemaphoreType.DMA((self.n_copies,))]


def _start_all(sends):
    for cp in sends:
        cp.start()


def _wait_all(sends, arrivals):
    for cp in arrivals:
        cp.wait_recv()
    for cp in sends:
        cp.wait_send()


def _run_exchange(ex, name):
    ni = len(ex.operands)

    def body(*refs):
        sends, arrivals = ex.copies(refs[:ni], refs[ni:-2], refs[-2], refs[-1])
        _start_all(sends)
        _wait_all(sends, arrivals)

    return list(pl.pallas_call(
        body, name=name, in_specs=[_HBM] * ni, out_specs=[_HBM] * len(ex.out_shape), out_shape=ex.out_shape,
        scratch_shapes=ex.sem_shapes(), input_output_aliases=ex.aliases)(*ex.operands))


def _host_call(body, carry, *, name, grid, in_specs, out_specs, out_shape, operands, scratch_shapes=()):
    in_specs, out_specs, out_shape, scratch_shapes = list(in_specs), list(out_specs), list(out_shape), list(scratch_shapes)
    if carry is None:
        res = pl.pallas_call(body, name=name, grid=grid, in_specs=in_specs, out_specs=out_specs, out_shape=out_shape,
                             scratch_shapes=scratch_shapes, compiler_params=_cparams(*["parallel"] * len(grid)))(*operands)
        return list(res), []
    n_in, n_out, n_scr, c_in, c_out = len(in_specs), len(out_specs), len(scratch_shapes), len(carry.operands), len(carry.out_shape)
    steps = math.prod(grid)

    def wrapped(*refs):
        ins, refs = refs[:n_in], refs[n_in:]
        c_ins, refs = refs[:c_in], refs[c_in:]
        outs, refs = refs[:n_out], refs[n_out:]
        c_outs, refs = refs[:c_out], refs[c_out:]
        scr, (send_sems, recv_sems) = refs[:n_scr], refs[n_scr:]
        step = 0
        for d, size in enumerate(grid):
            step = step * size + pl.program_id(d)

        @pl.when(step == 0)
        def _():
            _start_all(carry.copies(c_ins, c_outs, send_sems, recv_sems)[0])

        body(*ins, *outs, *scr)

        @pl.when(step == steps - 1)
        def _():
            _wait_all(*carry.copies(c_ins, c_outs, send_sems, recv_sems))

    res = pl.pallas_call(
        wrapped, name=name, grid=grid, in_specs=in_specs + [_HBM] * c_in, out_specs=out_specs + [_HBM] * c_out,
        out_shape=out_shape + carry.out_shape, scratch_shapes=scratch_shapes + carry.sem_shapes(),
        input_output_aliases={n_in + i: n_out + j for i, j in carry.aliases.items()},
        compiler_params=_cparams(*["arbitrary"] * len(grid)))(*operands, *carry.operands)
    return list(res[:n_out]), list(res[n_out:])


def _half(which, rows):
    h = rows // 2
    return pl.ds(pl.multiple_of(which * h, 16), h)


def _like(arrays, shape_of=lambda t: t.shape):
    return [jax.ShapeDtypeStruct(shape_of(t), t.dtype) for t in arrays]


def _gather_ici(shards, layer):
    n = len(shards)

    def copies(ins, outs, send_sems, recv_sems, base=0):
        x, y, c, others = _place()
        me = 2 * x + y
        sends, arrivals = [], []
        for a in range(n):
            rows = _half(c, shards[a].shape[1])
            for k, (ox, oy) in enumerate(others):
                sems = (send_sems.at[base + 3 * a + k], recv_sems.at[base + 3 * a + k],(ox, oy, c))
                sends.append(_remote(ins[a].at[layer, rows], outs[a].at[me, rows], *sems))
                landed = outs[a].at[2 * ox + oy, rows]
                arrivals.append(_remote(landed, landed, *sems))
        return sends, arrivals

    return _Exchange(shards, _like(shards, lambda t: (N_CHIPS,) + t.shape[1:]), 3 * n, copies)


def _gather_d2d(gathered):
    n = len(gathered)

    def copies(ins, outs, send_sems, recv_sems, base=0):
        x, y, c, others = _place()
        sends, arrivals = [], []
        for a in range(n):
            r = gathered[a].shape[1]
            for k, (ox, oy) in enumerate(others):
                sems = (send_sems.at[base + 3 * a + k], recv_sems.at[base + 3 * a + k],(x, y, 1 - c))
                mine, theirs = outs[a].at[2 * ox + oy, _half(c, r)], outs[a].at[2 * ox + oy, _half(1 - c, r)]
                sends.append(_remote(mine, mine, *sems))
                arrivals.append(_remote(theirs, theirs, *sems))
        return sends, arrivals

    return _Exchange(gathered, _like(gathered), 3 * n, copies, aliases={a: a for a in range(n)})


def _swap_halves(g):
    n = len(g)

    def copies(ins, outs, send_sems, recv_sems, base=0):
        x, y, c, _ = _place()
        sends, arrivals = [], []
        for a in range(n):
            sems = (send_sems.at[base + a], recv_sems.at[base + a], (x, y, 1 - c))
            sends.append(_remote(ins[a].at[:, _half(1 - c, g[a].shape[1])], outs[a], *sems))
            arrivals.append(_remote(outs[a], outs[a], *sems))
        return sends, arrivals

    return _Exchange(g, _like(g, lambda t: (t.shape[0], t.shape[1] // 2, t.shape[2])), n, copies)


def _scatter_shards(ps):
    n = len(ps)

    def copies(ins, outs, send_sems, recv_sems, base=0):
        x, y, c, others = _place()
        me = 2 * x + y
        sends, arrivals = [], []
        for a in range(n):
            for k, (ox, oy) in enumerate(others):
                sems = (send_sems.at[base + 3 * a + k], recv_sems.at[base + 3 * a + k],(ox, oy, c))
                sends.append(_remote(ins[a].at[2 * ox + oy], outs[a].at[me], *sems))
                slot = outs[a].at[2 * ox + oy]
                arrivals.append(_remote(slot, slot, *sems))
        return sends, arrivals

    return _Exchange(ps, _like(ps), 3 * n, copies)


def _share_halves(mine):
    n = len(mine)

    def copies(ins, outs, send_sems, recv_sems, base=0):
        x, y, c, _ = _place()
        sends, arrivals = [], []
        for a in range(n):
            sems = (send_sems.at[base + a], recv_sems.at[base + a], (x, y, 1 - c))
            sends.append(_remote(ins[a], outs[a], *sems))
            arrivals.append(_remote(outs[a], outs[a], *sems))
        return sends, arrivals

    return _Exchange(mine, _like(mine), n, copies)


def _row_tile(r):
    for cand in (256, 352, 128):
        if r % cand == 0:
            return cand
    return r


def _pair_sum(g, other, core, name):
    ns, h, w = other.shape
    tr = _row_tile(h)
    per_half = h // tr

    def body(core_ref, g_ref, o_ref, out_ref):
        out_ref[...] = (g_ref[...] + o_ref[...]).astype(out_ref.dtype)

    blk = pl.BlockSpec((None, tr, w), lambda k, i, core_ref: (k, i, 0))
    grid_spec = pltpu.PrefetchScalarGridSpec(
        num_scalar_prefetch=1, grid=(ns, per_half),
        in_specs=[pl.BlockSpec((None, tr, w), lambda k, i, core_ref: (k, core_ref[0] * per_half + i, 0)), blk], out_specs=blk)
    return pl.pallas_call(
        body, name=name, grid_spec=grid_spec, out_shape=jax.ShapeDtypeStruct((ns, h, w), MXU_DTYPE),
        compiler_params=_cparams("parallel", "parallel"),
    )(core.reshape(1).astype(jnp.int32), g, other)


def _chip_sum(q, p, chip, name):
    ns, r, w = q.shape
    tr = _row_tile(r)

    def body(chip_ref, q_ref, own_ref, out_ref):
        me = chip_ref[0]
        own = own_ref[...].astype(F32)
        terms = [jnp.where(me == k, own, q_ref[k].astype(F32)) for k in range(ns)]
        out_ref[...] = ((terms[0] + terms[1]) + terms[2]) + terms[3]

    grid_spec = pltpu.PrefetchScalarGridSpec(
        num_scalar_prefetch=1, grid=(r // tr,),
        in_specs=[pl.BlockSpec((ns, tr, w), lambda i, chip_ref: (0, i, 0)),
                  pl.BlockSpec((None, tr, w), lambda i, chip_ref: (chip_ref[0], i, 0))],
        out_specs=pl.BlockSpec((tr, w), lambda i, chip_ref: (i, 0)))
    return pl.pallas_call(
        body, name=name, grid_spec=grid_spec, out_shape=jax.ShapeDtypeStruct((r, w), F32),
        compiler_params=_cparams("parallel"),
    )(chip.reshape(1).astype(jnp.int32), q, p)


def _merge(exchanges):
    if len(exchanges) <= 1:
        return exchanges[0] if exchanges else None
    operands, out_shape, aliases, spans, n = [], [], {}, [], 0
    for ex in exchanges:
        spans.append((len(operands), len(out_shape), n))
        aliases.update({len(operands) + i: len(out_shape) + j for i, j in ex.aliases.items()})
        operands += ex.operands
        out_shape += ex.out_shape
        n += ex.n_copies

    def copies(ins, outs, send_sems, recv_sems, base=0):
        sends, arrivals = [], []
        for ex, (i0, o0, s0) in zip(exchanges, spans):
            s, a = ex.copies(ins[i0:i0 + len(ex.operands)], outs[o0:o0 + len(ex.out_shape)], send_sems, recv_sems, base + s0)
            sends += s
            arrivals += a
        return sends, arrivals

    return _Exchange(operands, out_shape, n, copies, aliases)


def _take(hooks, host):
    stages = (hooks or {}).pop(host, [])
    exchanges = [make() for make, _ in stages]

    def finish(results):
        for (_, done), ex in zip(stages, exchanges):
            done(results[:len(ex.out_shape)])
            results = results[len(ex.out_shape):]

    return _merge(exchanges), finish


def _hook(hooks, host, make, done):
    hooks.setdefault(host, []).append((make, done))


class _WeightPrefetch:
    def __init__(self, names, shards, layer, chip):
        self.names, self.shards, self.layer, self.chip, self.result = names, [shards[n] for n in names], layer, chip, None

    def first(self):
        return _gather_ici(self.shards, self.layer)

    def got_first(self, arrived):
        self.arrived = arrived

    def second(self):
        return _gather_d2d(self.arrived)

    def got_second(self, gathered):
        self.result = {name: lax.dynamic_update_index_in_dim(got, own[self.layer], self.chip, 0)
                       for name, got, own in zip(self.names, gathered, self.shards)}

    def ride(self, hooks, first_host, second_host):
        _hook(hooks, first_host, self.first, self.got_first)
        _hook(hooks, second_host, self.second, self.got_second)

    def run(self, tag):
        self.got_first(_run_exchange(self.first(), f"gather_ici_{tag}"))
        self.got_second(_run_exchange(self.second(), f"gather_d2d_{tag}"))


class _GradReduce:
    def __init__(self, g, chip, core, tag):
        self.names, self.g, self.chip, self.core, self.tag, self.result = list(g), list(g.values()), chip, core, tag, None

    def swap(self):
        return _swap_halves(self.g)

    def got_swap(self, theirs):
        self.pair = [_pair_sum(g, t, self.core, f"pair_sum_{n}_{self.tag}") for n, g, t in zip(self.names, self.g, theirs)]

    def scatter(self):
        return _scatter_shards(self.pair)

    def got_scatter(self, q):
        self.mine = [_chip_sum(qa, pa, self.chip, f"chip_sum_{n}_{self.tag}") for n, qa, pa in zip(self.names, q, self.pair)]

    def share(self):
        return _share_halves(self.mine)

    def got_share(self, theirs):
        self.result = {n: jnp.where(self.core == 0, jnp.concatenate([a, b]), jnp.concatenate([b, a]))
                       for n, a, b in zip(self.names, self.mine, theirs)}

    def ride(self, hooks, swap_host, scatter_host, share_host):
        _hook(hooks, swap_host, self.swap, self.got_swap)
        _hook(hooks, scatter_host, self.scatter, self.got_scatter)
        _hook(hooks, share_host, self.share, self.got_share)

    def run(self):
        self.got_swap(_run_exchange(self.swap(), f"swap_halves_{self.tag}"))
        self.got_scatter(_run_exchange(self.scatter(), f"scatter_shards_{self.tag}"))
        self.got_share(_run_exchange(self.share(), f"share_halves_{self.tag}"))


class _LayerWeights:
    def __init__(self, gathered):
        self.gathered, self.made = gathered, {}

    def __getitem__(self, key):
        if key not in self.made:
            cols = lambda t: jnp.swapaxes(t, 0, 1).reshape(t.shape[1], -1)
            rows = lambda t: t.reshape(-1, t.shape[2])
            if key == "w_in":
                made = jnp.pad(cols(self.gathered("w_in")), ((0, 0), (0, PROJ_PAD - PROJ)))
            elif key == "w_gu":
                made = jnp.concatenate([cols(self.gathered("w_gate")), cols(self.gathered("w_up"))], axis=-1)
            else:
                made = rows(self.gathered(key))
            self.made[key] = made
        return self.made[key]


def _gather_small(pk, name):
    rows, w = pk.shape

    def body(pk_ref, all_ref, sum_ref, send_sems, recv_sems):
        x, y, c, _ = _place()
        me = 4 * x + 2 * y + c
        all_ref[me] = pk_ref[...]
        flips = [(fx, fy, fc) for fx in (0, 1) for fy in (0, 1) for fc in (0, 1)][1:]
        peers = [(x ^ fx, y ^ fy, c ^ fc) for fx, fy, fc in flips]
        sends = [_remote(pk_ref, all_ref.at[me], send_sems.at[k], recv_sems.at[k], peer) for k, peer in enumerate(peers)]
        for cp in sends:
            cp.start()
        for k, (px, py, pc) in enumerate(peers):
            slot = all_ref.at[4 * px + 2 * py + pc]
            _remote(slot, slot, send_sems.at[k], recv_sems.at[k], (px, py, pc)).wait_recv()
        for cp in sends:
            cp.wait_send()
        total = all_ref[0]
        for d in range(1, N_DEV):
            total = total + all_ref[d]
        sum_ref[...] = total

    vm = pl.BlockSpec(memory_space=pltpu.VMEM)
    return pl.pallas_call(
        body, name=name, in_specs=[vm], out_specs=[vm, vm],
        out_shape=[jax.ShapeDtypeStruct((N_DEV, rows, w), F32), jax.ShapeDtypeStruct((rows, w), F32)],
        scratch_shapes=[pltpu.SemaphoreType.DMA((7,)), pltpu.SemaphoreType.DMA((7,))],
    )(pk)


def _row_layout(c, nb, s):
    ch = jnp.swapaxes(c[:, :N_HEADS].reshape(nb, s, N_HEADS), 1, 2)
    ccol = jnp.broadcast_to(ch[..., None], (nb, N_HEADS, s, ATT))
    crow = jnp.broadcast_to(ch.reshape(nb, N_HEADS, s // ATT, 1, ATT), (nb, N_HEADS, s // ATT, 8, ATT))
    return ccol, crow


def _dil_bias(rel_bias, name):
    def body(rel_ref, t_ref, o_ref):
        for p in range(len(DIL_PATTERNS)):
            table = t_ref[p]

            def bucket(k, accs, table=table):
                return tuple(jnp.where(table == k, rel_ref[k, h], acc) for h, acc in enumerate(accs))

            accs = lax.fori_loop(0, REL_BUCKETS, bucket, tuple(jnp.full((BLK, 2 * BLK), NEG, F32) for _ in range(N_HEADS)))
            for h in range(N_HEADS):
                o_ref[p, h] = accs[h]

    vm = pl.BlockSpec(memory_space=pltpu.VMEM)
    return pl.pallas_call(
        body, name=name, in_specs=[pl.BlockSpec(memory_space=pltpu.SMEM), vm], out_specs=vm,
        out_shape=jax.ShapeDtypeStruct((len(DIL_PATTERNS), N_HEADS, BLK, 2 * BLK), F32),
        compiler_params=pltpu.CompilerParams(vmem_limit_bytes=VMEM_LIMIT),
    )(rel_bias, jnp.asarray(_bucket_table()))


def _layer_forward(x, x_b, wts, small, bias, nb, s, tag, hooks=None):
    proj = _matmul(x_b, wts["w_in"], "proj", tag)

    carry, finish = _take(hooks, "sb_fwd")
    (o_sb, tails_sb), carried = _sb_fwd(proj, nb, s, f"sb_fwd_{tag}", carry)
    finish(carried)

    carry, finish = _take(hooks, "dil_fwd")
    (o_dl, lse_dl), carried = _dil_attention_fwd(proj, bias, nb, s, f"dil_fwd_{tag}", carry)
    finish(carried)

    fb = jnp.zeros((8, BLK), F32).at[0, :N_HEADS].set(small["f_bias"])
    csum = _fox_gates_fwd(proj, fb, nb, s, f"fox_gates_{tag}")
    ccol, crow = _row_layout(csum, nb, s)
    carry, finish = _take(hooks, "fox_fwd")
    (o_fx, lse_fx), carried = _fox_fwd(proj, ccol, crow, nb, s, f"fox_fwd_{tag}", carry)
    finish(carried)

    cw = jnp.zeros((8, CONV_W), F32).at[:3].set(small["conv_w"])
    o_cv = _conv_fwd(proj, cw, nb, s, f"conv_fwd_{tag}")

    mixed = jnp.concatenate([o_sb, o_dl, o_fx, o_cv], axis=-1).astype(MXU_DTYPE)
    pre1, x1, x1_b = _matmul_post_norm(mixed, wts["w_out"], x, small["ln1_g"], small["ln1_b"], f"out_proj_ln1_{tag}")
    carry, finish = _take(hooks, "ffn_in")
    (gate, up, hid), carried = _ffn_in(x1_b, wts["w_gu"], f"ffn_in_{tag}", carry)
    finish(carried)
    pre2, x2, x2_b = _matmul_post_norm(hid, wts["w_down"], x1, small["ln2_g"], small["ln2_b"], f"ffn_out_ln2_{tag}")
    saved = dict(x_b=x_b, proj=proj, tails_sb=tails_sb, bias=bias, o_dl=o_dl, lse_dl=lse_dl, fb=fb, ccol=ccol, crow=crow, o_fx=o_fx,
                 lse_fx=lse_fx, cw=cw, mixed=mixed, pre1=pre1, x1_b=x1_b, gate=gate, up=up, hid=hid, pre2=pre2)
    return (x2, x2_b), saved


def _layer_backward(dx2, sv, wts, small, nb, s, tag, hooks=None, ffn_grads_ready=None):
    t = nb * s
    dpre2, dpre2_b, dgb2 = _ln_bwd(dx2, sv["pre2"], small["ln2_g"], f"ln2_bwd_{tag}")
    carry, finish = _take(hooks, "ffn_out_dx")
    (dgate, dup), carried = _ffn_out_dx(dpre2_b, wts["w_down"], sv["gate"], sv["up"], f"ffn_out_dx_{tag}", carry)
    finish(carried)
    dw_down = _matmul(sv["hid"], dpre2_b, "ffn_out_dw", tag, trans_a=True)
    dx1 = _ffn_in_dx(dgate, dup, wts["w_gu"], dpre2, f"ffn_in_dx_{tag}")
    x1_b = sv["x1_b"]
    dw_gate = _matmul(x1_b, dgate, "ffn_in_dw", f"{tag}_gate", trans_a=True)
    dw_up = _matmul(x1_b, dup, "ffn_in_dw", f"{tag}_up", trans_a=True)

    dpre1, dpre1_b, dgb1 = _ln_bwd(dx1, sv["pre1"], small["ln1_g"], f"ln1_bwd_{tag}")
    dmixed = _matmul(dpre1_b, wts["w_out"], "out_proj_dx", tag, trans_b=True)
    dw_out = _matmul(sv["mixed"], dpre1_b, "out_proj_dw", tag, trans_a=True)
    if ffn_grads_ready:
        ffn_grads_ready(dict(w_down=dw_down, w_gate=dw_gate, w_up=dw_up, w_out=dw_out))
    proj = sv["proj"]

    carry, finish = _take(hooks, "sb_bwd")
    (dq_sb, dk_sb, dv_sb), carried = _sb_bwd(proj, dmixed, sv["tails_sb"], nb, s, f"sb_bwd_{tag}", carry)
    finish(carried)

    delta_dl = _delta_kernel(dmixed, sv["o_dl"], nb, s, f"dil_delta_{tag}")
    carry, finish = _take(hooks, "dil_bwd")
    (dq_dl, dk_dl, dv_dl, gbias), carried = _dil_attention_bwd(proj, dmixed, sv["lse_dl"], delta_dl, sv["bias"], nb, s,
                                                               f"dil_bwd_{tag}", carry)
    finish(carried)
    drel = _bucket_reduce(gbias, jnp.asarray(_bucket_table()), f"rel_bias_grad_{tag}")

    carry, finish = _take(hooks, "fox_bwd")
    (dq_fx, dk_fx, dv_fx, dcol), carried = _fox_bwd(proj, dmixed, sv["lse_fx"], sv["ccol"], sv["crow"], nb, s,
                                                    f"fox_bwd_{tag}", carry)
    finish(carried)
    dcs = -jnp.swapaxes(dcol[:, :, :, 0, :].reshape(nb, N_HEADS, s), 1, 2).reshape(t, N_HEADS)
    dcs = jnp.pad(dcs, ((0, 0), (0, BLK - N_HEADS)))
    dfx, dfb = _fox_gates_bwd(dcs, proj, sv["fb"], nb, s, f"fox_gates_bwd_{tag}")

    dgates, dcw = _conv_bwd(dmixed, proj, sv["cw"], nb, s, f"conv_bwd_{tag}")

    dproj = jnp.concatenate([dq_sb, dk_sb, dv_sb, dq_dl, dk_dl, dv_dl, dq_fx, dk_fx, dv_fx, dgates, dfx],
                            axis=-1).astype(MXU_DTYPE)
    dx = _matmul(dproj, wts["w_in"], "proj_dx", tag, add=dpre1, add_scale=ALPHA, trans_b=True)
    dw_in = _matmul(sv["x_b"], dproj, "proj_dw", tag, trans_a=True)

    grads = dict(w_in=dw_in[:, :PROJ], w_out=dw_out, w_gate=dw_gate, w_up=dw_up, w_down=dw_down,
                 ln1_g=dgb1[0], ln1_b=dgb1[1], ln2_g=dgb2[0], ln2_b=dgb2[1], conv_w=dcw[:3], f_bias=dfb[0, :N_HEADS],
                 rel_bias=drel[:N_HEADS, :REL_BUCKETS].T)
    return dx, grads


class _NoExchanges:
    def forward_hooks(self, layer):
        return None

    def backward_hooks(self, layer):
        return None

    def ffn_grads_ready(self, layer):
        return None

    def layer_done(self, layer, grads):
        pass


def _local_step(x, target, weights_of, small_all, schedule=None):
    schedule = schedule or _NoExchanges()
    nb, s, d = x.shape
    h = x.reshape(nb * s, d)
    h_b = h.astype(MXU_DTYPE)
    bias = _dil_bias(small_all[0]["rel_bias"], "dil_bias")
    saved = []
    for layer in range(DEPTH):
        wts = weights_of(layer)
        (h, h_b), sv = _layer_forward(h, h_b, wts, small_all[layer], bias, nb, s, f"l{layer}", schedule.forward_hooks(layer))
        saved.append((sv, wts))
    dy, lossp = _loss_kernel(h, target.reshape(nb * s, d), "loss")
    grads = [None] * DEPTH
    for layer in reversed(range(DEPTH)):
        sv, wts = saved[layer]
        dy, grads[layer] = _layer_backward(dy, sv, wts, small_all[layer], nb, s, f"l{layer}",
                                           schedule.backward_hooks(layer), schedule.ffn_grads_ready(layer))
        schedule.layer_done(layer, grads[layer])
    return lossp, dy.reshape(nb, s, d), grads


_BIG = ("w_in", "w_out", "w_gate", "w_up", "w_down")
_COL_SHARDED = ("w_in", "w_gate", "w_up")


class _Schedule:
    def __init__(self, shards, chip, core):
        self.chip, self.core, self.reduces = chip, core, [[] for _ in range(DEPTH)]
        first = _WeightPrefetch(["w_in"], shards, 0, chip)
        first.run("l0_w_in")
        rest = _WeightPrefetch(["w_out", "w_gate", "w_up", "w_down"], shards, 0, chip)
        ahead_a = _WeightPrefetch(["w_in", "w_out", "w_down"], shards, 1, chip)
        ahead_b = _WeightPrefetch(["w_gate", "w_up"], shards, 1, chip)
        self.fetches = [[first, rest], [ahead_a, ahead_b]]
        self.forward, self.backward = [{} for _ in range(DEPTH)], [{} for _ in range(DEPTH)]
        rest.ride(self.forward[0], "sb_fwd", "fox_fwd")
        ahead_a.ride(self.forward[0], "dil_fwd", "ffn_in")
        ahead_b.ride(self.forward[0], "fox_fwd", "ffn_in")

    def weights(self, layer):
        def gathered(name):
            return next(f.result[name] for f in self.fetches[layer] if name in f.names)
        return _LayerWeights(gathered)

    def forward_hooks(self, layer):
        return self.forward[layer]

    def backward_hooks(self, layer):
        return self.backward[layer]

    def _reduce(self, layer, grads, tag):
        red = _GradReduce({name: _by_chip(name, g) for name, g in grads.items()}, self.chip, self.core, tag)
        self.reduces[layer].append(red)
        return red

    def ffn_grads_ready(self, layer):
        if layer != 0:
            return None

        def ready(early):
            self._reduce(0, early, "l0_early").ride(self.backward[0], "sb_bwd", "dil_bwd", "fox_bwd")

        return ready

    def layer_done(self, layer, grads):
        if layer == 1:
            self._reduce(1, {name: grads[name] for name in _BIG}, "l1").ride(self.backward[0], "ffn_out_dx", "sb_bwd", "fox_bwd")
        else:
            self._reduce(0, dict(w_in=grads["w_in"]), "l0_w_in").run()

    def reduced(self, layer, name):
        return next(r.result[name] for r in self.reduces[layer] if name in r.names)


def _by_chip(name, g):
    if name in _COL_SHARDED:
        return jnp.swapaxes(g.reshape(g.shape[0], N_CHIPS, -1), 0, 1)
    return g.reshape(N_CHIPS, -1, g.shape[1])


_SMALL_LAYOUT = (("ln1_g", 0), ("ln1_b", 2), ("ln2_g", 4), ("ln2_b", 6), ("conv_w", 8))
_ROW_MISC = 10
_ROW_LOSS = 11


def _pack_small(per_layer, rel_bias, loss=None):
    pk = jnp.zeros((SMALL_ROWS, D_MODEL), F32)
    for name, row in _SMALL_LAYOUT:
        for l in range(DEPTH):
            v = per_layer[l][name].reshape(-1)
            pk = pk.at[row + l, :v.shape[0]].set(v)
    fb = jnp.concatenate([per_layer[l]["f_bias"] for l in range(DEPTH)])
    pk = pk.at[_ROW_MISC, :2 * N_HEADS].set(fb)
    pk = pk.at[_ROW_MISC, BLK:BLK + REL_BUCKETS * N_HEADS].set(rel_bias.reshape(-1))
    if loss is not None:
        pk = pk.at[_ROW_LOSS, 0].set(loss)
    return pk


def _unpack_small(pk, conv_cols):
    out = {}
    for name, row in _SMALL_LAYOUT:
        n = 3 * conv_cols if name == "conv_w" else D_MODEL
        v = pk[row:row + DEPTH, :n]
        out[name] = v.reshape(DEPTH, 3, conv_cols) if name == "conv_w" else v
    out["f_bias"] = pk[_ROW_MISC, :2 * N_HEADS].reshape(DEPTH, N_HEADS)
    out["rel_bias"] = pk[_ROW_MISC, BLK:BLK + REL_BUCKETS * N_HEADS].reshape(REL_BUCKETS, N_HEADS)
    return out


_WEIGHTS = ("w_in", "f_bias", "conv_w", "w_out", "rel_bias", "ln1_g", "ln1_b", "w_gate", "w_up", "w_down", "ln2_g", "ln2_b")


def kernel(x, w_in, f_bias, conv_w, w_out, rel_bias, ln1_g, ln1_b, w_gate, w_up, w_down, ln2_g, ln2_b, loss_target, m_w_in, m_f_bias, m_conv_w, m_w_out, m_rel_bias, m_ln1_g, m_ln1_b, m_w_gate, m_w_up, m_w_down, m_ln2_g, m_ln2_b, v_w_in, v_f_bias, v_conv_w, v_w_out, v_rel_bias, v_ln1_g, v_ln1_b, v_w_gate, v_w_up, v_w_down, v_ln2_g, v_ln2_b):
    w = dict(w_in=w_in, f_bias=f_bias, conv_w=conv_w, w_out=w_out, rel_bias=rel_bias, ln1_g=ln1_g, ln1_b=ln1_b,
             w_gate=w_gate, w_up=w_up, w_down=w_down, ln2_g=ln2_g, ln2_b=ln2_b)
    m = dict(w_in=m_w_in, f_bias=m_f_bias, conv_w=m_conv_w, w_out=m_w_out, rel_bias=m_rel_bias, ln1_g=m_ln1_g,
             ln1_b=m_ln1_b, w_gate=m_w_gate, w_up=m_w_up, w_down=m_w_down, ln2_g=m_ln2_g, ln2_b=m_ln2_b)
    v = dict(w_in=v_w_in, f_bias=v_f_bias, conv_w=v_conv_w, w_out=v_w_out, rel_bias=v_rel_bias, ln1_g=v_ln1_g,
             ln1_b=v_ln1_b, w_gate=v_w_gate, w_up=v_w_up, w_down=v_w_down, ln2_g=v_ln2_g, ln2_b=v_ln2_b)
    chip = 2 * lax.axis_index("x") + lax.axis_index("y")
    core = lax.axis_index("c")
    conv_shard = CONV_W // N_CHIPS

    schedule = _Schedule({name: w[name].astype(MXU_DTYPE) for name in _BIG}, chip, core)
    cw_pk = jnp.zeros((8, D_MODEL), F32).at[0, :DEPTH * 3 * conv_shard].set(conv_w.reshape(-1))
    cw_all, _ = _gather_small(cw_pk, "gather_conv_w")
    cw_chips = cw_all[0::2, 0, :DEPTH * 3 * conv_shard].reshape(N_CHIPS, DEPTH, 3, conv_shard)
    conv_full = jnp.moveaxis(cw_chips, 0, 2).reshape(DEPTH, 3, CONV_W)
    small_all = [dict(f_bias=f_bias[l], conv_w=conv_full[l], rel_bias=rel_bias, ln1_g=ln1_g[l], ln1_b=ln1_b[l],
                      ln2_g=ln2_g[l], ln2_b=ln2_b[l]) for l in range(DEPTH)]

    lossp, grad_x, grads = _local_step(x, loss_target, schedule.weights, small_all, schedule)
    big_g = {name: jnp.stack([schedule.reduced(l, name) for l in range(DEPTH)]) for name in _BIG}

    drel = grads[0]["rel_bias"] + grads[1]["rel_bias"]
    small_pk = _pack_small(grads, drel, lossp[0, 0])
    _, small_sum = _gather_small(small_pk, "gather_small_grads")
    loss = small_sum[_ROW_LOSS, 0]
    small_g = _unpack_small(small_sum, CONV_W)
    small_g["conv_w"] = lax.dynamic_slice_in_dim(small_g["conv_w"], chip * conv_shard, conv_shard, axis=2)

    out_g, out_d, out_m, out_v = dict(small_g), {}, {}, {}
    for name in _BIG:
        out_g[name] = big_g[name]
        out_d[name], out_m[name], out_v[name] = _adamw(w[name], big_g[name], m[name], v[name], f"adamw_{name}")
    as_3d = lambda t: t if t.ndim == 3 else t[None]
    for name in _WEIGHTS:
        if name not in _BIG:
            stepped = _adamw(as_3d(w[name]), as_3d(small_g[name]), as_3d(m[name]), as_3d(v[name]), f"adamw_{name}")
            out_d[name], out_m[name], out_v[name] = (t.reshape(w[name].shape) for t in stepped)

    return (loss, grad_x, *[out_g[n] for n in _WEIGHTS], *[out_d[n] for n in _WEIGHTS],
            *[out_m[n] for n in _WEIGHTS], *[out_v[n] for n in _WEIGHTS])
```

```python
import functools
import math

import numpy as np
import jax
import jax.numpy as jnp
from jax import lax
from jax.experimental import pallas as pl
from jax.experimental.pallas import tpu as pltpu

F32 = jnp.float32
BF16 = jnp.bfloat16
MXU_DTYPE = BF16

D_MODEL = 1024
HEAD_DIM = 64
N_HEADS = 4
BLK = 128
ATT = 256
QT = 512
CONV_W = 256
PROJ = 3076
PROJ_PAD = 3200
D_FF = 2816
DEPTH = 2
ALPHA = (2 * DEPTH) ** 0.25
LN_EPS = 1e-5
NEG = -1e30
DIL_PATTERNS = ((128, 1), (512, 4), (2048, 16))
REL_BUCKETS = 32
N_CHIPS = 4
N_DEV = 8
SMALL_ROWS = 16

ADAM_LR = 0.001
ADAM_B1 = 0.9
ADAM_B2 = 0.999
ADAM_EPS = 1e-08
ADAM_WD = 0.01
ADAM_STEP = 10

VMEM_LIMIT = 56 * 2 ** 20
MESH = pl.DeviceIdType.MESH


def _cparams(*sem):
    return pltpu.CompilerParams(dimension_semantics=tuple(sem), vmem_limit_bytes=VMEM_LIMIT)


def _dot(a, b):
    return jnp.dot(a.astype(MXU_DTYPE), b.astype(MXU_DTYPE), preferred_element_type=F32)


def _dot_nt(a, b):
    return lax.dot_general(a.astype(MXU_DTYPE), b.astype(MXU_DTYPE), (((1,), (1,)), ((), ())),
                           preferred_element_type=F32)


def _dot_tn(a, b):
    return lax.dot_general(a.astype(MXU_DTYPE), b.astype(MXU_DTYPE), (((0,), (0,)), ((), ())),
                           preferred_element_type=F32)


def _split_dot(x, ones, passes):
    acc, rest = None, x
    for p in range(passes):
        piece = rest.astype(MXU_DTYPE)
        part = jnp.dot(piece, ones, preferred_element_type=F32)
        acc = part if acc is None else acc + part
        if p + 1 < passes:
            rest = rest - piece.astype(F32)
    return acc


def _split_dot_lhs(ones, x, passes):
    acc, rest = None, x
    for p in range(passes):
        piece = rest.astype(MXU_DTYPE)
        part = jnp.dot(ones, piece, preferred_element_type=F32)
        acc = part if acc is None else acc + part
        if p + 1 < passes:
            rest = rest - piece.astype(F32)
    return acc


def _iota2(shape, axis):
    return lax.broadcasted_iota(jnp.int32, shape, axis)


_TILES = {"proj": (2048, 640, 1024), "ffn_out_dw": (1408, 1024, 2048),
          "ffn_in_dw": (1024, 1408, 2048), "out_proj_dx": (1024, 1024, 1024),
          "out_proj_dw": (1024, 1024, 2048), "proj_dx": (1024, 512, 3200), "proj_dw": (1024, 640, 2048)}


def _matmul(a, b, kind, tag, *, out_dtype=F32, add=None, add_scale=1.0, trans_a=False, trans_b=False):
    k, m = a.shape if trans_a else a.shape[::-1]
    n = b.shape[0] if trans_b else b.shape[1]
    tm, tn, tk = _TILES[kind]
    tm, tk, name = min(tm, m), min(tk, k), f"{kind}_{tag}"
    assert m % tm == 0 and n % tn == 0 and k % tk == 0, (a.shape, b.shape, tm, tn, tk)
    nk = k // tk

    def body(*refs):
        if add is None:
            a_ref, b_ref, o_ref = refs[:3]
            c_ref, scr = None, refs[3:]
        else:
            a_ref, b_ref, c_ref, o_ref = refs[:4]
            scr = refs[4:]
        dot = _dot_tn if trans_a else _dot_nt if trans_b else _dot
        part = dot(a_ref[...], b_ref[...])

        def finish(acc):
            if c_ref is not None:
                acc = acc + add_scale * c_ref[...]
            o_ref[...] = acc.astype(out_dtype)

        if nk == 1:
            finish(part)
        else:
            acc_ref = scr[0]
            kk = pl.program_id(2)

            @pl.when(kk == 0)
            def _():
                acc_ref[...] = part

            @pl.when(kk > 0)
            def _():
                acc_ref[...] += part

            @pl.when(kk == nk - 1)
            def _():
                finish(acc_ref[...])

    b_spec = pl.BlockSpec((tn, tk), lambda i, j, kk: (j, kk)) if trans_b else pl.BlockSpec((tk, tn), lambda i, j, kk: (kk, j))
    a_spec = pl.BlockSpec((tk, tm), lambda i, j, kk: (kk, i)) if trans_a else pl.BlockSpec((tm, tk), lambda i, j, kk: (i, kk))
    in_specs = [a_spec, b_spec]
    operands = [a, b]
    if add is not None:
        in_specs.append(pl.BlockSpec((tm, tn), lambda i, j, kk: (i, j)))
        operands.append(add)
    return pl.pallas_call(
        body, name=name, grid=(m // tm, n // tn, nk), in_specs=in_specs,
        out_specs=pl.BlockSpec((tm, tn), lambda i, j, kk: (i, j)),
        out_shape=jax.ShapeDtypeStruct((m, n), out_dtype),
        scratch_shapes=[pltpu.VMEM((tm, tn), F32)] if nk > 1 else [],
        compiler_params=_cparams("parallel", "parallel", "arbitrary"),
    )(*operands)


def _matmul_post_norm(a, b, xin, g, beta, name):
    t, k = a.shape
    d = b.shape[1]
    tm = 512

    def body(a_ref, b_ref, x_ref, g_ref, beta_ref, pre_ref, y_ref, yb_ref):
        pre = ALPHA * x_ref[...] + _dot(a_ref[...], b_ref[...])
        xhat, _ = _ln_stats(pre)
        y = xhat * g_ref[...] + beta_ref[...]
        pre_ref[...] = pre
        y_ref[...] = y
        yb_ref[...] = y.astype(yb_ref.dtype)

    row = pl.BlockSpec((tm, d), lambda i: (i, 0))
    vec = pl.BlockSpec((1, d), lambda i: (0, 0))
    return pl.pallas_call(
        body, name=name, grid=(t // tm,),
        in_specs=[pl.BlockSpec((tm, k), lambda i: (i, 0)), pl.BlockSpec((k, d), lambda i: (0, 0)), row, vec, vec],
        out_specs=[row, row, row],
        out_shape=[jax.ShapeDtypeStruct((t, d), F32)] * 2 + [jax.ShapeDtypeStruct((t, d), MXU_DTYPE)],
        compiler_params=_cparams("parallel"),
    )(a, b, xin, g.reshape(1, d), beta.reshape(1, d))


def _ffn_in(x1, w_gu, name, carry=None):
    t, d = x1.shape
    tm, tn = 512, D_FF // 2
    nj = D_FF // tn

    def body(x_ref, wg_ref, wu_ref, gate_ref, up_ref, h_ref):
        xb = x_ref[...].astype(MXU_DTYPE)
        gate = _dot(xb, wg_ref[...])
        up = _dot(xb, wu_ref[...])
        gate_ref[...] = gate
        up_ref[...] = up
        h_ref[...] = (gate * (1.0 / (1.0 + jnp.exp(-gate))) * up).astype(h_ref.dtype)

    out = pl.BlockSpec((tm, tn), lambda i, j: (i, j))
    return _host_call(
        body, carry, name=name, grid=(t // tm, nj),
        in_specs=[pl.BlockSpec((tm, d), lambda i, j: (i, 0)), pl.BlockSpec((d, tn), lambda i, j: (0, j)),
                  pl.BlockSpec((d, tn), lambda i, j: (0, nj + j))],
        out_specs=[out, out, out],
        out_shape=[jax.ShapeDtypeStruct((t, D_FF), F32)] * 2 + [jax.ShapeDtypeStruct((t, D_FF), MXU_DTYPE)],
        operands=(x1, w_gu, w_gu))


def _ffn_out_dx(dy, w_down, gate, up, name, carry=None):
    t, d = dy.shape
    tm, tn = 512, D_FF // 2

    def body(dy_ref, w_ref, gate_ref, up_ref, dg_ref, du_ref):
        dh = _dot_nt(dy_ref[...], w_ref[...])
        gate = gate_ref[...]
        sig = 1.0 / (1.0 + jnp.exp(-gate))
        dg_ref[...] = (dh * up_ref[...] * sig * (1.0 + gate * (1.0 - sig))).astype(dg_ref.dtype)
        du_ref[...] = (dh * gate * sig).astype(du_ref.dtype)

    tile = pl.BlockSpec((tm, tn), lambda i, j: (i, j))
    return _host_call(
        body, carry, name=name, grid=(t // tm, D_FF // tn),
        in_specs=[pl.BlockSpec((tm, d), lambda i, j: (i, 0)), pl.BlockSpec((tn, d), lambda i, j: (j, 0)), tile, tile],
        out_specs=[tile, tile], out_shape=[jax.ShapeDtypeStruct((t, D_FF), MXU_DTYPE)] * 2,
        operands=(dy, w_down, gate, up))


def _ffn_in_dx(dgate, dup, w_gu, add, name):
    t = dgate.shape[0]
    d = w_gu.shape[0]
    tm, tk = 1024, D_FF // 2
    nk = D_FF // tk

    def body(dg_ref, du_ref, wg_ref, wu_ref, add_ref, o_ref, acc_ref):
        kk = pl.program_id(1)
        part = _dot_nt(dg_ref[...], wg_ref[...]) + _dot_nt(du_ref[...], wu_ref[...])

        @pl.when(kk == 0)
        def _():
            acc_ref[...] = part

        @pl.when(kk > 0)
        def _():
            acc_ref[...] += part

        @pl.when(kk == nk - 1)
        def _():
            o_ref[...] = acc_ref[...] + ALPHA * add_ref[...]

    act = pl.BlockSpec((tm, tk), lambda i, kk: (i, kk))
    row = pl.BlockSpec((tm, d), lambda i, kk: (i, 0))
    return pl.pallas_call(
        body, name=name, grid=(t // tm, nk),
        in_specs=[act, act, pl.BlockSpec((d, tk), lambda i, kk: (0, kk)), pl.BlockSpec((d, tk), lambda i, kk: (0, nk + kk)), row],
        out_specs=row, out_shape=jax.ShapeDtypeStruct((t, d), F32), scratch_shapes=[pltpu.VMEM((tm, d), F32)],
        compiler_params=_cparams("parallel", "arbitrary"),
    )(dgate, dup, w_gu, w_gu, add)


def _ln_stats(pre):
    mu = jnp.mean(pre, axis=-1, keepdims=True)
    xc = pre - mu
    var = jnp.mean(xc * xc, axis=-1, keepdims=True)
    rstd = lax.rsqrt(var + LN_EPS)
    return xc * rstd, rstd


def _ln_bwd(dy, pre, g, name):
    t, d = dy.shape
    tile = 256

    def body(dy_ref, pre_ref, g_ref, dpre_ref, dpre_b_ref, dgb_ref):
        dyv = dy_ref[...]
        xhat, rstd = _ln_stats(pre_ref[...])
        dxh = dyv * g_ref[...]
        m1 = jnp.mean(dxh, axis=-1, keepdims=True)
        m2 = jnp.mean(dxh * xhat, axis=-1, keepdims=True)
        dpre = rstd * (dxh - m1 - xhat * m2)
        dpre_ref[...] = dpre
        dpre_b_ref[...] = dpre.astype(dpre_b_ref.dtype)

        @pl.when(pl.program_id(0) == 0)
        def _():
            dgb_ref[...] = jnp.zeros_like(dgb_ref)

        dgb_ref[0:1, :] += jnp.sum(dyv * xhat, axis=0, keepdims=True)
        dgb_ref[1:2, :] += jnp.sum(dyv, axis=0, keepdims=True)

    row = pl.BlockSpec((tile, d), lambda i: (i, 0))
    return pl.pallas_call(
        body, name=name, grid=(t // tile,), in_specs=[row, row, pl.BlockSpec((1, d), lambda i: (0, 0))],
        out_specs=[row, row, pl.BlockSpec((8, d), lambda i: (0, 0))],
        out_shape=[jax.ShapeDtypeStruct((t, d), F32), jax.ShapeDtypeStruct((t, d), MXU_DTYPE), jax.ShapeDtypeStruct((8, d), F32)],
        compiler_params=_cparams("arbitrary"),
    )(dy, pre, g.reshape(1, d))


def _loss_kernel(y, target, name):
    t, d = y.shape
    tile = 512

    def body(y_ref, t_ref, dy_ref, l_ref):
        err = y_ref[...] - t_ref[...]
        dy_ref[...] = err * (1.0 / d)

        @pl.when(pl.program_id(0) == 0)
        def _():
            l_ref[...] = jnp.zeros_like(l_ref)

        l_ref[...] += jnp.sum(err * err) * (0.5 / d)

    row = pl.BlockSpec((tile, d), lambda i: (i, 0))
    return pl.pallas_call(
        body, name=name, grid=(t // tile,), in_specs=[row, row],
        out_specs=[row, pl.BlockSpec((8, 128), lambda i: (0, 0))],
        out_shape=[jax.ShapeDtypeStruct((t, d), F32), jax.ShapeDtypeStruct((8, 128), F32)],
        compiler_params=_cparams("arbitrary"),
    )(y, target)


def _adamw(w, g, m, v, name):
    nl, r, c = w.shape
    tr = r
    for cand in (256, 352, 128, 64, 16, 8):
        if r % cand == 0:
            tr = cand
            break

    def body(w_ref, g_ref, m_ref, v_ref, d_ref, nm_ref, nv_ref):
        gv = g_ref[...]
        nm = ADAM_B1 * m_ref[...] + (1.0 - ADAM_B1) * gv
        nv = ADAM_B2 * v_ref[...] + (1.0 - ADAM_B2) * (gv * gv)
        m_hat = nm / (1.0 - ADAM_B1 ** ADAM_STEP)
        v_hat = nv / (1.0 - ADAM_B2 ** ADAM_STEP)
        d_ref[...] = -ADAM_LR * (m_hat / (jnp.sqrt(v_hat) + ADAM_EPS) + ADAM_WD * w_ref[...])
        nm_ref[...] = nm
        nv_ref[...] = nv

    blk = pl.BlockSpec((1, tr, c), lambda l, i: (l, i, 0))
    return pl.pallas_call(
        body, name=name, grid=(nl, r // tr), in_specs=[blk] * 4, out_specs=[blk] * 3,
        out_shape=[jax.ShapeDtypeStruct(w.shape, F32)] * 3, compiler_params=_cparams("parallel", "parallel"),
    )(w, g, m, v)


def _shift_down(u, k, rows):
    return jnp.where(rows >= k, pltpu.roll(u, k, 0), 0.0)


def _shift_up(u, k, rows, s):
    return jnp.where(rows < s - k, pltpu.roll(u, s - k, 0), 0.0)


def _conv_fwd(proj, conv_w, nb, s, name):
    def body(b_ref, c_ref, h_ref, w_ref, o_ref):
        rows = _iota2((s, CONV_W), 0)
        u = c_ref[...] * h_ref[...]
        y = w_ref[2:3, :] * u + w_ref[1:2, :] * _shift_down(u, 1, rows) + w_ref[0:1, :] * _shift_down(u, 2, rows)
        o_ref[...] = b_ref[...] * y

    col = lambda j: pl.BlockSpec((s, CONV_W), lambda b: (b, j))
    return pl.pallas_call(
        body, name=name, grid=(nb,),
        in_specs=[col(9), col(10), col(11), pl.BlockSpec((8, CONV_W), lambda b: (0, 0))],
        out_specs=pl.BlockSpec((s, CONV_W), lambda b: (b, 0)),
        out_shape=jax.ShapeDtypeStruct((nb * s, CONV_W), F32), compiler_params=_cparams("parallel"),
    )(proj, proj, proj, conv_w)


def _conv_bwd(dmixed, proj, conv_w, nb, s, name):
    def body(do_ref, b_ref, c_ref, h_ref, w_ref, dg_ref, dw_ref):
        rows = _iota2((s, CONV_W), 0)
        cg, hg, bg, dout = c_ref[...], h_ref[...], b_ref[...], do_ref[...]
        u = cg * hg
        u1 = _shift_down(u, 1, rows)
        u2 = _shift_down(u, 2, rows)
        y = w_ref[2:3, :] * u + w_ref[1:2, :] * u1 + w_ref[0:1, :] * u2
        dy = dout * bg
        du = w_ref[2:3, :] * dy + w_ref[1:2, :] * _shift_up(dy, 1, rows, s) + w_ref[0:1, :] * _shift_up(dy, 2, rows, s)
        dg_ref[:, 0:CONV_W] = dout * y
        dg_ref[:, CONV_W:2 * CONV_W] = du * hg
        dg_ref[:, 2 * CONV_W:3 * CONV_W] = du * cg

        @pl.when(pl.program_id(0) == 0)
        def _():
            dw_ref[...] = jnp.zeros_like(dw_ref)

        dw_ref[0:1, :] += jnp.sum(dy * u2, axis=0, keepdims=True)
        dw_ref[1:2, :] += jnp.sum(dy * u1, axis=0, keepdims=True)
        dw_ref[2:3, :] += jnp.sum(dy * u, axis=0, keepdims=True)

    col = lambda j: pl.BlockSpec((s, CONV_W), lambda b: (b, j))
    return pl.pallas_call(
        body, name=name, grid=(nb,),
        in_specs=[col(3), col(9), col(10), col(11), pl.BlockSpec((8, CONV_W), lambda b: (0, 0))],
        out_specs=[pl.BlockSpec((s, 3 * CONV_W), lambda b: (b, 0)), pl.BlockSpec((8, CONV_W), lambda b: (0, 0))],
        out_shape=[jax.ShapeDtypeStruct((nb * s, 3 * CONV_W), F32), jax.ShapeDtypeStruct((8, CONV_W), F32)],
        compiler_params=_cparams("arbitrary"),
    )(dmixed, proj, proj, proj, conv_w)


def _col_spec(s, base):
    return pl.BlockSpec((s, BLK), lambda b, p: (b, base + p))


def _qrows(i):
    return pl.ds(pl.multiple_of(i * QT, QT), QT)


def _rows(j):
    return pl.ds(pl.multiple_of(j * ATT, ATT), ATT)


def _keys_upto(i):
    return (i + 1) * (QT // ATT)


def _triangle(keep):
    return keep(_iota2((ATT, ATT), 0), _iota2((ATT, ATT), 1)).astype(MXU_DTYPE)


def _rows128(i):
    return pl.ds(pl.multiple_of(i * BLK, BLK), BLK)


def _log_sigmoid_parts(z):
    e = jnp.exp(-jnp.abs(z))
    l1p = jnp.log(1.0 + e)
    lb = jnp.minimum(z, 0.0) - l1p
    return lb, lb - z, e


MIXER_W = N_HEADS * HEAD_DIM


def _mixer_spec(s, block):
    return pl.BlockSpec((s, MIXER_W), lambda b: (b, block))


def _heads_spec(s, width):
    return pl.BlockSpec((None, N_HEADS, s, width), lambda b: (b, 0, 0, 0))


def _head_masks(heads=2):
    lane = _iota2((1, heads * HEAD_DIM), 1)
    return [(lane >= h * HEAD_DIM) & (lane < (h + 1) * HEAD_DIM) for h in range(heads)]


def _split_heads(ref, scr, sels):
    for h, sel in enumerate(sels):
        scr[h] = jnp.where(sel, ref[...], 0.0).astype(MXU_DTYPE)


def _sb_fwd(proj, nb, s, name, carry=None):
    def body(q_ref, k_ref, v_ref, o_ref, tails_ref, km, vm):
        sels = _head_masks(N_HEADS)
        _split_heads(k_ref, km, sels)
        _split_heads(v_ref, vm, sels)
        rows = _iota2((QT, ATT), 0)
        cols = _iota2((QT, ATT), 1)
        lane = _iota2((QT, BLK), 1)
        later = _triangle(lambda r, c: r > c)
        tails_ref[...] = jnp.zeros_like(tails_ref)

        def qblock(i, _):
            qi = (q_ref[_qrows(i), :] * 0.125).astype(MXU_DTYPE)

            def kblock(t, state):
                carries, acc = state
                j = _keys_upto(i) - 1 - t
                strict = (cols + (j * ATT - i * QT)) < rows
                out = []
                for h in range(N_HEADS):
                    tails_ref[h, _qrows(i), :] = jnp.where(lane == j, carries[h], tails_ref[h, _qrows(i), :])
                    z = _dot_nt(qi, km[h, _rows(j), :])
                    lb, lr, _ = _log_sigmoid_parts(z)
                    lr = jnp.where(strict, lr, 0.0)
                    tail = _split_dot(lr, later, 2) + carries[h]
                    a = jnp.where(strict, jnp.exp(lb + tail), 0.0)
                    acc = acc + _dot(a, vm[h, _rows(j), :])
                    out.append(carries[h] + jnp.sum(lr, axis=-1, keepdims=True))
                return tuple(out), acc

            init = ((jnp.zeros((QT, 1), F32),) * N_HEADS, jnp.zeros((QT, MIXER_W), F32))
            _, acc = lax.fori_loop(0, _keys_upto(i), kblock, init)
            o_ref[_qrows(i), :] = acc
            return 0

        lax.fori_loop(0, s // QT, qblock, 0)

    return _host_call(
        body, carry, name=name, grid=(nb,), in_specs=[_mixer_spec(s, 0), _mixer_spec(s, 1), _mixer_spec(s, 2)],
        out_specs=[_mixer_spec(s, 0), _heads_spec(s, BLK)],
        out_shape=[jax.ShapeDtypeStruct((nb * s, MIXER_W), F32), jax.ShapeDtypeStruct((nb, N_HEADS, s, BLK), F32)],
        scratch_shapes=[pltpu.VMEM((N_HEADS, s, MIXER_W), MXU_DTYPE)] * 2, operands=(proj, proj, proj))


def _sb_bwd(proj, dmixed, tails, nb, s, name, carry=None):
    def body(q_ref, k_ref, v_ref, do_ref, tails_ref, dq_ref, dk_ref, dv_ref, km, vm):
        sels = _head_masks(N_HEADS)
        _split_heads(k_ref, km, sels)
        _split_heads(v_ref, vm, sels)
        rows = _iota2((QT, ATT), 0)
        cols = _iota2((QT, ATT), 1)
        lane = _iota2((QT, BLK), 1)
        later = _triangle(lambda r, c: r > c)
        earlier = _triangle(lambda r, c: r < c)
        dk_ref[...] = jnp.zeros_like(dk_ref)
        dv_ref[...] = jnp.zeros_like(dv_ref)

        def qblock(i, _):
            qi = (q_ref[_qrows(i), :] * 0.125).astype(MXU_DTYPE)
            doi = do_ref[_qrows(i), :].astype(MXU_DTYPE)
            qm = [jnp.where(sel, qi, 0.0) for sel in sels]
            dom = [jnp.where(sel, doi, 0.0) for sel in sels]
            tails_i = [tails_ref[h, _qrows(i), :] for h in range(N_HEADS)]

            def kblock(j, state):
                csums, dq = state
                strict = (cols + (j * ATT - i * QT)) < rows
                out = []
                for h in range(N_HEADS):
                    z = _dot_nt(qi, km[h, _rows(j), :])
                    lb, lr, _ = _log_sigmoid_parts(z)
                    lr = jnp.where(strict, lr, 0.0)
                    after = jnp.sum(jnp.where(lane == j, tails_i[h], 0.0), axis=-1, keepdims=True)
                    a = jnp.where(strict, jnp.exp(lb + _split_dot(lr, later, 2) + after), 0.0)
                    dl = a * _dot_nt(doi, vm[h, _rows(j), :])
                    beta = jnp.exp(lb)
                    before = _split_dot(dl, earlier, 2) + csums[h]
                    dz = jnp.where(strict, dl * (1.0 - beta) - beta * before, 0.0).astype(MXU_DTYPE)
                    dq = dq + _dot(dz, km[h, _rows(j), :])
                    dk_ref[_rows(j), :] += _dot_tn(dz, qm[h])
                    dv_ref[_rows(j), :] += _dot_tn(a, dom[h])
                    out.append(csums[h] + jnp.sum(dl, axis=-1, keepdims=True))
                return tuple(out), dq

            init = ((jnp.zeros((QT, 1), F32),) * N_HEADS, jnp.zeros((QT, MIXER_W), F32))
            _, dq = lax.fori_loop(0, _keys_upto(i), kblock, init)
            dq_ref[_qrows(i), :] = dq * 0.125
            return 0

        lax.fori_loop(0, s // QT, qblock, 0)

    out = _mixer_spec(s, 0)
    return _host_call(
        body, carry, name=name, grid=(nb,),
        in_specs=[_mixer_spec(s, 0), _mixer_spec(s, 1), _mixer_spec(s, 2), out, _heads_spec(s, BLK)], out_specs=[out] * 3,
        out_shape=[jax.ShapeDtypeStruct((nb * s, MIXER_W), F32)] * 3,
        scratch_shapes=[pltpu.VMEM((N_HEADS, s, MIXER_W), MXU_DTYPE)] * 2, operands=(proj, proj, proj, dmixed, tails))


def _pair_spec(s, width):
    return pl.BlockSpec((None, 2, s, width), lambda b, p: (b, p, 0, 0))


def _fox_fwd(proj, ccol, crow, nb, s, name, carry=None):
    nblk = s // ATT

    def body(q_ref, k_ref, v_ref, cc_ref, cr_ref, o_ref, lse_ref, km, vm):
        sels = _head_masks()
        _split_heads(k_ref, km, sels)
        _split_heads(v_ref, vm, sels)
        rows = _iota2((QT, ATT), 0)
        cols = _iota2((QT, ATT), 1)

        def qblock(i, _):
            qi = (q_ref[_qrows(i), :] * 0.125).astype(MXU_DTYPE)
            ci = [cc_ref[h, _qrows(i), :] for h in range(2)]

            def kblock(j, state):
                ms, ls, acc = state
                causal = (cols + (j * ATT - i * QT)) <= rows
                new_m, new_l, scales, parts = [], [], [], []
                for h in range(2):
                    z = _dot_nt(qi, km[h, _rows(j), :]) + (ci[h] - cr_ref[h, j][0:1, :])
                    z = jnp.where(causal, z, NEG)
                    m_new = jnp.maximum(ms[h], jnp.max(z, axis=-1, keepdims=True))
                    p = jnp.exp(z - m_new)
                    scale = jnp.exp(ms[h] - m_new)
                    new_m.append(m_new)
                    new_l.append(scale * ls[h] + jnp.sum(p, axis=-1, keepdims=True))
                    scales.append(scale)
                    parts.append(_dot(p, vm[h, _rows(j), :]))
                acc = jnp.where(sels[0], scales[0], scales[1]) * acc + parts[0] + parts[1]
                return tuple(new_m), tuple(new_l), acc

            init = ((jnp.full((QT, 1), NEG, F32),) * 2, (jnp.zeros((QT, 1), F32),) * 2, jnp.zeros((QT, BLK), F32))
            ms, ls, acc = lax.fori_loop(0, _keys_upto(i), kblock, init)
            o_ref[_qrows(i), :] = acc / jnp.where(sels[0], ls[0], ls[1])
            for h in range(2):
                lse_ref[h, _qrows(i), :] = jnp.broadcast_to(ms[h] + jnp.log(ls[h]), (QT, ATT))
            return 0

        lax.fori_loop(0, s // QT, qblock, 0)

    crow_spec = pl.BlockSpec((None, 2, nblk, 8, ATT), lambda b, p: (b, p, 0, 0, 0))
    return _host_call(
        body, carry, name=name, grid=(nb, 2),
        in_specs=[_col_spec(s, 12), _col_spec(s, 14), _col_spec(s, 16), _pair_spec(s, ATT), crow_spec],
        out_specs=[_col_spec(s, 0), _pair_spec(s, ATT)],
        out_shape=[jax.ShapeDtypeStruct((nb * s, 2 * BLK), F32), jax.ShapeDtypeStruct((nb, N_HEADS, s, ATT), F32)],
        scratch_shapes=[pltpu.VMEM((2, s, BLK), MXU_DTYPE)] * 2, operands=(proj, proj, proj, ccol, crow))


def _fox_bwd(proj, dmixed, lse, ccol, crow, nb, s, name, carry=None):
    nblk = s // ATT

    def body(q_ref, k_ref, v_ref, do_ref, lse_ref, cc_ref, cr_ref, dq_ref, dk_ref, dv_ref, dc_ref, km, vm, p_scr, dp_scr):
        sels = _head_masks()
        _split_heads(k_ref, km, sels)
        _split_heads(v_ref, vm, sels)
        rows = _iota2((QT, ATT), 0)
        cols = _iota2((QT, ATT), 1)
        dk_ref[...] = jnp.zeros_like(dk_ref)
        dv_ref[...] = jnp.zeros_like(dv_ref)
        dc_ref[...] = jnp.zeros_like(dc_ref)

        def qblock(i, _):
            qi = (q_ref[_qrows(i), :] * 0.125).astype(MXU_DTYPE)
            doi = do_ref[_qrows(i), :].astype(MXU_DTYPE)
            qm = [jnp.where(sel, qi, 0.0) for sel in sels]
            dom = [jnp.where(sel, doi, 0.0) for sel in sels]
            ci = [cc_ref[h, _qrows(i), :] for h in range(2)]
            lsei = [lse_ref[h, _qrows(i), :] for h in range(2)]

            def probs(j, h):
                z = _dot_nt(qi, km[h, _rows(j), :]) + (ci[h] - cr_ref[h, j][0:1, :])
                p = jnp.where((cols + (j * ATT - i * QT)) <= rows, jnp.exp(z - lsei[h]), 0.0)
                return p, _dot_nt(doi, vm[h, _rows(j), :])

            def row_term(j, accs):
                out = []
                for h in range(2):
                    p, dp = probs(j, h)
                    p_scr[h, j] = p
                    dp_scr[h, j] = dp
                    out.append(accs[h] + jnp.sum(p * dp, axis=-1, keepdims=True))
                return tuple(out)

            di = lax.fori_loop(0, _keys_upto(i), row_term, (jnp.zeros((QT, 1), F32),) * 2)

            def kblock(j, dq):
                for h in range(2):
                    p = p_scr[h, j]
                    ds = p * (dp_scr[h, j] - di[h])
                    dc_ref[h, j] += jnp.broadcast_to(jnp.sum(ds, axis=0, keepdims=True), (8, ATT))
                    ds = ds.astype(MXU_DTYPE)
                    dk_ref[_rows(j), :] += _dot_tn(ds, qm[h])
                    dv_ref[_rows(j), :] += _dot_tn(p, dom[h])
                    dq = dq + _dot(ds, km[h, _rows(j), :])
                return dq

            dq = lax.fori_loop(0, _keys_upto(i), kblock, jnp.zeros((QT, BLK), F32))
            dq_ref[_qrows(i), :] = dq * 0.125
            return 0

        lax.fori_loop(0, s // QT, qblock, 0)

    crow_spec = pl.BlockSpec((None, 2, nblk, 8, ATT), lambda b, p: (b, p, 0, 0, 0))
    wide, cols_out = _pair_spec(s, ATT), _col_spec(s, 0)
    return _host_call(
        body, carry, name=name, grid=(nb, 2),
        in_specs=[_col_spec(s, 12), _col_spec(s, 14), _col_spec(s, 16), _col_spec(s, 4), wide, wide, crow_spec],
        out_specs=[cols_out, cols_out, cols_out, crow_spec],
        out_shape=[jax.ShapeDtypeStruct((nb * s, 2 * BLK), F32)] * 3 + [jax.ShapeDtypeStruct((nb, N_HEADS, nblk, 8, ATT), F32)],
        scratch_shapes=[pltpu.VMEM((2, s, BLK), MXU_DTYPE)] * 2 + [pltpu.VMEM((2, nblk, QT, ATT), F32)] * 2,
        operands=(proj, proj, proj, dmixed, lse, ccol, crow))


def _fox_gates_fwd(proj, f_bias, nb, s, name):
    chunk = 256

    def body(f_ref, b_ref, c_ref):
        lower = (_iota2((chunk, chunk), 0) >= _iota2((chunk, chunk), 1)).astype(MXU_DTYPE)
        carry = jnp.zeros((1, BLK), F32)
        for n in range(s // chunk):
            rows = pl.ds(n * chunk, chunk)
            lf, _, _ = _log_sigmoid_parts(f_ref[rows, :] + b_ref[0:1, :])
            c = _split_dot_lhs(lower, lf, 3) + carry
            c_ref[rows, :] = c
            carry = c[chunk - 1:chunk, :]

    return pl.pallas_call(
        body, name=name, grid=(nb,),
        in_specs=[pl.BlockSpec((s, BLK), lambda b: (b, (PROJ_PAD - BLK) // BLK)), pl.BlockSpec((8, BLK), lambda b: (0, 0))],
        out_specs=pl.BlockSpec((s, BLK), lambda b: (b, 0)),
        out_shape=jax.ShapeDtypeStruct((nb * s, BLK), F32), compiler_params=_cparams("parallel"),
    )(proj, f_bias)


def _fox_gates_bwd(dc, proj, f_bias, nb, s, name):
    chunk = 256

    def body(dc_ref, f_ref, b_ref, df_ref, db_ref):
        upper = (_iota2((chunk, chunk), 0) <= _iota2((chunk, chunk), 1)).astype(MXU_DTYPE)
        carry = jnp.zeros((1, BLK), F32)
        total = jnp.zeros((1, BLK), F32)
        for n in reversed(range(s // chunk)):
            rows = pl.ds(n * chunk, chunk)
            dlf = _split_dot_lhs(upper, dc_ref[rows, :], 3) + carry
            carry = dlf[0:1, :]
            pre = f_ref[rows, :] + b_ref[0:1, :]
            e = jnp.exp(-jnp.abs(pre))
            df = dlf * (jnp.where(pre >= 0.0, e, 1.0) / (1.0 + e))
            df_ref[rows, :] = df
            total = total + jnp.sum(df, axis=0, keepdims=True)

        @pl.when(pl.program_id(0) == 0)
        def _():
            db_ref[...] = jnp.zeros_like(db_ref)

        db_ref[0:1, :] += total

    return pl.pallas_call(
        body, name=name, grid=(nb,),
        in_specs=[pl.BlockSpec((s, BLK), lambda b: (b, 0)), pl.BlockSpec((s, BLK), lambda b: (b, (PROJ_PAD - BLK) // BLK)),
                  pl.BlockSpec((8, BLK), lambda b: (0, 0))],
        out_specs=[pl.BlockSpec((s, BLK), lambda b: (b, 0)), pl.BlockSpec((8, BLK), lambda b: (0, 0))],
        out_shape=[jax.ShapeDtypeStruct((nb * s, BLK), F32), jax.ShapeDtypeStruct((8, BLK), F32)],
        compiler_params=_cparams("arbitrary"),
    )(dc, proj, f_bias)


def _delta_kernel(dmixed, o, nb, s, name):
    def body(do_ref, o_ref, d_ref):
        prod = do_ref[...] * o_ref[...]
        for h, sel in enumerate(_head_masks()):
            d_ref[h] = jnp.broadcast_to(jnp.sum(jnp.where(sel, prod, 0.0), axis=-1, keepdims=True), (s, BLK))

    return pl.pallas_call(
        body, name=name, grid=(nb, 2), in_specs=[_col_spec(s, 2), _col_spec(s, 0)], out_specs=_pair_spec(s, BLK),
        out_shape=jax.ShapeDtypeStruct((nb, N_HEADS, s, BLK), F32), compiler_params=_cparams("parallel", "parallel"),
    )(dmixed, o)


def _t5_bucket_np(dist):
    max_exact = REL_BUCKETS // 2
    nf = np.maximum(dist, 1).astype(np.float32)
    large = max_exact + (np.log(nf / max_exact) / math.log(2048 / max_exact) * (REL_BUCKETS - max_exact)).astype(np.int32)
    large = np.minimum(large, REL_BUCKETS - 1)
    return np.where(dist < max_exact, dist, large)


def _bucket_table():
    qi = np.arange(BLK)[:, None]
    kj = np.arange(2 * BLK)[None, :]
    dist = qi + BLK - kj
    tables = []
    for window, dil in DIL_PATTERNS:
        in_band = (dist >= 0) & (dist <= window // dil)
        tables.append(np.where(in_band, _t5_bucket_np(np.maximum(dist, 0) * dil), -1).astype(np.int32))
    return np.stack(tables)


def _dil_scores(qb, kp, kc, b_ref, h, prev_valid):
    zp = _dot_nt(qb, kp) + b_ref[h, :, 0:BLK]
    zp = jnp.where(prev_valid, zp, NEG)
    zc = _dot_nt(qb, kc) + b_ref[h, :, BLK:2 * BLK]
    return zp, zc


def _residue_rows(b, seg, dil):
    if dil == 1:
        return _rows128(b), _rows128(jnp.maximum(b - 1, 0)), b > 0
    r, n = b // seg, b % seg
    cur = pl.ds(r + dil * n * BLK, BLK, stride=dil)
    prev = pl.ds(r + dil * jnp.maximum(n - 1, 0) * BLK, BLK, stride=dil)
    return cur, prev, n > 0


def _dil_attention_fwd(proj, bias, nb, s, name, carry=None):
    nblk = s // BLK

    def body(q_ref, k_ref, v_ref, b_ref, out_ref, lse_ref, o_scr, l_scr):
        sels = _head_masks()
        for p, (_, dil) in enumerate(DIL_PATTERNS):
            seg = s // dil // BLK

            def block(b, _, p=p, seg=seg, dil=dil):
                cur, prev, has_prev = _residue_rows(b, seg, dil)
                qb = (q_ref[cur, :] * 0.125).astype(MXU_DTYPE)
                kp, kc = k_ref[prev, :].astype(MXU_DTYPE), k_ref[cur, :].astype(MXU_DTYPE)
                vp, vc = v_ref[prev, :].astype(MXU_DTYPE), v_ref[cur, :].astype(MXU_DTYPE)
                acc = jnp.zeros((BLK, BLK), F32)
                for h, sel in enumerate(sels):
                    zp, zc = _dil_scores(qb, jnp.where(sel, kp, 0.0), jnp.where(sel, kc, 0.0), b_ref.at[p], h, has_prev)
                    m = jnp.maximum(jnp.max(zp, axis=-1, keepdims=True), jnp.max(zc, axis=-1, keepdims=True))
                    pp = jnp.exp(zp - m)
                    pc = jnp.exp(zc - m)
                    den = jnp.sum(pp, axis=-1, keepdims=True) + jnp.sum(pc, axis=-1, keepdims=True)
                    acc = acc + (_dot(pp, jnp.where(sel, vp, 0.0)) + _dot(pc, jnp.where(sel, vc, 0.0))) / den
                    l_scr[p, h, cur, :] = jnp.broadcast_to(m + jnp.log(den), (BLK, BLK))
                o_scr[p, cur, :] = acc
                return 0

            lax.fori_loop(0, nblk, block, 0, unroll=8)

        weights, dens = [], []
        for h in range(2):
            m = jnp.maximum(jnp.maximum(l_scr[0, h], l_scr[1, h]), l_scr[2, h])
            w = [jnp.exp(l_scr[p, h] - m) for p in range(3)]
            den = w[0] + w[1] + w[2]
            lse_ref[h] = m + jnp.log(den)
            weights.append(w)
            dens.append(den)
        num = sum(jnp.where(sels[0], weights[0][p], weights[1][p]) * o_scr[p] for p in range(3))
        out_ref[...] = num / jnp.where(sels[0], dens[0], dens[1])

    bias_spec = pl.BlockSpec((3, 2, BLK, 2 * BLK), lambda b, p: (0, p, 0, 0))
    return _host_call(
        body, carry, name=name, grid=(nb, 2), in_specs=[_col_spec(s, 6), _col_spec(s, 8), _col_spec(s, 10), bias_spec],
        out_specs=[_col_spec(s, 0), _pair_spec(s, BLK)],
        out_shape=[jax.ShapeDtypeStruct((nb * s, 2 * BLK), F32), jax.ShapeDtypeStruct((nb, N_HEADS, s, BLK), F32)],
        scratch_shapes=[pltpu.VMEM((3, s, BLK), F32), pltpu.VMEM((3, 2, s, BLK), F32)], operands=(proj, proj, proj, bias))


def _dil_attention_bwd(proj, dmixed, lse, delta, bias, nb, s, name, carry=None):
    nblk = s // BLK

    def body(q_ref, k_ref, v_ref, do_ref, lse_ref, dl_ref, b_ref, dq_ref, dk_ref, dv_ref, g_ref):
        sels = _head_masks()
        dq_ref[...] = jnp.zeros_like(dq_ref)
        dk_ref[...] = jnp.zeros_like(dk_ref)
        dv_ref[...] = jnp.zeros_like(dv_ref)
        g_ref[...] = jnp.zeros_like(g_ref)
        for p, (_, dil) in enumerate(DIL_PATTERNS):
            seg = s // dil // BLK

            def block(b, _, p=p, seg=seg, dil=dil):
                cur, prev, has_prev = _residue_rows(b, seg, dil)
                qb = (q_ref[cur, :] * 0.125).astype(MXU_DTYPE)
                dob = do_ref[cur, :].astype(MXU_DTYPE)
                kp, kc = k_ref[prev, :].astype(MXU_DTYPE), k_ref[cur, :].astype(MXU_DTYPE)
                vp, vc = v_ref[prev, :].astype(MXU_DTYPE), v_ref[cur, :].astype(MXU_DTYPE)
                dq = jnp.zeros((BLK, BLK), F32)
                dkp, dkc, dvp, dvc = dq, dq, dq, dq
                for h, sel in enumerate(sels):
                    kph, kch = jnp.where(sel, kp, 0.0), jnp.where(sel, kc, 0.0)
                    qh, doh = jnp.where(sel, qb, 0.0), jnp.where(sel, dob, 0.0)
                    lse_h = lse_ref[h, cur, :]
                    dlt = dl_ref[h, cur, :]
                    zp, zc = _dil_scores(qb, kph, kch, b_ref.at[p], h, has_prev)
                    pp = jnp.exp(zp - lse_h)
                    pc = jnp.exp(zc - lse_h)
                    dsp = pp * (_dot_nt(dob, jnp.where(sel, vp, 0.0)) - dlt)
                    dsc = pc * (_dot_nt(dob, jnp.where(sel, vc, 0.0)) - dlt)
                    g_ref[h, p, :, 0:BLK] += dsp
                    g_ref[h, p, :, BLK:2 * BLK] += dsc
                    dsp = dsp.astype(MXU_DTYPE)
                    dsc = dsc.astype(MXU_DTYPE)
                    dq = dq + _dot(dsp, kph) + _dot(dsc, kch)
                    dkp, dkc = dkp + _dot_tn(dsp, qh), dkc + _dot_tn(dsc, qh)
                    dvp, dvc = dvp + _dot_tn(pp, doh), dvc + _dot_tn(pc, doh)
                dq_ref[cur, :] += dq * 0.125
                dk_ref[prev, :] += dkp
                dk_ref[cur, :] += dkc
                dv_ref[prev, :] += dvp
                dv_ref[cur, :] += dvc
                return 0

            lax.fori_loop(0, nblk, block, 0, unroll=4)

    bias_spec = pl.BlockSpec((3, 2, BLK, 2 * BLK), lambda b, p: (0, p, 0, 0))
    cols, stats = _col_spec(s, 0), _pair_spec(s, BLK)
    return _host_call(
        body, carry, name=name, grid=(nb, 2),
        in_specs=[_col_spec(s, 6), _col_spec(s, 8), _col_spec(s, 10), _col_spec(s, 2), stats, stats, bias_spec],
        out_specs=[cols, cols, cols, pl.BlockSpec((None, 2, 3, BLK, 2 * BLK), lambda b, p: (b, p, 0, 0, 0))],
        out_shape=[jax.ShapeDtypeStruct((nb * s, 2 * BLK), F32)] * 3 + [jax.ShapeDtypeStruct((nb, N_HEADS, 3, BLK, 2 * BLK), F32)],
        operands=(proj, proj, proj, dmixed, lse, delta, bias))


def _bucket_reduce(gbias, table, name):
    nb = gbias.shape[0]

    def body(g_ref, t_ref, o_ref):
        row = _iota2((8, BLK), 0)
        lane = _iota2((8, BLK), 1)
        gsum = [[sum(g_ref[b, h, p] for b in range(nb)) for p in range(3)] for h in range(N_HEADS)]

        def bucket(k, acc):
            for h in range(N_HEADS):
                tot = sum(jnp.sum(jnp.where(t_ref[p] == k, gsum[h][p], 0.0)) for p in range(3))
                acc = acc + jnp.where((row == h) & (lane == k), tot, 0.0)
            return acc

        o_ref[...] = lax.fori_loop(0, REL_BUCKETS, bucket, jnp.zeros((8, BLK), F32))

    vm = pl.BlockSpec(memory_space=pltpu.VMEM)
    return pl.pallas_call(
        body, name=name, in_specs=[vm, vm], out_specs=vm, out_shape=jax.ShapeDtypeStruct((8, BLK), F32),
        compiler_params=pltpu.CompilerParams(vmem_limit_bytes=VMEM_LIMIT),
    )(gbias, table)


def _place():
    x, y, c = lax.axis_index("x"), lax.axis_index("y"), lax.axis_index("c")
    others = [(1 - x, y), (x, 1 - y), (1 - x, 1 - y)]
    return x, y, c, others


def _remote(src, dst, send_sem, recv_sem, to):
    return pltpu.make_async_remote_copy(src_ref=src, dst_ref=dst, send_sem=send_sem, recv_sem=recv_sem,
                                        device_id=to, device_id_type=MESH)


_HBM = pl.BlockSpec(memory_space=pl.ANY)


class _Exchange:
    def __init__(self, operands, out_shape, n_copies, copies, aliases=None):
        self.operands, self.out_shape, self.n_copies, self.copies = list(operands), list(out_shape), n_copies, copies
        self.aliases = dict(aliases or {})

    def sem_shapes(self):
        return [pltpu.SemaphoreType.DMA((self.n_copies,)), pltpu.SemaphoreType.DMA((self.n_copies,))]


def _start_all(sends):
    for cp in sends:
        cp.start()


def _wait_all(sends, arrivals):
    for cp in arrivals:
        cp.wait_recv()
    for cp in sends:
        cp.wait_send()


def _run_exchange(ex, name):
    ni = len(ex.operands)

    def body(*refs):
        sends, arrivals = ex.copies(refs[:ni], refs[ni:-2], refs[-2], refs[-1])
        _start_all(sends)
        _wait_all(sends, arrivals)

    return list(pl.pallas_call(
        body, name=name, in_specs=[_HBM] * ni, out_specs=[_HBM] * len(ex.out_shape), out_shape=ex.out_shape,
        scratch_shapes=ex.sem_shapes(), input_output_aliases=ex.aliases)(*ex.operands))


def _host_call(body, carry, *, name, grid, in_specs, out_specs, out_shape, operands, scratch_shapes=()):
    in_specs, out_specs, out_shape, scratch_shapes = list(in_specs), list(out_specs), list(out_shape), list(scratch_shapes)
    if carry is None:
        res = pl.pallas_call(body, name=name, grid=grid, in_specs=in_specs, out_specs=out_specs, out_shape=out_shape,
                             scratch_shapes=scratch_shapes, compiler_params=_cparams(*["parallel"] * len(grid)))(*operands)
        return list(res), []
    n_in, n_out, n_scr, c_in, c_out = len(in_specs), len(out_specs), len(scratch_shapes), len(carry.operands), len(carry.out_shape)
    steps = math.prod(grid)

    def wrapped(*refs):
        ins, refs = refs[:n_in], refs[n_in:]
        c_ins, refs = refs[:c_in], refs[c_in:]
        outs, refs = refs[:n_out], refs[n_out:]
        c_outs, refs = refs[:c_out], refs[c_out:]
        scr, (send_sems, recv_sems) = refs[:n_scr], refs[n_scr:]
        step = 0
        for d, size in enumerate(grid):
            step = step * size + pl.program_id(d)

        @pl.when(step == 0)
        def _():
            _start_all(carry.copies(c_ins, c_outs, send_sems, recv_sems)[0])

        body(*ins, *outs, *scr)

        @pl.when(step == steps - 1)
        def _():
            _wait_all(*carry.copies(c_ins, c_outs, send_sems, recv_sems))

    res = pl.pallas_call(
        wrapped, name=name, grid=grid, in_specs=in_specs + [_HBM] * c_in, out_specs=out_specs + [_HBM] * c_out,
        out_shape=out_shape + carry.out_shape, scratch_shapes=scratch_shapes + carry.sem_shapes(),
        input_output_aliases={n_in + i: n_out + j for i, j in carry.aliases.items()},
        compiler_params=_cparams(*["arbitrary"] * len(grid)))(*operands, *carry.operands)
    return list(res[:n_out]), list(res[n_out:])


def _half(which, rows):
    h = rows // 2
    return pl.ds(pl.multiple_of(which * h, 16), h)


def _like(arrays, shape_of=lambda t: t.shape):
    return [jax.ShapeDtypeStruct(shape_of(t), t.dtype) for t in arrays]


def _gather_ici(shards, layer):
    n = len(shards)

    def copies(ins, outs, send_sems, recv_sems, base=0):
        x, y, c, others = _place()
        me = 2 * x + y
        sends, arrivals = [], []
        for a in range(n):
            rows = _half(c, shards[a].shape[1])
            for k, (ox, oy) in enumerate(others):
                sems = (send_sems.at[base + 3 * a + k], recv_sems.at[base + 3 * a + k],(ox, oy, c))
                sends.append(_remote(ins[a].at[layer, rows], outs[a].at[me, rows], *sems))
                landed = outs[a].at[2 * ox + oy, rows]
                arrivals.append(_remote(landed, landed, *sems))
        return sends, arrivals

    return _Exchange(shards, _like(shards, lambda t: (N_CHIPS,) + t.shape[1:]), 3 * n, copies)


def _gather_d2d(gathered):
    n = len(gathered)

    def copies(ins, outs, send_sems, recv_sems, base=0):
        x, y, c, others = _place()
        sends, arrivals = [], []
        for a in range(n):
            r = gathered[a].shape[1]
            for k, (ox, oy) in enumerate(others):
                sems = (send_sems.at[base + 3 * a + k], recv_sems.at[base + 3 * a + k],(x, y, 1 - c))
                mine, theirs = outs[a].at[2 * ox + oy, _half(c, r)], outs[a].at[2 * ox + oy, _half(1 - c, r)]
                sends.append(_remote(mine, mine, *sems))
                arrivals.append(_remote(theirs, theirs, *sems))
        return sends, arrivals

    return _Exchange(gathered, _like(gathered), 3 * n, copies, aliases={a: a for a in range(n)})


def _swap_halves(g):
    n = len(g)

    def copies(ins, outs, send_sems, recv_sems, base=0):
        x, y, c, _ = _place()
        sends, arrivals = [], []
        for a in range(n):
            sems = (send_sems.at[base + a], recv_sems.at[base + a], (x, y, 1 - c))
            sends.append(_remote(ins[a].at[:, _half(1 - c, g[a].shape[1])], outs[a], *sems))
            arrivals.append(_remote(outs[a], outs[a], *sems))
        return sends, arrivals

    return _Exchange(g, _like(g, lambda t: (t.shape[0], t.shape[1] // 2, t.shape[2])), n, copies)


def _scatter_shards(ps):
    n = len(ps)

    def copies(ins, outs, send_sems, recv_sems, base=0):
        x, y, c, others = _place()
        me = 2 * x + y
        sends, arrivals = [], []
        for a in range(n):
            for k, (ox, oy) in enumerate(others):
                sems = (send_sems.at[base + 3 * a + k], recv_sems.at[base + 3 * a + k],(ox, oy, c))
                sends.append(_remote(ins[a].at[2 * ox + oy], outs[a].at[me], *sems))
                slot = outs[a].at[2 * ox + oy]
                arrivals.append(_remote(slot, slot, *sems))
        return sends, arrivals

    return _Exchange(ps, _like(ps), 3 * n, copies)


def _share_halves(mine):
    n = len(mine)

    def copies(ins, outs, send_sems, recv_sems, base=0):
        x, y, c, _ = _place()
        sends, arrivals = [], []
        for a in range(n):
            sems = (send_sems.at[base + a], recv_sems.at[base + a], (x, y, 1 - c))
            sends.append(_remote(ins[a], outs[a], *sems))
            arrivals.append(_remote(outs[a], outs[a], *sems))
        return sends, arrivals

    return _Exchange(mine, _like(mine), n, copies)


def _row_tile(r):
    for cand in (256, 352, 128):
        if r % cand == 0:
            return cand
    return r


def _pair_sum(g, other, core, name):
    ns, h, w = other.shape
    tr = _row_tile(h)
    per_half = h // tr

    def body(core_ref, g_ref, o_ref, out_ref):
        out_ref[...] = (g_ref[...] + o_ref[...]).astype(out_ref.dtype)

    blk = pl.BlockSpec((None, tr, w), lambda k, i, core_ref: (k, i, 0))
    grid_spec = pltpu.PrefetchScalarGridSpec(
        num_scalar_prefetch=1, grid=(ns, per_half),
        in_specs=[pl.BlockSpec((None, tr, w), lambda k, i, core_ref: (k, core_ref[0] * per_half + i, 0)), blk], out_specs=blk)
    return pl.pallas_call(
        body, name=name, grid_spec=grid_spec, out_shape=jax.ShapeDtypeStruct((ns, h, w), MXU_DTYPE),
        compiler_params=_cparams("parallel", "parallel"),
    )(core.reshape(1).astype(jnp.int32), g, other)


def _chip_sum(q, p, chip, name):
    ns, r, w = q.shape
    tr = _row_tile(r)

    def body(chip_ref, q_ref, own_ref, out_ref):
        me = chip_ref[0]
        own = own_ref[...].astype(F32)
        terms = [jnp.where(me == k, own, q_ref[k].astype(F32)) for k in range(ns)]
        out_ref[...] = ((terms[0] + terms[1]) + terms[2]) + terms[3]

    grid_spec = pltpu.PrefetchScalarGridSpec(
        num_scalar_prefetch=1, grid=(r // tr,),
        in_specs=[pl.BlockSpec((ns, tr, w), lambda i, chip_ref: (0, i, 0)),
                  pl.BlockSpec((None, tr, w), lambda i, chip_ref: (chip_ref[0], i, 0))],
        out_specs=pl.BlockSpec((tr, w), lambda i, chip_ref: (i, 0)))
    return pl.pallas_call(
        body, name=name, grid_spec=grid_spec, out_shape=jax.ShapeDtypeStruct((r, w), F32),
        compiler_params=_cparams("parallel"),
    )(chip.reshape(1).astype(jnp.int32), q, p)


def _merge(exchanges):
    if len(exchanges) <= 1:
        return exchanges[0] if exchanges else None
    operands, out_shape, aliases, spans, n = [], [], {}, [], 0
    for ex in exchanges:
        spans.append((len(operands), len(out_shape), n))
        aliases.update({len(operands) + i: len(out_shape) + j for i, j in ex.aliases.items()})
        operands += ex.operands
        out_shape += ex.out_shape
        n += ex.n_copies

    def copies(ins, outs, send_sems, recv_sems, base=0):
        sends, arrivals = [], []
        for ex, (i0, o0, s0) in zip(exchanges, spans):
            s, a = ex.copies(ins[i0:i0 + len(ex.operands)], outs[o0:o0 + len(ex.out_shape)], send_sems, recv_sems, base + s0)
            sends += s
            arrivals += a
        return sends, arrivals

    return _Exchange(operands, out_shape, n, copies, aliases)


def _take(hooks, host):
    stages = (hooks or {}).pop(host, [])
    exchanges = [make() for make, _ in stages]

    def finish(results):
        for (_, done), ex in zip(stages, exchanges):
            done(results[:len(ex.out_shape)])
            results = results[len(ex.out_shape):]

    return _merge(exchanges), finish


def _hook(hooks, host, make, done):
    hooks.setdefault(host, []).append((make, done))


class _WeightPrefetch:
    def __init__(self, names, shards, layer, chip):
        self.names, self.shards, self.layer, self.chip, self.result = names, [shards[n] for n in names], layer, chip, None

    def first(self):
        return _gather_ici(self.shards, self.layer)

    def got_first(self, arrived):
        self.arrived = arrived

    def second(self):
        return _gather_d2d(self.arrived)

    def got_second(self, gathered):
        self.result = {name: lax.dynamic_update_index_in_dim(got, own[self.layer], self.chip, 0)
                       for name, got, own in zip(self.names, gathered, self.shards)}

    def ride(self, hooks, first_host, second_host):
        _hook(hooks, first_host, self.first, self.got_first)
        _hook(hooks, second_host, self.second, self.got_second)

    def run(self, tag):
        self.got_first(_run_exchange(self.first(), f"gather_ici_{tag}"))
        self.got_second(_run_exchange(self.second(), f"gather_d2d_{tag}"))


class _GradReduce:
    def __init__(self, g, chip, core, tag):
        self.names, self.g, self.chip, self.core, self.tag, self.result = list(g), list(g.values()), chip, core, tag, None

    def swap(self):
        return _swap_halves(self.g)

    def got_swap(self, theirs):
        self.pair = [_pair_sum(g, t, self.core, f"pair_sum_{n}_{self.tag}") for n, g, t in zip(self.names, self.g, theirs)]

    def scatter(self):
        return _scatter_shards(self.pair)

    def got_scatter(self, q):
        self.mine = [_chip_sum(qa, pa, self.chip, f"chip_sum_{n}_{self.tag}") for n, qa, pa in zip(self.names, q, self.pair)]

    def share(self):
        return _share_halves(self.mine)

    def got_share(self, theirs):
        self.result = {n: jnp.where(self.core == 0, jnp.concatenate([a, b]), jnp.concatenate([b, a]))
                       for n, a, b in zip(self.names, self.mine, theirs)}

    def ride(self, hooks, swap_host, scatter_host, share_host):
        _hook(hooks, swap_host, self.swap, self.got_swap)
        _hook(hooks, scatter_host, self.scatter, self.got_scatter)
        _hook(hooks, share_host, self.share, self.got_share)

    def run(self):
        self.got_swap(_run_exchange(self.swap(), f"swap_halves_{self.tag}"))
        self.got_scatter(_run_exchange(self.scatter(), f"scatter_shards_{self.tag}"))
        self.got_share(_run_exchange(self.share(), f"share_halves_{self.tag}"))


class _LayerWeights:
    def __init__(self, gathered):
        self.gathered, self.made = gathered, {}

    def __getitem__(self, key):
        if key not in self.made:
            cols = lambda t: jnp.swapaxes(t, 0, 1).reshape(t.shape[1], -1)
            rows = lambda t: t.reshape(-1, t.shape[2])
            if key == "w_in":
                made = jnp.pad(cols(self.gathered("w_in")), ((0, 0), (0, PROJ_PAD - PROJ)))
            elif key == "w_gu":
                made = jnp.concatenate([cols(self.gathered("w_gate")), cols(self.gathered("w_up"))], axis=-1)
            else:
                made = rows(self.gathered(key))
            self.made[key] = made
        return self.made[key]


def _gather_small(pk, name):
    rows, w = pk.shape

    def body(pk_ref, all_ref, sum_ref, send_sems, recv_sems):
        x, y, c, _ = _place()
        me = 4 * x + 2 * y + c
        all_ref[me] = pk_ref[...]
        flips = [(fx, fy, fc) for fx in (0, 1) for fy in (0, 1) for fc in (0, 1)][1:]
        peers = [(x ^ fx, y ^ fy, c ^ fc) for fx, fy, fc in flips]
        sends = [_remote(pk_ref, all_ref.at[me], send_sems.at[k], recv_sems.at[k], peer) for k, peer in enumerate(peers)]
        for cp in sends:
            cp.start()
        for k, (px, py, pc) in enumerate(peers):
            slot = all_ref.at[4 * px + 2 * py + pc]
            _remote(slot, slot, send_sems.at[k], recv_sems.at[k], (px, py, pc)).wait_recv()
        for cp in sends:
            cp.wait_send()
        total = all_ref[0]
        for d in range(1, N_DEV):
            total = total + all_ref[d]
        sum_ref[...] = total

    vm = pl.BlockSpec(memory_space=pltpu.VMEM)
    return pl.pallas_call(
        body, name=name, in_specs=[vm], out_specs=[vm, vm],
        out_shape=[jax.ShapeDtypeStruct((N_DEV, rows, w), F32), jax.ShapeDtypeStruct((rows, w), F32)],
        scratch_shapes=[pltpu.SemaphoreType.DMA((7,)), pltpu.SemaphoreType.DMA((7,))],
    )(pk)


def _row_layout(c, nb, s):
    ch = jnp.swapaxes(c[:, :N_HEADS].reshape(nb, s, N_HEADS), 1, 2)
    ccol = jnp.broadcast_to(ch[..., None], (nb, N_HEADS, s, ATT))
    crow = jnp.broadcast_to(ch.reshape(nb, N_HEADS, s // ATT, 1, ATT), (nb, N_HEADS, s // ATT, 8, ATT))
    return ccol, crow


def _dil_bias(rel_bias, name):
    def body(rel_ref, t_ref, o_ref):
        for p in range(len(DIL_PATTERNS)):
            table = t_ref[p]

            def bucket(k, accs, table=table):
                return tuple(jnp.where(table == k, rel_ref[k, h], acc) for h, acc in enumerate(accs))

            accs = lax.fori_loop(0, REL_BUCKETS, bucket, tuple(jnp.full((BLK, 2 * BLK), NEG, F32) for _ in range(N_HEADS)))
            for h in range(N_HEADS):
                o_ref[p, h] = accs[h]

    vm = pl.BlockSpec(memory_space=pltpu.VMEM)
    return pl.pallas_call(
        body, name=name, in_specs=[pl.BlockSpec(memory_space=pltpu.SMEM), vm], out_specs=vm,
        out_shape=jax.ShapeDtypeStruct((len(DIL_PATTERNS), N_HEADS, BLK, 2 * BLK), F32),
        compiler_params=pltpu.CompilerParams(vmem_limit_bytes=VMEM_LIMIT),
    )(rel_bias, jnp.asarray(_bucket_table()))


def _layer_forward(x, x_b, wts, small, bias, nb, s, tag, hooks=None):
    proj = _matmul(x_b, wts["w_in"], "proj", tag)

    carry, finish = _take(hooks, "sb_fwd")
    (o_sb, tails_sb), carried = _sb_fwd(proj, nb, s, f"sb_fwd_{tag}", carry)
    finish(carried)

    carry, finish = _take(hooks, "dil_fwd")
    (o_dl, lse_dl), carried = _dil_attention_fwd(proj, bias, nb, s, f"dil_fwd_{tag}", carry)
    finish(carried)

    fb = jnp.zeros((8, BLK), F32).at[0, :N_HEADS].set(small["f_bias"])
    csum = _fox_gates_fwd(proj, fb, nb, s, f"fox_gates_{tag}")
    ccol, crow = _row_layout(csum, nb, s)
    carry, finish = _take(hooks, "fox_fwd")
    (o_fx, lse_fx), carried = _fox_fwd(proj, ccol, crow, nb, s, f"fox_fwd_{tag}", carry)
    finish(carried)

    cw = jnp.zeros((8, CONV_W), F32).at[:3].set(small["conv_w"])
    o_cv = _conv_fwd(proj, cw, nb, s, f"conv_fwd_{tag}")

    mixed = jnp.concatenate([o_sb, o_dl, o_fx, o_cv], axis=-1).astype(MXU_DTYPE)
    pre1, x1, x1_b = _matmul_post_norm(mixed, wts["w_out"], x, small["ln1_g"], small["ln1_b"], f"out_proj_ln1_{tag}")
    carry, finish = _take(hooks, "ffn_in")
    (gate, up, hid), carried = _ffn_in(x1_b, wts["w_gu"], f"ffn_in_{tag}", carry)
    finish(carried)
    pre2, x2, x2_b = _matmul_post_norm(hid, wts["w_down"], x1, small["ln2_g"], small["ln2_b"], f"ffn_out_ln2_{tag}")
    saved = dict(x_b=x_b, proj=proj, tails_sb=tails_sb, bias=bias, o_dl=o_dl, lse_dl=lse_dl, fb=fb, ccol=ccol, crow=crow, o_fx=o_fx,
                 lse_fx=lse_fx, cw=cw, mixed=mixed, pre1=pre1, x1_b=x1_b, gate=gate, up=up, hid=hid, pre2=pre2)
    return (x2, x2_b), saved


def _layer_backward(dx2, sv, wts, small, nb, s, tag, hooks=None, ffn_grads_ready=None):
    t = nb * s
    dpre2, dpre2_b, dgb2 = _ln_bwd(dx2, sv["pre2"], small["ln2_g"], f"ln2_bwd_{tag}")
    carry, finish = _take(hooks, "ffn_out_dx")
    (dgate, dup), carried = _ffn_out_dx(dpre2_b, wts["w_down"], sv["gate"], sv["up"], f"ffn_out_dx_{tag}", carry)
    finish(carried)
    dw_down = _matmul(sv["hid"], dpre2_b, "ffn_out_dw", tag, trans_a=True)
    dx1 = _ffn_in_dx(dgate, dup, wts["w_gu"], dpre2, f"ffn_in_dx_{tag}")
    x1_b = sv["x1_b"]
    dw_gate = _matmul(x1_b, dgate, "ffn_in_dw", f"{tag}_gate", trans_a=True)
    dw_up = _matmul(x1_b, dup, "ffn_in_dw", f"{tag}_up", trans_a=True)

    dpre1, dpre1_b, dgb1 = _ln_bwd(dx1, sv["pre1"], small["ln1_g"], f"ln1_bwd_{tag}")
    dmixed = _matmul(dpre1_b, wts["w_out"], "out_proj_dx", tag, trans_b=True)
    dw_out = _matmul(sv["mixed"], dpre1_b, "out_proj_dw", tag, trans_a=True)
    if ffn_grads_ready:
        ffn_grads_ready(dict(w_down=dw_down, w_gate=dw_gate, w_up=dw_up, w_out=dw_out))
    proj = sv["proj"]

    carry, finish = _take(hooks, "sb_bwd")
    (dq_sb, dk_sb, dv_sb), carried = _sb_bwd(proj, dmixed, sv["tails_sb"], nb, s, f"sb_bwd_{tag}", carry)
    finish(carried)

    delta_dl = _delta_kernel(dmixed, sv["o_dl"], nb, s, f"dil_delta_{tag}")
    carry, finish = _take(hooks, "dil_bwd")
    (dq_dl, dk_dl, dv_dl, gbias), carried = _dil_attention_bwd(proj, dmixed, sv["lse_dl"], delta_dl, sv["bias"], nb, s,
                                                               f"dil_bwd_{tag}", carry)
    finish(carried)
    drel = _bucket_reduce(gbias, jnp.asarray(_bucket_table()), f"rel_bias_grad_{tag}")

    carry, finish = _take(hooks, "fox_bwd")
    (dq_fx, dk_fx, dv_fx, dcol), carried = _fox_bwd(proj, dmixed, sv["lse_fx"], sv["ccol"], sv["crow"], nb, s,
                                                    f"fox_bwd_{tag}", carry)
    finish(carried)
    dcs = -jnp.swapaxes(dcol[:, :, :, 0, :].reshape(nb, N_HEADS, s), 1, 2).reshape(t, N_HEADS)
    dcs = jnp.pad(dcs, ((0, 0), (0, BLK - N_HEADS)))
    dfx, dfb = _fox_gates_bwd(dcs, proj, sv["fb"], nb, s, f"fox_gates_bwd_{tag}")

    dgates, dcw = _conv_bwd(dmixed, proj, sv["cw"], nb, s, f"conv_bwd_{tag}")

    dproj = jnp.concatenate([dq_sb, dk_sb, dv_sb, dq_dl, dk_dl, dv_dl, dq_fx, dk_fx, dv_fx, dgates, dfx],
                            axis=-1).astype(MXU_DTYPE)
    dx = _matmul(dproj, wts["w_in"], "proj_dx", tag, add=dpre1, add_scale=ALPHA, trans_b=True)
    dw_in = _matmul(sv["x_b"], dproj, "proj_dw", tag, trans_a=True)

    grads = dict(w_in=dw_in[:, :PROJ], w_out=dw_out, w_gate=dw_gate, w_up=dw_up, w_down=dw_down,
                 ln1_g=dgb1[0], ln1_b=dgb1[1], ln2_g=dgb2[0], ln2_b=dgb2[1], conv_w=dcw[:3], f_bias=dfb[0, :N_HEADS],
                 rel_bias=drel[:N_HEADS, :REL_BUCKETS].T)
    return dx, grads


class _NoExchanges:
    def forward_hooks(self, layer):
        return None

    def backward_hooks(self, layer):
        return None

    def ffn_grads_ready(self, layer):
        return None

    def layer_done(self, layer, grads):
        pass


def _local_step(x, target, weights_of, small_all, schedule=None):
    schedule = schedule or _NoExchanges()
    nb, s, d = x.shape
    h = x.reshape(nb * s, d)
    h_b = h.astype(MXU_DTYPE)
    bias = _dil_bias(small_all[0]["rel_bias"], "dil_bias")
    saved = []
    for layer in range(DEPTH):
        wts = weights_of(layer)
        (h, h_b), sv = _layer_forward(h, h_b, wts, small_all[layer], bias, nb, s, f"l{layer}", schedule.forward_hooks(layer))
        saved.append((sv, wts))
    dy, lossp = _loss_kernel(h, target.reshape(nb * s, d), "loss")
    grads = [None] * DEPTH
    for layer in reversed(range(DEPTH)):
        sv, wts = saved[layer]
        dy, grads[layer] = _layer_backward(dy, sv, wts, small_all[layer], nb, s, f"l{layer}",
                                           schedule.backward_hooks(layer), schedule.ffn_grads_ready(layer))
        schedule.layer_done(layer, grads[layer])
    return lossp, dy.reshape(nb, s, d), grads


_BIG = ("w_in", "w_out", "w_gate", "w_up", "w_down")
_COL_SHARDED = ("w_in", "w_gate", "w_up")


class _Schedule:
    def __init__(self, shards, chip, core):
        self.chip, self.core, self.reduces = chip, core, [[] for _ in range(DEPTH)]
        first = _WeightPrefetch(["w_in"], shards, 0, chip)
        first.run("l0_w_in")
        rest = _WeightPrefetch(["w_out", "w_gate", "w_up", "w_down"], shards, 0, chip)
        ahead_a = _WeightPrefetch(["w_in", "w_out", "w_down"], shards, 1, chip)
        ahead_b = _WeightPrefetch(["w_gate", "w_up"], shards, 1, chip)
        self.fetches = [[first, rest], [ahead_a, ahead_b]]
        self.forward, self.backward = [{} for _ in range(DEPTH)], [{} for _ in range(DEPTH)]
        rest.ride(self.forward[0], "sb_fwd", "fox_fwd")
        ahead_a.ride(self.forward[0], "dil_fwd", "ffn_in")
        ahead_b.ride(self.forward[0], "fox_fwd", "ffn_in")

    def weights(self, layer):
        def gathered(name):
            return next(f.result[name] for f in self.fetches[layer] if name in f.names)
        return _LayerWeights(gathered)

    def forward_hooks(self, layer):
        return self.forward[layer]

    def backward_hooks(self, layer):
        return self.backward[layer]

    def _reduce(self, layer, grads, tag):
        red = _GradReduce({name: _by_chip(name, g) for name, g in grads.items()}, self.chip, self.core, tag)
        self.reduces[layer].append(red)
        return red

    def ffn_grads_ready(self, layer):
        if layer != 0:
            return None

        def ready(early):
            self._reduce(0, early, "l0_early").ride(self.backward[0], "sb_bwd", "dil_bwd", "fox_bwd")

        return ready

    def layer_done(self, layer, grads):
        if layer == 1:
            self._reduce(1, {name: grads[name] for name in _BIG}, "l1").ride(self.backward[0], "ffn_out_dx", "sb_bwd", "fox_bwd")
        else:
            self._reduce(0, dict(w_in=grads["w_in"]), "l0_w_in").run()

    def reduced(self, layer, name):
        return next(r.result[name] for r in self.reduces[layer] if name in r.names)


def _by_chip(name, g):
    if name in _COL_SHARDED:
        return jnp.swapaxes(g.reshape(g.shape[0], N_CHIPS, -1), 0, 1)
    return g.reshape(N_CHIPS, -1, g.shape[1])


_SMALL_LAYOUT = (("ln1_g", 0), ("ln1_b", 2), ("ln2_g", 4), ("ln2_b", 6), ("conv_w", 8))
_ROW_MISC = 10
_ROW_LOSS = 11


def _pack_small(per_layer, rel_bias, loss=None):
    pk = jnp.zeros((SMALL_ROWS, D_MODEL), F32)
    for name, row in _SMALL_LAYOUT:
        for l in range(DEPTH):
            v = per_layer[l][name].reshape(-1)
            pk = pk.at[row + l, :v.shape[0]].set(v)
    fb = jnp.concatenate([per_layer[l]["f_bias"] for l in range(DEPTH)])
    pk = pk.at[_ROW_MISC, :2 * N_HEADS].set(fb)
    pk = pk.at[_ROW_MISC, BLK:BLK + REL_BUCKETS * N_HEADS].set(rel_bias.reshape(-1))
    if loss is not None:
        pk = pk.at[_ROW_LOSS, 0].set(loss)
    return pk


def _unpack_small(pk, conv_cols):
    out = {}
    for name, row in _SMALL_LAYOUT:
        n = 3 * conv_cols if name == "conv_w" else D_MODEL
        v = pk[row:row + DEPTH, :n]
        out[name] = v.reshape(DEPTH, 3, conv_cols) if name == "conv_w" else v
    out["f_bias"] = pk[_ROW_MISC, :2 * N_HEADS].reshape(DEPTH, N_HEADS)
    out["rel_bias"] = pk[_ROW_MISC, BLK:BLK + REL_BUCKETS * N_HEADS].reshape(REL_BUCKETS, N_HEADS)
    return out


_WEIGHTS = ("w_in", "f_bias", "conv_w", "w_out", "rel_bias", "ln1_g", "ln1_b", "w_gate", "w_up", "w_down", "ln2_g", "ln2_b")


def kernel(x, w_in, f_bias, conv_w, w_out, rel_bias, ln1_g, ln1_b, w_gate, w_up, w_down, ln2_g, ln2_b, loss_target, m_w_in, m_f_bias, m_conv_w, m_w_out, m_rel_bias, m_ln1_g, m_ln1_b, m_w_gate, m_w_up, m_w_down, m_ln2_g, m_ln2_b, v_w_in, v_f_bias, v_conv_w, v_w_out, v_rel_bias, v_ln1_g, v_ln1_b, v_w_gate, v_w_up, v_w_down, v_ln2_g, v_ln2_b):
    w = dict(w_in=w_in, f_bias=f_bias, conv_w=conv_w, w_out=w_out, rel_bias=rel_bias, ln1_g=ln1_g, ln1_b=ln1_b,
             w_gate=w_gate, w_up=w_up, w_down=w_down, ln2_g=ln2_g, ln2_b=ln2_b)
    m = dict(w_in=m_w_in, f_bias=m_f_bias, conv_w=m_conv_w, w_out=m_w_out, rel_bias=m_rel_bias, ln1_g=m_ln1_g,
             ln1_b=m_ln1_b, w_gate=m_w_gate, w_up=m_w_up, w_down=m_w_down, ln2_g=m_ln2_g, ln2_b=m_ln2_b)
    v = dict(w_in=v_w_in, f_bias=v_f_bias, conv_w=v_conv_w, w_out=v_w_out, rel_bias=v_rel_bias, ln1_g=v_ln1_g,
             ln1_b=v_ln1_b, w_gate=v_w_gate, w_up=v_w_up, w_down=v_w_down, ln2_g=v_ln2_g, ln2_b=v_ln2_b)
    chip = 2 * lax.axis_index("x") + lax.axis_index("y")
    core = lax.axis_index("c")
    conv_shard = CONV_W // N_CHIPS

    schedule = _Schedule({name: w[name].astype(MXU_DTYPE) for name in _BIG}, chip, core)
    cw_pk = jnp.zeros((8, D_MODEL), F32).at[0, :DEPTH * 3 * conv_shard].set(conv_w.reshape(-1))
    cw_all, _ = _gather_small(cw_pk, "gather_conv_w")
    cw_chips = cw_all[0::2, 0, :DEPTH * 3 * conv_shard].reshape(N_CHIPS, DEPTH, 3, conv_shard)
    conv_full = jnp.moveaxis(cw_chips, 0, 2).reshape(DEPTH, 3, CONV_W)
    small_all = [dict(f_bias=f_bias[l], conv_w=conv_full[l], rel_bias=rel_bias, ln1_g=ln1_g[l], ln1_b=ln1_b[l],
                      ln2_g=ln2_g[l], ln2_b=ln2_b[l]) for l in range(DEPTH)]

    lossp, grad_x, grads = _local_step(x, loss_target, schedule.weights, small_all, schedule)
    big_g = {name: jnp.stack([schedule.reduced(l, name) for l in range(DEPTH)]) for name in _BIG}

    drel = grads[0]["rel_bias"] + grads[1]["rel_bias"]
    small_pk = _pack_small(grads, drel, lossp[0, 0])
    _, small_sum = _gather_small(small_pk, "gather_small_grads")
    loss = small_sum[_ROW_LOSS, 0]
    small_g = _unpack_small(small_sum, CONV_W)
    small_g["conv_w"] = lax.dynamic_slice_in_dim(small_g["conv_w"], chip * conv_shard, conv_shard, axis=2)

    out_g, out_d, out_m, out_v = dict(small_g), {}, {}, {}
    for name in _BIG:
        out_g[name] = big_g[name]
        out_d[name], out_m[name], out_v[name] = _adamw(w[name], big_g[name], m[name], v[name], f"adamw_{name}")
    as_3d = lambda t: t if t.ndim == 3 else t[None]
    for name in _WEIGHTS:
        if name not in _BIG:
            stepped = _adamw(as_3d(w[name]), as_3d(small_g[name]), as_3d(m[name]), as_3d(v[name]), f"adamw_{name}")
            out_d[name], out_m[name], out_v[name] = (t.reshape(w[name].shape) for t in stepped)

    return (loss, grad_x, *[out_g[n] for n in _WEIGHTS], *[out_d[n] for n in _WEIGHTS],
            *[out_m[n] for n in _WEIGHTS], *[out_v[n] for n in _WEIGHTS])
```

```python
import functools
import math

import numpy as np
import jax
import jax.numpy as jnp
from jax import lax
from jax.experimental import pallas as pl
from jax.experimental.pallas import tpu as pltpu

F32 = jnp.float32
BF16 = jnp.bfloat16
MXU_DTYPE = BF16

D_MODEL = 1024
HEAD_DIM = 64
N_HEADS = 4
BLK = 128
ATT = 256
QT = 512
CONV_W = 256
PROJ = 3076
PROJ_PAD = 3200
D_FF = 2816
DEPTH = 2
ALPHA = (2 * DEPTH) ** 0.25
LN_EPS = 1e-5
NEG = -1e30
DIL_PATTERNS = ((128, 1), (512, 4), (2048, 16))
REL_BUCKETS = 32
N_CHIPS = 4
N_DEV = 8
SMALL_ROWS = 16

ADAM_LR = 0.001
ADAM_B1 = 0.9
ADAM_B2 = 0.999
ADAM_EPS = 1e-08
ADAM_WD = 0.01
ADAM_STEP = 10

VMEM_LIMIT = 56 * 2 ** 20
MESH = pl.DeviceIdType.MESH


def _cparams(*sem):
    return pltpu.CompilerParams(dimension_semantics=tuple(sem), vmem_limit_bytes=VMEM_LIMIT)


def _dot(a, b):
    return jnp.dot(a.astype(MXU_DTYPE), b.astype(MXU_DTYPE), preferred_element_type=F32)


def _dot_nt(a, b):
    return lax.dot_general(a.astype(MXU_DTYPE), b.astype(MXU_DTYPE), (((1,), (1,)), ((), ())),
                           preferred_element_type=F32)


def _dot_tn(a, b):
    return lax.dot_general(a.astype(MXU_DTYPE), b.astype(MXU_DTYPE), (((0,), (0,)), ((), ())),
                           preferred_element_type=F32)


def _split_dot(x, ones, passes):
    acc, rest = None, x
    for p in range(passes):
        piece = rest.astype(MXU_DTYPE)
        part = jnp.dot(piece, ones, preferred_element_type=F32)
        acc = part if acc is None else acc + part
        if p + 1 < passes:
            rest = rest - piece.astype(F32)
    return acc


def _split_dot_lhs(ones, x, passes):
    acc, rest = None, x
    for p in range(passes):
        piece = rest.astype(MXU_DTYPE)
        part = jnp.dot(ones, piece, preferred_element_type=F32)
        acc = part if acc is None else acc + part
        if p + 1 < passes:
            rest = rest - piece.astype(F32)
    return acc


def _iota2(shape, axis):
    return lax.broadcasted_iota(jnp.int32, shape, axis)


_TILES = {"proj": (2048, 640, 1024), "ffn_out_dw": (1408, 1024, 2048),
          "ffn_in_dw": (1024, 1408, 2048), "out_proj_dx": (1024, 1024, 1024),
          "out_proj_dw": (1024, 1024, 2048), "proj_dx": (1024, 512, 3200), "proj_dw": (1024, 640, 2048)}


def _matmul(a, b, kind, tag, *, out_dtype=F32, add=None, add_scale=1.0, trans_a=False, trans_b=False):
    k, m = a.shape if trans_a else a.shape[::-1]
    n = b.shape[0] if trans_b else b.shape[1]
    tm, tn, tk = _TILES[kind]
    tm, tk, name = min(tm, m), min(tk, k), f"{kind}_{tag}"
    assert m % tm == 0 and n % tn == 0 and k % tk == 0, (a.shape, b.shape, tm, tn, tk)
    nk = k // tk

    def body(*refs):
        if add is None:
            a_ref, b_ref, o_ref = refs[:3]
            c_ref, scr = None, refs[3:]
        else:
            a_ref, b_ref, c_ref, o_ref = refs[:4]
            scr = refs[4:]
        dot = _dot_tn if trans_a else _dot_nt if trans_b else _dot
        part = dot(a_ref[...], b_ref[...])

        def finish(acc):
            if c_ref is not None:
                acc = acc + add_scale * c_ref[...]
            o_ref[...] = acc.astype(out_dtype)

        if nk == 1:
            finish(part)
        else:
            acc_ref = scr[0]
            kk = pl.program_id(2)

            @pl.when(kk == 0)
            def _():
                acc_ref[...] = part

            @pl.when(kk > 0)
            def _():
                acc_ref[...] += part

            @pl.when(kk == nk - 1)
            def _():
                finish(acc_ref[...])

    b_spec = pl.BlockSpec((tn, tk), lambda i, j, kk: (j, kk)) if trans_b else pl.BlockSpec((tk, tn), lambda i, j, kk: (kk, j))
    a_spec = pl.BlockSpec((tk, tm), lambda i, j, kk: (kk, i)) if trans_a else pl.BlockSpec((tm, tk), lambda i, j, kk: (i, kk))
    in_specs = [a_spec, b_spec]
    operands = [a, b]
    if add is not None:
        in_specs.append(pl.BlockSpec((tm, tn), lambda i, j, kk: (i, j)))
        operands.append(add)
    return pl.pallas_call(
        body, name=name, grid=(m // tm, n // tn, nk), in_specs=in_specs,
        out_specs=pl.BlockSpec((tm, tn), lambda i, j, kk: (i, j)),
        out_shape=jax.ShapeDtypeStruct((m, n), out_dtype),
        scratch_shapes=[pltpu.VMEM((tm, tn), F32)] if nk > 1 else [],
        compiler_params=_cparams("parallel", "parallel", "arbitrary"),
    )(*operands)


def _matmul_post_norm(a, b, xin, g, beta, name):
    t, k = a.shape
    d = b.shape[1]
    tm = 512

    def body(a_ref, b_ref, x_ref, g_ref, beta_ref, pre_ref, y_ref, yb_ref):
        pre = ALPHA * x_ref[...] + _dot(a_ref[...], b_ref[...])
        xhat, _ = _ln_stats(pre)
        y = xhat * g_ref[...] + beta_ref[...]
        pre_ref[...] = pre
        y_ref[...] = y
        yb_ref[...] = y.astype(yb_ref.dtype)

    row = pl.BlockSpec((tm, d), lambda i: (i, 0))
    vec = pl.BlockSpec((1, d), lambda i: (0, 0))
    return pl.pallas_call(
        body, name=name, grid=(t // tm,),
        in_specs=[pl.BlockSpec((tm, k), lambda i: (i, 0)), pl.BlockSpec((k, d), lambda i: (0, 0)), row, vec, vec],
        out_specs=[row, row, row],
        out_shape=[jax.ShapeDtypeStruct((t, d), F32)] * 2 + [jax.ShapeDtypeStruct((t, d), MXU_DTYPE)],
        compiler_params=_cparams("parallel"),
    )(a, b, xin, g.reshape(1, d), beta.reshape(1, d))


def _ffn_in(x1, w_gu, name, carry=None):
    t, d = x1.shape
    tm, tn = 512, D_FF // 2
    nj = D_FF // tn

    def body(x_ref, wg_ref, wu_ref, gate_ref, up_ref, h_ref):
        xb = x_ref[...].astype(MXU_DTYPE)
        gate = _dot(xb, wg_ref[...])
        up = _dot(xb, wu_ref[...])
        gate_ref[...] = gate
        up_ref[...] = up
        h_ref[...] = (gate * (1.0 / (1.0 + jnp.exp(-gate))) * up).astype(h_ref.dtype)

    out = pl.BlockSpec((tm, tn), lambda i, j: (i, j))
    return _host_call(
        body, carry, name=name, grid=(t // tm, nj),
        in_specs=[pl.BlockSpec((tm, d), lambda i, j: (i, 0)), pl.BlockSpec((d, tn), lambda i, j: (0, j)),
                  pl.BlockSpec((d, tn), lambda i, j: (0, nj + j))],
        out_specs=[out, out, out],
        out_shape=[jax.ShapeDtypeStruct((t, D_FF), F32)] * 2 + [jax.ShapeDtypeStruct((t, D_FF), MXU_DTYPE)],
        operands=(x1, w_gu, w_gu))


def _ffn_out_dx(dy, w_down, gate, up, name, carry=None):
    t, d = dy.shape
    tm, tn = 512, D_FF // 2

    def body(dy_ref, w_ref, gate_ref, up_ref, dg_ref, du_ref):
        dh = _dot_nt(dy_ref[...], w_ref[...])
        gate = gate_ref[...]
        sig = 1.0 / (1.0 + jnp.exp(-gate))
        dg_ref[...] = (dh * up_ref[...] * sig * (1.0 + gate * (1.0 - sig))).astype(dg_ref.dtype)
        du_ref[...] = (dh * gate * sig).astype(du_ref.dtype)

    tile = pl.BlockSpec((tm, tn), lambda i, j: (i, j))
    return _host_call(
        body, carry, name=name, grid=(t // tm, D_FF // tn),
        in_specs=[pl.BlockSpec((tm, d), lambda i, j: (i, 0)), pl.BlockSpec((tn, d), lambda i, j: (j, 0)), tile, tile],
        out_specs=[tile, tile], out_shape=[jax.ShapeDtypeStruct((t, D_FF), MXU_DTYPE)] * 2,
        operands=(dy, w_down, gate, up))


def _ffn_in_dx(dgate, dup, w_gu, add, name):
    t = dgate.shape[0]
    d = w_gu.shape[0]
    tm, tk = 1024, D_FF // 2
    nk = D_FF // tk

    def body(dg_ref, du_ref, wg_ref, wu_ref, add_ref, o_ref, acc_ref):
        kk = pl.program_id(1)
        part = _dot_nt(dg_ref[...], wg_ref[...]) + _dot_nt(du_ref[...], wu_ref[...])

        @pl.when(kk == 0)
        def _():
            acc_ref[...] = part

        @pl.when(kk > 0)
        def _():
            acc_ref[...] += part

        @pl.when(kk == nk - 1)
        def _():
            o_ref[...] = acc_ref[...] + ALPHA * add_ref[...]

    act = pl.BlockSpec((tm, tk), lambda i, kk: (i, kk))
    row = pl.BlockSpec((tm, d), lambda i, kk: (i, 0))
    return pl.pallas_call(
        body, name=name, grid=(t // tm, nk),
        in_specs=[act, act, pl.BlockSpec((d, tk), lambda i, kk: (0, kk)), pl.BlockSpec((d, tk), lambda i, kk: (0, nk + kk)), row],
        out_specs=row, out_shape=jax.ShapeDtypeStruct((t, d), F32), scratch_shapes=[pltpu.VMEM((tm, d), F32)],
        compiler_params=_cparams("parallel", "arbitrary"),
    )(dgate, dup, w_gu, w_gu, add)


def _ln_stats(pre):
    mu = jnp.mean(pre, axis=-1, keepdims=True)
    xc = pre - mu
    var = jnp.mean(xc * xc, axis=-1, keepdims=True)
    rstd = lax.rsqrt(var + LN_EPS)
    return xc * rstd, rstd


def _ln_bwd(dy, pre, g, name):
    t, d = dy.shape
    tile = 256

    def body(dy_ref, pre_ref, g_ref, dpre_ref, dpre_b_ref, dgb_ref):
        dyv = dy_ref[...]
        xhat, rstd = _ln_stats(pre_ref[...])
        dxh = dyv * g_ref[...]
        m1 = jnp.mean(dxh, axis=-1, keepdims=True)
        m2 = jnp.mean(dxh * xhat, axis=-1, keepdims=True)
        dpre = rstd * (dxh - m1 - xhat * m2)
        dpre_ref[...] = dpre
        dpre_b_ref[...] = dpre.astype(dpre_b_ref.dtype)

        @pl.when(pl.program_id(0) == 0)
        def _():
            dgb_ref[...] = jnp.zeros_like(dgb_ref)

        dgb_ref[0:1, :] += jnp.sum(dyv * xhat, axis=0, keepdims=True)
        dgb_ref[1:2, :] += jnp.sum(dyv, axis=0, keepdims=True)

    row = pl.BlockSpec((tile, d), lambda i: (i, 0))
    return pl.pallas_call(
        body, name=name, grid=(t // tile,), in_specs=[row, row, pl.BlockSpec((1, d), lambda i: (0, 0))],
        out_specs=[row, row, pl.BlockSpec((8, d), lambda i: (0, 0))],
        out_shape=[jax.ShapeDtypeStruct((t, d), F32), jax.ShapeDtypeStruct((t, d), MXU_DTYPE), jax.ShapeDtypeStruct((8, d), F32)],
        compiler_params=_cparams("arbitrary"),
    )(dy, pre, g.reshape(1, d))


def _loss_kernel(y, target, name):
    t, d = y.shape
    tile = 512

    def body(y_ref, t_ref, dy_ref, l_ref):
        err = y_ref[...] - t_ref[...]
        dy_ref[...] = err * (1.0 / d)

        @pl.when(pl.program_id(0) == 0)
        def _():
            l_ref[...] = jnp.zeros_like(l_ref)

        l_ref[...] += jnp.sum(err * err) * (0.5 / d)

    row = pl.BlockSpec((tile, d), lambda i: (i, 0))
    return pl.pallas_call(
        body, name=name, grid=(t // tile,), in_specs=[row, row],
        out_specs=[row, pl.BlockSpec((8, 128), lambda i: (0, 0))],
        out_shape=[jax.ShapeDtypeStruct((t, d), F32), jax.ShapeDtypeStruct((8, 128), F32)],
        compiler_params=_cparams("arbitrary"),
    )(y, target)


def _adamw(w, g, m, v, name):
    nl, r, c = w.shape
    tr = r
    for cand in (256, 352, 128, 64, 16, 8):
        if r % cand == 0:
            tr = cand
            break

    def body(w_ref, g_ref, m_ref, v_ref, d_ref, nm_ref, nv_ref):
        gv = g_ref[...]
        nm = ADAM_B1 * m_ref[...] + (1.0 - ADAM_B1) * gv
        nv = ADAM_B2 * v_ref[...] + (1.0 - ADAM_B2) * (gv * gv)
        m_hat = nm / (1.0 - ADAM_B1 ** ADAM_STEP)
        v_hat = nv / (1.0 - ADAM_B2 ** ADAM_STEP)
        d_ref[...] = -ADAM_LR * (m_hat / (jnp.sqrt(v_hat) + ADAM_EPS) + ADAM_WD * w_ref[...])
        nm_ref[...] = nm
        nv_ref[...] = nv

    blk = pl.BlockSpec((1, tr, c), lambda l, i: (l, i, 0))
    return pl.pallas_call(
        body, name=name, grid=(nl, r // tr), in_specs=[blk] * 4, out_specs=[blk] * 3,
        out_shape=[jax.ShapeDtypeStruct(w.shape, F32)] * 3, compiler_params=_cparams("parallel", "parallel"),
    )(w, g, m, v)


def _shift_down(u, k, rows):
    return jnp.where(rows >= k, pltpu.roll(u, k, 0), 0.0)


def _shift_up(u, k, rows, s):
    return jnp.where(rows < s - k, pltpu.roll(u, s - k, 0), 0.0)


def _conv_fwd(proj, conv_w, nb, s, name):
    def body(b_ref, c_ref, h_ref, w_ref, o_ref):
        rows = _iota2((s, CONV_W), 0)
        u = c_ref[...] * h_ref[...]
        y = w_ref[2:3, :] * u + w_ref[1:2, :] * _shift_down(u, 1, rows) + w_ref[0:1, :] * _shift_down(u, 2, rows)
        o_ref[...] = b_ref[...] * y

    col = lambda j: pl.BlockSpec((s, CONV_W), lambda b: (b, j))
    return pl.pallas_call(
        body, name=name, grid=(nb,),
        in_specs=[col(9), col(10), col(11), pl.BlockSpec((8, CONV_W), lambda b: (0, 0))],
        out_specs=pl.BlockSpec((s, CONV_W), lambda b: (b, 0)),
        out_shape=jax.ShapeDtypeStruct((nb * s, CONV_W), F32), compiler_params=_cparams("parallel"),
    )(proj, proj, proj, conv_w)


def _conv_bwd(dmixed, proj, conv_w, nb, s, name):
    def body(do_ref, b_ref, c_ref, h_ref, w_ref, dg_ref, dw_ref):
        rows = _iota2((s, CONV_W), 0)
        cg, hg, bg, dout = c_ref[...], h_ref[...], b_ref[...], do_ref[...]
        u = cg * hg
        u1 = _shift_down(u, 1, rows)
        u2 = _shift_down(u, 2, rows)
        y = w_ref[2:3, :] * u + w_ref[1:2, :] * u1 + w_ref[0:1, :] * u2
        dy = dout * bg
        du = w_ref[2:3, :] * dy + w_ref[1:2, :] * _shift_up(dy, 1, rows, s) + w_ref[0:1, :] * _shift_up(dy, 2, rows, s)
        dg_ref[:, 0:CONV_W] = dout * y
        dg_ref[:, CONV_W:2 * CONV_W] = du * hg
        dg_ref[:, 2 * CONV_W:3 * CONV_W] = du * cg

        @pl.when(pl.program_id(0) == 0)
        def _():
            dw_ref[...] = jnp.zeros_like(dw_ref)

        dw_ref[0:1, :] += jnp.sum(dy * u2, axis=0, keepdims=True)
        dw_ref[1:2, :] += jnp.sum(dy * u1, axis=0, keepdims=True)
        dw_ref[2:3, :] += jnp.sum(dy * u, axis=0, keepdims=True)

    col = lambda j: pl.BlockSpec((s, CONV_W), lambda b: (b, j))
    return pl.pallas_call(
        body, name=name, grid=(nb,),
        in_specs=[col(3), col(9), col(10), col(11), pl.BlockSpec((8, CONV_W), lambda b: (0, 0))],
        out_specs=[pl.BlockSpec((s, 3 * CONV_W), lambda b: (b, 0)), pl.BlockSpec((8, CONV_W), lambda b: (0, 0))],
        out_shape=[jax.ShapeDtypeStruct((nb * s, 3 * CONV_W), F32), jax.ShapeDtypeStruct((8, CONV_W), F32)],
        compiler_params=_cparams("arbitrary"),
    )(dmixed, proj, proj, proj, conv_w)


def _col_spec(s, base):
    return pl.BlockSpec((s, BLK), lambda b, p: (b, base + p))


def _qrows(i):
    return pl.ds(pl.multiple_of(i * QT, QT), QT)


def _rows(j):
    return pl.ds(pl.multiple_of(j * ATT, ATT), ATT)


def _keys_upto(i):
    return (i + 1) * (QT // ATT)


def _triangle(keep):
    return keep(_iota2((ATT, ATT), 0), _iota2((ATT, ATT), 1)).astype(MXU_DTYPE)


def _rows128(i):
    return pl.ds(pl.multiple_of(i * BLK, BLK), BLK)


def _log_sigmoid_parts(z):
    e = jnp.exp(-jnp.abs(z))
    l1p = jnp.log(1.0 + e)
    lb = jnp.minimum(z, 0.0) - l1p
    return lb, lb - z, e


MIXER_W = N_HEADS * HEAD_DIM


def _mixer_spec(s, block):
    return pl.BlockSpec((s, MIXER_W), lambda b: (b, block))


def _heads_spec(s, width):
    return pl.BlockSpec((None, N_HEADS, s, width), lambda b: (b, 0, 0, 0))


def _head_masks(heads=2):
    lane = _iota2((1, heads * HEAD_DIM), 1)
    return [(lane >= h * HEAD_DIM) & (lane < (h + 1) * HEAD_DIM) for h in range(heads)]


def _split_heads(ref, scr, sels):
    for h, sel in enumerate(sels):
        scr[h] = jnp.where(sel, ref[...], 0.0).astype(MXU_DTYPE)


def _sb_fwd(proj, nb, s, name, carry=None):
    def body(q_ref, k_ref, v_ref, o_ref, tails_ref, km, vm):
        sels = _head_masks(N_HEADS)
        _split_heads(k_ref, km, sels)
        _split_heads(v_ref, vm, sels)
        rows = _iota2((QT, ATT), 0)
        cols = _iota2((QT, ATT), 1)
        lane = _iota2((QT, BLK), 1)
        later = _triangle(lambda r, c: r > c)
        tails_ref[...] = jnp.zeros_like(tails_ref)

        def qblock(i, _):
            qi = (q_ref[_qrows(i), :] * 0.125).astype(MXU_DTYPE)

            def kblock(t, state):
                carries, acc = state
                j = _keys_upto(i) - 1 - t
                strict = (cols + (j * ATT - i * QT)) < rows
                out = []
                for h in range(N_HEADS):
                    tails_ref[h, _qrows(i), :] = jnp.where(lane == j, carries[h], tails_ref[h, _qrows(i), :])
                    z = _dot_nt(qi, km[h, _rows(j), :])
                    lb, lr, _ = _log_sigmoid_parts(z)
                    lr = jnp.where(strict, lr, 0.0)
                    tail = _split_dot(lr, later, 2) + carries[h]
                    a = jnp.where(strict, jnp.exp(lb + tail), 0.0)
                    acc = acc + _dot(a, vm[h, _rows(j), :])
                    out.append(carries[h] + jnp.sum(lr, axis=-1, keepdims=True))
                return tuple(out), acc

            init = ((jnp.zeros((QT, 1), F32),) * N_HEADS, jnp.zeros((QT, MIXER_W), F32))
            _, acc = lax.fori_loop(0, _keys_upto(i), kblock, init)
            o_ref[_qrows(i), :] = acc
            return 0

        lax.fori_loop(0, s // QT, qblock, 0)

    return _host_call(
        body, carry, name=name, grid=(nb,), in_specs=[_mixer_spec(s, 0), _mixer_spec(s, 1), _mixer_spec(s, 2)],
        out_specs=[_mixer_spec(s, 0), _heads_spec(s, BLK)],
        out_shape=[jax.ShapeDtypeStruct((nb * s, MIXER_W), F32), jax.ShapeDtypeStruct((nb, N_HEADS, s, BLK), F32)],
        scratch_shapes=[pltpu.VMEM((N_HEADS, s, MIXER_W), MXU_DTYPE)] * 2, operands=(proj, proj, proj))


def _sb_bwd(proj, dmixed, tails, nb, s, name, carry=None):
    def body(q_ref, k_ref, v_ref, do_ref, tails_ref, dq_ref, dk_ref, dv_ref, km, vm):
        sels = _head_masks(N_HEADS)
        _split_heads(k_ref, km, sels)
        _split_heads(v_ref, vm, sels)
        rows = _iota2((QT, ATT), 0)
        cols = _iota2((QT, ATT), 1)
        lane = _iota2((QT, BLK), 1)
        later = _triangle(lambda r, c: r > c)
        earlier = _triangle(lambda r, c: r < c)
        dk_ref[...] = jnp.zeros_like(dk_ref)
        dv_ref[...] = jnp.zeros_like(dv_ref)

        def qblock(i, _):
            qi = (q_ref[_qrows(i), :] * 0.125).astype(MXU_DTYPE)
            doi = do_ref[_qrows(i), :].astype(MXU_DTYPE)
            qm = [jnp.where(sel, qi, 0.0) for sel in sels]
            dom = [jnp.where(sel, doi, 0.0) for sel in sels]
            tails_i = [tails_ref[h, _qrows(i), :] for h in range(N_HEADS)]

            def kblock(j, state):
                csums, dq = state
                strict = (cols + (j * ATT - i * QT)) < rows
                out = []
                for h in range(N_HEADS):
                    z = _dot_nt(qi, km[h, _rows(j), :])
                    lb, lr, _ = _log_sigmoid_parts(z)
                    lr = jnp.where(strict, lr, 0.0)
                    after = jnp.sum(jnp.where(lane == j, tails_i[h], 0.0), axis=-1, keepdims=True)
                    a = jnp.where(strict, jnp.exp(lb + _split_dot(lr, later, 2) + after), 0.0)
                    dl = a * _dot_nt(doi, vm[h, _rows(j), :])
                    beta = jnp.exp(lb)
                    before = _split_dot(dl, earlier, 2) + csums[h]
                    dz = jnp.where(strict, dl * (1.0 - beta) - beta * before, 0.0).astype(MXU_DTYPE)
                    dq = dq + _dot(dz, km[h, _rows(j), :])
                    dk_ref[_rows(j), :] += _dot_tn(dz, qm[h])
                    dv_ref[_rows(j), :] += _dot_tn(a, dom[h])
                    out.append(csums[h] + jnp.sum(dl, axis=-1, keepdims=True))
                return tuple(out), dq

            init = ((jnp.zeros((QT, 1), F32),) * N_HEADS, jnp.zeros((QT, MIXER_W), F32))
            _, dq = lax.fori_loop(0, _keys_upto(i), kblock, init)
            dq_ref[_qrows(i), :] = dq * 0.125
            return 0

        lax.fori_loop(0, s // QT, qblock, 0)

    out = _mixer_spec(s, 0)
    return _host_call(
        body, carry, name=name, grid=(nb,),
        in_specs=[_mixer_spec(s, 0), _mixer_spec(s, 1), _mixer_spec(s, 2), out, _heads_spec(s, BLK)], out_specs=[out] * 3,
        out_shape=[jax.ShapeDtypeStruct((nb * s, MIXER_W), F32)] * 3,
        scratch_shapes=[pltpu.VMEM((N_HEADS, s, MIXER_W), MXU_DTYPE)] * 2, operands=(proj, proj, proj, dmixed, tails))


def _pair_spec(s, width):
    return pl.BlockSpec((None, 2, s, width), lambda b, p: (b, p, 0, 0))


def _fox_fwd(proj, ccol, crow, nb, s, name, carry=None):
    nblk = s // ATT

    def body(q_ref, k_ref, v_ref, cc_ref, cr_ref, o_ref, lse_ref, km, vm):
        sels = _head_masks()
        _split_heads(k_ref, km, sels)
        _split_heads(v_ref, vm, sels)
        rows = _iota2((QT, ATT), 0)
        cols = _iota2((QT, ATT), 1)

        def qblock(i, _):
            qi = (q_ref[_qrows(i), :] * 0.125).astype(MXU_DTYPE)
            ci = [cc_ref[h, _qrows(i), :] for h in range(2)]

            def kblock(j, state):
                ms, ls, acc = state
                causal = (cols + (j * ATT - i * QT)) <= rows
                new_m, new_l, scales, parts = [], [], [], []
                for h in range(2):
                    z = _dot_nt(qi, km[h, _rows(j), :]) + (ci[h] - cr_ref[h, j][0:1, :])
                    z = jnp.where(causal, z, NEG)
                    m_new = jnp.maximum(ms[h], jnp.max(z, axis=-1, keepdims=True))
                    p = jnp.exp(z - m_new)
                    scale = jnp.exp(ms[h] - m_new)
                    new_m.append(m_new)
                    new_l.append(scale * ls[h] + jnp.sum(p, axis=-1, keepdims=True))
                    scales.append(scale)
                    parts.append(_dot(p, vm[h, _rows(j), :]))
                acc = jnp.where(sels[0], scales[0], scales[1]) * acc + parts[0] + parts[1]
                return tuple(new_m), tuple(new_l), acc

            init = ((jnp.full((QT, 1), NEG, F32),) * 2, (jnp.zeros((QT, 1), F32),) * 2, jnp.zeros((QT, BLK), F32))
            ms, ls, acc = lax.fori_loop(0, _keys_upto(i), kblock, init)
            o_ref[_qrows(i), :] = acc / jnp.where(sels[0], ls[0], ls[1])
            for h in range(2):
                lse_ref[h, _qrows(i), :] = jnp.broadcast_to(ms[h] + jnp.log(ls[h]), (QT, ATT))
            return 0

        lax.fori_loop(0, s // QT, qblock, 0)

    crow_spec = pl.BlockSpec((None, 2, nblk, 8, ATT), lambda b, p: (b, p, 0, 0, 0))
    return _host_call(
        body, carry, name=name, grid=(nb, 2),
        in_specs=[_col_spec(s, 12), _col_spec(s, 14), _col_spec(s, 16), _pair_spec(s, ATT), crow_spec],
        out_specs=[_col_spec(s, 0), _pair_spec(s, ATT)],
        out_shape=[jax.ShapeDtypeStruct((nb * s, 2 * BLK), F32), jax.ShapeDtypeStruct((nb, N_HEADS, s, ATT), F32)],
        scratch_shapes=[pltpu.VMEM((2, s, BLK), MXU_DTYPE)] * 2, operands=(proj, proj, proj, ccol, crow))


def _fox_bwd(proj, dmixed, lse, ccol, crow, nb, s, name, carry=None):
    nblk = s // ATT

    def body(q_ref, k_ref, v_ref, do_ref, lse_ref, cc_ref, cr_ref, dq_ref, dk_ref, dv_ref, dc_ref, km, vm, p_scr, dp_scr):
        sels = _head_masks()
        _split_heads(k_ref, km, sels)
        _split_heads(v_ref, vm, sels)
        rows = _iota2((QT, ATT), 0)
        cols = _iota2((QT, ATT), 1)
        dk_ref[...] = jnp.zeros_like(dk_ref)
        dv_ref[...] = jnp.zeros_like(dv_ref)
        dc_ref[...] = jnp.zeros_like(dc_ref)

        def qblock(i, _):
            qi = (q_ref[_qrows(i), :] * 0.125).astype(MXU_DTYPE)
            doi = do_ref[_qrows(i), :].astype(MXU_DTYPE)
            qm = [jnp.where(sel, qi, 0.0) for sel in sels]
            dom = [jnp.where(sel, doi, 0.0) for sel in sels]
            ci = [cc_ref[h, _qrows(i), :] for h in range(2)]
            lsei = [lse_ref[h, _qrows(i), :] for h in range(2)]

            def probs(j, h):
                z = _dot_nt(qi, km[h, _rows(j), :]) + (ci[h] - cr_ref[h, j][0:1, :])
                p = jnp.where((cols + (j * ATT - i * QT)) <= rows, jnp.exp(z - lsei[h]), 0.0)
                return p, _dot_nt(doi, vm[h, _rows(j), :])

            def row_term(j, accs):
                out = []
                for h in range(2):
                    p, dp = probs(j, h)
                    p_scr[h, j] = p
                    dp_scr[h, j] = dp
                    out.append(accs[h] + jnp.sum(p * dp, axis=-1, keepdims=True))
                return tuple(out)

            di = lax.fori_loop(0, _keys_upto(i), row_term, (jnp.zeros((QT, 1), F32),) * 2)

            def kblock(j, dq):
                for h in range(2):
                    p = p_scr[h, j]
                    ds = p * (dp_scr[h, j] - di[h])
                    dc_ref[h, j] += jnp.broadcast_to(jnp.sum(ds, axis=0, keepdims=True), (8, ATT))
                    ds = ds.astype(MXU_DTYPE)
                    dk_ref[_rows(j), :] += _dot_tn(ds, qm[h])
                    dv_ref[_rows(j), :] += _dot_tn(p, dom[h])
                    dq = dq + _dot(ds, km[h, _rows(j), :])
                return dq

            dq = lax.fori_loop(0, _keys_upto(i), kblock, jnp.zeros((QT, BLK), F32))
            dq_ref[_qrows(i), :] = dq * 0.125
            return 0

        lax.fori_loop(0, s // QT, qblock, 0)

    crow_spec = pl.BlockSpec((None, 2, nblk, 8, ATT), lambda b, p: (b, p, 0, 0, 0))
    wide, cols_out = _pair_spec(s, ATT), _col_spec(s, 0)
    return _host_call(
        body, carry, name=name, grid=(nb, 2),
        in_specs=[_col_spec(s, 12), _col_spec(s, 14), _col_spec(s, 16), _col_spec(s, 4), wide, wide, crow_spec],
        out_specs=[cols_out, cols_out, cols_out, crow_spec],
        out_shape=[jax.ShapeDtypeStruct((nb * s, 2 * BLK), F32)] * 3 + [jax.ShapeDtypeStruct((nb, N_HEADS, nblk, 8, ATT), F32)],
        scratch_shapes=[pltpu.VMEM((2, s, BLK), MXU_DTYPE)] * 2 + [pltpu.VMEM((2, nblk, QT, ATT), F32)] * 2,
        operands=(proj, proj, proj, dmixed, lse, ccol, crow))


def _fox_gates_fwd(proj, f_bias, nb, s, name):
    chunk = 256

    def body(f_ref, b_ref, c_ref):
        lower = (_iota2((chunk, chunk), 0) >= _iota2((chunk, chunk), 1)).astype(MXU_DTYPE)
        carry = jnp.zeros((1, BLK), F32)
        for n in range(s // chunk):
            rows = pl.ds(n * chunk, chunk)
            lf, _, _ = _log_sigmoid_parts(f_ref[rows, :] + b_ref[0:1, :])
            c = _split_dot_lhs(lower, lf, 3) + carry
            c_ref[rows, :] = c
            carry = c[chunk - 1:chunk, :]

    return pl.pallas_call(
        body, name=name, grid=(nb,),
        in_specs=[pl.BlockSpec((s, BLK), lambda b: (b, (PROJ_PAD - BLK) // BLK)), pl.BlockSpec((8, BLK), lambda b: (0, 0))],
        out_specs=pl.BlockSpec((s, BLK), lambda b: (b, 0)),
        out_shape=jax.ShapeDtypeStruct((nb * s, BLK), F32), compiler_params=_cparams("parallel"),
    )(proj, f_bias)


def _fox_gates_bwd(dc, proj, f_bias, nb, s, name):
    chunk = 256

    def body(dc_ref, f_ref, b_ref, df_ref, db_ref):
        upper = (_iota2((chunk, chunk), 0) <= _iota2((chunk, chunk), 1)).astype(MXU_DTYPE)
        carry = jnp.zeros((1, BLK), F32)
        total = jnp.zeros((1, BLK), F32)
        for n in reversed(range(s // chunk)):
            rows = pl.ds(n * chunk, chunk)
            dlf = _split_dot_lhs(upper, dc_ref[rows, :], 3) + carry
            carry = dlf[0:1, :]
            pre = f_ref[rows, :] + b_ref[0:1, :]
            e = jnp.exp(-jnp.abs(pre))
            df = dlf * (jnp.where(pre >= 0.0, e, 1.0) / (1.0 + e))
            df_ref[rows, :] = df
            total = total + jnp.sum(df, axis=0, keepdims=True)

        @pl.when(pl.program_id(0) == 0)
        def _():
            db_ref[...] = jnp.zeros_like(db_ref)

        db_ref[0:1, :] += total

    return pl.pallas_call(
        body, name=name, grid=(nb,),
        in_specs=[pl.BlockSpec((s, BLK), lambda b: (b, 0)), pl.BlockSpec((s, BLK), lambda b: (b, (PROJ_PAD - BLK) // BLK)),
                  pl.BlockSpec((8, BLK), lambda b: (0, 0))],
        out_specs=[pl.BlockSpec((s, BLK), lambda b: (b, 0)), pl.BlockSpec((8, BLK), lambda b: (0, 0))],
        out_shape=[jax.ShapeDtypeStruct((nb * s, BLK), F32), jax.ShapeDtypeStruct((8, BLK), F32)],
        compiler_params=_cparams("arbitrary"),
    )(dc, proj, f_bias)


def _delta_kernel(dmixed, o, nb, s, name):
    def body(do_ref, o_ref, d_ref):
        prod = do_ref[...] * o_ref[...]
        for h, sel in enumerate(_head_masks()):
            d_ref[h] = jnp.broadcast_to(jnp.sum(jnp.where(sel, prod, 0.0), axis=-1, keepdims=True), (s, BLK))

    return pl.pallas_call(
        body, name=name, grid=(nb, 2), in_specs=[_col_spec(s, 2), _col_spec(s, 0)], out_specs=_pair_spec(s, BLK),
        out_shape=jax.ShapeDtypeStruct((nb, N_HEADS, s, BLK), F32), compiler_params=_cparams("parallel", "parallel"),
    )(dmixed, o)


def _t5_bucket_np(dist):
    max_exact = REL_BUCKETS // 2
    nf = np.maximum(dist, 1).astype(np.float32)
    large = max_exact + (np.log(nf / max_exact) / math.log(2048 / max_exact) * (REL_BUCKETS - max_exact)).astype(np.int32)
    large = np.minimum(large, REL_BUCKETS - 1)
    return np.where(dist < max_exact, dist, large)


def _bucket_table():
    qi = np.arange(BLK)[:, None]
    kj = np.arange(2 * BLK)[None, :]
    dist = qi + BLK - kj
    tables = []
    for window, dil in DIL_PATTERNS:
        in_band = (dist >= 0) & (dist <= window // dil)
        tables.append(np.where(in_band, _t5_bucket_np(np.maximum(dist, 0) * dil), -1).astype(np.int32))
    return np.stack(tables)


def _dil_scores(qb, kp, kc, b_ref, h, prev_valid):
    zp = _dot_nt(qb, kp) + b_ref[h, :, 0:BLK]
    zp = jnp.where(prev_valid, zp, NEG)
    zc = _dot_nt(qb, kc) + b_ref[h, :, BLK:2 * BLK]
    return zp, zc


def _residue_rows(b, seg, dil):
    if dil == 1:
        return _rows128(b), _rows128(jnp.maximum(b - 1, 0)), b > 0
    r, n = b // seg, b % seg
    cur = pl.ds(r + dil * n * BLK, BLK, stride=dil)
    prev = pl.ds(r + dil * jnp.maximum(n - 1, 0) * BLK, BLK, stride=dil)
    return cur, prev, n > 0


def _dil_attention_fwd(proj, bias, nb, s, name, carry=None):
    nblk = s // BLK

    def body(q_ref, k_ref, v_ref, b_ref, out_ref, lse_ref, o_scr, l_scr):
        sels = _head_masks()
        for p, (_, dil) in enumerate(DIL_PATTERNS):
            seg = s // dil // BLK

            def block(b, _, p=p, seg=seg, dil=dil):
                cur, prev, has_prev = _residue_rows(b, seg, dil)
                qb = (q_ref[cur, :] * 0.125).astype(MXU_DTYPE)
                kp, kc = k_ref[prev, :].astype(MXU_DTYPE), k_ref[cur, :].astype(MXU_DTYPE)
                vp, vc = v_ref[prev, :].astype(MXU_DTYPE), v_ref[cur, :].astype(MXU_DTYPE)
                acc = jnp.zeros((BLK, BLK), F32)
                for h, sel in enumerate(sels):
                    zp, zc = _dil_scores(qb, jnp.where(sel, kp, 0.0), jnp.where(sel, kc, 0.0), b_ref.at[p], h, has_prev)
                    m = jnp.maximum(jnp.max(zp, axis=-1, keepdims=True), jnp.max(zc, axis=-1, keepdims=True))
                    pp = jnp.exp(zp - m)
                    pc = jnp.exp(zc - m)
                    den = jnp.sum(pp, axis=-1, keepdims=True) + jnp.sum(pc, axis=-1, keepdims=True)
                    acc = acc + (_dot(pp, jnp.where(sel, vp, 0.0)) + _dot(pc, jnp.where(sel, vc, 0.0))) / den
                    l_scr[p, h, cur, :] = jnp.broadcast_to(m + jnp.log(den), (BLK, BLK))
                o_scr[p, cur, :] = acc
                return 0

            lax.fori_loop(0, nblk, block, 0, unroll=8)

        weights, dens = [], []
        for h in range(2):
            m = jnp.maximum(jnp.maximum(l_scr[0, h], l_scr[1, h]), l_scr[2, h])
            w = [jnp.exp(l_scr[p, h] - m) for p in range(3)]
            den = w[0] + w[1] + w[2]
            lse_ref[h] = m + jnp.log(den)
            weights.append(w)
            dens.append(den)
        num = sum(jnp.where(sels[0], weights[0][p], weights[1][p]) * o_scr[p] for p in range(3))
        out_ref[...] = num / jnp.where(sels[0], dens[0], dens[1])

    bias_spec = pl.BlockSpec((3, 2, BLK, 2 * BLK), lambda b, p: (0, p, 0, 0))
    return _host_call(
        body, carry, name=name, grid=(nb, 2), in_specs=[_col_spec(s, 6), _col_spec(s, 8), _col_spec(s, 10), bias_spec],
        out_specs=[_col_spec(s, 0), _pair_spec(s, BLK)],
        out_shape=[jax.ShapeDtypeStruct((nb * s, 2 * BLK), F32), jax.ShapeDtypeStruct((nb, N_HEADS, s, BLK), F32)],
        scratch_shapes=[pltpu.VMEM((3, s, BLK), F32), pltpu.VMEM((3, 2, s, BLK), F32)], operands=(proj, proj, proj, bias))


def _dil_attention_bwd(proj, dmixed, lse, delta, bias, nb, s, name, carry=None):
    nblk = s // BLK

    def body(q_ref, k_ref, v_ref, do_ref, lse_ref, dl_ref, b_ref, dq_ref, dk_ref, dv_ref, g_ref):
        sels = _head_masks()
        dq_ref[...] = jnp.zeros_like(dq_ref)
        dk_ref[...] = jnp.zeros_like(dk_ref)
        dv_ref[...] = jnp.zeros_like(dv_ref)
        g_ref[...] = jnp.zeros_like(g_ref)
        for p, (_, dil) in enumerate(DIL_PATTERNS):
            seg = s // dil // BLK

            def block(b, _, p=p, seg=seg, dil=dil):
                cur, prev, has_prev = _residue_rows(b, seg, dil)
                qb = (q_ref[cur, :] * 0.125).astype(MXU_DTYPE)
                dob = do_ref[cur, :].astype(MXU_DTYPE)
                kp, kc = k_ref[prev, :].astype(MXU_DTYPE), k_ref[cur, :].astype(MXU_DTYPE)
                vp, vc = v_ref[prev, :].astype(MXU_DTYPE), v_ref[cur, :].astype(MXU_DTYPE)
                dq = jnp.zeros((BLK, BLK), F32)
                dkp, dkc, dvp, dvc = dq, dq, dq, dq
                for h, sel in enumerate(sels):
                    kph, kch = jnp.where(sel, kp, 0.0), jnp.where(sel, kc, 0.0)
                    qh, doh = jnp.where(sel, qb, 0.0), jnp.where(sel, dob, 0.0)
                    lse_h = lse_ref[h, cur, :]
                    dlt = dl_ref[h, cur, :]
                    zp, zc = _dil_scores(qb, kph, kch, b_ref.at[p], h, has_prev)
                    pp = jnp.exp(zp - lse_h)
                    pc = jnp.exp(zc - lse_h)
                    dsp = pp * (_dot_nt(dob, jnp.where(sel, vp, 0.0)) - dlt)
                    dsc = pc * (_dot_nt(dob, jnp.where(sel, vc, 0.0)) - dlt)
                    g_ref[h, p, :, 0:BLK] += dsp
                    g_ref[h, p, :, BLK:2 * BLK] += dsc
                    dsp = dsp.astype(MXU_DTYPE)
                    dsc = dsc.astype(MXU_DTYPE)
                    dq = dq + _dot(dsp, kph) + _dot(dsc, kch)
                    dkp, dkc = dkp + _dot_tn(dsp, qh), dkc + _dot_tn(dsc, qh)
                    dvp, dvc = dvp + _dot_tn(pp, doh), dvc + _dot_tn(pc, doh)
                dq_ref[cur, :] += dq * 0.125
                dk_ref[prev, :] += dkp
                dk_ref[cur, :] += dkc
                dv_ref[prev, :] += dvp
                dv_ref[cur, :] += dvc
                return 0

            lax.fori_loop(0, nblk, block, 0, unroll=8)

    bias_spec = pl.BlockSpec((3, 2, BLK, 2 * BLK), lambda b, p: (0, p, 0, 0))
    cols, stats = _col_spec(s, 0), _pair_spec(s, BLK)
    return _host_call(
        body, carry, name=name, grid=(nb, 2),
        in_specs=[_col_spec(s, 6), _col_spec(s, 8), _col_spec(s, 10), _col_spec(s, 2), stats, stats, bias_spec],
        out_specs=[cols, cols, cols, pl.BlockSpec((None, 2, 3, BLK, 2 * BLK), lambda b, p: (b, p, 0, 0, 0))],
        out_shape=[jax.ShapeDtypeStruct((nb * s, 2 * BLK), F32)] * 3 + [jax.ShapeDtypeStruct((nb, N_HEADS, 3, BLK, 2 * BLK), F32)],
        operands=(proj, proj, proj, dmixed, lse, delta, bias))


def _bucket_reduce(gbias, table, name):
    nb = gbias.shape[0]

    def body(g_ref, t_ref, o_ref):
        row = _iota2((8, BLK), 0)
        lane = _iota2((8, BLK), 1)
        gsum = [[sum(g_ref[b, h, p] for b in range(nb)) for p in range(3)] for h in range(N_HEADS)]

        def bucket(k, acc):
            for h in range(N_HEADS):
                tot = sum(jnp.sum(jnp.where(t_ref[p] == k, gsum[h][p], 0.0)) for p in range(3))
                acc = acc + jnp.where((row == h) & (lane == k), tot, 0.0)
            return acc

        o_ref[...] = lax.fori_loop(0, REL_BUCKETS, bucket, jnp.zeros((8, BLK), F32))

    vm = pl.BlockSpec(memory_space=pltpu.VMEM)
    return pl.pallas_call(
        body, name=name, in_specs=[vm, vm], out_specs=vm, out_shape=jax.ShapeDtypeStruct((8, BLK), F32),
        compiler_params=pltpu.CompilerParams(vmem_limit_bytes=VMEM_LIMIT),
    )(gbias, table)


def _place():
    x, y, c = lax.axis_index("x"), lax.axis_index("y"), lax.axis_index("c")
    others = [(1 - x, y), (x, 1 - y), (1 - x, 1 - y)]
    return x, y, c, others


def _remote(src, dst, send_sem, recv_sem, to):
    return pltpu.make_async_remote_copy(src_ref=src, dst_ref=dst, send_sem=send_sem, recv_sem=recv_sem,
                                        device_id=to, device_id_type=MESH)


_HBM = pl.BlockSpec(memory_space=pl.ANY)


class _Exchange:
    def __init__(self, operands, out_shape, n_copies, copies, aliases=None):
        self.operands, self.out_shape, self.n_copies, self.copies = list(operands), list(out_shape), n_copies, copies
        self.aliases = dict(aliases or {})

    def sem_shapes(self):
        return [pltpu.SemaphoreType.DMA((self.n_copies,)), pltpu.SemaphoreType.DMA((self.n_copies,))]


def _start_all(sends):
    for cp in sends:
        cp.start()


def _wait_all(sends, arrivals):
    for cp in arrivals:
        cp.wait_recv()
    for cp in sends:
        cp.wait_send()


def _run_exchange(ex, name):
    ni = len(ex.operands)

    def body(*refs):
        sends, arrivals = ex.copies(refs[:ni], refs[ni:-2], refs[-2], refs[-1])
        _start_all(sends)
        _wait_all(sends, arrivals)

    return list(pl.pallas_call(
        body, name=name, in_specs=[_HBM] * ni, out_specs=[_HBM] * len(ex.out_shape), out_shape=ex.out_shape,
        scratch_shapes=ex.sem_shapes(), input_output_aliases=ex.aliases)(*ex.operands))


def _host_call(body, carry, *, name, grid, in_specs, out_specs, out_shape, operands, scratch_shapes=()):
    in_specs, out_specs, out_shape, scratch_shapes = list(in_specs), list(out_specs), list(out_shape), list(scratch_shapes)
    if carry is None:
        res = pl.pallas_call(body, name=name, grid=grid, in_specs=in_specs, out_specs=out_specs, out_shape=out_shape,
                             scratch_shapes=scratch_shapes, compiler_params=_cparams(*["parallel"] * len(grid)))(*operands)
        return list(res), []
    n_in, n_out, n_scr, c_in, c_out = len(in_specs), len(out_specs), len(scratch_shapes), len(carry.operands), len(carry.out_shape)
    steps = math.prod(grid)

    def wrapped(*refs):
        ins, refs = refs[:n_in], refs[n_in:]
        c_ins, refs = refs[:c_in], refs[c_in:]
        outs, refs = refs[:n_out], refs[n_out:]
        c_outs, refs = refs[:c_out], refs[c_out:]
        scr, (send_sems, recv_sems) = refs[:n_scr], refs[n_scr:]
        step = 0
        for d, size in enumerate(grid):
            step = step * size + pl.program_id(d)

        @pl.when(step == 0)
        def _():
            _start_all(carry.copies(c_ins, c_outs, send_sems, recv_sems)[0])

        body(*ins, *outs, *scr)

        @pl.when(step == steps - 1)
        def _():
            _wait_all(*carry.copies(c_ins, c_outs, send_sems, recv_sems))

    res = pl.pallas_call(
        wrapped, name=name, grid=grid, in_specs=in_specs + [_HBM] * c_in, out_specs=out_specs + [_HBM] * c_out,
        out_shape=out_shape + carry.out_shape, scratch_shapes=scratch_shapes + carry.sem_shapes(),
        input_output_aliases={n_in + i: n_out + j for i, j in carry.aliases.items()},
        compiler_params=_cparams(*["arbitrary"] * len(grid)))(*operands, *carry.operands)
    return list(res[:n_out]), list(res[n_out:])


def _half(which, rows):
    h = rows // 2
    return pl.ds(pl.multiple_of(which * h, 16), h)


def _like(arrays, shape_of=lambda t: t.shape):
    return [jax.ShapeDtypeStruct(shape_of(t), t.dtype) for t in arrays]


def _gather_ici(shards, layer):
    n = len(shards)

    def copies(ins, outs, send_sems, recv_sems, base=0):
        x, y, c, others = _place()
        me = 2 * x + y
        sends, arrivals = [], []
        for a in range(n):
            rows = _half(c, shards[a].shape[1])
            for k, (ox, oy) in enumerate(others):
                sems = (send_sems.at[base + 3 * a + k], recv_sems.at[base + 3 * a + k],(ox, oy, c))
                sends.append(_remote(ins[a].at[layer, rows], outs[a].at[me, rows], *sems))
                landed = outs[a].at[2 * ox + oy, rows]
                arrivals.append(_remote(landed, landed, *sems))
        return sends, arrivals

    return _Exchange(shards, _like(shards, lambda t: (N_CHIPS,) + t.shape[1:]), 3 * n, copies)


def _gather_d2d(gathered):
    n = len(gathered)

    def copies(ins, outs, send_sems, recv_sems, base=0):
        x, y, c, others = _place()
        sends, arrivals = [], []
        for a in range(n):
            r = gathered[a].shape[1]
            for k, (ox, oy) in enumerate(others):
                sems = (send_sems.at[base + 3 * a + k], recv_sems.at[base + 3 * a + k],(x, y, 1 - c))
                mine, theirs = outs[a].at[2 * ox + oy, _half(c, r)], outs[a].at[2 * ox + oy, _half(1 - c, r)]
                sends.append(_remote(mine, mine, *sems))
                arrivals.append(_remote(theirs, theirs, *sems))
        return sends, arrivals

    return _Exchange(gathered, _like(gathered), 3 * n, copies, aliases={a: a for a in range(n)})


def _swap_halves(g):
    n = len(g)

    def copies(ins, outs, send_sems, recv_sems, base=0):
        x, y, c, _ = _place()
        sends, arrivals = [], []
        for a in range(n):
            sems = (send_sems.at[base + a], recv_sems.at[base + a], (x, y, 1 - c))
            sends.append(_remote(ins[a].at[:, _half(1 - c, g[a].shape[1])], outs[a], *sems))
            arrivals.append(_remote(outs[a], outs[a], *sems))
        return sends, arrivals

    return _Exchange(g, _like(g, lambda t: (t.shape[0], t.shape[1] // 2, t.shape[2])), n, copies)


def _scatter_shards(ps):
    n = len(ps)

    def copies(ins, outs, send_sems, recv_sems, base=0):
        x, y, c, others = _place()
        me = 2 * x + y
        sends, arrivals = [], []
        for a in range(n):
            for k, (ox, oy) in enumerate(others):
                sems = (send_sems.at[base + 3 * a + k], recv_sems.at[base + 3 * a + k],(ox, oy, c))
                sends.append(_remote(ins[a].at[2 * ox + oy], outs[a].at[me], *sems))
                slot = outs[a].at[2 * ox + oy]
                arrivals.append(_remote(slot, slot, *sems))
        return sends, arrivals

    return _Exchange(ps, _like(ps), 3 * n, copies)


def _share_halves(mine):
    n = len(mine)

    def copies(ins, outs, send_sems, recv_sems, base=0):
        x, y, c, _ = _place()
        sends, arrivals = [], []
        for a in range(n):
            sems = (send_sems.at[base + a], recv_sems.at[base + a], (x, y, 1 - c))
            sends.append(_remote(ins[a], outs[a], *sems))
            arrivals.append(_remote(outs[a], outs[a], *sems))
        return sends, arrivals

    return _Exchange(mine, _like(mine), n, copies)


def _row_tile(r):
    for cand in (256, 352, 128):
        if r % cand == 0:
            return cand
    return r


def _pair_sum(g, other, core, name):
    ns, h, w = other.shape
    tr = _row_tile(h)
    per_half = h // tr

    def body(core_ref, g_ref, o_ref, out_ref):
        out_ref[...] = (g_ref[...] + o_ref[...]).astype(out_ref.dtype)

    blk = pl.BlockSpec((None, tr, w), lambda k, i, core_ref: (k, i, 0))
    grid_spec = pltpu.PrefetchScalarGridSpec(
        num_scalar_prefetch=1, grid=(ns, per_half),
        in_specs=[pl.BlockSpec((None, tr, w), lambda k, i, core_ref: (k, core_ref[0] * per_half + i, 0)), blk], out_specs=blk)
    return pl.pallas_call(
        body, name=name, grid_spec=grid_spec, out_shape=jax.ShapeDtypeStruct((ns, h, w), MXU_DTYPE),
        compiler_params=_cparams("parallel", "parallel"),
    )(core.reshape(1).astype(jnp.int32), g, other)


def _chip_sum(q, p, chip, name):
    ns, r, w = q.shape
    tr = _row_tile(r)

    def body(chip_ref, q_ref, own_ref, out_ref):
        me = chip_ref[0]
        own = own_ref[...].astype(F32)
        terms = [jnp.where(me == k, own, q_ref[k].astype(F32)) for k in range(ns)]
        out_ref[...] = ((terms[0] + terms[1]) + terms[2]) + terms[3]

    grid_spec = pltpu.PrefetchScalarGridSpec(
        num_scalar_prefetch=1, grid=(r // tr,),
        in_specs=[pl.BlockSpec((ns, tr, w), lambda i, chip_ref: (0, i, 0)),
                  pl.BlockSpec((None, tr, w), lambda i, chip_ref: (chip_ref[0], i, 0))],
        out_specs=pl.BlockSpec((tr, w), lambda i, chip_ref: (i, 0)))
    return pl.pallas_call(
        body, name=name, grid_spec=grid_spec, out_shape=jax.ShapeDtypeStruct((r, w), F32),
        compiler_params=_cparams("parallel"),
    )(chip.reshape(1).astype(jnp.int32), q, p)


def _merge(exchanges):
    if len(exchanges) <= 1:
        return exchanges[0] if exchanges else None
    operands, out_shape, aliases, spans, n = [], [], {}, [], 0
    for ex in exchanges:
        spans.append((len(operands), len(out_shape), n))
        aliases.update({len(operands) + i: len(out_shape) + j for i, j in ex.aliases.items()})
        operands += ex.operands
        out_shape += ex.out_shape
        n += ex.n_copies

    def copies(ins, outs, send_sems, recv_sems, base=0):
        sends, arrivals = [], []
        for ex, (i0, o0, s0) in zip(exchanges, spans):
            s, a = ex.copies(ins[i0:i0 + len(ex.operands)], outs[o0:o0 + len(ex.out_shape)], send_sems, recv_sems, base + s0)
            sends += s
            arrivals += a
        return sends, arrivals

    return _Exchange(operands, out_shape, n, copies, aliases)


def _take(hooks, host):
    stages = (hooks or {}).pop(host, [])
    exchanges = [make() for make, _ in stages]

    def finish(results):
        for (_, done), ex in zip(stages, exchanges):
            done(results[:len(ex.out_shape)])
            results = results[len(ex.out_shape):]

    return _merge(exchanges), finish


def _hook(hooks, host, make, done):
    hooks.setdefault(host, []).append((make, done))


class _WeightPrefetch:
    def __init__(self, names, shards, layer, chip):
        self.names, self.shards, self.layer, self.chip, self.result = names, [shards[n] for n in names], layer, chip, None

    def first(self):
        return _gather_ici(self.shards, self.layer)

    def got_first(self, arrived):
        self.arrived = arrived

    def second(self):
        return _gather_d2d(self.arrived)

    def got_second(self, gathered):
        self.result = {name: lax.dynamic_update_index_in_dim(got, own[self.layer], self.chip, 0)
                       for name, got, own in zip(self.names, gathered, self.shards)}

    def ride(self, hooks, first_host, second_host):
        _hook(hooks, first_host, self.first, self.got_first)
        _hook(hooks, second_host, self.second, self.got_second)

    def run(self, tag):
        self.got_first(_run_exchange(self.first(), f"gather_ici_{tag}"))
        self.got_second(_run_exchange(self.second(), f"gather_d2d_{tag}"))


class _GradReduce:
    def __init__(self, g, chip, core, tag):
        self.names, self.g, self.chip, self.core, self.tag, self.result = list(g), list(g.values()), chip, core, tag, None

    def swap(self):
        return _swap_halves(self.g)

    def got_swap(self, theirs):
        self.pair = [_pair_sum(g, t, self.core, f"pair_sum_{n}_{self.tag}") for n, g, t in zip(self.names, self.g, theirs)]

    def scatter(self):
        return _scatter_shards(self.pair)

    def got_scatter(self, q):
        self.mine = [_chip_sum(qa, pa, self.chip, f"chip_sum_{n}_{self.tag}") for n, qa, pa in zip(self.names, q, self.pair)]

    def share(self):
        return _share_halves(self.mine)

    def got_share(self, theirs):
        self.result = {n: jnp.where(self.core == 0, jnp.concatenate([a, b]), jnp.concatenate([b, a]))
                       for n, a, b in zip(self.names, self.mine, theirs)}

    def ride(self, hooks, swap_host, scatter_host, share_host):
        _hook(hooks, swap_host, self.swap, self.got_swap)
        _hook(hooks, scatter_host, self.scatter, self.got_scatter)
        _hook(hooks, share_host, self.share, self.got_share)

    def run(self):
        self.got_swap(_run_exchange(self.swap(), f"swap_halves_{self.tag}"))
        self.got_scatter(_run_exchange(self.scatter(), f"scatter_shards_{self.tag}"))
        self.got_share(_run_exchange(self.share(), f"share_halves_{self.tag}"))


class _LayerWeights:
    def __init__(self, gathered):
        self.gathered, self.made = gathered, {}

    def __getitem__(self, key):
        if key not in self.made:
            cols = lambda t: jnp.swapaxes(t, 0, 1).reshape(t.shape[1], -1)
            rows = lambda t: t.reshape(-1, t.shape[2])
            if key == "w_in":
                made = jnp.pad(cols(self.gathered("w_in")), ((0, 0), (0, PROJ_PAD - PROJ)))
            elif key == "w_gu":
                made = jnp.concatenate([cols(self.gathered("w_gate")), cols(self.gathered("w_up"))], axis=-1)
            else:
                made = rows(self.gathered(key))
            self.made[key] = made
        return self.made[key]


def _gather_small(pk, name):
    rows, w = pk.shape

    def body(pk_ref, all_ref, sum_ref, send_sems, recv_sems):
        x, y, c, _ = _place()
        me = 4 * x + 2 * y + c
        all_ref[me] = pk_ref[...]
        flips = [(fx, fy, fc) for fx in (0, 1) for fy in (0, 1) for fc in (0, 1)][1:]
        peers = [(x ^ fx, y ^ fy, c ^ fc) for fx, fy, fc in flips]
        sends = [_remote(pk_ref, all_ref.at[me], send_sems.at[k], recv_sems.at[k], peer) for k, peer in enumerate(peers)]
        for cp in sends:
            cp.start()
        for k, (px, py, pc) in enumerate(peers):
            slot = all_ref.at[4 * px + 2 * py + pc]
            _remote(slot, slot, send_sems.at[k], recv_sems.at[k], (px, py, pc)).wait_recv()
        for cp in sends:
            cp.wait_send()
        total = all_ref[0]
        for d in range(1, N_DEV):
            total = total + all_ref[d]
        sum_ref[...] = total

    vm = pl.BlockSpec(memory_space=pltpu.VMEM)
    return pl.pallas_call(
        body, name=name, in_specs=[vm], out_specs=[vm, vm],
        out_shape=[jax.ShapeDtypeStruct((N_DEV, rows, w), F32), jax.ShapeDtypeStruct((rows, w), F32)],
        scratch_shapes=[pltpu.SemaphoreType.DMA((7,)), pltpu.SemaphoreType.DMA((7,))],
    )(pk)


def _row_layout(c, nb, s):
    ch = jnp.swapaxes(c[:, :N_HEADS].reshape(nb, s, N_HEADS), 1, 2)
    ccol = jnp.broadcast_to(ch[..., None], (nb, N_HEADS, s, ATT))
    crow = jnp.broadcast_to(ch.reshape(nb, N_HEADS, s // ATT, 1, ATT), (nb, N_HEADS, s // ATT, 8, ATT))
    return ccol, crow


def _dil_bias(rel_bias, name):
    def body(rel_ref, t_ref, o_ref):
        for p in range(len(DIL_PATTERNS)):
            table = t_ref[p]

            def bucket(k, accs, table=table):
                return tuple(jnp.where(table == k, rel_ref[k, h], acc) for h, acc in enumerate(accs))

            accs = lax.fori_loop(0, REL_BUCKETS, bucket, tuple(jnp.full((BLK, 2 * BLK), NEG, F32) for _ in range(N_HEADS)))
            for h in range(N_HEADS):
                o_ref[p, h] = accs[h]

    vm = pl.BlockSpec(memory_space=pltpu.VMEM)
    return pl.pallas_call(
        body, name=name, in_specs=[pl.BlockSpec(memory_space=pltpu.SMEM), vm], out_specs=vm,
        out_shape=jax.ShapeDtypeStruct((len(DIL_PATTERNS), N_HEADS, BLK, 2 * BLK), F32),
        compiler_params=pltpu.CompilerParams(vmem_limit_bytes=VMEM_LIMIT),
    )(rel_bias, jnp.asarray(_bucket_table()))


def _layer_forward(x, x_b, wts, small, bias, nb, s, tag, hooks=None):
    proj = _matmul(x_b, wts["w_in"], "proj", tag)

    carry, finish = _take(hooks, "sb_fwd")
    (o_sb, tails_sb), carried = _sb_fwd(proj, nb, s, f"sb_fwd_{tag}", carry)
    finish(carried)

    carry, finish = _take(hooks, "dil_fwd")
    (o_dl, lse_dl), carried = _dil_attention_fwd(proj, bias, nb, s, f"dil_fwd_{tag}", carry)
    finish(carried)

    fb = jnp.zeros((8, BLK), F32).at[0, :N_HEADS].set(small["f_bias"])
    csum = _fox_gates_fwd(proj, fb, nb, s, f"fox_gates_{tag}")
    ccol, crow = _row_layout(csum, nb, s)
    carry, finish = _take(hooks, "fox_fwd")
    (o_fx, lse_fx), carried = _fox_fwd(proj, ccol, crow, nb, s, f"fox_fwd_{tag}", carry)
    finish(carried)

    cw = jnp.zeros((8, CONV_W), F32).at[:3].set(small["conv_w"])
    o_cv = _conv_fwd(proj, cw, nb, s, f"conv_fwd_{tag}")

    mixed = jnp.concatenate([o_sb, o_dl, o_fx, o_cv], axis=-1).astype(MXU_DTYPE)
    pre1, x1, x1_b = _matmul_post_norm(mixed, wts["w_out"], x, small["ln1_g"], small["ln1_b"], f"out_proj_ln1_{tag}")
    carry, finish = _take(hooks, "ffn_in")
    (gate, up, hid), carried = _ffn_in(x1_b, wts["w_gu"], f"ffn_in_{tag}", carry)
    finish(carried)
    pre2, x2, x2_b = _matmul_post_norm(hid, wts["w_down"], x1, small["ln2_g"], small["ln2_b"], f"ffn_out_ln2_{tag}")
    saved = dict(x_b=x_b, proj=proj, tails_sb=tails_sb, bias=bias, o_dl=o_dl, lse_dl=lse_dl, fb=fb, ccol=ccol, crow=crow, o_fx=o_fx,
                 lse_fx=lse_fx, cw=cw, mixed=mixed, pre1=pre1, x1_b=x1_b, gate=gate, up=up, hid=hid, pre2=pre2)
    return (x2, x2_b), saved


def _layer_backward(dx2, sv, wts, small, nb, s, tag, hooks=None, ffn_grads_ready=None):
    t = nb * s
    dpre2, dpre2_b, dgb2 = _ln_bwd(dx2, sv["pre2"], small["ln2_g"], f"ln2_bwd_{tag}")
    carry, finish = _take(hooks, "ffn_out_dx")
    (dgate, dup), carried = _ffn_out_dx(dpre2_b, wts["w_down"], sv["gate"], sv["up"], f"ffn_out_dx_{tag}", carry)
    finish(carried)
    dw_down = _matmul(sv["hid"], dpre2_b, "ffn_out_dw", tag, trans_a=True)
    dx1 = _ffn_in_dx(dgate, dup, wts["w_gu"], dpre2, f"ffn_in_dx_{tag}")
    x1_b = sv["x1_b"]
    dw_gate = _matmul(x1_b, dgate, "ffn_in_dw", f"{tag}_gate", trans_a=True)
    dw_up = _matmul(x1_b, dup, "ffn_in_dw", f"{tag}_up", trans_a=True)

    dpre1, dpre1_b, dgb1 = _ln_bwd(dx1, sv["pre1"], small["ln1_g"], f"ln1_bwd_{tag}")
    dmixed = _matmul(dpre1_b, wts["w_out"], "out_proj_dx", tag, trans_b=True)
    dw_out = _matmul(sv["mixed"], dpre1_b, "out_proj_dw", tag, trans_a=True)
    if ffn_grads_ready:
        ffn_grads_ready(dict(w_down=dw_down, w_gate=dw_gate, w_up=dw_up, w_out=dw_out))
    proj = sv["proj"]

    carry, finish = _take(hooks, "sb_bwd")
    (dq_sb, dk_sb, dv_sb), carried = _sb_bwd(proj, dmixed, sv["tails_sb"], nb, s, f"sb_bwd_{tag}", carry)
    finish(carried)

    delta_dl = _delta_kernel(dmixed, sv["o_dl"], nb, s, f"dil_delta_{tag}")
    carry, finish = _take(hooks, "dil_bwd")
    (dq_dl, dk_dl, dv_dl, gbias), carried = _dil_attention_bwd(proj, dmixed, sv["lse_dl"], delta_dl, sv["bias"], nb, s,
                                                               f"dil_bwd_{tag}", carry)
    finish(carried)
    drel = _bucket_reduce(gbias, jnp.asarray(_bucket_table()), f"rel_bias_grad_{tag}")

    carry, finish = _take(hooks, "fox_bwd")
    (dq_fx, dk_fx, dv_fx, dcol), carried = _fox_bwd(proj, dmixed, sv["lse_fx"], sv["ccol"], sv["crow"], nb, s,
                                                    f"fox_bwd_{tag}", carry)
    finish(carried)
    dcs = -jnp.swapaxes(dcol[:, :, :, 0, :].reshape(nb, N_HEADS, s), 1, 2).reshape(t, N_HEADS)
    dcs = jnp.pad(dcs, ((0, 0), (0, BLK - N_HEADS)))
    dfx, dfb = _fox_gates_bwd(dcs, proj, sv["fb"], nb, s, f"fox_gates_bwd_{tag}")

    dgates, dcw = _conv_bwd(dmixed, proj, sv["cw"], nb, s, f"conv_bwd_{tag}")

    dproj = jnp.concatenate([dq_sb, dk_sb, dv_sb, dq_dl, dk_dl, dv_dl, dq_fx, dk_fx, dv_fx, dgates, dfx],
                            axis=-1).astype(MXU_DTYPE)
    dx = _matmul(dproj, wts["w_in"], "proj_dx", tag, add=dpre1, add_scale=ALPHA, trans_b=True)
    dw_in = _matmul(sv["x_b"], dproj, "proj_dw", tag, trans_a=True)

    grads = dict(w_in=dw_in[:, :PROJ], w_out=dw_out, w_gate=dw_gate, w_up=dw_up, w_down=dw_down,
                 ln1_g=dgb1[0], ln1_b=dgb1[1], ln2_g=dgb2[0], ln2_b=dgb2[1], conv_w=dcw[:3], f_bias=dfb[0, :N_HEADS],
                 rel_bias=drel[:N_HEADS, :REL_BUCKETS].T)
    return dx, grads


class _NoExchanges:
    def forward_hooks(self, layer):
        return None

    def backward_hooks(self, layer):
        return None

    def ffn_grads_ready(self, layer):
        return None

    def layer_done(self, layer, grads):
        pass


def _local_step(x, target, weights_of, small_all, schedule=None):
    schedule = schedule or _NoExchanges()
    nb, s, d = x.shape
    h = x.reshape(nb * s, d)
    h_b = h.astype(MXU_DTYPE)
    bias = _dil_bias(small_all[0]["rel_bias"], "dil_bias")
    saved = []
    for layer in range(DEPTH):
        wts = weights_of(layer)
        (h, h_b), sv = _layer_forward(h, h_b, wts, small_all[layer], bias, nb, s, f"l{layer}", schedule.forward_hooks(layer))
        saved.append((sv, wts))
    dy, lossp = _loss_kernel(h, target.reshape(nb * s, d), "loss")
    grads = [None] * DEPTH
    for layer in reversed(range(DEPTH)):
        sv, wts = saved[layer]
        dy, grads[layer] = _layer_backward(dy, sv, wts, small_all[layer], nb, s, f"l{layer}",
                                           schedule.backward_hooks(layer), schedule.ffn_grads_ready(layer))
        schedule.layer_done(layer, grads[layer])
    return lossp, dy.reshape(nb, s, d), grads


_BIG = ("w_in", "w_out", "w_gate", "w_up", "w_down")
_COL_SHARDED = ("w_in", "w_gate", "w_up")


class _Schedule:
    def __init__(self, shards, chip, core):
        self.chip, self.core, self.reduces = chip, core, [[] for _ in range(DEPTH)]
        first = _WeightPrefetch(["w_in"], shards, 0, chip)
        first.run("l0_w_in")
        rest = _WeightPrefetch(["w_out", "w_gate", "w_up", "w_down"], shards, 0, chip)
        ahead_a = _WeightPrefetch(["w_in", "w_out", "w_down"], shards, 1, chip)
        ahead_b = _WeightPrefetch(["w_gate", "w_up"], shards, 1, chip)
        self.fetches = [[first, rest], [ahead_a, ahead_b]]
        self.forward, self.backward = [{} for _ in range(DEPTH)], [{} for _ in range(DEPTH)]
        rest.ride(self.forward[0], "sb_fwd", "fox_fwd")
        ahead_a.ride(self.forward[0], "dil_fwd", "ffn_in")
        ahead_b.ride(self.forward[0], "fox_fwd", "ffn_in")

    def weights(self, layer):
        def gathered(name):
            return next(f.result[name] for f in self.fetches[layer] if name in f.names)
        return _LayerWeights(gathered)

    def forward_hooks(self, layer):
        return self.forward[layer]

    def backward_hooks(self, layer):
        return self.backward[layer]

    def _reduce(self, layer, grads, tag):
        red = _GradReduce({name: _by_chip(name, g) for name, g in grads.items()}, self.chip, self.core, tag)
        self.reduces[layer].append(red)
        return red

    def ffn_grads_ready(self, layer):
        if layer != 0:
            return None

        def ready(early):
            self._reduce(0, early, "l0_early").ride(self.backward[0], "sb_bwd", "dil_bwd", "fox_bwd")

        return ready

    def layer_done(self, layer, grads):
        if layer == 1:
            self._reduce(1, {name: grads[name] for name in _BIG}, "l1").ride(self.backward[0], "ffn_out_dx", "sb_bwd", "fox_bwd")
        else:
            self._reduce(0, dict(w_in=grads["w_in"]), "l0_w_in").run()

    def reduced(self, layer, name):
        return next(r.result[name] for r in self.reduces[layer] if name in r.names)


def _by_chip(name, g):
    if name in _COL_SHARDED:
        return jnp.swapaxes(g.reshape(g.shape[0], N_CHIPS, -1), 0, 1)
    return g.reshape(N_CHIPS, -1, g.shape[1])


_SMALL_LAYOUT = (("ln1_g", 0), ("ln1_b", 2), ("ln2_g", 4), ("ln2_b", 6), ("conv_w", 8))
_ROW_MISC = 10
_ROW_LOSS = 11


def _pack_small(per_layer, rel_bias, loss=None):
    pk = jnp.zeros((SMALL_ROWS, D_MODEL), F32)
    for name, row in _SMALL_LAYOUT:
        for l in range(DEPTH):
            v = per_layer[l][name].reshape(-1)
            pk = pk.at[row + l, :v.shape[0]].set(v)
    fb = jnp.concatenate([per_layer[l]["f_bias"] for l in range(DEPTH)])
    pk = pk.at[_ROW_MISC, :2 * N_HEADS].set(fb)
    pk = pk.at[_ROW_MISC, BLK:BLK + REL_BUCKETS * N_HEADS].set(rel_bias.reshape(-1))
    if loss is not None:
        pk = pk.at[_ROW_LOSS, 0].set(loss)
    return pk


def _unpack_small(pk, conv_cols):
    out = {}
    for name, row in _SMALL_LAYOUT:
        n = 3 * conv_cols if name == "conv_w" else D_MODEL
        v = pk[row:row + DEPTH, :n]
        out[name] = v.reshape(DEPTH, 3, conv_cols) if name == "conv_w" else v
    out["f_bias"] = pk[_ROW_MISC, :2 * N_HEADS].reshape(DEPTH, N_HEADS)
    out["rel_bias"] = pk[_ROW_MISC, BLK:BLK + REL_BUCKETS * N_HEADS].reshape(REL_BUCKETS, N_HEADS)
    return out


_WEIGHTS = ("w_in", "f_bias", "conv_w", "w_out", "rel_bias", "ln1_g", "ln1_b", "w_gate", "w_up", "w_down", "ln2_g", "ln2_b")


def kernel(x, w_in, f_bias, conv_w, w_out, rel_bias, ln1_g, ln1_b, w_gate, w_up, w_down, ln2_g, ln2_b, loss_target, m_w_in, m_f_bias, m_conv_w, m_w_out, m_rel_bias, m_ln1_g, m_ln1_b, m_w_gate, m_w_up, m_w_down, m_ln2_g, m_ln2_b, v_w_in, v_f_bias, v_conv_w, v_w_out, v_rel_bias, v_ln1_g, v_ln1_b, v_w_gate, v_w_up, v_w_down, v_ln2_g, v_ln2_b):
    w = dict(w_in=w_in, f_bias=f_bias, conv_w=conv_w, w_out=w_out, rel_bias=rel_bias, ln1_g=ln1_g, ln1_b=ln1_b,
             w_gate=w_gate, w_up=w_up, w_down=w_down, ln2_g=ln2_g, ln2_b=ln2_b)
    m = dict(w_in=m_w_in, f_bias=m_f_bias, conv_w=m_conv_w, w_out=m_w_out, rel_bias=m_rel_bias, ln1_g=m_ln1_g,
             ln1_b=m_ln1_b, w_gate=m_w_gate, w_up=m_w_up, w_down=m_w_down, ln2_g=m_ln2_g, ln2_b=m_ln2_b)
    v = dict(w_in=v_w_in, f_bias=v_f_bias, conv_w=v_conv_w, w_out=v_w_out, rel_bias=v_rel_bias, ln1_g=v_ln1_g,
             ln1_b=v_ln1_b, w_gate=v_w_gate, w_up=v_w_up, w_down=v_w_down, ln2_g=v_ln2_g, ln2_b=v_ln2_b)
    chip = 2 * lax.axis_index("x") + lax.axis_index("y")
    core = lax.axis_index("c")
    conv_shard = CONV_W // N_CHIPS

    schedule = _Schedule({name: w[name].astype(MXU_DTYPE) for name in _BIG}, chip, core)
    cw_pk = jnp.zeros((8, D_MODEL), F32).at[0, :DEPTH * 3 * conv_shard].set(conv_w.reshape(-1))
    cw_all, _ = _gather_small(cw_pk, "gather_conv_w")
    cw_chips = cw_all[0::2, 0, :DEPTH * 3 * conv_shard].reshape(N_CHIPS, DEPTH, 3, conv_shard)
    conv_full = jnp.moveaxis(cw_chips, 0, 2).reshape(DEPTH, 3, CONV_W)
    small_all = [dict(f_bias=f_bias[l], conv_w=conv_full[l], rel_bias=rel_bias, ln1_g=ln1_g[l], ln1_b=ln1_b[l],
                      ln2_g=ln2_g[l], ln2_b=ln2_b[l]) for l in range(DEPTH)]

    lossp, grad_x, grads = _local_step(x, loss_target, schedule.weights, small_all, schedule)
    big_g = {name: jnp.stack([schedule.reduced(l, name) for l in range(DEPTH)]) for name in _BIG}

    drel = grads[0]["rel_bias"] + grads[1]["rel_bias"]
    small_pk = _pack_small(grads, drel, lossp[0, 0])
    _, small_sum = _gather_small(small_pk, "gather_small_grads")
    loss = small_sum[_ROW_LOSS, 0]
    small_g = _unpack_small(small_sum, CONV_W)
    small_g["conv_w"] = lax.dynamic_slice_in_dim(small_g["conv_w"], chip * conv_shard, conv_shard, axis=2)

    out_g, out_d, out_m, out_v = dict(small_g), {}, {}, {}
    for name in _BIG:
        out_g[name] = big_g[name]
        out_d[name], out_m[name], out_v[name] = _adamw(w[name], big_g[name], m[name], v[name], f"adamw_{name}")
    as_3d = lambda t: t if t.ndim == 3 else t[None]
    for name in _WEIGHTS:
        if name not in _BIG:
            stepped = _adamw(as_3d(w[name]), as_3d(small_g[name]), as_3d(m[name]), as_3d(v[name]), f"adamw_{name}")
            out_d[name], out_m[name], out_v[name] = (t.reshape(w[name].shape) for t in stepped)

    return (loss, grad_x, *[out_g[n] for n in _WEIGHTS], *[out_d[n] for n in _WEIGHTS],
            *[out_m[n] for n in _WEIGHTS], *[out_v[n] for n in _WEIGHTS])
```
